```python
import jax, jax.numpy as jnp
from jax import lax
import numpy as np

D_MODEL = 1024
BATCH = 8
SEQ = 4096
DEPTH = 4

N_A_LAYERS = DEPTH // 2
N_B_LAYERS = DEPTH - N_A_LAYERS
POOL_WINDOWS = (2, 4, 8, 16)
N_POOL_GROUPS = len(POOL_WINDOWS)
POOL_GROUP = D_MODEL // N_POOL_GROUPS
N_HEADS = 8
QK_NOPE = 128
QK_ROPE = 64
V_HEAD = 128
QK_HEAD = QK_NOPE + QK_ROPE
Q_RANK = 3 * D_MODEL // 8
KV_RANK = D_MODEL // 4
ROPE_THETA = 10000.0
Q_BLOCK = 128
D_FF = ((8 * D_MODEL // 3 + 127) // 128) * 128
CONV_WIDTH = 3
EPS = 1e-6
N_MOD = 6
MAX_POS_OFFSET = 1024

kernel_name = "yoco_pool_mla_adaln_convglu"


def rmsnorm(x, g):
    x32 = x.astype(jnp.float32)
    y = x32 * lax.rsqrt(jnp.mean(x32 * x32, axis=-1, keepdims=True) + EPS)
    return y.astype(x.dtype) * g


def modulate(h, shift, scale):
    return h * (1 + scale[:, None, :]) + shift[:, None, :]


def rope_tables(positions):
    inv = 1.0 / (ROPE_THETA ** (jnp.arange(0, QK_ROPE, 2, dtype=jnp.float32) / QK_ROPE))
    ang = positions.astype(jnp.float32)[..., None] * inv
    return jnp.cos(ang), jnp.sin(ang)


def apply_rope(x, cos, sin):
    x32 = x.astype(jnp.float32)
    x1, x2 = jnp.split(x32, 2, axis=-1)
    out = jnp.concatenate([x1 * cos - x2 * sin, x2 * cos + x1 * sin], axis=-1)
    return out.astype(x.dtype)


def trailing_mean_minus_self(h, w):
    s = h.shape[1]
    h32 = h.astype(jnp.float32)
    cs = jnp.cumsum(h32, axis=1)
    cs_lag = jnp.pad(cs, ((0, 0), (w, 0), (0, 0)))[:, :s]
    count = jnp.minimum(jnp.arange(1, s + 1, dtype=jnp.float32), float(w))
    mean = (cs - cs_lag) / count[None, :, None]
    return (mean - h32).astype(h.dtype)


def pool_mixer(h, w_g, b_g, scale):
    bsz, s, d = h.shape
    hg = h.reshape(bsz, s, N_POOL_GROUPS, POOL_GROUP)
    pooled = jnp.stack([trailing_mean_minus_self(hg[:, :, g], POOL_WINDOWS[g])
                        for g in range(N_POOL_GROUPS)], axis=2)
    y = jnp.einsum('bsgc,gcd->bsgd', pooled, w_g).reshape(bsz, s, d) + b_g
    return y * scale


def conv_glu_ffn(h, w_up, conv_w, conv_b, w_down):
    s = h.shape[1]
    a, v = jnp.split(h @ w_up, 2, axis=-1)
    ap = jnp.pad(a, ((0, 0), (CONV_WIDTH - 1, 0), (0, 0)))
    a = sum(ap[:, k:k + s] * conv_w[k] for k in range(CONV_WIDTH)) + conv_b
    return (jax.nn.gelu(a, approximate=False) * v) @ w_down


def shared_kv(x, kv_in_g, w_dkv, ckv_norm_g, w_uk, w_uv, cos, sin):
    bsz, s, _ = x.shape
    kv = rmsnorm(x, kv_in_g) @ w_dkv
    c_kv = rmsnorm(kv[..., :KV_RANK], ckv_norm_g)
    k_rope = apply_rope(kv[..., KV_RANK:], cos, sin)
    k_nope = (c_kv @ w_uk).reshape(bsz, s, N_HEADS, QK_NOPE)
    k = jnp.concatenate([k_nope, jnp.broadcast_to(k_rope[:, :, None, :],
                                                  (bsz, s, N_HEADS, QK_ROPE))], axis=-1)
    v = (c_kv @ w_uv).reshape(bsz, s, N_HEADS, V_HEAD)
    return k, v


def causal_block_attention(q, k, v):
    s = q.shape[1]
    q = q * (QK_HEAD ** -0.5)
    outs = []
    for i in range(s // Q_BLOCK):
        q0 = i * Q_BLOCK
        k_end = q0 + Q_BLOCK
        sc = jnp.einsum('bqhd,bkhd->bhqk', q[:, q0:k_end], k[:, :k_end],
                        preferred_element_type=jnp.float32)
        mask = jnp.arange(k_end)[None, :] <= jnp.arange(q0, k_end)[:, None]
        sc = jnp.where(mask, sc, jnp.finfo(jnp.float32).min)
        p = jax.nn.softmax(sc, axis=-1).astype(v.dtype)
        outs.append(jnp.einsum('bhqk,bkhd->bqhd', p, v[:, :k_end]))
    return jnp.concatenate(outs, axis=1)


def mla_mixer(h, k, v, w_dq, q_norm_g, w_uq, w_o, cos, sin):
    bsz, s, _ = h.shape
    c_q = rmsnorm(h @ w_dq, q_norm_g)
    q = (c_q @ w_uq).reshape(bsz, s, N_HEADS, QK_HEAD)
    q = jnp.concatenate([q[..., :QK_NOPE],
                         apply_rope(q[..., QK_NOPE:], cos[:, :, None], sin[:, :, None])], axis=-1)
    o = causal_block_attention(q, k, v)
    return o.reshape(bsz, s, N_HEADS * V_HEAD) @ w_o


def _fwd_setup_inputs(seed: int = 0) -> dict:
    key = jax.random.key(seed)
    ks = jax.random.split(key, 26)
    f32 = jnp.float32
    nrm = lambda k, shape, s: jax.random.normal(k, shape, f32) * s
    d, f = D_MODEL, D_FF
    positions = (jnp.arange(SEQ, dtype=jnp.int32)[None, :]
                 + jax.random.randint(ks[2], (BATCH, 1), 0, MAX_POS_OFFSET, dtype=jnp.int32))
    return {
        "x": nrm(ks[0], (BATCH, SEQ, d), 1.0),
        "c": nrm(ks[1], (BATCH, d), 1.0),
        "positions": positions,
        "mod_w": nrm(ks[3], (DEPTH, d, N_MOD * d), d ** -0.5),
        "mod_b": nrm(ks[4], (DEPTH, N_MOD * d), 0.01),
        "norm1_g": 1.0 + nrm(ks[5], (DEPTH, d), 0.02),
        "norm2_g": 1.0 + nrm(ks[6], (DEPTH, d), 0.02),
        "pool_w": nrm(ks[7], (N_A_LAYERS, N_POOL_GROUPS, POOL_GROUP, POOL_GROUP), POOL_GROUP ** -0.5),
        "pool_b": nrm(ks[8], (N_A_LAYERS, d), 0.01),
        "pool_scale": 1.0 + nrm(ks[9], (N_A_LAYERS, d), 0.1),
        "kv_in_g": 1.0 + nrm(ks[10], (d,), 0.02),
        "w_dkv": nrm(ks[11], (d, KV_RANK + QK_ROPE), d ** -0.5),
        "ckv_norm_g": 1.0 + nrm(ks[12], (KV_RANK,), 0.02),
        "w_uk": nrm(ks[13], (KV_RANK, N_HEADS * QK_NOPE), KV_RANK ** -0.5),
        "w_uv": nrm(ks[14], (KV_RANK, N_HEADS * V_HEAD), KV_RANK ** -0.5),
        "w_dq": nrm(ks[15], (N_B_LAYERS, d, Q_RANK), d ** -0.5),
        "q_norm_g": 1.0 + nrm(ks[16], (N_B_LAYERS, Q_RANK), 0.02),
        "w_uq": nrm(ks[17], (N_B_LAYERS, Q_RANK, N_HEADS * QK_HEAD), Q_RANK ** -0.5),
        "w_o": nrm(ks[18], (N_B_LAYERS, N_HEADS * V_HEAD, d), (N_HEADS * V_HEAD) ** -0.5),
        "w_up": nrm(ks[19], (DEPTH, d, 2 * f), d ** -0.5),
        "conv_w": nrm(ks[20], (DEPTH, CONV_WIDTH, f), CONV_WIDTH ** -0.5),
        "conv_b": nrm(ks[21], (DEPTH, f), 0.01),
        "w_down": nrm(ks[22], (DEPTH, f, d), f ** -0.5),
        "final_g": 1.0 + nrm(ks[23], (d,), 0.02),
    }


def _fwd_reference(x, c, positions, mod_w, mod_b, norm1_g, norm2_g, pool_w, pool_b, pool_scale,
              kv_in_g, w_dkv, ckv_norm_g, w_uk, w_uv, w_dq, q_norm_g, w_uq, w_o,
              w_up, conv_w, conv_b, w_down, final_g):
    cos, sin = rope_tables(positions)
    mods = jnp.einsum('bd,lde->lbe', jax.nn.silu(c), mod_w) + mod_b[:, None, :]
    k = v = None
    for l in range(DEPTH):
        sh1, sc1, g1, sh2, sc2, g2 = jnp.split(mods[l], N_MOD, axis=-1)
        h = modulate(rmsnorm(x, norm1_g[l]), sh1, sc1)
        if l < N_A_LAYERS:
            y = pool_mixer(h, pool_w[l], pool_b[l], pool_scale[l])
        else:
            j = l - N_A_LAYERS
            y = mla_mixer(h, k, v, w_dq[j], q_norm_g[j], w_uq[j], w_o[j], cos, sin)
        x = x + g1[:, None, :] * y
        h = modulate(rmsnorm(x, norm2_g[l]), sh2, sc2)
        x = x + g2[:, None, :] * conv_glu_ffn(h, w_up[l], conv_w[l], conv_b[l], w_down[l])
        if l == N_A_LAYERS - 1:
            k, v = shared_kv(x, kv_in_g, w_dkv, ckv_norm_g, w_uk, w_uv, cos, sin)
    return rmsnorm(x, final_g)


import jax as _jax
import jax.numpy as _jnp

TWIN_FORMAT = 'train_step'
FWD_PARAMS = ['x', 'c', 'positions', 'mod_w', 'mod_b', 'norm1_g', 'norm2_g', 'pool_w', 'pool_b', 'pool_scale', 'kv_in_g', 'w_dkv', 'ckv_norm_g', 'w_uk', 'w_uv', 'w_dq', 'q_norm_g', 'w_uq', 'w_o', 'w_up', 'conv_w', 'conv_b', 'w_down', 'final_g']
TWIN_WEIGHTS = ['mod_w', 'mod_b', 'norm1_g', 'norm2_g', 'pool_w', 'pool_b', 'pool_scale', 'kv_in_g', 'w_dkv', 'ckv_norm_g', 'w_uk', 'w_uv', 'w_dq', 'q_norm_g', 'w_uq', 'w_o', 'w_up', 'conv_w', 'conv_b', 'w_down', 'final_g']
TWIN_DIFF_INPUT = 'x'
TWIN_INPUTS = ['x', 'c', 'positions', 'mod_w', 'mod_b', 'norm1_g', 'norm2_g', 'pool_w', 'pool_b', 'pool_scale', 'kv_in_g', 'w_dkv', 'ckv_norm_g', 'w_uk', 'w_uv', 'w_dq', 'q_norm_g', 'w_uq', 'w_o', 'w_up', 'conv_w', 'conv_b', 'w_down', 'final_g', 'loss_target', 'm_mod_w', 'm_mod_b', 'm_norm1_g', 'm_norm2_g', 'm_pool_w', 'm_pool_b', 'm_pool_scale', 'm_kv_in_g', 'm_w_dkv', 'm_ckv_norm_g', 'm_w_uk', 'm_w_uv', 'm_w_dq', 'm_q_norm_g', 'm_w_uq', 'm_w_o', 'm_w_up', 'm_conv_w', 'm_conv_b', 'm_w_down', 'm_final_g', 'v_mod_w', 'v_mod_b', 'v_norm1_g', 'v_norm2_g', 'v_pool_w', 'v_pool_b', 'v_pool_scale', 'v_kv_in_g', 'v_w_dkv', 'v_ckv_norm_g', 'v_w_uk', 'v_w_uv', 'v_w_dq', 'v_q_norm_g', 'v_w_uq', 'v_w_o', 'v_w_up', 'v_conv_w', 'v_conv_b', 'v_w_down', 'v_final_g']
TWIN_OUTPUTS = ['loss', 'grad_x', 'grad_mod_w', 'grad_mod_b', 'grad_norm1_g', 'grad_norm2_g', 'grad_pool_w', 'grad_pool_b', 'grad_pool_scale', 'grad_kv_in_g', 'grad_w_dkv', 'grad_ckv_norm_g', 'grad_w_uk', 'grad_w_uv', 'grad_w_dq', 'grad_q_norm_g', 'grad_w_uq', 'grad_w_o', 'grad_w_up', 'grad_conv_w', 'grad_conv_b', 'grad_w_down', 'grad_final_g', 'delta_mod_w', 'delta_mod_b', 'delta_norm1_g', 'delta_norm2_g', 'delta_pool_w', 'delta_pool_b', 'delta_pool_scale', 'delta_kv_in_g', 'delta_w_dkv', 'delta_ckv_norm_g', 'delta_w_uk', 'delta_w_uv', 'delta_w_dq', 'delta_q_norm_g', 'delta_w_uq', 'delta_w_o', 'delta_w_up', 'delta_conv_w', 'delta_conv_b', 'delta_w_down', 'delta_final_g', 'new_m_mod_w', 'new_m_mod_b', 'new_m_norm1_g', 'new_m_norm2_g', 'new_m_pool_w', 'new_m_pool_b', 'new_m_pool_scale', 'new_m_kv_in_g', 'new_m_w_dkv', 'new_m_ckv_norm_g', 'new_m_w_uk', 'new_m_w_uv', 'new_m_w_dq', 'new_m_q_norm_g', 'new_m_w_uq', 'new_m_w_o', 'new_m_w_up', 'new_m_conv_w', 'new_m_conv_b', 'new_m_w_down', 'new_m_final_g', 'new_v_mod_w', 'new_v_mod_b', 'new_v_norm1_g', 'new_v_norm2_g', 'new_v_pool_w', 'new_v_pool_b', 'new_v_pool_scale', 'new_v_kv_in_g', 'new_v_w_dkv', 'new_v_ckv_norm_g', 'new_v_w_uk', 'new_v_w_uv', 'new_v_w_dq', 'new_v_q_norm_g', 'new_v_w_uq', 'new_v_w_o', 'new_v_w_up', 'new_v_conv_w', 'new_v_conv_b', 'new_v_w_down', 'new_v_final_g']
TWIN_LEAF_KINDS = {'loss': 'loss', 'grad_x': 'grad_x', 'grad_mod_w': 'grad_w', 'grad_mod_b': 'grad_w', 'grad_norm1_g': 'grad_w', 'grad_norm2_g': 'grad_w', 'grad_pool_w': 'grad_w', 'grad_pool_b': 'grad_w', 'grad_pool_scale': 'grad_w', 'grad_kv_in_g': 'grad_w', 'grad_w_dkv': 'grad_w', 'grad_ckv_norm_g': 'grad_w', 'grad_w_uk': 'grad_w', 'grad_w_uv': 'grad_w', 'grad_w_dq': 'grad_w', 'grad_q_norm_g': 'grad_w', 'grad_w_uq': 'grad_w', 'grad_w_o': 'grad_w', 'grad_w_up': 'grad_w', 'grad_conv_w': 'grad_w', 'grad_conv_b': 'grad_w', 'grad_w_down': 'grad_w', 'grad_final_g': 'grad_w', 'delta_mod_w': 'delta_w', 'delta_mod_b': 'delta_w', 'delta_norm1_g': 'delta_w', 'delta_norm2_g': 'delta_w', 'delta_pool_w': 'delta_w', 'delta_pool_b': 'delta_w', 'delta_pool_scale': 'delta_w', 'delta_kv_in_g': 'delta_w', 'delta_w_dkv': 'delta_w', 'delta_ckv_norm_g': 'delta_w', 'delta_w_uk': 'delta_w', 'delta_w_uv': 'delta_w', 'delta_w_dq': 'delta_w', 'delta_q_norm_g': 'delta_w', 'delta_w_uq': 'delta_w', 'delta_w_o': 'delta_w', 'delta_w_up': 'delta_w', 'delta_conv_w': 'delta_w', 'delta_conv_b': 'delta_w', 'delta_w_down': 'delta_w', 'delta_final_g': 'delta_w', 'new_m_mod_w': 'new_m', 'new_m_mod_b': 'new_m', 'new_m_norm1_g': 'new_m', 'new_m_norm2_g': 'new_m', 'new_m_pool_w': 'new_m', 'new_m_pool_b': 'new_m', 'new_m_pool_scale': 'new_m', 'new_m_kv_in_g': 'new_m', 'new_m_w_dkv': 'new_m', 'new_m_ckv_norm_g': 'new_m', 'new_m_w_uk': 'new_m', 'new_m_w_uv': 'new_m', 'new_m_w_dq': 'new_m', 'new_m_q_norm_g': 'new_m', 'new_m_w_uq': 'new_m', 'new_m_w_o': 'new_m', 'new_m_w_up': 'new_m', 'new_m_conv_w': 'new_m', 'new_m_conv_b': 'new_m', 'new_m_w_down': 'new_m', 'new_m_final_g': 'new_m', 'new_v_mod_w': 'new_v', 'new_v_mod_b': 'new_v', 'new_v_norm1_g': 'new_v', 'new_v_norm2_g': 'new_v', 'new_v_pool_w': 'new_v', 'new_v_pool_b': 'new_v', 'new_v_pool_scale': 'new_v', 'new_v_kv_in_g': 'new_v', 'new_v_w_dkv': 'new_v', 'new_v_ckv_norm_g': 'new_v', 'new_v_w_uk': 'new_v', 'new_v_w_uv': 'new_v', 'new_v_w_dq': 'new_v', 'new_v_q_norm_g': 'new_v', 'new_v_w_uq': 'new_v', 'new_v_w_o': 'new_v', 'new_v_w_up': 'new_v', 'new_v_conv_w': 'new_v', 'new_v_conv_b': 'new_v', 'new_v_w_down': 'new_v', 'new_v_final_g': 'new_v'}


def _forward(args):
    return _fwd_reference(*[args[k] for k in FWD_PARAMS])


def _output_shape():
    out = _jax.eval_shape(lambda: _forward(_fwd_setup_inputs(0)))
    return out.shape, out.dtype

N_MICROBATCH = 1
ADAM_LR = 0.001
ADAM_B1 = 0.9
ADAM_B2 = 0.999
ADAM_EPS = 1e-08
ADAM_WD = 0.01
ADAM_STEP = 10
PER_EXAMPLE_BATCH_AXIS = {'x': 0, 'c': 0, 'positions': 0, 'loss_target': 0}
SHARED_INPUTS = []
_WEIGHT_DTYPES = {'mod_w': _jnp.float32, 'mod_b': _jnp.float32, 'norm1_g': _jnp.float32, 'norm2_g': _jnp.float32, 'pool_w': _jnp.float32, 'pool_b': _jnp.float32, 'pool_scale': _jnp.float32, 'kv_in_g': _jnp.float32, 'w_dkv': _jnp.float32, 'ckv_norm_g': _jnp.float32, 'w_uk': _jnp.float32, 'w_uv': _jnp.float32, 'w_dq': _jnp.float32, 'q_norm_g': _jnp.float32, 'w_uq': _jnp.float32, 'w_o': _jnp.float32, 'w_up': _jnp.float32, 'conv_w': _jnp.float32, 'conv_b': _jnp.float32, 'w_down': _jnp.float32, 'final_g': _jnp.float32}
MOMENT_SCALE = {'mod_w': 6.165490e-02, 'mod_b': 9.957571e-02, 'norm1_g': 7.453140e-02, 'norm2_g': 1.061671e-01, 'pool_w': 1.048439e-01, 'pool_b': 1.012188e-01, 'pool_scale': 1.417796e-01, 'kv_in_g': 2.964222e-02, 'w_dkv': 5.317209e-02, 'ckv_norm_g': 5.962641e-02, 'w_uk': 1.359236e-02, 'w_uv': 2.538144e-02, 'w_dq': 1.894897e-02, 'q_norm_g': 1.849604e-02, 'w_uq': 9.369141e-03, 'w_o': 1.803635e-02, 'w_up': 5.193049e-02, 'conv_w': 5.278784e-02, 'conv_b': 3.874786e-02, 'w_down': 8.562155e-02, 'final_g': 3.275438e+01}


def _to_microbatches(a, axis):
    t = _jnp.moveaxis(a, axis, 0)
    t = t.reshape((N_MICROBATCH, t.shape[0] // N_MICROBATCH) + t.shape[1:])
    return _jnp.moveaxis(t, 1, axis + 1)


def setup_inputs(seed: int = 0) -> dict:
    inp = _fwd_setup_inputs(seed)
    key = _jax.random.fold_in(_jax.random.key(seed), 7919)
    shape, _ = _output_shape()
    out = dict(inp)
    out["loss_target"] = _jax.random.normal(_jax.random.fold_in(key, 0), shape, _jnp.float32)
    for i, name in enumerate(TWIN_WEIGHTS):
        w = inp[name].astype(_jnp.float32)
        if MOMENT_SCALE is None:
            s = _jnp.sqrt(_jnp.mean(_jnp.square(w)) + 1e-30)
        else:
            s = MOMENT_SCALE[name]
        km, kv = _jax.random.split(_jax.random.fold_in(key, i + 1))
        out[name] = w
        out["m_" + name] = s * _jax.random.normal(km, w.shape, _jnp.float32)
        out["v_" + name] = (s * s) * _jax.random.uniform(kv, w.shape, _jnp.float32, 0.5, 1.5)
    if N_MICROBATCH > 1:
        for name, axis in PER_EXAMPLE_BATCH_AXIS.items():
            out[name] = _to_microbatches(out[name], axis)
    return {'x': out['x'], 'c': out['c'], 'positions': out['positions'], 'mod_w': out['mod_w'], 'mod_b': out['mod_b'], 'norm1_g': out['norm1_g'], 'norm2_g': out['norm2_g'], 'pool_w': out['pool_w'], 'pool_b': out['pool_b'], 'pool_scale': out['pool_scale'], 'kv_in_g': out['kv_in_g'], 'w_dkv': out['w_dkv'], 'ckv_norm_g': out['ckv_norm_g'], 'w_uk': out['w_uk'], 'w_uv': out['w_uv'], 'w_dq': out['w_dq'], 'q_norm_g': out['q_norm_g'], 'w_uq': out['w_uq'], 'w_o': out['w_o'], 'w_up': out['w_up'], 'conv_w': out['conv_w'], 'conv_b': out['conv_b'], 'w_down': out['w_down'], 'final_g': out['final_g'], 'loss_target': out['loss_target'], 'm_mod_w': out['m_mod_w'], 'm_mod_b': out['m_mod_b'], 'm_norm1_g': out['m_norm1_g'], 'm_norm2_g': out['m_norm2_g'], 'm_pool_w': out['m_pool_w'], 'm_pool_b': out['m_pool_b'], 'm_pool_scale': out['m_pool_scale'], 'm_kv_in_g': out['m_kv_in_g'], 'm_w_dkv': out['m_w_dkv'], 'm_ckv_norm_g': out['m_ckv_norm_g'], 'm_w_uk': out['m_w_uk'], 'm_w_uv': out['m_w_uv'], 'm_w_dq': out['m_w_dq'], 'm_q_norm_g': out['m_q_norm_g'], 'm_w_uq': out['m_w_uq'], 'm_w_o': out['m_w_o'], 'm_w_up': out['m_w_up'], 'm_conv_w': out['m_conv_w'], 'm_conv_b': out['m_conv_b'], 'm_w_down': out['m_w_down'], 'm_final_g': out['m_final_g'], 'v_mod_w': out['v_mod_w'], 'v_mod_b': out['v_mod_b'], 'v_norm1_g': out['v_norm1_g'], 'v_norm2_g': out['v_norm2_g'], 'v_pool_w': out['v_pool_w'], 'v_pool_b': out['v_pool_b'], 'v_pool_scale': out['v_pool_scale'], 'v_kv_in_g': out['v_kv_in_g'], 'v_w_dkv': out['v_w_dkv'], 'v_ckv_norm_g': out['v_ckv_norm_g'], 'v_w_uk': out['v_w_uk'], 'v_w_uv': out['v_w_uv'], 'v_w_dq': out['v_w_dq'], 'v_q_norm_g': out['v_q_norm_g'], 'v_w_uq': out['v_w_uq'], 'v_w_o': out['v_w_o'], 'v_w_up': out['v_w_up'], 'v_conv_w': out['v_conv_w'], 'v_conv_b': out['v_conv_b'], 'v_w_down': out['v_w_down'], 'v_final_g': out['v_final_g']}


def _loss(weights, diff, rest, loss_target):
    with _jax.named_scope("forward"):
        args = {**rest, TWIN_DIFF_INPUT: diff, **{k: w.astype(_WEIGHT_DTYPES[k]) for k, w in weights.items()}}
        y = _forward(args)
    with _jax.named_scope("loss_head"):
        err = _jnp.square(y.astype(_jnp.float32) - loss_target)
        return 0.5 * _jnp.sum(_jnp.mean(err, axis=-1)) if err.ndim else 0.5 * err


def _adamw(w, g, m, v):
    m = ADAM_B1 * m + (1.0 - ADAM_B1) * g
    v = ADAM_B2 * v + (1.0 - ADAM_B2) * _jnp.square(g)
    m_hat = m / (1.0 - ADAM_B1 ** ADAM_STEP)
    v_hat = v / (1.0 - ADAM_B2 ** ADAM_STEP)
    delta = -ADAM_LR * (m_hat / (_jnp.sqrt(v_hat) + ADAM_EPS) + ADAM_WD * w)
    return delta, m, v


def reference(x, c, positions, mod_w, mod_b, norm1_g, norm2_g, pool_w, pool_b, pool_scale, kv_in_g, w_dkv, ckv_norm_g, w_uk, w_uv, w_dq, q_norm_g, w_uq, w_o, w_up, conv_w, conv_b, w_down, final_g, loss_target, m_mod_w, m_mod_b, m_norm1_g, m_norm2_g, m_pool_w, m_pool_b, m_pool_scale, m_kv_in_g, m_w_dkv, m_ckv_norm_g, m_w_uk, m_w_uv, m_w_dq, m_q_norm_g, m_w_uq, m_w_o, m_w_up, m_conv_w, m_conv_b, m_w_down, m_final_g, v_mod_w, v_mod_b, v_norm1_g, v_norm2_g, v_pool_w, v_pool_b, v_pool_scale, v_kv_in_g, v_w_dkv, v_ckv_norm_g, v_w_uk, v_w_uv, v_w_dq, v_q_norm_g, v_w_uq, v_w_o, v_w_up, v_conv_w, v_conv_b, v_w_down, v_final_g):
    given = dict(x=x, c=c, positions=positions, mod_w=mod_w, mod_b=mod_b, norm1_g=norm1_g, norm2_g=norm2_g, pool_w=pool_w, pool_b=pool_b, pool_scale=pool_scale, kv_in_g=kv_in_g, w_dkv=w_dkv, ckv_norm_g=ckv_norm_g, w_uk=w_uk, w_uv=w_uv, w_dq=w_dq, q_norm_g=q_norm_g, w_uq=w_uq, w_o=w_o, w_up=w_up, conv_w=conv_w, conv_b=conv_b, w_down=w_down, final_g=final_g, loss_target=loss_target, m_mod_w=m_mod_w, m_mod_b=m_mod_b, m_norm1_g=m_norm1_g, m_norm2_g=m_norm2_g, m_pool_w=m_pool_w, m_pool_b=m_pool_b, m_pool_scale=m_pool_scale, m_kv_in_g=m_kv_in_g, m_w_dkv=m_w_dkv, m_ckv_norm_g=m_ckv_norm_g, m_w_uk=m_w_uk, m_w_uv=m_w_uv, m_w_dq=m_w_dq, m_q_norm_g=m_q_norm_g, m_w_uq=m_w_uq, m_w_o=m_w_o, m_w_up=m_w_up, m_conv_w=m_conv_w, m_conv_b=m_conv_b, m_w_down=m_w_down, m_final_g=m_final_g, v_mod_w=v_mod_w, v_mod_b=v_mod_b, v_norm1_g=v_norm1_g, v_norm2_g=v_norm2_g, v_pool_w=v_pool_w, v_pool_b=v_pool_b, v_pool_scale=v_pool_scale, v_kv_in_g=v_kv_in_g, v_w_dkv=v_w_dkv, v_ckv_norm_g=v_ckv_norm_g, v_w_uk=v_w_uk, v_w_uv=v_w_uv, v_w_dq=v_w_dq, v_q_norm_g=v_q_norm_g, v_w_uq=v_w_uq, v_w_o=v_w_o, v_w_up=v_w_up, v_conv_w=v_conv_w, v_conv_b=v_conv_b, v_w_down=v_w_down, v_final_g=v_final_g)
    weights = {n: given[n] for n in TWIN_WEIGHTS}
    shared = {n: given[n] for n in SHARED_INPUTS}
    per_example = {n: given[n] for n in ['x', 'c', 'positions']}
    grad_fn = _jax.value_and_grad(_loss, argnums=(0, 1))

    def one_microbatch(ex, loss_target):
        ex = dict(ex)
        diff = ex.pop(TWIN_DIFF_INPUT)
        return grad_fn(weights, diff, {**shared, **ex}, loss_target)

    if N_MICROBATCH == 1:
        loss, (grad_w, grad_x) = one_microbatch(per_example, given["loss_target"])
    else:
        def body(carry, xs):
            loss_sum, grad_sum = carry
            l_k, (gw_k, gx_k) = one_microbatch(xs[0], xs[1])
            with _jax.named_scope("update"):
                return (loss_sum + l_k, _jax.tree.map(_jnp.add, grad_sum, gw_k)), gx_k

        init = (_jnp.zeros((), _jnp.float32), _jax.tree.map(_jnp.zeros_like, weights))
        (loss, grad_w), grad_x = _jax.lax.scan(body, init, (per_example, given["loss_target"]))
    with _jax.named_scope("update"):
        delta_w, new_m, new_v = {}, {}, {}
        for n in TWIN_WEIGHTS:
            delta_w[n], new_m[n], new_v[n] = _adamw(weights[n], grad_w[n], given["m_" + n], given["v_" + n])
    return (loss, grad_x, *[grad_w[n] for n in TWIN_WEIGHTS], *[delta_w[n] for n in TWIN_WEIGHTS],
            *[new_m[n] for n in TWIN_WEIGHTS], *[new_v[n] for n in TWIN_WEIGHTS])
```

```python
import functools

import jax
import jax.numpy as jnp
from jax import lax
from jax.experimental import pallas as pl
from jax.experimental.pallas import tpu as pltpu

F32 = jnp.float32
BF16 = jnp.bfloat16

D_MODEL = 1024
DEPTH = 4
N_A_LAYERS = 2
N_B_LAYERS = 2
POOL_WINDOWS = (2, 4, 8, 16)
POOL_GROUP = 256
N_HEADS = 8
QK_NOPE = 128
QK_ROPE = 64
V_HEAD = 128
QK_HEAD = QK_NOPE + QK_ROPE
Q_RANK = 384
KV_RANK = 256
ROPE_THETA = 10000.0
D_FF = 2816
EPS = 1e-6
N_MOD = 6
ADAM_LR = 0.001
ADAM_B1 = 0.9
ADAM_B2 = 0.999
ADAM_EPS = 1e-08
ADAM_WD = 0.01
ADAM_STEP = 10

N_DEV = 8
LANES = 128
Q_EXT = 256
VMEM_LIMIT_BYTES = 48 * 1024 * 1024
MESH = pl.DeviceIdType.MESH
NEG_BIG = -0.7 * float(jnp.finfo(jnp.float32).max)


def _params(sem):
    return pltpu.CompilerParams(dimension_semantics=sem, vmem_limit_bytes=VMEM_LIMIT_BYTES)


def _tile(n, cap):
    if n <= cap:
        return n
    best = None
    for d in range(LANES, cap + 1, LANES):
        if n % d == 0:
            best = d
    assert best is not None, (n, cap)
    return best


def _dot(a, b, dims):
    return lax.dot_general(a, b, (dims, ((), ())), preferred_element_type=F32)


NN = ((1,), (0,))
NT = ((1,), (1,))
TN = ((0,), (0,))


def _mm(name, a, b, mode="nn", out_dtype=BF16, add=None, resid=None, gate=None,
        tm_cap=1024, tn_cap=1408, tk_cap=1408):
    if mode == "tn":
        kdim, m = a.shape
    else:
        m, kdim = a.shape
    n = b.shape[0] if mode == "nt" else b.shape[1]
    tm, tn, tk = _tile(m, tm_cap), _tile(n, tn_cap), _tile(kdim, tk_cap)
    nk = kdim // tk
    dims = {"nn": NN, "nt": NT, "tn": TN}[mode]
    a_spec = pl.BlockSpec((tk, tm), lambda i, j, k: (k, i)) if mode == "tn" else pl.BlockSpec((tm, tk), lambda i, j, k: (i, k))
    b_spec = pl.BlockSpec((tn, tk), lambda i, j, k: (j, k)) if mode == "nt" else pl.BlockSpec((tk, tn), lambda i, j, k: (k, j))
    o_spec = pl.BlockSpec((tm, tn), lambda i, j, k: (i, j))
    g_spec = pl.BlockSpec((1, tn), lambda i, j, k: (0, j))
    gated = resid is not None

    def body(*refs):
        a_ref, b_ref = refs[0], refs[1]
        acc = refs[-1]
        k = pl.program_id(2)

        @pl.when(k == 0)
        def _():
            acc[...] = jnp.zeros_like(acc)

        acc[...] += _dot(a_ref[...].astype(BF16), b_ref[...].astype(BF16), dims)

        @pl.when(k == nk - 1)
        def _():
            if gated:
                r_ref, g_ref, y_ref, x_ref = refs[2:6]
                y_ref[...] = acc[...]
                x_ref[...] = r_ref[...] + g_ref[...] * acc[...]
            elif add is not None:
                refs[3][...] = (acc[...] + refs[2][...].astype(F32)).astype(out_dtype)
            else:
                refs[2][...] = acc[...].astype(out_dtype)

    ins, in_specs = [a, b], [a_spec, b_spec]
    if gated:
        ins += [resid, gate]
        in_specs += [o_spec, g_spec]
        out_shape = (jax.ShapeDtypeStruct((m, n), F32), jax.ShapeDtypeStruct((m, n), F32))
        out_specs = (o_spec, o_spec)
    else:
        if add is not None:
            ins.append(add)
            in_specs.append(o_spec)
        out_shape = jax.ShapeDtypeStruct((m, n), out_dtype)
        out_specs = o_spec
    return pl.pallas_call(
        body, name=name, grid=(m // tm, n // tn, nk), in_specs=in_specs, out_specs=out_specs, out_shape=out_shape,
        scratch_shapes=[pltpu.VMEM((tm, tn), F32)],
        compiler_params=_params(("parallel", "parallel", "arbitrary")),
    )(*ins)


def _rowwise(name, fn, tiled, bcast, outs, sums=(), tr=512):
    tiled = [t if isinstance(t, tuple) else (t, t.shape[1], 0) for t in tiled]
    s = tiled[0][0].shape[0]
    tr = min(tr, s)
    assert s % tr == 0
    n_t, n_b, n_o = len(tiled), len(bcast), len(outs)

    def body(*refs):
        i = pl.program_id(0)
        vals = [r[...] for r in refs[:n_t + n_b]]
        o_vals, s_vals = fn(*vals)
        for r, v in zip(refs[n_t + n_b:n_t + n_b + n_o], o_vals):
            r[...] = v.astype(r.dtype)
        s_refs = refs[n_t + n_b + n_o:]

        @pl.when(i == 0)
        def _():
            for r in s_refs:
                r[...] = jnp.zeros_like(r)

        for r, v in zip(s_refs, s_vals):
            r[...] += v

    in_specs = [pl.BlockSpec((tr, n), functools.partial(lambda cb, i: (i, cb), cb)) for (_, n, cb) in tiled]
    in_specs += [pl.BlockSpec(b.shape, functools.partial(lambda nd, i: (0,) * nd, b.ndim)) for b in bcast]
    out_specs = [pl.BlockSpec((tr, n), lambda i: (i, 0)) for (n, _) in outs]
    out_specs += [pl.BlockSpec((1, n), lambda i: (0, 0)) for n in sums]
    out_shape = [jax.ShapeDtypeStruct((s, n), dt) for (n, dt) in outs]
    out_shape += [jax.ShapeDtypeStruct((1, n), F32) for n in sums]
    res = pl.pallas_call(
        body, name=name, grid=(s // tr,), in_specs=in_specs, out_specs=tuple(out_specs), out_shape=tuple(out_shape),
        compiler_params=_params(("arbitrary",)),
    )(*[t[0] for t in tiled], *bcast)
    return res


def _colsum(v):
    return jnp.sum(v, axis=0, keepdims=True)


def _rms_fwd(name, x, g, scale=None, shift=None, out_dtype=BF16, ncols=None):
    mod = scale is not None

    def fn(xv, gv, *ss):
        y = xv * lax.rsqrt(jnp.mean(xv * xv, axis=-1, keepdims=True) + EPS) * gv
        if mod:
            y = y * (1.0 + ss[0]) + ss[1]
        return (y,), ()

    n = ncols or x.shape[1]
    return _rowwise(name, fn, [(x, n, 0)], [g] + ([scale, shift] if mod else []), [(n, out_dtype)])[0]


def _rms_bwd(name, x, g, dh, scale=None, dx_in=None, ncols=None, out_dtype=F32):
    mod = scale is not None
    has_in = dx_in is not None

    def fn(*vals):
        xv, dhv = vals[0], vals[1].astype(F32)
        rest = list(vals[2:])
        dxi = rest.pop(0) if has_in else None
        gv = rest.pop(0)
        rstd = lax.rsqrt(jnp.mean(xv * xv, axis=-1, keepdims=True) + EPS)
        xhat = xv * rstd
        sums = []
        if mod:
            sc = rest.pop(0)
            dyn = dhv * (1.0 + sc)
            dshift, dscale = _colsum(dhv), _colsum(dhv * (xhat * gv))
        else:
            dyn = dhv
        dg = _colsum(dyn * xhat)
        dxhat = dyn * gv
        dx = rstd * (dxhat - xhat * jnp.mean(dxhat * xhat, axis=-1, keepdims=True))
        if has_in:
            dx = dx + dxi
        sums = [dg] + ([dshift, dscale] if mod else [])
        return (dx,), sums

    n = ncols or x.shape[1]
    tiled = [(x, n, 0), dh] + ([dx_in] if has_in else [])
    return _rowwise(name, fn, tiled, [g] + ([scale] if mod else []), [(n, out_dtype)], [n] * (3 if mod else 1))


def _gate_bwd(name, dxn, y, g):
    def fn(dv, yv, gv):
        return (gv * dv,), (_colsum(dv * yv),)

    n = dxn.shape[1]
    return _rowwise(name, fn, [dxn, y], [g], [(n, BF16)], [n])


def _loss_head(name, x, g, target):
    n = x.shape[1]

    def fn(xv, tv, gv):
        rstd = lax.rsqrt(jnp.mean(xv * xv, axis=-1, keepdims=True) + EPS)
        xhat = xv * rstd
        err = xhat * gv - tv
        loss = 0.5 * jnp.sum(jnp.sum(err * err, axis=-1, keepdims=True) / n, axis=0, keepdims=True)
        dy = err / n
        dg = _colsum(dy * xhat)
        dxhat = dy * gv
        dx = rstd * (dxhat - xhat * jnp.mean(dxhat * xhat, axis=-1, keepdims=True))
        return (dx,), (dg, jnp.broadcast_to(loss, (1, LANES)))

    return _rowwise(name, fn, [x, target], [g], [(n, F32)], [n, LANES])


def _krope_fwd(name, kv_ext, tabk):
    def fn(xv, tv):
        t = xv * tv
        return (t + pltpu.roll(t, 64, 1),), ()

    return _rowwise(name, fn, [(kv_ext, LANES, 2), tabk], [], [(LANES, BF16)])[0]


def _krope_bwd(name, dkd, tabk):
    def fn(dv, tv):
        return ((dv + pltpu.roll(dv, 64, 1)) * tv,), ()

    return _rowwise(name, fn, [dkd, tabk], [], [(LANES, F32)])[0]


def _adamw(name, w, g, m, v):
    def fn(wv, gv, mv, vv):
        m2 = ADAM_B1 * mv + (1.0 - ADAM_B1) * gv
        v2 = ADAM_B2 * vv + (1.0 - ADAM_B2) * (gv * gv)
        m_hat = m2 / (1.0 - ADAM_B1 ** ADAM_STEP)
        v_hat = v2 / (1.0 - ADAM_B2 ** ADAM_STEP)
        delta = -ADAM_LR * (m_hat / (jnp.sqrt(v_hat) + ADAM_EPS) + ADAM_WD * wv)
        return (delta, m2, v2), ()

    r, c = w.shape
    tr = r
    for cand in (512, 256, 128, 64, 32, 16, 8):
        if r % cand == 0 and r > cand:
            tr = cand
            break
    return _rowwise(name, fn, [w, g, m, v], [], [(c, F32)] * 3, tr=tr)


def _sum8(name, parts):
    _, r, c = parts.shape
    tr = r
    for cand in (1024, 512, 256, 128, 64, 32, 16):
        if r % cand == 0 and r > cand:
            tr = cand
            break

    def body(p_ref, o_ref):
        acc = p_ref[0].astype(F32)
        for k in range(1, N_DEV):
            acc = acc + p_ref[k].astype(F32)
        o_ref[...] = acc

    return pl.pallas_call(
        body, name=name, grid=(r // tr,), in_specs=[pl.BlockSpec((N_DEV, tr, c), lambda i: (0, i, 0))],
        out_specs=pl.BlockSpec((tr, c), lambda i: (i, 0)), out_shape=jax.ShapeDtypeStruct((r, c), F32),
        compiler_params=_params(("parallel",)),
    )(parts)


def _mods_fwd(name, c_all, w, b):
    depth, d, n = w.shape

    def body(c_ref, w_ref, b_ref, o_ref):
        cv = c_ref[...]
        sc = (cv * (1.0 / (1.0 + jnp.exp(-cv)))).astype(BF16)
        o_ref[0] = _dot(sc, w_ref[0].astype(BF16), NN) + b_ref[0]

    return pl.pallas_call(
        body, name=name, grid=(depth,),
        in_specs=[pl.BlockSpec(c_all.shape, lambda l: (0, 0)), pl.BlockSpec((1, d, n), lambda l: (l, 0, 0)),
                  pl.BlockSpec((1, 1, n), lambda l: (l, 0, 0))],
        out_specs=pl.BlockSpec((1, c_all.shape[0], n), lambda l: (l, 0, 0)),
        out_shape=jax.ShapeDtypeStruct((depth, c_all.shape[0], n), F32),
        compiler_params=_params(("parallel",)),
    )(c_all, w, b.reshape(depth, 1, n))


def _mods_bwd(name, c_all, dm):
    depth, rows, n = dm.shape
    d = c_all.shape[1]

    def body(c_ref, dm_ref, o_ref):
        cv = c_ref[...]
        sc = (cv * (1.0 / (1.0 + jnp.exp(-cv)))).astype(BF16)
        o_ref[0] = _dot(sc, dm_ref[0].astype(BF16), TN)

    return pl.pallas_call(
        body, name=name, grid=(depth,),
        in_specs=[pl.BlockSpec(c_all.shape, lambda l: (0, 0)), pl.BlockSpec((1, rows, n), lambda l: (l, 0, 0))],
        out_specs=pl.BlockSpec((1, d, n), lambda l: (l, 0, 0)),
        out_shape=jax.ShapeDtypeStruct((depth, d, n), F32),
        compiler_params=_params(("parallel",)),
    )(c_all, dm)


POOL_TILE = 256


def _split_dot(band, val):
    hi = val.astype(BF16)
    lo = (val - hi.astype(F32)).astype(BF16)
    return _dot(band, hi, NN) + _dot(band, lo, NN)


def _pool_fwd(name, h1, x, pw, pb, ps, g1):
    s, d = h1.shape
    t = POOL_TILE

    def body(hc_ref, hp_ref, x_ref, pw_ref, pb_ref, ps_ref, g_ref, xo_ref, zb_ref, pooled_ref):
        i = pl.program_id(0)
        r = lax.broadcasted_iota(jnp.int32, (t, t), 0)
        j = lax.broadcasted_iota(jnp.int32, (t, t), 1)
        pos = (i * t + lax.broadcasted_iota(jnp.int32, (t, 1), 0) + 1).astype(F32)
        has_prev = (i > 0).astype(F32)
        for grp, w in enumerate(POOL_WINDOWS):
            cs = slice(grp * POOL_GROUP, (grp + 1) * POOL_GROUP)
            hc = hc_ref[:, cs]
            band_cur = ((r - j >= 0) & (r - j < w)).astype(BF16)
            band_prev = (r + t - j < w).astype(BF16)
            ssum = _split_dot(band_cur, hc) + has_prev * _split_dot(band_prev, hp_ref[:, cs])
            pooled = (ssum / jnp.minimum(pos, float(w)) - hc).astype(BF16)
            zb = _dot(pooled, pw_ref[grp], NN) + pb_ref[:, cs]
            xo_ref[:, cs] = x_ref[:, cs] + g_ref[:, cs] * (zb * ps_ref[:, cs])
            zb_ref[:, cs] = zb
            pooled_ref[:, cs] = pooled

    row = pl.BlockSpec((t, d), lambda i: (i, 0))
    vec = pl.BlockSpec((1, d), lambda i: (0, 0))
    return pl.pallas_call(
        body, name=name, grid=(s // t,),
        in_specs=[row, pl.BlockSpec((t, d), lambda i: (jnp.maximum(i - 1, 0), 0)), row,
                  pl.BlockSpec(pw.shape, lambda i: (0, 0, 0)), vec, vec, vec],
        out_specs=(row, row, row),
        out_shape=(jax.ShapeDtypeStruct((s, d), F32), jax.ShapeDtypeStruct((s, d), F32), jax.ShapeDtypeStruct((s, d), BF16)),
        compiler_params=_params(("parallel",)),
    )(h1, h1, x, pw, pb, ps, g1)


def _pool_bwd(name, dxn, zb, pooled, pw, ps, g1):
    s, d = dxn.shape
    t = POOL_TILE
    nt = s // t

    def body(dc_ref, dn_ref, zb_ref, pooled_ref, pw_ref, ps_ref, g_ref, dh_ref, dpw_ref, dpb_ref, dps_ref, dg_ref):
        i = pl.program_id(0)

        @pl.when(i == 0)
        def _():
            dpw_ref[...] = jnp.zeros_like(dpw_ref)
            dpb_ref[...] = jnp.zeros_like(dpb_ref)
            dps_ref[...] = jnp.zeros_like(dps_ref)
            dg_ref[...] = jnp.zeros_like(dg_ref)

        jj = lax.broadcasted_iota(jnp.int32, (t, t), 0)
        rr = lax.broadcasted_iota(jnp.int32, (t, t), 1)
        pos = (i * t + lax.broadcasted_iota(jnp.int32, (t, 1), 0) + 1).astype(F32)
        has_next = (i < nt - 1).astype(F32)
        for grp, w in enumerate(POOL_WINDOWS):
            cs = slice(grp * POOL_GROUP, (grp + 1) * POOL_GROUP)
            gv, psv, zbv, dxc = g_ref[:, cs], ps_ref[:, cs], zb_ref[:, cs], dc_ref[:, cs]
            dg_ref[:, cs] += _colsum(dxc * (zbv * psv))
            dy = gv * dxc
            dps_ref[:, cs] += _colsum(dy * zbv)
            dz = dy * psv
            dpb_ref[:, cs] += _colsum(dz)
            dzb = dz.astype(BF16)
            dpw_ref[grp] += _dot(pooled_ref[:, cs], dzb, TN)
            dp = _dot(dzb, pw_ref[grp], NT)
            dzn = (gv * dn_ref[:, cs] * psv).astype(BF16)
            dpn = _dot(dzn, pw_ref[grp], NT) * (has_next / float(w))
            band_cur = ((rr - jj >= 0) & (rr - jj < w)).astype(BF16)
            band_next = (rr + t - jj < w).astype(BF16)
            dh_ref[:, cs] = _split_dot(band_cur, dp / jnp.minimum(pos, float(w))) + _split_dot(band_next, dpn) - dp

    row = pl.BlockSpec((t, d), lambda i: (i, 0))
    vec = pl.BlockSpec((1, d), lambda i: (0, 0))
    wspec = pl.BlockSpec(pw.shape, lambda i: (0, 0, 0))
    return pl.pallas_call(
        body, name=name, grid=(nt,),
        in_specs=[row, pl.BlockSpec((t, d), lambda i: (jnp.minimum(i + 1, nt - 1), 0)), row, row, wspec, vec, vec],
        out_specs=(row, wspec, vec, vec, vec),
        out_shape=(jax.ShapeDtypeStruct((s, d), F32), jax.ShapeDtypeStruct(pw.shape, F32),
                   jax.ShapeDtypeStruct((1, d), F32), jax.ShapeDtypeStruct((1, d), F32), jax.ShapeDtypeStruct((1, d), F32)),
        compiler_params=_params(("arbitrary",)),
    )(dxn, dxn, zb, pooled, pw, ps, g1)


GLU_TILE = 256
HALO = 16
INV_SQRT2 = 0.7071067811865476
INV_SQRT_2PI = 0.3989422804014327


def _gelu(xv):
    return 0.5 * xv * (1.0 + lax.erf(xv * INV_SQRT2))


def _gelu_grad(xv):
    return 0.5 * (1.0 + lax.erf(xv * INV_SQRT2)) + xv * (INV_SQRT_2PI * jnp.exp(-0.5 * xv * xv))


def _glu_fwd(name, ua, uv, cw, cb):
    s, f = ua.shape
    t, tf = GLU_TILE, _tile(f, 1408)

    def body(a_ref, ah_ref, v_ref, cw_ref, cb_ref, o_ref):
        i = pl.program_id(1)
        has_prev = (i > 0).astype(F32)
        ext = jnp.concatenate([ah_ref[...].astype(F32) * has_prev, a_ref[...].astype(F32)], axis=0)
        e1 = pltpu.roll(ext, 1, 0)[HALO:]
        e2 = pltpu.roll(ext, 2, 0)[HALO:]
        pre = e2 * cw_ref[0:1, :] + e1 * cw_ref[1:2, :] + ext[HALO:] * cw_ref[2:3, :] + cb_ref[...]
        o_ref[...] = (_gelu(pre) * v_ref[...].astype(F32)).astype(o_ref.dtype)

    blk = pl.BlockSpec((t, tf), lambda j, i: (i, j))
    halo = pl.BlockSpec((HALO, tf), lambda j, i: (jnp.maximum(i * (t // HALO) - 1, 0), j))
    return pl.pallas_call(
        body, name=name, grid=(f // tf, s // t),
        in_specs=[blk, halo, blk, pl.BlockSpec((3, tf), lambda j, i: (0, j)), pl.BlockSpec((1, tf), lambda j, i: (0, j))],
        out_specs=blk, out_shape=jax.ShapeDtypeStruct((s, f), BF16),
        compiler_params=_params(("parallel", "parallel")),
    )(ua, ua, uv, cw, cb)


def _glu_bwd(name, ua, uv, dgl, cw, cb):
    s, f = ua.shape
    t, tf = GLU_TILE, _tile(f, 1408)
    nt = s // t
    te = t + HALO

    def body(a_ref, ah_ref, an_ref, v_ref, vn_ref, d_ref, dn_ref, cw_ref, cb_ref, da_ref, dv_ref, dcw_ref, dcb_ref):
        i = pl.program_id(1)

        @pl.when(i == 0)
        def _():
            dcw_ref[...] = jnp.zeros_like(dcw_ref)
            dcb_ref[...] = jnp.zeros_like(dcb_ref)

        has_prev = (i > 0).astype(F32)
        has_next = (i < nt - 1).astype(F32)
        ext = jnp.concatenate([ah_ref[...].astype(F32) * has_prev, a_ref[...].astype(F32), an_ref[...].astype(F32)], axis=0)
        e0 = ext[HALO:]
        e1 = pltpu.roll(ext, 1, 0)[HALO:]
        e2 = pltpu.roll(ext, 2, 0)[HALO:]
        c0, c1, c2 = cw_ref[0:1, :], cw_ref[1:2, :], cw_ref[2:3, :]
        pre = e2 * c0 + e1 * c1 + e0 * c2 + cb_ref[...]
        vx = jnp.concatenate([v_ref[...].astype(F32), vn_ref[...].astype(F32)], axis=0)
        dx = jnp.concatenate([d_ref[...].astype(F32), dn_ref[...].astype(F32) * has_next], axis=0)
        dpre = dx * vx * _gelu_grad(pre)
        up1 = pltpu.roll(dpre, te - 1, 0)
        up2 = pltpu.roll(dpre, te - 2, 0)
        da_ref[...] = (dpre * c2 + up1 * c1 + up2 * c0)[:t].astype(da_ref.dtype)
        dv_ref[...] = (d_ref[...].astype(F32) * _gelu(pre[:t])).astype(dv_ref.dtype)
        dpt = dpre[:t]
        dcb_ref[...] += _colsum(dpt)
        dcw_ref[0:1, :] += _colsum(e2[:t] * dpt)
        dcw_ref[1:2, :] += _colsum(e1[:t] * dpt)
        dcw_ref[2:3, :] += _colsum(e0[:t] * dpt)

    blk = pl.BlockSpec((t, tf), lambda j, i: (i, j))
    prev = pl.BlockSpec((HALO, tf), lambda j, i: (jnp.maximum(i * (t // HALO) - 1, 0), j))
    nxt = pl.BlockSpec((HALO, tf), lambda j, i: (jnp.minimum((i + 1) * (t // HALO), s // HALO - 1), j))
    w3 = pl.BlockSpec((3, tf), lambda j, i: (0, j))
    w1 = pl.BlockSpec((1, tf), lambda j, i: (0, j))
    return pl.pallas_call(
        body, name=name, grid=(f // tf, nt),
        in_specs=[blk, prev, nxt, blk, nxt, blk, nxt, w3, w1],
        out_specs=(blk, blk, w3, w1),
        out_shape=(jax.ShapeDtypeStruct((s, f), BF16), jax.ShapeDtypeStruct((s, f), BF16),
                   jax.ShapeDtypeStruct((3, f), F32), jax.ShapeDtypeStruct((1, f), F32)),
        compiler_params=_params(("parallel", "arbitrary")),
    )(ua, ua, ua, uv, uv, dgl, dgl, cw, cb)


ATT_TILE = 512


def _scores(q, kn_blk, kd_blk):
    return _dot(q[:, :QK_NOPE], kn_blk, NT) + _dot(q[:, QK_NOPE:], kd_blk, NT)


def _causal_mask(sv, q0, k0):
    row = q0 + lax.broadcasted_iota(jnp.int32, sv.shape, 0)
    col = k0 + lax.broadcasted_iota(jnp.int32, sv.shape, 1)
    return jnp.where(col <= row, sv, NEG_BIG)


def _attn_fwd(name, q_ext, tabq, kn, kd, v):
    s = q_ext.shape[0]
    t = min(ATT_TILE, s)
    nq = s // t

    def body(q_ref, tab_ref, kn_ref, kd_ref, v_ref, o_ref, lse_ref, acc_ref, m_ref, l_ref):
        qi = pl.program_id(1)
        q = (q_ref[...].astype(F32) * tab_ref[...]).astype(BF16)
        acc_ref[...] = jnp.zeros_like(acc_ref)
        m_ref[...] = jnp.full_like(m_ref, NEG_BIG)
        l_ref[...] = jnp.zeros_like(l_ref)

        def step(j, masked):
            ks = pl.ds(pl.multiple_of(j * t, t), t)
            sv = _scores(q, kn_ref[ks, :], kd_ref[ks, :])
            if masked:
                sv = _causal_mask(sv, qi * t, j * t)
            m_prev = m_ref[...]
            m_new = jnp.maximum(m_prev, jnp.max(sv, axis=-1, keepdims=True))
            alpha = jnp.exp(m_prev - m_new)
            p = jnp.exp(sv - m_new)
            l_ref[...] = alpha * l_ref[...] + jnp.sum(p, axis=-1, keepdims=True)
            acc_ref[...] = alpha * acc_ref[...] + _dot(p.astype(BF16), v_ref[ks, :], NN)
            m_ref[...] = m_new

        def full_step(j, carry):
            step(j, False)
            return carry

        lax.fori_loop(0, qi, full_step, 0)
        step(qi, True)
        o_ref[...] = (acc_ref[...] / l_ref[...]).astype(o_ref.dtype)
        lse_ref[...] = jnp.broadcast_to(m_ref[...] + jnp.log(l_ref[...]), lse_ref.shape)

    head_q = pl.BlockSpec((t, Q_EXT), lambda h, i: (i, h))
    head_o = pl.BlockSpec((t, V_HEAD), lambda h, i: (i, h))
    kv_all = pl.BlockSpec((s, LANES), lambda h, i: (0, h))
    return pl.pallas_call(
        body, name=name, grid=(N_HEADS, nq),
        in_specs=[head_q, pl.BlockSpec((t, Q_EXT), lambda h, i: (i, 0)), kv_all, pl.BlockSpec((s, LANES), lambda h, i: (0, 0)), kv_all],
        out_specs=(head_o, head_o),
        out_shape=(jax.ShapeDtypeStruct((s, N_HEADS * V_HEAD), BF16), jax.ShapeDtypeStruct((s, N_HEADS * LANES), F32)),
        scratch_shapes=[pltpu.VMEM((t, V_HEAD), F32), pltpu.VMEM((t, 1), F32), pltpu.VMEM((t, 1), F32)],
        compiler_params=_params(("parallel", "parallel")),
    )(q_ext, tabq, kn, kd, v)


def _attn_dq(name, q_ext, tabq, kn, kd, v, o, lse, do):
    s = q_ext.shape[0]
    t = min(ATT_TILE, s)
    nq = s // t

    def body(q_ref, tab_ref, kn_ref, kd_ref, v_ref, o_ref, lse_ref, do_ref, dq_ref, delta_ref, accn_ref, accr_ref):
        qi = pl.program_id(1)
        tab = tab_ref[...]
        q = (q_ref[...].astype(F32) * tab).astype(BF16)
        dov = do_ref[...]
        delta = jnp.sum(dov.astype(F32) * o_ref[...].astype(F32), axis=-1, keepdims=True)
        lse = lse_ref[:, 0:1]
        accn_ref[...] = jnp.zeros_like(accn_ref)
        accr_ref[...] = jnp.zeros_like(accr_ref)

        def step(j, masked):
            ks = pl.ds(pl.multiple_of(j * t, t), t)
            kn_blk, kd_blk = kn_ref[ks, :], kd_ref[ks, :]
            sv = _scores(q, kn_blk, kd_blk)
            if masked:
                sv = _causal_mask(sv, qi * t, j * t)
            p = jnp.exp(sv - lse)
            dp = _dot(dov, v_ref[ks, :], NT)
            ds = (p * (dp - delta)).astype(BF16)
            accn_ref[...] += _dot(ds, kn_blk, NN)
            accr_ref[...] += _dot(ds, kd_blk, NN)

        def full_step(j, carry):
            step(j, False)
            return carry

        lax.fori_loop(0, qi, full_step, 0)
        step(qi, True)
        dq_ref[:, :QK_NOPE] = (accn_ref[...] * tab[:, :QK_NOPE]).astype(dq_ref.dtype)
        dq_ref[:, QK_NOPE:] = (accr_ref[...] * tab[:, QK_NOPE:]).astype(dq_ref.dtype)
        delta_ref[...] = jnp.broadcast_to(delta, delta_ref.shape)

    head_q = pl.BlockSpec((t, Q_EXT), lambda h, i: (i, h))
    head_o = pl.BlockSpec((t, V_HEAD), lambda h, i: (i, h))
    kv_all = pl.BlockSpec((s, LANES), lambda h, i: (0, h))
    return pl.pallas_call(
        body, name=name, grid=(N_HEADS, nq),
        in_specs=[head_q, pl.BlockSpec((t, Q_EXT), lambda h, i: (i, 0)), kv_all, pl.BlockSpec((s, LANES), lambda h, i: (0, 0)), kv_all,
                  head_o, head_o, head_o],
        out_specs=(head_q, head_o),
        out_shape=(jax.ShapeDtypeStruct((s, N_HEADS * Q_EXT), BF16), jax.ShapeDtypeStruct((s, N_HEADS * LANES), F32)),
        scratch_shapes=[pltpu.VMEM((t, QK_NOPE), F32), pltpu.VMEM((t, QK_NOPE), F32)],
        compiler_params=_params(("parallel", "parallel")),
    )(q_ext, tabq, kn, kd, v, o, lse, do)


def _attn_dkv(name, q_ext, tabq, kn, kd, v, lse, delta, do, acc_in=None):
    s = q_ext.shape[0]
    t = min(ATT_TILE, s)
    nq = s // t
    has_in = acc_in is not None

    def body(*refs):
        q_ref, tab_ref, kn_ref, kd_ref, v_ref, lse_ref, delta_ref, do_ref = refs[:8]
        dkn_ref, dkd_ref, dv_ref, accn_ref, accv_ref = refs[-5:]
        kj, h = pl.program_id(0), pl.program_id(1)
        kn_blk, kd_blk, v_blk = kn_ref[...], kd_ref[...], v_ref[...]
        accn_ref[...] = jnp.zeros_like(accn_ref)
        accv_ref[...] = jnp.zeros_like(accv_ref)

        @pl.when(h == 0)
        def _():
            if has_in:
                dkd_ref[...] = refs[9][...]
            else:
                dkd_ref[...] = jnp.zeros_like(dkd_ref)

        def step(i, masked):
            qs = pl.ds(pl.multiple_of(i * t, t), t)
            q = (q_ref[qs, :].astype(F32) * tab_ref[qs, :]).astype(BF16)
            sv = _scores(q, kn_blk, kd_blk)
            if masked:
                sv = _causal_mask(sv, i * t, kj * t)
            p = jnp.exp(sv - lse_ref[qs, 0:1])
            dov = do_ref[qs, :]
            accv_ref[...] += _dot(p.astype(BF16), dov, TN)
            dp = _dot(dov, v_blk, NT)
            ds = (p * (dp - delta_ref[qs, 0:1])).astype(BF16)
            accn_ref[...] += _dot(ds, q[:, :QK_NOPE], TN)
            dkd_ref[...] += _dot(ds, q[:, QK_NOPE:], TN)

        def full_step(i, carry):
            step(i, False)
            return carry

        step(kj, True)
        lax.fori_loop(kj + 1, nq, full_step, 0)
        if has_in:
            dkn_ref[...] = accn_ref[...] + refs[8][...]
            dv_ref[...] = accv_ref[...] + refs[10][...]
        else:
            dkn_ref[...] = accn_ref[...]
            dv_ref[...] = accv_ref[...]

    q_all = pl.BlockSpec((s, Q_EXT), lambda j, h: (0, h))
    o_all = pl.BlockSpec((s, LANES), lambda j, h: (0, h))
    kblk = pl.BlockSpec((t, LANES), lambda j, h: (j, h))
    kdblk = pl.BlockSpec((t, LANES), lambda j, h: (j, 0))
    ins = [q_ext, tabq, kn, kd, v, lse, delta, do]
    in_specs = [q_all, pl.BlockSpec((s, Q_EXT), lambda j, h: (0, 0)), kblk, kdblk, kblk, o_all, o_all, o_all]
    if has_in:
        ins += list(acc_in)
        in_specs += [kblk, kdblk, kblk]
    return pl.pallas_call(
        body, name=name, grid=(nq, N_HEADS), in_specs=in_specs, out_specs=(kblk, kdblk, kblk),
        out_shape=(jax.ShapeDtypeStruct((s, N_HEADS * LANES), F32), jax.ShapeDtypeStruct((s, LANES), F32),
                   jax.ShapeDtypeStruct((s, N_HEADS * LANES), F32)),
        scratch_shapes=[pltpu.VMEM((t, LANES), F32), pltpu.VMEM((t, LANES), F32)],
        compiler_params=_params(("parallel", "arbitrary")),
    )(*ins)


def _swap_halves(w):
    half = w.shape[-1] // 2
    return jnp.concatenate([-w[..., half:], w[..., :half]], axis=-1)


def _unswap_halves(g):
    half = g.shape[-1] // 2
    return jnp.concatenate([g[..., half:], -g[..., :half]], axis=-1)


def _extend_w_uq(w):
    r = w.reshape(Q_RANK, N_HEADS, QK_HEAD)
    rope = r[..., QK_NOPE:]
    return jnp.concatenate([r[..., :QK_NOPE], rope, _swap_halves(rope)], axis=-1).reshape(Q_RANK, N_HEADS * Q_EXT)


def _fold_w_uq_grad(g):
    r = g.reshape(Q_RANK, N_HEADS, Q_EXT)
    rope = r[..., QK_NOPE:QK_HEAD] + _unswap_halves(r[..., QK_HEAD:])
    return jnp.concatenate([r[..., :QK_NOPE], rope], axis=-1).reshape(Q_RANK, N_HEADS * QK_HEAD)


def _extend_w_dkv(w):
    return jnp.concatenate([w, _swap_halves(w[:, KV_RANK:])], axis=-1)


def _fold_w_dkv_grad(g):
    rope = g[:, KV_RANK:KV_RANK + QK_ROPE] + _unswap_halves(g[:, KV_RANK + QK_ROPE:])
    return jnp.concatenate([g[:, :KV_RANK], rope], axis=-1)


def _rope_tables(positions):
    inv = 1.0 / (ROPE_THETA ** (jnp.arange(0, QK_ROPE, 2, dtype=F32) / QK_ROPE))
    ang = positions.astype(F32)[:, None] * inv
    cos, sin = jnp.cos(ang), jnp.sin(ang)
    tabk = jnp.concatenate([cos, cos, sin, sin], axis=-1)
    scale = QK_HEAD ** -0.5
    tabq = jnp.concatenate([jnp.full((positions.shape[0], QK_NOPE), scale, F32), tabk * scale], axis=-1)
    return tabq, tabk


def _forward_backward(x, target, mods, tabq, tabk, w):
    row = lambda vec: vec.reshape(1, -1)
    mod = [[row(mods[l, k * D_MODEL:(k + 1) * D_MODEL]) for k in range(N_MOD)] for l in range(DEPTH)]
    saved = []
    kv = None
    for l in range(DEPTH):
        sh1, sc1, g1, sh2, sc2, g2 = mod[l]
        x_in = x
        if l < N_A_LAYERS:
            h1 = _rms_fwd(f"norm1_fwd_{l}", x, row(w["norm1_g"][l]), sc1, sh1, out_dtype=F32)
            x_mid, zb, pooled = _pool_fwd(f"pool_fwd_{l}", h1, x, w["pool_w"][l], row(w["pool_b"][l]), row(w["pool_scale"][l]), g1)
            mix = (zb, pooled)
        else:
            j = l - N_A_LAYERS
            h1 = _rms_fwd(f"norm1_fwd_{l}", x, row(w["norm1_g"][l]), sc1, sh1)
            cq_pre = _mm(f"dq_fwd_{l}", h1, w["w_dq"][j], out_dtype=F32)
            cq = _rms_fwd(f"qnorm_fwd_{l}", cq_pre, row(w["q_norm_g"][j]))
            q_ext = _mm(f"uq_fwd_{l}", cq, w["w_uq_ext"][j])
            o, lse = _attn_fwd(f"attn_fwd_{l}", q_ext, tabq, kv["kn"], kv["kd"], kv["v"])
            y, x_mid = _mm(f"wo_fwd_{l}", o, w["w_o"][j], resid=x, gate=g1)
            mix = (h1, cq_pre, cq, q_ext, o, lse, y)
        h2 = _rms_fwd(f"norm2_fwd_{l}", x_mid, row(w["norm2_g"][l]), sc2, sh2)
        ua = _mm(f"up_a_fwd_{l}", h2, w["w_up_a"][l])
        uv = _mm(f"up_v_fwd_{l}", h2, w["w_up_v"][l])
        gl = _glu_fwd(f"glu_fwd_{l}", ua, uv, w["conv_w"][l], row(w["conv_b"][l]))
        y2, x = _mm(f"down_fwd_{l}", gl, w["w_down"][l], resid=x_mid, gate=g2)
        saved.append((x_in, x_mid, h2, ua, uv, gl, y2, mix))
        if l == N_A_LAYERS - 1:
            kvn = _rms_fwd("kvin_fwd", x, row(w["kv_in_g"]))
            kv_ext = _mm("dkv_fwd", kvn, w["w_dkv_ext"], out_dtype=F32)
            ckv = _rms_fwd("ckv_fwd", kv_ext, row(w["ckv_norm_g"]), ncols=KV_RANK)
            kv = dict(x=x, kvn=kvn, kv_ext=kv_ext, ckv=ckv, kd=_krope_fwd("krope_fwd", kv_ext, tabk),
                      kn=_mm("uk_fwd", ckv, w["w_uk"]), v=_mm("uv_fwd", ckv, w["w_uv"]))

    dx, dfinal_g, loss = _loss_head("loss_head", x, row(w["final_g"]), target)
    g = {"final_g": dfinal_g.reshape(-1)}
    per_layer = {k: [None] * DEPTH for k in ("norm1_g", "norm2_g", "w_up_a", "w_up_v", "conv_w", "conv_b", "w_down")}
    per_a = {k: [None] * N_A_LAYERS for k in ("pool_w", "pool_b", "pool_scale")}
    per_b = {k: [None] * N_B_LAYERS for k in ("w_dq", "q_norm_g", "w_uq_ext", "w_o")}
    dmods = [None] * DEPTH
    dkv = None
    for l in reversed(range(DEPTH)):
        sh1, sc1, g1, sh2, sc2, g2 = mod[l]
        x_in, x_mid, h2, ua, uv, gl, y2, mix = saved[l]
        if l == N_A_LAYERS - 1:
            dkn, dkd, dv = dkv
            dckv = _mm("uk_bwd", dkn, w["w_uk"], mode="nt", out_dtype=F32)
            dckv = _mm("uv_bwd", dv, w["w_uv"], mode="nt", out_dtype=F32, add=dckv)
            g["w_uk"] = _mm("uk_wgrad", kv["ckv"], dkn, mode="tn", out_dtype=F32)
            g["w_uv"] = _mm("uv_wgrad", kv["ckv"], dv, mode="tn", out_dtype=F32)
            dkr = _krope_bwd("krope_bwd", dkd, tabk)
            dc, dckv_g = _rms_bwd("ckv_bwd", kv["kv_ext"], row(w["ckv_norm_g"]), dckv, ncols=KV_RANK, out_dtype=BF16)
            dkv_ext = jnp.concatenate([dc, dkr.astype(BF16)], axis=-1)
            dkvn = _mm("dkv_bwd", dkv_ext, w["w_dkv_ext"], mode="nt")
            g["w_dkv_ext"] = _mm("dkv_wgrad", kv["kvn"], dkv_ext, mode="tn", out_dtype=F32)
            dx, dkv_in_g = _rms_bwd("kvin_bwd", kv["x"], row(w["kv_in_g"]), dkvn, dx_in=dx)
            g["ckv_norm_g"], g["kv_in_g"] = dckv_g.reshape(-1), dkv_in_g.reshape(-1)
        dy2, dg2 = _gate_bwd(f"gate2_bwd_{l}", dx, y2, g2)
        dgl = _mm(f"down_bwd_{l}", dy2, w["w_down"][l], mode="nt")
        per_layer["w_down"][l] = _mm(f"down_wgrad_{l}", gl, dy2, mode="tn", out_dtype=F32)
        da, dv_, dcw, dcb = _glu_bwd(f"glu_bwd_{l}", ua, uv, dgl, w["conv_w"][l], row(w["conv_b"][l]))
        dh2 = _mm(f"up_a_bwd_{l}", da, w["w_up_a"][l], mode="nt", out_dtype=F32)
        dh2 = _mm(f"up_v_bwd_{l}", dv_, w["w_up_v"][l], mode="nt", out_dtype=F32, add=dh2)
        per_layer["w_up_a"][l] = _mm(f"up_a_wgrad_{l}", h2, da, mode="tn", out_dtype=F32)
        per_layer["w_up_v"][l] = _mm(f"up_v_wgrad_{l}", h2, dv_, mode="tn", out_dtype=F32)
        per_layer["conv_w"][l], per_layer["conv_b"][l] = dcw, dcb.reshape(-1)
        dx_mid, dn2, dsh2, dsc2 = _rms_bwd(f"norm2_bwd_{l}", x_mid, row(w["norm2_g"][l]), dh2, sc2, dx_in=dx)
        per_layer["norm2_g"][l] = dn2.reshape(-1)
        if l < N_A_LAYERS:
            zb, pooled = mix
            dh1, dpw, dpb, dps, dg1 = _pool_bwd(f"pool_bwd_{l}", dx_mid, zb, pooled, w["pool_w"][l], row(w["pool_scale"][l]), g1)
            per_a["pool_w"][l], per_a["pool_b"][l], per_a["pool_scale"][l] = dpw, dpb.reshape(-1), dps.reshape(-1)
        else:
            j = l - N_A_LAYERS
            h1, cq_pre, cq, q_ext, o, lse, y = mix
            dy, dg1 = _gate_bwd(f"gate1_bwd_{l}", dx_mid, y, g1)
            do = _mm(f"wo_bwd_{l}", dy, w["w_o"][j], mode="nt")
            per_b["w_o"][j] = _mm(f"wo_wgrad_{l}", o, dy, mode="tn", out_dtype=F32)
            dq_ext, delta = _attn_dq(f"attn_dq_{l}", q_ext, tabq, kv["kn"], kv["kd"], kv["v"], o, lse, do)
            dkv = _attn_dkv(f"attn_dkv_{l}", q_ext, tabq, kv["kn"], kv["kd"], kv["v"], lse, delta, do, acc_in=dkv)
            dcq = _mm(f"uq_bwd_{l}", dq_ext, w["w_uq_ext"][j], mode="nt", out_dtype=F32)
            per_b["w_uq_ext"][j] = _mm(f"uq_wgrad_{l}", cq, dq_ext, mode="tn", out_dtype=F32)
            dcq_pre, dqn = _rms_bwd(f"qnorm_bwd_{l}", cq_pre, row(w["q_norm_g"][j]), dcq, out_dtype=BF16)
            per_b["q_norm_g"][j] = dqn.reshape(-1)
            dh1 = _mm(f"dq_bwd_{l}", dcq_pre, w["w_dq"][j], mode="nt")
            per_b["w_dq"][j] = _mm(f"dq_wgrad_{l}", h1, dcq_pre, mode="tn", out_dtype=F32)
        dx, dn1, dsh1, dsc1 = _rms_bwd(f"norm1_bwd_{l}", x_in, row(w["norm1_g"][l]), dh1, sc1, dx_in=dx_mid)
        per_layer["norm1_g"][l] = dn1.reshape(-1)
        dmods[l] = jnp.concatenate([dsh1, dsc1, dg1, dsh2, dsc2, dg2], axis=-1).reshape(-1)
    for group in (per_layer, per_a, per_b):
        for k, vals in group.items():
            g[k] = jnp.stack(vals)
    return loss, dx, g, jnp.stack(dmods)


def _my_index():
    return 4 * lax.axis_index("x") + 2 * lax.axis_index("y") + lax.axis_index("c")


def _peer(k):
    x, y, c = lax.axis_index("x"), lax.axis_index("y"), lax.axis_index("c")
    return (1 - x if k & 4 else x, 1 - y if k & 2 else y, 1 - c if k & 1 else c)


def _index_of(pos):
    return 4 * pos[0] + 2 * pos[1] + pos[2]


def _exchange(name, x, scatter):
    block = x.shape[1:] if scatter else x.shape

    def body(x_ref, o_ref, send_sems, recv_sems, local_sem):
        me = _my_index()
        mine = pltpu.make_async_copy(x_ref.at[me] if scatter else x_ref, o_ref.at[me], local_sem)
        mine.start()
        sends = []
        for k in range(1, N_DEV):
            peer = _peer(k)
            cp = pltpu.make_async_remote_copy(
                src_ref=x_ref.at[_index_of(peer)] if scatter else x_ref, dst_ref=o_ref.at[me],
                send_sem=send_sems.at[k - 1], recv_sem=recv_sems.at[k - 1], device_id=peer, device_id_type=MESH)
            cp.start()
            sends.append(cp)
        for k in range(1, N_DEV):
            peer = _peer(k)
            pltpu.make_async_remote_copy(
                src_ref=x_ref.at[me] if scatter else x_ref, dst_ref=o_ref.at[_index_of(peer)],
                send_sem=send_sems.at[k - 1], recv_sem=recv_sems.at[k - 1], device_id=peer, device_id_type=MESH).wait_recv()
        for cp in sends:
            cp.wait_send()
        mine.wait()

    return pl.pallas_call(
        body, name=name, out_shape=jax.ShapeDtypeStruct((N_DEV,) + tuple(block), x.dtype),
        in_specs=[pl.BlockSpec(memory_space=pl.ANY)], out_specs=pl.BlockSpec(memory_space=pl.ANY),
        scratch_shapes=[pltpu.SemaphoreType.DMA((N_DEV - 1,)), pltpu.SemaphoreType.DMA((N_DEV - 1,)), pltpu.SemaphoreType.DMA(())],
    )(x)


def _pack(arrays, dtype, row_multiple):
    flat = jnp.concatenate([a.astype(dtype).reshape(-1) for a in arrays])
    rows = -(-flat.shape[0] // (LANES * row_multiple)) * row_multiple
    return jnp.pad(flat, (0, rows * LANES - flat.shape[0])).reshape(rows, LANES)


def _pack8(arrays, dtype, row_multiple):
    flat = jnp.concatenate([a.astype(dtype).reshape(N_DEV, -1) for a in arrays], axis=1)
    rows = -(-flat.shape[1] // (LANES * row_multiple)) * row_multiple
    return jnp.pad(flat, ((0, 0), (0, rows * LANES - flat.shape[1]))).reshape(N_DEV, rows, LANES)


def _unpack(packed, shapes):
    lead = packed.shape[:-2]
    flat = packed.reshape(lead + (-1,))
    out, off = [], 0
    for shp in shapes:
        size = 1
        for d in shp:
            size *= d
        out.append(flat[..., off:off + size].reshape(lead + tuple(shp)))
        off += size
    return out


def _unshard(g8, axis):
    t = jnp.moveaxis(g8, 0, axis)
    return t.reshape(t.shape[:axis] + (t.shape[axis] * t.shape[axis + 1],) + t.shape[axis + 2:])


def _shard8(full, axis):
    n = full.shape[axis] // N_DEV
    t = full.reshape(full.shape[:axis] + (N_DEV, n) + full.shape[axis + 1:])
    return jnp.moveaxis(t, axis, 0)


MATMUL_WEIGHTS = (("pool_w", 2), ("w_dkv", 0), ("w_uk", 1), ("w_uv", 1), ("w_dq", 1), ("w_uq", 2), ("w_o", 1), ("w_up", 2), ("w_down", 1))
VECTOR_WEIGHTS = (("pool_b", 1), ("pool_scale", 1), ("conv_w", 2))
REPLICATED_WEIGHTS = ("norm1_g", "norm2_g", "kv_in_g", "ckv_norm_g", "q_norm_g", "conv_b", "final_g")
WEIGHT_ORDER = ("mod_w", "mod_b", "norm1_g", "norm2_g", "pool_w", "pool_b", "pool_scale", "kv_in_g", "w_dkv", "ckv_norm_g", "w_uk",
                "w_uv", "w_dq", "q_norm_g", "w_uq", "w_o", "w_up", "conv_w", "conv_b", "w_down", "final_g")
BIG_ROW_MULTIPLE = 1024
SMALL_ROW_MULTIPLE = 16


def _as_2d(a):
    if a.ndim == 1:
        return a.reshape(-1, LANES)
    return a.reshape(-1, a.shape[-1])


def kernel(x, c, positions, mod_w, mod_b, norm1_g, norm2_g, pool_w, pool_b, pool_scale, kv_in_g, w_dkv, ckv_norm_g, w_uk, w_uv, w_dq, q_norm_g, w_uq, w_o, w_up, conv_w, conv_b, w_down, final_g, loss_target, m_mod_w, m_mod_b, m_norm1_g, m_norm2_g, m_pool_w, m_pool_b, m_pool_scale, m_kv_in_g, m_w_dkv, m_ckv_norm_g, m_w_uk, m_w_uv, m_w_dq, m_q_norm_g, m_w_uq, m_w_o, m_w_up, m_conv_w, m_conv_b, m_w_down, m_final_g, v_mod_w, v_mod_b, v_norm1_g, v_norm2_g, v_pool_w, v_pool_b, v_pool_scale, v_kv_in_g, v_w_dkv, v_ckv_norm_g, v_w_uk, v_w_uv, v_w_dq, v_q_norm_g, v_w_uq, v_w_o, v_w_up, v_conv_w, v_conv_b, v_w_down, v_final_g):
    shard = dict(mod_w=mod_w, mod_b=mod_b, norm1_g=norm1_g, norm2_g=norm2_g, pool_w=pool_w, pool_b=pool_b, pool_scale=pool_scale,
                 kv_in_g=kv_in_g, w_dkv=w_dkv, ckv_norm_g=ckv_norm_g, w_uk=w_uk, w_uv=w_uv, w_dq=w_dq, q_norm_g=q_norm_g, w_uq=w_uq,
                 w_o=w_o, w_up=w_up, conv_w=conv_w, conv_b=conv_b, w_down=w_down, final_g=final_g)
    mom_m = dict(mod_w=m_mod_w, mod_b=m_mod_b, norm1_g=m_norm1_g, norm2_g=m_norm2_g, pool_w=m_pool_w, pool_b=m_pool_b,
                 pool_scale=m_pool_scale, kv_in_g=m_kv_in_g, w_dkv=m_w_dkv, ckv_norm_g=m_ckv_norm_g, w_uk=m_w_uk, w_uv=m_w_uv,
                 w_dq=m_w_dq, q_norm_g=m_q_norm_g, w_uq=m_w_uq, w_o=m_w_o, w_up=m_w_up, conv_w=m_conv_w, conv_b=m_conv_b,
                 w_down=m_w_down, final_g=m_final_g)
    mom_v = dict(mod_w=v_mod_w, mod_b=v_mod_b, norm1_g=v_norm1_g, norm2_g=v_norm2_g, pool_w=v_pool_w, pool_b=v_pool_b,
                 pool_scale=v_pool_scale, kv_in_g=v_kv_in_g, w_dkv=v_w_dkv, ckv_norm_g=v_ckv_norm_g, w_uk=v_w_uk, w_uv=v_w_uv,
                 w_dq=v_w_dq, q_norm_g=v_q_norm_g, w_uq=v_w_uq, w_o=v_w_o, w_up=v_w_up, conv_w=v_conv_w, conv_b=v_conv_b,
                 w_down=v_w_down, final_g=v_final_g)
    me = _my_index()
    d6 = N_MOD * D_MODEL
    mod_cols = d6 // N_DEV

    small_in = [c] + [shard[k] for k, _ in VECTOR_WEIGHTS]
    small_all = _exchange("gather_vectors", _pack(small_in, F32, SMALL_ROW_MULTIPLE), scatter=False)
    parts = _unpack(small_all, [a.shape for a in small_in])
    c_all = jnp.pad(parts[0].reshape(N_DEV, D_MODEL), ((0, N_DEV), (0, 0)))
    w = {k: _unshard(p, ax + 0) for (k, ax), p in zip(VECTOR_WEIGHTS, parts[1:])}
    big_in = [shard[k] for k, _ in MATMUL_WEIGHTS]
    big_all = _exchange("gather_weights", _pack(big_in, BF16, BIG_ROW_MULTIPLE), scatter=False)
    for (k, ax), p in zip(MATMUL_WEIGHTS, _unpack(big_all, [a.shape for a in big_in])):
        w[k] = _unshard(p, ax)
    for k in REPLICATED_WEIGHTS:
        w[k] = shard[k]
    w["w_up_a"], w["w_up_v"] = w["w_up"][:, :, :D_FF], w["w_up"][:, :, D_FF:]
    w["w_uq_ext"] = jnp.stack([_extend_w_uq(w["w_uq"][j]) for j in range(N_B_LAYERS)])
    w["w_dkv_ext"] = _extend_w_dkv(w["w_dkv"])

    my_mod_b = lax.dynamic_slice_in_dim(mod_b, me * mod_cols, mod_cols, axis=1)
    mods_mine = _mods_fwd("mods_fwd", c_all, mod_w, my_mod_b)
    mods_all = _exchange("gather_mods", _pack([mods_mine], F32, SMALL_ROW_MULTIPLE), scatter=False)
    mods_all = _unpack(mods_all, [mods_mine.shape])[0]
    mods = lax.dynamic_index_in_dim(mods_all, me, axis=2, keepdims=False)
    mods = jnp.moveaxis(mods, 0, 1).reshape(DEPTH, d6)

    tabq, tabk = _rope_tables(positions[0])
    loss_row, dx, g, dmods = _forward_backward(x[0], loss_target[0], mods, tabq, tabk, w)
    g["w_up"] = jnp.concatenate([g.pop("w_up_a"), g.pop("w_up_v")], axis=-1)
    g["w_uq"] = jnp.stack([_fold_w_uq_grad(g["w_uq_ext"][j]) for j in range(N_B_LAYERS)])
    g["w_dkv"] = _fold_w_dkv_grad(g["w_dkv_ext"])

    big_shapes = [shard[k].shape for k, _ in MATMUL_WEIGHTS]
    sent = _pack8([_shard8(g[k], ax) for k, ax in MATMUL_WEIGHTS], BF16, BIG_ROW_MULTIPLE)
    got = _exchange("scatter_grads", sent, scatter=True)
    grads = dict(zip([k for k, _ in MATMUL_WEIGHTS], _unpack(_sum8("sum_grads", got), big_shapes)))

    small_names = REPLICATED_WEIGHTS + tuple(k for k, _ in VECTOR_WEIGHTS)
    small_out = [dmods] + [g[k] for k in small_names] + [loss_row]
    small_shapes = [a.shape for a in small_out]
    small_got = _exchange("gather_small_grads", _pack(small_out, F32, SMALL_ROW_MULTIPLE), scatter=False)
    summed = _unpack(_sum8("sum_small_grads", small_got), small_shapes)
    grads["mod_b"] = summed[0]
    for k, s in zip(small_names, summed[1:-1]):
        grads[k] = s
    for k, ax in VECTOR_WEIGHTS:
        n = shard[k].shape[ax]
        grads[k] = lax.dynamic_slice_in_dim(grads[k], me * n, n, axis=ax)
    loss = summed[-1][0, 0]
    dmods_all = _unpack(small_got, small_shapes)[0]
    dm_mine = lax.dynamic_slice_in_dim(dmods_all, me * mod_cols, mod_cols, axis=2)
    dm_mine = jnp.pad(jnp.moveaxis(dm_mine, 0, 1), ((0, 0), (0, N_DEV), (0, 0)))
    grads["mod_w"] = _mods_bwd("mods_bwd", c_all, dm_mine)

    delta, new_m, new_v = {}, {}, {}
    for k in WEIGHT_ORDER:
        shp = shard[k].shape
        grads[k] = grads[k].reshape(shp)
        d_, m_, v_ = _adamw(f"adamw_{k}", _as_2d(shard[k]), _as_2d(grads[k]), _as_2d(mom_m[k]), _as_2d(mom_v[k]))
        delta[k], new_m[k], new_v[k] = d_.reshape(shp), m_.reshape(shp), v_.reshape(shp)
    return (loss, dx[None], *[grads[k] for k in WEIGHT_ORDER], *[delta[k] for k in WEIGHT_ORDER],
            *[new_m[k] for k in WEIGHT_ORDER], *[new_v[k] for k in WEIGHT_ORDER])
```

```python
import functools

import jax
import jax.numpy as jnp
from jax import lax
from jax.experimental import pallas as pl
from jax.experimental.pallas import tpu as pltpu

F32 = jnp.float32
BF16 = jnp.bfloat16

D_MODEL = 1024
DEPTH = 4
N_A_LAYERS = 2
N_B_LAYERS = 2
POOL_WINDOWS = (2, 4, 8, 16)
POOL_GROUP = 256
N_HEADS = 8
QK_NOPE = 128
QK_ROPE = 64
V_HEAD = 128
QK_HEAD = QK_NOPE + QK_ROPE
Q_RANK = 384
KV_RANK = 256
ROPE_THETA = 10000.0
D_FF = 2816
EPS = 1e-6
N_MOD = 6
ADAM_LR = 0.001
ADAM_B1 = 0.9
ADAM_B2 = 0.999
ADAM_EPS = 1e-08
ADAM_WD = 0.01
ADAM_STEP = 10

N_DEV = 8
LANES = 128
Q_EXT = 256
VMEM_LIMIT_BYTES = 48 * 1024 * 1024
MESH = pl.DeviceIdType.MESH
NEG_BIG = -0.7 * float(jnp.finfo(jnp.float32).max)


def _params(sem):
    return pltpu.CompilerParams(dimension_semantics=sem, vmem_limit_bytes=VMEM_LIMIT_BYTES)


def _tile(n, cap):
    if n <= cap:
        return n
    best = None
    for d in range(LANES, cap + 1, LANES):
        if n % d == 0:
            best = d
    assert best is not None, (n, cap)
    return best


def _dot(a, b, dims):
    return lax.dot_general(a, b, (dims, ((), ())), preferred_element_type=F32)


NN = ((1,), (0,))
NT = ((1,), (1,))
TN = ((0,), (0,))


def _mm(name, a, b, mode="nn", out_dtype=BF16, add=None, resid=None, gate=None,
        tm_cap=1024, tn_cap=1408, tk_cap=1408):
    if mode == "tn":
        kdim, m = a.shape
    else:
        m, kdim = a.shape
    n = b.shape[0] if mode == "nt" else b.shape[1]
    tm, tn, tk = _tile(m, tm_cap), _tile(n, tn_cap), _tile(kdim, tk_cap)
    nk = kdim // tk
    dims = {"nn": NN, "nt": NT, "tn": TN}[mode]
    a_spec = pl.BlockSpec((tk, tm), lambda i, j, k: (k, i)) if mode == "tn" else pl.BlockSpec((tm, tk), lambda i, j, k: (i, k))
    b_spec = pl.BlockSpec((tn, tk), lambda i, j, k: (j, k)) if mode == "nt" else pl.BlockSpec((tk, tn), lambda i, j, k: (k, j))
    o_spec = pl.BlockSpec((tm, tn), lambda i, j, k: (i, j))
    g_spec = pl.BlockSpec((1, tn), lambda i, j, k: (0, j))
    gated = resid is not None

    def body(*refs):
        a_ref, b_ref = refs[0], refs[1]
        acc = refs[-1]
        k = pl.program_id(2)

        @pl.when(k == 0)
        def _():
            acc[...] = jnp.zeros_like(acc)

        acc[...] += _dot(a_ref[...].astype(BF16), b_ref[...].astype(BF16), dims)

        @pl.when(k == nk - 1)
        def _():
            if gated:
                r_ref, g_ref, y_ref, x_ref = refs[2:6]
                y_ref[...] = acc[...]
                x_ref[...] = r_ref[...] + g_ref[...] * acc[...]
            elif add is not None:
                refs[3][...] = (acc[...] + refs[2][...].astype(F32)).astype(out_dtype)
            else:
                refs[2][...] = acc[...].astype(out_dtype)

    ins, in_specs = [a, b], [a_spec, b_spec]
    if gated:
        ins += [resid, gate]
        in_specs += [o_spec, g_spec]
        out_shape = (jax.ShapeDtypeStruct((m, n), F32), jax.ShapeDtypeStruct((m, n), F32))
        out_specs = (o_spec, o_spec)
    else:
        if add is not None:
            ins.append(add)
            in_specs.append(o_spec)
        out_shape = jax.ShapeDtypeStruct((m, n), out_dtype)
        out_specs = o_spec
    return pl.pallas_call(
        body, name=name, grid=(m // tm, n // tn, nk), in_specs=in_specs, out_specs=out_specs, out_shape=out_shape,
        scratch_shapes=[pltpu.VMEM((tm, tn), F32)],
        compiler_params=_params(("parallel", "parallel", "arbitrary")),
    )(*ins)


def _rowwise(name, fn, tiled, bcast, outs, sums=(), tr=512):
    tiled = [t if isinstance(t, tuple) else (t, t.shape[1], 0) for t in tiled]
    s = tiled[0][0].shape[0]
    tr = min(tr, s)
    assert s % tr == 0
    n_t, n_b, n_o = len(tiled), len(bcast), len(outs)

    def body(*refs):
        i = pl.program_id(0)
        vals = [r[...] for r in refs[:n_t + n_b]]
        o_vals, s_vals = fn(*vals)
        for r, v in zip(refs[n_t + n_b:n_t + n_b + n_o], o_vals):
            r[...] = v.astype(r.dtype)
        s_refs = refs[n_t + n_b + n_o:]

        @pl.when(i == 0)
        def _():
            for r in s_refs:
                r[...] = jnp.zeros_like(r)

        for r, v in zip(s_refs, s_vals):
            r[...] += v

    in_specs = [pl.BlockSpec((tr, n), functools.partial(lambda cb, i: (i, cb), cb)) for (_, n, cb) in tiled]
    in_specs += [pl.BlockSpec(b.shape, functools.partial(lambda nd, i: (0,) * nd, b.ndim)) for b in bcast]
    out_specs = [pl.BlockSpec((tr, n), lambda i: (i, 0)) for (n, _) in outs]
    out_specs += [pl.BlockSpec((1, n), lambda i: (0, 0)) for n in sums]
    out_shape = [jax.ShapeDtypeStruct((s, n), dt) for (n, dt) in outs]
    out_shape += [jax.ShapeDtypeStruct((1, n), F32) for n in sums]
    res = pl.pallas_call(
        body, name=name, grid=(s // tr,), in_specs=in_specs, out_specs=tuple(out_specs), out_shape=tuple(out_shape),
        compiler_params=_params(("arbitrary",)),
    )(*[t[0] for t in tiled], *bcast)
    return res


def _colsum(v):
    return jnp.sum(v, axis=0, keepdims=True)


def _rms_fwd(name, x, g, scale=None, shift=None, out_dtype=BF16, ncols=None):
    mod = scale is not None

    def fn(xv, gv, *ss):
        y = xv * lax.rsqrt(jnp.mean(xv * xv, axis=-1, keepdims=True) + EPS) * gv
        if mod:
            y = y * (1.0 + ss[0]) + ss[1]
        return (y,), ()

    n = ncols or x.shape[1]
    return _rowwise(name, fn, [(x, n, 0)], [g] + ([scale, shift] if mod else []), [(n, out_dtype)])[0]


def _rms_bwd(name, x, g, dh, scale=None, dx_in=None, ncols=None, out_dtype=F32):
    mod = scale is not None
    has_in = dx_in is not None

    def fn(*vals):
        xv, dhv = vals[0], vals[1].astype(F32)
        rest = list(vals[2:])
        dxi = rest.pop(0) if has_in else None
        gv = rest.pop(0)
        rstd = lax.rsqrt(jnp.mean(xv * xv, axis=-1, keepdims=True) + EPS)
        xhat = xv * rstd
        sums = []
        if mod:
            sc = rest.pop(0)
            dyn = dhv * (1.0 + sc)
            dshift, dscale = _colsum(dhv), _colsum(dhv * (xhat * gv))
        else:
            dyn = dhv
        dg = _colsum(dyn * xhat)
        dxhat = dyn * gv
        dx = rstd * (dxhat - xhat * jnp.mean(dxhat * xhat, axis=-1, keepdims=True))
        if has_in:
            dx = dx + dxi
        sums = [dg] + ([dshift, dscale] if mod else [])
        return (dx,), sums

    n = ncols or x.shape[1]
    tiled = [(x, n, 0), dh] + ([dx_in] if has_in else [])
    return _rowwise(name, fn, tiled, [g] + ([scale] if mod else []), [(n, out_dtype)], [n] * (3 if mod else 1))


def _gate_bwd(name, dxn, y, g):
    def fn(dv, yv, gv):
        return (gv * dv,), (_colsum(dv * yv),)

    n = dxn.shape[1]
    return _rowwise(name, fn, [dxn, y], [g], [(n, BF16)], [n])


def _loss_head(name, x, g, target):
    n = x.shape[1]

    def fn(xv, tv, gv):
        rstd = lax.rsqrt(jnp.mean(xv * xv, axis=-1, keepdims=True) + EPS)
        xhat = xv * rstd
        err = xhat * gv - tv
        loss = 0.5 * jnp.sum(jnp.sum(err * err, axis=-1, keepdims=True) / n, axis=0, keepdims=True)
        dy = err / n
        dg = _colsum(dy * xhat)
        dxhat = dy * gv
        dx = rstd * (dxhat - xhat * jnp.mean(dxhat * xhat, axis=-1, keepdims=True))
        return (dx,), (dg, jnp.broadcast_to(loss, (1, LANES)))

    return _rowwise(name, fn, [x, target], [g], [(n, F32)], [n, LANES])


def _krope_fwd(name, kv_ext, tabk):
    def fn(xv, tv):
        t = xv * tv
        return (t + pltpu.roll(t, 64, 1),), ()

    return _rowwise(name, fn, [(kv_ext, LANES, 2), tabk], [], [(LANES, BF16)])[0]


def _krope_bwd(name, dkd, tabk):
    def fn(dv, tv):
        return ((dv + pltpu.roll(dv, 64, 1)) * tv,), ()

    return _rowwise(name, fn, [dkd, tabk], [], [(LANES, F32)])[0]


def _adamw(name, w, g, m, v):
    def fn(wv, gv, mv, vv):
        m2 = ADAM_B1 * mv + (1.0 - ADAM_B1) * gv
        v2 = ADAM_B2 * vv + (1.0 - ADAM_B2) * (gv * gv)
        m_hat = m2 / (1.0 - ADAM_B1 ** ADAM_STEP)
        v_hat = v2 / (1.0 - ADAM_B2 ** ADAM_STEP)
        delta = -ADAM_LR * (m_hat / (jnp.sqrt(v_hat) + ADAM_EPS) + ADAM_WD * wv)
        return (delta, m2, v2), ()

    r, c = w.shape
    tr = r
    for cand in (512, 256, 128, 64, 32, 16, 8):
        if r % cand == 0 and r > cand:
            tr = cand
            break
    return _rowwise(name, fn, [w, g, m, v], [], [(c, F32)] * 3, tr=tr)


def _sum8(name, parts):
    _, r, c = parts.shape
    tr = r
    for cand in (2048, 1024, 512, 256, 128, 64, 32, 16):
        if r % cand == 0 and r > cand and cand * c <= 256 * 1024:
            tr = cand
            break

    def body(p_ref, o_ref):
        acc = p_ref[0].astype(F32)
        for k in range(1, N_DEV):
            acc = acc + p_ref[k].astype(F32)
        o_ref[...] = acc

    return pl.pallas_call(
        body, name=name, grid=(r // tr,), in_specs=[pl.BlockSpec((N_DEV, tr, c), lambda i: (0, i, 0))],
        out_specs=pl.BlockSpec((tr, c), lambda i: (i, 0)), out_shape=jax.ShapeDtypeStruct((r, c), F32),
        compiler_params=_params(("parallel",)),
    )(parts)


def _mods_fwd(name, c_all, w, b):
    depth, d, n = w.shape

    def body(c_ref, w_ref, b_ref, o_ref):
        cv = c_ref[...]
        sc = (cv * (1.0 / (1.0 + jnp.exp(-cv)))).astype(BF16)
        o_ref[0] = _dot(sc, w_ref[0].astype(BF16), NN) + b_ref[0]

    return pl.pallas_call(
        body, name=name, grid=(depth,),
        in_specs=[pl.BlockSpec(c_all.shape, lambda l: (0, 0)), pl.BlockSpec((1, d, n), lambda l: (l, 0, 0)),
                  pl.BlockSpec((1, 1, n), lambda l: (l, 0, 0))],
        out_specs=pl.BlockSpec((1, c_all.shape[0], n), lambda l: (l, 0, 0)),
        out_shape=jax.ShapeDtypeStruct((depth, c_all.shape[0], n), F32),
        compiler_params=_params(("parallel",)),
    )(c_all, w, b.reshape(depth, 1, n))


def _mods_bwd(name, c_all, dm):
    depth, rows, n = dm.shape
    d = c_all.shape[1]

    def body(c_ref, dm_ref, o_ref):
        cv = c_ref[...]
        sc = (cv * (1.0 / (1.0 + jnp.exp(-cv)))).astype(BF16)
        o_ref[0] = _dot(sc, dm_ref[0].astype(BF16), TN)

    return pl.pallas_call(
        body, name=name, grid=(depth,),
        in_specs=[pl.BlockSpec(c_all.shape, lambda l: (0, 0)), pl.BlockSpec((1, rows, n), lambda l: (l, 0, 0))],
        out_specs=pl.BlockSpec((1, d, n), lambda l: (l, 0, 0)),
        out_shape=jax.ShapeDtypeStruct((depth, d, n), F32),
        compiler_params=_params(("parallel",)),
    )(c_all, dm)


POOL_TILE = 256


def _split_dot(band, val):
    hi = val.astype(BF16)
    lo = (val - hi.astype(F32)).astype(BF16)
    return _dot(band, hi, NN) + _dot(band, lo, NN)


def _pool_fwd(name, h1, x, pw, pb, ps, g1):
    s, d = h1.shape
    t = POOL_TILE

    def body(hc_ref, hp_ref, x_ref, pw_ref, pb_ref, ps_ref, g_ref, xo_ref, zb_ref, pooled_ref):
        i = pl.program_id(0)
        r = lax.broadcasted_iota(jnp.int32, (t, t), 0)
        j = lax.broadcasted_iota(jnp.int32, (t, t), 1)
        pos = (i * t + lax.broadcasted_iota(jnp.int32, (t, 1), 0) + 1).astype(F32)
        has_prev = (i > 0).astype(F32)
        for grp, w in enumerate(POOL_WINDOWS):
            cs = slice(grp * POOL_GROUP, (grp + 1) * POOL_GROUP)
            hc = hc_ref[:, cs]
            band_cur = ((r - j >= 0) & (r - j < w)).astype(BF16)
            band_prev = (r + t - j < w).astype(BF16)
            ssum = _split_dot(band_cur, hc) + has_prev * _split_dot(band_prev, hp_ref[:, cs])
            pooled = (ssum / jnp.minimum(pos, float(w)) - hc).astype(BF16)
            zb = _dot(pooled, pw_ref[grp], NN) + pb_ref[:, cs]
            xo_ref[:, cs] = x_ref[:, cs] + g_ref[:, cs] * (zb * ps_ref[:, cs])
            zb_ref[:, cs] = zb
            pooled_ref[:, cs] = pooled

    row = pl.BlockSpec((t, d), lambda i: (i, 0))
    vec = pl.BlockSpec((1, d), lambda i: (0, 0))
    return pl.pallas_call(
        body, name=name, grid=(s // t,),
        in_specs=[row, pl.BlockSpec((t, d), lambda i: (jnp.maximum(i - 1, 0), 0)), row,
                  pl.BlockSpec(pw.shape, lambda i: (0, 0, 0)), vec, vec, vec],
        out_specs=(row, row, row),
        out_shape=(jax.ShapeDtypeStruct((s, d), F32), jax.ShapeDtypeStruct((s, d), F32), jax.ShapeDtypeStruct((s, d), BF16)),
        compiler_params=_params(("parallel",)),
    )(h1, h1, x, pw, pb, ps, g1)


def _pool_bwd(name, dxn, zb, pooled, pw, ps, g1):
    s, d = dxn.shape
    t = POOL_TILE
    nt = s // t

    def body(dc_ref, dn_ref, zb_ref, pooled_ref, pw_ref, ps_ref, g_ref, dh_ref, dpw_ref, dpb_ref, dps_ref, dg_ref):
        i = pl.program_id(0)

        @pl.when(i == 0)
        def _():
            dpw_ref[...] = jnp.zeros_like(dpw_ref)
            dpb_ref[...] = jnp.zeros_like(dpb_ref)
            dps_ref[...] = jnp.zeros_like(dps_ref)
            dg_ref[...] = jnp.zeros_like(dg_ref)

        jj = lax.broadcasted_iota(jnp.int32, (t, t), 0)
        rr = lax.broadcasted_iota(jnp.int32, (t, t), 1)
        pos = (i * t + lax.broadcasted_iota(jnp.int32, (t, 1), 0) + 1).astype(F32)
        has_next = (i < nt - 1).astype(F32)
        for grp, w in enumerate(POOL_WINDOWS):
            cs = slice(grp * POOL_GROUP, (grp + 1) * POOL_GROUP)
            gv, psv, zbv, dxc = g_ref[:, cs], ps_ref[:, cs], zb_ref[:, cs], dc_ref[:, cs]
            dg_ref[:, cs] += _colsum(dxc * (zbv * psv))
            dy = gv * dxc
            dps_ref[:, cs] += _colsum(dy * zbv)
            dz = dy * psv
            dpb_ref[:, cs] += _colsum(dz)
            dzb = dz.astype(BF16)
            dpw_ref[grp] += _dot(pooled_ref[:, cs], dzb, TN)
            dp = _dot(dzb, pw_ref[grp], NT)
            dzn = (gv * dn_ref[:, cs] * psv).astype(BF16)
            dpn = _dot(dzn, pw_ref[grp], NT) * (has_next / float(w))
            band_cur = ((rr - jj >= 0) & (rr - jj < w)).astype(BF16)
            band_next = (rr + t - jj < w).astype(BF16)
            dh_ref[:, cs] = _split_dot(band_cur, dp / jnp.minimum(pos, float(w))) + _split_dot(band_next, dpn) - dp

    row = pl.BlockSpec((t, d), lambda i: (i, 0))
    vec = pl.BlockSpec((1, d), lambda i: (0, 0))
    wspec = pl.BlockSpec(pw.shape, lambda i: (0, 0, 0))
    return pl.pallas_call(
        body, name=name, grid=(nt,),
        in_specs=[row, pl.BlockSpec((t, d), lambda i: (jnp.minimum(i + 1, nt - 1), 0)), row, row, wspec, vec, vec],
        out_specs=(row, wspec, vec, vec, vec),
        out_shape=(jax.ShapeDtypeStruct((s, d), F32), jax.ShapeDtypeStruct(pw.shape, F32),
                   jax.ShapeDtypeStruct((1, d), F32), jax.ShapeDtypeStruct((1, d), F32), jax.ShapeDtypeStruct((1, d), F32)),
        compiler_params=_params(("arbitrary",)),
    )(dxn, dxn, zb, pooled, pw, ps, g1)


GLU_TILE = 256
HALO = 16
INV_SQRT2 = 0.7071067811865476
INV_SQRT_2PI = 0.3989422804014327


def _gelu(xv):
    return 0.5 * xv * (1.0 + lax.erf(xv * INV_SQRT2))


def _gelu_grad(xv):
    return 0.5 * (1.0 + lax.erf(xv * INV_SQRT2)) + xv * (INV_SQRT_2PI * jnp.exp(-0.5 * xv * xv))


def _glu_fwd(name, ua, uv, cw, cb):
    s, f = ua.shape
    t, tf = GLU_TILE, _tile(f, 1408)

    def body(a_ref, ah_ref, v_ref, cw_ref, cb_ref, o_ref):
        i = pl.program_id(1)
        has_prev = (i > 0).astype(F32)
        ext = jnp.concatenate([ah_ref[...].astype(F32) * has_prev, a_ref[...].astype(F32)], axis=0)
        e1 = pltpu.roll(ext, 1, 0)[HALO:]
        e2 = pltpu.roll(ext, 2, 0)[HALO:]
        pre = e2 * cw_ref[0:1, :] + e1 * cw_ref[1:2, :] + ext[HALO:] * cw_ref[2:3, :] + cb_ref[...]
        o_ref[...] = (_gelu(pre) * v_ref[...].astype(F32)).astype(o_ref.dtype)

    blk = pl.BlockSpec((t, tf), lambda j, i: (i, j))
    halo = pl.BlockSpec((HALO, tf), lambda j, i: (jnp.maximum(i * (t // HALO) - 1, 0), j))
    return pl.pallas_call(
        body, name=name, grid=(f // tf, s // t),
        in_specs=[blk, halo, blk, pl.BlockSpec((3, tf), lambda j, i: (0, j)), pl.BlockSpec((1, tf), lambda j, i: (0, j))],
        out_specs=blk, out_shape=jax.ShapeDtypeStruct((s, f), BF16),
        compiler_params=_params(("parallel", "parallel")),
    )(ua, ua, uv, cw, cb)


def _glu_bwd(name, ua, uv, dgl, cw, cb):
    s, f = ua.shape
    t, tf = GLU_TILE, _tile(f, 1408)
    nt = s // t
    te = t + HALO

    def body(a_ref, ah_ref, an_ref, v_ref, vn_ref, d_ref, dn_ref, cw_ref, cb_ref, da_ref, dv_ref, dcw_ref, dcb_ref):
        i = pl.program_id(1)

        @pl.when(i == 0)
        def _():
            dcw_ref[...] = jnp.zeros_like(dcw_ref)
            dcb_ref[...] = jnp.zeros_like(dcb_ref)

        has_prev = (i > 0).astype(F32)
        has_next = (i < nt - 1).astype(F32)
        ext = jnp.concatenate([ah_ref[...].astype(F32) * has_prev, a_ref[...].astype(F32), an_ref[...].astype(F32)], axis=0)
        e0 = ext[HALO:]
        e1 = pltpu.roll(ext, 1, 0)[HALO:]
        e2 = pltpu.roll(ext, 2, 0)[HALO:]
        c0, c1, c2 = cw_ref[0:1, :], cw_ref[1:2, :], cw_ref[2:3, :]
        pre = e2 * c0 + e1 * c1 + e0 * c2 + cb_ref[...]
        vx = jnp.concatenate([v_ref[...].astype(F32), vn_ref[...].astype(F32)], axis=0)
        dx = jnp.concatenate([d_ref[...].astype(F32), dn_ref[...].astype(F32) * has_next], axis=0)
        dpre = dx * vx * _gelu_grad(pre)
        up1 = pltpu.roll(dpre, te - 1, 0)
        up2 = pltpu.roll(dpre, te - 2, 0)
        da_ref[...] = (dpre * c2 + up1 * c1 + up2 * c0)[:t].astype(da_ref.dtype)
        dv_ref[...] = (d_ref[...].astype(F32) * _gelu(pre[:t])).astype(dv_ref.dtype)
        dpt = dpre[:t]
        dcb_ref[...] += _colsum(dpt)
        dcw_ref[0:1, :] += _colsum(e2[:t] * dpt)
        dcw_ref[1:2, :] += _colsum(e1[:t] * dpt)
        dcw_ref[2:3, :] += _colsum(e0[:t] * dpt)

    blk = pl.BlockSpec((t, tf), lambda j, i: (i, j))
    prev = pl.BlockSpec((HALO, tf), lambda j, i: (jnp.maximum(i * (t // HALO) - 1, 0), j))
    nxt = pl.BlockSpec((HALO, tf), lambda j, i: (jnp.minimum((i + 1) * (t // HALO), s // HALO - 1), j))
    w3 = pl.BlockSpec((3, tf), lambda j, i: (0, j))
    w1 = pl.BlockSpec((1, tf), lambda j, i: (0, j))
    return pl.pallas_call(
        body, name=name, grid=(f // tf, nt),
        in_specs=[blk, prev, nxt, blk, nxt, blk, nxt, w3, w1],
        out_specs=(blk, blk, w3, w1),
        out_shape=(jax.ShapeDtypeStruct((s, f), BF16), jax.ShapeDtypeStruct((s, f), BF16),
                   jax.ShapeDtypeStruct((3, f), F32), jax.ShapeDtypeStruct((1, f), F32)),
        compiler_params=_params(("parallel", "arbitrary")),
    )(ua, ua, ua, uv, uv, dgl, dgl, cw, cb)


ATT_TILE = 512


def _scores(q, kn_blk, kd_blk):
    return _dot(q[:, :QK_NOPE], kn_blk, NT) + _dot(q[:, QK_NOPE:], kd_blk, NT)


def _causal_mask(sv, q0, k0):
    row = q0 + lax.broadcasted_iota(jnp.int32, sv.shape, 0)
    col = k0 + lax.broadcasted_iota(jnp.int32, sv.shape, 1)
    return jnp.where(col <= row, sv, NEG_BIG)


def _attn_fwd(name, q_ext, tabq, kn, kd, v):
    s = q_ext.shape[0]
    t = min(ATT_TILE, s)
    nq = s // t

    def body(q_ref, tab_ref, kn_ref, kd_ref, v_ref, o_ref, lse_ref, acc_ref, m_ref, l_ref):
        qi = pl.program_id(1)
        q = (q_ref[...].astype(F32) * tab_ref[...]).astype(BF16)
        acc_ref[...] = jnp.zeros_like(acc_ref)
        m_ref[...] = jnp.full_like(m_ref, NEG_BIG)
        l_ref[...] = jnp.zeros_like(l_ref)

        def step(j, masked):
            ks = pl.ds(pl.multiple_of(j * t, t), t)
            sv = _scores(q, kn_ref[ks, :], kd_ref[ks, :])
            if masked:
                sv = _causal_mask(sv, qi * t, j * t)
            m_prev = m_ref[...]
            m_new = jnp.maximum(m_prev, jnp.max(sv, axis=-1, keepdims=True))
            alpha = jnp.exp(m_prev - m_new)
            p = jnp.exp(sv - m_new)
            l_ref[...] = alpha * l_ref[...] + jnp.sum(p, axis=-1, keepdims=True)
            acc_ref[...] = alpha * acc_ref[...] + _dot(p.astype(BF16), v_ref[ks, :], NN)
            m_ref[...] = m_new

        def full_step(j, carry):
            step(j, False)
            return carry

        lax.fori_loop(0, qi, full_step, 0)
        step(qi, True)
        o_ref[...] = (acc_ref[...] / l_ref[...]).astype(o_ref.dtype)
        lse_ref[...] = jnp.broadcast_to(m_ref[...] + jnp.log(l_ref[...]), lse_ref.shape)

    head_q = pl.BlockSpec((t, Q_EXT), lambda h, i: (i, h))
    head_o = pl.BlockSpec((t, V_HEAD), lambda h, i: (i, h))
    kv_all = pl.BlockSpec((s, LANES), lambda h, i: (0, h))
    return pl.pallas_call(
        body, name=name, grid=(N_HEADS, nq),
        in_specs=[head_q, pl.BlockSpec((t, Q_EXT), lambda h, i: (i, 0)), kv_all, pl.BlockSpec((s, LANES), lambda h, i: (0, 0)), kv_all],
        out_specs=(head_o, head_o),
        out_shape=(jax.ShapeDtypeStruct((s, N_HEADS * V_HEAD), BF16), jax.ShapeDtypeStruct((s, N_HEADS * LANES), F32)),
        scratch_shapes=[pltpu.VMEM((t, V_HEAD), F32), pltpu.VMEM((t, 1), F32), pltpu.VMEM((t, 1), F32)],
        compiler_params=_params(("parallel", "parallel")),
    )(q_ext, tabq, kn, kd, v)


def _attn_dq(name, q_ext, tabq, kn, kd, v, o, lse, do):
    s = q_ext.shape[0]
    t = min(ATT_TILE, s)
    nq = s // t

    def body(q_ref, tab_ref, kn_ref, kd_ref, v_ref, o_ref, lse_ref, do_ref, dq_ref, delta_ref, accn_ref, accr_ref):
        qi = pl.program_id(1)
        tab = tab_ref[...]
        q = (q_ref[...].astype(F32) * tab).astype(BF16)
        dov = do_ref[...]
        delta = jnp.sum(dov.astype(F32) * o_ref[...].astype(F32), axis=-1, keepdims=True)
        lse = lse_ref[:, 0:1]
        accn_ref[...] = jnp.zeros_like(accn_ref)
        accr_ref[...] = jnp.zeros_like(accr_ref)

        def step(j, masked):
            ks = pl.ds(pl.multiple_of(j * t, t), t)
            kn_blk, kd_blk = kn_ref[ks, :], kd_ref[ks, :]
            sv = _scores(q, kn_blk, kd_blk)
            if masked:
                sv = _causal_mask(sv, qi * t, j * t)
            p = jnp.exp(sv - lse)
            dp = _dot(dov, v_ref[ks, :], NT)
            ds = (p * (dp - delta)).astype(BF16)
            accn_ref[...] += _dot(ds, kn_blk, NN)
            accr_ref[...] += _dot(ds, kd_blk, NN)

        def full_step(j, carry):
            step(j, False)
            return carry

        lax.fori_loop(0, qi, full_step, 0)
        step(qi, True)
        dq_ref[:, :QK_NOPE] = (accn_ref[...] * tab[:, :QK_NOPE]).astype(dq_ref.dtype)
        dq_ref[:, QK_NOPE:] = (accr_ref[...] * tab[:, QK_NOPE:]).astype(dq_ref.dtype)
        delta_ref[...] = jnp.broadcast_to(delta, delta_ref.shape)

    head_q = pl.BlockSpec((t, Q_EXT), lambda h, i: (i, h))
    head_o = pl.BlockSpec((t, V_HEAD), lambda h, i: (i, h))
    kv_all = pl.BlockSpec((s, LANES), lambda h, i: (0, h))
    return pl.pallas_call(
        body, name=name, grid=(N_HEADS, nq),
        in_specs=[head_q, pl.BlockSpec((t, Q_EXT), lambda h, i: (i, 0)), kv_all, pl.BlockSpec((s, LANES), lambda h, i: (0, 0)), kv_all,
                  head_o, head_o, head_o],
        out_specs=(head_q, head_o),
        out_shape=(jax.ShapeDtypeStruct((s, N_HEADS * Q_EXT), BF16), jax.ShapeDtypeStruct((s, N_HEADS * LANES), F32)),
        scratch_shapes=[pltpu.VMEM((t, QK_NOPE), F32), pltpu.VMEM((t, QK_NOPE), F32)],
        compiler_params=_params(("parallel", "parallel")),
    )(q_ext, tabq, kn, kd, v, o, lse, do)


def _attn_dkv(name, q_ext, tabq, kn, kd, v, lse, delta, do, acc_in=None):
    s = q_ext.shape[0]
    t = min(ATT_TILE, s)
    nq = s // t
    has_in = acc_in is not None

    def body(*refs):
        q_ref, tab_ref, kn_ref, kd_ref, v_ref, lse_ref, delta_ref, do_ref = refs[:8]
        dkn_ref, dkd_ref, dv_ref, accn_ref, accv_ref = refs[-5:]
        kj, h = pl.program_id(0), pl.program_id(1)
        kn_blk, kd_blk, v_blk = kn_ref[...], kd_ref[...], v_ref[...]
        accn_ref[...] = jnp.zeros_like(accn_ref)
        accv_ref[...] = jnp.zeros_like(accv_ref)

        @pl.when(h == 0)
        def _():
            if has_in:
                dkd_ref[...] = refs[9][...]
            else:
                dkd_ref[...] = jnp.zeros_like(dkd_ref)

        def step(i, masked):
            qs = pl.ds(pl.multiple_of(i * t, t), t)
            q = (q_ref[qs, :].astype(F32) * tab_ref[qs, :]).astype(BF16)
            sv = _scores(q, kn_blk, kd_blk)
            if masked:
                sv = _causal_mask(sv, i * t, kj * t)
            p = jnp.exp(sv - lse_ref[qs, 0:1])
            dov = do_ref[qs, :]
            accv_ref[...] += _dot(p.astype(BF16), dov, TN)
            dp = _dot(dov, v_blk, NT)
            ds = (p * (dp - delta_ref[qs, 0:1])).astype(BF16)
            accn_ref[...] += _dot(ds, q[:, :QK_NOPE], TN)
            dkd_ref[...] += _dot(ds, q[:, QK_NOPE:], TN)

        def full_step(i, carry):
            step(i, False)
            return carry

        step(kj, True)
        lax.fori_loop(kj + 1, nq, full_step, 0)
        if has_in:
            dkn_ref[...] = accn_ref[...] + refs[8][...]
            dv_ref[...] = accv_ref[...] + refs[10][...]
        else:
            dkn_ref[...] = accn_ref[...]
            dv_ref[...] = accv_ref[...]

    q_all = pl.BlockSpec((s, Q_EXT), lambda j, h: (0, h))
    o_all = pl.BlockSpec((s, LANES), lambda j, h: (0, h))
    kblk = pl.BlockSpec((t, LANES), lambda j, h: (j, h))
    kdblk = pl.BlockSpec((t, LANES), lambda j, h: (j, 0))
    ins = [q_ext, tabq, kn, kd, v, lse, delta, do]
    in_specs = [q_all, pl.BlockSpec((s, Q_EXT), lambda j, h: (0, 0)), kblk, kdblk, kblk, o_all, o_all, o_all]
    if has_in:
        ins += list(acc_in)
        in_specs += [kblk, kdblk, kblk]
    return pl.pallas_call(
        body, name=name, grid=(nq, N_HEADS), in_specs=in_specs, out_specs=(kblk, kdblk, kblk),
        out_shape=(jax.ShapeDtypeStruct((s, N_HEADS * LANES), F32), jax.ShapeDtypeStruct((s, LANES), F32),
                   jax.ShapeDtypeStruct((s, N_HEADS * LANES), F32)),
        scratch_shapes=[pltpu.VMEM((t, LANES), F32), pltpu.VMEM((t, LANES), F32)],
        compiler_params=_params(("parallel", "arbitrary")),
    )(*ins)


def _swap_halves(w):
    half = w.shape[-1] // 2
    return jnp.concatenate([-w[..., half:], w[..., :half]], axis=-1)


def _unswap_halves(g):
    half = g.shape[-1] // 2
    return jnp.concatenate([g[..., half:], -g[..., :half]], axis=-1)


def _extend_w_uq(w):
    r = w.reshape(Q_RANK, N_HEADS, QK_HEAD)
    rope = r[..., QK_NOPE:]
    return jnp.concatenate([r[..., :QK_NOPE], rope, _swap_halves(rope)], axis=-1).reshape(Q_RANK, N_HEADS * Q_EXT)


def _fold_w_uq_grad(g):
    r = g.reshape(Q_RANK, N_HEADS, Q_EXT)
    rope = r[..., QK_NOPE:QK_HEAD] + _unswap_halves(r[..., QK_HEAD:])
    return jnp.concatenate([r[..., :QK_NOPE], rope], axis=-1).reshape(Q_RANK, N_HEADS * QK_HEAD)


def _extend_w_dkv(w):
    return jnp.concatenate([w, _swap_halves(w[:, KV_RANK:])], axis=-1)


def _fold_w_dkv_grad(g):
    rope = g[:, KV_RANK:KV_RANK + QK_ROPE] + _unswap_halves(g[:, KV_RANK + QK_ROPE:])
    return jnp.concatenate([g[:, :KV_RANK], rope], axis=-1)


def _rope_tables(positions):
    inv = 1.0 / (ROPE_THETA ** (jnp.arange(0, QK_ROPE, 2, dtype=F32) / QK_ROPE))
    ang = positions.astype(F32)[:, None] * inv
    cos, sin = jnp.cos(ang), jnp.sin(ang)
    tabk = jnp.concatenate([cos, cos, sin, sin], axis=-1)
    scale = QK_HEAD ** -0.5
    tabq = jnp.concatenate([jnp.full((positions.shape[0], QK_NOPE), scale, F32), tabk * scale], axis=-1)
    return tabq, tabk


def _forward_backward(x, target, mods, tabq, tabk, w):
    row = lambda vec: vec.reshape(1, -1)
    mod = [[row(mods[l, k * D_MODEL:(k + 1) * D_MODEL]) for k in range(N_MOD)] for l in range(DEPTH)]
    saved = []
    kv = None
    for l in range(DEPTH):
        sh1, sc1, g1, sh2, sc2, g2 = mod[l]
        x_in = x
        if l < N_A_LAYERS:
            h1 = _rms_fwd(f"norm1_fwd_{l}", x, row(w["norm1_g"][l]), sc1, sh1, out_dtype=F32)
            x_mid, zb, pooled = _pool_fwd(f"pool_fwd_{l}", h1, x, w["pool_w"][l], row(w["pool_b"][l]), row(w["pool_scale"][l]), g1)
            mix = (zb, pooled)
        else:
            j = l - N_A_LAYERS
            h1 = _rms_fwd(f"norm1_fwd_{l}", x, row(w["norm1_g"][l]), sc1, sh1)
            cq_pre = _mm(f"dq_fwd_{l}", h1, w["w_dq"][j], out_dtype=F32)
            cq = _rms_fwd(f"qnorm_fwd_{l}", cq_pre, row(w["q_norm_g"][j]))
            q_ext = _mm(f"uq_fwd_{l}", cq, w["w_uq_ext"][j])
            o, lse = _attn_fwd(f"attn_fwd_{l}", q_ext, tabq, kv["kn"], kv["kd"], kv["v"])
            y, x_mid = _mm(f"wo_fwd_{l}", o, w["w_o"][j], resid=x, gate=g1)
            mix = (h1, cq_pre, cq, q_ext, o, lse, y)
        h2 = _rms_fwd(f"norm2_fwd_{l}", x_mid, row(w["norm2_g"][l]), sc2, sh2)
        ua = _mm(f"up_a_fwd_{l}", h2, w["w_up_a"][l])
        uv = _mm(f"up_v_fwd_{l}", h2, w["w_up_v"][l])
        gl = _glu_fwd(f"glu_fwd_{l}", ua, uv, w["conv_w"][l], row(w["conv_b"][l]))
        y2, x = _mm(f"down_fwd_{l}", gl, w["w_down"][l], resid=x_mid, gate=g2)
        saved.append((x_in, x_mid, h2, ua, uv, gl, y2, mix))
        if l == N_A_LAYERS - 1:
            kvn = _rms_fwd("kvin_fwd", x, row(w["kv_in_g"]))
            kv_ext = _mm("dkv_fwd", kvn, w["w_dkv_ext"], out_dtype=F32)
            ckv = _rms_fwd("ckv_fwd", kv_ext, row(w["ckv_norm_g"]), ncols=KV_RANK)
            kv = dict(x=x, kvn=kvn, kv_ext=kv_ext, ckv=ckv, kd=_krope_fwd("krope_fwd", kv_ext, tabk),
                      kn=_mm("uk_fwd", ckv, w["w_uk"]), v=_mm("uv_fwd", ckv, w["w_uv"]))

    dx, dfinal_g, loss = _loss_head("loss_head", x, row(w["final_g"]), target)
    g = {"final_g": dfinal_g.reshape(-1)}
    per_layer = {k: [None] * DEPTH for k in ("norm1_g", "norm2_g", "w_up_a", "w_up_v", "conv_w", "conv_b", "w_down")}
    per_a = {k: [None] * N_A_LAYERS for k in ("pool_w", "pool_b", "pool_scale")}
    per_b = {k: [None] * N_B_LAYERS for k in ("w_dq", "q_norm_g", "w_uq_ext", "w_o")}
    dmods = [None] * DEPTH
    dkv = None
    for l in reversed(range(DEPTH)):
        sh1, sc1, g1, sh2, sc2, g2 = mod[l]
        x_in, x_mid, h2, ua, uv, gl, y2, mix = saved[l]
        if l == N_A_LAYERS - 1:
            dkn, dkd, dv = dkv
            dckv = _mm("uk_bwd", dkn, w["w_uk"], mode="nt", out_dtype=F32)
            dckv = _mm("uv_bwd", dv, w["w_uv"], mode="nt", out_dtype=F32, add=dckv)
            g["w_uk"] = _mm("uk_wgrad", kv["ckv"], dkn, mode="tn", out_dtype=F32)
            g["w_uv"] = _mm("uv_wgrad", kv["ckv"], dv, mode="tn", out_dtype=F32)
            dkr = _krope_bwd("krope_bwd", dkd, tabk)
            dc, dckv_g = _rms_bwd("ckv_bwd", kv["kv_ext"], row(w["ckv_norm_g"]), dckv, ncols=KV_RANK, out_dtype=BF16)
            dkv_ext = jnp.concatenate([dc, dkr.astype(BF16)], axis=-1)
            dkvn = _mm("dkv_bwd", dkv_ext, w["w_dkv_ext"], mode="nt")
            g["w_dkv_ext"] = _mm("dkv_wgrad", kv["kvn"], dkv_ext, mode="tn", out_dtype=F32)
            dx, dkv_in_g = _rms_bwd("kvin_bwd", kv["x"], row(w["kv_in_g"]), dkvn, dx_in=dx)
            g["ckv_norm_g"], g["kv_in_g"] = dckv_g.reshape(-1), dkv_in_g.reshape(-1)
        dy2, dg2 = _gate_bwd(f"gate2_bwd_{l}", dx, y2, g2)
        dgl = _mm(f"down_bwd_{l}", dy2, w["w_down"][l], mode="nt")
        per_layer["w_down"][l] = _mm(f"down_wgrad_{l}", gl, dy2, mode="tn", out_dtype=F32)
        da, dv_, dcw, dcb = _glu_bwd(f"glu_bwd_{l}", ua, uv, dgl, w["conv_w"][l], row(w["conv_b"][l]))
        dh2 = _mm(f"up_a_bwd_{l}", da, w["w_up_a"][l], mode="nt", out_dtype=F32)
        dh2 = _mm(f"up_v_bwd_{l}", dv_, w["w_up_v"][l], mode="nt", out_dtype=F32, add=dh2)
        per_layer["w_up_a"][l] = _mm(f"up_a_wgrad_{l}", h2, da, mode="tn", out_dtype=F32)
        per_layer["w_up_v"][l] = _mm(f"up_v_wgrad_{l}", h2, dv_, mode="tn", out_dtype=F32)
        per_layer["conv_w"][l], per_layer["conv_b"][l] = dcw, dcb.reshape(-1)
        dx_mid, dn2, dsh2, dsc2 = _rms_bwd(f"norm2_bwd_{l}", x_mid, row(w["norm2_g"][l]), dh2, sc2, dx_in=dx)
        per_layer["norm2_g"][l] = dn2.reshape(-1)
        if l < N_A_LAYERS:
            zb, pooled = mix
            dh1, dpw, dpb, dps, dg1 = _pool_bwd(f"pool_bwd_{l}", dx_mid, zb, pooled, w["pool_w"][l], row(w["pool_scale"][l]), g1)
            per_a["pool_w"][l], per_a["pool_b"][l], per_a["pool_scale"][l] = dpw, dpb.reshape(-1), dps.reshape(-1)
        else:
            j = l - N_A_LAYERS
            h1, cq_pre, cq, q_ext, o, lse, y = mix
            dy, dg1 = _gate_bwd(f"gate1_bwd_{l}", dx_mid, y, g1)
            do = _mm(f"wo_bwd_{l}", dy, w["w_o"][j], mode="nt")
            per_b["w_o"][j] = _mm(f"wo_wgrad_{l}", o, dy, mode="tn", out_dtype=F32)
            dq_ext, delta = _attn_dq(f"attn_dq_{l}", q_ext, tabq, kv["kn"], kv["kd"], kv["v"], o, lse, do)
            dkv = _attn_dkv(f"attn_dkv_{l}", q_ext, tabq, kv["kn"], kv["kd"], kv["v"], lse, delta, do, acc_in=dkv)
            dcq = _mm(f"uq_bwd_{l}", dq_ext, w["w_uq_ext"][j], mode="nt", out_dtype=F32)
            per_b["w_uq_ext"][j] = _mm(f"uq_wgrad_{l}", cq, dq_ext, mode="tn", out_dtype=F32)
            dcq_pre, dqn = _rms_bwd(f"qnorm_bwd_{l}", cq_pre, row(w["q_norm_g"][j]), dcq, out_dtype=BF16)
            per_b["q_norm_g"][j] = dqn.reshape(-1)
            dh1 = _mm(f"dq_bwd_{l}", dcq_pre, w["w_dq"][j], mode="nt")
            per_b["w_dq"][j] = _mm(f"dq_wgrad_{l}", h1, dcq_pre, mode="tn", out_dtype=F32)
        dx, dn1, dsh1, dsc1 = _rms_bwd(f"norm1_bwd_{l}", x_in, row(w["norm1_g"][l]), dh1, sc1, dx_in=dx_mid)
        per_layer["norm1_g"][l] = dn1.reshape(-1)
        dmods[l] = jnp.concatenate([dsh1, dsc1, dg1, dsh2, dsc2, dg2], axis=-1).reshape(-1)
    for group in (per_layer, per_a, per_b):
        for k, vals in group.items():
            g[k] = jnp.stack(vals)
    return loss, dx, g, jnp.stack(dmods)


def _my_index():
    return 4 * lax.axis_index("x") + 2 * lax.axis_index("y") + lax.axis_index("c")


def _peer(k):
    x, y, c = lax.axis_index("x"), lax.axis_index("y"), lax.axis_index("c")
    return (1 - x if k & 4 else x, 1 - y if k & 2 else y, 1 - c if k & 1 else c)


def _index_of(pos):
    return 4 * pos[0] + 2 * pos[1] + pos[2]


def _exchange_many(name, arrays, scatter):
    n = len(arrays)
    blocks = [tuple(a.shape[1:]) if scatter else tuple(a.shape) for a in arrays]

    def body(*refs):
        x_refs, o_refs = refs[:n], refs[n:2 * n]
        send_sems, recv_sems, local_sems = refs[2 * n:]
        me = _my_index()
        started = []
        for a in range(n):
            mine = pltpu.make_async_copy(x_refs[a].at[me] if scatter else x_refs[a], o_refs[a].at[me], local_sems.at[a])
            mine.start()
            started.append(mine)
        sends = []
        for k in range(1, N_DEV):
            peer = _peer(k)
            for a in range(n):
                cp = pltpu.make_async_remote_copy(
                    src_ref=x_refs[a].at[_index_of(peer)] if scatter else x_refs[a], dst_ref=o_refs[a].at[me],
                    send_sem=send_sems.at[a, k - 1], recv_sem=recv_sems.at[a, k - 1], device_id=peer, device_id_type=MESH)
                cp.start()
                sends.append(cp)
        for k in range(1, N_DEV):
            peer = _peer(k)
            for a in range(n):
                pltpu.make_async_remote_copy(
                    src_ref=x_refs[a].at[me] if scatter else x_refs[a], dst_ref=o_refs[a].at[_index_of(peer)],
                    send_sem=send_sems.at[a, k - 1], recv_sem=recv_sems.at[a, k - 1], device_id=peer, device_id_type=MESH).wait_recv()
        for cp in sends:
            cp.wait_send()
        for mine in started:
            mine.wait()

    return pl.pallas_call(
        body, name=name, out_shape=tuple(jax.ShapeDtypeStruct((N_DEV,) + blk, a.dtype) for blk, a in zip(blocks, arrays)),
        in_specs=[pl.BlockSpec(memory_space=pl.ANY)] * n, out_specs=tuple([pl.BlockSpec(memory_space=pl.ANY)] * n),
        scratch_shapes=[pltpu.SemaphoreType.DMA((n, N_DEV - 1)), pltpu.SemaphoreType.DMA((n, N_DEV - 1)), pltpu.SemaphoreType.DMA((n,))],
    )(*arrays)


def _exchange(name, x, scatter):
    return _exchange_many(name, [x], scatter)[0]


def _pack(arrays, dtype, row_multiple):
    flat = jnp.concatenate([a.astype(dtype).reshape(-1) for a in arrays])
    rows = -(-flat.shape[0] // (LANES * row_multiple)) * row_multiple
    return jnp.pad(flat, (0, rows * LANES - flat.shape[0])).reshape(rows, LANES)


def _pack8(arrays, dtype, row_multiple):
    flat = jnp.concatenate([a.astype(dtype).reshape(N_DEV, -1) for a in arrays], axis=1)
    rows = -(-flat.shape[1] // (LANES * row_multiple)) * row_multiple
    return jnp.pad(flat, ((0, 0), (0, rows * LANES - flat.shape[1]))).reshape(N_DEV, rows, LANES)


def _unpack(packed, shapes):
    lead = packed.shape[:-2]
    flat = packed.reshape(lead + (-1,))
    out, off = [], 0
    for shp in shapes:
        size = 1
        for d in shp:
            size *= d
        out.append(flat[..., off:off + size].reshape(lead + tuple(shp)))
        off += size
    return out


def _unshard(g8, axis):
    return jnp.concatenate([g8[j] for j in range(N_DEV)], axis=axis)


def _shard8(full, axis):
    n = full.shape[axis] // N_DEV
    return jnp.stack([lax.slice_in_dim(full, j * n, (j + 1) * n, axis=axis) for j in range(N_DEV)])


MATMUL_WEIGHTS = (("pool_w", 2), ("w_dkv", 0), ("w_uk", 1), ("w_uv", 1), ("w_dq", 1), ("w_uq", 2), ("w_o", 1), ("w_up", 2), ("w_down", 1))
VECTOR_WEIGHTS = (("pool_b", 1), ("pool_scale", 1), ("conv_w", 2))
REPLICATED_WEIGHTS = ("norm1_g", "norm2_g", "kv_in_g", "ckv_norm_g", "q_norm_g", "conv_b", "final_g")
WEIGHT_ORDER = ("mod_w", "mod_b", "norm1_g", "norm2_g", "pool_w", "pool_b", "pool_scale", "kv_in_g", "w_dkv", "ckv_norm_g", "w_uk",
                "w_uv", "w_dq", "q_norm_g", "w_uq", "w_o", "w_up", "conv_w", "conv_b", "w_down", "final_g")
BIG_ROW_MULTIPLE = 1024
SMALL_ROW_MULTIPLE = 16


def _as_2d(a):
    if a.ndim == 1:
        return a.reshape(-1, LANES)
    return a.reshape(-1, a.shape[-1])


def kernel(x, c, positions, mod_w, mod_b, norm1_g, norm2_g, pool_w, pool_b, pool_scale, kv_in_g, w_dkv, ckv_norm_g, w_uk, w_uv, w_dq, q_norm_g, w_uq, w_o, w_up, conv_w, conv_b, w_down, final_g, loss_target, m_mod_w, m_mod_b, m_norm1_g, m_norm2_g, m_pool_w, m_pool_b, m_pool_scale, m_kv_in_g, m_w_dkv, m_ckv_norm_g, m_w_uk, m_w_uv, m_w_dq, m_q_norm_g, m_w_uq, m_w_o, m_w_up, m_conv_w, m_conv_b, m_w_down, m_final_g, v_mod_w, v_mod_b, v_norm1_g, v_norm2_g, v_pool_w, v_pool_b, v_pool_scale, v_kv_in_g, v_w_dkv, v_ckv_norm_g, v_w_uk, v_w_uv, v_w_dq, v_q_norm_g, v_w_uq, v_w_o, v_w_up, v_conv_w, v_conv_b, v_w_down, v_final_g):
    shard = dict(mod_w=mod_w, mod_b=mod_b, norm1_g=norm1_g, norm2_g=norm2_g, pool_w=pool_w, pool_b=pool_b, pool_scale=pool_scale,
                 kv_in_g=kv_in_g, w_dkv=w_dkv, ckv_norm_g=ckv_norm_g, w_uk=w_uk, w_uv=w_uv, w_dq=w_dq, q_norm_g=q_norm_g, w_uq=w_uq,
                 w_o=w_o, w_up=w_up, conv_w=conv_w, conv_b=conv_b, w_down=w_down, final_g=final_g)
    mom_m = dict(mod_w=m_mod_w, mod_b=m_mod_b, norm1_g=m_norm1_g, norm2_g=m_norm2_g, pool_w=m_pool_w, pool_b=m_pool_b,
                 pool_scale=m_pool_scale, kv_in_g=m_kv_in_g, w_dkv=m_w_dkv, ckv_norm_g=m_ckv_norm_g, w_uk=m_w_uk, w_uv=m_w_uv,
                 w_dq=m_w_dq, q_norm_g=m_q_norm_g, w_uq=m_w_uq, w_o=m_w_o, w_up=m_w_up, conv_w=m_conv_w, conv_b=m_conv_b,
                 w_down=m_w_down, final_g=m_final_g)
    mom_v = dict(mod_w=v_mod_w, mod_b=v_mod_b, norm1_g=v_norm1_g, norm2_g=v_norm2_g, pool_w=v_pool_w, pool_b=v_pool_b,
                 pool_scale=v_pool_scale, kv_in_g=v_kv_in_g, w_dkv=v_w_dkv, ckv_norm_g=v_ckv_norm_g, w_uk=v_w_uk, w_uv=v_w_uv,
                 w_dq=v_w_dq, q_norm_g=v_q_norm_g, w_uq=v_w_uq, w_o=v_w_o, w_up=v_w_up, conv_w=v_conv_w, conv_b=v_conv_b,
                 w_down=v_w_down, final_g=v_final_g)
    me = _my_index()
    d6 = N_MOD * D_MODEL
    mod_cols = d6 // N_DEV

    small_in = [c] + [shard[k] for k, _ in VECTOR_WEIGHTS]
    small_all = _exchange("gather_vectors", _pack(small_in, F32, SMALL_ROW_MULTIPLE), scatter=False)
    parts = _unpack(small_all, [a.shape for a in small_in])
    c_all = jnp.pad(parts[0].reshape(N_DEV, D_MODEL), ((0, N_DEV), (0, 0)))
    w = {k: _unshard(p, ax + 0) for (k, ax), p in zip(VECTOR_WEIGHTS, parts[1:])}
    big_all = dict(zip([k for k, _ in MATMUL_WEIGHTS],
                       _exchange_many("gather_weights", [shard[k].astype(BF16) for k, _ in MATMUL_WEIGHTS], scatter=False)))
    half = N_DEV // 2
    for k, ax in MATMUL_WEIGHTS:
        if k == "w_up":
            w["w_up_a"] = jnp.concatenate([big_all[k][j] for j in range(half)], axis=-1)
            w["w_up_v"] = jnp.concatenate([big_all[k][j] for j in range(half, N_DEV)], axis=-1)
        elif k == "w_uq":
            rope = big_all[k][..., QK_NOPE:]
            ext = jnp.concatenate([big_all[k][..., :QK_NOPE], rope, _swap_halves(rope)], axis=-1)
            w["w_uq_ext"] = _unshard(ext, 2)
        else:
            w[k] = _unshard(big_all[k], ax)
    for k in REPLICATED_WEIGHTS:
        w[k] = shard[k]
    w["w_dkv_ext"] = _extend_w_dkv(w["w_dkv"])

    my_mod_b = lax.dynamic_slice_in_dim(mod_b, me * mod_cols, mod_cols, axis=1)
    mods_mine = _mods_fwd("mods_fwd", c_all, mod_w, my_mod_b)
    mods_all = _exchange("gather_mods", _pack([mods_mine], F32, SMALL_ROW_MULTIPLE), scatter=False)
    mods_all = _unpack(mods_all, [mods_mine.shape])[0]
    mods = lax.dynamic_index_in_dim(mods_all, me, axis=2, keepdims=False)
    mods = jnp.moveaxis(mods, 0, 1).reshape(DEPTH, d6)

    tabq, tabk = _rope_tables(positions[0])
    loss_row, dx, g, dmods = _forward_backward(x[0], loss_target[0], mods, tabq, tabk, w)
    g["w_dkv"] = _fold_w_dkv_grad(g["w_dkv_ext"])

    sent = []
    for k, ax in MATMUL_WEIGHTS:
        if k == "w_up":
            n = shard[k].shape[ax]
            cut = [lax.slice_in_dim(g[part], j * n, (j + 1) * n, axis=2) for part in ("w_up_a", "w_up_v") for j in range(half)]
            sent.append(jnp.stack(cut).astype(BF16))
        elif k == "w_uq":
            ext = _shard8(g["w_uq_ext"], 2)
            rope = ext[..., QK_NOPE:QK_HEAD] + _unswap_halves(ext[..., QK_HEAD:])
            sent.append(jnp.concatenate([ext[..., :QK_NOPE], rope], axis=-1).astype(BF16))
        else:
            sent.append(_shard8(g[k], ax).astype(BF16))
    got = _exchange_many("scatter_grads", sent, scatter=True)
    grads = {}
    for (k, _), p in zip(MATMUL_WEIGHTS, got):
        shp = shard[k].shape
        grads[k] = _sum8(f"sum_grads_{k}", p.reshape(N_DEV, -1, shp[-1])).reshape(shp)

    small_names = REPLICATED_WEIGHTS + tuple(k for k, _ in VECTOR_WEIGHTS)
    small_out = [dmods] + [g[k] for k in small_names] + [loss_row]
    small_shapes = [a.shape for a in small_out]
    small_got = _exchange("gather_small_grads", _pack(small_out, F32, SMALL_ROW_MULTIPLE), scatter=False)
    summed = _unpack(_sum8("sum_small_grads", small_got), small_shapes)
    grads["mod_b"] = summed[0]
    for k, s in zip(small_names, summed[1:-1]):
        grads[k] = s
    for k, ax in VECTOR_WEIGHTS:
        n = shard[k].shape[ax]
        grads[k] = lax.dynamic_slice_in_dim(grads[k], me * n, n, axis=ax)
    loss = summed[-1][0, 0]
    dmods_all = _unpack(small_got, small_shapes)[0]
    dm_mine = lax.dynamic_slice_in_dim(dmods_all, me * mod_cols, mod_cols, axis=2)
    dm_mine = jnp.pad(jnp.moveaxis(dm_mine, 0, 1), ((0, 0), (0, N_DEV), (0, 0)))
    grads["mod_w"] = _mods_bwd("mods_bwd", c_all, dm_mine)

    delta, new_m, new_v = {}, {}, {}
    for k in WEIGHT_ORDER:
        shp = shard[k].shape
        grads[k] = grads[k].reshape(shp)
        d_, m_, v_ = _adamw(f"adamw_{k}", _as_2d(shard[k]), _as_2d(grads[k]), _as_2d(mom_m[k]), _as_2d(mom_v[k]))
        delta[k], new_m[k], new_v[k] = d_.reshape(shp), m_.reshape(shp), v_.reshape(shp)
    return (loss, dx[None], *[grads[k] for k in WEIGHT_ORDER], *[delta[k] for k in WEIGHT_ORDER],
            *[new_m[k] for k in WEIGHT_ORDER], *[new_v[k] for k in WEIGHT_ORDER])
```

```python
import functools

import jax
import jax.numpy as jnp
from jax import lax
from jax.experimental import pallas as pl
from jax.experimental.pallas import tpu as pltpu

F32 = jnp.float32
BF16 = jnp.bfloat16

D_MODEL = 1024
DEPTH = 4
N_A_LAYERS = 2
N_B_LAYERS = 2
POOL_WINDOWS = (2, 4, 8, 16)
POOL_GROUP = 256
N_HEADS = 8
QK_NOPE = 128
QK_ROPE = 64
V_HEAD = 128
QK_HEAD = QK_NOPE + QK_ROPE
Q_RANK = 384
KV_RANK = 256
ROPE_THETA = 10000.0
D_FF = 2816
EPS = 1e-6
N_MOD = 6
ADAM_LR = 0.001
ADAM_B1 = 0.9
ADAM_B2 = 0.999
ADAM_EPS = 1e-08
ADAM_WD = 0.01
ADAM_STEP = 10

N_DEV = 8
LANES = 128
Q_EXT = 256
VMEM_LIMIT_BYTES = 48 * 1024 * 1024
MESH = pl.DeviceIdType.MESH
NEG_BIG = -0.7 * float(jnp.finfo(jnp.float32).max)


def _params(sem):
    return pltpu.CompilerParams(dimension_semantics=sem, vmem_limit_bytes=VMEM_LIMIT_BYTES)


def _tile(n, cap):
    if n <= cap:
        return n
    best = None
    for d in range(LANES, cap + 1, LANES):
        if n % d == 0:
            best = d
    assert best is not None, (n, cap)
    return best


def _dot(a, b, dims):
    return lax.dot_general(a, b, (dims, ((), ())), preferred_element_type=F32)


NN = ((1,), (0,))
NT = ((1,), (1,))
TN = ((0,), (0,))


def _mm(name, a, b, mode="nn", out_dtype=BF16, add=None, resid=None, gate=None, rowtab=None,
        tm_cap=1024, tn_cap=1408, tk_cap=1408):
    if mode == "tn":
        kdim, m = a.shape
    else:
        m, kdim = a.shape
    n = b.shape[0] if mode == "nt" else b.shape[1]
    tm, tn, tk = _tile(m, tm_cap), _tile(n, tn_cap), _tile(kdim, tk_cap)
    nk = kdim // tk
    dims = {"nn": NN, "nt": NT, "tn": TN}[mode]
    a_spec = pl.BlockSpec((tk, tm), lambda i, j, k: (k, i)) if mode == "tn" else pl.BlockSpec((tm, tk), lambda i, j, k: (i, k))
    b_spec = pl.BlockSpec((tn, tk), lambda i, j, k: (j, k)) if mode == "nt" else pl.BlockSpec((tk, tn), lambda i, j, k: (k, j))
    o_spec = pl.BlockSpec((tm, tn), lambda i, j, k: (i, j))
    g_spec = pl.BlockSpec((1, tn), lambda i, j, k: (0, j))
    gated = resid is not None

    def body(*refs):
        a_ref, b_ref = refs[0], refs[1]
        acc = refs[-1]
        k = pl.program_id(2)

        @pl.when(k == 0)
        def _():
            acc[...] = jnp.zeros_like(acc)

        acc[...] += _dot(a_ref[...].astype(BF16), b_ref[...].astype(BF16), dims)

        @pl.when(k == nk - 1)
        def _():
            if gated:
                r_ref, g_ref, y_ref, x_ref = refs[2:6]
                y_ref[...] = acc[...]
                x_ref[...] = r_ref[...] + g_ref[...] * acc[...]
            elif add is not None:
                refs[3][...] = (acc[...] + refs[2][...].astype(F32)).astype(out_dtype)
            elif rowtab is not None:
                tab = refs[2][...]
                refs[3][...] = (acc[...] * jnp.concatenate([tab] * (tn // tab.shape[1]), axis=1)).astype(out_dtype)
            else:
                refs[2][...] = acc[...].astype(out_dtype)

    ins, in_specs = [a, b], [a_spec, b_spec]
    if rowtab is not None:
        assert tn % rowtab.shape[1] == 0 and not gated and add is None
        ins.append(rowtab)
        in_specs.append(pl.BlockSpec((tm, rowtab.shape[1]), lambda i, j, k: (i, 0)))
    if gated:
        ins += [resid, gate]
        in_specs += [o_spec, g_spec]
        out_shape = (jax.ShapeDtypeStruct((m, n), F32), jax.ShapeDtypeStruct((m, n), F32))
        out_specs = (o_spec, o_spec)
    else:
        if add is not None:
            ins.append(add)
            in_specs.append(o_spec)
        out_shape = jax.ShapeDtypeStruct((m, n), out_dtype)
        out_specs = o_spec
    return pl.pallas_call(
        body, name=name, grid=(m // tm, n // tn, nk), in_specs=in_specs, out_specs=out_specs, out_shape=out_shape,
        scratch_shapes=[pltpu.VMEM((tm, tn), F32)],
        compiler_params=_params(("parallel", "parallel", "arbitrary")),
    )(*ins)


def _rowwise(name, fn, tiled, bcast, outs, sums=(), tr=512):
    tiled = [t if isinstance(t, tuple) else (t, t.shape[1], 0) for t in tiled]
    s = tiled[0][0].shape[0]
    tr = min(tr, s)
    assert s % tr == 0
    n_t, n_b, n_o = len(tiled), len(bcast), len(outs)

    def body(*refs):
        i = pl.program_id(0)
        vals = [r[...] for r in refs[:n_t + n_b]]
        o_vals, s_vals = fn(*vals)
        for r, v in zip(refs[n_t + n_b:n_t + n_b + n_o], o_vals):
            r[...] = v.astype(r.dtype)
        s_refs = refs[n_t + n_b + n_o:]

        @pl.when(i == 0)
        def _():
            for r in s_refs:
                r[...] = jnp.zeros_like(r)

        for r, v in zip(s_refs, s_vals):
            r[...] += v

    in_specs = [pl.BlockSpec((tr, n), functools.partial(lambda cb, i: (i, cb), cb)) for (_, n, cb) in tiled]
    in_specs += [pl.BlockSpec(b.shape, functools.partial(lambda nd, i: (0,) * nd, b.ndim)) for b in bcast]
    out_specs = [pl.BlockSpec((tr, n), lambda i: (i, 0)) for (n, _) in outs]
    out_specs += [pl.BlockSpec((1, n), lambda i: (0, 0)) for n in sums]
    out_shape = [jax.ShapeDtypeStruct((s, n), dt) for (n, dt) in outs]
    out_shape += [jax.ShapeDtypeStruct((1, n), F32) for n in sums]
    res = pl.pallas_call(
        body, name=name, grid=(s // tr,), in_specs=in_specs, out_specs=tuple(out_specs), out_shape=tuple(out_shape),
        compiler_params=_params(("arbitrary",)),
    )(*[t[0] for t in tiled], *bcast)
    return res


def _colsum(v):
    return jnp.sum(v, axis=0, keepdims=True)


def _rms_fwd(name, x, g, scale=None, shift=None, out_dtype=BF16, ncols=None):
    mod = scale is not None

    def fn(xv, gv, *ss):
        y = xv * lax.rsqrt(jnp.mean(xv * xv, axis=-1, keepdims=True) + EPS) * gv
        if mod:
            y = y * (1.0 + ss[0]) + ss[1]
        return (y,), ()

    n = ncols or x.shape[1]
    return _rowwise(name, fn, [(x, n, 0)], [g] + ([scale, shift] if mod else []), [(n, out_dtype)])[0]


def _rms_bwd(name, x, g, dh, scale=None, dx_in=None, ncols=None, out_dtype=F32):
    mod = scale is not None
    has_in = dx_in is not None

    def fn(*vals):
        xv, dhv = vals[0], vals[1].astype(F32)
        rest = list(vals[2:])
        dxi = rest.pop(0) if has_in else None
        gv = rest.pop(0)
        rstd = lax.rsqrt(jnp.mean(xv * xv, axis=-1, keepdims=True) + EPS)
        xhat = xv * rstd
        sums = []
        if mod:
            sc = rest.pop(0)
            dyn = dhv * (1.0 + sc)
            dshift, dscale = _colsum(dhv), _colsum(dhv * (xhat * gv))
        else:
            dyn = dhv
        dg = _colsum(dyn * xhat)
        dxhat = dyn * gv
        dx = rstd * (dxhat - xhat * jnp.mean(dxhat * xhat, axis=-1, keepdims=True))
        if has_in:
            dx = dx + dxi
        sums = [dg] + ([dshift, dscale] if mod else [])
        return (dx,), sums

    n = ncols or x.shape[1]
    tiled = [(x, n, 0), dh] + ([dx_in] if has_in else [])
    return _rowwise(name, fn, tiled, [g] + ([scale] if mod else []), [(n, out_dtype)], [n] * (3 if mod else 1))


def _gate_bwd(name, dxn, y, g):
    def fn(dv, yv, gv):
        return (gv * dv,), (_colsum(dv * yv),)

    n = dxn.shape[1]
    return _rowwise(name, fn, [dxn, y], [g], [(n, BF16)], [n])


def _loss_head(name, x, g, target):
    n = x.shape[1]

    def fn(xv, tv, gv):
        rstd = lax.rsqrt(jnp.mean(xv * xv, axis=-1, keepdims=True) + EPS)
        xhat = xv * rstd
        err = xhat * gv - tv
        loss = 0.5 * jnp.sum(jnp.sum(err * err, axis=-1, keepdims=True) / n, axis=0, keepdims=True)
        dy = err / n
        dg = _colsum(dy * xhat)
        dxhat = dy * gv
        dx = rstd * (dxhat - xhat * jnp.mean(dxhat * xhat, axis=-1, keepdims=True))
        return (dx,), (dg, jnp.broadcast_to(loss, (1, LANES)))

    return _rowwise(name, fn, [x, target], [g], [(n, F32)], [n, LANES])


def _krope_fwd(name, kv_ext, tabk):
    def fn(xv, tv):
        t = xv * tv
        return (t + pltpu.roll(t, 64, 1),), ()

    return _rowwise(name, fn, [(kv_ext, LANES, 2), tabk], [], [(LANES, BF16)])[0]


def _krope_bwd(name, dkd, tabk):
    def fn(dv, tv):
        return ((dv + pltpu.roll(dv, 64, 1)) * tv,), ()

    return _rowwise(name, fn, [dkd, tabk], [], [(LANES, F32)])[0]


def _adamw(name, w, g, m, v):
    def fn(wv, gv, mv, vv):
        m2 = ADAM_B1 * mv + (1.0 - ADAM_B1) * gv
        v2 = ADAM_B2 * vv + (1.0 - ADAM_B2) * (gv * gv)
        m_hat = m2 / (1.0 - ADAM_B1 ** ADAM_STEP)
        v_hat = v2 / (1.0 - ADAM_B2 ** ADAM_STEP)
        delta = -ADAM_LR * (m_hat / (jnp.sqrt(v_hat) + ADAM_EPS) + ADAM_WD * wv)
        return (delta, m2, v2), ()

    r, c = w.shape
    tr = r
    for cand in (512, 256, 128, 64, 32, 16, 8):
        if r % cand == 0 and r > cand:
            tr = cand
            break
    return _rowwise(name, fn, [w, g, m, v], [], [(c, F32)] * 3, tr=tr)


def _sum8(name, parts):
    _, r, c = parts.shape
    tr = r
    for cand in (2048, 1024, 512, 256, 128, 64, 32, 16):
        if r % cand == 0 and r > cand and cand * c <= 256 * 1024:
            tr = cand
            break

    def body(p_ref, o_ref):
        acc = p_ref[0].astype(F32)
        for k in range(1, N_DEV):
            acc = acc + p_ref[k].astype(F32)
        o_ref[...] = acc

    return pl.pallas_call(
        body, name=name, grid=(r // tr,), in_specs=[pl.BlockSpec((N_DEV, tr, c), lambda i: (0, i, 0))],
        out_specs=pl.BlockSpec((tr, c), lambda i: (i, 0)), out_shape=jax.ShapeDtypeStruct((r, c), F32),
        compiler_params=_params(("parallel",)),
    )(parts)


def _mods_fwd(name, c_all, w, b):
    depth, d, n = w.shape

    def body(c_ref, w_ref, b_ref, o_ref):
        cv = c_ref[...]
        sc = (cv * (1.0 / (1.0 + jnp.exp(-cv)))).astype(BF16)
        o_ref[0] = _dot(sc, w_ref[0].astype(BF16), NN) + b_ref[0]

    return pl.pallas_call(
        body, name=name, grid=(depth,),
        in_specs=[pl.BlockSpec(c_all.shape, lambda l: (0, 0)), pl.BlockSpec((1, d, n), lambda l: (l, 0, 0)),
                  pl.BlockSpec((1, 1, n), lambda l: (l, 0, 0))],
        out_specs=pl.BlockSpec((1, c_all.shape[0], n), lambda l: (l, 0, 0)),
        out_shape=jax.ShapeDtypeStruct((depth, c_all.shape[0], n), F32),
        compiler_params=_params(("parallel",)),
    )(c_all, w, b.reshape(depth, 1, n))


def _mods_bwd(name, c_all, dm):
    depth, rows, n = dm.shape
    d = c_all.shape[1]

    def body(c_ref, dm_ref, o_ref):
        cv = c_ref[...]
        sc = (cv * (1.0 / (1.0 + jnp.exp(-cv)))).astype(BF16)
        o_ref[0] = _dot(sc, dm_ref[0].astype(BF16), TN)

    return pl.pallas_call(
        body, name=name, grid=(depth,),
        in_specs=[pl.BlockSpec(c_all.shape, lambda l: (0, 0)), pl.BlockSpec((1, rows, n), lambda l: (l, 0, 0))],
        out_specs=pl.BlockSpec((1, d, n), lambda l: (l, 0, 0)),
        out_shape=jax.ShapeDtypeStruct((depth, d, n), F32),
        compiler_params=_params(("parallel",)),
    )(c_all, dm)


POOL_TILE = 256


def _split_dot(band, val):
    hi = val.astype(BF16)
    lo = (val - hi.astype(F32)).astype(BF16)
    return _dot(band, hi, NN) + _dot(band, lo, NN)


def _pool_fwd(name, h1, x, pw, pb, ps, g1):
    s, d = h1.shape
    t = POOL_TILE

    def body(hc_ref, hp_ref, x_ref, pw_ref, pb_ref, ps_ref, g_ref, xo_ref, zb_ref, pooled_ref):
        i = pl.program_id(0)
        r = lax.broadcasted_iota(jnp.int32, (t, t), 0)
        j = lax.broadcasted_iota(jnp.int32, (t, t), 1)
        pos = (i * t + lax.broadcasted_iota(jnp.int32, (t, 1), 0) + 1).astype(F32)
        has_prev = (i > 0).astype(F32)
        for grp, w in enumerate(POOL_WINDOWS):
            cs = slice(grp * POOL_GROUP, (grp + 1) * POOL_GROUP)
            hc = hc_ref[:, cs]
            band_cur = ((r - j >= 0) & (r - j < w)).astype(BF16)
            band_prev = (r + t - j < w).astype(BF16)
            ssum = _split_dot(band_cur, hc) + has_prev * _split_dot(band_prev, hp_ref[:, cs])
            pooled = (ssum / jnp.minimum(pos, float(w)) - hc).astype(BF16)
            zb = _dot(pooled, pw_ref[grp], NN) + pb_ref[:, cs]
            xo_ref[:, cs] = x_ref[:, cs] + g_ref[:, cs] * (zb * ps_ref[:, cs])
            zb_ref[:, cs] = zb
            pooled_ref[:, cs] = pooled

    row = pl.BlockSpec((t, d), lambda i: (i, 0))
    vec = pl.BlockSpec((1, d), lambda i: (0, 0))
    return pl.pallas_call(
        body, name=name, grid=(s // t,),
        in_specs=[row, pl.BlockSpec((t, d), lambda i: (jnp.maximum(i - 1, 0), 0)), row,
                  pl.BlockSpec(pw.shape, lambda i: (0, 0, 0)), vec, vec, vec],
        out_specs=(row, row, row),
        out_shape=(jax.ShapeDtypeStruct((s, d), F32), jax.ShapeDtypeStruct((s, d), F32), jax.ShapeDtypeStruct((s, d), BF16)),
        compiler_params=_params(("parallel",)),
    )(h1, h1, x, pw, pb, ps, g1)


def _pool_bwd(name, dxn, zb, pooled, pw, ps, g1):
    s, d = dxn.shape
    t = POOL_TILE
    nt = s // t

    def body(dc_ref, dn_ref, zb_ref, pooled_ref, pw_ref, ps_ref, g_ref, dh_ref, dpw_ref, dpb_ref, dps_ref, dg_ref):
        i = pl.program_id(0)

        @pl.when(i == 0)
        def _():
            dpw_ref[...] = jnp.zeros_like(dpw_ref)
            dpb_ref[...] = jnp.zeros_like(dpb_ref)
            dps_ref[...] = jnp.zeros_like(dps_ref)
            dg_ref[...] = jnp.zeros_like(dg_ref)

        jj = lax.broadcasted_iota(jnp.int32, (t, t), 0)
        rr = lax.broadcasted_iota(jnp.int32, (t, t), 1)
        pos = (i * t + lax.broadcasted_iota(jnp.int32, (t, 1), 0) + 1).astype(F32)
        has_next = (i < nt - 1).astype(F32)
        for grp, w in enumerate(POOL_WINDOWS):
            cs = slice(grp * POOL_GROUP, (grp + 1) * POOL_GROUP)
            gv, psv, zbv, dxc = g_ref[:, cs], ps_ref[:, cs], zb_ref[:, cs], dc_ref[:, cs]
            dg_ref[:, cs] += _colsum(dxc * (zbv * psv))
            dy = gv * dxc
            dps_ref[:, cs] += _colsum(dy * zbv)
            dz = dy * psv
            dpb_ref[:, cs] += _colsum(dz)
            dzb = dz.astype(BF16)
            dpw_ref[grp] += _dot(pooled_ref[:, cs], dzb, TN)
            dp = _dot(dzb, pw_ref[grp], NT)
            dzn = (gv * dn_ref[:, cs] * psv).astype(BF16)
            dpn = _dot(dzn, pw_ref[grp], NT) * (has_next / float(w))
            band_cur = ((rr - jj >= 0) & (rr - jj < w)).astype(BF16)
            band_next = (rr + t - jj < w).astype(BF16)
            dh_ref[:, cs] = _split_dot(band_cur, dp / jnp.minimum(pos, float(w))) + _split_dot(band_next, dpn) - dp

    row = pl.BlockSpec((t, d), lambda i: (i, 0))
    vec = pl.BlockSpec((1, d), lambda i: (0, 0))
    wspec = pl.BlockSpec(pw.shape, lambda i: (0, 0, 0))
    return pl.pallas_call(
        body, name=name, grid=(nt,),
        in_specs=[row, pl.BlockSpec((t, d), lambda i: (jnp.minimum(i + 1, nt - 1), 0)), row, row, wspec, vec, vec],
        out_specs=(row, wspec, vec, vec, vec),
        out_shape=(jax.ShapeDtypeStruct((s, d), F32), jax.ShapeDtypeStruct(pw.shape, F32),
                   jax.ShapeDtypeStruct((1, d), F32), jax.ShapeDtypeStruct((1, d), F32), jax.ShapeDtypeStruct((1, d), F32)),
        compiler_params=_params(("arbitrary",)),
    )(dxn, dxn, zb, pooled, pw, ps, g1)


GLU_TILE = 256
HALO = 16
INV_SQRT2 = 0.7071067811865476
INV_SQRT_2PI = 0.3989422804014327


def _gelu(xv):
    return 0.5 * xv * (1.0 + lax.erf(xv * INV_SQRT2))


def _gelu_grad(xv):
    return 0.5 * (1.0 + lax.erf(xv * INV_SQRT2)) + xv * (INV_SQRT_2PI * jnp.exp(-0.5 * xv * xv))


def _glu_fwd(name, ua, uv, cw, cb):
    s, f = ua.shape
    t, tf = GLU_TILE, _tile(f, 1408)

    def body(a_ref, ah_ref, v_ref, cw_ref, cb_ref, o_ref):
        i = pl.program_id(1)
        has_prev = (i > 0).astype(F32)
        ext = jnp.concatenate([ah_ref[...].astype(F32) * has_prev, a_ref[...].astype(F32)], axis=0)
        e1 = pltpu.roll(ext, 1, 0)[HALO:]
        e2 = pltpu.roll(ext, 2, 0)[HALO:]
        pre = e2 * cw_ref[0:1, :] + e1 * cw_ref[1:2, :] + ext[HALO:] * cw_ref[2:3, :] + cb_ref[...]
        o_ref[...] = (_gelu(pre) * v_ref[...].astype(F32)).astype(o_ref.dtype)

    blk = pl.BlockSpec((t, tf), lambda j, i: (i, j))
    halo = pl.BlockSpec((HALO, tf), lambda j, i: (jnp.maximum(i * (t // HALO) - 1, 0), j))
    return pl.pallas_call(
        body, name=name, grid=(f // tf, s // t),
        in_specs=[blk, halo, blk, pl.BlockSpec((3, tf), lambda j, i: (0, j)), pl.BlockSpec((1, tf), lambda j, i: (0, j))],
        out_specs=blk, out_shape=jax.ShapeDtypeStruct((s, f), BF16),
        compiler_params=_params(("parallel", "parallel")),
    )(ua, ua, uv, cw, cb)


def _glu_bwd(name, ua, uv, dgl, cw, cb):
    s, f = ua.shape
    t, tf = GLU_TILE, _tile(f, 1408)
    nt = s // t
    te = t + HALO

    def body(a_ref, ah_ref, an_ref, v_ref, vn_ref, d_ref, dn_ref, cw_ref, cb_ref, da_ref, dv_ref, dcw_ref, dcb_ref):
        i = pl.program_id(1)

        @pl.when(i == 0)
        def _():
            dcw_ref[...] = jnp.zeros_like(dcw_ref)
            dcb_ref[...] = jnp.zeros_like(dcb_ref)

        has_prev = (i > 0).astype(F32)
        has_next = (i < nt - 1).astype(F32)
        ext = jnp.concatenate([ah_ref[...].astype(F32) * has_prev, a_ref[...].astype(F32), an_ref[...].astype(F32)], axis=0)
        e0 = ext[HALO:]
        e1 = pltpu.roll(ext, 1, 0)[HALO:]
        e2 = pltpu.roll(ext, 2, 0)[HALO:]
        c0, c1, c2 = cw_ref[0:1, :], cw_ref[1:2, :], cw_ref[2:3, :]
        pre = e2 * c0 + e1 * c1 + e0 * c2 + cb_ref[...]
        vx = jnp.concatenate([v_ref[...].astype(F32), vn_ref[...].astype(F32)], axis=0)
        dx = jnp.concatenate([d_ref[...].astype(F32), dn_ref[...].astype(F32) * has_next], axis=0)
        dpre = dx * vx * _gelu_grad(pre)
        up1 = pltpu.roll(dpre, te - 1, 0)
        up2 = pltpu.roll(dpre, te - 2, 0)
        da_ref[...] = (dpre * c2 + up1 * c1 + up2 * c0)[:t].astype(da_ref.dtype)
        dv_ref[...] = (d_ref[...].astype(F32) * _gelu(pre[:t])).astype(dv_ref.dtype)
        dpt = dpre[:t]
        dcb_ref[...] += _colsum(dpt)
        dcw_ref[0:1, :] += _colsum(e2[:t] * dpt)
        dcw_ref[1:2, :] += _colsum(e1[:t] * dpt)
        dcw_ref[2:3, :] += _colsum(e0[:t] * dpt)

    blk = pl.BlockSpec((t, tf), lambda j, i: (i, j))
    prev = pl.BlockSpec((HALO, tf), lambda j, i: (jnp.maximum(i * (t // HALO) - 1, 0), j))
    nxt = pl.BlockSpec((HALO, tf), lambda j, i: (jnp.minimum((i + 1) * (t // HALO), s // HALO - 1), j))
    w3 = pl.BlockSpec((3, tf), lambda j, i: (0, j))
    w1 = pl.BlockSpec((1, tf), lambda j, i: (0, j))
    return pl.pallas_call(
        body, name=name, grid=(f // tf, nt),
        in_specs=[blk, prev, nxt, blk, nxt, blk, nxt, w3, w1],
        out_specs=(blk, blk, w3, w1),
        out_shape=(jax.ShapeDtypeStruct((s, f), BF16), jax.ShapeDtypeStruct((s, f), BF16),
                   jax.ShapeDtypeStruct((3, f), F32), jax.ShapeDtypeStruct((1, f), F32)),
        compiler_params=_params(("parallel", "arbitrary")),
    )(ua, ua, ua, uv, uv, dgl, dgl, cw, cb)


ATT_TILE = 512
LOG2E = 1.4426950408889634
LN2 = 0.6931471805599453


def _head_blocks_t(a, width):
    s = a.shape[0]
    t = min(ATT_TILE, s)
    return a.reshape(s // t, t, N_HEADS, width).transpose(2, 0, 3, 1)


def _head_rows(a):
    s = a.shape[0]
    t = min(ATT_TILE, s)
    r = a.reshape(s // t, t, N_HEADS, LANES)[..., 0].transpose(2, 0, 1)
    return jnp.broadcast_to(r[:, :, None, :], (N_HEADS, s // t, 8, t))


def _causal_mask(sv, q0, k0):
    row = q0 + lax.broadcasted_iota(jnp.int32, sv.shape, 0)
    col = k0 + lax.broadcasted_iota(jnp.int32, sv.shape, 1)
    return jnp.where(col <= row, sv, NEG_BIG)


def _attn_fwd(name, q_rot, kt4, v_ext):
    s = q_rot.shape[0]
    t = min(ATT_TILE, s)
    nq = s // t

    def body(q_ref, kt_ref, v_ref, o_ref, lse_ref, acc_ref, m_ref):
        qi = pl.program_id(1)
        q = q_ref[...]
        acc_ref[...] = jnp.zeros_like(acc_ref)
        m_ref[...] = jnp.full_like(m_ref, NEG_BIG)

        def step(j, masked):
            sv = _dot(q, kt_ref[0, j], NN)
            if masked:
                sv = _causal_mask(sv, qi * t, j * t)
            m_prev = m_ref[...]
            m_new = jnp.maximum(m_prev, jnp.max(sv, axis=-1, keepdims=True))
            p = jnp.exp2(sv - m_new).astype(BF16)
            acc_ref[...] = jnp.exp2(m_prev - m_new) * acc_ref[...] + _dot(p, v_ref[pl.ds(pl.multiple_of(j * t, t), t), :], NN)
            m_ref[...] = m_new

        def full_step(j, carry):
            step(j, False)
            return carry

        lax.fori_loop(0, qi, full_step, 0)
        step(qi, True)
        l = acc_ref[:, V_HEAD:V_HEAD + 1]
        o_ref[...] = (acc_ref[:, :V_HEAD] / l).astype(o_ref.dtype)
        lse_ref[...] = jnp.broadcast_to(m_ref[...] + jnp.log(l) * LOG2E, lse_ref.shape)

    head_q = pl.BlockSpec((t, Q_EXT), lambda h, i: (i, h))
    head_o = pl.BlockSpec((t, V_HEAD), lambda h, i: (i, h))
    return pl.pallas_call(
        body, name=name, grid=(N_HEADS, nq),
        in_specs=[head_q, pl.BlockSpec((1, nq, Q_EXT, t), lambda h, i: (h, 0, 0, 0)), pl.BlockSpec((s, Q_EXT), lambda h, i: (0, h))],
        out_specs=(head_o, head_o),
        out_shape=(jax.ShapeDtypeStruct((s, N_HEADS * V_HEAD), BF16), jax.ShapeDtypeStruct((s, N_HEADS * LANES), F32)),
        scratch_shapes=[pltpu.VMEM((t, Q_EXT), F32), pltpu.VMEM((t, 1), F32)],
        compiler_params=_params(("parallel", "parallel")),
    )(q_rot, kt4, v_ext)


def _attn_dq(name, q_rot, tabq, kt4, kfull, vt4, o, lse, do):
    s = q_rot.shape[0]
    t = min(ATT_TILE, s)
    nq = s // t

    def body(q_ref, tab_ref, kt_ref, k_ref, vt_ref, o_ref, lse_ref, do_ref, dq_ref, delta_ref, acc_ref):
        qi = pl.program_id(1)
        q = q_ref[...]
        dov = do_ref[...]
        delta = jnp.sum(dov.astype(F32) * o_ref[...].astype(F32), axis=-1, keepdims=True)
        lse = lse_ref[:, 0:1]
        acc_ref[...] = jnp.zeros_like(acc_ref)

        def step(j, masked):
            sv = _dot(q, kt_ref[0, j], NN)
            if masked:
                sv = _causal_mask(sv, qi * t, j * t)
            p = jnp.exp2(sv - lse)
            dp = _dot(dov, vt_ref[0, j], NN)
            ds = (p * (dp - delta)).astype(BF16)
            acc_ref[...] += _dot(ds, k_ref[pl.ds(pl.multiple_of(j * t, t), t), :], NN)

        def full_step(j, carry):
            step(j, False)
            return carry

        lax.fori_loop(0, qi, full_step, 0)
        step(qi, True)
        dq_ref[...] = (acc_ref[...] * (tab_ref[...] * LN2)).astype(dq_ref.dtype)
        delta_ref[...] = jnp.broadcast_to(delta, delta_ref.shape)

    head_q = pl.BlockSpec((t, Q_EXT), lambda h, i: (i, h))
    head_o = pl.BlockSpec((t, V_HEAD), lambda h, i: (i, h))
    return pl.pallas_call(
        body, name=name, grid=(N_HEADS, nq),
        in_specs=[head_q, pl.BlockSpec((t, Q_EXT), lambda h, i: (i, 0)), pl.BlockSpec((1, nq, Q_EXT, t), lambda h, i: (h, 0, 0, 0)),
                  pl.BlockSpec((s, Q_EXT), lambda h, i: (0, h)), pl.BlockSpec((1, nq, V_HEAD, t), lambda h, i: (h, 0, 0, 0)),
                  head_o, head_o, head_o],
        out_specs=(head_q, head_o),
        out_shape=(jax.ShapeDtypeStruct((s, N_HEADS * Q_EXT), BF16), jax.ShapeDtypeStruct((s, N_HEADS * LANES), F32)),
        scratch_shapes=[pltpu.VMEM((t, Q_EXT), F32)],
        compiler_params=_params(("parallel", "parallel")),
    )(q_rot, tabq, kt4, kfull, vt4, o, lse, do)


def _attn_dkv(name, kfull, v, qt4, q_rot, dot4, do, lse_row, delta_row, acc_in=None):
    s = kfull.shape[0]
    t = min(ATT_TILE, s)
    nq = s // t
    has_in = acc_in is not None

    def body(*refs):
        k_ref, v_ref, qt_ref, q_ref, dot_ref, do_ref, lse_ref, delta_ref = refs[:8]
        dkn_ref, dkd_ref, dv_ref, acck_ref, accv_ref = refs[-5:]
        kj, h = pl.program_id(0), pl.program_id(1)
        k_blk, v_blk = k_ref[...], v_ref[...]
        acck_ref[...] = jnp.zeros_like(acck_ref)
        accv_ref[...] = jnp.zeros_like(accv_ref)

        def step(i, masked):
            qs = pl.ds(pl.multiple_of(i * t, t), t)
            st = _dot(k_blk, qt_ref[0, i], NN)
            if masked:
                krow = lax.broadcasted_iota(jnp.int32, st.shape, 0)
                qcol = lax.broadcasted_iota(jnp.int32, st.shape, 1)
                st = jnp.where(krow <= qcol, st, NEG_BIG)
            pt = jnp.exp2(st - lse_ref[0, i, 0:1, :])
            accv_ref[...] += _dot(pt.astype(BF16), do_ref[qs, :], NN)
            dpt = _dot(v_blk, dot_ref[0, i], NN)
            dst = (pt * (dpt - delta_ref[0, i, 0:1, :])).astype(BF16)
            acck_ref[...] += _dot(dst, q_ref[qs, :], NN)

        def full_step(i, carry):
            step(i, False)
            return carry

        step(kj, True)
        lax.fori_loop(kj + 1, nq, full_step, 0)
        dk = acck_ref[...] * LN2
        dkn, dkd = dk[:, :QK_NOPE], dk[:, QK_NOPE:]
        if has_in:
            dkn = dkn + refs[8][...]
            dv_ref[...] = accv_ref[...] + refs[10][...]
        else:
            dv_ref[...] = accv_ref[...]
        dkn_ref[...] = dkn

        @pl.when(h == 0)
        def _():
            if has_in:
                dkd_ref[...] = dkd + refs[9][...]
            else:
                dkd_ref[...] = dkd

        @pl.when(h > 0)
        def _():
            dkd_ref[...] += dkd

    kblk = pl.BlockSpec((t, LANES), lambda j, h: (j, h))
    kdblk = pl.BlockSpec((t, LANES), lambda j, h: (j, 0))
    col = pl.BlockSpec((s, LANES), lambda j, h: (0, h))
    stat = pl.BlockSpec((1, nq, 8, t), lambda j, h: (h, 0, 0, 0))
    ins = [kfull, v, qt4, q_rot, dot4, do, lse_row, delta_row]
    in_specs = [pl.BlockSpec((t, Q_EXT), lambda j, h: (j, h)), kblk, pl.BlockSpec((1, nq, Q_EXT, t), lambda j, h: (h, 0, 0, 0)),
                pl.BlockSpec((s, Q_EXT), lambda j, h: (0, h)), pl.BlockSpec((1, nq, V_HEAD, t), lambda j, h: (h, 0, 0, 0)), col, stat, stat]
    if has_in:
        ins += list(acc_in)
        in_specs += [kblk, kdblk, kblk]
    return pl.pallas_call(
        body, name=name, grid=(nq, N_HEADS), in_specs=in_specs, out_specs=(kblk, kdblk, kblk),
        out_shape=(jax.ShapeDtypeStruct((s, N_HEADS * LANES), F32), jax.ShapeDtypeStruct((s, LANES), F32),
                   jax.ShapeDtypeStruct((s, N_HEADS * LANES), F32)),
        scratch_shapes=[pltpu.VMEM((t, Q_EXT), F32), pltpu.VMEM((t, LANES), F32)],
        compiler_params=_params(("parallel", "arbitrary")),
    )(*ins)


def _swap_halves(w):
    half = w.shape[-1] // 2
    return jnp.concatenate([-w[..., half:], w[..., :half]], axis=-1)


def _unswap_halves(g):
    half = g.shape[-1] // 2
    return jnp.concatenate([g[..., half:], -g[..., :half]], axis=-1)


def _extend_w_uq(w):
    r = w.reshape(Q_RANK, N_HEADS, QK_HEAD)
    rope = r[..., QK_NOPE:]
    return jnp.concatenate([r[..., :QK_NOPE], rope, _swap_halves(rope)], axis=-1).reshape(Q_RANK, N_HEADS * Q_EXT)


def _fold_w_uq_grad(g):
    r = g.reshape(Q_RANK, N_HEADS, Q_EXT)
    rope = r[..., QK_NOPE:QK_HEAD] + _unswap_halves(r[..., QK_HEAD:])
    return jnp.concatenate([r[..., :QK_NOPE], rope], axis=-1).reshape(Q_RANK, N_HEADS * QK_HEAD)


def _extend_w_dkv(w):
    return jnp.concatenate([w, _swap_halves(w[:, KV_RANK:])], axis=-1)


def _fold_w_dkv_grad(g):
    rope = g[:, KV_RANK:KV_RANK + QK_ROPE] + _unswap_halves(g[:, KV_RANK + QK_ROPE:])
    return jnp.concatenate([g[:, :KV_RANK], rope], axis=-1)


def _rope_tables(positions):
    inv = 1.0 / (ROPE_THETA ** (jnp.arange(0, QK_ROPE, 2, dtype=F32) / QK_ROPE))
    ang = positions.astype(F32)[:, None] * inv
    cos, sin = jnp.cos(ang), jnp.sin(ang)
    tabk = jnp.concatenate([cos, cos, sin, sin], axis=-1)
    scale = QK_HEAD ** -0.5 * LOG2E
    tabq = jnp.concatenate([jnp.full((positions.shape[0], QK_NOPE), scale, F32), tabk * scale], axis=-1)
    return tabq, tabk


def _forward_backward(x, target, mods, tabq, tabk, w):
    row = lambda vec: vec.reshape(1, -1)
    mod = [[row(mods[l, k * D_MODEL:(k + 1) * D_MODEL]) for k in range(N_MOD)] for l in range(DEPTH)]
    saved = []
    kv = None
    for l in range(DEPTH):
        sh1, sc1, g1, sh2, sc2, g2 = mod[l]
        x_in = x
        if l < N_A_LAYERS:
            h1 = _rms_fwd(f"norm1_fwd_{l}", x, row(w["norm1_g"][l]), sc1, sh1, out_dtype=F32)
            x_mid, zb, pooled = _pool_fwd(f"pool_fwd_{l}", h1, x, w["pool_w"][l], row(w["pool_b"][l]), row(w["pool_scale"][l]), g1)
            mix = (zb, pooled)
        else:
            j = l - N_A_LAYERS
            h1 = _rms_fwd(f"norm1_fwd_{l}", x, row(w["norm1_g"][l]), sc1, sh1)
            cq_pre = _mm(f"dq_fwd_{l}", h1, w["w_dq"][j], out_dtype=F32)
            cq = _rms_fwd(f"qnorm_fwd_{l}", cq_pre, row(w["q_norm_g"][j]))
            q_rot = _mm(f"uq_fwd_{l}", cq, w["w_uq_ext"][j], rowtab=tabq)
            o, lse = _attn_fwd(f"attn_fwd_{l}", q_rot, kv["kt4"], kv["v_ext"])
            y, x_mid = _mm(f"wo_fwd_{l}", o, w["w_o"][j], resid=x, gate=g1)
            mix = (h1, cq_pre, cq, q_rot, o, lse, y)
        h2 = _rms_fwd(f"norm2_fwd_{l}", x_mid, row(w["norm2_g"][l]), sc2, sh2)
        ua = _mm(f"up_a_fwd_{l}", h2, w["w_up_a"][l])
        uv = _mm(f"up_v_fwd_{l}", h2, w["w_up_v"][l])
        gl = _glu_fwd(f"glu_fwd_{l}", ua, uv, w["conv_w"][l], row(w["conv_b"][l]))
        y2, x = _mm(f"down_fwd_{l}", gl, w["w_down"][l], resid=x_mid, gate=g2)
        saved.append((x_in, x_mid, h2, ua, uv, gl, y2, mix))
        if l == N_A_LAYERS - 1:
            kvn = _rms_fwd("kvin_fwd", x, row(w["kv_in_g"]))
            kv_ext = _mm("dkv_fwd", kvn, w["w_dkv_ext"], out_dtype=F32)
            ckv = _rms_fwd("ckv_fwd", kv_ext, row(w["ckv_norm_g"]), ncols=KV_RANK)
            kd = _krope_fwd("krope_fwd", kv_ext, tabk)
            kn, v = _mm("uk_fwd", ckv, w["w_uk"]), _mm("uv_fwd", ckv, w["w_uv"])
            heads = lambda a: [a[:, h * LANES:(h + 1) * LANES] for h in range(N_HEADS)]
            kfull = jnp.concatenate([part for kh in heads(kn) for part in (kh, kd)], axis=-1)
            v_ext = jnp.concatenate([part for vh in heads(v) for part in (vh, jnp.ones_like(vh))], axis=-1)
            kv = dict(x=x, kvn=kvn, kv_ext=kv_ext, ckv=ckv, v=v, kfull=kfull, v_ext=v_ext,
                      kt4=_head_blocks_t(kfull, Q_EXT), vt4=_head_blocks_t(v, V_HEAD))

    dx, dfinal_g, loss = _loss_head("loss_head", x, row(w["final_g"]), target)
    g = {"final_g": dfinal_g.reshape(-1)}
    per_layer = {k: [None] * DEPTH for k in ("norm1_g", "norm2_g", "w_up_a", "w_up_v", "conv_w", "conv_b", "w_down")}
    per_a = {k: [None] * N_A_LAYERS for k in ("pool_w", "pool_b", "pool_scale")}
    per_b = {k: [None] * N_B_LAYERS for k in ("w_dq", "q_norm_g", "w_uq_ext", "w_o")}
    dmods = [None] * DEPTH
    dkv = None
    for l in reversed(range(DEPTH)):
        sh1, sc1, g1, sh2, sc2, g2 = mod[l]
        x_in, x_mid, h2, ua, uv, gl, y2, mix = saved[l]
        if l == N_A_LAYERS - 1:
            dkn, dkd, dv = dkv
            dckv = _mm("uk_bwd", dkn, w["w_uk"], mode="nt", out_dtype=F32)
            dckv = _mm("uv_bwd", dv, w["w_uv"], mode="nt", out_dtype=F32, add=dckv)
            g["w_uk"] = _mm("uk_wgrad", kv["ckv"], dkn, mode="tn", out_dtype=F32)
            g["w_uv"] = _mm("uv_wgrad", kv["ckv"], dv, mode="tn", out_dtype=F32)
            dkr = _krope_bwd("krope_bwd", dkd, tabk)
            dc, dckv_g = _rms_bwd("ckv_bwd", kv["kv_ext"], row(w["ckv_norm_g"]), dckv, ncols=KV_RANK, out_dtype=BF16)
            dkv_ext = jnp.concatenate([dc, dkr.astype(BF16)], axis=-1)
            dkvn = _mm("dkv_bwd", dkv_ext, w["w_dkv_ext"], mode="nt")
            g["w_dkv_ext"] = _mm("dkv_wgrad", kv["kvn"], dkv_ext, mode="tn", out_dtype=F32)
            dx, dkv_in_g = _rms_bwd("kvin_bwd", kv["x"], row(w["kv_in_g"]), dkvn, dx_in=dx)
            g["ckv_norm_g"], g["kv_in_g"] = dckv_g.reshape(-1), dkv_in_g.reshape(-1)
        dy2, dg2 = _gate_bwd(f"gate2_bwd_{l}", dx, y2, g2)
        dgl = _mm(f"down_bwd_{l}", dy2, w["w_down"][l], mode="nt")
        per_layer["w_down"][l] = _mm(f"down_wgrad_{l}", gl, dy2, mode="tn", out_dtype=F32)
        da, dv_, dcw, dcb = _glu_bwd(f"glu_bwd_{l}", ua, uv, dgl, w["conv_w"][l], row(w["conv_b"][l]))
        dh2 = _mm(f"up_a_bwd_{l}", da, w["w_up_a"][l], mode="nt", out_dtype=F32)
        dh2 = _mm(f"up_v_bwd_{l}", dv_, w["w_up_v"][l], mode="nt", out_dtype=F32, add=dh2)
        per_layer["w_up_a"][l] = _mm(f"up_a_wgrad_{l}", h2, da, mode="tn", out_dtype=F32)
        per_layer["w_up_v"][l] = _mm(f"up_v_wgrad_{l}", h2, dv_, mode="tn", out_dtype=F32)
        per_layer["conv_w"][l], per_layer["conv_b"][l] = dcw, dcb.reshape(-1)
        dx_mid, dn2, dsh2, dsc2 = _rms_bwd(f"norm2_bwd_{l}", x_mid, row(w["norm2_g"][l]), dh2, sc2, dx_in=dx)
        per_layer["norm2_g"][l] = dn2.reshape(-1)
        if l < N_A_LAYERS:
            zb, pooled = mix
            dh1, dpw, dpb, dps, dg1 = _pool_bwd(f"pool_bwd_{l}", dx_mid, zb, pooled, w["pool_w"][l], row(w["pool_scale"][l]), g1)
            per_a["pool_w"][l], per_a["pool_b"][l], per_a["pool_scale"][l] = dpw, dpb.reshape(-1), dps.reshape(-1)
        else:
            j = l - N_A_LAYERS
            h1, cq_pre, cq, q_rot, o, lse, y = mix
            dy, dg1 = _gate_bwd(f"gate1_bwd_{l}", dx_mid, y, g1)
            do = _mm(f"wo_bwd_{l}", dy, w["w_o"][j], mode="nt")
            per_b["w_o"][j] = _mm(f"wo_wgrad_{l}", o, dy, mode="tn", out_dtype=F32)
            dq_ext, delta = _attn_dq(f"attn_dq_{l}", q_rot, tabq, kv["kt4"], kv["kfull"], kv["vt4"], o, lse, do)
            dkv = _attn_dkv(f"attn_dkv_{l}", kv["kfull"], kv["v"], _head_blocks_t(q_rot, Q_EXT), q_rot, _head_blocks_t(do, V_HEAD), do,
                            _head_rows(lse), _head_rows(delta), acc_in=dkv)
            dcq = _mm(f"uq_bwd_{l}", dq_ext, w["w_uq_ext"][j], mode="nt", out_dtype=F32)
            per_b["w_uq_ext"][j] = _mm(f"uq_wgrad_{l}", cq, dq_ext, mode="tn", out_dtype=F32)
            dcq_pre, dqn = _rms_bwd(f"qnorm_bwd_{l}", cq_pre, row(w["q_norm_g"][j]), dcq, out_dtype=BF16)
            per_b["q_norm_g"][j] = dqn.reshape(-1)
            dh1 = _mm(f"dq_bwd_{l}", dcq_pre, w["w_dq"][j], mode="nt")
            per_b["w_dq"][j] = _mm(f"dq_wgrad_{l}", h1, dcq_pre, mode="tn", out_dtype=F32)
        dx, dn1, dsh1, dsc1 = _rms_bwd(f"norm1_bwd_{l}", x_in, row(w["norm1_g"][l]), dh1, sc1, dx_in=dx_mid)
        per_layer["norm1_g"][l] = dn1.reshape(-1)
        dmods[l] = jnp.concatenate([dsh1, dsc1, dg1, dsh2, dsc2, dg2], axis=-1).reshape(-1)
    for group in (per_layer, per_a, per_b):
        for k, vals in group.items():
            g[k] = jnp.stack(vals)
    return loss, dx, g, jnp.stack(dmods)


def _my_index():
    return 4 * lax.axis_index("x") + 2 * lax.axis_index("y") + lax.axis_index("c")


def _peer(k):
    x, y, c = lax.axis_index("x"), lax.axis_index("y"), lax.axis_index("c")
    return (1 - x if k & 4 else x, 1 - y if k & 2 else y, 1 - c if k & 1 else c)


def _index_of(pos):
    return 4 * pos[0] + 2 * pos[1] + pos[2]


def _exchange_many(name, arrays, scatter):
    n = len(arrays)
    blocks = [tuple(a.shape[1:]) if scatter else tuple(a.shape) for a in arrays]

    def body(*refs):
        x_refs, o_refs = refs[:n], refs[n:2 * n]
        send_sems, recv_sems, local_sems = refs[2 * n:]
        me = _my_index()
        started = []
        for a in range(n):
            mine = pltpu.make_async_copy(x_refs[a].at[me] if scatter else x_refs[a], o_refs[a].at[me], local_sems.at[a])
            mine.start()
            started.append(mine)
        sends = []
        for k in range(1, N_DEV):
            peer = _peer(k)
            for a in range(n):
                cp = pltpu.make_async_remote_copy(
                    src_ref=x_refs[a].at[_index_of(peer)] if scatter else x_refs[a], dst_ref=o_refs[a].at[me],
                    send_sem=send_sems.at[a, k - 1], recv_sem=recv_sems.at[a, k - 1], device_id=peer, device_id_type=MESH)
                cp.start()
                sends.append(cp)
        for k in range(1, N_DEV):
            peer = _peer(k)
            for a in range(n):
                pltpu.make_async_remote_copy(
                    src_ref=x_refs[a].at[me] if scatter else x_refs[a], dst_ref=o_refs[a].at[_index_of(peer)],
                    send_sem=send_sems.at[a, k - 1], recv_sem=recv_sems.at[a, k - 1], device_id=peer, device_id_type=MESH).wait_recv()
        for cp in sends:
            cp.wait_send()
        for mine in started:
            mine.wait()

    return pl.pallas_call(
        body, name=name, out_shape=tuple(jax.ShapeDtypeStruct((N_DEV,) + blk, a.dtype) for blk, a in zip(blocks, arrays)),
        in_specs=[pl.BlockSpec(memory_space=pl.ANY)] * n, out_specs=tuple([pl.BlockSpec(memory_space=pl.ANY)] * n),
        scratch_shapes=[pltpu.SemaphoreType.DMA((n, N_DEV - 1)), pltpu.SemaphoreType.DMA((n, N_DEV - 1)), pltpu.SemaphoreType.DMA((n,))],
    )(*arrays)


def _exchange(name, x, scatter):
    return _exchange_many(name, [x], scatter)[0]


def _pack(arrays, dtype, row_multiple):
    flat = jnp.concatenate([a.astype(dtype).reshape(-1) for a in arrays])
    rows = -(-flat.shape[0] // (LANES * row_multiple)) * row_multiple
    return jnp.pad(flat, (0, rows * LANES - flat.shape[0])).reshape(rows, LANES)


def _pack8(arrays, dtype, row_multiple):
    flat = jnp.concatenate([a.astype(dtype).reshape(N_DEV, -1) for a in arrays], axis=1)
    rows = -(-flat.shape[1] // (LANES * row_multiple)) * row_multiple
    return jnp.pad(flat, ((0, 0), (0, rows * LANES - flat.shape[1]))).reshape(N_DEV, rows, LANES)


def _unpack(packed, shapes):
    lead = packed.shape[:-2]
    flat = packed.reshape(lead + (-1,))
    out, off = [], 0
    for shp in shapes:
        size = 1
        for d in shp:
            size *= d
        out.append(flat[..., off:off + size].reshape(lead + tuple(shp)))
        off += size
    return out


def _unshard(g8, axis):
    return jnp.concatenate([g8[j] for j in range(N_DEV)], axis=axis)


def _shard8(full, axis):
    n = full.shape[axis] // N_DEV
    return jnp.stack([lax.slice_in_dim(full, j * n, (j + 1) * n, axis=axis) for j in range(N_DEV)])


MATMUL_WEIGHTS = (("pool_w", 2), ("w_dkv", 0), ("w_uk", 1), ("w_uv", 1), ("w_dq", 1), ("w_uq", 2), ("w_o", 1), ("w_up", 2), ("w_down", 1))
VECTOR_WEIGHTS = (("pool_b", 1), ("pool_scale", 1), ("conv_w", 2))
REPLICATED_WEIGHTS = ("norm1_g", "norm2_g", "kv_in_g", "ckv_norm_g", "q_norm_g", "conv_b", "final_g")
WEIGHT_ORDER = ("mod_w", "mod_b", "norm1_g", "norm2_g", "pool_w", "pool_b", "pool_scale", "kv_in_g", "w_dkv", "ckv_norm_g", "w_uk",
                "w_uv", "w_dq", "q_norm_g", "w_uq", "w_o", "w_up", "conv_w", "conv_b", "w_down", "final_g")
BIG_ROW_MULTIPLE = 1024
SMALL_ROW_MULTIPLE = 16


def _as_2d(a):
    if a.ndim == 1:
        return a.reshape(-1, LANES)
    return a.reshape(-1, a.shape[-1])


def kernel(x, c, positions, mod_w, mod_b, norm1_g, norm2_g, pool_w, pool_b, pool_scale, kv_in_g, w_dkv, ckv_norm_g, w_uk, w_uv, w_dq, q_norm_g, w_uq, w_o, w_up, conv_w, conv_b, w_down, final_g, loss_target, m_mod_w, m_mod_b, m_norm1_g, m_norm2_g, m_pool_w, m_pool_b, m_pool_scale, m_kv_in_g, m_w_dkv, m_ckv_norm_g, m_w_uk, m_w_uv, m_w_dq, m_q_norm_g, m_w_uq, m_w_o, m_w_up, m_conv_w, m_conv_b, m_w_down, m_final_g, v_mod_w, v_mod_b, v_norm1_g, v_norm2_g, v_pool_w, v_pool_b, v_pool_scale, v_kv_in_g, v_w_dkv, v_ckv_norm_g, v_w_uk, v_w_uv, v_w_dq, v_q_norm_g, v_w_uq, v_w_o, v_w_up, v_conv_w, v_conv_b, v_w_down, v_final_g):
    shard = dict(mod_w=mod_w, mod_b=mod_b, norm1_g=norm1_g, norm2_g=norm2_g, pool_w=pool_w, pool_b=pool_b, pool_scale=pool_scale,
                 kv_in_g=kv_in_g, w_dkv=w_dkv, ckv_norm_g=ckv_norm_g, w_uk=w_uk, w_uv=w_uv, w_dq=w_dq, q_norm_g=q_norm_g, w_uq=w_uq,
                 w_o=w_o, w_up=w_up, conv_w=conv_w, conv_b=conv_b, w_down=w_down, final_g=final_g)
    mom_m = dict(mod_w=m_mod_w, mod_b=m_mod_b, norm1_g=m_norm1_g, norm2_g=m_norm2_g, pool_w=m_pool_w, pool_b=m_pool_b,
                 pool_scale=m_pool_scale, kv_in_g=m_kv_in_g, w_dkv=m_w_dkv, ckv_norm_g=m_ckv_norm_g, w_uk=m_w_uk, w_uv=m_w_uv,
                 w_dq=m_w_dq, q_norm_g=m_q_norm_g, w_uq=m_w_uq, w_o=m_w_o, w_up=m_w_up, conv_w=m_conv_w, conv_b=m_conv_b,
                 w_down=m_w_down, final_g=m_final_g)
    mom_v = dict(mod_w=v_mod_w, mod_b=v_mod_b, norm1_g=v_norm1_g, norm2_g=v_norm2_g, pool_w=v_pool_w, pool_b=v_pool_b,
                 pool_scale=v_pool_scale, kv_in_g=v_kv_in_g, w_dkv=v_w_dkv, ckv_norm_g=v_ckv_norm_g, w_uk=v_w_uk, w_uv=v_w_uv,
                 w_dq=v_w_dq, q_norm_g=v_q_norm_g, w_uq=v_w_uq, w_o=v_w_o, w_up=v_w_up, conv_w=v_conv_w, conv_b=v_conv_b,
                 w_down=v_w_down, final_g=v_final_g)
    me = _my_index()
    d6 = N_MOD * D_MODEL
    mod_cols = d6 // N_DEV

    small_in = [c] + [shard[k] for k, _ in VECTOR_WEIGHTS]
    small_all = _exchange("gather_vectors", _pack(small_in, F32, SMALL_ROW_MULTIPLE), scatter=False)
    parts = _unpack(small_all, [a.shape for a in small_in])
    c_all = jnp.pad(parts[0].reshape(N_DEV, D_MODEL), ((0, N_DEV), (0, 0)))
    w = {k: _unshard(p, ax + 0) for (k, ax), p in zip(VECTOR_WEIGHTS, parts[1:])}
    big_all = dict(zip([k for k, _ in MATMUL_WEIGHTS],
                       _exchange_many("gather_weights", [shard[k].astype(BF16) for k, _ in MATMUL_WEIGHTS], scatter=False)))
    half = N_DEV // 2
    for k, ax in MATMUL_WEIGHTS:
        if k == "w_up":
            w["w_up_a"] = jnp.concatenate([big_all[k][j] for j in range(half)], axis=-1)
            w["w_up_v"] = jnp.concatenate([big_all[k][j] for j in range(half, N_DEV)], axis=-1)
        elif k == "w_uq":
            rope = big_all[k][..., QK_NOPE:]
            ext = jnp.concatenate([big_all[k][..., :QK_NOPE], rope, _swap_halves(rope)], axis=-1)
            w["w_uq_ext"] = _unshard(ext, 2)
        else:
            w[k] = _unshard(big_all[k], ax)
    for k in REPLICATED_WEIGHTS:
        w[k] = shard[k]
    w["w_dkv_ext"] = _extend_w_dkv(w["w_dkv"])

    my_mod_b = lax.dynamic_slice_in_dim(mod_b, me * mod_cols, mod_cols, axis=1)
    mods_mine = _mods_fwd("mods_fwd", c_all, mod_w, my_mod_b)
    mods_all = _exchange("gather_mods", _pack([mods_mine], F32, SMALL_ROW_MULTIPLE), scatter=False)
    mods_all = _unpack(mods_all, [mods_mine.shape])[0]
    mods = lax.dynamic_index_in_dim(mods_all, me, axis=2, keepdims=False)
    mods = jnp.moveaxis(mods, 0, 1).reshape(DEPTH, d6)

    tabq, tabk = _rope_tables(positions[0])
    loss_row, dx, g, dmods = _forward_backward(x[0], loss_target[0], mods, tabq, tabk, w)
    g["w_dkv"] = _fold_w_dkv_grad(g["w_dkv_ext"])

    sent = []
    for k, ax in MATMUL_WEIGHTS:
        if k == "w_up":
            n = shard[k].shape[ax]
            cut = [lax.slice_in_dim(g[part], j * n, (j + 1) * n, axis=2) for part in ("w_up_a", "w_up_v") for j in range(half)]
            sent.append(jnp.stack(cut).astype(BF16))
        elif k == "w_uq":
            ext = _shard8(g["w_uq_ext"], 2)
            rope = ext[..., QK_NOPE:QK_HEAD] + _unswap_halves(ext[..., QK_HEAD:])
            sent.append(jnp.concatenate([ext[..., :QK_NOPE], rope], axis=-1).astype(BF16))
        else:
            sent.append(_shard8(g[k], ax).astype(BF16))
    got = _exchange_many("scatter_grads", sent, scatter=True)
    grads = {}
    for (k, _), p in zip(MATMUL_WEIGHTS, got):
        shp = shard[k].shape
        grads[k] = _sum8(f"sum_grads_{k}", p.reshape(N_DEV, -1, shp[-1])).reshape(shp)

    small_names = REPLICATED_WEIGHTS + tuple(k for k, _ in VECTOR_WEIGHTS)
    small_out = [dmods] + [g[k] for k in small_names] + [loss_row]
    small_shapes = [a.shape for a in small_out]
    small_got = _exchange("gather_small_grads", _pack(small_out, F32, SMALL_ROW_MULTIPLE), scatter=False)
    summed = _unpack(_sum8("sum_small_grads", small_got), small_shapes)
    grads["mod_b"] = summed[0]
    for k, s in zip(small_names, summed[1:-1]):
        grads[k] = s
    for k, ax in VECTOR_WEIGHTS:
        n = shard[k].shape[ax]
        grads[k] = lax.dynamic_slice_in_dim(grads[k], me * n, n, axis=ax)
    loss = summed[-1][0, 0]
    dmods_all = _unpack(small_got, small_shapes)[0]
    dm_mine = lax.dynamic_slice_in_dim(dmods_all, me * mod_cols, mod_cols, axis=2)
    dm_mine = jnp.pad(jnp.moveaxis(dm_mine, 0, 1), ((0, 0), (0, N_DEV), (0, 0)))
    grads["mod_w"] = _mods_bwd("mods_bwd", c_all, dm_mine)

    delta, new_m, new_v = {}, {}, {}
    for k in WEIGHT_ORDER:
        shp = shard[k].shape
        grads[k] = grads[k].reshape(shp)
        d_, m_, v_ = _adamw(f"adamw_{k}", _as_2d(shard[k]), _as_2d(grads[k]), _as_2d(mom_m[k]), _as_2d(mom_v[k]))
        delta[k], new_m[k], new_v[k] = d_.reshape(shp), m_.reshape(shp), v_.reshape(shp)
    return (loss, dx[None], *[grads[k] for k in WEIGHT_ORDER], *[delta[k] for k in WEIGHT_ORDER],
            *[new_m[k] for k in WEIGHT_ORDER], *[new_v[k] for k in WEIGHT_ORDER])
```

```python
import functools

import jax
import jax.numpy as jnp
from jax import lax
from jax.experimental import pallas as pl
from jax.experimental.pallas import tpu as pltpu

F32 = jnp.float32
BF16 = jnp.bfloat16

D_MODEL = 1024
DEPTH = 4
N_A_LAYERS = 2
N_B_LAYERS = 2
POOL_WINDOWS = (2, 4, 8, 16)
POOL_GROUP = 256
N_HEADS = 8
QK_NOPE = 128
QK_ROPE = 64
V_HEAD = 128
QK_HEAD = QK_NOPE + QK_ROPE
Q_RANK = 384
KV_RANK = 256
ROPE_THETA = 10000.0
D_FF = 2816
EPS = 1e-6
N_MOD = 6
ADAM_LR = 0.001
ADAM_B1 = 0.9
ADAM_B2 = 0.999
ADAM_EPS = 1e-08
ADAM_WD = 0.01
ADAM_STEP = 10

N_DEV = 8
LANES = 128
Q_EXT = 256
VMEM_LIMIT_BYTES = 48 * 1024 * 1024
MESH = pl.DeviceIdType.MESH
NEG_BIG = -0.7 * float(jnp.finfo(jnp.float32).max)


def _params(sem):
    return pltpu.CompilerParams(dimension_semantics=sem, vmem_limit_bytes=VMEM_LIMIT_BYTES)


def _tile(n, cap):
    if n <= cap:
        return n
    best = None
    for d in range(LANES, cap + 1, LANES):
        if n % d == 0:
            best = d
    assert best is not None, (n, cap)
    return best


def _dot(a, b, dims):
    return lax.dot_general(a, b, (dims, ((), ())), preferred_element_type=F32)


NN = ((1,), (0,))
NT = ((1,), (1,))
TN = ((0,), (0,))


def _mm(name, a, b, mode="nn", out_dtype=BF16, add=None, resid=None, gate=None, rowtab=None,
        tm_cap=1024, tn_cap=1408, tk_cap=1408):
    if mode == "tn":
        kdim, m = a.shape
    else:
        m, kdim = a.shape
    n = b.shape[0] if mode == "nt" else b.shape[1]
    tm, tn, tk = _tile(m, tm_cap), _tile(n, tn_cap), _tile(kdim, tk_cap)
    nk = kdim // tk
    dims = {"nn": NN, "nt": NT, "tn": TN}[mode]
    a_spec = pl.BlockSpec((tk, tm), lambda i, j, k: (k, i)) if mode == "tn" else pl.BlockSpec((tm, tk), lambda i, j, k: (i, k))
    b_spec = pl.BlockSpec((tn, tk), lambda i, j, k: (j, k)) if mode == "nt" else pl.BlockSpec((tk, tn), lambda i, j, k: (k, j))
    o_spec = pl.BlockSpec((tm, tn), lambda i, j, k: (i, j))
    g_spec = pl.BlockSpec((1, tn), lambda i, j, k: (0, j))
    gated = resid is not None

    def body(*refs):
        a_ref, b_ref = refs[0], refs[1]
        acc = refs[-1]
        k = pl.program_id(2)

        @pl.when(k == 0)
        def _():
            acc[...] = jnp.zeros_like(acc)

        acc[...] += _dot(a_ref[...].astype(BF16), b_ref[...].astype(BF16), dims)

        @pl.when(k == nk - 1)
        def _():
            if gated:
                r_ref, g_ref, y_ref, x_ref = refs[2:6]
                y_ref[...] = acc[...]
                x_ref[...] = r_ref[...] + g_ref[...] * acc[...]
            elif add is not None:
                refs[3][...] = (acc[...] + refs[2][...].astype(F32)).astype(out_dtype)
            elif rowtab is not None:
                tab = refs[2][...]
                refs[3][...] = (acc[...] * jnp.concatenate([tab] * (tn // tab.shape[1]), axis=1)).astype(out_dtype)
            else:
                refs[2][...] = acc[...].astype(out_dtype)

    ins, in_specs = [a, b], [a_spec, b_spec]
    if rowtab is not None:
        assert tn % rowtab.shape[1] == 0 and not gated and add is None
        ins.append(rowtab)
        in_specs.append(pl.BlockSpec((tm, rowtab.shape[1]), lambda i, j, k: (i, 0)))
    if gated:
        ins += [resid, gate]
        in_specs += [o_spec, g_spec]
        out_shape = (jax.ShapeDtypeStruct((m, n), F32), jax.ShapeDtypeStruct((m, n), F32))
        out_specs = (o_spec, o_spec)
    else:
        if add is not None:
            ins.append(add)
            in_specs.append(o_spec)
        out_shape = jax.ShapeDtypeStruct((m, n), out_dtype)
        out_specs = o_spec
    return pl.pallas_call(
        body, name=name, grid=(m // tm, n // tn, nk), in_specs=in_specs, out_specs=out_specs, out_shape=out_shape,
        scratch_shapes=[pltpu.VMEM((tm, tn), F32)],
        compiler_params=_params(("parallel", "parallel", "arbitrary")),
    )(*ins)


def _rowwise(name, fn, tiled, bcast, outs, sums=(), tr=512):
    tiled = [t if isinstance(t, tuple) else (t, t.shape[1], 0) for t in tiled]
    s = tiled[0][0].shape[0]
    tr = min(tr, s)
    assert s % tr == 0
    n_t, n_b, n_o = len(tiled), len(bcast), len(outs)

    def body(*refs):
        i = pl.program_id(0)
        vals = [r[...] for r in refs[:n_t + n_b]]
        o_vals, s_vals = fn(*vals)
        for r, v in zip(refs[n_t + n_b:n_t + n_b + n_o], o_vals):
            r[...] = v.astype(r.dtype)
        s_refs = refs[n_t + n_b + n_o:]

        @pl.when(i == 0)
        def _():
            for r in s_refs:
                r[...] = jnp.zeros_like(r)

        for r, v in zip(s_refs, s_vals):
            r[...] += v

    in_specs = [pl.BlockSpec((tr, n), functools.partial(lambda cb, i: (i, cb), cb)) for (_, n, cb) in tiled]
    in_specs += [pl.BlockSpec(b.shape, functools.partial(lambda nd, i: (0,) * nd, b.ndim)) for b in bcast]
    out_specs = [pl.BlockSpec((tr, n), lambda i: (i, 0)) for (n, _) in outs]
    out_specs += [pl.BlockSpec((1, n), lambda i: (0, 0)) for n in sums]
    out_shape = [jax.ShapeDtypeStruct((s, n), dt) for (n, dt) in outs]
    out_shape += [jax.ShapeDtypeStruct((1, n), F32) for n in sums]
    res = pl.pallas_call(
        body, name=name, grid=(s // tr,), in_specs=in_specs, out_specs=tuple(out_specs), out_shape=tuple(out_shape),
        compiler_params=_params(("arbitrary",)),
    )(*[t[0] for t in tiled], *bcast)
    return res


def _colsum(v):
    return jnp.sum(v, axis=0, keepdims=True)


def _rms_fwd(name, x, g, scale=None, shift=None, out_dtype=BF16, ncols=None):
    mod = scale is not None

    def fn(xv, gv, *ss):
        y = xv * lax.rsqrt(jnp.mean(xv * xv, axis=-1, keepdims=True) + EPS) * gv
        if mod:
            y = y * (1.0 + ss[0]) + ss[1]
        return (y,), ()

    n = ncols or x.shape[1]
    return _rowwise(name, fn, [(x, n, 0)], [g] + ([scale, shift] if mod else []), [(n, out_dtype)])[0]


def _rms_bwd(name, x, g, dh, scale=None, dx_in=None, ncols=None, out_dtype=F32):
    mod = scale is not None
    has_in = dx_in is not None

    def fn(*vals):
        xv, dhv = vals[0], vals[1].astype(F32)
        rest = list(vals[2:])
        dxi = rest.pop(0) if has_in else None
        gv = rest.pop(0)
        rstd = lax.rsqrt(jnp.mean(xv * xv, axis=-1, keepdims=True) + EPS)
        xhat = xv * rstd
        sums = []
        if mod:
            sc = rest.pop(0)
            dyn = dhv * (1.0 + sc)
            dshift, dscale = _colsum(dhv), _colsum(dhv * (xhat * gv))
        else:
            dyn = dhv
        dg = _colsum(dyn * xhat)
        dxhat = dyn * gv
        dx = rstd * (dxhat - xhat * jnp.mean(dxhat * xhat, axis=-1, keepdims=True))
        if has_in:
            dx = dx + dxi
        sums = [dg] + ([dshift, dscale] if mod else [])
        return (dx,), sums

    n = ncols or x.shape[1]
    tiled = [(x, n, 0), dh] + ([dx_in] if has_in else [])
    return _rowwise(name, fn, tiled, [g] + ([scale] if mod else []), [(n, out_dtype)], [n] * (3 if mod else 1))


def _gate_bwd(name, dxn, y, g):
    def fn(dv, yv, gv):
        return (gv * dv,), (_colsum(dv * yv),)

    n = dxn.shape[1]
    return _rowwise(name, fn, [dxn, y], [g], [(n, BF16)], [n])


def _loss_head(name, x, g, target):
    n = x.shape[1]

    def fn(xv, tv, gv):
        rstd = lax.rsqrt(jnp.mean(xv * xv, axis=-1, keepdims=True) + EPS)
        xhat = xv * rstd
        err = xhat * gv - tv
        loss = 0.5 * jnp.sum(jnp.sum(err * err, axis=-1, keepdims=True) / n, axis=0, keepdims=True)
        dy = err / n
        dg = _colsum(dy * xhat)
        dxhat = dy * gv
        dx = rstd * (dxhat - xhat * jnp.mean(dxhat * xhat, axis=-1, keepdims=True))
        return (dx,), (dg, jnp.broadcast_to(loss, (1, LANES)))

    return _rowwise(name, fn, [x, target], [g], [(n, F32)], [n, LANES])


def _krope_fwd(name, kv_ext, tabk):
    def fn(xv, tv):
        t = xv * tv
        return (t + pltpu.roll(t, 64, 1),), ()

    return _rowwise(name, fn, [(kv_ext, LANES, 2), tabk], [], [(LANES, BF16)])[0]


def _krope_bwd(name, dkd, tabk):
    def fn(dv, tv):
        return ((dv + pltpu.roll(dv, 64, 1)) * tv,), ()

    return _rowwise(name, fn, [dkd, tabk], [], [(LANES, F32)])[0]


def _adamw(name, w, g, m, v):
    def fn(wv, gv, mv, vv):
        m2 = ADAM_B1 * mv + (1.0 - ADAM_B1) * gv
        v2 = ADAM_B2 * vv + (1.0 - ADAM_B2) * (gv * gv)
        m_hat = m2 / (1.0 - ADAM_B1 ** ADAM_STEP)
        v_hat = v2 / (1.0 - ADAM_B2 ** ADAM_STEP)
        delta = -ADAM_LR * (m_hat / (jnp.sqrt(v_hat) + ADAM_EPS) + ADAM_WD * wv)
        return (delta, m2, v2), ()

    r, c = w.shape
    tr = r
    for cand in (512, 256, 128, 64, 32, 16, 8):
        if r % cand == 0 and r > cand:
            tr = cand
            break
    return _rowwise(name, fn, [w, g, m, v], [], [(c, F32)] * 3, tr=tr)


def _sum8(name, parts):
    _, r, c = parts.shape
    tr = r
    for cand in (2048, 1024, 512, 256, 128, 64, 32, 16):
        if r % cand == 0 and r > cand and cand * c <= 256 * 1024:
            tr = cand
            break

    def body(p_ref, o_ref):
        acc = p_ref[0].astype(F32)
        for k in range(1, N_DEV):
            acc = acc + p_ref[k].astype(F32)
        o_ref[...] = acc

    return pl.pallas_call(
        body, name=name, grid=(r // tr,), in_specs=[pl.BlockSpec((N_DEV, tr, c), lambda i: (0, i, 0))],
        out_specs=pl.BlockSpec((tr, c), lambda i: (i, 0)), out_shape=jax.ShapeDtypeStruct((r, c), F32),
        compiler_params=_params(("parallel",)),
    )(parts)


def _mods_fwd(name, c_all, w, b):
    depth, d, n = w.shape

    def body(c_ref, w_ref, b_ref, o_ref):
        cv = c_ref[...]
        sc = (cv * (1.0 / (1.0 + jnp.exp(-cv)))).astype(BF16)
        o_ref[0] = _dot(sc, w_ref[0].astype(BF16), NN) + b_ref[0]

    return pl.pallas_call(
        body, name=name, grid=(depth,),
        in_specs=[pl.BlockSpec(c_all.shape, lambda l: (0, 0)), pl.BlockSpec((1, d, n), lambda l: (l, 0, 0)),
                  pl.BlockSpec((1, 1, n), lambda l: (l, 0, 0))],
        out_specs=pl.BlockSpec((1, c_all.shape[0], n), lambda l: (l, 0, 0)),
        out_shape=jax.ShapeDtypeStruct((depth, c_all.shape[0], n), F32),
        compiler_params=_params(("parallel",)),
    )(c_all, w, b.reshape(depth, 1, n))


def _mods_bwd(name, c_all, dm):
    depth, rows, n = dm.shape
    d = c_all.shape[1]

    def body(c_ref, dm_ref, o_ref):
        cv = c_ref[...]
        sc = (cv * (1.0 / (1.0 + jnp.exp(-cv)))).astype(BF16)
        o_ref[0] = _dot(sc, dm_ref[0].astype(BF16), TN)

    return pl.pallas_call(
        body, name=name, grid=(depth,),
        in_specs=[pl.BlockSpec(c_all.shape, lambda l: (0, 0)), pl.BlockSpec((1, rows, n), lambda l: (l, 0, 0))],
        out_specs=pl.BlockSpec((1, d, n), lambda l: (l, 0, 0)),
        out_shape=jax.ShapeDtypeStruct((depth, d, n), F32),
        compiler_params=_params(("parallel",)),
    )(c_all, dm)


POOL_TILE = 256


def _split_dot(band, val):
    hi = val.astype(BF16)
    lo = (val - hi.astype(F32)).astype(BF16)
    return _dot(band, hi, NN) + _dot(band, lo, NN)


def _pool_fwd(name, h1, x, pw, pb, ps, g1):
    s, d = h1.shape
    t = POOL_TILE

    def body(hc_ref, hp_ref, x_ref, pw_ref, pb_ref, ps_ref, g_ref, xo_ref, zb_ref, pooled_ref):
        i = pl.program_id(0)
        r = lax.broadcasted_iota(jnp.int32, (t, t), 0)
        j = lax.broadcasted_iota(jnp.int32, (t, t), 1)
        pos = (i * t + lax.broadcasted_iota(jnp.int32, (t, 1), 0) + 1).astype(F32)
        has_prev = (i > 0).astype(F32)
        for grp, w in enumerate(POOL_WINDOWS):
            cs = slice(grp * POOL_GROUP, (grp + 1) * POOL_GROUP)
            hc = hc_ref[:, cs]
            band_cur = ((r - j >= 0) & (r - j < w)).astype(BF16)
            band_prev = (r + t - j < w).astype(BF16)
            ssum = _split_dot(band_cur, hc) + has_prev * _split_dot(band_prev, hp_ref[:, cs])
            pooled = (ssum / jnp.minimum(pos, float(w)) - hc).astype(BF16)
            zb = _dot(pooled, pw_ref[grp], NN) + pb_ref[:, cs]
            xo_ref[:, cs] = x_ref[:, cs] + g_ref[:, cs] * (zb * ps_ref[:, cs])
            zb_ref[:, cs] = zb
            pooled_ref[:, cs] = pooled

    row = pl.BlockSpec((t, d), lambda i: (i, 0))
    vec = pl.BlockSpec((1, d), lambda i: (0, 0))
    return pl.pallas_call(
        body, name=name, grid=(s // t,),
        in_specs=[row, pl.BlockSpec((t, d), lambda i: (jnp.maximum(i - 1, 0), 0)), row,
                  pl.BlockSpec(pw.shape, lambda i: (0, 0, 0)), vec, vec, vec],
        out_specs=(row, row, row),
        out_shape=(jax.ShapeDtypeStruct((s, d), F32), jax.ShapeDtypeStruct((s, d), F32), jax.ShapeDtypeStruct((s, d), BF16)),
        compiler_params=_params(("parallel",)),
    )(h1, h1, x, pw, pb, ps, g1)


def _pool_bwd(name, dxn, zb, pooled, pw, ps, g1):
    s, d = dxn.shape
    t = POOL_TILE
    nt = s // t

    def body(dc_ref, dn_ref, zb_ref, pooled_ref, pw_ref, ps_ref, g_ref, dh_ref, dpw_ref, dpb_ref, dps_ref, dg_ref):
        i = pl.program_id(0)

        @pl.when(i == 0)
        def _():
            dpw_ref[...] = jnp.zeros_like(dpw_ref)
            dpb_ref[...] = jnp.zeros_like(dpb_ref)
            dps_ref[...] = jnp.zeros_like(dps_ref)
            dg_ref[...] = jnp.zeros_like(dg_ref)

        jj = lax.broadcasted_iota(jnp.int32, (t, t), 0)
        rr = lax.broadcasted_iota(jnp.int32, (t, t), 1)
        pos = (i * t + lax.broadcasted_iota(jnp.int32, (t, 1), 0) + 1).astype(F32)
        has_next = (i < nt - 1).astype(F32)
        for grp, w in enumerate(POOL_WINDOWS):
            cs = slice(grp * POOL_GROUP, (grp + 1) * POOL_GROUP)
            gv, psv, zbv, dxc = g_ref[:, cs], ps_ref[:, cs], zb_ref[:, cs], dc_ref[:, cs]
            dg_ref[:, cs] += _colsum(dxc * (zbv * psv))
            dy = gv * dxc
            dps_ref[:, cs] += _colsum(dy * zbv)
            dz = dy * psv
            dpb_ref[:, cs] += _colsum(dz)
            dzb = dz.astype(BF16)
            dpw_ref[grp] += _dot(pooled_ref[:, cs], dzb, TN)
            dp = _dot(dzb, pw_ref[grp], NT)
            dzn = (gv * dn_ref[:, cs] * psv).astype(BF16)
            dpn = _dot(dzn, pw_ref[grp], NT) * (has_next / float(w))
            band_cur = ((rr - jj >= 0) & (rr - jj < w)).astype(BF16)
            band_next = (rr + t - jj < w).astype(BF16)
            dh_ref[:, cs] = _split_dot(band_cur, dp / jnp.minimum(pos, float(w))) + _split_dot(band_next, dpn) - dp

    row = pl.BlockSpec((t, d), lambda i: (i, 0))
    vec = pl.BlockSpec((1, d), lambda i: (0, 0))
    wspec = pl.BlockSpec(pw.shape, lambda i: (0, 0, 0))
    return pl.pallas_call(
        body, name=name, grid=(nt,),
        in_specs=[row, pl.BlockSpec((t, d), lambda i: (jnp.minimum(i + 1, nt - 1), 0)), row, row, wspec, vec, vec],
        out_specs=(row, wspec, vec, vec, vec),
        out_shape=(jax.ShapeDtypeStruct((s, d), F32), jax.ShapeDtypeStruct(pw.shape, F32),
                   jax.ShapeDtypeStruct((1, d), F32), jax.ShapeDtypeStruct((1, d), F32), jax.ShapeDtypeStruct((1, d), F32)),
        compiler_params=_params(("arbitrary",)),
    )(dxn, dxn, zb, pooled, pw, ps, g1)


GLU_TILE = 256
HALO = 16
INV_SQRT2 = 0.7071067811865476
INV_SQRT_2PI = 0.3989422804014327


def _gelu(xv):
    return 0.5 * xv * (1.0 + lax.erf(xv * INV_SQRT2))


def _gelu_grad(xv):
    return 0.5 * (1.0 + lax.erf(xv * INV_SQRT2)) + xv * (INV_SQRT_2PI * jnp.exp(-0.5 * xv * xv))


def _glu_fwd(name, ua, uv, cw, cb):
    s, f = ua.shape
    t, tf = GLU_TILE, _tile(f, 1408)

    def body(a_ref, ah_ref, v_ref, cw_ref, cb_ref, o_ref):
        i = pl.program_id(1)
        has_prev = (i > 0).astype(F32)
        ext = jnp.concatenate([ah_ref[...].astype(F32) * has_prev, a_ref[...].astype(F32)], axis=0)
        e1 = pltpu.roll(ext, 1, 0)[HALO:]
        e2 = pltpu.roll(ext, 2, 0)[HALO:]
        pre = e2 * cw_ref[0:1, :] + e1 * cw_ref[1:2, :] + ext[HALO:] * cw_ref[2:3, :] + cb_ref[...]
        o_ref[...] = (_gelu(pre) * v_ref[...].astype(F32)).astype(o_ref.dtype)

    blk = pl.BlockSpec((t, tf), lambda j, i: (i, j))
    halo = pl.BlockSpec((HALO, tf), lambda j, i: (jnp.maximum(i * (t // HALO) - 1, 0), j))
    return pl.pallas_call(
        body, name=name, grid=(f // tf, s // t),
        in_specs=[blk, halo, blk, pl.BlockSpec((3, tf), lambda j, i: (0, j)), pl.BlockSpec((1, tf), lambda j, i: (0, j))],
        out_specs=blk, out_shape=jax.ShapeDtypeStruct((s, f), BF16),
        compiler_params=_params(("parallel", "parallel")),
    )(ua, ua, uv, cw, cb)


def _glu_bwd(name, ua, uv, dgl, cw, cb):
    s, f = ua.shape
    t, tf = GLU_TILE, _tile(f, 1408)
    nt = s // t
    te = t + HALO

    def body(a_ref, ah_ref, an_ref, v_ref, vn_ref, d_ref, dn_ref, cw_ref, cb_ref, da_ref, dv_ref, dcw_ref, dcb_ref):
        i = pl.program_id(1)

        @pl.when(i == 0)
        def _():
            dcw_ref[...] = jnp.zeros_like(dcw_ref)
            dcb_ref[...] = jnp.zeros_like(dcb_ref)

        has_prev = (i > 0).astype(F32)
        has_next = (i < nt - 1).astype(F32)
        ext = jnp.concatenate([ah_ref[...].astype(F32) * has_prev, a_ref[...].astype(F32), an_ref[...].astype(F32)], axis=0)
        e0 = ext[HALO:]
        e1 = pltpu.roll(ext, 1, 0)[HALO:]
        e2 = pltpu.roll(ext, 2, 0)[HALO:]
        c0, c1, c2 = cw_ref[0:1, :], cw_ref[1:2, :], cw_ref[2:3, :]
        pre = e2 * c0 + e1 * c1 + e0 * c2 + cb_ref[...]
        vx = jnp.concatenate([v_ref[...].astype(F32), vn_ref[...].astype(F32)], axis=0)
        dx = jnp.concatenate([d_ref[...].astype(F32), dn_ref[...].astype(F32) * has_next], axis=0)
        dpre = dx * vx * _gelu_grad(pre)
        up1 = pltpu.roll(dpre, te - 1, 0)
        up2 = pltpu.roll(dpre, te - 2, 0)
        da_ref[...] = (dpre * c2 + up1 * c1 + up2 * c0)[:t].astype(da_ref.dtype)
        dv_ref[...] = (d_ref[...].astype(F32) * _gelu(pre[:t])).astype(dv_ref.dtype)
        dpt = dpre[:t]
        dcb_ref[...] += _colsum(dpt)
        dcw_ref[0:1, :] += _colsum(e2[:t] * dpt)
        dcw_ref[1:2, :] += _colsum(e1[:t] * dpt)
        dcw_ref[2:3, :] += _colsum(e0[:t] * dpt)

    blk = pl.BlockSpec((t, tf), lambda j, i: (i, j))
    prev = pl.BlockSpec((HALO, tf), lambda j, i: (jnp.maximum(i * (t // HALO) - 1, 0), j))
    nxt = pl.BlockSpec((HALO, tf), lambda j, i: (jnp.minimum((i + 1) * (t // HALO), s // HALO - 1), j))
    w3 = pl.BlockSpec((3, tf), lambda j, i: (0, j))
    w1 = pl.BlockSpec((1, tf), lambda j, i: (0, j))
    return pl.pallas_call(
        body, name=name, grid=(f // tf, nt),
        in_specs=[blk, prev, nxt, blk, nxt, blk, nxt, w3, w1],
        out_specs=(blk, blk, w3, w1),
        out_shape=(jax.ShapeDtypeStruct((s, f), BF16), jax.ShapeDtypeStruct((s, f), BF16),
                   jax.ShapeDtypeStruct((3, f), F32), jax.ShapeDtypeStruct((1, f), F32)),
        compiler_params=_params(("parallel", "arbitrary")),
    )(ua, ua, ua, uv, uv, dgl, dgl, cw, cb)


ATT_TILE = 512
LOG2E = 1.4426950408889634
LN2 = 0.6931471805599453


def _head_blocks_t(a, width):
    s = a.shape[0]
    t = min(ATT_TILE, s)
    return a.reshape(s // t, t, N_HEADS, width).transpose(2, 0, 3, 1)


def _head_rows(a):
    s = a.shape[0]
    t = min(ATT_TILE, s)
    r = a.reshape(s // t, t, N_HEADS, LANES)[..., 0].transpose(2, 0, 1)
    return jnp.broadcast_to(r[:, :, None, :], (N_HEADS, s // t, 8, t))


def _causal_mask(sv, q0, k0):
    row = q0 + lax.broadcasted_iota(jnp.int32, sv.shape, 0)
    col = k0 + lax.broadcasted_iota(jnp.int32, sv.shape, 1)
    return jnp.where(col <= row, sv, NEG_BIG)


def _attn_fwd(name, q_rot, kt4, v_ext):
    s = q_rot.shape[0]
    t = min(ATT_TILE, s)
    nq = s // t

    def body(q_ref, kt_ref, v_ref, o_ref, lse_ref, acc_ref, m_ref):
        qi = pl.program_id(1)
        q = q_ref[...]
        acc_ref[...] = jnp.zeros_like(acc_ref)
        m_ref[...] = jnp.full_like(m_ref, NEG_BIG)

        def step(j, masked):
            sv = _dot(q, kt_ref[0, j], NN)
            if masked:
                sv = _causal_mask(sv, qi * t, j * t)
            m_prev = m_ref[...]
            m_new = jnp.maximum(m_prev, jnp.max(sv, axis=-1, keepdims=True))
            p = jnp.exp2(sv - m_new).astype(BF16)
            acc_ref[...] = jnp.exp2(m_prev - m_new) * acc_ref[...] + _dot(p, v_ref[pl.ds(pl.multiple_of(j * t, t), t), :], NN)
            m_ref[...] = m_new

        def full_step(j, carry):
            step(j, False)
            return carry

        lax.fori_loop(0, qi, full_step, 0)
        step(qi, True)
        l = acc_ref[:, V_HEAD:V_HEAD + 1]
        o_ref[...] = (acc_ref[:, :V_HEAD] / l).astype(o_ref.dtype)
        lse_ref[...] = jnp.broadcast_to(m_ref[...] + jnp.log(l) * LOG2E, lse_ref.shape)

    head_q = pl.BlockSpec((t, Q_EXT), lambda h, i: (i, h))
    head_o = pl.BlockSpec((t, V_HEAD), lambda h, i: (i, h))
    return pl.pallas_call(
        body, name=name, grid=(N_HEADS, nq),
        in_specs=[head_q, pl.BlockSpec((1, nq, Q_EXT, t), lambda h, i: (h, 0, 0, 0)), pl.BlockSpec((s, Q_EXT), lambda h, i: (0, h))],
        out_specs=(head_o, head_o),
        out_shape=(jax.ShapeDtypeStruct((s, N_HEADS * V_HEAD), BF16), jax.ShapeDtypeStruct((s, N_HEADS * LANES), F32)),
        scratch_shapes=[pltpu.VMEM((t, Q_EXT), F32), pltpu.VMEM((t, 1), F32)],
        compiler_params=_params(("parallel", "parallel")),
    )(q_rot, kt4, v_ext)


def _attn_dq(name, q_rot, tabq, kt4, kfull, vt4, o, lse, do):
    s = q_rot.shape[0]
    t = min(ATT_TILE, s)
    nq = s // t

    def body(q_ref, tab_ref, kt_ref, k_ref, vt_ref, o_ref, lse_ref, do_ref, dq_ref, delta_ref, acc_ref):
        qi = pl.program_id(1)
        q = q_ref[...]
        dov = do_ref[...]
        delta = jnp.sum(dov.astype(F32) * o_ref[...].astype(F32), axis=-1, keepdims=True)
        lse = lse_ref[:, 0:1]
        acc_ref[...] = jnp.zeros_like(acc_ref)

        def step(j, masked):
            sv = _dot(q, kt_ref[0, j], NN)
            if masked:
                sv = _causal_mask(sv, qi * t, j * t)
            p = jnp.exp2(sv - lse)
            dp = _dot(dov, vt_ref[0, j], NN)
            ds = (p * (dp - delta)).astype(BF16)
            acc_ref[...] += _dot(ds, k_ref[pl.ds(pl.multiple_of(j * t, t), t), :], NN)

        def full_step(j, carry):
            step(j, False)
            return carry

        lax.fori_loop(0, qi, full_step, 0)
        step(qi, True)
        dq_ref[...] = (acc_ref[...] * (tab_ref[...] * LN2)).astype(dq_ref.dtype)
        delta_ref[...] = jnp.broadcast_to(delta, delta_ref.shape)

    head_q = pl.BlockSpec((t, Q_EXT), lambda h, i: (i, h))
    head_o = pl.BlockSpec((t, V_HEAD), lambda h, i: (i, h))
    return pl.pallas_call(
        body, name=name, grid=(N_HEADS, nq),
        in_specs=[head_q, pl.BlockSpec((t, Q_EXT), lambda h, i: (i, 0)), pl.BlockSpec((1, nq, Q_EXT, t), lambda h, i: (h, 0, 0, 0)),
                  pl.BlockSpec((s, Q_EXT), lambda h, i: (0, h)), pl.BlockSpec((1, nq, V_HEAD, t), lambda h, i: (h, 0, 0, 0)),
                  head_o, head_o, head_o],
        out_specs=(head_q, head_o),
        out_shape=(jax.ShapeDtypeStruct((s, N_HEADS * Q_EXT), BF16), jax.ShapeDtypeStruct((s, N_HEADS * LANES), F32)),
        scratch_shapes=[pltpu.VMEM((t, Q_EXT), F32)],
        compiler_params=_params(("parallel", "parallel")),
    )(q_rot, tabq, kt4, kfull, vt4, o, lse, do)


def _attn_dkv(name, kfull, v, qt4, q_rot, dot4, do, lse_row, delta_row, acc_in=None):
    s = kfull.shape[0]
    t = min(ATT_TILE, s)
    nq = s // t
    has_in = acc_in is not None

    def body(*refs):
        k_ref, v_ref, qt_ref, q_ref, dot_ref, do_ref, lse_ref, delta_ref = refs[:8]
        dkn_ref, dkd_ref, dv_ref, acck_ref, accv_ref = refs[-5:]
        kj, h = pl.program_id(0), pl.program_id(1)
        k_blk, v_blk = k_ref[...], v_ref[...]
        acck_ref[...] = jnp.zeros_like(acck_ref)
        accv_ref[...] = jnp.zeros_like(accv_ref)

        def step(i, masked):
            qs = pl.ds(pl.multiple_of(i * t, t), t)
            st = _dot(k_blk, qt_ref[0, i], NN)
            if masked:
                krow = lax.broadcasted_iota(jnp.int32, st.shape, 0)
                qcol = lax.broadcasted_iota(jnp.int32, st.shape, 1)
                st = jnp.where(krow <= qcol, st, NEG_BIG)
            pt = jnp.exp2(st - lse_ref[0, i, 0:1, :])
            accv_ref[...] += _dot(pt.astype(BF16), do_ref[qs, :], NN)
            dpt = _dot(v_blk, dot_ref[0, i], NN)
            dst = (pt * (dpt - delta_ref[0, i, 0:1, :])).astype(BF16)
            acck_ref[...] += _dot(dst, q_ref[qs, :], NN)

        def full_step(i, carry):
            step(i, False)
            return carry

        step(kj, True)
        lax.fori_loop(kj + 1, nq, full_step, 0)
        dk = acck_ref[...] * LN2
        dkn, dkd = dk[:, :QK_NOPE], dk[:, QK_NOPE:]
        if has_in:
            dkn = dkn + refs[8][...]
            dv_ref[...] = accv_ref[...] + refs[10][...]
        else:
            dv_ref[...] = accv_ref[...]
        dkn_ref[...] = dkn

        @pl.when(h == 0)
        def _():
            if has_in:
                dkd_ref[...] = dkd + refs[9][...]
            else:
                dkd_ref[...] = dkd

        @pl.when(h > 0)
        def _():
            dkd_ref[...] += dkd

    kblk = pl.BlockSpec((t, LANES), lambda j, h: (j, h))
    kdblk = pl.BlockSpec((t, LANES), lambda j, h: (j, 0))
    col = pl.BlockSpec((s, LANES), lambda j, h: (0, h))
    stat = pl.BlockSpec((1, nq, 8, t), lambda j, h: (h, 0, 0, 0))
    ins = [kfull, v, qt4, q_rot, dot4, do, lse_row, delta_row]
    in_specs = [pl.BlockSpec((t, Q_EXT), lambda j, h: (j, h)), kblk, pl.BlockSpec((1, nq, Q_EXT, t), lambda j, h: (h, 0, 0, 0)),
                pl.BlockSpec((s, Q_EXT), lambda j, h: (0, h)), pl.BlockSpec((1, nq, V_HEAD, t), lambda j, h: (h, 0, 0, 0)), col, stat, stat]
    if has_in:
        ins += list(acc_in)
        in_specs += [kblk, kdblk, kblk]
    return pl.pallas_call(
        body, name=name, grid=(nq, N_HEADS), in_specs=in_specs, out_specs=(kblk, kdblk, kblk),
        out_shape=(jax.ShapeDtypeStruct((s, N_HEADS * LANES), F32), jax.ShapeDtypeStruct((s, LANES), F32),
                   jax.ShapeDtypeStruct((s, N_HEADS * LANES), F32)),
        scratch_shapes=[pltpu.VMEM((t, Q_EXT), F32), pltpu.VMEM((t, LANES), F32)],
        compiler_params=_params(("parallel", "arbitrary")),
    )(*ins)


def _swap_halves(w):
    half = w.shape[-1] // 2
    return jnp.concatenate([-w[..., half:], w[..., :half]], axis=-1)


def _unswap_halves(g):
    half = g.shape[-1] // 2
    return jnp.concatenate([g[..., half:], -g[..., :half]], axis=-1)


def _extend_w_uq(w):
    r = w.reshape(Q_RANK, N_HEADS, QK_HEAD)
    rope = r[..., QK_NOPE:]
    return jnp.concatenate([r[..., :QK_NOPE], rope, _swap_halves(rope)], axis=-1).reshape(Q_RANK, N_HEADS * Q_EXT)


def _fold_w_uq_grad(g):
    r = g.reshape(Q_RANK, N_HEADS, Q_EXT)
    rope = r[..., QK_NOPE:QK_HEAD] + _unswap_halves(r[..., QK_HEAD:])
    return jnp.concatenate([r[..., :QK_NOPE], rope], axis=-1).reshape(Q_RANK, N_HEADS * QK_HEAD)


def _extend_w_dkv(w):
    return jnp.concatenate([w, _swap_halves(w[:, KV_RANK:])], axis=-1)


def _fold_w_dkv_grad(g):
    rope = g[:, KV_RANK:KV_RANK + QK_ROPE] + _unswap_halves(g[:, KV_RANK + QK_ROPE:])
    return jnp.concatenate([g[:, :KV_RANK], rope], axis=-1)


def _rope_tables(positions):
    inv = 1.0 / (ROPE_THETA ** (jnp.arange(0, QK_ROPE, 2, dtype=F32) / QK_ROPE))
    ang = positions.astype(F32)[:, None] * inv
    cos, sin = jnp.cos(ang), jnp.sin(ang)
    tabk = jnp.concatenate([cos, cos, sin, sin], axis=-1)
    scale = QK_HEAD ** -0.5 * LOG2E
    tabq = jnp.concatenate([jnp.full((positions.shape[0], QK_NOPE), scale, F32), tabk * scale], axis=-1)
    return tabq, tabk


def _forward_backward(x, target, mods, tabq, tabk, final_g, fetch, push):
    row = lambda vec: vec.reshape(1, -1)
    mod = [[row(mods[l, k * D_MODEL:(k + 1) * D_MODEL]) for k in range(N_MOD)] for l in range(DEPTH)]
    saved, weights = [], []
    kv = None
    for l in range(DEPTH):
        w, tok = fetch(l, x)
        weights.append(w)
        sh1, sc1, g1, sh2, sc2, g2 = mod[l]
        sh1 = sh1 + tok
        if l == N_A_LAYERS:
            kvn = _rms_fwd("kvin_fwd", x, row(w["kv_in_g"]))
            kv_ext = _mm("dkv_fwd", kvn, w["w_dkv_ext"], out_dtype=F32)
            ckv = _rms_fwd("ckv_fwd", kv_ext, row(w["ckv_norm_g"]), ncols=KV_RANK)
            kd = _krope_fwd("krope_fwd", kv_ext, tabk)
            kn, v = _mm("uk_fwd", ckv, w["w_uk"]), _mm("uv_fwd", ckv, w["w_uv"])
            heads = lambda a: [a[:, h * LANES:(h + 1) * LANES] for h in range(N_HEADS)]
            kfull = jnp.concatenate([part for kh in heads(kn) for part in (kh, kd)], axis=-1)
            v_ext = jnp.concatenate([part for vh in heads(v) for part in (vh, jnp.ones_like(vh))], axis=-1)
            kv = dict(x=x, kvn=kvn, kv_ext=kv_ext, ckv=ckv, v=v, kfull=kfull, v_ext=v_ext,
                      kt4=_head_blocks_t(kfull, Q_EXT), vt4=_head_blocks_t(v, V_HEAD))
        x_in = x
        if l < N_A_LAYERS:
            h1 = _rms_fwd(f"norm1_fwd_{l}", x, row(w["norm1_g"]), sc1, sh1, out_dtype=F32)
            x_mid, zb, pooled = _pool_fwd(f"pool_fwd_{l}", h1, x, w["pool_w"], row(w["pool_b"]), row(w["pool_scale"]), g1)
            mix = (zb, pooled)
        else:
            h1 = _rms_fwd(f"norm1_fwd_{l}", x, row(w["norm1_g"]), sc1, sh1)
            cq_pre = _mm(f"dq_fwd_{l}", h1, w["w_dq"], out_dtype=F32)
            cq = _rms_fwd(f"qnorm_fwd_{l}", cq_pre, row(w["q_norm_g"]))
            q_rot = _mm(f"uq_fwd_{l}", cq, w["w_uq_ext"], rowtab=tabq)
            o, lse = _attn_fwd(f"attn_fwd_{l}", q_rot, kv["kt4"], kv["v_ext"])
            y, x_mid = _mm(f"wo_fwd_{l}", o, w["w_o"], resid=x, gate=g1)
            mix = (h1, cq_pre, cq, q_rot, o, lse, y)
        h2 = _rms_fwd(f"norm2_fwd_{l}", x_mid, row(w["norm2_g"]), sc2, sh2)
        ua = _mm(f"up_a_fwd_{l}", h2, w["w_up_a"])
        uv = _mm(f"up_v_fwd_{l}", h2, w["w_up_v"])
        gl = _glu_fwd(f"glu_fwd_{l}", ua, uv, w["conv_w"], row(w["conv_b"]))
        y2, x = _mm(f"down_fwd_{l}", gl, w["w_down"], resid=x_mid, gate=g2)
        saved.append((x_in, x_mid, h2, ua, uv, gl, y2, mix))

    dx, dfinal_g, loss = _loss_head("loss_head", x, row(final_g), target)
    g = {"final_g": dfinal_g.reshape(-1)}
    per_layer = {k: [None] * DEPTH for k in ("norm1_g", "norm2_g", "conv_w", "conv_b")}
    per_a = {k: [None] * N_A_LAYERS for k in ("pool_b", "pool_scale")}
    per_b = {k: [None] * N_B_LAYERS for k in ("q_norm_g",)}
    dmods = [None] * DEPTH
    dkv = None
    tok = 0.0
    for l in reversed(range(DEPTH)):
        w, big = weights[l], {}
        sh1, sc1, g1, sh2, sc2, g2 = mod[l]
        g2 = g2 + tok
        x_in, x_mid, h2, ua, uv, gl, y2, mix = saved[l]
        dy2, dg2 = _gate_bwd(f"gate2_bwd_{l}", dx, y2, g2)
        dgl = _mm(f"down_bwd_{l}", dy2, w["w_down"], mode="nt")
        big["w_down"] = _mm(f"down_wgrad_{l}", gl, dy2, mode="tn", out_dtype=F32)
        da, dv_, dcw, dcb = _glu_bwd(f"glu_bwd_{l}", ua, uv, dgl, w["conv_w"], row(w["conv_b"]))
        dh2 = _mm(f"up_a_bwd_{l}", da, w["w_up_a"], mode="nt", out_dtype=F32)
        dh2 = _mm(f"up_v_bwd_{l}", dv_, w["w_up_v"], mode="nt", out_dtype=F32, add=dh2)
        big["w_up_a"] = _mm(f"up_a_wgrad_{l}", h2, da, mode="tn", out_dtype=F32)
        big["w_up_v"] = _mm(f"up_v_wgrad_{l}", h2, dv_, mode="tn", out_dtype=F32)
        per_layer["conv_w"][l], per_layer["conv_b"][l] = dcw, dcb.reshape(-1)
        dx_mid, dn2, dsh2, dsc2 = _rms_bwd(f"norm2_bwd_{l}", x_mid, row(w["norm2_g"]), dh2, sc2, dx_in=dx)
        per_layer["norm2_g"][l] = dn2.reshape(-1)
        if l < N_A_LAYERS:
            zb, pooled = mix
            dh1, dpw, dpb, dps, dg1 = _pool_bwd(f"pool_bwd_{l}", dx_mid, zb, pooled, w["pool_w"], row(w["pool_scale"]), g1)
            big["pool_w"] = dpw
            per_a["pool_b"][l], per_a["pool_scale"][l] = dpb.reshape(-1), dps.reshape(-1)
        else:
            j = l - N_A_LAYERS
            h1, cq_pre, cq, q_rot, o, lse, y = mix
            dy, dg1 = _gate_bwd(f"gate1_bwd_{l}", dx_mid, y, g1)
            do = _mm(f"wo_bwd_{l}", dy, w["w_o"], mode="nt")
            big["w_o"] = _mm(f"wo_wgrad_{l}", o, dy, mode="tn", out_dtype=F32)
            dq_ext, delta = _attn_dq(f"attn_dq_{l}", q_rot, tabq, kv["kt4"], kv["kfull"], kv["vt4"], o, lse, do)
            dkv = _attn_dkv(f"attn_dkv_{l}", kv["kfull"], kv["v"], _head_blocks_t(q_rot, Q_EXT), q_rot, _head_blocks_t(do, V_HEAD), do,
                            _head_rows(lse), _head_rows(delta), acc_in=dkv)
            dcq = _mm(f"uq_bwd_{l}", dq_ext, w["w_uq_ext"], mode="nt", out_dtype=F32)
            big["w_uq_ext"] = _mm(f"uq_wgrad_{l}", cq, dq_ext, mode="tn", out_dtype=F32)
            dcq_pre, dqn = _rms_bwd(f"qnorm_bwd_{l}", cq_pre, row(w["q_norm_g"]), dcq, out_dtype=BF16)
            per_b["q_norm_g"][j] = dqn.reshape(-1)
            dh1 = _mm(f"dq_bwd_{l}", dcq_pre, w["w_dq"], mode="nt")
            big["w_dq"] = _mm(f"dq_wgrad_{l}", h1, dcq_pre, mode="tn", out_dtype=F32)
        dx, dn1, dsh1, dsc1 = _rms_bwd(f"norm1_bwd_{l}", x_in, row(w["norm1_g"]), dh1, sc1, dx_in=dx_mid)
        per_layer["norm1_g"][l] = dn1.reshape(-1)
        dmods[l] = jnp.concatenate([dsh1, dsc1, dg1, dsh2, dsc2, dg2], axis=-1).reshape(-1)
        if l == N_A_LAYERS:
            dkn, dkd, dv = dkv
            dckv = _mm("uk_bwd", dkn, w["w_uk"], mode="nt", out_dtype=F32)
            dckv = _mm("uv_bwd", dv, w["w_uv"], mode="nt", out_dtype=F32, add=dckv)
            big["w_uk"] = _mm("uk_wgrad", kv["ckv"], dkn, mode="tn", out_dtype=F32)
            big["w_uv"] = _mm("uv_wgrad", kv["ckv"], dv, mode="tn", out_dtype=F32)
            dkr = _krope_bwd("krope_bwd", dkd, tabk)
            dc, dckv_g = _rms_bwd("ckv_bwd", kv["kv_ext"], row(w["ckv_norm_g"]), dckv, ncols=KV_RANK, out_dtype=BF16)
            dkv_ext = jnp.concatenate([dc, dkr.astype(BF16)], axis=-1)
            dkvn = _mm("dkv_bwd", dkv_ext, w["w_dkv_ext"], mode="nt")
            big["w_dkv_ext"] = _mm("dkv_wgrad", kv["kvn"], dkv_ext, mode="tn", out_dtype=F32)
            dx, dkv_in_g = _rms_bwd("kvin_bwd", kv["x"], row(w["kv_in_g"]), dkvn, dx_in=dx)
            g["ckv_norm_g"], g["kv_in_g"] = dckv_g.reshape(-1), dkv_in_g.reshape(-1)
        tok = push(l, big, dx)
    for group in (per_layer, per_a, per_b):
        for k, vals in group.items():
            g[k] = jnp.stack(vals)
    return loss, dx, g, jnp.stack(dmods)


def _my_index():
    return 4 * lax.axis_index("x") + 2 * lax.axis_index("y") + lax.axis_index("c")


def _peer(k):
    x, y, c = lax.axis_index("x"), lax.axis_index("y"), lax.axis_index("c")
    return (1 - x if k & 4 else x, 1 - y if k & 2 else y, 1 - c if k & 1 else c)


def _index_of(pos):
    return 4 * pos[0] + 2 * pos[1] + pos[2]


def _exchange_many(name, arrays, scatter):
    n = len(arrays)
    blocks = [tuple(a.shape[1:]) if scatter else tuple(a.shape) for a in arrays]

    def body(*refs):
        x_refs, o_refs = refs[:n], refs[n:2 * n]
        send_sems, recv_sems, local_sems = refs[2 * n:]
        me = _my_index()
        started = []
        for a in range(n):
            mine = pltpu.make_async_copy(x_refs[a].at[me] if scatter else x_refs[a], o_refs[a].at[me], local_sems.at[a])
            mine.start()
            started.append(mine)
        sends = []
        for k in range(1, N_DEV):
            peer = _peer(k)
            for a in range(n):
                cp = pltpu.make_async_remote_copy(
                    src_ref=x_refs[a].at[_index_of(peer)] if scatter else x_refs[a], dst_ref=o_refs[a].at[me],
                    send_sem=send_sems.at[a, k - 1], recv_sem=recv_sems.at[a, k - 1], device_id=peer, device_id_type=MESH)
                cp.start()
                sends.append(cp)
        for k in range(1, N_DEV):
            peer = _peer(k)
            for a in range(n):
                pltpu.make_async_remote_copy(
                    src_ref=x_refs[a].at[me] if scatter else x_refs[a], dst_ref=o_refs[a].at[_index_of(peer)],
                    send_sem=send_sems.at[a, k - 1], recv_sem=recv_sems.at[a, k - 1], device_id=peer, device_id_type=MESH).wait_recv()
        for cp in sends:
            cp.wait_send()
        for mine in started:
            mine.wait()

    return pl.pallas_call(
        body, name=name, out_shape=tuple(jax.ShapeDtypeStruct((N_DEV,) + blk, a.dtype) for blk, a in zip(blocks, arrays)),
        in_specs=[pl.BlockSpec(memory_space=pl.ANY)] * n, out_specs=tuple([pl.BlockSpec(memory_space=pl.ANY)] * n),
        scratch_shapes=[pltpu.SemaphoreType.DMA((n, N_DEV - 1)), pltpu.SemaphoreType.DMA((n, N_DEV - 1)), pltpu.SemaphoreType.DMA((n,))],
    )(*arrays)


def _exchange(name, x, scatter):
    return _exchange_many(name, [x], scatter)[0]


HBM_SPEC = pl.BlockSpec(memory_space=pltpu.HBM)
SEM_SPEC = pl.BlockSpec(memory_space=pltpu.SEMAPHORE)
DATAFLOW = pltpu.SideEffectType.DATAFLOW_SIDE_EFFECTING


def _remote_copies(x_refs, land_refs, send_sems, recv_sems, scatter):
    me = _my_index()
    out, inc = [], []
    for k in range(1, N_DEV):
        peer = _peer(k)
        for a in range(len(x_refs)):
            pair = a * (N_DEV - 1) + k - 1
            sems = dict(send_sem=send_sems.at[pair], recv_sem=recv_sems.at[pair], device_id=peer, device_id_type=MESH)
            out.append(pltpu.make_async_remote_copy(
                src_ref=x_refs[a].at[_index_of(peer)] if scatter else x_refs[a], dst_ref=land_refs[a].at[me], **sems))
            inc.append(pltpu.make_async_remote_copy(
                src_ref=x_refs[a].at[me] if scatter else x_refs[a], dst_ref=land_refs[a].at[_index_of(peer)], **sems))
    return out, inc


def _exchange_start(name, arrays, scatter):
    n = len(arrays)
    blocks = [tuple(a.shape[1:]) if scatter else tuple(a.shape) for a in arrays]

    def body(*refs):
        x_refs, land_refs = refs[:n], refs[n:2 * n]
        send_sems, recv_sems = refs[2 * n], refs[2 * n + 1]
        for cp in _remote_copies(x_refs, land_refs, send_sems, recv_sems, scatter)[0]:
            cp.start()
        refs[-1][...] = jnp.zeros_like(refs[-1])

    sem_type = pltpu.SemaphoreType.DMA((n * (N_DEV - 1),))
    lands =[pltpu.with_memory_space_constraint(lax.empty((N_DEV,) + blk, a.dtype), pltpu.HBM) for blk, a in zip(blocks, arrays)]
    srcs = [pltpu.with_memory_space_constraint(a, pltpu.HBM) for a in arrays]
    res = pl.pallas_call(
        body, name=name,
        out_shape=(sem_type, sem_type, *[pltpu.HBM(a.shape, a.dtype) for a in srcs + lands], jax.ShapeDtypeStruct((8, LANES), F32)),
        in_specs=[HBM_SPEC] * (2 * n), out_specs=(SEM_SPEC, SEM_SPEC, *[HBM_SPEC] * (2 * n), pl.BlockSpec(memory_space=pltpu.VMEM)),
        input_output_aliases={i: 2 + i for i in range(2 * n)},
        compiler_params=pltpu.CompilerParams(has_side_effects=DATAFLOW),
    )(*srcs, *lands)
    return (res[0], res[1], list(res[2:2 + n]), list(res[2 + n:2 + 2 * n])), res[-1]


def _exchange_wait(name, handles, after, scatter):
    send_sems, recv_sems, srcs, lands = handles
    n = len(srcs)

    def body(*refs):
        x_refs, land_refs = refs[:n], refs[n:2 * n]
        out, inc = _remote_copies(x_refs, land_refs, refs[2 * n], refs[2 * n + 1], scatter)
        for cp in out:
            cp.wait_send()
        for cp in inc:
            cp.wait_recv()

    res = pl.pallas_call(
        body, name=name, out_shape=tuple(pltpu.HBM(a.shape, a.dtype) for a in srcs + lands),
        in_specs=[HBM_SPEC] * (2 * n) + [SEM_SPEC, SEM_SPEC, pl.BlockSpec(memory_space=pl.ANY)], out_specs=tuple([HBM_SPEC] * (2 * n)),
        input_output_aliases={i: i for i in range(2 * n)},
        compiler_params=pltpu.CompilerParams(has_side_effects=DATAFLOW),
    )(*srcs, *lands, send_sems, recv_sems, after)
    return list(res[n:])


def _pack(arrays, dtype, row_multiple):
    flat = jnp.concatenate([a.astype(dtype).reshape(-1) for a in arrays])
    rows = -(-flat.shape[0] // (LANES * row_multiple)) * row_multiple
    return jnp.pad(flat, (0, rows * LANES - flat.shape[0])).reshape(rows, LANES)


def _unpack(packed, shapes):
    lead = packed.shape[:-2]
    flat = packed.reshape(lead + (-1,))
    out, off = [], 0
    for shp in shapes:
        size = 1
        for d in shp:
            size *= d
        out.append(flat[..., off:off + size].reshape(lead + tuple(shp)))
        off += size
    return out


def _unshard(g8, axis):
    return jnp.concatenate([g8[j] for j in range(N_DEV)], axis=axis)


def _shard8(full, axis):
    n = full.shape[axis] // N_DEV
    return jnp.stack([lax.slice_in_dim(full, j * n, (j + 1) * n, axis=axis) for j in range(N_DEV)])


VECTOR_WEIGHTS = (("pool_b", 1), ("pool_scale", 1), ("conv_w", 2))
REPLICATED_WEIGHTS = ("norm1_g", "norm2_g", "kv_in_g", "ckv_norm_g", "q_norm_g", "conv_b", "final_g")
WEIGHT_ORDER = ("mod_w", "mod_b", "norm1_g", "norm2_g", "pool_w", "pool_b", "pool_scale", "kv_in_g", "w_dkv", "ckv_norm_g", "w_uk",
                "w_uv", "w_dq", "q_norm_g", "w_uq", "w_o", "w_up", "conv_w", "conv_b", "w_down", "final_g")
BIG_ROW_MULTIPLE = 1024
SMALL_ROW_MULTIPLE = 16


def _as_2d(a):
    if a.ndim == 1:
        return a.reshape(-1, LANES)
    return a.reshape(-1, a.shape[-1])


def kernel(x, c, positions, mod_w, mod_b, norm1_g, norm2_g, pool_w, pool_b, pool_scale, kv_in_g, w_dkv, ckv_norm_g, w_uk, w_uv, w_dq, q_norm_g, w_uq, w_o, w_up, conv_w, conv_b, w_down, final_g, loss_target, m_mod_w, m_mod_b, m_norm1_g, m_norm2_g, m_pool_w, m_pool_b, m_pool_scale, m_kv_in_g, m_w_dkv, m_ckv_norm_g, m_w_uk, m_w_uv, m_w_dq, m_q_norm_g, m_w_uq, m_w_o, m_w_up, m_conv_w, m_conv_b, m_w_down, m_final_g, v_mod_w, v_mod_b, v_norm1_g, v_norm2_g, v_pool_w, v_pool_b, v_pool_scale, v_kv_in_g, v_w_dkv, v_ckv_norm_g, v_w_uk, v_w_uv, v_w_dq, v_q_norm_g, v_w_uq, v_w_o, v_w_up, v_conv_w, v_conv_b, v_w_down, v_final_g):
    shard = dict(mod_w=mod_w, mod_b=mod_b, norm1_g=norm1_g, norm2_g=norm2_g, pool_w=pool_w, pool_b=pool_b, pool_scale=pool_scale,
                 kv_in_g=kv_in_g, w_dkv=w_dkv, ckv_norm_g=ckv_norm_g, w_uk=w_uk, w_uv=w_uv, w_dq=w_dq, q_norm_g=q_norm_g, w_uq=w_uq,
                 w_o=w_o, w_up=w_up, conv_w=conv_w, conv_b=conv_b, w_down=w_down, final_g=final_g)
    mom_m = dict(mod_w=m_mod_w, mod_b=m_mod_b, norm1_g=m_norm1_g, norm2_g=m_norm2_g, pool_w=m_pool_w, pool_b=m_pool_b,
                 pool_scale=m_pool_scale, kv_in_g=m_kv_in_g, w_dkv=m_w_dkv, ckv_norm_g=m_ckv_norm_g, w_uk=m_w_uk, w_uv=m_w_uv,
                 w_dq=m_w_dq, q_norm_g=m_q_norm_g, w_uq=m_w_uq, w_o=m_w_o, w_up=m_w_up, conv_w=m_conv_w, conv_b=m_conv_b,
                 w_down=m_w_down, final_g=m_final_g)
    mom_v = dict(mod_w=v_mod_w, mod_b=v_mod_b, norm1_g=v_norm1_g, norm2_g=v_norm2_g, pool_w=v_pool_w, pool_b=v_pool_b,
                 pool_scale=v_pool_scale, kv_in_g=v_kv_in_g, w_dkv=v_w_dkv, ckv_norm_g=v_ckv_norm_g, w_uk=v_w_uk, w_uv=v_w_uv,
                 w_dq=v_w_dq, q_norm_g=v_q_norm_g, w_uq=v_w_uq, w_o=v_w_o, w_up=v_w_up, conv_w=v_conv_w, conv_b=v_conv_b,
                 w_down=v_w_down, final_g=v_final_g)
    me = _my_index()
    d6 = N_MOD * D_MODEL
    mod_cols = d6 // N_DEV

    small_in = [c] + [shard[k] for k, _ in VECTOR_WEIGHTS]
    small_all = _exchange("gather_vectors", _pack(small_in, F32, SMALL_ROW_MULTIPLE), scatter=False)
    parts = _unpack(small_all, [a.shape for a in small_in])
    c_all = jnp.pad(parts[0].reshape(N_DEV, D_MODEL), ((0, N_DEV), (0, 0)))
    vec = {k: _unshard(p, ax) for (k, ax), p in zip(VECTOR_WEIGHTS, parts[1:])}

    my_mod_b = lax.dynamic_slice_in_dim(mod_b, me * mod_cols, mod_cols, axis=1)
    mods_mine = _mods_fwd("mods_fwd", c_all, mod_w, my_mod_b)
    mods_all = _exchange("gather_mods", _pack([mods_mine], F32, SMALL_ROW_MULTIPLE), scatter=False)
    mods_all = _unpack(mods_all, [mods_mine.shape])[0]
    mods = lax.dynamic_index_in_dim(mods_all, me, axis=2, keepdims=False)
    mods = jnp.moveaxis(mods, 0, 1).reshape(DEPTH, d6)

    tabq, tabk = _rope_tables(positions[0])
    half = N_DEV // 2
    up_cols = shard["w_up"].shape[2]

    def stage_pieces(l):
        keys = ["w_up", "w_down"]
        if l >= N_A_LAYERS:
            keys += ["w_dq", "w_uq", "w_o"]
        layer = l if l < N_A_LAYERS else l - N_A_LAYERS
        out = {k: shard[k][l if k in ("w_up", "w_down") else layer].astype(BF16) for k in keys}
        if l == 0:
            out["pool_w"] = shard["pool_w"].astype(BF16)
        if l == N_A_LAYERS:
            out.update({k: shard[k].astype(BF16) for k in ("w_dkv", "w_uk", "w_uv")})
        return out

    pool_all = []

    def whole_weights(l, got):
        cat = lambda a, axis, lo=0, hi=N_DEV: jnp.concatenate([a[j] for j in range(lo, hi)], axis=axis)
        w = dict(w_up_a=cat(got["w_up"], -1, 0, half), w_up_v=cat(got["w_up"], -1, half, N_DEV), w_down=got["w_down"].reshape(D_FF, D_MODEL),
                 norm1_g=norm1_g[l], norm2_g=norm2_g[l], conv_w=vec["conv_w"][l], conv_b=conv_b[l])
        if l == 0:
            pool_all.append(got["pool_w"])
        if l < N_A_LAYERS:
            w.update(pool_w=cat(pool_all[0][:, l], 1), pool_b=vec["pool_b"][l], pool_scale=vec["pool_scale"][l])
        else:
            rope = got["w_uq"][..., QK_NOPE:]
            ext = jnp.concatenate([got["w_uq"][..., :QK_NOPE], rope, _swap_halves(rope)], axis=-1)
            w.update(w_dq=got["w_dq"].reshape(D_MODEL, Q_RANK), w_uq_ext=cat(ext, -1), w_o=got["w_o"].reshape(D_MODEL, D_MODEL),
                     q_norm_g=q_norm_g[l - N_A_LAYERS])
        if l == N_A_LAYERS:
            w.update(w_dkv_ext=_extend_w_dkv(got["w_dkv"].reshape(D_MODEL, KV_RANK + QK_ROPE)), w_uk=cat(got["w_uk"], -1),
                     w_uv=cat(got["w_uv"], -1), kv_in_g=kv_in_g, ckv_norm_g=ckv_norm_g)
        return w

    def own_slot(lands, own):
        return [lax.dynamic_update_index_in_dim(p, o, me, 0) for p, o in zip(lands, own)]

    gathers = {}

    def fetch(l, after):
        if l == 0:
            pieces = stage_pieces(0)
            got = dict(zip(pieces, _exchange_many("gather_weights_0", list(pieces.values()), scatter=False)))
        else:
            handles, pieces = gathers.pop(l)
            lands = _exchange_wait(f"gather_wait_{l}", handles, after, scatter=False)
            got = dict(zip(pieces, own_slot(lands, list(pieces.values()))))
        tok = 0.0
        if l + 1 < DEPTH:
            nxt = stage_pieces(l + 1)
            handles, token = _exchange_start(f"gather_start_{l + 1}", list(nxt.values()), scatter=False)
            gathers[l + 1] = (handles, nxt)
            tok = token[0, 0]
        return whole_weights(l, got), tok

    scatters, pool_grads, piece_grads = {}, {}, {}

    def reduce_pieces(l, keys, got):
        for k, p in zip(keys, got):
            piece_grads[(k, l)] = _sum8(f"sum_grads_{k}_{l}", p.reshape(N_DEV, -1, p.shape[-1])).reshape(p.shape[1:])

    def push(l, big, after):
        cut = lambda a, n, axis: jnp.stack([lax.slice_in_dim(a, j * n, (j + 1) * n, axis=axis) for j in range(N_DEV)])
        sent = dict(w_up=jnp.stack([lax.slice_in_dim(big[part], j * up_cols, (j + 1) * up_cols, axis=1)
                                    for part in ("w_up_a", "w_up_v") for j in range(half)]),
                    w_down=big["w_down"].reshape(N_DEV, D_FF // N_DEV, D_MODEL))
        if l < N_A_LAYERS:
            pool_grads[l] = big["pool_w"]
        else:
            ext = cut(big["w_uq_ext"], Q_EXT, 1)
            rope = ext[..., QK_NOPE:QK_HEAD] + _unswap_halves(ext[..., QK_HEAD:])
            sent.update(w_dq=big["w_dq"].reshape(N_DEV, D_MODEL // N_DEV, Q_RANK), w_uq=jnp.concatenate([ext[..., :QK_NOPE], rope], axis=-1),
                        w_o=big["w_o"].reshape(N_DEV, D_MODEL // N_DEV, D_MODEL))
        if l == 0:
            sent["pool_w"] = _shard8(jnp.stack([pool_grads[a] for a in range(N_A_LAYERS)]), 2)
        if l == N_A_LAYERS:
            sent.update(w_dkv=_fold_w_dkv_grad(big["w_dkv_ext"]).reshape(N_DEV, D_MODEL // N_DEV, KV_RANK + QK_ROPE),
                        w_uk=cut(big["w_uk"], QK_NOPE, 1), w_uv=cut(big["w_uv"], V_HEAD, 1))
        sent = {k: a.astype(BF16) for k, a in sent.items()}
        if l + 1 < DEPTH:
            handles, keys, own = scatters.pop(l + 1)
            reduce_pieces(l + 1, keys, own_slot(_exchange_wait(f"scatter_wait_{l + 1}", handles, after, scatter=True), own))
        if l == 0:
            reduce_pieces(0, list(sent), _exchange_many("scatter_grads_0", list(sent.values()), scatter=True))
            return 0.0
        handles, token = _exchange_start(f"scatter_start_{l}", list(sent.values()), scatter=True)
        scatters[l] = (handles, list(sent), [lax.dynamic_index_in_dim(a, me, 0, keepdims=False) for a in sent.values()])
        return token[0, 0]

    loss_row, dx, g, dmods = _forward_backward(x[0], loss_target[0], mods, tabq, tabk, final_g, fetch, push)
    layers_of = lambda k, ls: jnp.stack([piece_grads[(k, l)] for l in ls])
    grads = dict(w_up=layers_of("w_up", range(DEPTH)), w_down=layers_of("w_down", range(DEPTH)), pool_w=piece_grads[("pool_w", 0)],
                 w_dkv=piece_grads[("w_dkv", N_A_LAYERS)], w_uk=piece_grads[("w_uk", N_A_LAYERS)], w_uv=piece_grads[("w_uv", N_A_LAYERS)])
    for k in ("w_dq", "w_uq", "w_o"):
        grads[k] = layers_of(k, range(N_A_LAYERS, DEPTH))

    small_names = REPLICATED_WEIGHTS + tuple(k for k, _ in VECTOR_WEIGHTS)
    small_out = [dmods] + [g[k] for k in small_names] + [loss_row]
    small_shapes = [a.shape for a in small_out]
    small_got = _exchange("gather_small_grads", _pack(small_out, F32, SMALL_ROW_MULTIPLE), scatter=False)
    summed = _unpack(_sum8("sum_small_grads", small_got), small_shapes)
    grads["mod_b"] = summed[0]
    for k, s in zip(small_names, summed[1:-1]):
        grads[k] = s
    for k, ax in VECTOR_WEIGHTS:
        n = shard[k].shape[ax]
        grads[k] = lax.dynamic_slice_in_dim(grads[k], me * n, n, axis=ax)
    loss = summed[-1][0, 0]
    dmods_all = _unpack(small_got, small_shapes)[0]
    dm_mine = lax.dynamic_slice_in_dim(dmods_all, me * mod_cols, mod_cols, axis=2)
    dm_mine = jnp.pad(jnp.moveaxis(dm_mine, 0, 1), ((0, 0), (0, N_DEV), (0, 0)))
    grads["mod_w"] = _mods_bwd("mods_bwd", c_all, dm_mine)

    delta, new_m, new_v = {}, {}, {}
    for k in WEIGHT_ORDER:
        shp = shard[k].shape
        grads[k] = grads[k].reshape(shp)
        d_, m_, v_ = _adamw(f"adamw_{k}", _as_2d(shard[k]), _as_2d(grads[k]), _as_2d(mom_m[k]), _as_2d(mom_v[k]))
        delta[k], new_m[k], new_v[k] = d_.reshape(shp), m_.reshape(shp), v_.reshape(shp)
    return (loss, dx[None], *[grads[k] for k in WEIGHT_ORDER], *[delta[k] for k in WEIGHT_ORDER],
            *[new_m[k] for k in WEIGHT_ORDER], *[new_v[k] for k in WEIGHT_ORDER])
```

```python
import functools

import jax
import jax.numpy as jnp
from jax import lax
from jax.experimental import pallas as pl
from jax.experimental.pallas import tpu as pltpu

F32 = jnp.float32
BF16 = jnp.bfloat16

D_MODEL = 1024
DEPTH = 4
N_A_LAYERS = 2
N_B_LAYERS = 2
POOL_WINDOWS = (2, 4, 8, 16)
POOL_GROUP = 256
N_HEADS = 8
QK_NOPE = 128
QK_ROPE = 64
V_HEAD = 128
QK_HEAD = QK_NOPE + QK_ROPE
Q_RANK = 384
KV_RANK = 256
ROPE_THETA = 10000.0
D_FF = 2816
EPS = 1e-6
N_MOD = 6
ADAM_LR = 0.001
ADAM_B1 = 0.9
ADAM_B2 = 0.999
ADAM_EPS = 1e-08
ADAM_WD = 0.01
ADAM_STEP = 10

N_DEV = 8
LANES = 128
Q_EXT = 256
VMEM_LIMIT_BYTES = 48 * 1024 * 1024
MESH = pl.DeviceIdType.MESH
NEG_BIG = -0.7 * float(jnp.finfo(jnp.float32).max)


def _params(sem):
    return pltpu.CompilerParams(dimension_semantics=sem, vmem_limit_bytes=VMEM_LIMIT_BYTES)


def _tile(n, cap):
    if n <= cap:
        return n
    best = None
    for d in range(LANES, cap + 1, LANES):
        if n % d == 0:
            best = d
    assert best is not None, (n, cap)
    return best


def _dot(a, b, dims):
    return lax.dot_general(a, b, (dims, ((), ())), preferred_element_type=F32)


NN = ((1,), (0,))
NT = ((1,), (1,))
TN = ((0,), (0,))


def _mm(name, a, b, mode="nn", out_dtype=BF16, add=None, resid=None, gate=None, rowtab=None,
        tm_cap=1024, tn_cap=1408, tk_cap=1408):
    if mode == "tn":
        kdim, m = a.shape
    else:
        m, kdim = a.shape
    n = b.shape[0] if mode == "nt" else b.shape[1]
    tm, tn, tk = _tile(m, tm_cap), _tile(n, tn_cap), _tile(kdim, tk_cap)
    nk = kdim // tk
    dims = {"nn": NN, "nt": NT, "tn": TN}[mode]
    a_spec = pl.BlockSpec((tk, tm), lambda i, j, k: (k, i)) if mode == "tn" else pl.BlockSpec((tm, tk), lambda i, j, k: (i, k))
    b_spec = pl.BlockSpec((tn, tk), lambda i, j, k: (j, k)) if mode == "nt" else pl.BlockSpec((tk, tn), lambda i, j, k: (k, j))
    o_spec = pl.BlockSpec((tm, tn), lambda i, j, k: (i, j))
    g_spec = pl.BlockSpec((1, tn), lambda i, j, k: (0, j))
    gated = resid is not None

    def body(*refs):
        a_ref, b_ref = refs[0], refs[1]
        acc = refs[-1]
        k = pl.program_id(2)

        @pl.when(k == 0)
        def _():
            acc[...] = jnp.zeros_like(acc)

        acc[...] += _dot(a_ref[...].astype(BF16), b_ref[...].astype(BF16), dims)

        @pl.when(k == nk - 1)
        def _():
            if gated:
                r_ref, g_ref, y_ref, x_ref = refs[2:6]
                y_ref[...] = acc[...]
                x_ref[...] = r_ref[...] + g_ref[...] * acc[...]
            elif add is not None:
                refs[3][...] = (acc[...] + refs[2][...].astype(F32)).astype(out_dtype)
            elif rowtab is not None:
                tab = refs[2][...]
                refs[3][...] = (acc[...] * jnp.concatenate([tab] * (tn // tab.shape[1]), axis=1)).astype(out_dtype)
            else:
                refs[2][...] = acc[...].astype(out_dtype)

    ins, in_specs = [a, b], [a_spec, b_spec]
    if rowtab is not None:
        assert tn % rowtab.shape[1] == 0 and not gated and add is None
        ins.append(rowtab)
        in_specs.append(pl.BlockSpec((tm, rowtab.shape[1]), lambda i, j, k: (i, 0)))
    if gated:
        ins += [resid, gate]
        in_specs += [o_spec, g_spec]
        out_shape = (jax.ShapeDtypeStruct((m, n), F32), jax.ShapeDtypeStruct((m, n), F32))
        out_specs = (o_spec, o_spec)
    else:
        if add is not None:
            ins.append(add)
            in_specs.append(o_spec)
        out_shape = jax.ShapeDtypeStruct((m, n), out_dtype)
        out_specs = o_spec
    return pl.pallas_call(
        body, name=name, grid=(m // tm, n // tn, nk), in_specs=in_specs, out_specs=out_specs, out_shape=out_shape,
        scratch_shapes=[pltpu.VMEM((tm, tn), F32)],
        compiler_params=_params(("parallel", "parallel", "arbitrary")),
    )(*ins)


def _rowwise(name, fn, tiled, bcast, outs, sums=(), tr=512):
    tiled = [t if isinstance(t, tuple) else (t, t.shape[1], 0) for t in tiled]
    s = tiled[0][0].shape[0]
    tr = min(tr, s)
    assert s % tr == 0
    n_t, n_b, n_o = len(tiled), len(bcast), len(outs)

    def body(*refs):
        i = pl.program_id(0)
        vals = [r[...] for r in refs[:n_t + n_b]]
        o_vals, s_vals = fn(*vals)
        for r, v in zip(refs[n_t + n_b:n_t + n_b + n_o], o_vals):
            r[...] = v.astype(r.dtype)
        s_refs = refs[n_t + n_b + n_o:]

        @pl.when(i == 0)
        def _():
            for r in s_refs:
                r[...] = jnp.zeros_like(r)

        for r, v in zip(s_refs, s_vals):
            r[...] += v

    in_specs = [pl.BlockSpec((tr, n), functools.partial(lambda cb, i: (i, cb), cb)) for (_, n, cb) in tiled]
    in_specs += [pl.BlockSpec(b.shape, functools.partial(lambda nd, i: (0,) * nd, b.ndim)) for b in bcast]
    out_specs = [pl.BlockSpec((tr, n), lambda i: (i, 0)) for (n, _) in outs]
    out_specs += [pl.BlockSpec((1, n), lambda i: (0, 0)) for n in sums]
    out_shape = [jax.ShapeDtypeStruct((s, n), dt) for (n, dt) in outs]
    out_shape += [jax.ShapeDtypeStruct((1, n), F32) for n in sums]
    res = pl.pallas_call(
        body, name=name, grid=(s // tr,), in_specs=in_specs, out_specs=tuple(out_specs), out_shape=tuple(out_shape),
        compiler_params=_params(("arbitrary",)),
    )(*[t[0] for t in tiled], *bcast)
    return res


def _colsum(v):
    return jnp.sum(v, axis=0, keepdims=True)


def _rms_fwd(name, x, g, scale=None, shift=None, out_dtype=BF16, ncols=None):
    mod = scale is not None

    def fn(xv, gv, *ss):
        y = xv * lax.rsqrt(jnp.mean(xv * xv, axis=-1, keepdims=True) + EPS) * gv
        if mod:
            y = y * (1.0 + ss[0]) + ss[1]
        return (y,), ()

    n = ncols or x.shape[1]
    return _rowwise(name, fn, [(x, n, 0)], [g] + ([scale, shift] if mod else []), [(n, out_dtype)])[0]


def _rms_bwd(name, x, g, dh, scale=None, dx_in=None, ncols=None, out_dtype=F32):
    mod = scale is not None
    has_in = dx_in is not None

    def fn(*vals):
        xv, dhv = vals[0], vals[1].astype(F32)
        rest = list(vals[2:])
        dxi = rest.pop(0) if has_in else None
        gv = rest.pop(0)
        rstd = lax.rsqrt(jnp.mean(xv * xv, axis=-1, keepdims=True) + EPS)
        xhat = xv * rstd
        sums = []
        if mod:
            sc = rest.pop(0)
            dyn = dhv * (1.0 + sc)
            dshift, dscale = _colsum(dhv), _colsum(dhv * (xhat * gv))
        else:
            dyn = dhv
        dg = _colsum(dyn * xhat)
        dxhat = dyn * gv
        dx = rstd * (dxhat - xhat * jnp.mean(dxhat * xhat, axis=-1, keepdims=True))
        if has_in:
            dx = dx + dxi
        sums = [dg] + ([dshift, dscale] if mod else [])
        return (dx,), sums

    n = ncols or x.shape[1]
    tiled = [(x, n, 0), dh] + ([dx_in] if has_in else [])
    return _rowwise(name, fn, tiled, [g] + ([scale] if mod else []), [(n, out_dtype)], [n] * (3 if mod else 1))


def _gate_bwd(name, dxn, y, g):
    def fn(dv, yv, gv):
        return (gv * dv,), (_colsum(dv * yv),)

    n = dxn.shape[1]
    return _rowwise(name, fn, [dxn, y], [g], [(n, BF16)], [n])


def _loss_head(name, x, g, target):
    n = x.shape[1]

    def fn(xv, tv, gv):
        rstd = lax.rsqrt(jnp.mean(xv * xv, axis=-1, keepdims=True) + EPS)
        xhat = xv * rstd
        err = xhat * gv - tv
        loss = 0.5 * jnp.sum(jnp.sum(err * err, axis=-1, keepdims=True) / n, axis=0, keepdims=True)
        dy = err / n
        dg = _colsum(dy * xhat)
        dxhat = dy * gv
        dx = rstd * (dxhat - xhat * jnp.mean(dxhat * xhat, axis=-1, keepdims=True))
        return (dx,), (dg, jnp.broadcast_to(loss, (1, LANES)))

    return _rowwise(name, fn, [x, target], [g], [(n, F32)], [n, LANES])


def _krope_fwd(name, kv_ext, tabk):
    def fn(xv, tv):
        t = xv * tv
        return (t + pltpu.roll(t, 64, 1),), ()

    return _rowwise(name, fn, [(kv_ext, LANES, 2), tabk], [], [(LANES, BF16)])[0]


def _krope_bwd(name, dkd, tabk):
    def fn(dv, tv):
        return ((dv + pltpu.roll(dv, 64, 1)) * tv,), ()

    return _rowwise(name, fn, [dkd, tabk], [], [(LANES, F32)])[0]


def _adamw(name, w, g, m, v):
    def fn(wv, gv, mv, vv):
        m2 = ADAM_B1 * mv + (1.0 - ADAM_B1) * gv
        v2 = ADAM_B2 * vv + (1.0 - ADAM_B2) * (gv * gv)
        m_hat = m2 / (1.0 - ADAM_B1 ** ADAM_STEP)
        v_hat = v2 / (1.0 - ADAM_B2 ** ADAM_STEP)
        delta = -ADAM_LR * (m_hat / (jnp.sqrt(v_hat) + ADAM_EPS) + ADAM_WD * wv)
        return (delta, m2, v2), ()

    r, c = w.shape
    tr = r
    for cand in (512, 256, 128, 64, 32, 16, 8):
        if r % cand == 0 and r > cand:
            tr = cand
            break
    return _rowwise(name, fn, [w, g, m, v], [], [(c, F32)] * 3, tr=tr)


def _sum8(name, parts):
    _, r, c = parts.shape
    tr = r
    for cand in (2048, 1024, 512, 256, 128, 64, 32, 16):
        if r % cand == 0 and r > cand and cand * c <= 256 * 1024:
            tr = cand
            break

    def body(p_ref, o_ref):
        acc = p_ref[0].astype(F32)
        for k in range(1, N_DEV):
            acc = acc + p_ref[k].astype(F32)
        o_ref[...] = acc

    return pl.pallas_call(
        body, name=name, grid=(r // tr,), in_specs=[pl.BlockSpec((N_DEV, tr, c), lambda i: (0, i, 0))],
        out_specs=pl.BlockSpec((tr, c), lambda i: (i, 0)), out_shape=jax.ShapeDtypeStruct((r, c), F32),
        compiler_params=_params(("parallel",)),
    )(parts)


def _mods_fwd(name, c_all, w, b):
    depth, d, n = w.shape

    def body(c_ref, w_ref, b_ref, o_ref):
        cv = c_ref[...]
        sc = (cv * (1.0 / (1.0 + jnp.exp(-cv)))).astype(BF16)
        o_ref[0] = _dot(sc, w_ref[0].astype(BF16), NN) + b_ref[0]

    return pl.pallas_call(
        body, name=name, grid=(depth,),
        in_specs=[pl.BlockSpec(c_all.shape, lambda l: (0, 0)), pl.BlockSpec((1, d, n), lambda l: (l, 0, 0)),
                  pl.BlockSpec((1, 1, n), lambda l: (l, 0, 0))],
        out_specs=pl.BlockSpec((1, c_all.shape[0], n), lambda l: (l, 0, 0)),
        out_shape=jax.ShapeDtypeStruct((depth, c_all.shape[0], n), F32),
        compiler_params=_params(("parallel",)),
    )(c_all, w, b.reshape(depth, 1, n))


def _mods_bwd(name, c_all, dm):
    depth, rows, n = dm.shape
    d = c_all.shape[1]

    def body(c_ref, dm_ref, o_ref):
        cv = c_ref[...]
        sc = (cv * (1.0 / (1.0 + jnp.exp(-cv)))).astype(BF16)
        o_ref[0] = _dot(sc, dm_ref[0].astype(BF16), TN)

    return pl.pallas_call(
        body, name=name, grid=(depth,),
        in_specs=[pl.BlockSpec(c_all.shape, lambda l: (0, 0)), pl.BlockSpec((1, rows, n), lambda l: (l, 0, 0))],
        out_specs=pl.BlockSpec((1, d, n), lambda l: (l, 0, 0)),
        out_shape=jax.ShapeDtypeStruct((depth, d, n), F32),
        compiler_params=_params(("parallel",)),
    )(c_all, dm)


POOL_TILE = 256


def _split_dot(band, val):
    hi = val.astype(BF16)
    lo = (val - hi.astype(F32)).astype(BF16)
    return _dot(band, hi, NN) + _dot(band, lo, NN)


def _pool_fwd(name, h1, x, pw, pb, ps, g1):
    s, d = h1.shape
    t = POOL_TILE

    def body(hc_ref, hp_ref, x_ref, pw_ref, pb_ref, ps_ref, g_ref, xo_ref, zb_ref, pooled_ref):
        i = pl.program_id(0)
        r = lax.broadcasted_iota(jnp.int32, (t, t), 0)
        j = lax.broadcasted_iota(jnp.int32, (t, t), 1)
        pos = (i * t + lax.broadcasted_iota(jnp.int32, (t, 1), 0) + 1).astype(F32)
        has_prev = (i > 0).astype(F32)
        for grp, w in enumerate(POOL_WINDOWS):
            cs = slice(grp * POOL_GROUP, (grp + 1) * POOL_GROUP)
            hc = hc_ref[:, cs]
            band_cur = ((r - j >= 0) & (r - j < w)).astype(BF16)
            band_prev = (r + t - j < w).astype(BF16)
            ssum = _split_dot(band_cur, hc) + has_prev * _split_dot(band_prev, hp_ref[:, cs])
            pooled = (ssum / jnp.minimum(pos, float(w)) - hc).astype(BF16)
            zb = _dot(pooled, pw_ref[grp], NN) + pb_ref[:, cs]
            xo_ref[:, cs] = x_ref[:, cs] + g_ref[:, cs] * (zb * ps_ref[:, cs])
            zb_ref[:, cs] = zb
            pooled_ref[:, cs] = pooled

    row = pl.BlockSpec((t, d), lambda i: (i, 0))
    vec = pl.BlockSpec((1, d), lambda i: (0, 0))
    return pl.pallas_call(
        body, name=name, grid=(s // t,),
        in_specs=[row, pl.BlockSpec((t, d), lambda i: (jnp.maximum(i - 1, 0), 0)), row,
                  pl.BlockSpec(pw.shape, lambda i: (0, 0, 0)), vec, vec, vec],
        out_specs=(row, row, row),
        out_shape=(jax.ShapeDtypeStruct((s, d), F32), jax.ShapeDtypeStruct((s, d), F32), jax.ShapeDtypeStruct((s, d), BF16)),
        compiler_params=_params(("parallel",)),
    )(h1, h1, x, pw, pb, ps, g1)


def _pool_bwd(name, dxn, zb, pooled, pw, ps, g1):
    s, d = dxn.shape
    t = POOL_TILE
    nt = s // t

    def body(dc_ref, dn_ref, zb_ref, pooled_ref, pw_ref, ps_ref, g_ref, dh_ref, dpw_ref, dpb_ref, dps_ref, dg_ref):
        i = pl.program_id(0)

        @pl.when(i == 0)
        def _():
            dpw_ref[...] = jnp.zeros_like(dpw_ref)
            dpb_ref[...] = jnp.zeros_like(dpb_ref)
            dps_ref[...] = jnp.zeros_like(dps_ref)
            dg_ref[...] = jnp.zeros_like(dg_ref)

        jj = lax.broadcasted_iota(jnp.int32, (t, t), 0)
        rr = lax.broadcasted_iota(jnp.int32, (t, t), 1)
        pos = (i * t + lax.broadcasted_iota(jnp.int32, (t, 1), 0) + 1).astype(F32)
        has_next = (i < nt - 1).astype(F32)
        for grp, w in enumerate(POOL_WINDOWS):
            cs = slice(grp * POOL_GROUP, (grp + 1) * POOL_GROUP)
            gv, psv, zbv, dxc = g_ref[:, cs], ps_ref[:, cs], zb_ref[:, cs], dc_ref[:, cs]
            dg_ref[:, cs] += _colsum(dxc * (zbv * psv))
            dy = gv * dxc
            dps_ref[:, cs] += _colsum(dy * zbv)
            dz = dy * psv
            dpb_ref[:, cs] += _colsum(dz)
            dzb = dz.astype(BF16)
            dpw_ref[grp] += _dot(pooled_ref[:, cs], dzb, TN)
            dp = _dot(dzb, pw_ref[grp], NT)
            dzn = (gv * dn_ref[:, cs] * psv).astype(BF16)
            dpn = _dot(dzn, pw_ref[grp], NT) * (has_next / float(w))
            band_cur = ((rr - jj >= 0) & (rr - jj < w)).astype(BF16)
            band_next = (rr + t - jj < w).astype(BF16)
            dh_ref[:, cs] = _split_dot(band_cur, dp / jnp.minimum(pos, float(w))) + _split_dot(band_next, dpn) - dp

    row = pl.BlockSpec((t, d), lambda i: (i, 0))
    vec = pl.BlockSpec((1, d), lambda i: (0, 0))
    wspec = pl.BlockSpec(pw.shape, lambda i: (0, 0, 0))
    return pl.pallas_call(
        body, name=name, grid=(nt,),
        in_specs=[row, pl.BlockSpec((t, d), lambda i: (jnp.minimum(i + 1, nt - 1), 0)), row, row, wspec, vec, vec],
        out_specs=(row, wspec, vec, vec, vec),
        out_shape=(jax.ShapeDtypeStruct((s, d), F32), jax.ShapeDtypeStruct(pw.shape, F32),
                   jax.ShapeDtypeStruct((1, d), F32), jax.ShapeDtypeStruct((1, d), F32), jax.ShapeDtypeStruct((1, d), F32)),
        compiler_params=_params(("arbitrary",)),
    )(dxn, dxn, zb, pooled, pw, ps, g1)


GLU_TILE = 256
HALO = 16
INV_SQRT2 = 0.7071067811865476
INV_SQRT_2PI = 0.3989422804014327


def _gelu(xv):
    return 0.5 * xv * (1.0 + lax.erf(xv * INV_SQRT2))


def _glu_fwd(name, ua, uv, cw, cb):
    s, f = ua.shape
    t, tf = GLU_TILE, _tile(f, 1408)

    def body(a_ref, ah_ref, v_ref, cw_ref, cb_ref, o_ref):
        i = pl.program_id(1)
        has_prev = (i > 0).astype(F32)
        ext = jnp.concatenate([ah_ref[...].astype(F32) * has_prev, a_ref[...].astype(F32)], axis=0)
        e1 = pltpu.roll(ext, 1, 0)[HALO:]
        e2 = pltpu.roll(ext, 2, 0)[HALO:]
        pre = e2 * cw_ref[0:1, :] + e1 * cw_ref[1:2, :] + ext[HALO:] * cw_ref[2:3, :] + cb_ref[...]
        o_ref[...] = (_gelu(pre) * v_ref[...].astype(F32)).astype(o_ref.dtype)

    blk = pl.BlockSpec((t, tf), lambda j, i: (i, j))
    halo = pl.BlockSpec((HALO, tf), lambda j, i: (jnp.maximum(i * (t // HALO) - 1, 0), j))
    return pl.pallas_call(
        body, name=name, grid=(f // tf, s // t),
        in_specs=[blk, halo, blk, pl.BlockSpec((3, tf), lambda j, i: (0, j)), pl.BlockSpec((1, tf), lambda j, i: (0, j))],
        out_specs=blk, out_shape=jax.ShapeDtypeStruct((s, f), BF16),
        compiler_params=_params(("parallel", "parallel")),
    )(ua, ua, uv, cw, cb)


def _glu_bwd(name, ua, uv, dgl, cw, cb):
    s, f = ua.shape
    t, tf = GLU_TILE, _tile(f, 1408)
    nt = s // t
    te = t + HALO

    def body(a_ref, ah_ref, an_ref, v_ref, vn_ref, d_ref, dn_ref, cw_ref, cb_ref, da_ref, dv_ref, dcw_ref, dcb_ref):
        i = pl.program_id(1)

        @pl.when(i == 0)
        def _():
            dcw_ref[...] = jnp.zeros_like(dcw_ref)
            dcb_ref[...] = jnp.zeros_like(dcb_ref)

        has_prev = (i > 0).astype(F32)
        has_next = (i < nt - 1).astype(F32)
        ext = jnp.concatenate([ah_ref[...].astype(F32) * has_prev, a_ref[...].astype(F32), an_ref[...].astype(F32)], axis=0)
        e0 = ext[HALO:]
        e1 = pltpu.roll(ext, 1, 0)[HALO:]
        e2 = pltpu.roll(ext, 2, 0)[HALO:]
        c0, c1, c2 = cw_ref[0:1, :], cw_ref[1:2, :], cw_ref[2:3, :]
        pre = e2 * c0 + e1 * c1 + e0 * c2 + cb_ref[...]
        vx = jnp.concatenate([v_ref[...].astype(F32), vn_ref[...].astype(F32)], axis=0)
        dx = jnp.concatenate([d_ref[...].astype(F32), dn_ref[...].astype(F32) * has_next], axis=0)
        cdf = 0.5 * (1.0 + lax.erf(pre * INV_SQRT2))
        dpre = dx * vx * (cdf + pre * (INV_SQRT_2PI * jnp.exp(-0.5 * pre * pre)))
        up1 = pltpu.roll(dpre, te - 1, 0)
        up2 = pltpu.roll(dpre, te - 2, 0)
        da_ref[...] = (dpre * c2 + up1 * c1 + up2 * c0)[:t].astype(da_ref.dtype)
        dv_ref[...] = (dx * (pre * cdf))[:t].astype(dv_ref.dtype)
        dpt = dpre[:t]
        dcb_ref[...] += _colsum(dpt)
        dcw_ref[0:1, :] += _colsum(e2[:t] * dpt)
        dcw_ref[1:2, :] += _colsum(e1[:t] * dpt)
        dcw_ref[2:3, :] += _colsum(e0[:t] * dpt)

    blk = pl.BlockSpec((t, tf), lambda j, i: (i, j))
    prev = pl.BlockSpec((HALO, tf), lambda j, i: (jnp.maximum(i * (t // HALO) - 1, 0), j))
    nxt = pl.BlockSpec((HALO, tf), lambda j, i: (jnp.minimum((i + 1) * (t // HALO), s // HALO - 1), j))
    w3 = pl.BlockSpec((3, tf), lambda j, i: (0, j))
    w1 = pl.BlockSpec((1, tf), lambda j, i: (0, j))
    return pl.pallas_call(
        body, name=name, grid=(f // tf, nt),
        in_specs=[blk, prev, nxt, blk, nxt, blk, nxt, w3, w1],
        out_specs=(blk, blk, w3, w1),
        out_shape=(jax.ShapeDtypeStruct((s, f), BF16), jax.ShapeDtypeStruct((s, f), BF16),
                   jax.ShapeDtypeStruct((3, f), F32), jax.ShapeDtypeStruct((1, f), F32)),
        compiler_params=_params(("parallel", "arbitrary")),
    )(ua, ua, ua, uv, uv, dgl, dgl, cw, cb)


ATT_TILE = 512
ATT_ROWS = 256
LOG2E = 1.4426950408889634
LN2 = 0.6931471805599453


def _head_blocks_t(a, width):
    s = a.shape[0]
    t = min(ATT_TILE, s)
    return a.reshape(s // t, t, N_HEADS, width).transpose(2, 0, 3, 1)


def _head_rows(a):
    s = a.shape[0]
    t = min(ATT_TILE, s)
    r = a.reshape(s // t, t, N_HEADS, LANES)[..., 0].transpose(2, 0, 1)
    return jnp.broadcast_to(r[:, :, None, :], (N_HEADS, s // t, 8, t))


def _causal_mask(sv, q0, k0):
    row = q0 + lax.broadcasted_iota(jnp.int32, sv.shape, 0)
    col = k0 + lax.broadcasted_iota(jnp.int32, sv.shape, 1)
    return jnp.where(col <= row, sv, NEG_BIG)


def _attn_fwd(name, q_rot, kt4, v_ext):
    s = q_rot.shape[0]
    t = min(ATT_TILE, s)
    nq = s // t

    rq = min(ATT_ROWS, t)

    def body(q_ref, kt_ref, v_ref, o_ref, lse_ref, acc_ref, m_ref):
        qi = pl.program_id(1)
        acc_ref[...] = jnp.zeros_like(acc_ref)
        m_ref[...] = jnp.full_like(m_ref, NEG_BIG)

        def step(j, masked):
            v_blk = v_ref[pl.ds(pl.multiple_of(j * t, t), t), :]
            for r in range(t // rq):
                rs = pl.ds(r * rq, rq)
                sv = _dot(q_ref[rs, :], kt_ref[0, j], NN)
                if masked:
                    sv = _causal_mask(sv, r * rq, 0)
                m_prev = m_ref[rs, :]
                m_new = jnp.maximum(m_prev, jnp.max(sv, axis=-1, keepdims=True))
                p = jnp.exp2(sv - m_new).astype(BF16)
                acc_ref[rs, :] = jnp.exp2(m_prev - m_new) * acc_ref[rs, :] + _dot(p, v_blk, NN)
                m_ref[rs, :] = m_new

        def full_step(j, carry):
            step(j, False)
            return carry

        lax.fori_loop(0, qi, full_step, 0)
        step(qi, True)
        l = acc_ref[:, V_HEAD:V_HEAD + 1]
        o_ref[...] = (acc_ref[:, :V_HEAD] / l).astype(o_ref.dtype)
        lse_ref[...] = jnp.broadcast_to(m_ref[...] + jnp.log(l) * LOG2E, lse_ref.shape)

    head_q = pl.BlockSpec((t, Q_EXT), lambda h, i: (i, h))
    head_o = pl.BlockSpec((t, V_HEAD), lambda h, i: (i, h))
    return pl.pallas_call(
        body, name=name, grid=(N_HEADS, nq),
        in_specs=[head_q, pl.BlockSpec((1, nq, Q_EXT, t), lambda h, i: (h, 0, 0, 0)), pl.BlockSpec((s, Q_EXT), lambda h, i: (0, h))],
        out_specs=(head_o, head_o),
        out_shape=(jax.ShapeDtypeStruct((s, N_HEADS * V_HEAD), BF16), jax.ShapeDtypeStruct((s, N_HEADS * LANES), F32)),
        scratch_shapes=[pltpu.VMEM((t, Q_EXT), F32), pltpu.VMEM((t, 1), F32)],
        compiler_params=_params(("parallel", "parallel")),
    )(q_rot, kt4, v_ext)


def _attn_dq(name, q_rot, tabq, kt4, kfull, vt4, o, lse, do):
    s = q_rot.shape[0]
    t = min(ATT_TILE, s)
    nq = s // t

    def body(q_ref, tab_ref, kt_ref, k_ref, vt_ref, o_ref, lse_ref, do_ref, dq_ref, delta_ref, acc_ref):
        qi = pl.program_id(1)
        q = q_ref[...]
        dov = do_ref[...]
        delta = jnp.sum(dov.astype(F32) * o_ref[...].astype(F32), axis=-1, keepdims=True)
        lse = lse_ref[:, 0:1]
        acc_ref[...] = jnp.zeros_like(acc_ref)

        def step(j, masked):
            sv = _dot(q, kt_ref[0, j], NN)
            if masked:
                sv = _causal_mask(sv, qi * t, j * t)
            p = jnp.exp2(sv - lse)
            dp = _dot(dov, vt_ref[0, j], NN)
            ds = (p * (dp - delta)).astype(BF16)
            acc_ref[...] += _dot(ds, k_ref[pl.ds(pl.multiple_of(j * t, t), t), :], NN)

        def full_step(j, carry):
            step(j, False)
            return carry

        lax.fori_loop(0, qi, full_step, 0)
        step(qi, True)
        dq_ref[...] = (acc_ref[...] * (tab_ref[...] * LN2)).astype(dq_ref.dtype)
        delta_ref[...] = jnp.broadcast_to(delta, delta_ref.shape)

    head_q = pl.BlockSpec((t, Q_EXT), lambda h, i: (i, h))
    head_o = pl.BlockSpec((t, V_HEAD), lambda h, i: (i, h))
    return pl.pallas_call(
        body, name=name, grid=(N_HEADS, nq),
        in_specs=[head_q, pl.BlockSpec((t, Q_EXT), lambda h, i: (i, 0)), pl.BlockSpec((1, nq, Q_EXT, t), lambda h, i: (h, 0, 0, 0)),
                  pl.BlockSpec((s, Q_EXT), lambda h, i: (0, h)), pl.BlockSpec((1, nq, V_HEAD, t), lambda h, i: (h, 0, 0, 0)),
                  head_o, head_o, head_o],
        out_specs=(head_q, head_o),
        out_shape=(jax.ShapeDtypeStruct((s, N_HEADS * Q_EXT), BF16), jax.ShapeDtypeStruct((s, N_HEADS * LANES), F32)),
        scratch_shapes=[pltpu.VMEM((t, Q_EXT), F32)],
        compiler_params=_params(("parallel", "parallel")),
    )(q_rot, tabq, kt4, kfull, vt4, o, lse, do)


def _attn_dkv(name, kfull, v, qt4, q_rot, dot4, do, lse_row, delta_row, acc_in=None):
    s = kfull.shape[0]
    t = min(ATT_TILE, s)
    nq = s // t
    has_in = acc_in is not None

    def body(*refs):
        k_ref, v_ref, qt_ref, q_ref, dot_ref, do_ref, lse_ref, delta_ref = refs[:8]
        dkn_ref, dkd_ref, dv_ref, acck_ref, accv_ref = refs[-5:]
        kj, h = pl.program_id(0), pl.program_id(1)
        k_blk, v_blk = k_ref[...], v_ref[...]
        acck_ref[...] = jnp.zeros_like(acck_ref)
        accv_ref[...] = jnp.zeros_like(accv_ref)

        def step(i, masked):
            qs = pl.ds(pl.multiple_of(i * t, t), t)
            st = _dot(k_blk, qt_ref[0, i], NN)
            if masked:
                krow = lax.broadcasted_iota(jnp.int32, st.shape, 0)
                qcol = lax.broadcasted_iota(jnp.int32, st.shape, 1)
                st = jnp.where(krow <= qcol, st, NEG_BIG)
            pt = jnp.exp2(st - lse_ref[0, i, 0:1, :])
            accv_ref[...] += _dot(pt.astype(BF16), do_ref[qs, :], NN)
            dpt = _dot(v_blk, dot_ref[0, i], NN)
            dst = (pt * (dpt - delta_ref[0, i, 0:1, :])).astype(BF16)
            acck_ref[...] += _dot(dst, q_ref[qs, :], NN)

        def full_step(i, carry):
            step(i, False)
            return carry

        step(kj, True)
        lax.fori_loop(kj + 1, nq, full_step, 0)
        dk = acck_ref[...] * LN2
        dkn, dkd = dk[:, :QK_NOPE], dk[:, QK_NOPE:]
        if has_in:
            dkn = dkn + refs[8][...]
            dv_ref[...] = accv_ref[...] + refs[10][...]
        else:
            dv_ref[...] = accv_ref[...]
        dkn_ref[...] = dkn

        @pl.when(h == 0)
        def _():
            if has_in:
                dkd_ref[...] = dkd + refs[9][...]
            else:
                dkd_ref[...] = dkd

        @pl.when(h > 0)
        def _():
            dkd_ref[...] += dkd

    kblk = pl.BlockSpec((t, LANES), lambda j, h: (j, h))
    kdblk = pl.BlockSpec((t, LANES), lambda j, h: (j, 0))
    col = pl.BlockSpec((s, LANES), lambda j, h: (0, h))
    stat = pl.BlockSpec((1, nq, 8, t), lambda j, h: (h, 0, 0, 0))
    ins = [kfull, v, qt4, q_rot, dot4, do, lse_row, delta_row]
    in_specs = [pl.BlockSpec((t, Q_EXT), lambda j, h: (j, h)), kblk, pl.BlockSpec((1, nq, Q_EXT, t), lambda j, h: (h, 0, 0, 0)),
                pl.BlockSpec((s, Q_EXT), lambda j, h: (0, h)), pl.BlockSpec((1, nq, V_HEAD, t), lambda j, h: (h, 0, 0, 0)), col, stat, stat]
    if has_in:
        ins += list(acc_in)
        in_specs += [kblk, kdblk, kblk]
    return pl.pallas_call(
        body, name=name, grid=(nq, N_HEADS), in_specs=in_specs, out_specs=(kblk, kdblk, kblk),
        out_shape=(jax.ShapeDtypeStruct((s, N_HEADS * LANES), F32), jax.ShapeDtypeStruct((s, LANES), F32),
                   jax.ShapeDtypeStruct((s, N_HEADS * LANES), F32)),
        scratch_shapes=[pltpu.VMEM((t, Q_EXT), F32), pltpu.VMEM((t, LANES), F32)],
        compiler_params=_params(("parallel", "arbitrary")),
    )(*ins)


def _swap_halves(w):
    half = w.shape[-1] // 2
    return jnp.concatenate([-w[..., half:], w[..., :half]], axis=-1)


def _unswap_halves(g):
    half = g.shape[-1] // 2
    return jnp.concatenate([g[..., half:], -g[..., :half]], axis=-1)


def _extend_w_uq(w):
    r = w.reshape(Q_RANK, N_HEADS, QK_HEAD)
    rope = r[..., QK_NOPE:]
    return jnp.concatenate([r[..., :QK_NOPE], rope, _swap_halves(rope)], axis=-1).reshape(Q_RANK, N_HEADS * Q_EXT)


def _fold_w_uq_grad(g):
    r = g.reshape(Q_RANK, N_HEADS, Q_EXT)
    rope = r[..., QK_NOPE:QK_HEAD] + _unswap_halves(r[..., QK_HEAD:])
    return jnp.concatenate([r[..., :QK_NOPE], rope], axis=-1).reshape(Q_RANK, N_HEADS * QK_HEAD)


def _extend_w_dkv(w):
    return jnp.concatenate([w, _swap_halves(w[:, KV_RANK:])], axis=-1)


def _fold_w_dkv_grad(g):
    rope = g[:, KV_RANK:KV_RANK + QK_ROPE] + _unswap_halves(g[:, KV_RANK + QK_ROPE:])
    return jnp.concatenate([g[:, :KV_RANK], rope], axis=-1)


def _rope_tables(positions):
    inv = 1.0 / (ROPE_THETA ** (jnp.arange(0, QK_ROPE, 2, dtype=F32) / QK_ROPE))
    ang = positions.astype(F32)[:, None] * inv
    cos, sin = jnp.cos(ang), jnp.sin(ang)
    tabk = jnp.concatenate([cos, cos, sin, sin], axis=-1)
    scale = QK_HEAD ** -0.5 * LOG2E
    tabq = jnp.concatenate([jnp.full((positions.shape[0], QK_NOPE), scale, F32), tabk * scale], axis=-1)
    return tabq, tabk


def _forward_backward(x, target, mods, tabq, tabk, final_g, fetch, push):
    row = lambda vec: vec.reshape(1, -1)
    mod = [[row(mods[l, k * D_MODEL:(k + 1) * D_MODEL]) for k in range(N_MOD)] for l in range(DEPTH)]
    saved, weights = [], []
    kv = None
    for l in range(DEPTH):
        w, tok = fetch(l, x)
        weights.append(w)
        sh1, sc1, g1, sh2, sc2, g2 = mod[l]
        sh1 = sh1 + tok
        if l == N_A_LAYERS:
            kvn = _rms_fwd("kvin_fwd", x, row(w["kv_in_g"]))
            kv_ext = _mm("dkv_fwd", kvn, w["w_dkv_ext"], out_dtype=F32)
            ckv = _rms_fwd("ckv_fwd", kv_ext, row(w["ckv_norm_g"]), ncols=KV_RANK)
            kd = _krope_fwd("krope_fwd", kv_ext, tabk)
            kn, v = _mm("uk_fwd", ckv, w["w_uk"]), _mm("uv_fwd", ckv, w["w_uv"])
            heads = lambda a: [a[:, h * LANES:(h + 1) * LANES] for h in range(N_HEADS)]
            kfull = jnp.concatenate([part for kh in heads(kn) for part in (kh, kd)], axis=-1)
            v_ext = jnp.concatenate([part for vh in heads(v) for part in (vh, jnp.ones_like(vh))], axis=-1)
            kv = dict(x=x, kvn=kvn, kv_ext=kv_ext, ckv=ckv, v=v, kfull=kfull, v_ext=v_ext,
                      kt4=_head_blocks_t(kfull, Q_EXT), vt4=_head_blocks_t(v, V_HEAD))
        x_in = x
        if l < N_A_LAYERS:
            h1 = _rms_fwd(f"norm1_fwd_{l}", x, row(w["norm1_g"]), sc1, sh1, out_dtype=F32)
            x_mid, zb, pooled = _pool_fwd(f"pool_fwd_{l}", h1, x, w["pool_w"], row(w["pool_b"]), row(w["pool_scale"]), g1)
            mix = (zb, pooled)
        else:
            h1 = _rms_fwd(f"norm1_fwd_{l}", x, row(w["norm1_g"]), sc1, sh1)
            cq_pre = _mm(f"dq_fwd_{l}", h1, w["w_dq"], out_dtype=F32)
            cq = _rms_fwd(f"qnorm_fwd_{l}", cq_pre, row(w["q_norm_g"]))
            q_rot = _mm(f"uq_fwd_{l}", cq, w["w_uq_ext"], rowtab=tabq)
            o, lse = _attn_fwd(f"attn_fwd_{l}", q_rot, kv["kt4"], kv["v_ext"])
            y, x_mid = _mm(f"wo_fwd_{l}", o, w["w_o"], resid=x, gate=g1)
            mix = (h1, cq_pre, cq, q_rot, o, lse, y)
        h2 = _rms_fwd(f"norm2_fwd_{l}", x_mid, row(w["norm2_g"]), sc2, sh2)
        ua = _mm(f"up_a_fwd_{l}", h2, w["w_up_a"])
        uv = _mm(f"up_v_fwd_{l}", h2, w["w_up_v"])
        gl = _glu_fwd(f"glu_fwd_{l}", ua, uv, w["conv_w"], row(w["conv_b"]))
        y2, x = _mm(f"down_fwd_{l}", gl, w["w_down"], resid=x_mid, gate=g2)
        saved.append((x_in, x_mid, h2, ua, uv, gl, y2, mix))

    dx, dfinal_g, loss = _loss_head("loss_head", x, row(final_g), target)
    g = {"final_g": dfinal_g.reshape(-1)}
    per_layer = {k: [None] * DEPTH for k in ("norm1_g", "norm2_g", "conv_w", "conv_b")}
    per_a = {k: [None] * N_A_LAYERS for k in ("pool_b", "pool_scale")}
    per_b = {k: [None] * N_B_LAYERS for k in ("q_norm_g",)}
    dmods = [None] * DEPTH
    dkv = None
    tok = 0.0
    for l in reversed(range(DEPTH)):
        w, big = weights[l], {}
        sh1, sc1, g1, sh2, sc2, g2 = mod[l]
        g2 = g2 + tok
        x_in, x_mid, h2, ua, uv, gl, y2, mix = saved[l]
        dy2, dg2 = _gate_bwd(f"gate2_bwd_{l}", dx, y2, g2)
        dgl = _mm(f"down_bwd_{l}", dy2, w["w_down"], mode="nt")
        big["w_down"] = _mm(f"down_wgrad_{l}", gl, dy2, mode="tn", out_dtype=F32, tm_cap=1408)
        da, dv_, dcw, dcb = _glu_bwd(f"glu_bwd_{l}", ua, uv, dgl, w["conv_w"], row(w["conv_b"]))
        dh2 = _mm(f"up_a_bwd_{l}", da, w["w_up_a"], mode="nt", out_dtype=F32)
        dh2 = _mm(f"up_v_bwd_{l}", dv_, w["w_up_v"], mode="nt", out_dtype=F32, add=dh2)
        big["w_up_a"] = _mm(f"up_a_wgrad_{l}", h2, da, mode="tn", out_dtype=F32)
        big["w_up_v"] = _mm(f"up_v_wgrad_{l}", h2, dv_, mode="tn", out_dtype=F32)
        per_layer["conv_w"][l], per_layer["conv_b"][l] = dcw, dcb.reshape(-1)
        dx_mid, dn2, dsh2, dsc2 = _rms_bwd(f"norm2_bwd_{l}", x_mid, row(w["norm2_g"]), dh2, sc2, dx_in=dx)
        per_layer["norm2_g"][l] = dn2.reshape(-1)
        if l < N_A_LAYERS:
            zb, pooled = mix
            dh1, dpw, dpb, dps, dg1 = _pool_bwd(f"pool_bwd_{l}", dx_mid, zb, pooled, w["pool_w"], row(w["pool_scale"]), g1)
            big["pool_w"] = dpw
            per_a["pool_b"][l], per_a["pool_scale"][l] = dpb.reshape(-1), dps.reshape(-1)
        else:
            j = l - N_A_LAYERS
            h1, cq_pre, cq, q_rot, o, lse, y = mix
            dy, dg1 = _gate_bwd(f"gate1_bwd_{l}", dx_mid, y, g1)
            do = _mm(f"wo_bwd_{l}", dy, w["w_o"], mode="nt")
            big["w_o"] = _mm(f"wo_wgrad_{l}", o, dy, mode="tn", out_dtype=F32)
            dq_ext, delta = _attn_dq(f"attn_dq_{l}", q_rot, tabq, kv["kt4"], kv["kfull"], kv["vt4"], o, lse, do)
            dkv = _attn_dkv(f"attn_dkv_{l}", kv["kfull"], kv["v"], _head_blocks_t(q_rot, Q_EXT), q_rot, _head_blocks_t(do, V_HEAD), do,
                            _head_rows(lse), _head_rows(delta), acc_in=dkv)
            dcq = _mm(f"uq_bwd_{l}", dq_ext, w["w_uq_ext"], mode="nt", out_dtype=F32)
            big["w_uq_ext"] = _mm(f"uq_wgrad_{l}", cq, dq_ext, mode="tn", out_dtype=F32)
            dcq_pre, dqn = _rms_bwd(f"qnorm_bwd_{l}", cq_pre, row(w["q_norm_g"]), dcq, out_dtype=BF16)
            per_b["q_norm_g"][j] = dqn.reshape(-1)
            dh1 = _mm(f"dq_bwd_{l}", dcq_pre, w["w_dq"], mode="nt")
            big["w_dq"] = _mm(f"dq_wgrad_{l}", h1, dcq_pre, mode="tn", out_dtype=F32)
        dx, dn1, dsh1, dsc1 = _rms_bwd(f"norm1_bwd_{l}", x_in, row(w["norm1_g"]), dh1, sc1, dx_in=dx_mid)
        per_layer["norm1_g"][l] = dn1.reshape(-1)
        dmods[l] = jnp.concatenate([dsh1, dsc1, dg1, dsh2, dsc2, dg2], axis=-1).reshape(-1)
        if l == N_A_LAYERS:
            dkn, dkd, dv = dkv
            dckv = _mm("uk_bwd", dkn, w["w_uk"], mode="nt", out_dtype=F32)
            dckv = _mm("uv_bwd", dv, w["w_uv"], mode="nt", out_dtype=F32, add=dckv)
            big["w_uk"] = _mm("uk_wgrad", kv["ckv"], dkn, mode="tn", out_dtype=F32)
            big["w_uv"] = _mm("uv_wgrad", kv["ckv"], dv, mode="tn", out_dtype=F32)
            dkr = _krope_bwd("krope_bwd", dkd, tabk)
            dc, dckv_g = _rms_bwd("ckv_bwd", kv["kv_ext"], row(w["ckv_norm_g"]), dckv, ncols=KV_RANK, out_dtype=BF16)
            dkv_ext = jnp.concatenate([dc, dkr.astype(BF16)], axis=-1)
            dkvn = _mm("dkv_bwd", dkv_ext, w["w_dkv_ext"], mode="nt")
            big["w_dkv_ext"] = _mm("dkv_wgrad", kv["kvn"], dkv_ext, mode="tn", out_dtype=F32)
            dx, dkv_in_g = _rms_bwd("kvin_bwd", kv["x"], row(w["kv_in_g"]), dkvn, dx_in=dx)
            g["ckv_norm_g"], g["kv_in_g"] = dckv_g.reshape(-1), dkv_in_g.reshape(-1)
        tok = push(l, big, dx)
    for group in (per_layer, per_a, per_b):
        for k, vals in group.items():
            g[k] = jnp.stack(vals)
    return loss, dx, g, jnp.stack(dmods)


def _my_index():
    return 4 * lax.axis_index("x") + 2 * lax.axis_index("y") + lax.axis_index("c")


def _peer(k):
    x, y, c = lax.axis_index("x"), lax.axis_index("y"), lax.axis_index("c")
    return (1 - x if k & 4 else x, 1 - y if k & 2 else y, 1 - c if k & 1 else c)


def _index_of(pos):
    return 4 * pos[0] + 2 * pos[1] + pos[2]


def _exchange_many(name, arrays, scatter):
    n = len(arrays)
    blocks = [tuple(a.shape[1:]) if scatter else tuple(a.shape) for a in arrays]

    def body(*refs):
        x_refs, o_refs = refs[:n], refs[n:2 * n]
        send_sems, recv_sems, local_sems = refs[2 * n:]
        me = _my_index()
        started = []
        for a in range(n):
            mine = pltpu.make_async_copy(x_refs[a].at[me] if scatter else x_refs[a], o_refs[a].at[me], local_sems.at[a])
            mine.start()
            started.append(mine)
        sends = []
        for k in range(1, N_DEV):
            peer = _peer(k)
            for a in range(n):
                cp = pltpu.make_async_remote_copy(
                    src_ref=x_refs[a].at[_index_of(peer)] if scatter else x_refs[a], dst_ref=o_refs[a].at[me],
                    send_sem=send_sems.at[a, k - 1], recv_sem=recv_sems.at[a, k - 1], device_id=peer, device_id_type=MESH)
                cp.start()
                sends.append(cp)
        for k in range(1, N_DEV):
            peer = _peer(k)
            for a in range(n):
                pltpu.make_async_remote_copy(
                    src_ref=x_refs[a].at[me] if scatter else x_refs[a], dst_ref=o_refs[a].at[_index_of(peer)],
                    send_sem=send_sems.at[a, k - 1], recv_sem=recv_sems.at[a, k - 1], device_id=peer, device_id_type=MESH).wait_recv()
        for cp in sends:
            cp.wait_send()
        for mine in started:
            mine.wait()

    return pl.pallas_call(
        body, name=name, out_shape=tuple(jax.ShapeDtypeStruct((N_DEV,) + blk, a.dtype) for blk, a in zip(blocks, arrays)),
        in_specs=[pl.BlockSpec(memory_space=pl.ANY)] * n, out_specs=tuple([pl.BlockSpec(memory_space=pl.ANY)] * n),
        scratch_shapes=[pltpu.SemaphoreType.DMA((n, N_DEV - 1)), pltpu.SemaphoreType.DMA((n, N_DEV - 1)), pltpu.SemaphoreType.DMA((n,))],
    )(*arrays)


def _exchange(name, x, scatter):
    return _exchange_many(name, [x], scatter)[0]


HBM_SPEC = pl.BlockSpec(memory_space=pltpu.HBM)
SEM_SPEC = pl.BlockSpec(memory_space=pltpu.SEMAPHORE)
DATAFLOW = pltpu.SideEffectType.DATAFLOW_SIDE_EFFECTING


def _remote_copies(x_refs, land_refs, send_sems, recv_sems, scatter):
    me = _my_index()
    out, inc = [], []
    for k in range(1, N_DEV):
        peer = _peer(k)
        for a in range(len(x_refs)):
            pair = a * (N_DEV - 1) + k - 1
            sems = dict(send_sem=send_sems.at[pair], recv_sem=recv_sems.at[pair], device_id=peer, device_id_type=MESH)
            out.append(pltpu.make_async_remote_copy(
                src_ref=x_refs[a].at[_index_of(peer)] if scatter else x_refs[a], dst_ref=land_refs[a].at[me], **sems))
            inc.append(pltpu.make_async_remote_copy(
                src_ref=x_refs[a].at[me] if scatter else x_refs[a], dst_ref=land_refs[a].at[_index_of(peer)], **sems))
    return out, inc


def _exchange_start(name, arrays, scatter):
    n = len(arrays)
    blocks = [tuple(a.shape[1:]) if scatter else tuple(a.shape) for a in arrays]

    def body(*refs):
        x_refs, land_refs = refs[:n], refs[n:2 * n]
        send_sems, recv_sems = refs[2 * n], refs[2 * n + 1]
        for cp in _remote_copies(x_refs, land_refs, send_sems, recv_sems, scatter)[0]:
            cp.start()
        refs[-1][...] = jnp.zeros_like(refs[-1])

    sem_type = pltpu.SemaphoreType.DMA((n * (N_DEV - 1),))
    lands =[pltpu.with_memory_space_constraint(lax.empty((N_DEV,) + blk, a.dtype), pltpu.HBM) for blk, a in zip(blocks, arrays)]
    srcs = [pltpu.with_memory_space_constraint(a, pltpu.HBM) for a in arrays]
    res = pl.pallas_call(
        body, name=name,
        out_shape=(sem_type, sem_type, *[pltpu.HBM(a.shape, a.dtype) for a in srcs + lands], jax.ShapeDtypeStruct((8, LANES), F32)),
        in_specs=[HBM_SPEC] * (2 * n), out_specs=(SEM_SPEC, SEM_SPEC, *[HBM_SPEC] * (2 * n), pl.BlockSpec(memory_space=pltpu.VMEM)),
        input_output_aliases={i: 2 + i for i in range(2 * n)},
        compiler_params=pltpu.CompilerParams(has_side_effects=DATAFLOW),
    )(*srcs, *lands)
    return (res[0], res[1], list(res[2:2 + n]), list(res[2 + n:2 + 2 * n])), res[-1]


def _exchange_wait(name, handles, after, scatter):
    send_sems, recv_sems, srcs, lands = handles
    n = len(srcs)

    def body(*refs):
        x_refs, land_refs = refs[:n], refs[n:2 * n]
        out, inc = _remote_copies(x_refs, land_refs, refs[2 * n], refs[2 * n + 1], scatter)
        for cp in out:
            cp.wait_send()
        for cp in inc:
            cp.wait_recv()

    res = pl.pallas_call(
        body, name=name, out_shape=tuple(pltpu.HBM(a.shape, a.dtype) for a in srcs + lands),
        in_specs=[HBM_SPEC] * (2 * n) + [SEM_SPEC, SEM_SPEC, pl.BlockSpec(memory_space=pl.ANY)], out_specs=tuple([HBM_SPEC] * (2 * n)),
        input_output_aliases={i: i for i in range(2 * n)},
        compiler_params=pltpu.CompilerParams(has_side_effects=DATAFLOW),
    )(*srcs, *lands, send_sems, recv_sems, after)
    return list(res[n:])


def _pack(arrays, dtype, row_multiple):
    flat = jnp.concatenate([a.astype(dtype).reshape(-1) for a in arrays])
    rows = -(-flat.shape[0] // (LANES * row_multiple)) * row_multiple
    return jnp.pad(flat, (0, rows * LANES - flat.shape[0])).reshape(rows, LANES)


def _unpack(packed, shapes):
    lead = packed.shape[:-2]
    flat = packed.reshape(lead + (-1,))
    out, off = [], 0
    for shp in shapes:
        size = 1
        for d in shp:
            size *= d
        out.append(flat[..., off:off + size].reshape(lead + tuple(shp)))
        off += size
    return out


def _unshard(g8, axis):
    return jnp.concatenate([g8[j] for j in range(N_DEV)], axis=axis)


def _shard8(full, axis):
    n = full.shape[axis] // N_DEV
    return jnp.stack([lax.slice_in_dim(full, j * n, (j + 1) * n, axis=axis) for j in range(N_DEV)])


VECTOR_WEIGHTS = (("pool_b", 1), ("pool_scale", 1), ("conv_w", 2))
REPLICATED_WEIGHTS = ("norm1_g", "norm2_g", "kv_in_g", "ckv_norm_g", "q_norm_g", "conv_b", "final_g")
WEIGHT_ORDER = ("mod_w", "mod_b", "norm1_g", "norm2_g", "pool_w", "pool_b", "pool_scale", "kv_in_g", "w_dkv", "ckv_norm_g", "w_uk",
                "w_uv", "w_dq", "q_norm_g", "w_uq", "w_o", "w_up", "conv_w", "conv_b", "w_down", "final_g")
BIG_ROW_MULTIPLE = 1024
SMALL_ROW_MULTIPLE = 16


def _as_2d(a):
    if a.ndim == 1:
        return a.reshape(-1, LANES)
    return a.reshape(-1, a.shape[-1])


def kernel(x, c, positions, mod_w, mod_b, norm1_g, norm2_g, pool_w, pool_b, pool_scale, kv_in_g, w_dkv, ckv_norm_g, w_uk, w_uv, w_dq, q_norm_g, w_uq, w_o, w_up, conv_w, conv_b, w_down, final_g, loss_target, m_mod_w, m_mod_b, m_norm1_g, m_norm2_g, m_pool_w, m_pool_b, m_pool_scale, m_kv_in_g, m_w_dkv, m_ckv_norm_g, m_w_uk, m_w_uv, m_w_dq, m_q_norm_g, m_w_uq, m_w_o, m_w_up, m_conv_w, m_conv_b, m_w_down, m_final_g, v_mod_w, v_mod_b, v_norm1_g, v_norm2_g, v_pool_w, v_pool_b, v_pool_scale, v_kv_in_g, v_w_dkv, v_ckv_norm_g, v_w_uk, v_w_uv, v_w_dq, v_q_norm_g, v_w_uq, v_w_o, v_w_up, v_conv_w, v_conv_b, v_w_down, v_final_g):
    shard = dict(mod_w=mod_w, mod_b=mod_b, norm1_g=norm1_g, norm2_g=norm2_g, pool_w=pool_w, pool_b=pool_b, pool_scale=pool_scale,
                 kv_in_g=kv_in_g, w_dkv=w_dkv, ckv_norm_g=ckv_norm_g, w_uk=w_uk, w_uv=w_uv, w_dq=w_dq, q_norm_g=q_norm_g, w_uq=w_uq,
                 w_o=w_o, w_up=w_up, conv_w=conv_w, conv_b=conv_b, w_down=w_down, final_g=final_g)
    mom_m = dict(mod_w=m_mod_w, mod_b=m_mod_b, norm1_g=m_norm1_g, norm2_g=m_norm2_g, pool_w=m_pool_w, pool_b=m_pool_b,
                 pool_scale=m_pool_scale, kv_in_g=m_kv_in_g, w_dkv=m_w_dkv, ckv_norm_g=m_ckv_norm_g, w_uk=m_w_uk, w_uv=m_w_uv,
                 w_dq=m_w_dq, q_norm_g=m_q_norm_g, w_uq=m_w_uq, w_o=m_w_o, w_up=m_w_up, conv_w=m_conv_w, conv_b=m_conv_b,
                 w_down=m_w_down, final_g=m_final_g)
    mom_v = dict(mod_w=v_mod_w, mod_b=v_mod_b, norm1_g=v_norm1_g, norm2_g=v_norm2_g, pool_w=v_pool_w, pool_b=v_pool_b,
                 pool_scale=v_pool_scale, kv_in_g=v_kv_in_g, w_dkv=v_w_dkv, ckv_norm_g=v_ckv_norm_g, w_uk=v_w_uk, w_uv=v_w_uv,
                 w_dq=v_w_dq, q_norm_g=v_q_norm_g, w_uq=v_w_uq, w_o=v_w_o, w_up=v_w_up, conv_w=v_conv_w, conv_b=v_conv_b,
                 w_down=v_w_down, final_g=v_final_g)
    me = _my_index()
    d6 = N_MOD * D_MODEL
    mod_cols = d6 // N_DEV

    half = N_DEV // 2
    up_cols = shard["w_up"].shape[2]

    def stage_pieces(l):
        keys = ["w_up", "w_down"]
        if l >= N_A_LAYERS:
            keys += ["w_dq", "w_uq", "w_o"]
        layer = l if l < N_A_LAYERS else l - N_A_LAYERS
        out = {k: shard[k][l if k in ("w_up", "w_down") else layer].astype(BF16) for k in keys}
        if l == 0:
            out["pool_w"] = shard["pool_w"].astype(BF16)
        if l == N_A_LAYERS:
            out.update({k: shard[k].astype(BF16) for k in ("w_dkv", "w_uk", "w_uv")})
        return out

    gathers = {}
    first = stage_pieces(0)
    handles, token = _exchange_start("gather_start_0", list(first.values()), scatter=False)
    gathers[0] = (handles, first)

    small_in = [c + token[0, 0]] + [shard[k] for k, _ in VECTOR_WEIGHTS]
    small_all = _exchange("gather_vectors", _pack(small_in, F32, SMALL_ROW_MULTIPLE), scatter=False)
    parts = _unpack(small_all, [a.shape for a in small_in])
    c_all = jnp.pad(parts[0].reshape(N_DEV, D_MODEL), ((0, N_DEV), (0, 0)))
    vec = {k: _unshard(p, ax) for (k, ax), p in zip(VECTOR_WEIGHTS, parts[1:])}

    my_mod_b = lax.dynamic_slice_in_dim(mod_b, me * mod_cols, mod_cols, axis=1)
    mods_mine = _mods_fwd("mods_fwd", c_all, mod_w, my_mod_b)
    mods_all = _exchange("gather_mods", _pack([mods_mine], F32, SMALL_ROW_MULTIPLE), scatter=False)
    mods_all = _unpack(mods_all, [mods_mine.shape])[0]
    mods = lax.dynamic_index_in_dim(mods_all, me, axis=2, keepdims=False)
    mods = jnp.moveaxis(mods, 0, 1).reshape(DEPTH, d6)

    tabq, tabk = _rope_tables(positions[0])
    pool_all = []

    def whole_weights(l, got):
        cat = lambda a, axis, lo=0, hi=N_DEV: jnp.concatenate([a[j] for j in range(lo, hi)], axis=axis)
        w = dict(w_up_a=cat(got["w_up"], -1, 0, half), w_up_v=cat(got["w_up"], -1, half, N_DEV), w_down=got["w_down"].reshape(D_FF, D_MODEL),
                 norm1_g=norm1_g[l], norm2_g=norm2_g[l], conv_w=vec["conv_w"][l], conv_b=conv_b[l])
        if l == 0:
            pool_all.append(got["pool_w"])
        if l < N_A_LAYERS:
            w.update(pool_w=cat(pool_all[0][:, l], 1), pool_b=vec["pool_b"][l], pool_scale=vec["pool_scale"][l])
        else:
            rope = got["w_uq"][..., QK_NOPE:]
            ext = jnp.concatenate([got["w_uq"][..., :QK_NOPE], rope, _swap_halves(rope)], axis=-1)
            w.update(w_dq=got["w_dq"].reshape(D_MODEL, Q_RANK), w_uq_ext=cat(ext, -1), w_o=got["w_o"].reshape(D_MODEL, D_MODEL),
                     q_norm_g=q_norm_g[l - N_A_LAYERS])
        if l == N_A_LAYERS:
            w.update(w_dkv_ext=_extend_w_dkv(got["w_dkv"].reshape(D_MODEL, KV_RANK + QK_ROPE)), w_uk=cat(got["w_uk"], -1),
                     w_uv=cat(got["w_uv"], -1), kv_in_g=kv_in_g, ckv_norm_g=ckv_norm_g)
        return w

    def own_slot(lands, own):
        return [lax.dynamic_update_index_in_dim(p, o, me, 0) for p, o in zip(lands, own)]

    def fetch(l, after):
        handles, pieces = gathers.pop(l)
        lands = _exchange_wait(f"gather_wait_{l}", handles, mods if l == 0 else after, scatter=False)
        got = dict(zip(pieces, own_slot(lands, list(pieces.values()))))
        tok = 0.0
        if l + 1 < DEPTH:
            nxt = stage_pieces(l + 1)
            handles, token = _exchange_start(f"gather_start_{l + 1}", list(nxt.values()), scatter=False)
            gathers[l + 1] = (handles, nxt)
            tok = token[0, 0]
        return whole_weights(l, got), tok

    scatters, pool_grads, piece_grads = {}, {}, {}

    def reduce_pieces(l, keys, got):
        for k, p in zip(keys, got):
            piece_grads[(k, l)] = _sum8(f"sum_grads_{k}_{l}", p.reshape(N_DEV, -1, p.shape[-1])).reshape(p.shape[1:])

    def push(l, big, after):
        cut = lambda a, n, axis: jnp.stack([lax.slice_in_dim(a, j * n, (j + 1) * n, axis=axis) for j in range(N_DEV)])
        sent = dict(w_up=jnp.stack([lax.slice_in_dim(big[part], j * up_cols, (j + 1) * up_cols, axis=1)
                                    for part in ("w_up_a", "w_up_v") for j in range(half)]),
                    w_down=big["w_down"].reshape(N_DEV, D_FF // N_DEV, D_MODEL))
        if l < N_A_LAYERS:
            pool_grads[l] = big["pool_w"]
        else:
            ext = cut(big["w_uq_ext"], Q_EXT, 1)
            rope = ext[..., QK_NOPE:QK_HEAD] + _unswap_halves(ext[..., QK_HEAD:])
            sent.update(w_dq=big["w_dq"].reshape(N_DEV, D_MODEL // N_DEV, Q_RANK), w_uq=jnp.concatenate([ext[..., :QK_NOPE], rope], axis=-1),
                        w_o=big["w_o"].reshape(N_DEV, D_MODEL // N_DEV, D_MODEL))
        if l == 0:
            sent["pool_w"] = _shard8(jnp.stack([pool_grads[a] for a in range(N_A_LAYERS)]), 2)
        if l == N_A_LAYERS:
            sent.update(w_dkv=_fold_w_dkv_grad(big["w_dkv_ext"]).reshape(N_DEV, D_MODEL // N_DEV, KV_RANK + QK_ROPE),
                        w_uk=cut(big["w_uk"], QK_NOPE, 1), w_uv=cut(big["w_uv"], V_HEAD, 1))
        sent = {k: a.astype(BF16) for k, a in sent.items()}
        if l + 1 < DEPTH:
            handles, keys, own = scatters.pop(l + 1)
            reduce_pieces(l + 1, keys, own_slot(_exchange_wait(f"scatter_wait_{l + 1}", handles, after, scatter=True), own))
        handles, token = _exchange_start(f"scatter_start_{l}", list(sent.values()), scatter=True)
        scatters[l] = (handles, list(sent), [lax.dynamic_index_in_dim(a, me, 0, keepdims=False) for a in sent.values()])
        scatters["token"] = token[0, 0]
        return token[0, 0]

    loss_row, dx, g, dmods = _forward_backward(x[0], loss_target[0], mods, tabq, tabk, final_g, fetch, push)
    last_token = scatters.pop("token")
    layers_of = lambda k, ls: jnp.stack([piece_grads[(k, l)] for l in ls])
    grads = dict(w_dkv=piece_grads[("w_dkv", N_A_LAYERS)], w_uk=piece_grads[("w_uk", N_A_LAYERS)], w_uv=piece_grads[("w_uv", N_A_LAYERS)])
    for k in ("w_dq", "w_uq", "w_o"):
        grads[k] = layers_of(k, range(N_A_LAYERS, DEPTH))

    small_names = REPLICATED_WEIGHTS + tuple(k for k, _ in VECTOR_WEIGHTS)
    small_out = [dmods] + [g[k] for k in small_names] + [loss_row + last_token]
    small_shapes = [a.shape for a in small_out]
    small_got = _exchange("gather_small_grads", _pack(small_out, F32, SMALL_ROW_MULTIPLE), scatter=False)
    summed = _unpack(_sum8("sum_small_grads", small_got), small_shapes)
    grads["mod_b"] = summed[0]
    for k, s in zip(small_names, summed[1:-1]):
        grads[k] = s
    for k, ax in VECTOR_WEIGHTS:
        n = shard[k].shape[ax]
        grads[k] = lax.dynamic_slice_in_dim(grads[k], me * n, n, axis=ax)
    loss = summed[-1][0, 0]
    dmods_all = _unpack(small_got, small_shapes)[0]
    dm_mine = lax.dynamic_slice_in_dim(dmods_all, me * mod_cols, mod_cols, axis=2)
    dm_mine = jnp.pad(jnp.moveaxis(dm_mine, 0, 1), ((0, 0), (0, N_DEV), (0, 0)))
    grads["mod_w"] = _mods_bwd("mods_bwd", c_all, dm_mine)

    delta, new_m, new_v = {}, {}, {}

    def adamw(k):
        shp = shard[k].shape
        grads[k] = grads[k].reshape(shp)
        d_, m_, v_ = _adamw(f"adamw_{k}", _as_2d(shard[k]), _as_2d(grads[k]), _as_2d(mom_m[k]), _as_2d(mom_v[k]))
        delta[k], new_m[k], new_v[k] = d_.reshape(shp), m_.reshape(shp), v_.reshape(shp)

    late = ("w_up", "w_down", "pool_w")
    for k in WEIGHT_ORDER:
        if k not in late:
            adamw(k)
    handles, keys, own = scatters.pop(0)
    reduce_pieces(0, keys, own_slot(_exchange_wait("scatter_wait_0", handles, delta["final_g"], scatter=True), own))
    grads.update(w_up=layers_of("w_up", range(DEPTH)), w_down=layers_of("w_down", range(DEPTH)), pool_w=piece_grads[("pool_w", 0)])
    for k in late:
        adamw(k)
    return (loss, dx[None], *[grads[k] for k in WEIGHT_ORDER], *[delta[k] for k in WEIGHT_ORDER],
            *[new_m[k] for k in WEIGHT_ORDER], *[new_v[k] for k in WEIGHT_ORDER])
```

```python
import functools

import jax
import jax.numpy as jnp
from jax import lax
from jax.experimental import pallas as pl
from jax.experimental.pallas import tpu as pltpu

F32 = jnp.float32
BF16 = jnp.bfloat16

D_MODEL = 1024
DEPTH = 4
N_A_LAYERS = 2
N_B_LAYERS = 2
POOL_WINDOWS = (2, 4, 8, 16)
POOL_GROUP = 256
N_HEADS = 8
QK_NOPE = 128
QK_ROPE = 64
V_HEAD = 128
QK_HEAD = QK_NOPE + QK_ROPE
Q_RANK = 384
KV_RANK = 256
ROPE_THETA = 10000.0
D_FF = 2816
EPS = 1e-6
N_MOD = 6
ADAM_LR = 0.001
ADAM_B1 = 0.9
ADAM_B2 = 0.999
ADAM_EPS = 1e-08
ADAM_WD = 0.01
ADAM_STEP = 10

N_DEV = 8
LANES = 128
Q_EXT = 256
VMEM_LIMIT_BYTES = 48 * 1024 * 1024
MESH = pl.DeviceIdType.MESH
NEG_BIG = -0.7 * float(jnp.finfo(jnp.float32).max)


def _params(sem):
    return pltpu.CompilerParams(dimension_semantics=sem, vmem_limit_bytes=VMEM_LIMIT_BYTES)


def _tile(n, cap):
    if n <= cap:
        return n
    best = None
    for d in range(LANES, cap + 1, LANES):
        if n % d == 0:
            best = d
    assert best is not None, (n, cap)
    return best


def _dot(a, b, dims):
    return lax.dot_general(a, b, (dims, ((), ())), preferred_element_type=F32)


NN = ((1,), (0,))
NT = ((1,), (1,))
TN = ((0,), (0,))


def _mm(name, a, b, mode="nn", out_dtype=BF16, add=None, resid=None, gate=None, rowtab=None,
        tm_cap=1024, tn_cap=1408, tk_cap=1408):
    if mode == "tn":
        kdim, m = a.shape
    else:
        m, kdim = a.shape
    n = b.shape[0] if mode == "nt" else b.shape[1]
    tm, tn, tk = _tile(m, tm_cap), _tile(n, tn_cap), _tile(kdim, tk_cap)
    nk = kdim // tk
    dims = {"nn": NN, "nt": NT, "tn": TN}[mode]
    a_spec = pl.BlockSpec((tk, tm), lambda i, j, k: (k, i)) if mode == "tn" else pl.BlockSpec((tm, tk), lambda i, j, k: (i, k))
    b_spec = pl.BlockSpec((tn, tk), lambda i, j, k: (j, k)) if mode == "nt" else pl.BlockSpec((tk, tn), lambda i, j, k: (k, j))
    o_spec = pl.BlockSpec((tm, tn), lambda i, j, k: (i, j))
    g_spec = pl.BlockSpec((1, tn), lambda i, j, k: (0, j))
    gated = resid is not None

    def body(*refs):
        a_ref, b_ref = refs[0], refs[1]
        acc = refs[-1]
        k = pl.program_id(2)

        @pl.when(k == 0)
        def _():
            acc[...] = jnp.zeros_like(acc)

        acc[...] += _dot(a_ref[...].astype(BF16), b_ref[...].astype(BF16), dims)

        @pl.when(k == nk - 1)
        def _():
            if gated:
                r_ref, g_ref, y_ref, x_ref = refs[2:6]
                y_ref[...] = acc[...]
                x_ref[...] = r_ref[...] + g_ref[...] * acc[...]
            elif add is not None:
                refs[3][...] = (acc[...] + refs[2][...].astype(F32)).astype(out_dtype)
            elif rowtab is not None:
                tab = refs[2][...]
                refs[3][...] = (acc[...] * jnp.concatenate([tab] * (tn // tab.shape[1]), axis=1)).astype(out_dtype)
            else:
                refs[2][...] = acc[...].astype(out_dtype)

    ins, in_specs = [a, b], [a_spec, b_spec]
    if rowtab is not None:
        assert tn % rowtab.shape[1] == 0 and not gated and add is None
        ins.append(rowtab)
        in_specs.append(pl.BlockSpec((tm, rowtab.shape[1]), lambda i, j, k: (i, 0)))
    if gated:
        ins += [resid, gate]
        in_specs += [o_spec, g_spec]
        out_shape = (jax.ShapeDtypeStruct((m, n), F32), jax.ShapeDtypeStruct((m, n), F32))
        out_specs = (o_spec, o_spec)
    else:
        if add is not None:
            ins.append(add)
            in_specs.append(o_spec)
        out_shape = jax.ShapeDtypeStruct((m, n), out_dtype)
        out_specs = o_spec
    return pl.pallas_call(
        body, name=name, grid=(m // tm, n // tn, nk), in_specs=in_specs, out_specs=out_specs, out_shape=out_shape,
        scratch_shapes=[pltpu.VMEM((tm, tn), F32)],
        compiler_params=_params(("parallel", "parallel", "arbitrary")),
    )(*ins)


def _rowwise(name, fn, tiled, bcast, outs, sums=(), tr=512):
    tiled = [t if isinstance(t, tuple) else (t, t.shape[1], 0) for t in tiled]
    s = tiled[0][0].shape[0]
    tr = min(tr, s)
    assert s % tr == 0
    n_t, n_b, n_o = len(tiled), len(bcast), len(outs)

    def body(*refs):
        i = pl.program_id(0)
        vals = [r[...] for r in refs[:n_t + n_b]]
        o_vals, s_vals = fn(*vals)
        for r, v in zip(refs[n_t + n_b:n_t + n_b + n_o], o_vals):
            r[...] = v.astype(r.dtype)
        s_refs = refs[n_t + n_b + n_o:]

        @pl.when(i == 0)
        def _():
            for r in s_refs:
                r[...] = jnp.zeros_like(r)

        for r, v in zip(s_refs, s_vals):
            r[...] += v

    in_specs = [pl.BlockSpec((tr, n), functools.partial(lambda cb, i: (i, cb), cb)) for (_, n, cb) in tiled]
    in_specs += [pl.BlockSpec(b.shape, functools.partial(lambda nd, i: (0,) * nd, b.ndim)) for b in bcast]
    out_specs = [pl.BlockSpec((tr, n), lambda i: (i, 0)) for (n, _) in outs]
    out_specs += [pl.BlockSpec((1, n), lambda i: (0, 0)) for n in sums]
    out_shape = [jax.ShapeDtypeStruct((s, n), dt) for (n, dt) in outs]
    out_shape += [jax.ShapeDtypeStruct((1, n), F32) for n in sums]
    res = pl.pallas_call(
        body, name=name, grid=(s // tr,), in_specs=in_specs, out_specs=tuple(out_specs), out_shape=tuple(out_shape),
        compiler_params=_params(("arbitrary",)),
    )(*[t[0] for t in tiled], *bcast)
    return res


def _colsum(v):
    return jnp.sum(v, axis=0, keepdims=True)


def _rms_fwd(name, x, g, scale=None, shift=None, out_dtype=BF16, ncols=None):
    mod = scale is not None

    def fn(xv, gv, *ss):
        y = xv * lax.rsqrt(jnp.mean(xv * xv, axis=-1, keepdims=True) + EPS) * gv
        if mod:
            y = y * (1.0 + ss[0]) + ss[1]
        return (y,), ()

    n = ncols or x.shape[1]
    return _rowwise(name, fn, [(x, n, 0)], [g] + ([scale, shift] if mod else []), [(n, out_dtype)])[0]


def _rms_bwd(name, x, g, dh, scale=None, dx_in=None, ncols=None, out_dtype=F32):
    mod = scale is not None
    has_in = dx_in is not None

    def fn(*vals):
        xv, dhv = vals[0], vals[1].astype(F32)
        rest = list(vals[2:])
        dxi = rest.pop(0) if has_in else None
        gv = rest.pop(0)
        rstd = lax.rsqrt(jnp.mean(xv * xv, axis=-1, keepdims=True) + EPS)
        xhat = xv * rstd
        sums = []
        if mod:
            sc = rest.pop(0)
            dyn = dhv * (1.0 + sc)
            dshift, dscale = _colsum(dhv), _colsum(dhv * (xhat * gv))
        else:
            dyn = dhv
        dg = _colsum(dyn * xhat)
        dxhat = dyn * gv
        dx = rstd * (dxhat - xhat * jnp.mean(dxhat * xhat, axis=-1, keepdims=True))
        if has_in:
            dx = dx + dxi
        sums = [dg] + ([dshift, dscale] if mod else [])
        return (dx,), sums

    n = ncols or x.shape[1]
    tiled = [(x, n, 0), dh] + ([dx_in] if has_in else [])
    return _rowwise(name, fn, tiled, [g] + ([scale] if mod else []), [(n, out_dtype)], [n] * (3 if mod else 1))


def _gate_bwd(name, dxn, y, g):
    def fn(dv, yv, gv):
        return (gv * dv,), (_colsum(dv * yv),)

    n = dxn.shape[1]
    return _rowwise(name, fn, [dxn, y], [g], [(n, BF16)], [n])


def _loss_head(name, x, g, target):
    n = x.shape[1]

    def fn(xv, tv, gv):
        rstd = lax.rsqrt(jnp.mean(xv * xv, axis=-1, keepdims=True) + EPS)
        xhat = xv * rstd
        err = xhat * gv - tv
        loss = 0.5 * jnp.sum(jnp.sum(err * err, axis=-1, keepdims=True) / n, axis=0, keepdims=True)
        dy = err / n
        dg = _colsum(dy * xhat)
        dxhat = dy * gv
        dx = rstd * (dxhat - xhat * jnp.mean(dxhat * xhat, axis=-1, keepdims=True))
        return (dx,), (dg, jnp.broadcast_to(loss, (1, LANES)))

    return _rowwise(name, fn, [x, target], [g], [(n, F32)], [n, LANES])


def _krope_fwd(name, kv_ext, tabk):
    def fn(xv, tv):
        t = xv * tv
        return (t + pltpu.roll(t, 64, 1),), ()

    return _rowwise(name, fn, [(kv_ext, LANES, 2), tabk], [], [(LANES, BF16)])[0]


def _krope_bwd(name, dkd, tabk):
    def fn(dv, tv):
        return ((dv + pltpu.roll(dv, 64, 1)) * tv,), ()

    return _rowwise(name, fn, [dkd, tabk], [], [(LANES, F32)])[0]


def _adamw(name, w, g, m, v):
    def fn(wv, gv, mv, vv):
        m2 = ADAM_B1 * mv + (1.0 - ADAM_B1) * gv
        v2 = ADAM_B2 * vv + (1.0 - ADAM_B2) * (gv * gv)
        m_hat = m2 / (1.0 - ADAM_B1 ** ADAM_STEP)
        v_hat = v2 / (1.0 - ADAM_B2 ** ADAM_STEP)
        delta = -ADAM_LR * (m_hat / (jnp.sqrt(v_hat) + ADAM_EPS) + ADAM_WD * wv)
        return (delta, m2, v2), ()

    r, c = w.shape
    tr = r
    for cand in (512, 256, 128, 64, 32, 16, 8):
        if r % cand == 0 and r > cand:
            tr = cand
            break
    return _rowwise(name, fn, [w, g, m, v], [], [(c, F32)] * 3, tr=tr)


def _sum8(name, parts):
    _, r, c = parts.shape
    tr = r
    for cand in (2048, 1024, 512, 256, 128, 64, 32, 16):
        if r % cand == 0 and r > cand and cand * c <= 256 * 1024:
            tr = cand
            break

    def body(p_ref, o_ref):
        acc = p_ref[0].astype(F32)
        for k in range(1, N_DEV):
            acc = acc + p_ref[k].astype(F32)
        o_ref[...] = acc

    return pl.pallas_call(
        body, name=name, grid=(r // tr,), in_specs=[pl.BlockSpec((N_DEV, tr, c), lambda i: (0, i, 0))],
        out_specs=pl.BlockSpec((tr, c), lambda i: (i, 0)), out_shape=jax.ShapeDtypeStruct((r, c), F32),
        compiler_params=_params(("parallel",)),
    )(parts)


def _mods_fwd(name, c_all, w, b):
    depth, d, n = w.shape

    def body(c_ref, w_ref, b_ref, o_ref):
        cv = c_ref[...]
        sc = (cv * (1.0 / (1.0 + jnp.exp(-cv)))).astype(BF16)
        o_ref[0] = _dot(sc, w_ref[0].astype(BF16), NN) + b_ref[0]

    return pl.pallas_call(
        body, name=name, grid=(depth,),
        in_specs=[pl.BlockSpec(c_all.shape, lambda l: (0, 0)), pl.BlockSpec((1, d, n), lambda l: (l, 0, 0)),
                  pl.BlockSpec((1, 1, n), lambda l: (l, 0, 0))],
        out_specs=pl.BlockSpec((1, c_all.shape[0], n), lambda l: (l, 0, 0)),
        out_shape=jax.ShapeDtypeStruct((depth, c_all.shape[0], n), F32),
        compiler_params=_params(("parallel",)),
    )(c_all, w, b.reshape(depth, 1, n))


def _mods_bwd(name, c_all, dm):
    depth, rows, n = dm.shape
    d = c_all.shape[1]

    def body(c_ref, dm_ref, o_ref):
        cv = c_ref[...]
        sc = (cv * (1.0 / (1.0 + jnp.exp(-cv)))).astype(BF16)
        o_ref[0] = _dot(sc, dm_ref[0].astype(BF16), TN)

    return pl.pallas_call(
        body, name=name, grid=(depth,),
        in_specs=[pl.BlockSpec(c_all.shape, lambda l: (0, 0)), pl.BlockSpec((1, rows, n), lambda l: (l, 0, 0))],
        out_specs=pl.BlockSpec((1, d, n), lambda l: (l, 0, 0)),
        out_shape=jax.ShapeDtypeStruct((depth, d, n), F32),
        compiler_params=_params(("parallel",)),
    )(c_all, dm)


POOL_TILE = 256


def _split_dot(band, val):
    hi = val.astype(BF16)
    lo = (val - hi.astype(F32)).astype(BF16)
    return _dot(band, hi, NN) + _dot(band, lo, NN)


def _pool_fwd(name, h1, x, pw, pb, ps, g1):
    s, d = h1.shape
    t = POOL_TILE

    def body(hc_ref, hp_ref, x_ref, pw_ref, pb_ref, ps_ref, g_ref, xo_ref, zb_ref, pooled_ref):
        i = pl.program_id(0)
        r = lax.broadcasted_iota(jnp.int32, (t, t), 0)
        j = lax.broadcasted_iota(jnp.int32, (t, t), 1)
        pos = (i * t + lax.broadcasted_iota(jnp.int32, (t, 1), 0) + 1).astype(F32)
        has_prev = (i > 0).astype(F32)
        for grp, w in enumerate(POOL_WINDOWS):
            cs = slice(grp * POOL_GROUP, (grp + 1) * POOL_GROUP)
            hc = hc_ref[:, cs]
            band_cur = ((r - j >= 0) & (r - j < w)).astype(BF16)
            band_prev = (r + t - j < w).astype(BF16)
            ssum = _split_dot(band_cur, hc) + has_prev * _split_dot(band_prev, hp_ref[:, cs])
            pooled = (ssum / jnp.minimum(pos, float(w)) - hc).astype(BF16)
            zb = _dot(pooled, pw_ref[grp], NN) + pb_ref[:, cs]
            xo_ref[:, cs] = x_ref[:, cs] + g_ref[:, cs] * (zb * ps_ref[:, cs])
            zb_ref[:, cs] = zb
            pooled_ref[:, cs] = pooled

    row = pl.BlockSpec((t, d), lambda i: (i, 0))
    vec = pl.BlockSpec((1, d), lambda i: (0, 0))
    return pl.pallas_call(
        body, name=name, grid=(s // t,),
        in_specs=[row, pl.BlockSpec((t, d), lambda i: (jnp.maximum(i - 1, 0), 0)), row,
                  pl.BlockSpec(pw.shape, lambda i: (0, 0, 0)), vec, vec, vec],
        out_specs=(row, row, row),
        out_shape=(jax.ShapeDtypeStruct((s, d), F32), jax.ShapeDtypeStruct((s, d), F32), jax.ShapeDtypeStruct((s, d), BF16)),
        compiler_params=_params(("parallel",)),
    )(h1, h1, x, pw, pb, ps, g1)


def _pool_bwd(name, dxn, zb, pooled, pw, ps, g1):
    s, d = dxn.shape
    t = POOL_TILE
    nt = s // t

    def body(dc_ref, dn_ref, zb_ref, pooled_ref, pw_ref, ps_ref, g_ref, dh_ref, dpw_ref, dpb_ref, dps_ref, dg_ref):
        i = pl.program_id(0)

        @pl.when(i == 0)
        def _():
            dpw_ref[...] = jnp.zeros_like(dpw_ref)
            dpb_ref[...] = jnp.zeros_like(dpb_ref)
            dps_ref[...] = jnp.zeros_like(dps_ref)
            dg_ref[...] = jnp.zeros_like(dg_ref)

        jj = lax.broadcasted_iota(jnp.int32, (t, t), 0)
        rr = lax.broadcasted_iota(jnp.int32, (t, t), 1)
        pos = (i * t + lax.broadcasted_iota(jnp.int32, (t, 1), 0) + 1).astype(F32)
        has_next = (i < nt - 1).astype(F32)
        for grp, w in enumerate(POOL_WINDOWS):
            cs = slice(grp * POOL_GROUP, (grp + 1) * POOL_GROUP)
            gv, psv, zbv, dxc = g_ref[:, cs], ps_ref[:, cs], zb_ref[:, cs], dc_ref[:, cs]
            dg_ref[:, cs] += _colsum(dxc * (zbv * psv))
            dy = gv * dxc
            dps_ref[:, cs] += _colsum(dy * zbv)
            dz = dy * psv
            dpb_ref[:, cs] += _colsum(dz)
            dzb = dz.astype(BF16)
            dpw_ref[grp] += _dot(pooled_ref[:, cs], dzb, TN)
            dp = _dot(dzb, pw_ref[grp], NT)
            dzn = (gv * dn_ref[:, cs] * psv).astype(BF16)
            dpn = _dot(dzn, pw_ref[grp], NT) * (has_next / float(w))
            band_cur = ((rr - jj >= 0) & (rr - jj < w)).astype(BF16)
            band_next = (rr + t - jj < w).astype(BF16)
            dh_ref[:, cs] = _split_dot(band_cur, dp / jnp.minimum(pos, float(w))) + _split_dot(band_next, dpn) - dp

    row = pl.BlockSpec((t, d), lambda i: (i, 0))
    vec = pl.BlockSpec((1, d), lambda i: (0, 0))
    wspec = pl.BlockSpec(pw.shape, lambda i: (0, 0, 0))
    return pl.pallas_call(
        body, name=name, grid=(nt,),
        in_specs=[row, pl.BlockSpec((t, d), lambda i: (jnp.minimum(i + 1, nt - 1), 0)), row, row, wspec, vec, vec],
        out_specs=(row, wspec, vec, vec, vec),
        out_shape=(jax.ShapeDtypeStruct((s, d), F32), jax.ShapeDtypeStruct(pw.shape, F32),
                   jax.ShapeDtypeStruct((1, d), F32), jax.ShapeDtypeStruct((1, d), F32), jax.ShapeDtypeStruct((1, d), F32)),
        compiler_params=_params(("arbitrary",)),
    )(dxn, dxn, zb, pooled, pw, ps, g1)


GLU_TILE = 256
HALO = 16
INV_SQRT2 = 0.7071067811865476
INV_SQRT_2PI = 0.3989422804014327


def _gelu(xv):
    return 0.5 * xv * (1.0 + lax.erf(xv * INV_SQRT2))


def _glu_fwd(name, ua, uv, cw, cb):
    s, f = ua.shape
    t, tf = GLU_TILE, _tile(f, 1408)

    def body(a_ref, ah_ref, v_ref, cw_ref, cb_ref, o_ref):
        i = pl.program_id(1)
        has_prev = (i > 0).astype(F32)
        ext = jnp.concatenate([ah_ref[...].astype(F32) * has_prev, a_ref[...].astype(F32)], axis=0)
        e1 = pltpu.roll(ext, 1, 0)[HALO:]
        e2 = pltpu.roll(ext, 2, 0)[HALO:]
        pre = e2 * cw_ref[0:1, :] + e1 * cw_ref[1:2, :] + ext[HALO:] * cw_ref[2:3, :] + cb_ref[...]
        o_ref[...] = (_gelu(pre) * v_ref[...].astype(F32)).astype(o_ref.dtype)

    blk = pl.BlockSpec((t, tf), lambda j, i: (i, j))
    halo = pl.BlockSpec((HALO, tf), lambda j, i: (jnp.maximum(i * (t // HALO) - 1, 0), j))
    return pl.pallas_call(
        body, name=name, grid=(f // tf, s // t),
        in_specs=[blk, halo, blk, pl.BlockSpec((3, tf), lambda j, i: (0, j)), pl.BlockSpec((1, tf), lambda j, i: (0, j))],
        out_specs=blk, out_shape=jax.ShapeDtypeStruct((s, f), BF16),
        compiler_params=_params(("parallel", "parallel")),
    )(ua, ua, uv, cw, cb)


def _glu_bwd(name, ua, uv, dgl, cw, cb):
    s, f = ua.shape
    t, tf = GLU_TILE, _tile(f, 1408)
    nt = s // t
    te = t + HALO

    def body(a_ref, ah_ref, an_ref, v_ref, vn_ref, d_ref, dn_ref, cw_ref, cb_ref, da_ref, dv_ref, dcw_ref, dcb_ref):
        i = pl.program_id(1)

        @pl.when(i == 0)
        def _():
            dcw_ref[...] = jnp.zeros_like(dcw_ref)
            dcb_ref[...] = jnp.zeros_like(dcb_ref)

        has_prev = (i > 0).astype(F32)
        has_next = (i < nt - 1).astype(F32)
        ext = jnp.concatenate([ah_ref[...].astype(F32) * has_prev, a_ref[...].astype(F32), an_ref[...].astype(F32)], axis=0)
        e0 = ext[HALO:]
        e1 = pltpu.roll(ext, 1, 0)[HALO:]
        e2 = pltpu.roll(ext, 2, 0)[HALO:]
        c0, c1, c2 = cw_ref[0:1, :], cw_ref[1:2, :], cw_ref[2:3, :]
        pre = e2 * c0 + e1 * c1 + e0 * c2 + cb_ref[...]
        vx = jnp.concatenate([v_ref[...].astype(F32), vn_ref[...].astype(F32)], axis=0)
        dx = jnp.concatenate([d_ref[...].astype(F32), dn_ref[...].astype(F32) * has_next], axis=0)
        cdf = 0.5 * (1.0 + lax.erf(pre * INV_SQRT2))
        dpre = dx * vx * (cdf + pre * (INV_SQRT_2PI * jnp.exp(-0.5 * pre * pre)))
        up1 = pltpu.roll(dpre, te - 1, 0)
        up2 = pltpu.roll(dpre, te - 2, 0)
        da_ref[...] = (dpre * c2 + up1 * c1 + up2 * c0)[:t].astype(da_ref.dtype)
        dv_ref[...] = (dx * (pre * cdf))[:t].astype(dv_ref.dtype)
        dpt = dpre[:t]
        dcb_ref[...] += _colsum(dpt)
        dcw_ref[0:1, :] += _colsum(e2[:t] * dpt)
        dcw_ref[1:2, :] += _colsum(e1[:t] * dpt)
        dcw_ref[2:3, :] += _colsum(e0[:t] * dpt)

    blk = pl.BlockSpec((t, tf), lambda j, i: (i, j))
    prev = pl.BlockSpec((HALO, tf), lambda j, i: (jnp.maximum(i * (t // HALO) - 1, 0), j))
    nxt = pl.BlockSpec((HALO, tf), lambda j, i: (jnp.minimum((i + 1) * (t // HALO), s // HALO - 1), j))
    w3 = pl.BlockSpec((3, tf), lambda j, i: (0, j))
    w1 = pl.BlockSpec((1, tf), lambda j, i: (0, j))
    return pl.pallas_call(
        body, name=name, grid=(f // tf, nt),
        in_specs=[blk, prev, nxt, blk, nxt, blk, nxt, w3, w1],
        out_specs=(blk, blk, w3, w1),
        out_shape=(jax.ShapeDtypeStruct((s, f), BF16), jax.ShapeDtypeStruct((s, f), BF16),
                   jax.ShapeDtypeStruct((3, f), F32), jax.ShapeDtypeStruct((1, f), F32)),
        compiler_params=_params(("parallel", "arbitrary")),
    )(ua, ua, ua, uv, uv, dgl, dgl, cw, cb)


ATT_TILE = 512
ATT_ROWS = 256
LOG2E = 1.4426950408889634
LN2 = 0.6931471805599453


def _head_blocks_t(a, width):
    s = a.shape[0]
    t = min(ATT_TILE, s)
    return a.reshape(s // t, t, N_HEADS, width).transpose(2, 0, 3, 1)


def _head_rows(a):
    s = a.shape[0]
    t = min(ATT_TILE, s)
    r = a.reshape(s // t, t, N_HEADS, LANES)[..., 0].transpose(2, 0, 1)
    return jnp.broadcast_to(r[:, :, None, :], (N_HEADS, s // t, 8, t))


def _causal_mask(sv, q0, k0):
    row = q0 + lax.broadcasted_iota(jnp.int32, sv.shape, 0)
    col = k0 + lax.broadcasted_iota(jnp.int32, sv.shape, 1)
    return jnp.where(col <= row, sv, NEG_BIG)


def _attn_fwd(name, q_rot, kt4, v_ext):
    s = q_rot.shape[0]
    t = min(ATT_TILE, s)
    nq = s // t

    rq = min(ATT_ROWS, t)

    def body(q_ref, kt_ref, v_ref, o_ref, lse_ref, acc_ref, m_ref):
        qi = pl.program_id(1)
        acc_ref[...] = jnp.zeros_like(acc_ref)
        m_ref[...] = jnp.full_like(m_ref, NEG_BIG)

        def step(j, masked):
            v_blk = v_ref[pl.ds(pl.multiple_of(j * t, t), t), :]
            for r in range(t // rq):
                rs = pl.ds(r * rq, rq)
                sv = _dot(q_ref[rs, :], kt_ref[0, j], NN)
                if masked:
                    sv = _causal_mask(sv, r * rq, 0)
                m_prev = m_ref[rs, :]
                m_new = jnp.maximum(m_prev, jnp.max(sv, axis=-1, keepdims=True))
                p = jnp.exp2(sv - m_new).astype(BF16)
                acc_ref[rs, :] = jnp.exp2(m_prev - m_new) * acc_ref[rs, :] + _dot(p, v_blk, NN)
                m_ref[rs, :] = m_new

        def full_step(j, carry):
            step(j, False)
            return carry

        lax.fori_loop(0, qi, full_step, 0)
        step(qi, True)
        l = acc_ref[:, V_HEAD:V_HEAD + 1]
        o_ref[...] = (acc_ref[:, :V_HEAD] / l).astype(o_ref.dtype)
        lse_ref[...] = jnp.broadcast_to(m_ref[...] + jnp.log(l) * LOG2E, lse_ref.shape)

    head_q = pl.BlockSpec((t, Q_EXT), lambda h, i: (i, h))
    head_o = pl.BlockSpec((t, V_HEAD), lambda h, i: (i, h))
    return pl.pallas_call(
        body, name=name, grid=(N_HEADS, nq),
        in_specs=[head_q, pl.BlockSpec((1, nq, Q_EXT, t), lambda h, i: (h, 0, 0, 0)), pl.BlockSpec((s, Q_EXT), lambda h, i: (0, h))],
        out_specs=(head_o, head_o),
        out_shape=(jax.ShapeDtypeStruct((s, N_HEADS * V_HEAD), BF16), jax.ShapeDtypeStruct((s, N_HEADS * LANES), F32)),
        scratch_shapes=[pltpu.VMEM((t, Q_EXT), F32), pltpu.VMEM((t, 1), F32)],
        compiler_params=_params(("parallel", "parallel")),
    )(q_rot, kt4, v_ext)


def _attn_dq(name, q_rot, tabq, kt4, kfull, vt4, o, lse, do):
    s = q_rot.shape[0]
    t = min(ATT_TILE, s)
    nq = s // t

    def body(q_ref, tab_ref, kt_ref, k_ref, vt_ref, o_ref, lse_ref, do_ref, dq_ref, delta_ref, acc_ref):
        qi = pl.program_id(1)
        q = q_ref[...]
        dov = do_ref[...]
        delta = jnp.sum(dov.astype(F32) * o_ref[...].astype(F32), axis=-1, keepdims=True)
        lse = lse_ref[:, 0:1]
        acc_ref[...] = jnp.zeros_like(acc_ref)

        def step(j, masked):
            sv = _dot(q, kt_ref[0, j], NN)
            if masked:
                sv = _causal_mask(sv, qi * t, j * t)
            p = jnp.exp2(sv - lse)
            dp = _dot(dov, vt_ref[0, j], NN)
            ds = (p * (dp - delta)).astype(BF16)
            acc_ref[...] += _dot(ds, k_ref[pl.ds(pl.multiple_of(j * t, t), t), :], NN)

        def full_step(j, carry):
            step(j, False)
            return carry

        lax.fori_loop(0, qi, full_step, 0)
        step(qi, True)
        dq_ref[...] = (acc_ref[...] * (tab_ref[...] * LN2)).astype(dq_ref.dtype)
        delta_ref[...] = jnp.broadcast_to(delta, delta_ref.shape)

    head_q = pl.BlockSpec((t, Q_EXT), lambda h, i: (i, h))
    head_o = pl.BlockSpec((t, V_HEAD), lambda h, i: (i, h))
    return pl.pallas_call(
        body, name=name, grid=(N_HEADS, nq),
        in_specs=[head_q, pl.BlockSpec((t, Q_EXT), lambda h, i: (i, 0)), pl.BlockSpec((1, nq, Q_EXT, t), lambda h, i: (h, 0, 0, 0)),
                  pl.BlockSpec((s, Q_EXT), lambda h, i: (0, h)), pl.BlockSpec((1, nq, V_HEAD, t), lambda h, i: (h, 0, 0, 0)),
                  head_o, head_o, head_o],
        out_specs=(head_q, head_o),
        out_shape=(jax.ShapeDtypeStruct((s, N_HEADS * Q_EXT), BF16), jax.ShapeDtypeStruct((s, N_HEADS * LANES), F32)),
        scratch_shapes=[pltpu.VMEM((t, Q_EXT), F32)],
        compiler_params=_params(("parallel", "parallel")),
    )(q_rot, tabq, kt4, kfull, vt4, o, lse, do)


def _attn_dkv(name, kfull, v, qt4, q_rot, dot4, do, lse_row, delta_row, acc_in=None):
    s = kfull.shape[0]
    t = min(ATT_TILE, s)
    nq = s // t
    has_in = acc_in is not None

    def body(*refs):
        k_ref, v_ref, qt_ref, q_ref, dot_ref, do_ref, lse_ref, delta_ref = refs[:8]
        dkn_ref, dkd_ref, dv_ref, acck_ref, accv_ref = refs[-5:]
        kj, h = pl.program_id(0), pl.program_id(1)
        k_blk, v_blk = k_ref[...], v_ref[...]
        acck_ref[...] = jnp.zeros_like(acck_ref)
        accv_ref[...] = jnp.zeros_like(accv_ref)

        def step(i, masked):
            qs = pl.ds(pl.multiple_of(i * t, t), t)
            st = _dot(k_blk, qt_ref[0, i], NN)
            if masked:
                krow = lax.broadcasted_iota(jnp.int32, st.shape, 0)
                qcol = lax.broadcasted_iota(jnp.int32, st.shape, 1)
                st = jnp.where(krow <= qcol, st, NEG_BIG)
            pt = jnp.exp2(st - lse_ref[0, i, 0:1, :])
            accv_ref[...] += _dot(pt.astype(BF16), do_ref[qs, :], NN)
            dpt = _dot(v_blk, dot_ref[0, i], NN)
            dst = (pt * (dpt - delta_ref[0, i, 0:1, :])).astype(BF16)
            acck_ref[...] += _dot(dst, q_ref[qs, :], NN)

        def full_step(i, carry):
            step(i, False)
            return carry

        step(kj, True)
        lax.fori_loop(kj + 1, nq, full_step, 0)
        dk = acck_ref[...] * LN2
        dkn, dkd = dk[:, :QK_NOPE], dk[:, QK_NOPE:]
        if has_in:
            dkn = dkn + refs[8][...]
            dv_ref[...] = accv_ref[...] + refs[10][...]
        else:
            dv_ref[...] = accv_ref[...]
        dkn_ref[...] = dkn

        @pl.when(h == 0)
        def _():
            if has_in:
                dkd_ref[...] = dkd + refs[9][...]
            else:
                dkd_ref[...] = dkd

        @pl.when(h > 0)
        def _():
            dkd_ref[...] += dkd

    kblk = pl.BlockSpec((t, LANES), lambda j, h: (j, h))
    kdblk = pl.BlockSpec((t, LANES), lambda j, h: (j, 0))
    col = pl.BlockSpec((s, LANES), lambda j, h: (0, h))
    stat = pl.BlockSpec((1, nq, 8, t), lambda j, h: (h, 0, 0, 0))
    ins = [kfull, v, qt4, q_rot, dot4, do, lse_row, delta_row]
    in_specs = [pl.BlockSpec((t, Q_EXT), lambda j, h: (j, h)), kblk, pl.BlockSpec((1, nq, Q_EXT, t), lambda j, h: (h, 0, 0, 0)),
                pl.BlockSpec((s, Q_EXT), lambda j, h: (0, h)), pl.BlockSpec((1, nq, V_HEAD, t), lambda j, h: (h, 0, 0, 0)), col, stat, stat]
    if has_in:
        ins += list(acc_in)
        in_specs += [kblk, kdblk, kblk]
    return pl.pallas_call(
        body, name=name, grid=(nq, N_HEADS), in_specs=in_specs, out_specs=(kblk, kdblk, kblk),
        out_shape=(jax.ShapeDtypeStruct((s, N_HEADS * LANES), F32), jax.ShapeDtypeStruct((s, LANES), F32),
                   jax.ShapeDtypeStruct((s, N_HEADS * LANES), F32)),
        scratch_shapes=[pltpu.VMEM((t, Q_EXT), F32), pltpu.VMEM((t, LANES), F32)],
        compiler_params=_params(("parallel", "arbitrary")),
    )(*ins)


def _swap_halves(w):
    half = w.shape[-1] // 2
    return jnp.concatenate([-w[..., half:], w[..., :half]], axis=-1)


def _unswap_halves(g):
    half = g.shape[-1] // 2
    return jnp.concatenate([g[..., half:], -g[..., :half]], axis=-1)


def _extend_w_uq(w):
    r = w.reshape(Q_RANK, N_HEADS, QK_HEAD)
    rope = r[..., QK_NOPE:]
    return jnp.concatenate([r[..., :QK_NOPE], rope, _swap_halves(rope)], axis=-1).reshape(Q_RANK, N_HEADS * Q_EXT)


def _fold_w_uq_grad(g):
    r = g.reshape(Q_RANK, N_HEADS, Q_EXT)
    rope = r[..., QK_NOPE:QK_HEAD] + _unswap_halves(r[..., QK_HEAD:])
    return jnp.concatenate([r[..., :QK_NOPE], rope], axis=-1).reshape(Q_RANK, N_HEADS * QK_HEAD)


def _extend_w_dkv(w):
    return jnp.concatenate([w, _swap_halves(w[:, KV_RANK:])], axis=-1)


def _fold_w_dkv_grad(g):
    rope = g[:, KV_RANK:KV_RANK + QK_ROPE] + _unswap_halves(g[:, KV_RANK + QK_ROPE:])
    return jnp.concatenate([g[:, :KV_RANK], rope], axis=-1)


def _rope_tables(positions):
    inv = 1.0 / (ROPE_THETA ** (jnp.arange(0, QK_ROPE, 2, dtype=F32) / QK_ROPE))
    ang = positions.astype(F32)[:, None] * inv
    cos, sin = jnp.cos(ang), jnp.sin(ang)
    tabk = jnp.concatenate([cos, cos, sin, sin], axis=-1)
    scale = QK_HEAD ** -0.5 * LOG2E
    tabq = jnp.concatenate([jnp.full((positions.shape[0], QK_NOPE), scale, F32), tabk * scale], axis=-1)
    return tabq, tabk


def _forward_backward(x, target, mods, tabq, tabk, final_g, fetch, push):
    row = lambda vec: vec.reshape(1, -1)
    mod = [[row(mods[l, k * D_MODEL:(k + 1) * D_MODEL]) for k in range(N_MOD)] for l in range(DEPTH)]
    saved, weights = [], []
    kv = None
    for l in range(DEPTH):
        w, tok = fetch(l, x)
        sh1, sc1, g1, sh2, sc2, g2 = mod[l]
        sh1 = sh1 + tok
        if l == N_A_LAYERS:
            kvn = _rms_fwd("kvin_fwd", x, row(w["kv_in_g"]))
            kv_ext = _mm("dkv_fwd", kvn, w["w_dkv_ext"], out_dtype=F32)
            ckv = _rms_fwd("ckv_fwd", kv_ext, row(w["ckv_norm_g"]), ncols=KV_RANK)
            kd = _krope_fwd("krope_fwd", kv_ext, tabk)
            kn, v = _mm("uk_fwd", ckv, w["w_uk"]), _mm("uv_fwd", ckv, w["w_uv"])
            heads = lambda a: [a[:, h * LANES:(h + 1) * LANES] for h in range(N_HEADS)]
            kfull = jnp.concatenate([part for kh in heads(kn) for part in (kh, kd)], axis=-1)
            v_ext = jnp.concatenate([part for vh in heads(v) for part in (vh, jnp.ones_like(vh))], axis=-1)
            kv = dict(x=x, kvn=kvn, kv_ext=kv_ext, ckv=ckv, v=v, kfull=kfull, v_ext=v_ext,
                      kt4=_head_blocks_t(kfull, Q_EXT), vt4=_head_blocks_t(v, V_HEAD))
        x_in = x
        if l < N_A_LAYERS:
            h1 = _rms_fwd(f"norm1_fwd_{l}", x, row(w["norm1_g"]), sc1, sh1, out_dtype=F32)
            x_mid, zb, pooled = _pool_fwd(f"pool_fwd_{l}", h1, x, w["pool_w"], row(w["pool_b"]), row(w["pool_scale"]), g1)
            mix = (zb, pooled)
        else:
            h1 = _rms_fwd(f"norm1_fwd_{l}", x, row(w["norm1_g"]), sc1, sh1)
            cq_pre = _mm(f"dq_fwd_{l}", h1, w["w_dq"], out_dtype=F32)
            cq = _rms_fwd(f"qnorm_fwd_{l}", cq_pre, row(w["q_norm_g"]))
            q_rot = _mm(f"uq_fwd_{l}", cq, w["w_uq_ext"], rowtab=tabq)
            o, lse = _attn_fwd(f"attn_fwd_{l}", q_rot, kv["kt4"], kv["v_ext"])
            y, x_mid = _mm(f"wo_fwd_{l}", o, w["w_o"], resid=x, gate=g1)
            mix = (h1, cq_pre, cq, q_rot, o, lse, y)
        h2 = _rms_fwd(f"norm2_fwd_{l}", x_mid, row(w["norm2_g"]), sc2, sh2)
        w_up_a, w_up_v = w["w_up"](h2)
        ua = _mm(f"up_a_fwd_{l}", h2, w_up_a)
        uv = _mm(f"up_v_fwd_{l}", h2, w_up_v)
        gl = _glu_fwd(f"glu_fwd_{l}", ua, uv, w["conv_w"], row(w["conv_b"]))
        w_down = w["w_down"](gl)
        y2, x = _mm(f"down_fwd_{l}", gl, w_down, resid=x_mid, gate=g2)
        saved.append((x_in, x_mid, h2, ua, uv, gl, y2, mix))
        weights.append(dict(w, w_up_a=w_up_a, w_up_v=w_up_v, w_down=w_down))

    dx, dfinal_g, loss = _loss_head("loss_head", x, row(final_g), target)
    g = {"final_g": dfinal_g.reshape(-1)}
    per_layer = {k: [None] * DEPTH for k in ("norm1_g", "norm2_g", "conv_w", "conv_b")}
    per_a = {k: [None] * N_A_LAYERS for k in ("pool_b", "pool_scale")}
    per_b = {k: [None] * N_B_LAYERS for k in ("q_norm_g",)}
    dmods = [None] * DEPTH
    dkv = None
    tok = 0.0
    for l in reversed(range(DEPTH)):
        w, big = weights[l], {}
        sh1, sc1, g1, sh2, sc2, g2 = mod[l]
        g2 = g2 + tok
        x_in, x_mid, h2, ua, uv, gl, y2, mix = saved[l]
        dy2, dg2 = _gate_bwd(f"gate2_bwd_{l}", dx, y2, g2)
        dgl = _mm(f"down_bwd_{l}", dy2, w["w_down"], mode="nt")
        tok = push(l, "down", dict(w_down=_mm(f"down_wgrad_{l}", gl, dy2, mode="tn", out_dtype=F32, tm_cap=1408)), None)
        da, dv_, dcw, dcb = _glu_bwd(f"glu_bwd_{l}", ua, uv, dgl, w["conv_w"], row(w["conv_b"]) + tok)
        dh2 = _mm(f"up_a_bwd_{l}", da, w["w_up_a"], mode="nt", out_dtype=F32)
        dh2 = _mm(f"up_v_bwd_{l}", dv_, w["w_up_v"], mode="nt", out_dtype=F32, add=dh2)
        tok = push(l, "up", dict(w_up_a=_mm(f"up_a_wgrad_{l}", h2, da, mode="tn", out_dtype=F32),
                                 w_up_v=_mm(f"up_v_wgrad_{l}", h2, dv_, mode="tn", out_dtype=F32)), None)
        per_layer["conv_w"][l], per_layer["conv_b"][l] = dcw, dcb.reshape(-1)
        dx_mid, dn2, dsh2, dsc2 = _rms_bwd(f"norm2_bwd_{l}", x_mid, row(w["norm2_g"]), dh2, sc2 + tok, dx_in=dx)
        per_layer["norm2_g"][l] = dn2.reshape(-1)
        if l < N_A_LAYERS:
            zb, pooled = mix
            dh1, dpw, dpb, dps, dg1 = _pool_bwd(f"pool_bwd_{l}", dx_mid, zb, pooled, w["pool_w"], row(w["pool_scale"]), g1)
            big["pool_w"] = dpw
            per_a["pool_b"][l], per_a["pool_scale"][l] = dpb.reshape(-1), dps.reshape(-1)
        else:
            j = l - N_A_LAYERS
            h1, cq_pre, cq, q_rot, o, lse, y = mix
            dy, dg1 = _gate_bwd(f"gate1_bwd_{l}", dx_mid, y, g1)
            do = _mm(f"wo_bwd_{l}", dy, w["w_o"], mode="nt")
            big["w_o"] = _mm(f"wo_wgrad_{l}", o, dy, mode="tn", out_dtype=F32)
            dq_ext, delta = _attn_dq(f"attn_dq_{l}", q_rot, tabq, kv["kt4"], kv["kfull"], kv["vt4"], o, lse, do)
            dkv = _attn_dkv(f"attn_dkv_{l}", kv["kfull"], kv["v"], _head_blocks_t(q_rot, Q_EXT), q_rot, _head_blocks_t(do, V_HEAD), do,
                            _head_rows(lse), _head_rows(delta), acc_in=dkv)
            dcq = _mm(f"uq_bwd_{l}", dq_ext, w["w_uq_ext"], mode="nt", out_dtype=F32)
            big["w_uq_ext"] = _mm(f"uq_wgrad_{l}", cq, dq_ext, mode="tn", out_dtype=F32)
            dcq_pre, dqn = _rms_bwd(f"qnorm_bwd_{l}", cq_pre, row(w["q_norm_g"]), dcq, out_dtype=BF16)
            per_b["q_norm_g"][j] = dqn.reshape(-1)
            dh1 = _mm(f"dq_bwd_{l}", dcq_pre, w["w_dq"], mode="nt")
            big["w_dq"] = _mm(f"dq_wgrad_{l}", h1, dcq_pre, mode="tn", out_dtype=F32)
        dx, dn1, dsh1, dsc1 = _rms_bwd(f"norm1_bwd_{l}", x_in, row(w["norm1_g"]), dh1, sc1, dx_in=dx_mid)
        per_layer["norm1_g"][l] = dn1.reshape(-1)
        dmods[l] = jnp.concatenate([dsh1, dsc1, dg1, dsh2, dsc2, dg2], axis=-1).reshape(-1)
        if l == N_A_LAYERS:
            dkn, dkd, dv = dkv
            dckv = _mm("uk_bwd", dkn, w["w_uk"], mode="nt", out_dtype=F32)
            dckv = _mm("uv_bwd", dv, w["w_uv"], mode="nt", out_dtype=F32, add=dckv)
            big["w_uk"] = _mm("uk_wgrad", kv["ckv"], dkn, mode="tn", out_dtype=F32)
            big["w_uv"] = _mm("uv_wgrad", kv["ckv"], dv, mode="tn", out_dtype=F32)
            dkr = _krope_bwd("krope_bwd", dkd, tabk)
            dc, dckv_g = _rms_bwd("ckv_bwd", kv["kv_ext"], row(w["ckv_norm_g"]), dckv, ncols=KV_RANK, out_dtype=BF16)
            dkv_ext = jnp.concatenate([dc, dkr.astype(BF16)], axis=-1)
            dkvn = _mm("dkv_bwd", dkv_ext, w["w_dkv_ext"], mode="nt")
            big["w_dkv_ext"] = _mm("dkv_wgrad", kv["kvn"], dkv_ext, mode="tn", out_dtype=F32)
            dx, dkv_in_g = _rms_bwd("kvin_bwd", kv["x"], row(w["kv_in_g"]), dkvn, dx_in=dx)
            g["ckv_norm_g"], g["kv_in_g"] = dckv_g.reshape(-1), dkv_in_g.reshape(-1)
        tok = push(l, "mix", big, dx)
    for group in (per_layer, per_a, per_b):
        for k, vals in group.items():
            g[k] = jnp.stack(vals)
    return loss, dx, g, jnp.stack(dmods)


def _my_index():
    return 4 * lax.axis_index("x") + 2 * lax.axis_index("y") + lax.axis_index("c")


def _peer(k):
    x, y, c = lax.axis_index("x"), lax.axis_index("y"), lax.axis_index("c")
    return (1 - x if k & 4 else x, 1 - y if k & 2 else y, 1 - c if k & 1 else c)


def _index_of(pos):
    return 4 * pos[0] + 2 * pos[1] + pos[2]


def _exchange_many(name, arrays, scatter):
    n = len(arrays)
    blocks = [tuple(a.shape[1:]) if scatter else tuple(a.shape) for a in arrays]

    def body(*refs):
        x_refs, o_refs = refs[:n], refs[n:2 * n]
        send_sems, recv_sems, local_sems = refs[2 * n:]
        me = _my_index()
        started = []
        for a in range(n):
            mine = pltpu.make_async_copy(x_refs[a].at[me] if scatter else x_refs[a], o_refs[a].at[me], local_sems.at[a])
            mine.start()
            started.append(mine)
        sends = []
        for k in range(1, N_DEV):
            peer = _peer(k)
            for a in range(n):
                cp = pltpu.make_async_remote_copy(
                    src_ref=x_refs[a].at[_index_of(peer)] if scatter else x_refs[a], dst_ref=o_refs[a].at[me],
                    send_sem=send_sems.at[a, k - 1], recv_sem=recv_sems.at[a, k - 1], device_id=peer, device_id_type=MESH)
                cp.start()
                sends.append(cp)
        for k in range(1, N_DEV):
            peer = _peer(k)
            for a in range(n):
                pltpu.make_async_remote_copy(
                    src_ref=x_refs[a].at[me] if scatter else x_refs[a], dst_ref=o_refs[a].at[_index_of(peer)],
                    send_sem=send_sems.at[a, k - 1], recv_sem=recv_sems.at[a, k - 1], device_id=peer, device_id_type=MESH).wait_recv()
        for cp in sends:
            cp.wait_send()
        for mine in started:
            mine.wait()

    return pl.pallas_call(
        body, name=name, out_shape=tuple(jax.ShapeDtypeStruct((N_DEV,) + blk, a.dtype) for blk, a in zip(blocks, arrays)),
        in_specs=[pl.BlockSpec(memory_space=pl.ANY)] * n, out_specs=tuple([pl.BlockSpec(memory_space=pl.ANY)] * n),
        scratch_shapes=[pltpu.SemaphoreType.DMA((n, N_DEV - 1)), pltpu.SemaphoreType.DMA((n, N_DEV - 1)), pltpu.SemaphoreType.DMA((n,))],
    )(*arrays)


def _exchange(name, x, scatter):
    return _exchange_many(name, [x], scatter)[0]


HBM_SPEC = pl.BlockSpec(memory_space=pltpu.HBM)
SEM_SPEC = pl.BlockSpec(memory_space=pltpu.SEMAPHORE)
DATAFLOW = pltpu.SideEffectType.DATAFLOW_SIDE_EFFECTING


def _remote_copies(x_refs, land_refs, send_sems, recv_sems, scatter, numbers=None):
    me = _my_index()
    numbers = list(range(len(x_refs))) if numbers is None else numbers
    out, inc = [], []
    for k in range(1, N_DEV):
        peer = _peer(k)
        for a in range(len(x_refs)):
            pair = numbers[a] * (N_DEV - 1) + k - 1
            sems = dict(send_sem=send_sems.at[pair], recv_sem=recv_sems.at[pair], device_id=peer, device_id_type=MESH)
            out.append(pltpu.make_async_remote_copy(
                src_ref=x_refs[a].at[_index_of(peer)] if scatter else x_refs[a], dst_ref=land_refs[a].at[me], **sems))
            inc.append(pltpu.make_async_remote_copy(
                src_ref=x_refs[a].at[me] if scatter else x_refs[a], dst_ref=land_refs[a].at[_index_of(peer)], **sems))
    return out, inc


def _exchange_start(name, arrays, scatter):
    n = len(arrays)
    blocks = [tuple(a.shape[1:]) if scatter else tuple(a.shape) for a in arrays]

    def body(*refs):
        x_refs, land_refs = refs[:n], refs[n:2 * n]
        send_sems, recv_sems = refs[2 * n], refs[2 * n + 1]
        for cp in _remote_copies(x_refs, land_refs, send_sems, recv_sems, scatter)[0]:
            cp.start()
        refs[-1][...] = jnp.zeros_like(refs[-1])

    sem_type = pltpu.SemaphoreType.DMA((n * (N_DEV - 1),))
    lands =[pltpu.with_memory_space_constraint(lax.empty((N_DEV,) + blk, a.dtype), pltpu.HBM) for blk, a in zip(blocks, arrays)]
    srcs = [pltpu.with_memory_space_constraint(a, pltpu.HBM) for a in arrays]
    res = pl.pallas_call(
        body, name=name,
        out_shape=(sem_type, sem_type, *[pltpu.HBM(a.shape, a.dtype) for a in srcs + lands], jax.ShapeDtypeStruct((8, LANES), F32)),
        in_specs=[HBM_SPEC] * (2 * n), out_specs=(SEM_SPEC, SEM_SPEC, *[HBM_SPEC] * (2 * n), pl.BlockSpec(memory_space=pltpu.VMEM)),
        input_output_aliases={i: 2 + i for i in range(2 * n)},
        compiler_params=pltpu.CompilerParams(has_side_effects=DATAFLOW),
    )(*srcs, *lands)
    return (res[0], res[1], list(res[2:2 + n]), list(res[2 + n:2 + 2 * n])), res[-1]


def _exchange_wait(name, handles, after, scatter, which=None):
    send_sems, recv_sems, srcs, lands = handles
    which = list(range(len(srcs))) if which is None else list(which)
    srcs, lands = [srcs[a] for a in which], [lands[a] for a in which]
    n = len(srcs)

    def body(*refs):
        x_refs, land_refs = refs[:n], refs[n:2 * n]
        out, inc = _remote_copies(x_refs, land_refs, refs[2 * n], refs[2 * n + 1], scatter, which)
        for cp in out:
            cp.wait_send()
        for cp in inc:
            cp.wait_recv()

    res = pl.pallas_call(
        body, name=name, out_shape=tuple(pltpu.HBM(a.shape, a.dtype) for a in srcs + lands),
        in_specs=[HBM_SPEC] * (2 * n) + [SEM_SPEC, SEM_SPEC, pl.BlockSpec(memory_space=pl.ANY)], out_specs=tuple([HBM_SPEC] * (2 * n)),
        input_output_aliases={i: i for i in range(2 * n)},
        compiler_params=pltpu.CompilerParams(has_side_effects=DATAFLOW),
    )(*srcs, *lands, send_sems, recv_sems, after)
    return list(res[n:])


def _pack(arrays, dtype, row_multiple):
    flat = jnp.concatenate([a.astype(dtype).reshape(-1) for a in arrays])
    rows = -(-flat.shape[0] // (LANES * row_multiple)) * row_multiple
    return jnp.pad(flat, (0, rows * LANES - flat.shape[0])).reshape(rows, LANES)


def _unpack(packed, shapes):
    lead = packed.shape[:-2]
    flat = packed.reshape(lead + (-1,))
    out, off = [], 0
    for shp in shapes:
        size = 1
        for d in shp:
            size *= d
        out.append(flat[..., off:off + size].reshape(lead + tuple(shp)))
        off += size
    return out


def _unshard(g8, axis):
    return jnp.concatenate([g8[j] for j in range(N_DEV)], axis=axis)


def _shard8(full, axis):
    n = full.shape[axis] // N_DEV
    return jnp.stack([lax.slice_in_dim(full, j * n, (j + 1) * n, axis=axis) for j in range(N_DEV)])


VECTOR_WEIGHTS = (("pool_b", 1), ("pool_scale", 1), ("conv_w", 2))
REPLICATED_WEIGHTS = ("norm1_g", "norm2_g", "kv_in_g", "ckv_norm_g", "q_norm_g", "conv_b", "final_g")
WEIGHT_ORDER = ("mod_w", "mod_b", "norm1_g", "norm2_g", "pool_w", "pool_b", "pool_scale", "kv_in_g", "w_dkv", "ckv_norm_g", "w_uk",
                "w_uv", "w_dq", "q_norm_g", "w_uq", "w_o", "w_up", "conv_w", "conv_b", "w_down", "final_g")
BIG_ROW_MULTIPLE = 1024
SMALL_ROW_MULTIPLE = 16


def _as_2d(a):
    if a.ndim == 1:
        return a.reshape(-1, LANES)
    return a.reshape(-1, a.shape[-1])


def kernel(x, c, positions, mod_w, mod_b, norm1_g, norm2_g, pool_w, pool_b, pool_scale, kv_in_g, w_dkv, ckv_norm_g, w_uk, w_uv, w_dq, q_norm_g, w_uq, w_o, w_up, conv_w, conv_b, w_down, final_g, loss_target, m_mod_w, m_mod_b, m_norm1_g, m_norm2_g, m_pool_w, m_pool_b, m_pool_scale, m_kv_in_g, m_w_dkv, m_ckv_norm_g, m_w_uk, m_w_uv, m_w_dq, m_q_norm_g, m_w_uq, m_w_o, m_w_up, m_conv_w, m_conv_b, m_w_down, m_final_g, v_mod_w, v_mod_b, v_norm1_g, v_norm2_g, v_pool_w, v_pool_b, v_pool_scale, v_kv_in_g, v_w_dkv, v_ckv_norm_g, v_w_uk, v_w_uv, v_w_dq, v_q_norm_g, v_w_uq, v_w_o, v_w_up, v_conv_w, v_conv_b, v_w_down, v_final_g):
    shard = dict(mod_w=mod_w, mod_b=mod_b, norm1_g=norm1_g, norm2_g=norm2_g, pool_w=pool_w, pool_b=pool_b, pool_scale=pool_scale,
                 kv_in_g=kv_in_g, w_dkv=w_dkv, ckv_norm_g=ckv_norm_g, w_uk=w_uk, w_uv=w_uv, w_dq=w_dq, q_norm_g=q_norm_g, w_uq=w_uq,
                 w_o=w_o, w_up=w_up, conv_w=conv_w, conv_b=conv_b, w_down=w_down, final_g=final_g)
    mom_m = dict(mod_w=m_mod_w, mod_b=m_mod_b, norm1_g=m_norm1_g, norm2_g=m_norm2_g, pool_w=m_pool_w, pool_b=m_pool_b,
                 pool_scale=m_pool_scale, kv_in_g=m_kv_in_g, w_dkv=m_w_dkv, ckv_norm_g=m_ckv_norm_g, w_uk=m_w_uk, w_uv=m_w_uv,
                 w_dq=m_w_dq, q_norm_g=m_q_norm_g, w_uq=m_w_uq, w_o=m_w_o, w_up=m_w_up, conv_w=m_conv_w, conv_b=m_conv_b,
                 w_down=m_w_down, final_g=m_final_g)
    mom_v = dict(mod_w=v_mod_w, mod_b=v_mod_b, norm1_g=v_norm1_g, norm2_g=v_norm2_g, pool_w=v_pool_w, pool_b=v_pool_b,
                 pool_scale=v_pool_scale, kv_in_g=v_kv_in_g, w_dkv=v_w_dkv, ckv_norm_g=v_ckv_norm_g, w_uk=v_w_uk, w_uv=v_w_uv,
                 w_dq=v_w_dq, q_norm_g=v_q_norm_g, w_uq=v_w_uq, w_o=v_w_o, w_up=v_w_up, conv_w=v_conv_w, conv_b=v_conv_b,
                 w_down=v_w_down, final_g=v_final_g)
    me = _my_index()
    d6 = N_MOD * D_MODEL
    mod_cols = d6 // N_DEV

    small_in = [c] + [shard[k] for k, _ in VECTOR_WEIGHTS]
    small_all = _exchange("gather_vectors", _pack(small_in, F32, SMALL_ROW_MULTIPLE), scatter=False)
    parts = _unpack(small_all, [a.shape for a in small_in])
    c_all = jnp.pad(parts[0].reshape(N_DEV, D_MODEL), ((0, N_DEV), (0, 0)))
    vec = {k: _unshard(p, ax) for (k, ax), p in zip(VECTOR_WEIGHTS, parts[1:])}

    my_mod_b = lax.dynamic_slice_in_dim(mod_b, me * mod_cols, mod_cols, axis=1)
    mods_mine = _mods_fwd("mods_fwd", c_all, mod_w, my_mod_b)
    mods_all = _exchange("gather_mods", _pack([mods_mine], F32, SMALL_ROW_MULTIPLE), scatter=False)
    mods_all = _unpack(mods_all, [mods_mine.shape])[0]
    mods = lax.dynamic_index_in_dim(mods_all, me, axis=2, keepdims=False)
    mods = jnp.moveaxis(mods, 0, 1).reshape(DEPTH, d6)

    tabq, tabk = _rope_tables(positions[0])
    half = N_DEV // 2
    up_cols = shard["w_up"].shape[2]
    cat = lambda a, axis, lo=0, hi=N_DEV: jnp.concatenate([a[j] for j in range(lo, hi)], axis=axis)

    def stage_pieces(l):
        out = {"pool_w": shard["pool_w"].astype(BF16)} if l == 0 else {}
        out.update(w_up=shard["w_up"][l].astype(BF16), w_down=shard["w_down"][l].astype(BF16))
        if l >= N_A_LAYERS:
            out.update({k: shard[k][l - N_A_LAYERS].astype(BF16) for k in ("w_dq", "w_uq", "w_o")})
        if l == N_A_LAYERS:
            out.update({k: shard[k].astype(BF16) for k in ("w_dkv", "w_uk", "w_uv")})
        return out

    gathers, pool_all = {}, []

    def start_gather(l, behind=None):
        pieces = stage_pieces(l)
        if behind is not None:
            pieces, _ = lax.optimization_barrier((pieces, behind))
        handles, token = _exchange_start(f"gather_start_{l}", list(pieces.values()), scatter=False)
        gathers[l] = (handles, pieces)
        return token[0, 0]

    def wait_gather(l, keys, after, tag=""):
        handles, pieces = gathers[l]
        which = [list(pieces).index(k) for k in keys]
        lands = _exchange_wait(f"gather_wait_{l}{tag}", handles, after, scatter=False, which=which)
        return dict(zip(keys, own_slot(lands, [pieces[k] for k in keys])))

    def whole_weights(l, got):
        w = dict(norm1_g=norm1_g[l], norm2_g=norm2_g[l], conv_w=vec["conv_w"][l], conv_b=conv_b[l])
        if l == 0:
            pool_all.append(got["pool_w"])
        if l < N_A_LAYERS:
            w.update(pool_w=cat(pool_all[0][:, l], 1), pool_b=vec["pool_b"][l], pool_scale=vec["pool_scale"][l])
        else:
            rope = got["w_uq"][..., QK_NOPE:]
            ext = jnp.concatenate([got["w_uq"][..., :QK_NOPE], rope, _swap_halves(rope)], axis=-1)
            w.update(w_dq=got["w_dq"].reshape(D_MODEL, Q_RANK), w_uq_ext=cat(ext, -1), w_o=got["w_o"].reshape(D_MODEL, D_MODEL),
                     q_norm_g=q_norm_g[l - N_A_LAYERS])
        if l == N_A_LAYERS:
            w.update(w_dkv_ext=_extend_w_dkv(got["w_dkv"].reshape(D_MODEL, KV_RANK + QK_ROPE)), w_uk=cat(got["w_uk"], -1),
                     w_uv=cat(got["w_uv"], -1), kv_in_g=kv_in_g, ckv_norm_g=ckv_norm_g)
        return w

    def own_slot(lands, own):
        return [lax.dynamic_update_index_in_dim(p, o, me, 0) for p, o in zip(lands, own)]

    def fetch(l, after):
        up_parts = lambda g8: (cat(g8, -1, 0, half), cat(g8, -1, half, N_DEV))
        if l == 0:
            start_gather(0, behind=mods)
            got = wait_gather(0, ["pool_w"], mods, "_pool")
            w_up = lambda aft: up_parts(wait_gather(0, ["w_up"], aft, "_up")["w_up"])
            w_down = lambda aft: wait_gather(0, ["w_down"], aft, "_down")["w_down"].reshape(D_FF, D_MODEL)
        else:
            got = wait_gather(l, list(gathers[l][1]), after)
            up, down = up_parts(got["w_up"]), got["w_down"].reshape(D_FF, D_MODEL)
            w_up, w_down = (lambda aft: up), (lambda aft: down)
        w = dict(whole_weights(l, got), w_up=w_up, w_down=w_down)
        return w, (start_gather(l + 1) if l + 1 < DEPTH else 0.0)

    scatters, pending, pool_grads, piece_grads = {}, {}, {}, {}

    def reduce_pieces(l, keys, got):
        for k, p in zip(keys, got):
            piece_grads[(k, l)] = _sum8(f"sum_grads_{k}_{l}", p.reshape(N_DEV, -1, p.shape[-1])).reshape(p.shape[1:])

    def start_scatter(name, sent):
        sent = {k: a.astype(BF16) for k, a in sent.items()}
        handles, token = _exchange_start(f"scatter_start_{name}", list(sent.values()), scatter=True)
        scatters[name] = (handles, list(sent), [lax.dynamic_index_in_dim(a, me, 0, keepdims=False) for a in sent.values()])
        return token[0, 0]

    def finish_scatter(name, l, after):
        handles, keys, own = scatters.pop(name)
        reduce_pieces(l, keys, own_slot(_exchange_wait(f"scatter_wait_{name}", handles, after, scatter=True), own))

    def push(l, part, big, after):
        cut = lambda a, n, axis: jnp.stack([lax.slice_in_dim(a, j * n, (j + 1) * n, axis=axis) for j in range(N_DEV)])
        sent = {}
        if part == "down":
            sent["w_down"] = big["w_down"].reshape(N_DEV, D_FF // N_DEV, D_MODEL)
        elif part == "up":
            sent["w_up"] = jnp.stack([lax.slice_in_dim(big[half_], j * up_cols, (j + 1) * up_cols, axis=1)
                                      for half_ in ("w_up_a", "w_up_v") for j in range(half)])
        elif l < N_A_LAYERS:
            pool_grads[l] = big["pool_w"]
        else:
            ext = cut(big["w_uq_ext"], Q_EXT, 1)
            rope = ext[..., QK_NOPE:QK_HEAD] + _unswap_halves(ext[..., QK_HEAD:])
            sent.update(w_dq=big["w_dq"].reshape(N_DEV, D_MODEL // N_DEV, Q_RANK), w_uq=jnp.concatenate([ext[..., :QK_NOPE], rope], axis=-1),
                        w_o=big["w_o"].reshape(N_DEV, D_MODEL // N_DEV, D_MODEL))
        if part == "mix" and l == N_A_LAYERS:
            sent.update(w_dkv=_fold_w_dkv_grad(big["w_dkv_ext"]).reshape(N_DEV, D_MODEL // N_DEV, KV_RANK + QK_ROPE),
                        w_uk=cut(big["w_uk"], QK_NOPE, 1), w_uv=cut(big["w_uv"], V_HEAD, 1))
        if l == 0 and part != "mix":
            return start_scatter(f"0_{part}", sent)
        if l == 0:
            finish_scatter("1", 1, after)
            pool = _shard8(jnp.stack([pool_grads[a] for a in range(N_A_LAYERS)]), 2).astype(BF16)
            reduce_pieces(0, ["pool_w"], _exchange_many("scatter_pool_grads", [pool], scatter=True))
            return 0.0
        pending.setdefault(l, {}).update(sent)
        if part != "mix":
            return 0.0
        if l + 1 < DEPTH:
            finish_scatter(str(l + 1), l + 1, after)
        return start_scatter(str(l), pending.pop(l))

    loss_row, dx, g, dmods = _forward_backward(x[0], loss_target[0], mods, tabq, tabk, final_g, fetch, push)
    layers_of = lambda k, ls: jnp.stack([piece_grads[(k, l)] for l in ls])
    grads = dict(w_dkv=piece_grads[("w_dkv", N_A_LAYERS)], w_uk=piece_grads[("w_uk", N_A_LAYERS)], w_uv=piece_grads[("w_uv", N_A_LAYERS)])
    for k in ("w_dq", "w_uq", "w_o"):
        grads[k] = layers_of(k, range(N_A_LAYERS, DEPTH))

    small_names = REPLICATED_WEIGHTS + tuple(k for k, _ in VECTOR_WEIGHTS)
    small_out = [dmods] + [g[k] for k in small_names] + [loss_row]
    small_shapes = [a.shape for a in small_out]
    small_got = _exchange("gather_small_grads", _pack(small_out, F32, SMALL_ROW_MULTIPLE), scatter=False)
    summed = _unpack(_sum8("sum_small_grads", small_got), small_shapes)
    grads["mod_b"] = summed[0]
    for k, s in zip(small_names, summed[1:-1]):
        grads[k] = s
    for k, ax in VECTOR_WEIGHTS:
        n = shard[k].shape[ax]
        grads[k] = lax.dynamic_slice_in_dim(grads[k], me * n, n, axis=ax)
    loss = summed[-1][0, 0]
    dmods_all = _unpack(small_got, small_shapes)[0]
    dm_mine = lax.dynamic_slice_in_dim(dmods_all, me * mod_cols, mod_cols, axis=2)
    dm_mine = jnp.pad(jnp.moveaxis(dm_mine, 0, 1), ((0, 0), (0, N_DEV), (0, 0)))
    grads["mod_w"] = _mods_bwd("mods_bwd", c_all, dm_mine)

    delta, new_m, new_v = {}, {}, {}

    def adamw(k):
        shp = shard[k].shape
        grads[k] = grads[k].reshape(shp)
        d_, m_, v_ = _adamw(f"adamw_{k}", _as_2d(shard[k]), _as_2d(grads[k]), _as_2d(mom_m[k]), _as_2d(mom_v[k]))
        delta[k], new_m[k], new_v[k] = d_.reshape(shp), m_.reshape(shp), v_.reshape(shp)

    late = ("w_up", "w_down", "pool_w")
    for k in WEIGHT_ORDER:
        if k not in late:
            adamw(k)
    finish_scatter("0_down", 0, delta["final_g"])
    finish_scatter("0_up", 0, delta["final_g"])
    grads.update(w_up=layers_of("w_up", range(DEPTH)), w_down=layers_of("w_down", range(DEPTH)), pool_w=piece_grads[("pool_w", 0)])
    for k in late:
        adamw(k)
    return (loss, dx[None], *[grads[k] for k in WEIGHT_ORDER], *[delta[k] for k in WEIGHT_ORDER],
            *[new_m[k] for k in WEIGHT_ORDER], *[new_v[k] for k in WEIGHT_ORDER])
```

```python
import functools

import jax
import jax.numpy as jnp
from jax import lax
from jax.experimental import pallas as pl
from jax.experimental.pallas import tpu as pltpu

F32 = jnp.float32
BF16 = jnp.bfloat16

D_MODEL = 1024
DEPTH = 4
N_A_LAYERS = 2
N_B_LAYERS = 2
POOL_WINDOWS = (2, 4, 8, 16)
POOL_GROUP = 256
N_HEADS = 8
QK_NOPE = 128
QK_ROPE = 64
V_HEAD = 128
QK_HEAD = QK_NOPE + QK_ROPE
Q_RANK = 384
KV_RANK = 256
ROPE_THETA = 10000.0
D_FF = 2816
EPS = 1e-6
N_MOD = 6
ADAM_LR = 0.001
ADAM_B1 = 0.9
ADAM_B2 = 0.999
ADAM_EPS = 1e-08
ADAM_WD = 0.01
ADAM_STEP = 10

N_DEV = 8
LANES = 128
Q_EXT = 256
VMEM_LIMIT_BYTES = 48 * 1024 * 1024
MESH = pl.DeviceIdType.MESH
NEG_BIG = -0.7 * float(jnp.finfo(jnp.float32).max)


def _params(sem):
    return pltpu.CompilerParams(dimension_semantics=sem, vmem_limit_bytes=VMEM_LIMIT_BYTES)


def _tile(n, cap):
    if n <= cap:
        return n
    best = None
    for d in range(LANES, cap + 1, LANES):
        if n % d == 0:
            best = d
    assert best is not None, (n, cap)
    return best


def _dot(a, b, dims):
    return lax.dot_general(a, b, (dims, ((), ())), preferred_element_type=F32)


NN = ((1,), (0,))
NT = ((1,), (1,))
TN = ((0,), (0,))


def _mm(name, a, b, mode="nn", out_dtype=BF16, add=None, resid=None, gate=None, rowtab=None,
        tm_cap=1024, tn_cap=1408, tk_cap=1408):
    if mode == "tn":
        kdim, m = a.shape
    else:
        m, kdim = a.shape
    n = b.shape[0] if mode == "nt" else b.shape[1]
    tm, tn, tk = _tile(m, tm_cap), _tile(n, tn_cap), _tile(kdim, tk_cap)
    nk = kdim // tk
    dims = {"nn": NN, "nt": NT, "tn": TN}[mode]
    a_spec = pl.BlockSpec((tk, tm), lambda i, j, k: (k, i)) if mode == "tn" else pl.BlockSpec((tm, tk), lambda i, j, k: (i, k))
    b_spec = pl.BlockSpec((tn, tk), lambda i, j, k: (j, k)) if mode == "nt" else pl.BlockSpec((tk, tn), lambda i, j, k: (k, j))
    o_spec = pl.BlockSpec((tm, tn), lambda i, j, k: (i, j))
    g_spec = pl.BlockSpec((1, tn), lambda i, j, k: (0, j))
    gated = resid is not None

    def body(*refs):
        a_ref, b_ref = refs[0], refs[1]
        acc = refs[-1]
        k = pl.program_id(2)

        @pl.when(k == 0)
        def _():
            acc[...] = jnp.zeros_like(acc)

        acc[...] += _dot(a_ref[...].astype(BF16), b_ref[...].astype(BF16), dims)

        @pl.when(k == nk - 1)
        def _():
            if gated:
                r_ref, g_ref, y_ref, x_ref = refs[2:6]
                y_ref[...] = acc[...]
                x_ref[...] = r_ref[...] + g_ref[...] * acc[...]
            elif add is not None:
                refs[3][...] = (acc[...] + refs[2][...].astype(F32)).astype(out_dtype)
            elif rowtab is not None:
                tab = refs[2][...]
                refs[3][...] = (acc[...] * jnp.concatenate([tab] * (tn // tab.shape[1]), axis=1)).astype(out_dtype)
            else:
                refs[2][...] = acc[...].astype(out_dtype)

    ins, in_specs = [a, b], [a_spec, b_spec]
    if rowtab is not None:
        assert tn % rowtab.shape[1] == 0 and not gated and add is None
        ins.append(rowtab)
        in_specs.append(pl.BlockSpec((tm, rowtab.shape[1]), lambda i, j, k: (i, 0)))
    if gated:
        ins += [resid, gate]
        in_specs += [o_spec, g_spec]
        out_shape = (jax.ShapeDtypeStruct((m, n), F32), jax.ShapeDtypeStruct((m, n), F32))
        out_specs = (o_spec, o_spec)
    else:
        if add is not None:
            ins.append(add)
            in_specs.append(o_spec)
        out_shape = jax.ShapeDtypeStruct((m, n), out_dtype)
        out_specs = o_spec
    return pl.pallas_call(
        body, name=name, grid=(m // tm, n // tn, nk), in_specs=in_specs, out_specs=out_specs, out_shape=out_shape,
        scratch_shapes=[pltpu.VMEM((tm, tn), F32)],
        compiler_params=_params(("parallel", "parallel", "arbitrary")),
    )(*ins)


def _rowwise(name, fn, tiled, bcast, outs, sums=(), tr=512):
    tiled = [t if isinstance(t, tuple) else (t, t.shape[1], 0) for t in tiled]
    s = tiled[0][0].shape[0]
    tr = min(tr, s)
    assert s % tr == 0
    n_t, n_b, n_o = len(tiled), len(bcast), len(outs)

    def body(*refs):
        i = pl.program_id(0)
        vals = [r[...] for r in refs[:n_t + n_b]]
        o_vals, s_vals = fn(*vals)
        for r, v in zip(refs[n_t + n_b:n_t + n_b + n_o], o_vals):
            r[...] = v.astype(r.dtype)
        s_refs = refs[n_t + n_b + n_o:]

        @pl.when(i == 0)
        def _():
            for r in s_refs:
                r[...] = jnp.zeros_like(r)

        for r, v in zip(s_refs, s_vals):
            r[...] += v

    in_specs = [pl.BlockSpec((tr, n), functools.partial(lambda cb, i: (i, cb), cb)) for (_, n, cb) in tiled]
    in_specs += [pl.BlockSpec(b.shape, functools.partial(lambda nd, i: (0,) * nd, b.ndim)) for b in bcast]
    out_specs = [pl.BlockSpec((tr, n), lambda i: (i, 0)) for (n, _) in outs]
    out_specs += [pl.BlockSpec((1, n), lambda i: (0, 0)) for n in sums]
    out_shape = [jax.ShapeDtypeStruct((s, n), dt) for (n, dt) in outs]
    out_shape += [jax.ShapeDtypeStruct((1, n), F32) for n in sums]
    res = pl.pallas_call(
        body, name=name, grid=(s // tr,), in_specs=in_specs, out_specs=tuple(out_specs), out_shape=tuple(out_shape),
        compiler_params=_params(("arbitrary",)),
    )(*[t[0] for t in tiled], *bcast)
    return res


def _colsum(v):
    return jnp.sum(v, axis=0, keepdims=True)


def _rms_fwd(name, x, g, scale=None, shift=None, out_dtype=BF16, ncols=None):
    mod = scale is not None

    def fn(xv, gv, *ss):
        y = xv * lax.rsqrt(jnp.mean(xv * xv, axis=-1, keepdims=True) + EPS) * gv
        if mod:
            y = y * (1.0 + ss[0]) + ss[1]
        return (y,), ()

    n = ncols or x.shape[1]
    return _rowwise(name, fn, [(x, n, 0)], [g] + ([scale, shift] if mod else []), [(n, out_dtype)])[0]


def _rms_bwd(name, x, g, dh, scale=None, dx_in=None, ncols=None, out_dtype=F32):
    mod = scale is not None
    has_in = dx_in is not None

    def fn(*vals):
        xv, dhv = vals[0], vals[1].astype(F32)
        rest = list(vals[2:])
        dxi = rest.pop(0) if has_in else None
        gv = rest.pop(0)
        rstd = lax.rsqrt(jnp.mean(xv * xv, axis=-1, keepdims=True) + EPS)
        xhat = xv * rstd
        sums = []
        if mod:
            sc = rest.pop(0)
            dyn = dhv * (1.0 + sc)
            dshift, dscale = _colsum(dhv), _colsum(dhv * (xhat * gv))
        else:
            dyn = dhv
        dg = _colsum(dyn * xhat)
        dxhat = dyn * gv
        dx = rstd * (dxhat - xhat * jnp.mean(dxhat * xhat, axis=-1, keepdims=True))
        if has_in:
            dx = dx + dxi
        sums = [dg] + ([dshift, dscale] if mod else [])
        return (dx,), sums

    n = ncols or x.shape[1]
    tiled = [(x, n, 0), dh] + ([dx_in] if has_in else [])
    return _rowwise(name, fn, tiled, [g] + ([scale] if mod else []), [(n, out_dtype)], [n] * (3 if mod else 1))


def _gate_bwd(name, dxn, y, g):
    def fn(dv, yv, gv):
        return (gv * dv,), (_colsum(dv * yv),)

    n = dxn.shape[1]
    return _rowwise(name, fn, [dxn, y], [g], [(n, BF16)], [n])


def _loss_head(name, x, g, target):
    n = x.shape[1]

    def fn(xv, tv, gv):
        rstd = lax.rsqrt(jnp.mean(xv * xv, axis=-1, keepdims=True) + EPS)
        xhat = xv * rstd
        err = xhat * gv - tv
        loss = 0.5 * jnp.sum(jnp.sum(err * err, axis=-1, keepdims=True) / n, axis=0, keepdims=True)
        dy = err / n
        dg = _colsum(dy * xhat)
        dxhat = dy * gv
        dx = rstd * (dxhat - xhat * jnp.mean(dxhat * xhat, axis=-1, keepdims=True))
        return (dx,), (dg, jnp.broadcast_to(loss, (1, LANES)))

    return _rowwise(name, fn, [x, target], [g], [(n, F32)], [n, LANES])


def _krope_fwd(name, kv_ext, tabk):
    def fn(xv, tv):
        t = xv * tv
        return (t + pltpu.roll(t, 64, 1),), ()

    return _rowwise(name, fn, [(kv_ext, LANES, 2), tabk], [], [(LANES, BF16)])[0]


def _krope_bwd(name, dkd, tabk):
    def fn(dv, tv):
        return ((dv + pltpu.roll(dv, 64, 1)) * tv,), ()

    return _rowwise(name, fn, [dkd, tabk], [], [(LANES, F32)])[0]


def _adamw(name, w, g, m, v):
    def fn(wv, gv, mv, vv):
        m2 = ADAM_B1 * mv + (1.0 - ADAM_B1) * gv
        v2 = ADAM_B2 * vv + (1.0 - ADAM_B2) * (gv * gv)
        m_hat = m2 / (1.0 - ADAM_B1 ** ADAM_STEP)
        v_hat = v2 / (1.0 - ADAM_B2 ** ADAM_STEP)
        delta = -ADAM_LR * (m_hat / (jnp.sqrt(v_hat) + ADAM_EPS) + ADAM_WD * wv)
        return (delta, m2, v2), ()

    r, c = w.shape
    tr = r
    for cand in (512, 256, 128, 64, 32, 16, 8):
        if r % cand == 0 and r > cand:
            tr = cand
            break
    return _rowwise(name, fn, [w, g, m, v], [], [(c, F32)] * 3, tr=tr)


def _sum8(name, parts):
    _, r, c = parts.shape
    tr = r
    for cand in (2048, 1024, 512, 256, 128, 64, 32, 16):
        if r % cand == 0 and r > cand and cand * c <= 256 * 1024:
            tr = cand
            break

    def body(p_ref, o_ref):
        acc = p_ref[0].astype(F32)
        for k in range(1, N_DEV):
            acc = acc + p_ref[k].astype(F32)
        o_ref[...] = acc

    return pl.pallas_call(
        body, name=name, grid=(r // tr,), in_specs=[pl.BlockSpec((N_DEV, tr, c), lambda i: (0, i, 0))],
        out_specs=pl.BlockSpec((tr, c), lambda i: (i, 0)), out_shape=jax.ShapeDtypeStruct((r, c), F32),
        compiler_params=_params(("parallel",)),
    )(parts)


def _mods_fwd(name, c_all, w, b):
    depth, d, n = w.shape

    def body(c_ref, w_ref, b_ref, o_ref):
        cv = c_ref[...]
        sc = (cv * (1.0 / (1.0 + jnp.exp(-cv)))).astype(BF16)
        o_ref[0] = _dot(sc, w_ref[0].astype(BF16), NN) + b_ref[0]

    return pl.pallas_call(
        body, name=name, grid=(depth,),
        in_specs=[pl.BlockSpec(c_all.shape, lambda l: (0, 0)), pl.BlockSpec((1, d, n), lambda l: (l, 0, 0)),
                  pl.BlockSpec((1, 1, n), lambda l: (l, 0, 0))],
        out_specs=pl.BlockSpec((1, c_all.shape[0], n), lambda l: (l, 0, 0)),
        out_shape=jax.ShapeDtypeStruct((depth, c_all.shape[0], n), F32),
        compiler_params=_params(("parallel",)),
    )(c_all, w, b.reshape(depth, 1, n))


def _mods_bwd(name, c_all, dm):
    depth, rows, n = dm.shape
    d = c_all.shape[1]

    def body(c_ref, dm_ref, o_ref):
        cv = c_ref[...]
        sc = (cv * (1.0 / (1.0 + jnp.exp(-cv)))).astype(BF16)
        o_ref[0] = _dot(sc, dm_ref[0].astype(BF16), TN)

    return pl.pallas_call(
        body, name=name, grid=(depth,),
        in_specs=[pl.BlockSpec(c_all.shape, lambda l: (0, 0)), pl.BlockSpec((1, rows, n), lambda l: (l, 0, 0))],
        out_specs=pl.BlockSpec((1, d, n), lambda l: (l, 0, 0)),
        out_shape=jax.ShapeDtypeStruct((depth, d, n), F32),
        compiler_params=_params(("parallel",)),
    )(c_all, dm)


POOL_TILE = 256


def _split_dot(band, val):
    hi = val.astype(BF16)
    lo = (val - hi.astype(F32)).astype(BF16)
    return _dot(band, hi, NN) + _dot(band, lo, NN)


def _pool_fwd(name, h1, x, pw, pb, ps, g1):
    s, d = h1.shape
    t = POOL_TILE

    def body(hc_ref, hp_ref, x_ref, pw_ref, pb_ref, ps_ref, g_ref, xo_ref, zb_ref, pooled_ref):
        i = pl.program_id(0)
        r = lax.broadcasted_iota(jnp.int32, (t, t), 0)
        j = lax.broadcasted_iota(jnp.int32, (t, t), 1)
        pos = (i * t + lax.broadcasted_iota(jnp.int32, (t, 1), 0) + 1).astype(F32)
        has_prev = (i > 0).astype(F32)
        for grp, w in enumerate(POOL_WINDOWS):
            cs = slice(grp * POOL_GROUP, (grp + 1) * POOL_GROUP)
            hc = hc_ref[:, cs]
            band_cur = ((r - j >= 0) & (r - j < w)).astype(BF16)
            band_prev = (r + t - j < w).astype(BF16)
            ssum = _split_dot(band_cur, hc) + has_prev * _split_dot(band_prev, hp_ref[:, cs])
            pooled = (ssum / jnp.minimum(pos, float(w)) - hc).astype(BF16)
            zb = _dot(pooled, pw_ref[grp], NN) + pb_ref[:, cs]
            xo_ref[:, cs] = x_ref[:, cs] + g_ref[:, cs] * (zb * ps_ref[:, cs])
            zb_ref[:, cs] = zb
            pooled_ref[:, cs] = pooled

    row = pl.BlockSpec((t, d), lambda i: (i, 0))
    vec = pl.BlockSpec((1, d), lambda i: (0, 0))
    return pl.pallas_call(
        body, name=name, grid=(s // t,),
        in_specs=[row, pl.BlockSpec((t, d), lambda i: (jnp.maximum(i - 1, 0), 0)), row,
                  pl.BlockSpec(pw.shape, lambda i: (0, 0, 0)), vec, vec, vec],
        out_specs=(row, row, row),
        out_shape=(jax.ShapeDtypeStruct((s, d), F32), jax.ShapeDtypeStruct((s, d), F32), jax.ShapeDtypeStruct((s, d), BF16)),
        compiler_params=_params(("parallel",)),
    )(h1, h1, x, pw, pb, ps, g1)


def _pool_bwd(name, dxn, zb, pooled, pw, ps, g1):
    s, d = dxn.shape
    t = POOL_TILE
    nt = s // t

    def body(dc_ref, dn_ref, zb_ref, pooled_ref, pw_ref, ps_ref, g_ref, dh_ref, dpw_ref, dpb_ref, dps_ref, dg_ref):
        i = pl.program_id(0)

        @pl.when(i == 0)
        def _():
            dpw_ref[...] = jnp.zeros_like(dpw_ref)
            dpb_ref[...] = jnp.zeros_like(dpb_ref)
            dps_ref[...] = jnp.zeros_like(dps_ref)
            dg_ref[...] = jnp.zeros_like(dg_ref)

        jj = lax.broadcasted_iota(jnp.int32, (t, t), 0)
        rr = lax.broadcasted_iota(jnp.int32, (t, t), 1)
        pos = (i * t + lax.broadcasted_iota(jnp.int32, (t, 1), 0) + 1).astype(F32)
        has_next = (i < nt - 1).astype(F32)
        for grp, w in enumerate(POOL_WINDOWS):
            cs = slice(grp * POOL_GROUP, (grp + 1) * POOL_GROUP)
            gv, psv, zbv, dxc = g_ref[:, cs], ps_ref[:, cs], zb_ref[:, cs], dc_ref[:, cs]
            dg_ref[:, cs] += _colsum(dxc * (zbv * psv))
            dy = gv * dxc
            dps_ref[:, cs] += _colsum(dy * zbv)
            dz = dy * psv
            dpb_ref[:, cs] += _colsum(dz)
            dzb = dz.astype(BF16)
            dpw_ref[grp] += _dot(pooled_ref[:, cs], dzb, TN)
            dp = _dot(dzb, pw_ref[grp], NT)
            dzn = (gv * dn_ref[:, cs] * psv).astype(BF16)
            dpn = _dot(dzn, pw_ref[grp], NT) * (has_next / float(w))
            band_cur = ((rr - jj >= 0) & (rr - jj < w)).astype(BF16)
            band_next = (rr + t - jj < w).astype(BF16)
            dh_ref[:, cs] = _split_dot(band_cur, dp / jnp.minimum(pos, float(w))) + _split_dot(band_next, dpn) - dp

    row = pl.BlockSpec((t, d), lambda i: (i, 0))
    vec = pl.BlockSpec((1, d), lambda i: (0, 0))
    wspec = pl.BlockSpec(pw.shape, lambda i: (0, 0, 0))
    return pl.pallas_call(
        body, name=name, grid=(nt,),
        in_specs=[row, pl.BlockSpec((t, d), lambda i: (jnp.minimum(i + 1, nt - 1), 0)), row, row, wspec, vec, vec],
        out_specs=(row, wspec, vec, vec, vec),
        out_shape=(jax.ShapeDtypeStruct((s, d), F32), jax.ShapeDtypeStruct(pw.shape, F32),
                   jax.ShapeDtypeStruct((1, d), F32), jax.ShapeDtypeStruct((1, d), F32), jax.ShapeDtypeStruct((1, d), F32)),
        compiler_params=_params(("arbitrary",)),
    )(dxn, dxn, zb, pooled, pw, ps, g1)


GLU_TILE = 256
HALO = 16
INV_SQRT2 = 0.7071067811865476
INV_SQRT_2PI = 0.3989422804014327


def _gelu(xv):
    return 0.5 * xv * (1.0 + lax.erf(xv * INV_SQRT2))


def _glu_fwd(name, ua, uv, cw, cb):
    s, f = ua.shape
    t, tf = GLU_TILE, _tile(f, 1408)

    def body(a_ref, ah_ref, v_ref, cw_ref, cb_ref, o_ref):
        i = pl.program_id(1)
        has_prev = (i > 0).astype(F32)
        ext = jnp.concatenate([ah_ref[...].astype(F32) * has_prev, a_ref[...].astype(F32)], axis=0)
        e1 = pltpu.roll(ext, 1, 0)[HALO:]
        e2 = pltpu.roll(ext, 2, 0)[HALO:]
        pre = e2 * cw_ref[0:1, :] + e1 * cw_ref[1:2, :] + ext[HALO:] * cw_ref[2:3, :] + cb_ref[...]
        o_ref[...] = (_gelu(pre) * v_ref[...].astype(F32)).astype(o_ref.dtype)

    blk = pl.BlockSpec((t, tf), lambda j, i: (i, j))
    halo = pl.BlockSpec((HALO, tf), lambda j, i: (jnp.maximum(i * (t // HALO) - 1, 0), j))
    return pl.pallas_call(
        body, name=name, grid=(f // tf, s // t),
        in_specs=[blk, halo, blk, pl.BlockSpec((3, tf), lambda j, i: (0, j)), pl.BlockSpec((1, tf), lambda j, i: (0, j))],
        out_specs=blk, out_shape=jax.ShapeDtypeStruct((s, f), BF16),
        compiler_params=_params(("parallel", "parallel")),
    )(ua, ua, uv, cw, cb)


def _glu_bwd(name, ua, uv, dgl, cw, cb):
    s, f = ua.shape
    t, tf = GLU_TILE, _tile(f, 1408)
    nt = s // t
    te = t + HALO

    def body(a_ref, ah_ref, an_ref, v_ref, vn_ref, d_ref, dn_ref, cw_ref, cb_ref, da_ref, dv_ref, dcw_ref, dcb_ref):
        i = pl.program_id(1)

        @pl.when(i == 0)
        def _():
            dcw_ref[...] = jnp.zeros_like(dcw_ref)
            dcb_ref[...] = jnp.zeros_like(dcb_ref)

        has_prev = (i > 0).astype(F32)
        has_next = (i < nt - 1).astype(F32)
        ext = jnp.concatenate([ah_ref[...].astype(F32) * has_prev, a_ref[...].astype(F32), an_ref[...].astype(F32)], axis=0)
        e0 = ext[HALO:]
        e1 = pltpu.roll(ext, 1, 0)[HALO:]
        e2 = pltpu.roll(ext, 2, 0)[HALO:]
        c0, c1, c2 = cw_ref[0:1, :], cw_ref[1:2, :], cw_ref[2:3, :]
        pre = e2 * c0 + e1 * c1 + e0 * c2 + cb_ref[...]
        vx = jnp.concatenate([v_ref[...].astype(F32), vn_ref[...].astype(F32)], axis=0)
        dx = jnp.concatenate([d_ref[...].astype(F32), dn_ref[...].astype(F32) * has_next], axis=0)
        cdf = 0.5 * (1.0 + lax.erf(pre * INV_SQRT2))
        dpre = dx * vx * (cdf + pre * (INV_SQRT_2PI * jnp.exp(-0.5 * pre * pre)))
        up1 = pltpu.roll(dpre, te - 1, 0)
        up2 = pltpu.roll(dpre, te - 2, 0)
        da_ref[...] = (dpre * c2 + up1 * c1 + up2 * c0)[:t].astype(da_ref.dtype)
        dv_ref[...] = (dx * (pre * cdf))[:t].astype(dv_ref.dtype)
        dpt = dpre[:t]
        dcb_ref[...] += _colsum(dpt)
        dcw_ref[0:1, :] += _colsum(e2[:t] * dpt)
        dcw_ref[1:2, :] += _colsum(e1[:t] * dpt)
        dcw_ref[2:3, :] += _colsum(e0[:t] * dpt)

    blk = pl.BlockSpec((t, tf), lambda j, i: (i, j))
    prev = pl.BlockSpec((HALO, tf), lambda j, i: (jnp.maximum(i * (t // HALO) - 1, 0), j))
    nxt = pl.BlockSpec((HALO, tf), lambda j, i: (jnp.minimum((i + 1) * (t // HALO), s // HALO - 1), j))
    w3 = pl.BlockSpec((3, tf), lambda j, i: (0, j))
    w1 = pl.BlockSpec((1, tf), lambda j, i: (0, j))
    return pl.pallas_call(
        body, name=name, grid=(f // tf, nt),
        in_specs=[blk, prev, nxt, blk, nxt, blk, nxt, w3, w1],
        out_specs=(blk, blk, w3, w1),
        out_shape=(jax.ShapeDtypeStruct((s, f), BF16), jax.ShapeDtypeStruct((s, f), BF16),
                   jax.ShapeDtypeStruct((3, f), F32), jax.ShapeDtypeStruct((1, f), F32)),
        compiler_params=_params(("parallel", "arbitrary")),
    )(ua, ua, ua, uv, uv, dgl, dgl, cw, cb)


ATT_TILE = 512
ATT_ROWS = 256
LOG2E = 1.4426950408889634
LN2 = 0.6931471805599453


def _head_blocks_t(a, width):
    s = a.shape[0]
    t = min(ATT_TILE, s)
    return a.reshape(s // t, t, N_HEADS, width).transpose(2, 0, 3, 1)


def _head_rows(a):
    s = a.shape[0]
    t = min(ATT_TILE, s)
    r = a.reshape(s // t, t, N_HEADS, LANES)[..., 0].transpose(2, 0, 1)
    return jnp.broadcast_to(r[:, :, None, :], (N_HEADS, s // t, 8, t))


def _causal_mask(sv, q0, k0):
    row = q0 + lax.broadcasted_iota(jnp.int32, sv.shape, 0)
    col = k0 + lax.broadcasted_iota(jnp.int32, sv.shape, 1)
    return jnp.where(col <= row, sv, NEG_BIG)


def _attn_fwd(name, q_rot, kt4, v_ext):
    s = q_rot.shape[0]
    t = min(ATT_TILE, s)
    nq = s // t

    rq = min(ATT_ROWS, t)

    def body(q_ref, kt_ref, v_ref, o_ref, lse_ref, acc_ref, m_ref):
        qi = pl.program_id(1)
        acc_ref[...] = jnp.zeros_like(acc_ref)
        m_ref[...] = jnp.full_like(m_ref, NEG_BIG)

        def step(j, masked):
            v_blk = v_ref[pl.ds(pl.multiple_of(j * t, t), t), :]
            for r in range(t // rq):
                rs = pl.ds(r * rq, rq)
                sv = _dot(q_ref[rs, :], kt_ref[0, j], NN)
                if masked:
                    sv = _causal_mask(sv, r * rq, 0)
                m_prev = m_ref[rs, :]
                m_new = jnp.maximum(m_prev, jnp.max(sv, axis=-1, keepdims=True))
                p = jnp.exp2(sv - m_new).astype(BF16)
                acc_ref[rs, :] = jnp.exp2(m_prev - m_new) * acc_ref[rs, :] + _dot(p, v_blk, NN)
                m_ref[rs, :] = m_new

        def full_step(j, carry):
            step(j, False)
            return carry

        lax.fori_loop(0, qi, full_step, 0)
        step(qi, True)
        l = acc_ref[:, V_HEAD:V_HEAD + 1]
        o_ref[...] = (acc_ref[:, :V_HEAD] / l).astype(o_ref.dtype)
        lse_ref[...] = jnp.broadcast_to(m_ref[...] + jnp.log(l) * LOG2E, lse_ref.shape)

    head_q = pl.BlockSpec((t, Q_EXT), lambda h, i: (i, h))
    head_o = pl.BlockSpec((t, V_HEAD), lambda h, i: (i, h))
    return pl.pallas_call(
        body, name=name, grid=(N_HEADS, nq),
        in_specs=[head_q, pl.BlockSpec((1, nq, Q_EXT, t), lambda h, i: (h, 0, 0, 0)), pl.BlockSpec((s, Q_EXT), lambda h, i: (0, h))],
        out_specs=(head_o, head_o),
        out_shape=(jax.ShapeDtypeStruct((s, N_HEADS * V_HEAD), BF16), jax.ShapeDtypeStruct((s, N_HEADS * LANES), F32)),
        scratch_shapes=[pltpu.VMEM((t, Q_EXT), F32), pltpu.VMEM((t, 1), F32)],
        compiler_params=_params(("parallel", "parallel")),
    )(q_rot, kt4, v_ext)


def _attn_dq(name, q_rot, tabq, kt4, kfull, vt4, o, lse, do):
    s = q_rot.shape[0]
    t = min(ATT_TILE, s)
    nq = s // t

    def body(q_ref, tab_ref, kt_ref, k_ref, vt_ref, o_ref, lse_ref, do_ref, dq_ref, delta_ref, acc_ref):
        qi = pl.program_id(1)
        q = q_ref[...]
        dov = do_ref[...]
        delta = jnp.sum(dov.astype(F32) * o_ref[...].astype(F32), axis=-1, keepdims=True)
        lse = lse_ref[:, 0:1]
        acc_ref[...] = jnp.zeros_like(acc_ref)

        def step(j, masked):
            sv = _dot(q, kt_ref[0, j], NN)
            if masked:
                sv = _causal_mask(sv, qi * t, j * t)
            p = jnp.exp2(sv - lse)
            dp = _dot(dov, vt_ref[0, j], NN)
            ds = (p * (dp - delta)).astype(BF16)
            acc_ref[...] += _dot(ds, k_ref[pl.ds(pl.multiple_of(j * t, t), t), :], NN)

        def full_step(j, carry):
            step(j, False)
            return carry

        lax.fori_loop(0, qi, full_step, 0)
        step(qi, True)
        dq_ref[...] = (acc_ref[...] * (tab_ref[...] * LN2)).astype(dq_ref.dtype)
        delta_ref[...] = jnp.broadcast_to(delta, delta_ref.shape)

    head_q = pl.BlockSpec((t, Q_EXT), lambda h, i: (i, h))
    head_o = pl.BlockSpec((t, V_HEAD), lambda h, i: (i, h))
    return pl.pallas_call(
        body, name=name, grid=(N_HEADS, nq),
        in_specs=[head_q, pl.BlockSpec((t, Q_EXT), lambda h, i: (i, 0)), pl.BlockSpec((1, nq, Q_EXT, t), lambda h, i: (h, 0, 0, 0)),
                  pl.BlockSpec((s, Q_EXT), lambda h, i: (0, h)), pl.BlockSpec((1, nq, V_HEAD, t), lambda h, i: (h, 0, 0, 0)),
                  head_o, head_o, head_o],
        out_specs=(head_q, head_o),
        out_shape=(jax.ShapeDtypeStruct((s, N_HEADS * Q_EXT), BF16), jax.ShapeDtypeStruct((s, N_HEADS * LANES), F32)),
        scratch_shapes=[pltpu.VMEM((t, Q_EXT), F32)],
        compiler_params=_params(("parallel", "parallel")),
    )(q_rot, tabq, kt4, kfull, vt4, o, lse, do)


def _attn_dkv(name, kfull, v, qt4, q_rot, dot4, do, lse_row, delta_row, acc_in=None):
    s = kfull.shape[0]
    t = min(ATT_TILE, s)
    nq = s // t
    has_in = acc_in is not None

    def body(*refs):
        k_ref, v_ref, qt_ref, q_ref, dot_ref, do_ref, lse_ref, delta_ref = refs[:8]
        dkn_ref, dkd_ref, dv_ref, acck_ref, accv_ref = refs[-5:]
        kj, h = pl.program_id(0), pl.program_id(1)
        k_blk, v_blk = k_ref[...], v_ref[...]
        acck_ref[...] = jnp.zeros_like(acck_ref)
        accv_ref[...] = jnp.zeros_like(accv_ref)

        def step(i, masked):
            qs = pl.ds(pl.multiple_of(i * t, t), t)
            st = _dot(k_blk, qt_ref[0, i], NN)
            if masked:
                krow = lax.broadcasted_iota(jnp.int32, st.shape, 0)
                qcol = lax.broadcasted_iota(jnp.int32, st.shape, 1)
                st = jnp.where(krow <= qcol, st, NEG_BIG)
            pt = jnp.exp2(st - lse_ref[0, i, 0:1, :])
            accv_ref[...] += _dot(pt.astype(BF16), do_ref[qs, :], NN)
            dpt = _dot(v_blk, dot_ref[0, i], NN)
            dst = (pt * (dpt - delta_ref[0, i, 0:1, :])).astype(BF16)
            acck_ref[...] += _dot(dst, q_ref[qs, :], NN)

        def full_step(i, carry):
            step(i, False)
            return carry

        step(kj, True)
        lax.fori_loop(kj + 1, nq, full_step, 0)
        dk = acck_ref[...] * LN2
        dkn, dkd = dk[:, :QK_NOPE], dk[:, QK_NOPE:]
        if has_in:
            dkn = dkn + refs[8][...]
            dv_ref[...] = accv_ref[...] + refs[10][...]
        else:
            dv_ref[...] = accv_ref[...]
        dkn_ref[...] = dkn

        @pl.when(h == 0)
        def _():
            if has_in:
                dkd_ref[...] = dkd + refs[9][...]
            else:
                dkd_ref[...] = dkd

        @pl.when(h > 0)
        def _():
            dkd_ref[...] += dkd

    kblk = pl.BlockSpec((t, LANES), lambda j, h: (j, h))
    kdblk = pl.BlockSpec((t, LANES), lambda j, h: (j, 0))
    col = pl.BlockSpec((s, LANES), lambda j, h: (0, h))
    stat = pl.BlockSpec((1, nq, 8, t), lambda j, h: (h, 0, 0, 0))
    ins = [kfull, v, qt4, q_rot, dot4, do, lse_row, delta_row]
    in_specs = [pl.BlockSpec((t, Q_EXT), lambda j, h: (j, h)), kblk, pl.BlockSpec((1, nq, Q_EXT, t), lambda j, h: (h, 0, 0, 0)),
                pl.BlockSpec((s, Q_EXT), lambda j, h: (0, h)), pl.BlockSpec((1, nq, V_HEAD, t), lambda j, h: (h, 0, 0, 0)), col, stat, stat]
    if has_in:
        ins += list(acc_in)
        in_specs += [kblk, kdblk, kblk]
    return pl.pallas_call(
        body, name=name, grid=(nq, N_HEADS), in_specs=in_specs, out_specs=(kblk, kdblk, kblk),
        out_shape=(jax.ShapeDtypeStruct((s, N_HEADS * LANES), F32), jax.ShapeDtypeStruct((s, LANES), F32),
                   jax.ShapeDtypeStruct((s, N_HEADS * LANES), F32)),
        scratch_shapes=[pltpu.VMEM((t, Q_EXT), F32), pltpu.VMEM((t, LANES), F32)],
        compiler_params=_params(("parallel", "arbitrary")),
    )(*ins)


def _swap_halves(w):
    half = w.shape[-1] // 2
    return jnp.concatenate([-w[..., half:], w[..., :half]], axis=-1)


def _unswap_halves(g):
    half = g.shape[-1] // 2
    return jnp.concatenate([g[..., half:], -g[..., :half]], axis=-1)


def _extend_w_uq(w):
    r = w.reshape(Q_RANK, N_HEADS, QK_HEAD)
    rope = r[..., QK_NOPE:]
    return jnp.concatenate([r[..., :QK_NOPE], rope, _swap_halves(rope)], axis=-1).reshape(Q_RANK, N_HEADS * Q_EXT)


def _fold_w_uq_grad(g):
    r = g.reshape(Q_RANK, N_HEADS, Q_EXT)
    rope = r[..., QK_NOPE:QK_HEAD] + _unswap_halves(r[..., QK_HEAD:])
    return jnp.concatenate([r[..., :QK_NOPE], rope], axis=-1).reshape(Q_RANK, N_HEADS * QK_HEAD)


def _extend_w_dkv(w):
    return jnp.concatenate([w, _swap_halves(w[:, KV_RANK:])], axis=-1)


def _fold_w_dkv_grad(g):
    rope = g[:, KV_RANK:KV_RANK + QK_ROPE] + _unswap_halves(g[:, KV_RANK + QK_ROPE:])
    return jnp.concatenate([g[:, :KV_RANK], rope], axis=-1)


def _rope_tables(positions):
    inv = 1.0 / (ROPE_THETA ** (jnp.arange(0, QK_ROPE, 2, dtype=F32) / QK_ROPE))
    ang = positions.astype(F32)[:, None] * inv
    cos, sin = jnp.cos(ang), jnp.sin(ang)
    tabk = jnp.concatenate([cos, cos, sin, sin], axis=-1)
    scale = QK_HEAD ** -0.5 * LOG2E
    tabq = jnp.concatenate([jnp.full((positions.shape[0], QK_NOPE), scale, F32), tabk * scale], axis=-1)
    return tabq, tabk


def _forward_backward(x, target, mods, tabq, tabk, final_g, fetch, push):
    row = lambda vec: vec.reshape(1, -1)
    mod = [[row(mods[l, k * D_MODEL:(k + 1) * D_MODEL]) for k in range(N_MOD)] for l in range(DEPTH)]
    saved, weights = [], []
    kv = None
    for l in range(DEPTH):
        w, tok = fetch(l, x)
        sh1, sc1, g1, sh2, sc2, g2 = mod[l]
        sh1 = sh1 + tok
        if l == N_A_LAYERS:
            kvn = _rms_fwd("kvin_fwd", x, row(w["kv_in_g"]))
            kv_ext = _mm("dkv_fwd", kvn, w["w_dkv_ext"], out_dtype=F32)
            ckv = _rms_fwd("ckv_fwd", kv_ext, row(w["ckv_norm_g"]), ncols=KV_RANK)
            kd = _krope_fwd("krope_fwd", kv_ext, tabk)
            kn, v = _mm("uk_fwd", ckv, w["w_uk"]), _mm("uv_fwd", ckv, w["w_uv"])
            heads = lambda a: [a[:, h * LANES:(h + 1) * LANES] for h in range(N_HEADS)]
            kfull = jnp.concatenate([part for kh in heads(kn) for part in (kh, kd)], axis=-1)
            v_ext = jnp.concatenate([part for vh in heads(v) for part in (vh, jnp.ones_like(vh))], axis=-1)
            kv = dict(x=x, kvn=kvn, kv_ext=kv_ext, ckv=ckv, v=v, kfull=kfull, v_ext=v_ext,
                      kt4=_head_blocks_t(kfull, Q_EXT), vt4=_head_blocks_t(v, V_HEAD))
        x_in = x
        if l < N_A_LAYERS:
            h1 = _rms_fwd(f"norm1_fwd_{l}", x, row(w["norm1_g"]), sc1, sh1, out_dtype=F32)
            x_mid, zb, pooled = _pool_fwd(f"pool_fwd_{l}", h1, x, w["pool_w"], row(w["pool_b"]), row(w["pool_scale"]), g1)
            mix = (zb, pooled)
        else:
            h1 = _rms_fwd(f"norm1_fwd_{l}", x, row(w["norm1_g"]), sc1, sh1)
            cq_pre = _mm(f"dq_fwd_{l}", h1, w["w_dq"], out_dtype=F32)
            cq = _rms_fwd(f"qnorm_fwd_{l}", cq_pre, row(w["q_norm_g"]))
            q_rot = _mm(f"uq_fwd_{l}", cq, w["w_uq_ext"], rowtab=tabq)
            o, lse = _attn_fwd(f"attn_fwd_{l}", q_rot, kv["kt4"], kv["v_ext"])
            y, x_mid = _mm(f"wo_fwd_{l}", o, w["w_o"], resid=x, gate=g1)
            mix = (h1, cq_pre, cq, q_rot, o, lse, y)
        h2 = _rms_fwd(f"norm2_fwd_{l}", x_mid, row(w["norm2_g"]), sc2, sh2)
        w_up_a, w_up_v = w["w_up"](h2)
        ua = _mm(f"up_a_fwd_{l}", h2, w_up_a)
        uv = _mm(f"up_v_fwd_{l}", h2, w_up_v)
        gl = _glu_fwd(f"glu_fwd_{l}", ua, uv, w["conv_w"], row(w["conv_b"]))
        w_down = w["w_down"](gl)
        y2, x = _mm(f"down_fwd_{l}", gl, w_down, resid=x_mid, gate=g2)
        saved.append((x_in, x_mid, h2, ua, uv, gl, y2, mix))
        weights.append(dict(w, w_up_a=w_up_a, w_up_v=w_up_v, w_down=w_down))

    dx, dfinal_g, loss = _loss_head("loss_head", x, row(final_g), target)
    g = {"final_g": dfinal_g.reshape(-1)}
    per_layer = {k: [None] * DEPTH for k in ("norm1_g", "norm2_g", "conv_w", "conv_b")}
    per_a = {k: [None] * N_A_LAYERS for k in ("pool_b", "pool_scale")}
    per_b = {k: [None] * N_B_LAYERS for k in ("q_norm_g",)}
    dmods = [None] * DEPTH
    dkv = None
    tok = 0.0
    for l in reversed(range(DEPTH)):
        w, big = weights[l], {}
        sh1, sc1, g1, sh2, sc2, g2 = mod[l]
        g2 = g2 + tok
        x_in, x_mid, h2, ua, uv, gl, y2, mix = saved[l]
        dy2, dg2 = _gate_bwd(f"gate2_bwd_{l}", dx, y2, g2)
        dgl = _mm(f"down_bwd_{l}", dy2, w["w_down"], mode="nt")
        tok = push(l, "down", dict(w_down=_mm(f"down_wgrad_{l}", gl, dy2, mode="tn", out_dtype=F32, tm_cap=1408)), None)
        da, dv_, dcw, dcb = _glu_bwd(f"glu_bwd_{l}", ua, uv, dgl, w["conv_w"], row(w["conv_b"]) + tok)
        dh2 = _mm(f"up_a_bwd_{l}", da, w["w_up_a"], mode="nt", out_dtype=F32)
        dh2 = _mm(f"up_v_bwd_{l}", dv_, w["w_up_v"], mode="nt", out_dtype=F32, add=dh2)
        tok = push(l, "up", dict(w_up_a=_mm(f"up_a_wgrad_{l}", h2, da, mode="tn", out_dtype=F32),
                                 w_up_v=_mm(f"up_v_wgrad_{l}", h2, dv_, mode="tn", out_dtype=F32)), None)
        per_layer["conv_w"][l], per_layer["conv_b"][l] = dcw, dcb.reshape(-1)
        dx_mid, dn2, dsh2, dsc2 = _rms_bwd(f"norm2_bwd_{l}", x_mid, row(w["norm2_g"]), dh2, sc2 + tok, dx_in=dx)
        per_layer["norm2_g"][l] = dn2.reshape(-1)
        if l < N_A_LAYERS:
            zb, pooled = mix
            dh1, dpw, dpb, dps, dg1 = _pool_bwd(f"pool_bwd_{l}", dx_mid, zb, pooled, w["pool_w"], row(w["pool_scale"]), g1)
            big["pool_w"] = dpw
            per_a["pool_b"][l], per_a["pool_scale"][l] = dpb.reshape(-1), dps.reshape(-1)
        else:
            j = l - N_A_LAYERS
            h1, cq_pre, cq, q_rot, o, lse, y = mix
            dy, dg1 = _gate_bwd(f"gate1_bwd_{l}", dx_mid, y, g1)
            do = _mm(f"wo_bwd_{l}", dy, w["w_o"], mode="nt")
            big["w_o"] = _mm(f"wo_wgrad_{l}", o, dy, mode="tn", out_dtype=F32)
            dq_ext, delta = _attn_dq(f"attn_dq_{l}", q_rot, tabq, kv["kt4"], kv["kfull"], kv["vt4"], o, lse, do)
            dkv = _attn_dkv(f"attn_dkv_{l}", kv["kfull"], kv["v"], _head_blocks_t(q_rot, Q_EXT), q_rot, _head_blocks_t(do, V_HEAD), do,
                            _head_rows(lse), _head_rows(delta), acc_in=dkv)
            dcq = _mm(f"uq_bwd_{l}", dq_ext, w["w_uq_ext"], mode="nt", out_dtype=F32)
            big["w_uq_ext"] = _mm(f"uq_wgrad_{l}", cq, dq_ext, mode="tn", out_dtype=F32)
            dcq_pre, dqn = _rms_bwd(f"qnorm_bwd_{l}", cq_pre, row(w["q_norm_g"]), dcq, out_dtype=BF16)
            per_b["q_norm_g"][j] = dqn.reshape(-1)
            dh1 = _mm(f"dq_bwd_{l}", dcq_pre, w["w_dq"], mode="nt")
            big["w_dq"] = _mm(f"dq_wgrad_{l}", h1, dcq_pre, mode="tn", out_dtype=F32)
        dx, dn1, dsh1, dsc1 = _rms_bwd(f"norm1_bwd_{l}", x_in, row(w["norm1_g"]), dh1, sc1, dx_in=dx_mid)
        per_layer["norm1_g"][l] = dn1.reshape(-1)
        dmods[l] = jnp.concatenate([dsh1, dsc1, dg1, dsh2, dsc2, dg2], axis=-1).reshape(-1)
        if l == N_A_LAYERS:
            dkn, dkd, dv = dkv
            dckv = _mm("uk_bwd", dkn, w["w_uk"], mode="nt", out_dtype=F32)
            dckv = _mm("uv_bwd", dv, w["w_uv"], mode="nt", out_dtype=F32, add=dckv)
            big["w_uk"] = _mm("uk_wgrad", kv["ckv"], dkn, mode="tn", out_dtype=F32)
            big["w_uv"] = _mm("uv_wgrad", kv["ckv"], dv, mode="tn", out_dtype=F32)
            dkr = _krope_bwd("krope_bwd", dkd, tabk)
            dc, dckv_g = _rms_bwd("ckv_bwd", kv["kv_ext"], row(w["ckv_norm_g"]), dckv, ncols=KV_RANK, out_dtype=BF16)
            dkv_ext = jnp.concatenate([dc, dkr.astype(BF16)], axis=-1)
            dkvn = _mm("dkv_bwd", dkv_ext, w["w_dkv_ext"], mode="nt")
            big["w_dkv_ext"] = _mm("dkv_wgrad", kv["kvn"], dkv_ext, mode="tn", out_dtype=F32)
            dx, dkv_in_g = _rms_bwd("kvin_bwd", kv["x"], row(w["kv_in_g"]), dkvn, dx_in=dx)
            g["ckv_norm_g"], g["kv_in_g"] = dckv_g.reshape(-1), dkv_in_g.reshape(-1)
        tok = push(l, "mix", big, dx)
    for group in (per_layer, per_a, per_b):
        for k, vals in group.items():
            g[k] = jnp.stack(vals)
    return loss, dx, g, jnp.stack(dmods)


def _my_index():
    return 4 * lax.axis_index("x") + 2 * lax.axis_index("y") + lax.axis_index("c")


def _peer(k):
    x, y, c = lax.axis_index("x"), lax.axis_index("y"), lax.axis_index("c")
    return (1 - x if k & 4 else x, 1 - y if k & 2 else y, 1 - c if k & 1 else c)


def _index_of(pos):
    return 4 * pos[0] + 2 * pos[1] + pos[2]


def _exchange_many(name, arrays, scatter):
    n = len(arrays)
    blocks = [tuple(a.shape[1:]) if scatter else tuple(a.shape) for a in arrays]

    def body(*refs):
        x_refs, o_refs = refs[:n], refs[n:2 * n]
        send_sems, recv_sems, local_sems = refs[2 * n:]
        me = _my_index()
        started = []
        for a in range(n):
            mine = pltpu.make_async_copy(x_refs[a].at[me] if scatter else x_refs[a], o_refs[a].at[me], local_sems.at[a])
            mine.start()
            started.append(mine)
        sends = []
        for k in range(1, N_DEV):
            peer = _peer(k)
            for a in range(n):
                cp = pltpu.make_async_remote_copy(
                    src_ref=x_refs[a].at[_index_of(peer)] if scatter else x_refs[a], dst_ref=o_refs[a].at[me],
                    send_sem=send_sems.at[a, k - 1], recv_sem=recv_sems.at[a, k - 1], device_id=peer, device_id_type=MESH)
                cp.start()
                sends.append(cp)
        for k in range(1, N_DEV):
            peer = _peer(k)
            for a in range(n):
                pltpu.make_async_remote_copy(
                    src_ref=x_refs[a].at[me] if scatter else x_refs[a], dst_ref=o_refs[a].at[_index_of(peer)],
                    send_sem=send_sems.at[a, k - 1], recv_sem=recv_sems.at[a, k - 1], device_id=peer, device_id_type=MESH).wait_recv()
        for cp in sends:
            cp.wait_send()
        for mine in started:
            mine.wait()

    return pl.pallas_call(
        body, name=name, out_shape=tuple(jax.ShapeDtypeStruct((N_DEV,) + blk, a.dtype) for blk, a in zip(blocks, arrays)),
        in_specs=[pl.BlockSpec(memory_space=pl.ANY)] * n, out_specs=tuple([pl.BlockSpec(memory_space=pl.ANY)] * n),
        scratch_shapes=[pltpu.SemaphoreType.DMA((n, N_DEV - 1)), pltpu.SemaphoreType.DMA((n, N_DEV - 1)), pltpu.SemaphoreType.DMA((n,))],
    )(*arrays)


def _exchange(name, x, scatter):
    return _exchange_many(name, [x], scatter)[0]


HBM_SPEC = pl.BlockSpec(memory_space=pltpu.HBM)
SEM_SPEC = pl.BlockSpec(memory_space=pltpu.SEMAPHORE)
DATAFLOW = pltpu.SideEffectType.DATAFLOW_SIDE_EFFECTING


def _remote_copies(x_refs, land_refs, send_sems, recv_sems, scatter, numbers=None):
    me = _my_index()
    numbers = list(range(len(x_refs))) if numbers is None else numbers
    out, inc = [], []
    for a in range(len(x_refs)):
        for k in range(1, N_DEV):
            peer = _peer(k)
            pair = numbers[a] * (N_DEV - 1) + k - 1
            sems = dict(send_sem=send_sems.at[pair], recv_sem=recv_sems.at[pair], device_id=peer, device_id_type=MESH)
            out.append(pltpu.make_async_remote_copy(
                src_ref=x_refs[a].at[_index_of(peer)] if scatter else x_refs[a], dst_ref=land_refs[a].at[me], **sems))
            inc.append(pltpu.make_async_remote_copy(
                src_ref=x_refs[a].at[me] if scatter else x_refs[a], dst_ref=land_refs[a].at[_index_of(peer)], **sems))
    return out, inc


def _exchange_start(name, arrays, scatter):
    n = len(arrays)
    blocks = [tuple(a.shape[1:]) if scatter else tuple(a.shape) for a in arrays]

    def body(*refs):
        x_refs, land_refs = refs[:n], refs[n:2 * n]
        send_sems, recv_sems = refs[2 * n], refs[2 * n + 1]
        for cp in _remote_copies(x_refs, land_refs, send_sems, recv_sems, scatter)[0]:
            cp.start()
        refs[-1][...] = jnp.zeros_like(refs[-1])

    sem_type = pltpu.SemaphoreType.DMA((n * (N_DEV - 1),))
    lands =[pltpu.with_memory_space_constraint(lax.empty((N_DEV,) + blk, a.dtype), pltpu.HBM) for blk, a in zip(blocks, arrays)]
    srcs = [pltpu.with_memory_space_constraint(a, pltpu.HBM) for a in arrays]
    res = pl.pallas_call(
        body, name=name,
        out_shape=(sem_type, sem_type, *[pltpu.HBM(a.shape, a.dtype) for a in srcs + lands], jax.ShapeDtypeStruct((8, LANES), F32)),
        in_specs=[HBM_SPEC] * (2 * n), out_specs=(SEM_SPEC, SEM_SPEC, *[HBM_SPEC] * (2 * n), pl.BlockSpec(memory_space=pltpu.VMEM)),
        input_output_aliases={i: 2 + i for i in range(2 * n)},
        compiler_params=pltpu.CompilerParams(has_side_effects=DATAFLOW),
    )(*srcs, *lands)
    return (res[0], res[1], list(res[2:2 + n]), list(res[2 + n:2 + 2 * n])), res[-1]


def _exchange_wait(name, handles, after, scatter, which=None):
    send_sems, recv_sems, srcs, lands = handles
    which = list(range(len(srcs))) if which is None else list(which)
    srcs, lands = [srcs[a] for a in which], [lands[a] for a in which]
    n = len(srcs)

    def body(*refs):
        x_refs, land_refs = refs[:n], refs[n:2 * n]
        out, inc = _remote_copies(x_refs, land_refs, refs[2 * n], refs[2 * n + 1], scatter, which)
        for cp in out:
            cp.wait_send()
        for cp in inc:
            cp.wait_recv()

    res = pl.pallas_call(
        body, name=name, out_shape=tuple(pltpu.HBM(a.shape, a.dtype) for a in srcs + lands),
        in_specs=[HBM_SPEC] * (2 * n) + [SEM_SPEC, SEM_SPEC, pl.BlockSpec(memory_space=pl.ANY)], out_specs=tuple([HBM_SPEC] * (2 * n)),
        input_output_aliases={i: i for i in range(2 * n)},
        compiler_params=pltpu.CompilerParams(has_side_effects=DATAFLOW),
    )(*srcs, *lands, send_sems, recv_sems, after)
    return list(res[n:])


def _pack(arrays, dtype, row_multiple):
    flat = jnp.concatenate([a.astype(dtype).reshape(-1) for a in arrays])
    rows = -(-flat.shape[0] // (LANES * row_multiple)) * row_multiple
    return jnp.pad(flat, (0, rows * LANES - flat.shape[0])).reshape(rows, LANES)


def _unpack(packed, shapes):
    lead = packed.shape[:-2]
    flat = packed.reshape(lead + (-1,))
    out, off = [], 0
    for shp in shapes:
        size = 1
        for d in shp:
            size *= d
        out.append(flat[..., off:off + size].reshape(lead + tuple(shp)))
        off += size
    return out


def _unshard(g8, axis):
    return jnp.concatenate([g8[j] for j in range(N_DEV)], axis=axis)


def _shard8(full, axis):
    n = full.shape[axis] // N_DEV
    return jnp.stack([lax.slice_in_dim(full, j * n, (j + 1) * n, axis=axis) for j in range(N_DEV)])


VECTOR_WEIGHTS = (("pool_b", 1), ("pool_scale", 1), ("conv_w", 2))
REPLICATED_WEIGHTS = ("norm1_g", "norm2_g", "kv_in_g", "ckv_norm_g", "q_norm_g", "conv_b", "final_g")
WEIGHT_ORDER = ("mod_w", "mod_b", "norm1_g", "norm2_g", "pool_w", "pool_b", "pool_scale", "kv_in_g", "w_dkv", "ckv_norm_g", "w_uk",
                "w_uv", "w_dq", "q_norm_g", "w_uq", "w_o", "w_up", "conv_w", "conv_b", "w_down", "final_g")
BIG_ROW_MULTIPLE = 1024
SMALL_ROW_MULTIPLE = 16


def _as_2d(a):
    if a.ndim == 1:
        return a.reshape(-1, LANES)
    return a.reshape(-1, a.shape[-1])


def kernel(x, c, positions, mod_w, mod_b, norm1_g, norm2_g, pool_w, pool_b, pool_scale, kv_in_g, w_dkv, ckv_norm_g, w_uk, w_uv, w_dq, q_norm_g, w_uq, w_o, w_up, conv_w, conv_b, w_down, final_g, loss_target, m_mod_w, m_mod_b, m_norm1_g, m_norm2_g, m_pool_w, m_pool_b, m_pool_scale, m_kv_in_g, m_w_dkv, m_ckv_norm_g, m_w_uk, m_w_uv, m_w_dq, m_q_norm_g, m_w_uq, m_w_o, m_w_up, m_conv_w, m_conv_b, m_w_down, m_final_g, v_mod_w, v_mod_b, v_norm1_g, v_norm2_g, v_pool_w, v_pool_b, v_pool_scale, v_kv_in_g, v_w_dkv, v_ckv_norm_g, v_w_uk, v_w_uv, v_w_dq, v_q_norm_g, v_w_uq, v_w_o, v_w_up, v_conv_w, v_conv_b, v_w_down, v_final_g):
    shard = dict(mod_w=mod_w, mod_b=mod_b, norm1_g=norm1_g, norm2_g=norm2_g, pool_w=pool_w, pool_b=pool_b, pool_scale=pool_scale,
                 kv_in_g=kv_in_g, w_dkv=w_dkv, ckv_norm_g=ckv_norm_g, w_uk=w_uk, w_uv=w_uv, w_dq=w_dq, q_norm_g=q_norm_g, w_uq=w_uq,
                 w_o=w_o, w_up=w_up, conv_w=conv_w, conv_b=conv_b, w_down=w_down, final_g=final_g)
    mom_m = dict(mod_w=m_mod_w, mod_b=m_mod_b, norm1_g=m_norm1_g, norm2_g=m_norm2_g, pool_w=m_pool_w, pool_b=m_pool_b,
                 pool_scale=m_pool_scale, kv_in_g=m_kv_in_g, w_dkv=m_w_dkv, ckv_norm_g=m_ckv_norm_g, w_uk=m_w_uk, w_uv=m_w_uv,
                 w_dq=m_w_dq, q_norm_g=m_q_norm_g, w_uq=m_w_uq, w_o=m_w_o, w_up=m_w_up, conv_w=m_conv_w, conv_b=m_conv_b,
                 w_down=m_w_down, final_g=m_final_g)
    mom_v = dict(mod_w=v_mod_w, mod_b=v_mod_b, norm1_g=v_norm1_g, norm2_g=v_norm2_g, pool_w=v_pool_w, pool_b=v_pool_b,
                 pool_scale=v_pool_scale, kv_in_g=v_kv_in_g, w_dkv=v_w_dkv, ckv_norm_g=v_ckv_norm_g, w_uk=v_w_uk, w_uv=v_w_uv,
                 w_dq=v_w_dq, q_norm_g=v_q_norm_g, w_uq=v_w_uq, w_o=v_w_o, w_up=v_w_up, conv_w=v_conv_w, conv_b=v_conv_b,
                 w_down=v_w_down, final_g=v_final_g)
    me = _my_index()
    d6 = N_MOD * D_MODEL
    mod_cols = d6 // N_DEV

    small_in = [c] + [shard[k] for k, _ in VECTOR_WEIGHTS]
    small_all = _exchange("gather_vectors", _pack(small_in, F32, SMALL_ROW_MULTIPLE), scatter=False)
    parts = _unpack(small_all, [a.shape for a in small_in])
    c_all = jnp.pad(parts[0].reshape(N_DEV, D_MODEL), ((0, N_DEV), (0, 0)))
    vec = {k: _unshard(p, ax) for (k, ax), p in zip(VECTOR_WEIGHTS, parts[1:])}

    my_mod_b = lax.dynamic_slice_in_dim(mod_b, me * mod_cols, mod_cols, axis=1)
    mods_mine = _mods_fwd("mods_fwd", c_all, mod_w, my_mod_b)
    mods_all = _exchange("gather_mods", _pack([mods_mine], F32, SMALL_ROW_MULTIPLE), scatter=False)
    mods_all = _unpack(mods_all, [mods_mine.shape])[0]
    mods = lax.dynamic_index_in_dim(mods_all, me, axis=2, keepdims=False)
    mods = jnp.moveaxis(mods, 0, 1).reshape(DEPTH, d6)

    tabq, tabk = _rope_tables(positions[0])
    half = N_DEV // 2
    up_cols = shard["w_up"].shape[2]
    cat = lambda a, axis, lo=0, hi=N_DEV: jnp.concatenate([a[j] for j in range(lo, hi)], axis=axis)

    def stage_pieces(l):
        out = {"pool_w": shard["pool_w"].astype(BF16)} if l == 0 else {}
        out.update(w_up=shard["w_up"][l].astype(BF16), w_down=shard["w_down"][l].astype(BF16))
        if l >= N_A_LAYERS:
            out.update({k: shard[k][l - N_A_LAYERS].astype(BF16) for k in ("w_dq", "w_uq", "w_o")})
        if l == N_A_LAYERS:
            out.update({k: shard[k].astype(BF16) for k in ("w_dkv", "w_uk", "w_uv")})
        return out

    gathers, pool_all = {}, []

    def start_gather(l, behind=None):
        pieces = stage_pieces(l)
        if behind is not None:
            pieces, _ = lax.optimization_barrier((pieces, behind))
        handles, token = _exchange_start(f"gather_start_{l}", list(pieces.values()), scatter=False)
        gathers[l] = (handles, pieces)
        return token[0, 0]

    def wait_gather(l, keys, after, tag=""):
        handles, pieces = gathers[l]
        which = [list(pieces).index(k) for k in keys]
        lands = _exchange_wait(f"gather_wait_{l}{tag}", handles, after, scatter=False, which=which)
        return dict(zip(keys, own_slot(lands, [pieces[k] for k in keys])))

    def whole_weights(l, got):
        w = dict(norm1_g=norm1_g[l], norm2_g=norm2_g[l], conv_w=vec["conv_w"][l], conv_b=conv_b[l])
        if l == 0:
            pool_all.append(got["pool_w"])
        if l < N_A_LAYERS:
            w.update(pool_w=cat(pool_all[0][:, l], 1), pool_b=vec["pool_b"][l], pool_scale=vec["pool_scale"][l])
        else:
            rope = got["w_uq"][..., QK_NOPE:]
            ext = jnp.concatenate([got["w_uq"][..., :QK_NOPE], rope, _swap_halves(rope)], axis=-1)
            w.update(w_dq=got["w_dq"].reshape(D_MODEL, Q_RANK), w_uq_ext=cat(ext, -1), w_o=got["w_o"].reshape(D_MODEL, D_MODEL),
                     q_norm_g=q_norm_g[l - N_A_LAYERS])
        if l == N_A_LAYERS:
            w.update(w_dkv_ext=_extend_w_dkv(got["w_dkv"].reshape(D_MODEL, KV_RANK + QK_ROPE)), w_uk=cat(got["w_uk"], -1),
                     w_uv=cat(got["w_uv"], -1), kv_in_g=kv_in_g, ckv_norm_g=ckv_norm_g)
        return w

    def own_slot(lands, own):
        return [lax.dynamic_update_index_in_dim(p, o, me, 0) for p, o in zip(lands, own)]

    def fetch(l, after):
        up_parts = lambda g8: (cat(g8, -1, 0, half), cat(g8, -1, half, N_DEV))
        if l == 0:
            start_gather(0, behind=mods)
            got = wait_gather(0, ["pool_w"], mods, "_pool")
            w_up = lambda aft: up_parts(wait_gather(0, ["w_up"], aft, "_up")["w_up"])
            w_down = lambda aft: wait_gather(0, ["w_down"], aft, "_down")["w_down"].reshape(D_FF, D_MODEL)
        else:
            got = wait_gather(l, list(gathers[l][1]), after)
            up, down = up_parts(got["w_up"]), got["w_down"].reshape(D_FF, D_MODEL)
            w_up, w_down = (lambda aft: up), (lambda aft: down)
        w = dict(whole_weights(l, got), w_up=w_up, w_down=w_down)
        return w, (start_gather(l + 1) if l + 1 < DEPTH else 0.0)

    scatters, pending, pool_grads, piece_grads = {}, {}, {}, {}

    def reduce_pieces(l, keys, got):
        for k, p in zip(keys, got):
            piece_grads[(k, l)] = _sum8(f"sum_grads_{k}_{l}", p.reshape(N_DEV, -1, p.shape[-1])).reshape(p.shape[1:])

    def start_scatter(name, sent):
        sent = {k: a.astype(BF16) for k, a in sent.items()}
        handles, token = _exchange_start(f"scatter_start_{name}", list(sent.values()), scatter=True)
        scatters[name] = (handles, list(sent), [lax.dynamic_index_in_dim(a, me, 0, keepdims=False) for a in sent.values()])
        return token[0, 0]

    def finish_scatter(name, l, after):
        handles, keys, own = scatters.pop(name)
        reduce_pieces(l, keys, own_slot(_exchange_wait(f"scatter_wait_{name}", handles, after, scatter=True), own))

    def push(l, part, big, after):
        cut = lambda a, n, axis: jnp.stack([lax.slice_in_dim(a, j * n, (j + 1) * n, axis=axis) for j in range(N_DEV)])
        sent = {}
        if part == "down":
            sent["w_down"] = big["w_down"].reshape(N_DEV, D_FF // N_DEV, D_MODEL)
        elif part == "up":
            sent["w_up"] = jnp.stack([lax.slice_in_dim(big[half_], j * up_cols, (j + 1) * up_cols, axis=1)
                                      for half_ in ("w_up_a", "w_up_v") for j in range(half)])
        elif l < N_A_LAYERS:
            pool_grads[l] = big["pool_w"]
        else:
            ext = cut(big["w_uq_ext"], Q_EXT, 1)
            rope = ext[..., QK_NOPE:QK_HEAD] + _unswap_halves(ext[..., QK_HEAD:])
            sent.update(w_dq=big["w_dq"].reshape(N_DEV, D_MODEL // N_DEV, Q_RANK), w_uq=jnp.concatenate([ext[..., :QK_NOPE], rope], axis=-1),
                        w_o=big["w_o"].reshape(N_DEV, D_MODEL // N_DEV, D_MODEL))
        if part == "mix" and l == N_A_LAYERS:
            sent.update(w_dkv=_fold_w_dkv_grad(big["w_dkv_ext"]).reshape(N_DEV, D_MODEL // N_DEV, KV_RANK + QK_ROPE),
                        w_uk=cut(big["w_uk"], QK_NOPE, 1), w_uv=cut(big["w_uv"], V_HEAD, 1))
        if l == 0 and part != "mix":
            return start_scatter(f"0_{part}", sent)
        if l == 0:
            finish_scatter("1", 1, after)
            pool = _shard8(jnp.stack([pool_grads[a] for a in range(N_A_LAYERS)]), 2).astype(BF16)
            reduce_pieces(0, ["pool_w"], _exchange_many("scatter_pool_grads", [pool], scatter=True))
            return 0.0
        pending.setdefault(l, {}).update(sent)
        if part != "mix":
            return 0.0
        if l + 1 < DEPTH:
            finish_scatter(str(l + 1), l + 1, after)
        return start_scatter(str(l), pending.pop(l))

    loss_row, dx, g, dmods = _forward_backward(x[0], loss_target[0], mods, tabq, tabk, final_g, fetch, push)
    layers_of = lambda k, ls: jnp.stack([piece_grads[(k, l)] for l in ls])
    grads = dict(w_dkv=piece_grads[("w_dkv", N_A_LAYERS)], w_uk=piece_grads[("w_uk", N_A_LAYERS)], w_uv=piece_grads[("w_uv", N_A_LAYERS)])
    for k in ("w_dq", "w_uq", "w_o"):
        grads[k] = layers_of(k, range(N_A_LAYERS, DEPTH))

    small_names = REPLICATED_WEIGHTS + tuple(k for k, _ in VECTOR_WEIGHTS)
    small_out = [dmods] + [g[k] for k in small_names] + [loss_row]
    small_shapes = [a.shape for a in small_out]
    small_got = _exchange("gather_small_grads", _pack(small_out, F32, SMALL_ROW_MULTIPLE), scatter=False)
    summed = _unpack(_sum8("sum_small_grads", small_got), small_shapes)
    grads["mod_b"] = summed[0]
    for k, s in zip(small_names, summed[1:-1]):
        grads[k] = s
    for k, ax in VECTOR_WEIGHTS:
        n = shard[k].shape[ax]
        grads[k] = lax.dynamic_slice_in_dim(grads[k], me * n, n, axis=ax)
    loss = summed[-1][0, 0]
    dmods_all = _unpack(small_got, small_shapes)[0]
    dm_mine = lax.dynamic_slice_in_dim(dmods_all, me * mod_cols, mod_cols, axis=2)
    dm_mine = jnp.pad(jnp.moveaxis(dm_mine, 0, 1), ((0, 0), (0, N_DEV), (0, 0)))
    grads["mod_w"] = _mods_bwd("mods_bwd", c_all, dm_mine)

    delta, new_m, new_v = {}, {}, {}

    def adamw(k):
        shp = shard[k].shape
        grads[k] = grads[k].reshape(shp)
        d_, m_, v_ = _adamw(f"adamw_{k}", _as_2d(shard[k]), _as_2d(grads[k]), _as_2d(mom_m[k]), _as_2d(mom_v[k]))
        delta[k], new_m[k], new_v[k] = d_.reshape(shp), m_.reshape(shp), v_.reshape(shp)

    late = ("w_up", "w_down", "pool_w")
    for k in WEIGHT_ORDER:
        if k not in late:
            adamw(k)
    finish_scatter("0_down", 0, delta["final_g"])
    finish_scatter("0_up", 0, delta["final_g"])
    grads.update(w_up=layers_of("w_up", range(DEPTH)), w_down=layers_of("w_down", range(DEPTH)), pool_w=piece_grads[("pool_w", 0)])
    for k in late:
        adamw(k)
    return (loss, dx[None], *[grads[k] for k in WEIGHT_ORDER], *[delta[k] for k in WEIGHT_ORDER],
            *[new_m[k] for k in WEIGHT_ORDER], *[new_v[k] for k in WEIGHT_ORDER])
```

```python
import functools

import jax
import jax.numpy as jnp
from jax import lax
from jax.experimental import pallas as pl
from jax.experimental.pallas import tpu as pltpu

F32 = jnp.float32
BF16 = jnp.bfloat16

D_MODEL = 1024
DEPTH = 4
N_A_LAYERS = 2
N_B_LAYERS = 2
POOL_WINDOWS = (2, 4, 8, 16)
POOL_GROUP = 256
N_HEADS = 8
QK_NOPE = 128
QK_ROPE = 64
V_HEAD = 128
QK_HEAD = QK_NOPE + QK_ROPE
Q_RANK = 384
KV_RANK = 256
ROPE_THETA = 10000.0
D_FF = 2816
EPS = 1e-6
N_MOD = 6
ADAM_LR = 0.001
ADAM_B1 = 0.9
ADAM_B2 = 0.999
ADAM_EPS = 1e-08
ADAM_WD = 0.01
ADAM_STEP = 10

N_DEV = 8
LANES = 128
Q_EXT = 256
VMEM_LIMIT_BYTES = 48 * 1024 * 1024
MESH = pl.DeviceIdType.MESH
NEG_BIG = -0.7 * float(jnp.finfo(jnp.float32).max)


def _params(sem):
    return pltpu.CompilerParams(dimension_semantics=sem, vmem_limit_bytes=VMEM_LIMIT_BYTES)


def _tile(n, cap):
    if n <= cap:
        return n
    best = None
    for d in range(LANES, cap + 1, LANES):
        if n % d == 0:
            best = d
    assert best is not None, (n, cap)
    return best


def _dot(a, b, dims):
    return lax.dot_general(a, b, (dims, ((), ())), preferred_element_type=F32)


NN = ((1,), (0,))
NT = ((1,), (1,))
TN = ((0,), (0,))


def _mm(name, a, b, mode="nn", out_dtype=BF16, add=None, resid=None, gate=None, rowtab=None,
        tm_cap=1024, tn_cap=1408, tk_cap=1408):
    if mode == "tn":
        kdim, m = a.shape
    else:
        m, kdim = a.shape
    n = b.shape[0] if mode == "nt" else b.shape[1]
    tm, tn, tk = _tile(m, tm_cap), _tile(n, tn_cap), _tile(kdim, tk_cap)
    nk = kdim // tk
    dims = {"nn": NN, "nt": NT, "tn": TN}[mode]
    a_spec = pl.BlockSpec((tk, tm), lambda i, j, k: (k, i)) if mode == "tn" else pl.BlockSpec((tm, tk), lambda i, j, k: (i, k))
    b_spec = pl.BlockSpec((tn, tk), lambda i, j, k: (j, k)) if mode == "nt" else pl.BlockSpec((tk, tn), lambda i, j, k: (k, j))
    o_spec = pl.BlockSpec((tm, tn), lambda i, j, k: (i, j))
    g_spec = pl.BlockSpec((1, tn), lambda i, j, k: (0, j))
    gated = resid is not None

    def body(*refs):
        a_ref, b_ref = refs[0], refs[1]
        acc = refs[-1]
        k = pl.program_id(2)

        @pl.when(k == 0)
        def _():
            acc[...] = jnp.zeros_like(acc)

        acc[...] += _dot(a_ref[...].astype(BF16), b_ref[...].astype(BF16), dims)

        @pl.when(k == nk - 1)
        def _():
            if gated:
                r_ref, g_ref, y_ref, x_ref = refs[2:6]
                y_ref[...] = acc[...]
                x_ref[...] = r_ref[...] + g_ref[...] * acc[...]
            elif add is not None:
                refs[3][...] = (acc[...] + refs[2][...].astype(F32)).astype(out_dtype)
            elif rowtab is not None:
                tab = refs[2][...]
                refs[3][...] = (acc[...] * jnp.concatenate([tab] * (tn // tab.shape[1]), axis=1)).astype(out_dtype)
            else:
                refs[2][...] = acc[...].astype(out_dtype)

    ins, in_specs = [a, b], [a_spec, b_spec]
    if rowtab is not None:
        assert tn % rowtab.shape[1] == 0 and not gated and add is None
        ins.append(rowtab)
        in_specs.append(pl.BlockSpec((tm, rowtab.shape[1]), lambda i, j, k: (i, 0)))
    if gated:
        ins += [resid, gate]
        in_specs += [o_spec, g_spec]
        out_shape = (jax.ShapeDtypeStruct((m, n), F32), jax.ShapeDtypeStruct((m, n), F32))
        out_specs = (o_spec, o_spec)
    else:
        if add is not None:
            ins.append(add)
            in_specs.append(o_spec)
        out_shape = jax.ShapeDtypeStruct((m, n), out_dtype)
        out_specs = o_spec
    return pl.pallas_call(
        body, name=name, grid=(m // tm, n // tn, nk), in_specs=in_specs, out_specs=out_specs, out_shape=out_shape,
        scratch_shapes=[pltpu.VMEM((tm, tn), F32)],
        compiler_params=_params(("parallel", "parallel", "arbitrary")),
    )(*ins)


def _rowwise(name, fn, tiled, bcast, outs, sums=(), tr=512):
    tiled = [t if isinstance(t, tuple) else (t, t.shape[1], 0) for t in tiled]
    s = tiled[0][0].shape[0]
    tr = min(tr, s)
    assert s % tr == 0
    n_t, n_b, n_o = len(tiled), len(bcast), len(outs)

    def body(*refs):
        i = pl.program_id(0)
        vals = [r[...] for r in refs[:n_t + n_b]]
        o_vals, s_vals = fn(*vals)
        for r, v in zip(refs[n_t + n_b:n_t + n_b + n_o], o_vals):
            r[...] = v.astype(r.dtype)
        s_refs = refs[n_t + n_b + n_o:]

        @pl.when(i == 0)
        def _():
            for r in s_refs:
                r[...] = jnp.zeros_like(r)

        for r, v in zip(s_refs, s_vals):
            r[...] += v

    in_specs = [pl.BlockSpec((tr, n), functools.partial(lambda cb, i: (i, cb), cb)) for (_, n, cb) in tiled]
    in_specs += [pl.BlockSpec(b.shape, functools.partial(lambda nd, i: (0,) * nd, b.ndim)) for b in bcast]
    out_specs = [pl.BlockSpec((tr, n), lambda i: (i, 0)) for (n, _) in outs]
    out_specs += [pl.BlockSpec((1, n), lambda i: (0, 0)) for n in sums]
    out_shape = [jax.ShapeDtypeStruct((s, n), dt) for (n, dt) in outs]
    out_shape += [jax.ShapeDtypeStruct((1, n), F32) for n in sums]
    res = pl.pallas_call(
        body, name=name, grid=(s // tr,), in_specs=in_specs, out_specs=tuple(out_specs), out_shape=tuple(out_shape),
        compiler_params=_params(("arbitrary",)),
    )(*[t[0] for t in tiled], *bcast)
    return res


def _colsum(v):
    return jnp.sum(v, axis=0, keepdims=True)


def _rms_fwd(name, x, g, scale=None, shift=None, out_dtype=BF16, ncols=None):
    mod = scale is not None

    def fn(xv, gv, *ss):
        y = xv * lax.rsqrt(jnp.mean(xv * xv, axis=-1, keepdims=True) + EPS) * gv
        if mod:
            y = y * (1.0 + ss[0]) + ss[1]
        return (y,), ()

    n = ncols or x.shape[1]
    return _rowwise(name, fn, [(x, n, 0)], [g] + ([scale, shift] if mod else []), [(n, out_dtype)])[0]


def _rms_bwd(name, x, g, dh, scale=None, dx_in=None, ncols=None, out_dtype=F32):
    mod = scale is not None
    has_in = dx_in is not None

    def fn(*vals):
        xv, dhv = vals[0], vals[1].astype(F32)
        rest = list(vals[2:])
        dxi = rest.pop(0) if has_in else None
        gv = rest.pop(0)
        rstd = lax.rsqrt(jnp.mean(xv * xv, axis=-1, keepdims=True) + EPS)
        xhat = xv * rstd
        sums = []
        if mod:
            sc = rest.pop(0)
            dyn = dhv * (1.0 + sc)
            dshift, dscale = _colsum(dhv), _colsum(dhv * (xhat * gv))
        else:
            dyn = dhv
        dg = _colsum(dyn * xhat)
        dxhat = dyn * gv
        dx = rstd * (dxhat - xhat * jnp.mean(dxhat * xhat, axis=-1, keepdims=True))
        if has_in:
            dx = dx + dxi
        sums = [dg] + ([dshift, dscale] if mod else [])
        return (dx,), sums

    n = ncols or x.shape[1]
    tiled = [(x, n, 0), dh] + ([dx_in] if has_in else [])
    return _rowwise(name, fn, tiled, [g] + ([scale] if mod else []), [(n, out_dtype)], [n] * (3 if mod else 1))


def _gate_bwd(name, dxn, y, g):
    def fn(dv, yv, gv):
        return (gv * dv,), (_colsum(dv * yv),)

    n = dxn.shape[1]
    return _rowwise(name, fn, [dxn, y], [g], [(n, BF16)], [n])


def _loss_head(name, x, g, target):
    n = x.shape[1]

    def fn(xv, tv, gv):
        rstd = lax.rsqrt(jnp.mean(xv * xv, axis=-1, keepdims=True) + EPS)
        xhat = xv * rstd
        err = xhat * gv - tv
        loss = 0.5 * jnp.sum(jnp.sum(err * err, axis=-1, keepdims=True) / n, axis=0, keepdims=True)
        dy = err / n
        dg = _colsum(dy * xhat)
        dxhat = dy * gv
        dx = rstd * (dxhat - xhat * jnp.mean(dxhat * xhat, axis=-1, keepdims=True))
        return (dx,), (dg, jnp.broadcast_to(loss, (1, LANES)))

    return _rowwise(name, fn, [x, target], [g], [(n, F32)], [n, LANES])


def _krope_fwd(name, kv_ext, tabk):
    def fn(xv, tv):
        t = xv * tv
        return (t + pltpu.roll(t, 64, 1),), ()

    return _rowwise(name, fn, [(kv_ext, LANES, 2), tabk], [], [(LANES, BF16)])[0]


def _krope_bwd(name, dkd, tabk):
    def fn(dv, tv):
        return ((dv + pltpu.roll(dv, 64, 1)) * tv,), ()

    return _rowwise(name, fn, [dkd, tabk], [], [(LANES, F32)])[0]


def _adamw(name, w, g, m, v):
    def fn(wv, gv, mv, vv):
        m2 = ADAM_B1 * mv + (1.0 - ADAM_B1) * gv
        v2 = ADAM_B2 * vv + (1.0 - ADAM_B2) * (gv * gv)
        m_hat = m2 / (1.0 - ADAM_B1 ** ADAM_STEP)
        v_hat = v2 / (1.0 - ADAM_B2 ** ADAM_STEP)
        delta = -ADAM_LR * (m_hat / (jnp.sqrt(v_hat) + ADAM_EPS) + ADAM_WD * wv)
        return (delta, m2, v2), ()

    r, c = w.shape
    tr = r
    for cand in (512, 256, 128, 64, 32, 16, 8):
        if r % cand == 0 and r > cand:
            tr = cand
            break
    return _rowwise(name, fn, [w, g, m, v], [], [(c, F32)] * 3, tr=tr)


def _sum8(name, parts):
    _, r, c = parts.shape
    tr = r
    for cand in (2048, 1024, 512, 256, 128, 64, 32, 16):
        if r % cand == 0 and r > cand and cand * c <= 256 * 1024:
            tr = cand
            break

    def body(p_ref, o_ref):
        acc = p_ref[0].astype(F32)
        for k in range(1, N_DEV):
            acc = acc + p_ref[k].astype(F32)
        o_ref[...] = acc

    return pl.pallas_call(
        body, name=name, grid=(r // tr,), in_specs=[pl.BlockSpec((N_DEV, tr, c), lambda i: (0, i, 0))],
        out_specs=pl.BlockSpec((tr, c), lambda i: (i, 0)), out_shape=jax.ShapeDtypeStruct((r, c), F32),
        compiler_params=_params(("parallel",)),
    )(parts)


def _mods_fwd(name, c_all, w, b):
    depth, d, n = w.shape

    def body(c_ref, w_ref, b_ref, o_ref):
        cv = c_ref[...]
        sc = (cv * (1.0 / (1.0 + jnp.exp(-cv)))).astype(BF16)
        o_ref[0] = _dot(sc, w_ref[0].astype(BF16), NN) + b_ref[0]

    return pl.pallas_call(
        body, name=name, grid=(depth,),
        in_specs=[pl.BlockSpec(c_all.shape, lambda l: (0, 0)), pl.BlockSpec((1, d, n), lambda l: (l, 0, 0)),
                  pl.BlockSpec((1, 1, n), lambda l: (l, 0, 0))],
        out_specs=pl.BlockSpec((1, c_all.shape[0], n), lambda l: (l, 0, 0)),
        out_shape=jax.ShapeDtypeStruct((depth, c_all.shape[0], n), F32),
        compiler_params=_params(("parallel",)),
    )(c_all, w, b.reshape(depth, 1, n))


def _mods_bwd(name, c_all, dm):
    depth, rows, n = dm.shape
    d = c_all.shape[1]

    def body(c_ref, dm_ref, o_ref):
        cv = c_ref[...]
        sc = (cv * (1.0 / (1.0 + jnp.exp(-cv)))).astype(BF16)
        o_ref[0] = _dot(sc, dm_ref[0].astype(BF16), TN)

    return pl.pallas_call(
        body, name=name, grid=(depth,),
        in_specs=[pl.BlockSpec(c_all.shape, lambda l: (0, 0)), pl.BlockSpec((1, rows, n), lambda l: (l, 0, 0))],
        out_specs=pl.BlockSpec((1, d, n), lambda l: (l, 0, 0)),
        out_shape=jax.ShapeDtypeStruct((depth, d, n), F32),
        compiler_params=_params(("parallel",)),
    )(c_all, dm)


POOL_TILE = 256


def _split_dot(band, val):
    hi = val.astype(BF16)
    lo = (val - hi.astype(F32)).astype(BF16)
    return _dot(band, hi, NN) + _dot(band, lo, NN)


def _pool_fwd(name, h1, x, pw, pb, ps, g1):
    s, d = h1.shape
    t = POOL_TILE

    def body(hc_ref, hp_ref, x_ref, pw_ref, pb_ref, ps_ref, g_ref, xo_ref, zb_ref, pooled_ref):
        i = pl.program_id(0)
        r = lax.broadcasted_iota(jnp.int32, (t, t), 0)
        j = lax.broadcasted_iota(jnp.int32, (t, t), 1)
        pos = (i * t + lax.broadcasted_iota(jnp.int32, (t, 1), 0) + 1).astype(F32)
        has_prev = (i > 0).astype(F32)
        for grp, w in enumerate(POOL_WINDOWS):
            cs = slice(grp * POOL_GROUP, (grp + 1) * POOL_GROUP)
            hc = hc_ref[:, cs]
            band_cur = ((r - j >= 0) & (r - j < w)).astype(BF16)
            band_prev = (r + t - j < w).astype(BF16)
            ssum = _split_dot(band_cur, hc) + has_prev * _split_dot(band_prev, hp_ref[:, cs])
            pooled = (ssum / jnp.minimum(pos, float(w)) - hc).astype(BF16)
            zb = _dot(pooled, pw_ref[grp], NN) + pb_ref[:, cs]
            xo_ref[:, cs] = x_ref[:, cs] + g_ref[:, cs] * (zb * ps_ref[:, cs])
            zb_ref[:, cs] = zb
            pooled_ref[:, cs] = pooled

    row = pl.BlockSpec((t, d), lambda i: (i, 0))
    vec = pl.BlockSpec((1, d), lambda i: (0, 0))
    return pl.pallas_call(
        body, name=name, grid=(s // t,),
        in_specs=[row, pl.BlockSpec((t, d), lambda i: (jnp.maximum(i - 1, 0), 0)), row,
                  pl.BlockSpec(pw.shape, lambda i: (0, 0, 0)), vec, vec, vec],
        out_specs=(row, row, row),
        out_shape=(jax.ShapeDtypeStruct((s, d), F32), jax.ShapeDtypeStruct((s, d), F32), jax.ShapeDtypeStruct((s, d), BF16)),
        compiler_params=_params(("parallel",)),
    )(h1, h1, x, pw, pb, ps, g1)


def _pool_bwd(name, dxn, zb, pooled, pw, ps, g1):
    s, d = dxn.shape
    t = POOL_TILE
    nt = s // t

    def body(dc_ref, dn_ref, zb_ref, pooled_ref, pw_ref, ps_ref, g_ref, dh_ref, dpw_ref, dpb_ref, dps_ref, dg_ref):
        i = pl.program_id(0)

        @pl.when(i == 0)
        def _():
            dpw_ref[...] = jnp.zeros_like(dpw_ref)
            dpb_ref[...] = jnp.zeros_like(dpb_ref)
            dps_ref[...] = jnp.zeros_like(dps_ref)
            dg_ref[...] = jnp.zeros_like(dg_ref)

        jj = lax.broadcasted_iota(jnp.int32, (t, t), 0)
        rr = lax.broadcasted_iota(jnp.int32, (t, t), 1)
        pos = (i * t + lax.broadcasted_iota(jnp.int32, (t, 1), 0) + 1).astype(F32)
        has_next = (i < nt - 1).astype(F32)
        for grp, w in enumerate(POOL_WINDOWS):
            cs = slice(grp * POOL_GROUP, (grp + 1) * POOL_GROUP)
            gv, psv, zbv, dxc = g_ref[:, cs], ps_ref[:, cs], zb_ref[:, cs], dc_ref[:, cs]
            dg_ref[:, cs] += _colsum(dxc * (zbv * psv))
            dy = gv * dxc
            dps_ref[:, cs] += _colsum(dy * zbv)
            dz = dy * psv
            dpb_ref[:, cs] += _colsum(dz)
            dzb = dz.astype(BF16)
            dpw_ref[grp] += _dot(pooled_ref[:, cs], dzb, TN)
            dp = _dot(dzb, pw_ref[grp], NT)
            dzn = (gv * dn_ref[:, cs] * psv).astype(BF16)
            dpn = _dot(dzn, pw_ref[grp], NT) * (has_next / float(w))
            band_cur = ((rr - jj >= 0) & (rr - jj < w)).astype(BF16)
            band_next = (rr + t - jj < w).astype(BF16)
            dh_ref[:, cs] = _split_dot(band_cur, dp / jnp.minimum(pos, float(w))) + _split_dot(band_next, dpn) - dp

    row = pl.BlockSpec((t, d), lambda i: (i, 0))
    vec = pl.BlockSpec((1, d), lambda i: (0, 0))
    wspec = pl.BlockSpec(pw.shape, lambda i: (0, 0, 0))
    return pl.pallas_call(
        body, name=name, grid=(nt,),
        in_specs=[row, pl.BlockSpec((t, d), lambda i: (jnp.minimum(i + 1, nt - 1), 0)), row, row, wspec, vec, vec],
        out_specs=(row, wspec, vec, vec, vec),
        out_shape=(jax.ShapeDtypeStruct((s, d), F32), jax.ShapeDtypeStruct(pw.shape, F32),
                   jax.ShapeDtypeStruct((1, d), F32), jax.ShapeDtypeStruct((1, d), F32), jax.ShapeDtypeStruct((1, d), F32)),
        compiler_params=_params(("arbitrary",)),
    )(dxn, dxn, zb, pooled, pw, ps, g1)


GLU_TILE = 256
HALO = 16
INV_SQRT2 = 0.7071067811865476
INV_SQRT_2PI = 0.3989422804014327


def _gelu(xv):
    return 0.5 * xv * (1.0 + lax.erf(xv * INV_SQRT2))


def _glu_fwd(name, ua, uv, cw, cb):
    s, f = ua.shape
    t, tf = GLU_TILE, _tile(f, 1408)

    def body(a_ref, ah_ref, v_ref, cw_ref, cb_ref, o_ref):
        i = pl.program_id(1)
        has_prev = (i > 0).astype(F32)
        ext = jnp.concatenate([ah_ref[...].astype(F32) * has_prev, a_ref[...].astype(F32)], axis=0)
        e1 = pltpu.roll(ext, 1, 0)[HALO:]
        e2 = pltpu.roll(ext, 2, 0)[HALO:]
        pre = e2 * cw_ref[0:1, :] + e1 * cw_ref[1:2, :] + ext[HALO:] * cw_ref[2:3, :] + cb_ref[...]
        o_ref[...] = (_gelu(pre) * v_ref[...].astype(F32)).astype(o_ref.dtype)

    blk = pl.BlockSpec((t, tf), lambda j, i: (i, j))
    halo = pl.BlockSpec((HALO, tf), lambda j, i: (jnp.maximum(i * (t // HALO) - 1, 0), j))
    return pl.pallas_call(
        body, name=name, grid=(f // tf, s // t),
        in_specs=[blk, halo, blk, pl.BlockSpec((3, tf), lambda j, i: (0, j)), pl.BlockSpec((1, tf), lambda j, i: (0, j))],
        out_specs=blk, out_shape=jax.ShapeDtypeStruct((s, f), BF16),
        compiler_params=_params(("parallel", "parallel")),
    )(ua, ua, uv, cw, cb)


def _glu_bwd(name, ua, uv, dgl, cw, cb):
    s, f = ua.shape
    t, tf = GLU_TILE, _tile(f, 1408)
    nt = s // t
    te = t + HALO

    def body(a_ref, ah_ref, an_ref, v_ref, vn_ref, d_ref, dn_ref, cw_ref, cb_ref, da_ref, dv_ref, dcw_ref, dcb_ref):
        i = pl.program_id(1)

        @pl.when(i == 0)
        def _():
            dcw_ref[...] = jnp.zeros_like(dcw_ref)
            dcb_ref[...] = jnp.zeros_like(dcb_ref)

        has_prev = (i > 0).astype(F32)
        has_next = (i < nt - 1).astype(F32)
        ext = jnp.concatenate([ah_ref[...].astype(F32) * has_prev, a_ref[...].astype(F32), an_ref[...].astype(F32)], axis=0)
        e0 = ext[HALO:]
        e1 = pltpu.roll(ext, 1, 0)[HALO:]
        e2 = pltpu.roll(ext, 2, 0)[HALO:]
        c0, c1, c2 = cw_ref[0:1, :], cw_ref[1:2, :], cw_ref[2:3, :]
        pre = e2 * c0 + e1 * c1 + e0 * c2 + cb_ref[...]
        vx = jnp.concatenate([v_ref[...].astype(F32), vn_ref[...].astype(F32)], axis=0)
        dx = jnp.concatenate([d_ref[...].astype(F32), dn_ref[...].astype(F32) * has_next], axis=0)
        cdf = 0.5 * (1.0 + lax.erf(pre * INV_SQRT2))
        dpre = dx * vx * (cdf + pre * (INV_SQRT_2PI * jnp.exp(-0.5 * pre * pre)))
        up1 = pltpu.roll(dpre, te - 1, 0)
        up2 = pltpu.roll(dpre, te - 2, 0)
        da_ref[...] = (dpre * c2 + up1 * c1 + up2 * c0)[:t].astype(da_ref.dtype)
        dv_ref[...] = (dx * (pre * cdf))[:t].astype(dv_ref.dtype)
        dpt = dpre[:t]
        dcb_ref[...] += _colsum(dpt)
        dcw_ref[0:1, :] += _colsum(e2[:t] * dpt)
        dcw_ref[1:2, :] += _colsum(e1[:t] * dpt)
        dcw_ref[2:3, :] += _colsum(e0[:t] * dpt)

    blk = pl.BlockSpec((t, tf), lambda j, i: (i, j))
    prev = pl.BlockSpec((HALO, tf), lambda j, i: (jnp.maximum(i * (t // HALO) - 1, 0), j))
    nxt = pl.BlockSpec((HALO, tf), lambda j, i: (jnp.minimum((i + 1) * (t // HALO), s // HALO - 1), j))
    w3 = pl.BlockSpec((3, tf), lambda j, i: (0, j))
    w1 = pl.BlockSpec((1, tf), lambda j, i: (0, j))
    return pl.pallas_call(
        body, name=name, grid=(f // tf, nt),
        in_specs=[blk, prev, nxt, blk, nxt, blk, nxt, w3, w1],
        out_specs=(blk, blk, w3, w1),
        out_shape=(jax.ShapeDtypeStruct((s, f), BF16), jax.ShapeDtypeStruct((s, f), BF16),
                   jax.ShapeDtypeStruct((3, f), F32), jax.ShapeDtypeStruct((1, f), F32)),
        compiler_params=_params(("parallel", "arbitrary")),
    )(ua, ua, ua, uv, uv, dgl, dgl, cw, cb)


ATT_TILE = 512
ATT_ROWS = 256
LOG2E = 1.4426950408889634
LN2 = 0.6931471805599453


def _head_blocks_t(a, width):
    s = a.shape[0]
    t = min(ATT_TILE, s)
    return a.reshape(s // t, t, N_HEADS, width).transpose(2, 0, 3, 1)


def _causal_mask(sv, q0, k0):
    row = q0 + lax.broadcasted_iota(jnp.int32, sv.shape, 0)
    col = k0 + lax.broadcasted_iota(jnp.int32, sv.shape, 1)
    return jnp.where(col <= row, sv, NEG_BIG)


def _attn_fwd(name, q_rot, kt4, v_ext):
    s = q_rot.shape[0]
    t = min(ATT_TILE, s)
    nq = s // t

    rq = min(ATT_ROWS, t)

    def body(q_ref, kt_ref, v_ref, o_ref, lse_ref, row_ref, acc_ref, m_ref):
        qi = pl.program_id(1)
        acc_ref[...] = jnp.zeros_like(acc_ref)
        m_ref[...] = jnp.full_like(m_ref, NEG_BIG)

        def step(j, masked):
            v_blk = v_ref[pl.ds(pl.multiple_of(j * t, t), t), :]
            for r in range(t // rq):
                rs = pl.ds(r * rq, rq)
                sv = _dot(q_ref[rs, :], kt_ref[0, j], NN)
                if masked:
                    sv = _causal_mask(sv, r * rq, 0)
                m_prev = m_ref[rs, :]
                m_new = jnp.maximum(m_prev, jnp.max(sv, axis=-1, keepdims=True))
                p = jnp.exp2(sv - m_new).astype(BF16)
                acc_ref[rs, :] = jnp.exp2(m_prev - m_new) * acc_ref[rs, :] + _dot(p, v_blk, NN)
                m_ref[rs, :] = m_new

        def full_step(j, carry):
            step(j, False)
            return carry

        lax.fori_loop(0, qi, full_step, 0)
        step(qi, True)
        l = acc_ref[:, V_HEAD:V_HEAD + 1]
        o_ref[...] = (acc_ref[:, :V_HEAD] / l).astype(o_ref.dtype)
        lse = jnp.broadcast_to(m_ref[...] + jnp.log(l) * LOG2E, lse_ref.shape)
        lse_ref[...] = lse
        row_ref[0, 0] = jnp.transpose(lse)[0:8, :]

    head_q = pl.BlockSpec((t, Q_EXT), lambda h, i: (i, h))
    head_o = pl.BlockSpec((t, V_HEAD), lambda h, i: (i, h))
    return pl.pallas_call(
        body, name=name, grid=(N_HEADS, nq),
        in_specs=[head_q, pl.BlockSpec((1, nq, Q_EXT, t), lambda h, i: (h, 0, 0, 0)), pl.BlockSpec((s, Q_EXT), lambda h, i: (0, h))],
        out_specs=(head_o, head_o, pl.BlockSpec((1, 1, 8, t), lambda h, i: (h, i, 0, 0))),
        out_shape=(jax.ShapeDtypeStruct((s, N_HEADS * V_HEAD), BF16), jax.ShapeDtypeStruct((s, N_HEADS * LANES), F32),
                   jax.ShapeDtypeStruct((N_HEADS, nq, 8, t), F32)),
        scratch_shapes=[pltpu.VMEM((t, Q_EXT), F32), pltpu.VMEM((t, 1), F32)],
        compiler_params=_params(("parallel", "parallel")),
    )(q_rot, kt4, v_ext)


def _attn_dq(name, q_rot, tabq, kt4, kfull, vt4, o, lse, do):
    s = q_rot.shape[0]
    t = min(ATT_TILE, s)
    nq = s // t

    def body(q_ref, tab_ref, kt_ref, k_ref, vt_ref, o_ref, lse_ref, do_ref, dq_ref, delta_ref, acc_ref):
        qi = pl.program_id(1)
        q = q_ref[...]
        dov = do_ref[...]
        delta = jnp.sum(dov.astype(F32) * o_ref[...].astype(F32), axis=-1, keepdims=True)
        lse = lse_ref[:, 0:1]
        acc_ref[...] = jnp.zeros_like(acc_ref)

        def step(j, masked):
            sv = _dot(q, kt_ref[0, j], NN)
            if masked:
                sv = _causal_mask(sv, qi * t, j * t)
            p = jnp.exp2(sv - lse)
            dp = _dot(dov, vt_ref[0, j], NN)
            ds = (p * (dp - delta)).astype(BF16)
            acc_ref[...] += _dot(ds, k_ref[pl.ds(pl.multiple_of(j * t, t), t), :], NN)

        def full_step(j, carry):
            step(j, False)
            return carry

        lax.fori_loop(0, qi, full_step, 0)
        step(qi, True)
        dq_ref[...] = (acc_ref[...] * (tab_ref[...] * LN2)).astype(dq_ref.dtype)
        delta_ref[0, 0] = jnp.transpose(jnp.broadcast_to(delta, (t, LANES)))[0:8, :]

    head_q = pl.BlockSpec((t, Q_EXT), lambda h, i: (i, h))
    head_o = pl.BlockSpec((t, V_HEAD), lambda h, i: (i, h))
    return pl.pallas_call(
        body, name=name, grid=(N_HEADS, nq),
        in_specs=[head_q, pl.BlockSpec((t, Q_EXT), lambda h, i: (i, 0)), pl.BlockSpec((1, nq, Q_EXT, t), lambda h, i: (h, 0, 0, 0)),
                  pl.BlockSpec((s, Q_EXT), lambda h, i: (0, h)), pl.BlockSpec((1, nq, V_HEAD, t), lambda h, i: (h, 0, 0, 0)),
                  head_o, head_o, head_o],
        out_specs=(head_q, pl.BlockSpec((1, 1, 8, t), lambda h, i: (h, i, 0, 0))),
        out_shape=(jax.ShapeDtypeStruct((s, N_HEADS * Q_EXT), BF16), jax.ShapeDtypeStruct((N_HEADS, nq, 8, t), F32)),
        scratch_shapes=[pltpu.VMEM((t, Q_EXT), F32)],
        compiler_params=_params(("parallel", "parallel")),
    )(q_rot, tabq, kt4, kfull, vt4, o, lse, do)


def _attn_dkv(name, kfull, v, qt4, q_rot, dot4, do, lse_row, delta_row, acc_in=None):
    s = kfull.shape[0]
    t = min(ATT_TILE, s)
    nq = s // t
    has_in = acc_in is not None

    def body(*refs):
        k_ref, v_ref, qt_ref, q_ref, dot_ref, do_ref, lse_ref, delta_ref = refs[:8]
        dkn_ref, dkd_ref, dv_ref, acck_ref, accv_ref = refs[-5:]
        kj, h = pl.program_id(0), pl.program_id(1)
        k_blk, v_blk = k_ref[...], v_ref[...]
        acck_ref[...] = jnp.zeros_like(acck_ref)
        accv_ref[...] = jnp.zeros_like(accv_ref)

        def step(i, masked):
            qs = pl.ds(pl.multiple_of(i * t, t), t)
            st = _dot(k_blk, qt_ref[0, i], NN)
            if masked:
                krow = lax.broadcasted_iota(jnp.int32, st.shape, 0)
                qcol = lax.broadcasted_iota(jnp.int32, st.shape, 1)
                st = jnp.where(krow <= qcol, st, NEG_BIG)
            pt = jnp.exp2(st - lse_ref[0, i, 0:1, :])
            accv_ref[...] += _dot(pt.astype(BF16), do_ref[qs, :], NN)
            dpt = _dot(v_blk, dot_ref[0, i], NN)
            dst = (pt * (dpt - delta_ref[0, i, 0:1, :])).astype(BF16)
            acck_ref[...] += _dot(dst, q_ref[qs, :], NN)

        def full_step(i, carry):
            step(i, False)
            return carry

        step(kj, True)
        lax.fori_loop(kj + 1, nq, full_step, 0)
        dk = acck_ref[...] * LN2
        dkn, dkd = dk[:, :QK_NOPE], dk[:, QK_NOPE:]
        if has_in:
            dkn = dkn + refs[8][...]
            dv_ref[...] = accv_ref[...] + refs[10][...]
        else:
            dv_ref[...] = accv_ref[...]
        dkn_ref[...] = dkn

        @pl.when(h == 0)
        def _():
            if has_in:
                dkd_ref[...] = dkd + refs[9][...]
            else:
                dkd_ref[...] = dkd

        @pl.when(h > 0)
        def _():
            dkd_ref[...] += dkd

    kblk = pl.BlockSpec((t, LANES), lambda j, h: (j, h))
    kdblk = pl.BlockSpec((t, LANES), lambda j, h: (j, 0))
    col = pl.BlockSpec((s, LANES), lambda j, h: (0, h))
    stat = pl.BlockSpec((1, nq, 8, t), lambda j, h: (h, 0, 0, 0))
    ins = [kfull, v, qt4, q_rot, dot4, do, lse_row, delta_row]
    in_specs = [pl.BlockSpec((t, Q_EXT), lambda j, h: (j, h)), kblk, pl.BlockSpec((1, nq, Q_EXT, t), lambda j, h: (h, 0, 0, 0)),
                pl.BlockSpec((s, Q_EXT), lambda j, h: (0, h)), pl.BlockSpec((1, nq, V_HEAD, t), lambda j, h: (h, 0, 0, 0)), col, stat, stat]
    if has_in:
        ins += list(acc_in)
        in_specs += [kblk, kdblk, kblk]
    return pl.pallas_call(
        body, name=name, grid=(nq, N_HEADS), in_specs=in_specs, out_specs=(kblk, kdblk, kblk),
        out_shape=(jax.ShapeDtypeStruct((s, N_HEADS * LANES), F32), jax.ShapeDtypeStruct((s, LANES), F32),
                   jax.ShapeDtypeStruct((s, N_HEADS * LANES), F32)),
        scratch_shapes=[pltpu.VMEM((t, Q_EXT), F32), pltpu.VMEM((t, LANES), F32)],
        compiler_params=_params(("parallel", "arbitrary")),
    )(*ins)


def _swap_halves(w):
    half = w.shape[-1] // 2
    return jnp.concatenate([-w[..., half:], w[..., :half]], axis=-1)


def _unswap_halves(g):
    half = g.shape[-1] // 2
    return jnp.concatenate([g[..., half:], -g[..., :half]], axis=-1)


def _extend_w_uq(w):
    r = w.reshape(Q_RANK, N_HEADS, QK_HEAD)
    rope = r[..., QK_NOPE:]
    return jnp.concatenate([r[..., :QK_NOPE], rope, _swap_halves(rope)], axis=-1).reshape(Q_RANK, N_HEADS * Q_EXT)


def _fold_w_uq_grad(g):
    r = g.reshape(Q_RANK, N_HEADS, Q_EXT)
    rope = r[..., QK_NOPE:QK_HEAD] + _unswap_halves(r[..., QK_HEAD:])
    return jnp.concatenate([r[..., :QK_NOPE], rope], axis=-1).reshape(Q_RANK, N_HEADS * QK_HEAD)


def _extend_w_dkv(w):
    return jnp.concatenate([w, _swap_halves(w[:, KV_RANK:])], axis=-1)


def _fold_w_dkv_grad(g):
    rope = g[:, KV_RANK:KV_RANK + QK_ROPE] + _unswap_halves(g[:, KV_RANK + QK_ROPE:])
    return jnp.concatenate([g[:, :KV_RANK], rope], axis=-1)


def _rope_tables(positions):
    inv = 1.0 / (ROPE_THETA ** (jnp.arange(0, QK_ROPE, 2, dtype=F32) / QK_ROPE))
    ang = positions.astype(F32)[:, None] * inv
    cos, sin = jnp.cos(ang), jnp.sin(ang)
    tabk = jnp.concatenate([cos, cos, sin, sin], axis=-1)
    scale = QK_HEAD ** -0.5 * LOG2E
    tabq = jnp.concatenate([jnp.full((positions.shape[0], QK_NOPE), scale, F32), tabk * scale], axis=-1)
    return tabq, tabk


def _forward_backward(x, target, mods, tabq, tabk, final_g, fetch, push):
    row = lambda vec: vec.reshape(1, -1)
    mod = [[row(mods[l, k * D_MODEL:(k + 1) * D_MODEL]) for k in range(N_MOD)] for l in range(DEPTH)]
    saved, weights = [], []
    kv = None
    for l in range(DEPTH):
        w, tok = fetch(l, x)
        sh1, sc1, g1, sh2, sc2, g2 = mod[l]
        sh1 = sh1 + tok
        if l == N_A_LAYERS:
            kvn = _rms_fwd("kvin_fwd", x, row(w["kv_in_g"]))
            kv_ext = _mm("dkv_fwd", kvn, w["w_dkv_ext"], out_dtype=F32)
            ckv = _rms_fwd("ckv_fwd", kv_ext, row(w["ckv_norm_g"]), ncols=KV_RANK)
            kd = _krope_fwd("krope_fwd", kv_ext, tabk)
            kn, v = _mm("uk_fwd", ckv, w["w_uk"]), _mm("uv_fwd", ckv, w["w_uv"])
            heads = lambda a: [a[:, h * LANES:(h + 1) * LANES] for h in range(N_HEADS)]
            kfull = jnp.concatenate([part for kh in heads(kn) for part in (kh, kd)], axis=-1)
            v_ext = jnp.concatenate([part for vh in heads(v) for part in (vh, jnp.ones_like(vh))], axis=-1)
            kv = dict(x=x, kvn=kvn, kv_ext=kv_ext, ckv=ckv, v=v, kfull=kfull, v_ext=v_ext,
                      kt4=_head_blocks_t(kfull, Q_EXT), vt4=_head_blocks_t(v, V_HEAD))
        x_in = x
        if l < N_A_LAYERS:
            h1 = _rms_fwd(f"norm1_fwd_{l}", x, row(w["norm1_g"]), sc1, sh1, out_dtype=F32)
            x_mid, zb, pooled = _pool_fwd(f"pool_fwd_{l}", h1, x, w["pool_w"], row(w["pool_b"]), row(w["pool_scale"]), g1)
            mix = (zb, pooled)
        else:
            h1 = _rms_fwd(f"norm1_fwd_{l}", x, row(w["norm1_g"]), sc1, sh1)
            cq_pre = _mm(f"dq_fwd_{l}", h1, w["w_dq"], out_dtype=F32)
            cq = _rms_fwd(f"qnorm_fwd_{l}", cq_pre, row(w["q_norm_g"]))
            q_rot = _mm(f"uq_fwd_{l}", cq, w["w_uq_ext"], rowtab=tabq)
            o, lse, lse_row = _attn_fwd(f"attn_fwd_{l}", q_rot, kv["kt4"], kv["v_ext"])
            y, x_mid = _mm(f"wo_fwd_{l}", o, w["w_o"], resid=x, gate=g1)
            mix = (h1, cq_pre, cq, q_rot, o, lse, lse_row, y)
        h2 = _rms_fwd(f"norm2_fwd_{l}", x_mid, row(w["norm2_g"]), sc2, sh2)
        w_up_a, w_up_v = w["w_up"](h2)
        ua = _mm(f"up_a_fwd_{l}", h2, w_up_a)
        uv = _mm(f"up_v_fwd_{l}", h2, w_up_v)
        gl = _glu_fwd(f"glu_fwd_{l}", ua, uv, w["conv_w"], row(w["conv_b"]))
        w_down = w["w_down"](gl)
        y2, x = _mm(f"down_fwd_{l}", gl, w_down, resid=x_mid, gate=g2)
        saved.append((x_in, x_mid, h2, ua, uv, gl, y2, mix))
        weights.append(dict(w, w_up_a=w_up_a, w_up_v=w_up_v, w_down=w_down))

    dx, dfinal_g, loss = _loss_head("loss_head", x, row(final_g), target)
    g = {"final_g": dfinal_g.reshape(-1)}
    per_layer = {k: [None] * DEPTH for k in ("norm1_g", "norm2_g", "conv_w", "conv_b")}
    per_a = {k: [None] * N_A_LAYERS for k in ("pool_b", "pool_scale")}
    per_b = {k: [None] * N_B_LAYERS for k in ("q_norm_g",)}
    dmods = [None] * DEPTH
    dkv = None
    tok = 0.0
    for l in reversed(range(DEPTH)):
        w, big = weights[l], {}
        sh1, sc1, g1, sh2, sc2, g2 = mod[l]
        g2 = g2 + tok
        x_in, x_mid, h2, ua, uv, gl, y2, mix = saved[l]
        dy2, dg2 = _gate_bwd(f"gate2_bwd_{l}", dx, y2, g2)
        dgl = _mm(f"down_bwd_{l}", dy2, w["w_down"], mode="nt")
        tok = push(l, "down", dict(w_down=_mm(f"down_wgrad_{l}", gl, dy2, mode="tn", tm_cap=1408)), None)
        da, dv_, dcw, dcb = _glu_bwd(f"glu_bwd_{l}", ua, uv, dgl, w["conv_w"], row(w["conv_b"]) + tok)
        dh2 = _mm(f"up_a_bwd_{l}", da, w["w_up_a"], mode="nt", out_dtype=F32)
        dh2 = _mm(f"up_v_bwd_{l}", dv_, w["w_up_v"], mode="nt", out_dtype=F32, add=dh2)
        tok = push(l, "up", dict(w_up_a=_mm(f"up_a_wgrad_{l}", h2, da, mode="tn"), w_up_v=_mm(f"up_v_wgrad_{l}", h2, dv_, mode="tn")), None)
        per_layer["conv_w"][l], per_layer["conv_b"][l] = dcw, dcb.reshape(-1)
        dx_mid, dn2, dsh2, dsc2 = _rms_bwd(f"norm2_bwd_{l}", x_mid, row(w["norm2_g"]), dh2, sc2 + tok, dx_in=dx)
        per_layer["norm2_g"][l] = dn2.reshape(-1)
        if l < N_A_LAYERS:
            zb, pooled = mix
            dh1, dpw, dpb, dps, dg1 = _pool_bwd(f"pool_bwd_{l}", dx_mid, zb, pooled, w["pool_w"], row(w["pool_scale"]), g1)
            big["pool_w"] = dpw
            per_a["pool_b"][l], per_a["pool_scale"][l] = dpb.reshape(-1), dps.reshape(-1)
        else:
            j = l - N_A_LAYERS
            h1, cq_pre, cq, q_rot, o, lse, lse_row, y = mix
            dy, dg1 = _gate_bwd(f"gate1_bwd_{l}", dx_mid, y, g1)
            do = _mm(f"wo_bwd_{l}", dy, w["w_o"], mode="nt")
            big["w_o"] = _mm(f"wo_wgrad_{l}", o, dy, mode="tn")
            dq_ext, delta_row = _attn_dq(f"attn_dq_{l}", q_rot, tabq, kv["kt4"], kv["kfull"], kv["vt4"], o, lse, do)
            dkv = _attn_dkv(f"attn_dkv_{l}", kv["kfull"], kv["v"], _head_blocks_t(q_rot, Q_EXT), q_rot, _head_blocks_t(do, V_HEAD), do,
                            lse_row, delta_row, acc_in=dkv)
            dcq = _mm(f"uq_bwd_{l}", dq_ext, w["w_uq_ext"], mode="nt", out_dtype=F32)
            big["w_uq_ext"] = _mm(f"uq_wgrad_{l}", cq, dq_ext, mode="tn", out_dtype=F32)
            dcq_pre, dqn = _rms_bwd(f"qnorm_bwd_{l}", cq_pre, row(w["q_norm_g"]), dcq, out_dtype=BF16)
            per_b["q_norm_g"][j] = dqn.reshape(-1)
            dh1 = _mm(f"dq_bwd_{l}", dcq_pre, w["w_dq"], mode="nt")
            big["w_dq"] = _mm(f"dq_wgrad_{l}", h1, dcq_pre, mode="tn")
        dx, dn1, dsh1, dsc1 = _rms_bwd(f"norm1_bwd_{l}", x_in, row(w["norm1_g"]), dh1, sc1, dx_in=dx_mid)
        per_layer["norm1_g"][l] = dn1.reshape(-1)
        dmods[l] = jnp.concatenate([dsh1, dsc1, dg1, dsh2, dsc2, dg2], axis=-1).reshape(-1)
        if l == N_A_LAYERS:
            dkn, dkd, dv = dkv
            dckv = _mm("uk_bwd", dkn, w["w_uk"], mode="nt", out_dtype=F32)
            dckv = _mm("uv_bwd", dv, w["w_uv"], mode="nt", out_dtype=F32, add=dckv)
            big["w_uk"] = _mm("uk_wgrad", kv["ckv"], dkn, mode="tn")
            big["w_uv"] = _mm("uv_wgrad", kv["ckv"], dv, mode="tn")
            dkr = _krope_bwd("krope_bwd", dkd, tabk)
            dc, dckv_g = _rms_bwd("ckv_bwd", kv["kv_ext"], row(w["ckv_norm_g"]), dckv, ncols=KV_RANK, out_dtype=BF16)
            dkv_ext = jnp.concatenate([dc, dkr.astype(BF16)], axis=-1)
            dkvn = _mm("dkv_bwd", dkv_ext, w["w_dkv_ext"], mode="nt")
            big["w_dkv_ext"] = _mm("dkv_wgrad", kv["kvn"], dkv_ext, mode="tn", out_dtype=F32)
            dx, dkv_in_g = _rms_bwd("kvin_bwd", kv["x"], row(w["kv_in_g"]), dkvn, dx_in=dx)
            g["ckv_norm_g"], g["kv_in_g"] = dckv_g.reshape(-1), dkv_in_g.reshape(-1)
        tok = push(l, "mix", big, dx)
    for group in (per_layer, per_a, per_b):
        for k, vals in group.items():
            g[k] = jnp.stack(vals)
    return loss, dx, g, jnp.stack(dmods)


def _my_index():
    return 4 * lax.axis_index("x") + 2 * lax.axis_index("y") + lax.axis_index("c")


def _peer(k):
    x, y, c = lax.axis_index("x"), lax.axis_index("y"), lax.axis_index("c")
    return (1 - x if k & 4 else x, 1 - y if k & 2 else y, 1 - c if k & 1 else c)


def _index_of(pos):
    return 4 * pos[0] + 2 * pos[1] + pos[2]


def _exchange_many(name, arrays, scatter):
    n = len(arrays)
    blocks = [tuple(a.shape[1:]) if scatter else tuple(a.shape) for a in arrays]

    def body(*refs):
        x_refs, o_refs = refs[:n], refs[n:2 * n]
        send_sems, recv_sems, local_sems = refs[2 * n:]
        me = _my_index()
        started = []
        for a in range(n):
            mine = pltpu.make_async_copy(x_refs[a].at[me] if scatter else x_refs[a], o_refs[a].at[me], local_sems.at[a])
            mine.start()
            started.append(mine)
        sends = []
        for k in range(1, N_DEV):
            peer = _peer(k)
            for a in range(n):
                cp = pltpu.make_async_remote_copy(
                    src_ref=x_refs[a].at[_index_of(peer)] if scatter else x_refs[a], dst_ref=o_refs[a].at[me],
                    send_sem=send_sems.at[a, k - 1], recv_sem=recv_sems.at[a, k - 1], device_id=peer, device_id_type=MESH)
                cp.start()
                sends.append(cp)
        for k in range(1, N_DEV):
            peer = _peer(k)
            for a in range(n):
                pltpu.make_async_remote_copy(
                    src_ref=x_refs[a].at[me] if scatter else x_refs[a], dst_ref=o_refs[a].at[_index_of(peer)],
                    send_sem=send_sems.at[a, k - 1], recv_sem=recv_sems.at[a, k - 1], device_id=peer, device_id_type=MESH).wait_recv()
        for cp in sends:
            cp.wait_send()
        for mine in started:
            mine.wait()

    return pl.pallas_call(
        body, name=name, out_shape=tuple(jax.ShapeDtypeStruct((N_DEV,) + blk, a.dtype) for blk, a in zip(blocks, arrays)),
        in_specs=[pl.BlockSpec(memory_space=pl.ANY)] * n, out_specs=tuple([pl.BlockSpec(memory_space=pl.ANY)] * n),
        scratch_shapes=[pltpu.SemaphoreType.DMA((n, N_DEV - 1)), pltpu.SemaphoreType.DMA((n, N_DEV - 1)), pltpu.SemaphoreType.DMA((n,))],
    )(*arrays)


def _exchange(name, x, scatter):
    return _exchange_many(name, [x], scatter)[0]


HBM_SPEC = pl.BlockSpec(memory_space=pltpu.HBM)
SEM_SPEC = pl.BlockSpec(memory_space=pltpu.SEMAPHORE)
DATAFLOW = pltpu.SideEffectType.DATAFLOW_SIDE_EFFECTING


def _remote_copies(x_refs, land_refs, send_sems, recv_sems, scatter, numbers=None):
    me = _my_index()
    numbers = list(range(len(x_refs))) if numbers is None else numbers
    out, inc = [], []
    for a in range(len(x_refs)):
        for k in range(1, N_DEV):
            peer = _peer(k)
            pair = numbers[a] * (N_DEV - 1) + k - 1
            sems = dict(send_sem=send_sems.at[pair], recv_sem=recv_sems.at[pair], device_id=peer, device_id_type=MESH)
            out.append(pltpu.make_async_remote_copy(
                src_ref=x_refs[a].at[_index_of(peer)] if scatter else x_refs[a], dst_ref=land_refs[a].at[me], **sems))
            inc.append(pltpu.make_async_remote_copy(
                src_ref=x_refs[a].at[me] if scatter else x_refs[a], dst_ref=land_refs[a].at[_index_of(peer)], **sems))
    return out, inc


def _exchange_start(name, arrays, scatter):
    n = len(arrays)
    blocks = [tuple(a.shape[1:]) if scatter else tuple(a.shape) for a in arrays]

    def body(*refs):
        x_refs, land_refs = refs[:n], refs[n:2 * n]
        send_sems, recv_sems = refs[2 * n], refs[2 * n + 1]
        for cp in _remote_copies(x_refs, land_refs, send_sems, recv_sems, scatter)[0]:
            cp.start()
        refs[-1][...] = jnp.zeros_like(refs[-1])

    sem_type = pltpu.SemaphoreType.DMA((n * (N_DEV - 1),))
    lands =[pltpu.with_memory_space_constraint(lax.empty((N_DEV,) + blk, a.dtype), pltpu.HBM) for blk, a in zip(blocks, arrays)]
    srcs = [pltpu.with_memory_space_constraint(a, pltpu.HBM) for a in arrays]
    res = pl.pallas_call(
        body, name=name,
        out_shape=(sem_type, sem_type, *[pltpu.HBM(a.shape, a.dtype) for a in srcs + lands], jax.ShapeDtypeStruct((8, LANES), F32)),
        in_specs=[HBM_SPEC] * (2 * n), out_specs=(SEM_SPEC, SEM_SPEC, *[HBM_SPEC] * (2 * n), pl.BlockSpec(memory_space=pltpu.VMEM)),
        input_output_aliases={i: 2 + i for i in range(2 * n)},
        compiler_params=pltpu.CompilerParams(has_side_effects=DATAFLOW),
    )(*srcs, *lands)
    return (res[0], res[1], list(res[2:2 + n]), list(res[2 + n:2 + 2 * n])), res[-1]


def _exchange_wait(name, handles, after, scatter, which=None):
    send_sems, recv_sems, srcs, lands = handles
    which = list(range(len(srcs))) if which is None else list(which)
    srcs, lands = [srcs[a] for a in which], [lands[a] for a in which]
    n = len(srcs)

    def body(*refs):
        x_refs, land_refs = refs[:n], refs[n:2 * n]
        out, inc = _remote_copies(x_refs, land_refs, refs[2 * n], refs[2 * n + 1], scatter, which)
        for cp in out:
            cp.wait_send()
        for cp in inc:
            cp.wait_recv()

    res = pl.pallas_call(
        body, name=name, out_shape=tuple(pltpu.HBM(a.shape, a.dtype) for a in srcs + lands),
        in_specs=[HBM_SPEC] * (2 * n) + [SEM_SPEC, SEM_SPEC, pl.BlockSpec(memory_space=pl.ANY)], out_specs=tuple([HBM_SPEC] * (2 * n)),
        input_output_aliases={i: i for i in range(2 * n)},
        compiler_params=pltpu.CompilerParams(has_side_effects=DATAFLOW),
    )(*srcs, *lands, send_sems, recv_sems, after)
    return list(res[n:])


def _pack(arrays, dtype, row_multiple):
    flat = jnp.concatenate([a.astype(dtype).reshape(-1) for a in arrays])
    rows = -(-flat.shape[0] // (LANES * row_multiple)) * row_multiple
    return jnp.pad(flat, (0, rows * LANES - flat.shape[0])).reshape(rows, LANES)


def _unpack(packed, shapes):
    lead = packed.shape[:-2]
    flat = packed.reshape(lead + (-1,))
    out, off = [], 0
    for shp in shapes:
        size = 1
        for d in shp:
            size *= d
        out.append(flat[..., off:off + size].reshape(lead + tuple(shp)))
        off += size
    return out


def _unshard(g8, axis):
    return jnp.concatenate([g8[j] for j in range(N_DEV)], axis=axis)


def _shard8(full, axis):
    n = full.shape[axis] // N_DEV
    return jnp.stack([lax.slice_in_dim(full, j * n, (j + 1) * n, axis=axis) for j in range(N_DEV)])


VECTOR_WEIGHTS = (("pool_b", 1), ("pool_scale", 1), ("conv_w", 2))
REPLICATED_WEIGHTS = ("norm1_g", "norm2_g", "kv_in_g", "ckv_norm_g", "q_norm_g", "conv_b", "final_g")
WEIGHT_ORDER = ("mod_w", "mod_b", "norm1_g", "norm2_g", "pool_w", "pool_b", "pool_scale", "kv_in_g", "w_dkv", "ckv_norm_g", "w_uk",
                "w_uv", "w_dq", "q_norm_g", "w_uq", "w_o", "w_up", "conv_w", "conv_b", "w_down", "final_g")
BIG_ROW_MULTIPLE = 1024
SMALL_ROW_MULTIPLE = 16


def _as_2d(a):
    if a.ndim == 1:
        return a.reshape(-1, LANES)
    return a.reshape(-1, a.shape[-1])


def kernel(x, c, positions, mod_w, mod_b, norm1_g, norm2_g, pool_w, pool_b, pool_scale, kv_in_g, w_dkv, ckv_norm_g, w_uk, w_uv, w_dq, q_norm_g, w_uq, w_o, w_up, conv_w, conv_b, w_down, final_g, loss_target, m_mod_w, m_mod_b, m_norm1_g, m_norm2_g, m_pool_w, m_pool_b, m_pool_scale, m_kv_in_g, m_w_dkv, m_ckv_norm_g, m_w_uk, m_w_uv, m_w_dq, m_q_norm_g, m_w_uq, m_w_o, m_w_up, m_conv_w, m_conv_b, m_w_down, m_final_g, v_mod_w, v_mod_b, v_norm1_g, v_norm2_g, v_pool_w, v_pool_b, v_pool_scale, v_kv_in_g, v_w_dkv, v_ckv_norm_g, v_w_uk, v_w_uv, v_w_dq, v_q_norm_g, v_w_uq, v_w_o, v_w_up, v_conv_w, v_conv_b, v_w_down, v_final_g):
    shard = dict(mod_w=mod_w, mod_b=mod_b, norm1_g=norm1_g, norm2_g=norm2_g, pool_w=pool_w, pool_b=pool_b, pool_scale=pool_scale,
                 kv_in_g=kv_in_g, w_dkv=w_dkv, ckv_norm_g=ckv_norm_g, w_uk=w_uk, w_uv=w_uv, w_dq=w_dq, q_norm_g=q_norm_g, w_uq=w_uq,
                 w_o=w_o, w_up=w_up, conv_w=conv_w, conv_b=conv_b, w_down=w_down, final_g=final_g)
    mom_m = dict(mod_w=m_mod_w, mod_b=m_mod_b, norm1_g=m_norm1_g, norm2_g=m_norm2_g, pool_w=m_pool_w, pool_b=m_pool_b,
                 pool_scale=m_pool_scale, kv_in_g=m_kv_in_g, w_dkv=m_w_dkv, ckv_norm_g=m_ckv_norm_g, w_uk=m_w_uk, w_uv=m_w_uv,
                 w_dq=m_w_dq, q_norm_g=m_q_norm_g, w_uq=m_w_uq, w_o=m_w_o, w_up=m_w_up, conv_w=m_conv_w, conv_b=m_conv_b,
                 w_down=m_w_down, final_g=m_final_g)
    mom_v = dict(mod_w=v_mod_w, mod_b=v_mod_b, norm1_g=v_norm1_g, norm2_g=v_norm2_g, pool_w=v_pool_w, pool_b=v_pool_b,
                 pool_scale=v_pool_scale, kv_in_g=v_kv_in_g, w_dkv=v_w_dkv, ckv_norm_g=v_ckv_norm_g, w_uk=v_w_uk, w_uv=v_w_uv,
                 w_dq=v_w_dq, q_norm_g=v_q_norm_g, w_uq=v_w_uq, w_o=v_w_o, w_up=v_w_up, conv_w=v_conv_w, conv_b=v_conv_b,
                 w_down=v_w_down, final_g=v_final_g)
    me = _my_index()
    d6 = N_MOD * D_MODEL
    mod_cols = d6 // N_DEV

    small_in = [c] + [shard[k] for k, _ in VECTOR_WEIGHTS]
    small_all = _exchange("gather_vectors", _pack(small_in, F32, SMALL_ROW_MULTIPLE), scatter=False)
    parts = _unpack(small_all, [a.shape for a in small_in])
    c_all = jnp.pad(parts[0].reshape(N_DEV, D_MODEL), ((0, N_DEV), (0, 0)))
    vec = {k: _unshard(p, ax) for (k, ax), p in zip(VECTOR_WEIGHTS, parts[1:])}

    my_mod_b = lax.dynamic_slice_in_dim(mod_b, me * mod_cols, mod_cols, axis=1)
    mods_mine = _mods_fwd("mods_fwd", c_all, mod_w, my_mod_b)
    mods_all = _exchange("gather_mods", _pack([mods_mine], F32, SMALL_ROW_MULTIPLE), scatter=False)
    mods_all = _unpack(mods_all, [mods_mine.shape])[0]
    mods = lax.dynamic_index_in_dim(mods_all, me, axis=2, keepdims=False)
    mods = jnp.moveaxis(mods, 0, 1).reshape(DEPTH, d6)

    tabq, tabk = _rope_tables(positions[0])
    half = N_DEV // 2
    up_cols = shard["w_up"].shape[2]
    cat = lambda a, axis, lo=0, hi=N_DEV: jnp.concatenate([a[j] for j in range(lo, hi)], axis=axis)

    def stage_pieces(l):
        out = {"pool_w": shard["pool_w"].astype(BF16)} if l == 0 else {}
        out.update(w_up=shard["w_up"][l].astype(BF16), w_down=shard["w_down"][l].astype(BF16))
        if l >= N_A_LAYERS:
            out.update({k: shard[k][l - N_A_LAYERS].astype(BF16) for k in ("w_dq", "w_uq", "w_o")})
        if l == N_A_LAYERS:
            out.update({k: shard[k].astype(BF16) for k in ("w_dkv", "w_uk", "w_uv")})
        return out

    gathers, pool_all = {}, []

    def start_gather(l, behind=None):
        pieces = stage_pieces(l)
        if behind is not None:
            pieces, _ = lax.optimization_barrier((pieces, behind))
        handles, token = _exchange_start(f"gather_start_{l}", list(pieces.values()), scatter=False)
        gathers[l] = (handles, pieces)
        return token[0, 0]

    def wait_gather(l, keys, after, tag=""):
        handles, pieces = gathers[l]
        which = [list(pieces).index(k) for k in keys]
        lands = _exchange_wait(f"gather_wait_{l}{tag}", handles, after, scatter=False, which=which)
        return dict(zip(keys, own_slot(lands, [pieces[k] for k in keys])))

    def whole_weights(l, got):
        w = dict(norm1_g=norm1_g[l], norm2_g=norm2_g[l], conv_w=vec["conv_w"][l], conv_b=conv_b[l])
        if l == 0:
            pool_all.append(got["pool_w"])
        if l < N_A_LAYERS:
            w.update(pool_w=cat(pool_all[0][:, l], 1), pool_b=vec["pool_b"][l], pool_scale=vec["pool_scale"][l])
        else:
            rope = got["w_uq"][..., QK_NOPE:]
            ext = jnp.concatenate([got["w_uq"][..., :QK_NOPE], rope, _swap_halves(rope)], axis=-1)
            w.update(w_dq=got["w_dq"].reshape(D_MODEL, Q_RANK), w_uq_ext=cat(ext, -1), w_o=got["w_o"].reshape(D_MODEL, D_MODEL),
                     q_norm_g=q_norm_g[l - N_A_LAYERS])
        if l == N_A_LAYERS:
            w.update(w_dkv_ext=_extend_w_dkv(got["w_dkv"].reshape(D_MODEL, KV_RANK + QK_ROPE)), w_uk=cat(got["w_uk"], -1),
                     w_uv=cat(got["w_uv"], -1), kv_in_g=kv_in_g, ckv_norm_g=ckv_norm_g)
        return w

    def own_slot(lands, own):
        return [lax.dynamic_update_index_in_dim(p, o, me, 0) for p, o in zip(lands, own)]

    def fetch(l, after):
        up_parts = lambda g8: (cat(g8, -1, 0, half), cat(g8, -1, half, N_DEV))
        if l == 0:
            start_gather(0, behind=mods)
            got = wait_gather(0, ["pool_w"], mods, "_pool")
            w_up = lambda aft: up_parts(wait_gather(0, ["w_up"], aft, "_up")["w_up"])
            w_down = lambda aft: wait_gather(0, ["w_down"], aft, "_down")["w_down"].reshape(D_FF, D_MODEL)
        else:
            got = wait_gather(l, list(gathers[l][1]), after)
            up, down = up_parts(got["w_up"]), got["w_down"].reshape(D_FF, D_MODEL)
            w_up, w_down = (lambda aft: up), (lambda aft: down)
        w = dict(whole_weights(l, got), w_up=w_up, w_down=w_down)
        return w, (start_gather(l + 1) if l + 1 < DEPTH else 0.0)

    scatters, pending, pool_grads, piece_grads = {}, {}, {}, {}

    def reduce_pieces(l, keys, got):
        for k, p in zip(keys, got):
            piece_grads[(k, l)] = _sum8(f"sum_grads_{k}_{l}", p.reshape(N_DEV, -1, p.shape[-1])).reshape(p.shape[1:])

    def start_scatter(name, sent):
        sent = {k: a.astype(BF16) for k, a in sent.items()}
        handles, token = _exchange_start(f"scatter_start_{name}", list(sent.values()), scatter=True)
        scatters[name] = (handles, list(sent), [lax.dynamic_index_in_dim(a, me, 0, keepdims=False) for a in sent.values()])
        return token[0, 0]

    def finish_scatter(name, l, after):
        handles, keys, own = scatters.pop(name)
        reduce_pieces(l, keys, own_slot(_exchange_wait(f"scatter_wait_{name}", handles, after, scatter=True), own))

    def push(l, part, big, after):
        cut = lambda a, n, axis: jnp.stack([lax.slice_in_dim(a, j * n, (j + 1) * n, axis=axis) for j in range(N_DEV)])
        sent = {}
        if part == "down":
            sent["w_down"] = big["w_down"].reshape(N_DEV, D_FF // N_DEV, D_MODEL)
        elif part == "up":
            sent["w_up"] = jnp.stack([lax.slice_in_dim(big[half_], j * up_cols, (j + 1) * up_cols, axis=1)
                                      for half_ in ("w_up_a", "w_up_v") for j in range(half)])
        elif l < N_A_LAYERS:
            pool_grads[l] = big["pool_w"]
        else:
            ext = cut(big["w_uq_ext"], Q_EXT, 1)
            rope = ext[..., QK_NOPE:QK_HEAD] + _unswap_halves(ext[..., QK_HEAD:])
            sent.update(w_dq=big["w_dq"].reshape(N_DEV, D_MODEL // N_DEV, Q_RANK), w_uq=jnp.concatenate([ext[..., :QK_NOPE], rope], axis=-1),
                        w_o=big["w_o"].reshape(N_DEV, D_MODEL // N_DEV, D_MODEL))
        if part == "mix" and l == N_A_LAYERS:
            sent.update(w_dkv=_fold_w_dkv_grad(big["w_dkv_ext"]).reshape(N_DEV, D_MODEL // N_DEV, KV_RANK + QK_ROPE),
                        w_uk=cut(big["w_uk"], QK_NOPE, 1), w_uv=cut(big["w_uv"], V_HEAD, 1))
        if l == 0 and part != "mix":
            return start_scatter(f"0_{part}", sent)
        if l == 0:
            finish_scatter("1", 1, after)
            pool = _shard8(jnp.stack([pool_grads[a] for a in range(N_A_LAYERS)]), 2).astype(BF16)
            reduce_pieces(0, ["pool_w"], _exchange_many("scatter_pool_grads", [pool], scatter=True))
            return 0.0
        pending.setdefault(l, {}).update(sent)
        if part != "mix":
            return 0.0
        if l + 1 < DEPTH:
            finish_scatter(str(l + 1), l + 1, after)
        return start_scatter(str(l), pending.pop(l))

    loss_row, dx, g, dmods = _forward_backward(x[0], loss_target[0], mods, tabq, tabk, final_g, fetch, push)
    layers_of = lambda k, ls: jnp.stack([piece_grads[(k, l)] for l in ls])
    grads = dict(w_dkv=piece_grads[("w_dkv", N_A_LAYERS)], w_uk=piece_grads[("w_uk", N_A_LAYERS)], w_uv=piece_grads[("w_uv", N_A_LAYERS)])
    for k in ("w_dq", "w_uq", "w_o"):
        grads[k] = layers_of(k, range(N_A_LAYERS, DEPTH))

    small_names = REPLICATED_WEIGHTS + tuple(k for k, _ in VECTOR_WEIGHTS)
    small_out = [dmods] + [g[k] for k in small_names] + [loss_row]
    small_shapes = [a.shape for a in small_out]
    small_got = _exchange("gather_small_grads", _pack(small_out, F32, SMALL_ROW_MULTIPLE), scatter=False)
    summed = _unpack(_sum8("sum_small_grads", small_got), small_shapes)
    grads["mod_b"] = summed[0]
    for k, s in zip(small_names, summed[1:-1]):
        grads[k] = s
    for k, ax in VECTOR_WEIGHTS:
        n = shard[k].shape[ax]
        grads[k] = lax.dynamic_slice_in_dim(grads[k], me * n, n, axis=ax)
    loss = summed[-1][0, 0]
    dmods_all = _unpack(small_got, small_shapes)[0]
    dm_mine = lax.dynamic_slice_in_dim(dmods_all, me * mod_cols, mod_cols, axis=2)
    dm_mine = jnp.pad(jnp.moveaxis(dm_mine, 0, 1), ((0, 0), (0, N_DEV), (0, 0)))
    grads["mod_w"] = _mods_bwd("mods_bwd", c_all, dm_mine)

    delta, new_m, new_v = {}, {}, {}

    def adamw(k):
        shp = shard[k].shape
        grads[k] = grads[k].reshape(shp)
        d_, m_, v_ = _adamw(f"adamw_{k}", _as_2d(shard[k]), _as_2d(grads[k]), _as_2d(mom_m[k]), _as_2d(mom_v[k]))
        delta[k], new_m[k], new_v[k] = d_.reshape(shp), m_.reshape(shp), v_.reshape(shp)

    late = ("w_up", "w_down", "pool_w")
    for k in WEIGHT_ORDER:
        if k not in late:
            adamw(k)
    finish_scatter("0_down", 0, delta["final_g"])
    finish_scatter("0_up", 0, delta["final_g"])
    grads.update(w_up=layers_of("w_up", range(DEPTH)), w_down=layers_of("w_down", range(DEPTH)), pool_w=piece_grads[("pool_w", 0)])
    for k in late:
        adamw(k)
    return (loss, dx[None], *[grads[k] for k in WEIGHT_ORDER], *[delta[k] for k in WEIGHT_ORDER],
            *[new_m[k] for k in WEIGHT_ORDER], *[new_v[k] for k in WEIGHT_ORDER])
```

```python
import functools

import jax
import jax.numpy as jnp
from jax import lax
from jax.experimental import pallas as pl
from jax.experimental.pallas import tpu as pltpu

F32 = jnp.float32
BF16 = jnp.bfloat16

D_MODEL = 1024
DEPTH = 4
N_A_LAYERS = 2
N_B_LAYERS = 2
POOL_WINDOWS = (2, 4, 8, 16)
POOL_GROUP = 256
N_HEADS = 8
QK_NOPE = 128
QK_ROPE = 64
V_HEAD = 128
QK_HEAD = QK_NOPE + QK_ROPE
Q_RANK = 384
KV_RANK = 256
ROPE_THETA = 10000.0
D_FF = 2816
EPS = 1e-6
N_MOD = 6
ADAM_LR = 0.001
ADAM_B1 = 0.9
ADAM_B2 = 0.999
ADAM_EPS = 1e-08
ADAM_WD = 0.01
ADAM_STEP = 10

N_DEV = 8
LANES = 128
Q_EXT = 256
VMEM_LIMIT_BYTES = 48 * 1024 * 1024
MESH = pl.DeviceIdType.MESH
NEG_BIG = -0.7 * float(jnp.finfo(jnp.float32).max)


def _params(sem):
    return pltpu.CompilerParams(dimension_semantics=sem, vmem_limit_bytes=VMEM_LIMIT_BYTES)


def _tile(n, cap):
    if n <= cap:
        return n
    best = None
    for d in range(LANES, cap + 1, LANES):
        if n % d == 0:
            best = d
    assert best is not None, (n, cap)
    return best


def _dot(a, b, dims):
    return lax.dot_general(a, b, (dims, ((), ())), preferred_element_type=F32)


NN = ((1,), (0,))
NT = ((1,), (1,))
TN = ((0,), (0,))


def _mm(name, a, b, mode="nn", out_dtype=BF16, add=None, resid=None, gate=None, rowtab=None,
        tm_cap=1024, tn_cap=1408, tk_cap=1408):
    if mode == "tn":
        kdim, m = a.shape
    else:
        m, kdim = a.shape
    n = b.shape[0] if mode == "nt" else b.shape[1]
    tm, tn, tk = _tile(m, tm_cap), _tile(n, tn_cap), _tile(kdim, tk_cap)
    nk = kdim // tk
    dims = {"nn": NN, "nt": NT, "tn": TN}[mode]
    a_spec = pl.BlockSpec((tk, tm), lambda i, j, k: (k, i)) if mode == "tn" else pl.BlockSpec((tm, tk), lambda i, j, k: (i, k))
    b_spec = pl.BlockSpec((tn, tk), lambda i, j, k: (j, k)) if mode == "nt" else pl.BlockSpec((tk, tn), lambda i, j, k: (k, j))
    o_spec = pl.BlockSpec((tm, tn), lambda i, j, k: (i, j))
    g_spec = pl.BlockSpec((1, tn), lambda i, j, k: (0, j))
    gated = resid is not None

    def body(*refs):
        a_ref, b_ref = refs[0], refs[1]
        acc = refs[-1]
        k = pl.program_id(2)

        @pl.when(k == 0)
        def _():
            acc[...] = jnp.zeros_like(acc)

        acc[...] += _dot(a_ref[...].astype(BF16), b_ref[...].astype(BF16), dims)

        @pl.when(k == nk - 1)
        def _():
            if gated:
                r_ref, g_ref, y_ref, x_ref = refs[2:6]
                y_ref[...] = acc[...]
                x_ref[...] = r_ref[...] + g_ref[...] * acc[...]
            elif add is not None:
                refs[3][...] = (acc[...] + refs[2][...].astype(F32)).astype(out_dtype)
            elif rowtab is not None:
                tab = refs[2][...]
                refs[3][...] = (acc[...] * jnp.concatenate([tab] * (tn // tab.shape[1]), axis=1)).astype(out_dtype)
            else:
                refs[2][...] = acc[...].astype(out_dtype)

    ins, in_specs = [a, b], [a_spec, b_spec]
    if rowtab is not None:
        assert tn % rowtab.shape[1] == 0 and not gated and add is None
        ins.append(rowtab)
        in_specs.append(pl.BlockSpec((tm, rowtab.shape[1]), lambda i, j, k: (i, 0)))
    if gated:
        ins += [resid, gate]
        in_specs += [o_spec, g_spec]
        out_shape = (jax.ShapeDtypeStruct((m, n), F32), jax.ShapeDtypeStruct((m, n), F32))
        out_specs = (o_spec, o_spec)
    else:
        if add is not None:
            ins.append(add)
            in_specs.append(o_spec)
        out_shape = jax.ShapeDtypeStruct((m, n), out_dtype)
        out_specs = o_spec
    return pl.pallas_call(
        body, name=name, grid=(m // tm, n // tn, nk), in_specs=in_specs, out_specs=out_specs, out_shape=out_shape,
        scratch_shapes=[pltpu.VMEM((tm, tn), F32)],
        compiler_params=_params(("parallel", "parallel", "arbitrary")),
    )(*ins)


def _rowwise(name, fn, tiled, bcast, outs, sums=(), tr=512):
    tiled = [t if isinstance(t, tuple) else (t, t.shape[1], 0) for t in tiled]
    s = tiled[0][0].shape[0]
    tr = min(tr, s)
    assert s % tr == 0
    n_t, n_b, n_o = len(tiled), len(bcast), len(outs)

    def body(*refs):
        i = pl.program_id(0)
        vals = [r[...] for r in refs[:n_t + n_b]]
        o_vals, s_vals = fn(*vals)
        for r, v in zip(refs[n_t + n_b:n_t + n_b + n_o], o_vals):
            r[...] = v.astype(r.dtype)
        s_refs = refs[n_t + n_b + n_o:]

        @pl.when(i == 0)
        def _():
            for r in s_refs:
                r[...] = jnp.zeros_like(r)

        for r, v in zip(s_refs, s_vals):
            r[...] += v

    in_specs = [pl.BlockSpec((tr, n), functools.partial(lambda cb, i: (i, cb), cb)) for (_, n, cb) in tiled]
    in_specs += [pl.BlockSpec(b.shape, functools.partial(lambda nd, i: (0,) * nd, b.ndim)) for b in bcast]
    out_specs = [pl.BlockSpec((tr, n), lambda i: (i, 0)) for (n, _) in outs]
    out_specs += [pl.BlockSpec((1, n), lambda i: (0, 0)) for n in sums]
    out_shape = [jax.ShapeDtypeStruct((s, n), dt) for (n, dt) in outs]
    out_shape += [jax.ShapeDtypeStruct((1, n), F32) for n in sums]
    res = pl.pallas_call(
        body, name=name, grid=(s // tr,), in_specs=in_specs, out_specs=tuple(out_specs), out_shape=tuple(out_shape),
        compiler_params=_params(("arbitrary",)),
    )(*[t[0] for t in tiled], *bcast)
    return res


def _colsum(v):
    return jnp.sum(v, axis=0, keepdims=True)


def _rms_fwd(name, x, g, scale=None, shift=None, out_dtype=BF16, ncols=None):
    mod = scale is not None

    def fn(xv, gv, *ss):
        y = xv * lax.rsqrt(jnp.mean(xv * xv, axis=-1, keepdims=True) + EPS) * gv
        if mod:
            y = y * (1.0 + ss[0]) + ss[1]
        return (y,), ()

    n = ncols or x.shape[1]
    return _rowwise(name, fn, [(x, n, 0)], [g] + ([scale, shift] if mod else []), [(n, out_dtype)])[0]


def _rms_bwd(name, x, g, dh, scale=None, dx_in=None, ncols=None, out_dtype=F32):
    mod = scale is not None
    has_in = dx_in is not None

    def fn(*vals):
        xv, dhv = vals[0], vals[1].astype(F32)
        rest = list(vals[2:])
        dxi = rest.pop(0) if has_in else None
        gv = rest.pop(0)
        rstd = lax.rsqrt(jnp.mean(xv * xv, axis=-1, keepdims=True) + EPS)
        xhat = xv * rstd
        sums = []
        if mod:
            sc = rest.pop(0)
            dyn = dhv * (1.0 + sc)
            dshift, dscale = _colsum(dhv), _colsum(dhv * (xhat * gv))
        else:
            dyn = dhv
        dg = _colsum(dyn * xhat)
        dxhat = dyn * gv
        dx = rstd * (dxhat - xhat * jnp.mean(dxhat * xhat, axis=-1, keepdims=True))
        if has_in:
            dx = dx + dxi
        sums = [dg] + ([dshift, dscale] if mod else [])
        return (dx,), sums

    n = ncols or x.shape[1]
    tiled = [(x, n, 0), dh] + ([dx_in] if has_in else [])
    return _rowwise(name, fn, tiled, [g] + ([scale] if mod else []), [(n, out_dtype)], [n] * (3 if mod else 1))


def _gate_bwd(name, dxn, y, g):
    def fn(dv, yv, gv):
        return (gv * dv,), (_colsum(dv * yv),)

    n = dxn.shape[1]
    return _rowwise(name, fn, [dxn, y], [g], [(n, BF16)], [n])


def _loss_head(name, x, g, target):
    n = x.shape[1]

    def fn(xv, tv, gv):
        rstd = lax.rsqrt(jnp.mean(xv * xv, axis=-1, keepdims=True) + EPS)
        xhat = xv * rstd
        err = xhat * gv - tv
        loss = 0.5 * jnp.sum(jnp.sum(err * err, axis=-1, keepdims=True) / n, axis=0, keepdims=True)
        dy = err / n
        dg = _colsum(dy * xhat)
        dxhat = dy * gv
        dx = rstd * (dxhat - xhat * jnp.mean(dxhat * xhat, axis=-1, keepdims=True))
        return (dx,), (dg, jnp.broadcast_to(loss, (1, LANES)))

    return _rowwise(name, fn, [x, target], [g], [(n, F32)], [n, LANES])


def _krope_fwd(name, kv_ext, tabk):
    def fn(xv, tv):
        t = xv * tv
        return (t + pltpu.roll(t, 64, 1),), ()

    return _rowwise(name, fn, [(kv_ext, LANES, 2), tabk], [], [(LANES, BF16)])[0]


def _krope_bwd(name, dkd, tabk):
    def fn(dv, tv):
        return ((dv + pltpu.roll(dv, 64, 1)) * tv,), ()

    return _rowwise(name, fn, [dkd, tabk], [], [(LANES, F32)])[0]


def _adamw(name, w, g, m, v):
    def fn(wv, gv, mv, vv):
        m2 = ADAM_B1 * mv + (1.0 - ADAM_B1) * gv
        v2 = ADAM_B2 * vv + (1.0 - ADAM_B2) * (gv * gv)
        m_hat = m2 / (1.0 - ADAM_B1 ** ADAM_STEP)
        v_hat = v2 / (1.0 - ADAM_B2 ** ADAM_STEP)
        delta = -ADAM_LR * (m_hat / (jnp.sqrt(v_hat) + ADAM_EPS) + ADAM_WD * wv)
        return (delta, m2, v2), ()

    r, c = w.shape
    tr = r
    for cand in (512, 256, 128, 64, 32, 16, 8):
        if r % cand == 0 and r > cand:
            tr = cand
            break
    return _rowwise(name, fn, [w, g, m, v], [], [(c, F32)] * 3, tr=tr)


def _sum8(name, parts):
    _, r, c = parts.shape
    tr = r
    for cand in (2048, 1024, 512, 256, 128, 64, 32, 16):
        if r % cand == 0 and r > cand and cand * c <= 256 * 1024:
            tr = cand
            break

    def body(p_ref, o_ref):
        acc = p_ref[0].astype(F32)
        for k in range(1, N_DEV):
            acc = acc + p_ref[k].astype(F32)
        o_ref[...] = acc

    return pl.pallas_call(
        body, name=name, grid=(r // tr,), in_specs=[pl.BlockSpec((N_DEV, tr, c), lambda i: (0, i, 0))],
        out_specs=pl.BlockSpec((tr, c), lambda i: (i, 0)), out_shape=jax.ShapeDtypeStruct((r, c), F32),
        compiler_params=_params(("parallel",)),
    )(parts)


def _mods_fwd(name, c_all, w, b):
    depth, d, n = w.shape

    def body(c_ref, w_ref, b_ref, o_ref):
        cv = c_ref[...]
        sc = (cv * (1.0 / (1.0 + jnp.exp(-cv)))).astype(BF16)
        o_ref[0] = _dot(sc, w_ref[0].astype(BF16), NN) + b_ref[0]

    return pl.pallas_call(
        body, name=name, grid=(depth,),
        in_specs=[pl.BlockSpec(c_all.shape, lambda l: (0, 0)), pl.BlockSpec((1, d, n), lambda l: (l, 0, 0)),
                  pl.BlockSpec((1, 1, n), lambda l: (l, 0, 0))],
        out_specs=pl.BlockSpec((1, c_all.shape[0], n), lambda l: (l, 0, 0)),
        out_shape=jax.ShapeDtypeStruct((depth, c_all.shape[0], n), F32),
        compiler_params=_params(("parallel",)),
    )(c_all, w, b.reshape(depth, 1, n))


def _mods_bwd(name, c_all, dm):
    depth, rows, n = dm.shape
    d = c_all.shape[1]

    def body(c_ref, dm_ref, o_ref):
        cv = c_ref[...]
        sc = (cv * (1.0 / (1.0 + jnp.exp(-cv)))).astype(BF16)
        o_ref[0] = _dot(sc, dm_ref[0].astype(BF16), TN)

    return pl.pallas_call(
        body, name=name, grid=(depth,),
        in_specs=[pl.BlockSpec(c_all.shape, lambda l: (0, 0)), pl.BlockSpec((1, rows, n), lambda l: (l, 0, 0))],
        out_specs=pl.BlockSpec((1, d, n), lambda l: (l, 0, 0)),
        out_shape=jax.ShapeDtypeStruct((depth, d, n), F32),
        compiler_params=_params(("parallel",)),
    )(c_all, dm)


POOL_TILE = 256


def _split_dot(band, val):
    hi = val.astype(BF16)
    lo = (val - hi.astype(F32)).astype(BF16)
    return _dot(band, hi, NN) + _dot(band, lo, NN)


def _pool_fwd(name, h1, x, pw, pb, ps, g1):
    s, d = h1.shape
    t = POOL_TILE

    def body(hc_ref, hp_ref, x_ref, pw_ref, pb_ref, ps_ref, g_ref, xo_ref, zb_ref, pooled_ref):
        i = pl.program_id(0)
        r = lax.broadcasted_iota(jnp.int32, (t, t), 0)
        j = lax.broadcasted_iota(jnp.int32, (t, t), 1)
        pos = (i * t + lax.broadcasted_iota(jnp.int32, (t, 1), 0) + 1).astype(F32)
        has_prev = (i > 0).astype(F32)
        for grp, w in enumerate(POOL_WINDOWS):
            cs = slice(grp * POOL_GROUP, (grp + 1) * POOL_GROUP)
            hc = hc_ref[:, cs]
            band_cur = ((r - j >= 0) & (r - j < w)).astype(BF16)
            band_prev = (r + t - j < w).astype(BF16)
            ssum = _split_dot(band_cur, hc) + has_prev * _split_dot(band_prev, hp_ref[:, cs])
            pooled = (ssum / jnp.minimum(pos, float(w)) - hc).astype(BF16)
            zb = _dot(pooled, pw_ref[grp], NN) + pb_ref[:, cs]
            xo_ref[:, cs] = x_ref[:, cs] + g_ref[:, cs] * (zb * ps_ref[:, cs])
            zb_ref[:, cs] = zb
            pooled_ref[:, cs] = pooled

    row = pl.BlockSpec((t, d), lambda i: (i, 0))
    vec = pl.BlockSpec((1, d), lambda i: (0, 0))
    return pl.pallas_call(
        body, name=name, grid=(s // t,),
        in_specs=[row, pl.BlockSpec((t, d), lambda i: (jnp.maximum(i - 1, 0), 0)), row,
                  pl.BlockSpec(pw.shape, lambda i: (0, 0, 0)), vec, vec, vec],
        out_specs=(row, row, row),
        out_shape=(jax.ShapeDtypeStruct((s, d), F32), jax.ShapeDtypeStruct((s, d), F32), jax.ShapeDtypeStruct((s, d), BF16)),
        compiler_params=_params(("parallel",)),
    )(h1, h1, x, pw, pb, ps, g1)


def _pool_bwd(name, dxn, zb, pooled, pw, ps, g1):
    s, d = dxn.shape
    t = POOL_TILE
    nt = s // t

    def body(dc_ref, dn_ref, zb_ref, pooled_ref, pw_ref, ps_ref, g_ref, dh_ref, dpw_ref, dpb_ref, dps_ref, dg_ref):
        i = pl.program_id(0)

        @pl.when(i == 0)
        def _():
            dpw_ref[...] = jnp.zeros_like(dpw_ref)
            dpb_ref[...] = jnp.zeros_like(dpb_ref)
            dps_ref[...] = jnp.zeros_like(dps_ref)
            dg_ref[...] = jnp.zeros_like(dg_ref)

        jj = lax.broadcasted_iota(jnp.int32, (t, t), 0)
        rr = lax.broadcasted_iota(jnp.int32, (t, t), 1)
        pos = (i * t + lax.broadcasted_iota(jnp.int32, (t, 1), 0) + 1).astype(F32)
        has_next = (i < nt - 1).astype(F32)
        for grp, w in enumerate(POOL_WINDOWS):
            cs = slice(grp * POOL_GROUP, (grp + 1) * POOL_GROUP)
            gv, psv, zbv, dxc = g_ref[:, cs], ps_ref[:, cs], zb_ref[:, cs], dc_ref[:, cs]
            dg_ref[:, cs] += _colsum(dxc * (zbv * psv))
            dy = gv * dxc
            dps_ref[:, cs] += _colsum(dy * zbv)
            dz = dy * psv
            dpb_ref[:, cs] += _colsum(dz)
            dzb = dz.astype(BF16)
            dpw_ref[grp] += _dot(pooled_ref[:, cs], dzb, TN)
            dp = _dot(dzb, pw_ref[grp], NT)
            dzn = (gv * dn_ref[:, cs] * psv).astype(BF16)
            dpn = _dot(dzn, pw_ref[grp], NT) * (has_next / float(w))
            band_cur = ((rr - jj >= 0) & (rr - jj < w)).astype(BF16)
            band_next = (rr + t - jj < w).astype(BF16)
            dh_ref[:, cs] = _split_dot(band_cur, dp / jnp.minimum(pos, float(w))) + _split_dot(band_next, dpn) - dp

    row = pl.BlockSpec((t, d), lambda i: (i, 0))
    vec = pl.BlockSpec((1, d), lambda i: (0, 0))
    wspec = pl.BlockSpec(pw.shape, lambda i: (0, 0, 0))
    return pl.pallas_call(
        body, name=name, grid=(nt,),
        in_specs=[row, pl.BlockSpec((t, d), lambda i: (jnp.minimum(i + 1, nt - 1), 0)), row, row, wspec, vec, vec],
        out_specs=(row, wspec, vec, vec, vec),
        out_shape=(jax.ShapeDtypeStruct((s, d), F32), jax.ShapeDtypeStruct(pw.shape, F32),
                   jax.ShapeDtypeStruct((1, d), F32), jax.ShapeDtypeStruct((1, d), F32), jax.ShapeDtypeStruct((1, d), F32)),
        compiler_params=_params(("arbitrary",)),
    )(dxn, dxn, zb, pooled, pw, ps, g1)


GLU_TILE = 512
HALO = 16
INV_SQRT2 = 0.7071067811865476
INV_SQRT_2PI = 0.3989422804014327


def _up_glu_fwd(name, h2, wa, wv, cw, cb):
    s, d = h2.shape
    f = wa.shape[1]
    tm, tn = _tile(s, 1024), _tile(f, 1408)

    def body(h_ref, hh_ref, wa_ref, wv_ref, cw_ref, cb_ref, ua_ref, gl_ref, gpv_ref, ge_ref):
        i = pl.program_id(1)
        has_prev = (i > 0).astype(F32)
        a = _dot(h_ref[...], wa_ref[...], NN).astype(BF16)
        v = _dot(h_ref[...], wv_ref[...], NN)
        above = (_dot(hh_ref[...], wa_ref[...], NN) * has_prev).astype(BF16)
        ua_ref[...] = a
        ext = jnp.concatenate([above.astype(F32), a.astype(F32)], axis=0)
        e1 = pltpu.roll(ext, 1, 0)[HALO:]
        e2 = pltpu.roll(ext, 2, 0)[HALO:]
        pre = e2 * cw_ref[0:1, :] + e1 * cw_ref[1:2, :] + ext[HALO:] * cw_ref[2:3, :] + cb_ref[...]
        cdf = 0.5 * (1.0 + lax.erf(pre * INV_SQRT2))
        ge = pre * cdf
        gl_ref[...] = (ge * v).astype(gl_ref.dtype)
        gpv_ref[...] = ((cdf + pre * (INV_SQRT_2PI * jnp.exp(-0.5 * pre * pre))) * v).astype(gpv_ref.dtype)
        ge_ref[...] = ge.astype(ge_ref.dtype)

    blk = pl.BlockSpec((tm, tn), lambda j, i: (i, j))
    wspec = pl.BlockSpec((d, tn), lambda j, i: (0, j))
    return pl.pallas_call(
        body, name=name, grid=(f // tn, s // tm),
        in_specs=[pl.BlockSpec((tm, d), lambda j, i: (i, 0)), pl.BlockSpec((HALO, d), lambda j, i: (jnp.maximum(i * (tm // HALO) - 1, 0), 0)),
                  wspec, wspec, pl.BlockSpec((3, tn), lambda j, i: (0, j)), pl.BlockSpec((1, tn), lambda j, i: (0, j))],
        out_specs=(blk, blk, blk, blk), out_shape=tuple(jax.ShapeDtypeStruct((s, f), BF16) for _ in range(4)),
        compiler_params=_params(("parallel", "parallel")),
    )(h2, h2, wa, wv, cw, cb)


def _down_glu_bwd(name, dy2, wd, ua, gpv, ge, cw):
    s, f = ua.shape
    d = dy2.shape[1]
    t, tf = min(GLU_TILE, s), _tile(f, 1408)
    nt = s // t
    te = t + HALO

    def body(dy_ref, dyn_ref, wd_ref, a_ref, ah_ref, g_ref, gn_ref, ge_ref, cw_ref, da_ref, dv_ref, dcw_ref, dcb_ref):
        i = pl.program_id(1)

        @pl.when(i == 0)
        def _():
            dcw_ref[...] = jnp.zeros_like(dcw_ref)
            dcb_ref[...] = jnp.zeros_like(dcb_ref)

        has_prev = (i > 0).astype(F32)
        has_next = (i < nt - 1).astype(F32)
        wdv = wd_ref[...]
        dgl = _dot(dy_ref[...], wdv, NT)
        dgl_below = _dot(dyn_ref[...], wdv, NT) * has_next
        dpre = jnp.concatenate([dgl * g_ref[...].astype(F32), dgl_below * gn_ref[...].astype(F32)], axis=0)
        c0, c1, c2 = cw_ref[0:1, :], cw_ref[1:2, :], cw_ref[2:3, :]
        up1 = pltpu.roll(dpre, te - 1, 0)
        up2 = pltpu.roll(dpre, te - 2, 0)
        da_ref[...] = (dpre * c2 + up1 * c1 + up2 * c0)[:t].astype(da_ref.dtype)
        dv_ref[...] = (dgl * ge_ref[...].astype(F32)).astype(dv_ref.dtype)
        ext = jnp.concatenate([ah_ref[...].astype(F32) * has_prev, a_ref[...].astype(F32)], axis=0)
        dpt = dpre[:t]
        dcb_ref[...] += _colsum(dpt)
        dcw_ref[0:1, :] += _colsum(pltpu.roll(ext, 2, 0)[HALO:] * dpt)
        dcw_ref[1:2, :] += _colsum(pltpu.roll(ext, 1, 0)[HALO:] * dpt)
        dcw_ref[2:3, :] += _colsum(ext[HALO:] * dpt)

    blk = pl.BlockSpec((t, tf), lambda j, i: (i, j))
    prev = pl.BlockSpec((HALO, tf), lambda j, i: (jnp.maximum(i * (t // HALO) - 1, 0), j))
    below = lambda i: jnp.minimum((i + 1) * (t // HALO), s // HALO - 1)
    w3 = pl.BlockSpec((3, tf), lambda j, i: (0, j))
    w1 = pl.BlockSpec((1, tf), lambda j, i: (0, j))
    return pl.pallas_call(
        body, name=name, grid=(f // tf, nt),
        in_specs=[pl.BlockSpec((t, d), lambda j, i: (i, 0)), pl.BlockSpec((HALO, d), lambda j, i: (below(i), 0)),
                  pl.BlockSpec((tf, d), lambda j, i: (j, 0)), blk, prev, blk, pl.BlockSpec((HALO, tf), lambda j, i: (below(i), j)), blk, w3],
        out_specs=(blk, blk, w3, w1),
        out_shape=(jax.ShapeDtypeStruct((s, f), BF16), jax.ShapeDtypeStruct((s, f), BF16),
                   jax.ShapeDtypeStruct((3, f), F32), jax.ShapeDtypeStruct((1, f), F32)),
        compiler_params=_params(("parallel", "arbitrary")),
    )(dy2, dy2, wd, ua, ua, gpv, gpv, ge, cw)


ATT_TILE = 512
ATT_ROWS = 256
LOG2E = 1.4426950408889634
LN2 = 0.6931471805599453


def _head_blocks_t(a, width):
    s = a.shape[0]
    t = min(ATT_TILE, s)
    return a.reshape(s // t, t, N_HEADS, width).transpose(2, 0, 3, 1)


def _causal_mask(sv, q0, k0):
    row = q0 + lax.broadcasted_iota(jnp.int32, sv.shape, 0)
    col = k0 + lax.broadcasted_iota(jnp.int32, sv.shape, 1)
    return jnp.where(col <= row, sv, NEG_BIG)


def _attn_fwd(name, q_rot, kt4, v_ext):
    s = q_rot.shape[0]
    t = min(ATT_TILE, s)
    nq = s // t

    rq = min(ATT_ROWS, t)

    def body(q_ref, kt_ref, v_ref, o_ref, lse_ref, row_ref, acc_ref, m_ref):
        qi = pl.program_id(1)
        acc_ref[...] = jnp.zeros_like(acc_ref)
        m_ref[...] = jnp.full_like(m_ref, NEG_BIG)

        def step(j, masked):
            v_blk = v_ref[pl.ds(pl.multiple_of(j * t, t), t), :]
            for r in range(t // rq):
                rs = pl.ds(r * rq, rq)
                sv = _dot(q_ref[rs, :], kt_ref[0, j], NN)
                if masked:
                    sv = _causal_mask(sv, r * rq, 0)
                m_prev = m_ref[rs, :]
                m_new = jnp.maximum(m_prev, jnp.max(sv, axis=-1, keepdims=True))
                p = jnp.exp2(sv - m_new).astype(BF16)
                acc_ref[rs, :] = jnp.exp2(m_prev - m_new) * acc_ref[rs, :] + _dot(p, v_blk, NN)
                m_ref[rs, :] = m_new

        def full_step(j, carry):
            step(j, False)
            return carry

        lax.fori_loop(0, qi, full_step, 0)
        step(qi, True)
        l = acc_ref[:, V_HEAD:V_HEAD + 1]
        o_ref[...] = (acc_ref[:, :V_HEAD] / l).astype(o_ref.dtype)
        lse = jnp.broadcast_to(m_ref[...] + jnp.log(l) * LOG2E, lse_ref.shape)
        lse_ref[...] = lse
        row_ref[0, 0] = jnp.transpose(lse)[0:8, :]

    head_q = pl.BlockSpec((t, Q_EXT), lambda h, i: (i, h))
    head_o = pl.BlockSpec((t, V_HEAD), lambda h, i: (i, h))
    return pl.pallas_call(
        body, name=name, grid=(N_HEADS, nq),
        in_specs=[head_q, pl.BlockSpec((1, nq, Q_EXT, t), lambda h, i: (h, 0, 0, 0)), pl.BlockSpec((s, Q_EXT), lambda h, i: (0, h))],
        out_specs=(head_o, head_o, pl.BlockSpec((1, 1, 8, t), lambda h, i: (h, i, 0, 0))),
        out_shape=(jax.ShapeDtypeStruct((s, N_HEADS * V_HEAD), BF16), jax.ShapeDtypeStruct((s, N_HEADS * LANES), F32),
                   jax.ShapeDtypeStruct((N_HEADS, nq, 8, t), F32)),
        scratch_shapes=[pltpu.VMEM((t, Q_EXT), F32), pltpu.VMEM((t, 1), F32)],
        compiler_params=_params(("parallel", "parallel")),
    )(q_rot, kt4, v_ext)


def _attn_dq(name, q_rot, tabq, kt4, kfull, vt4, o, lse, do):
    s = q_rot.shape[0]
    t = min(ATT_TILE, s)
    nq = s // t

    def body(q_ref, tab_ref, kt_ref, k_ref, vt_ref, o_ref, lse_ref, do_ref, dq_ref, delta_ref, acc_ref):
        qi = pl.program_id(1)
        q = q_ref[...]
        dov = do_ref[...]
        delta = jnp.sum(dov.astype(F32) * o_ref[...].astype(F32), axis=-1, keepdims=True)
        lse = lse_ref[:, 0:1]
        acc_ref[...] = jnp.zeros_like(acc_ref)

        def step(j, masked):
            sv = _dot(q, kt_ref[0, j], NN)
            if masked:
                sv = _causal_mask(sv, qi * t, j * t)
            p = jnp.exp2(sv - lse)
            dp = _dot(dov, vt_ref[0, j], NN)
            ds = (p * (dp - delta)).astype(BF16)
            acc_ref[...] += _dot(ds, k_ref[pl.ds(pl.multiple_of(j * t, t), t), :], NN)

        def full_step(j, carry):
            step(j, False)
            return carry

        lax.fori_loop(0, qi, full_step, 0)
        step(qi, True)
        dq_ref[...] = (acc_ref[...] * (tab_ref[...] * LN2)).astype(dq_ref.dtype)
        delta_ref[0, 0] = jnp.transpose(jnp.broadcast_to(delta, (t, LANES)))[0:8, :]

    head_q = pl.BlockSpec((t, Q_EXT), lambda h, i: (i, h))
    head_o = pl.BlockSpec((t, V_HEAD), lambda h, i: (i, h))
    return pl.pallas_call(
        body, name=name, grid=(N_HEADS, nq),
        in_specs=[head_q, pl.BlockSpec((t, Q_EXT), lambda h, i: (i, 0)), pl.BlockSpec((1, nq, Q_EXT, t), lambda h, i: (h, 0, 0, 0)),
                  pl.BlockSpec((s, Q_EXT), lambda h, i: (0, h)), pl.BlockSpec((1, nq, V_HEAD, t), lambda h, i: (h, 0, 0, 0)),
                  head_o, head_o, head_o],
        out_specs=(head_q, pl.BlockSpec((1, 1, 8, t), lambda h, i: (h, i, 0, 0))),
        out_shape=(jax.ShapeDtypeStruct((s, N_HEADS * Q_EXT), BF16), jax.ShapeDtypeStruct((N_HEADS, nq, 8, t), F32)),
        scratch_shapes=[pltpu.VMEM((t, Q_EXT), F32)],
        compiler_params=_params(("parallel", "parallel")),
    )(q_rot, tabq, kt4, kfull, vt4, o, lse, do)


def _attn_dkv(name, kfull, v, qt4, q_rot, dot4, do, lse_row, delta_row, acc_in=None):
    s = kfull.shape[0]
    t = min(ATT_TILE, s)
    nq = s // t
    has_in = acc_in is not None

    def body(*refs):
        k_ref, v_ref, qt_ref, q_ref, dot_ref, do_ref, lse_ref, delta_ref = refs[:8]
        dkn_ref, dkd_ref, dv_ref, acck_ref, accv_ref = refs[-5:]
        kj, h = pl.program_id(0), pl.program_id(1)
        k_blk, v_blk = k_ref[...], v_ref[...]
        acck_ref[...] = jnp.zeros_like(acck_ref)
        accv_ref[...] = jnp.zeros_like(accv_ref)

        def step(i, masked):
            qs = pl.ds(pl.multiple_of(i * t, t), t)
            st = _dot(k_blk, qt_ref[0, i], NN)
            if masked:
                krow = lax.broadcasted_iota(jnp.int32, st.shape, 0)
                qcol = lax.broadcasted_iota(jnp.int32, st.shape, 1)
                st = jnp.where(krow <= qcol, st, NEG_BIG)
            pt = jnp.exp2(st - lse_ref[0, i, 0:1, :])
            accv_ref[...] += _dot(pt.astype(BF16), do_ref[qs, :], NN)
            dpt = _dot(v_blk, dot_ref[0, i], NN)
            dst = (pt * (dpt - delta_ref[0, i, 0:1, :])).astype(BF16)
            acck_ref[...] += _dot(dst, q_ref[qs, :], NN)

        def full_step(i, carry):
            step(i, False)
            return carry

        step(kj, True)
        lax.fori_loop(kj + 1, nq, full_step, 0)
        dk = acck_ref[...] * LN2
        dkn, dkd = dk[:, :QK_NOPE], dk[:, QK_NOPE:]
        if has_in:
            dkn = dkn + refs[8][...]
            dv_ref[...] = accv_ref[...] + refs[10][...]
        else:
            dv_ref[...] = accv_ref[...]
        dkn_ref[...] = dkn

        @pl.when(h == 0)
        def _():
            if has_in:
                dkd_ref[...] = dkd + refs[9][...]
            else:
                dkd_ref[...] = dkd

        @pl.when(h > 0)
        def _():
            dkd_ref[...] += dkd

    kblk = pl.BlockSpec((t, LANES), lambda j, h: (j, h))
    kdblk = pl.BlockSpec((t, LANES), lambda j, h: (j, 0))
    col = pl.BlockSpec((s, LANES), lambda j, h: (0, h))
    stat = pl.BlockSpec((1, nq, 8, t), lambda j, h: (h, 0, 0, 0))
    ins = [kfull, v, qt4, q_rot, dot4, do, lse_row, delta_row]
    in_specs = [pl.BlockSpec((t, Q_EXT), lambda j, h: (j, h)), kblk, pl.BlockSpec((1, nq, Q_EXT, t), lambda j, h: (h, 0, 0, 0)),
                pl.BlockSpec((s, Q_EXT), lambda j, h: (0, h)), pl.BlockSpec((1, nq, V_HEAD, t), lambda j, h: (h, 0, 0, 0)), col, stat, stat]
    if has_in:
        ins += list(acc_in)
        in_specs += [kblk, kdblk, kblk]
    return pl.pallas_call(
        body, name=name, grid=(nq, N_HEADS), in_specs=in_specs, out_specs=(kblk, kdblk, kblk),
        out_shape=(jax.ShapeDtypeStruct((s, N_HEADS * LANES), F32), jax.ShapeDtypeStruct((s, LANES), F32),
                   jax.ShapeDtypeStruct((s, N_HEADS * LANES), F32)),
        scratch_shapes=[pltpu.VMEM((t, Q_EXT), F32), pltpu.VMEM((t, LANES), F32)],
        compiler_params=_params(("parallel", "arbitrary")),
    )(*ins)


def _swap_halves(w):
    half = w.shape[-1] // 2
    return jnp.concatenate([-w[..., half:], w[..., :half]], axis=-1)


def _unswap_halves(g):
    half = g.shape[-1] // 2
    return jnp.concatenate([g[..., half:], -g[..., :half]], axis=-1)


def _extend_w_uq(w):
    r = w.reshape(Q_RANK, N_HEADS, QK_HEAD)
    rope = r[..., QK_NOPE:]
    return jnp.concatenate([r[..., :QK_NOPE], rope, _swap_halves(rope)], axis=-1).reshape(Q_RANK, N_HEADS * Q_EXT)


def _fold_w_uq_grad(g):
    r = g.reshape(Q_RANK, N_HEADS, Q_EXT)
    rope = r[..., QK_NOPE:QK_HEAD] + _unswap_halves(r[..., QK_HEAD:])
    return jnp.concatenate([r[..., :QK_NOPE], rope], axis=-1).reshape(Q_RANK, N_HEADS * QK_HEAD)


def _extend_w_dkv(w):
    return jnp.concatenate([w, _swap_halves(w[:, KV_RANK:])], axis=-1)


def _fold_w_dkv_grad(g):
    rope = g[:, KV_RANK:KV_RANK + QK_ROPE] + _unswap_halves(g[:, KV_RANK + QK_ROPE:])
    return jnp.concatenate([g[:, :KV_RANK], rope], axis=-1)


def _rope_tables(positions):
    inv = 1.0 / (ROPE_THETA ** (jnp.arange(0, QK_ROPE, 2, dtype=F32) / QK_ROPE))
    ang = positions.astype(F32)[:, None] * inv
    cos, sin = jnp.cos(ang), jnp.sin(ang)
    tabk = jnp.concatenate([cos, cos, sin, sin], axis=-1)
    scale = QK_HEAD ** -0.5 * LOG2E
    tabq = jnp.concatenate([jnp.full((positions.shape[0], QK_NOPE), scale, F32), tabk * scale], axis=-1)
    return tabq, tabk


def _forward_backward(x, target, mods, tabq, tabk, final_g, fetch, push):
    row = lambda vec: vec.reshape(1, -1)
    mod = [[row(mods[l, k * D_MODEL:(k + 1) * D_MODEL]) for k in range(N_MOD)] for l in range(DEPTH)]
    saved, weights = [], []
    kv = None
    for l in range(DEPTH):
        w, tok = fetch(l, x)
        sh1, sc1, g1, sh2, sc2, g2 = mod[l]
        sh1 = sh1 + tok
        if l == N_A_LAYERS:
            kvn = _rms_fwd("kvin_fwd", x, row(w["kv_in_g"]))
            kv_ext = _mm("dkv_fwd", kvn, w["w_dkv_ext"], out_dtype=F32)
            ckv = _rms_fwd("ckv_fwd", kv_ext, row(w["ckv_norm_g"]), ncols=KV_RANK)
            kd = _krope_fwd("krope_fwd", kv_ext, tabk)
            kn, v = _mm("uk_fwd", ckv, w["w_uk"]), _mm("uv_fwd", ckv, w["w_uv"])
            heads = lambda a: [a[:, h * LANES:(h + 1) * LANES] for h in range(N_HEADS)]
            kfull = jnp.concatenate([part for kh in heads(kn) for part in (kh, kd)], axis=-1)
            v_ext = jnp.concatenate([part for vh in heads(v) for part in (vh, jnp.ones_like(vh))], axis=-1)
            kv = dict(x=x, kvn=kvn, kv_ext=kv_ext, ckv=ckv, v=v, kfull=kfull, v_ext=v_ext,
                      kt4=_head_blocks_t(kfull, Q_EXT), vt4=_head_blocks_t(v, V_HEAD))
        x_in = x
        if l < N_A_LAYERS:
            h1 = _rms_fwd(f"norm1_fwd_{l}", x, row(w["norm1_g"]), sc1, sh1, out_dtype=F32)
            x_mid, zb, pooled = _pool_fwd(f"pool_fwd_{l}", h1, x, w["pool_w"], row(w["pool_b"]), row(w["pool_scale"]), g1)
            mix = (zb, pooled)
        else:
            h1 = _rms_fwd(f"norm1_fwd_{l}", x, row(w["norm1_g"]), sc1, sh1)
            cq_pre = _mm(f"dq_fwd_{l}", h1, w["w_dq"], out_dtype=F32)
            cq = _rms_fwd(f"qnorm_fwd_{l}", cq_pre, row(w["q_norm_g"]))
            q_rot = _mm(f"uq_fwd_{l}", cq, w["w_uq_ext"], rowtab=tabq)
            o, lse, lse_row = _attn_fwd(f"attn_fwd_{l}", q_rot, kv["kt4"], kv["v_ext"])
            y, x_mid = _mm(f"wo_fwd_{l}", o, w["w_o"], resid=x, gate=g1)
            mix = (h1, cq_pre, cq, q_rot, o, lse, lse_row, y)
        h2 = _rms_fwd(f"norm2_fwd_{l}", x_mid, row(w["norm2_g"]), sc2, sh2)
        w_up_a, w_up_v = w["w_up"](h2)
        ua, gl, gpv, ge = _up_glu_fwd(f"up_glu_fwd_{l}", h2, w_up_a, w_up_v, w["conv_w"], row(w["conv_b"]))
        w_down = w["w_down"](gl)
        y2, x = _mm(f"down_fwd_{l}", gl, w_down, resid=x_mid, gate=g2)
        saved.append((x_in, x_mid, h2, ua, gpv, ge, gl, y2, mix))
        weights.append(dict(w, w_up_a=w_up_a, w_up_v=w_up_v, w_down=w_down))

    dx, dfinal_g, loss = _loss_head("loss_head", x, row(final_g), target)
    g = {"final_g": dfinal_g.reshape(-1)}
    per_layer = {k: [None] * DEPTH for k in ("norm1_g", "norm2_g", "conv_w", "conv_b")}
    per_a = {k: [None] * N_A_LAYERS for k in ("pool_b", "pool_scale")}
    per_b = {k: [None] * N_B_LAYERS for k in ("q_norm_g",)}
    dmods = [None] * DEPTH
    dkv = None
    tok = 0.0
    for l in reversed(range(DEPTH)):
        w, big = weights[l], {}
        sh1, sc1, g1, sh2, sc2, g2 = mod[l]
        g2 = g2 + tok
        x_in, x_mid, h2, ua, gpv, ge, gl, y2, mix = saved[l]
        dy2, dg2 = _gate_bwd(f"gate2_bwd_{l}", dx, y2, g2)
        tok = push(l, "down", dict(w_down=_mm(f"down_wgrad_{l}", gl, dy2, mode="tn", tm_cap=1408)), None)
        da, dv_, dcw, dcb = _down_glu_bwd(f"down_glu_bwd_{l}", dy2, w["w_down"], ua, gpv, ge, w["conv_w"] + tok)
        dh2 = _mm(f"up_a_bwd_{l}", da, w["w_up_a"], mode="nt", out_dtype=F32)
        dh2 = _mm(f"up_v_bwd_{l}", dv_, w["w_up_v"], mode="nt", out_dtype=F32, add=dh2)
        tok = push(l, "up", dict(w_up_a=_mm(f"up_a_wgrad_{l}", h2, da, mode="tn"), w_up_v=_mm(f"up_v_wgrad_{l}", h2, dv_, mode="tn")), None)
        per_layer["conv_w"][l], per_layer["conv_b"][l] = dcw, dcb.reshape(-1)
        dx_mid, dn2, dsh2, dsc2 = _rms_bwd(f"norm2_bwd_{l}", x_mid, row(w["norm2_g"]), dh2, sc2 + tok, dx_in=dx)
        per_layer["norm2_g"][l] = dn2.reshape(-1)
        if l < N_A_LAYERS:
            zb, pooled = mix
            dh1, dpw, dpb, dps, dg1 = _pool_bwd(f"pool_bwd_{l}", dx_mid, zb, pooled, w["pool_w"], row(w["pool_scale"]), g1)
            big["pool_w"] = dpw
            per_a["pool_b"][l], per_a["pool_scale"][l] = dpb.reshape(-1), dps.reshape(-1)
        else:
            j = l - N_A_LAYERS
            h1, cq_pre, cq, q_rot, o, lse, lse_row, y = mix
            dy, dg1 = _gate_bwd(f"gate1_bwd_{l}", dx_mid, y, g1)
            do = _mm(f"wo_bwd_{l}", dy, w["w_o"], mode="nt")
            big["w_o"] = _mm(f"wo_wgrad_{l}", o, dy, mode="tn")
            dq_ext, delta_row = _attn_dq(f"attn_dq_{l}", q_rot, tabq, kv["kt4"], kv["kfull"], kv["vt4"], o, lse, do)
            dkv = _attn_dkv(f"attn_dkv_{l}", kv["kfull"], kv["v"], _head_blocks_t(q_rot, Q_EXT), q_rot, _head_blocks_t(do, V_HEAD), do,
                            lse_row, delta_row, acc_in=dkv)
            dcq = _mm(f"uq_bwd_{l}", dq_ext, w["w_uq_ext"], mode="nt", out_dtype=F32)
            big["w_uq_ext"] = _mm(f"uq_wgrad_{l}", cq, dq_ext, mode="tn", out_dtype=F32)
            dcq_pre, dqn = _rms_bwd(f"qnorm_bwd_{l}", cq_pre, row(w["q_norm_g"]), dcq, out_dtype=BF16)
            per_b["q_norm_g"][j] = dqn.reshape(-1)
            dh1 = _mm(f"dq_bwd_{l}", dcq_pre, w["w_dq"], mode="nt")
            big["w_dq"] = _mm(f"dq_wgrad_{l}", h1, dcq_pre, mode="tn")
        dx, dn1, dsh1, dsc1 = _rms_bwd(f"norm1_bwd_{l}", x_in, row(w["norm1_g"]), dh1, sc1, dx_in=dx_mid)
        per_layer["norm1_g"][l] = dn1.reshape(-1)
        dmods[l] = jnp.concatenate([dsh1, dsc1, dg1, dsh2, dsc2, dg2], axis=-1).reshape(-1)
        if l == N_A_LAYERS:
            dkn, dkd, dv = dkv
            dckv = _mm("uk_bwd", dkn, w["w_uk"], mode="nt", out_dtype=F32)
            dckv = _mm("uv_bwd", dv, w["w_uv"], mode="nt", out_dtype=F32, add=dckv)
            big["w_uk"] = _mm("uk_wgrad", kv["ckv"], dkn, mode="tn")
            big["w_uv"] = _mm("uv_wgrad", kv["ckv"], dv, mode="tn")
            dkr = _krope_bwd("krope_bwd", dkd, tabk)
            dc, dckv_g = _rms_bwd("ckv_bwd", kv["kv_ext"], row(w["ckv_norm_g"]), dckv, ncols=KV_RANK, out_dtype=BF16)
            dkv_ext = jnp.concatenate([dc, dkr.astype(BF16)], axis=-1)
            dkvn = _mm("dkv_bwd", dkv_ext, w["w_dkv_ext"], mode="nt")
            big["w_dkv_ext"] = _mm("dkv_wgrad", kv["kvn"], dkv_ext, mode="tn", out_dtype=F32)
            dx, dkv_in_g = _rms_bwd("kvin_bwd", kv["x"], row(w["kv_in_g"]), dkvn, dx_in=dx)
            g["ckv_norm_g"], g["kv_in_g"] = dckv_g.reshape(-1), dkv_in_g.reshape(-1)
        tok = push(l, "mix", big, dx)
    for group in (per_layer, per_a, per_b):
        for k, vals in group.items():
            g[k] = jnp.stack(vals)
    return loss, dx, g, jnp.stack(dmods)


def _my_index():
    return 4 * lax.axis_index("x") + 2 * lax.axis_index("y") + lax.axis_index("c")


def _peer(k):
    x, y, c = lax.axis_index("x"), lax.axis_index("y"), lax.axis_index("c")
    return (1 - x if k & 4 else x, 1 - y if k & 2 else y, 1 - c if k & 1 else c)


def _index_of(pos):
    return 4 * pos[0] + 2 * pos[1] + pos[2]


def _exchange_many(name, arrays, scatter):
    n = len(arrays)
    blocks = [tuple(a.shape[1:]) if scatter else tuple(a.shape) for a in arrays]

    def body(*refs):
        x_refs, o_refs = refs[:n], refs[n:2 * n]
        send_sems, recv_sems, local_sems = refs[2 * n:]
        me = _my_index()
        started = []
        for a in range(n):
            mine = pltpu.make_async_copy(x_refs[a].at[me] if scatter else x_refs[a], o_refs[a].at[me], local_sems.at[a])
            mine.start()
            started.append(mine)
        sends = []
        for k in range(1, N_DEV):
            peer = _peer(k)
            for a in range(n):
                cp = pltpu.make_async_remote_copy(
                    src_ref=x_refs[a].at[_index_of(peer)] if scatter else x_refs[a], dst_ref=o_refs[a].at[me],
                    send_sem=send_sems.at[a, k - 1], recv_sem=recv_sems.at[a, k - 1], device_id=peer, device_id_type=MESH)
                cp.start()
                sends.append(cp)
        for k in range(1, N_DEV):
            peer = _peer(k)
            for a in range(n):
                pltpu.make_async_remote_copy(
                    src_ref=x_refs[a].at[me] if scatter else x_refs[a], dst_ref=o_refs[a].at[_index_of(peer)],
                    send_sem=send_sems.at[a, k - 1], recv_sem=recv_sems.at[a, k - 1], device_id=peer, device_id_type=MESH).wait_recv()
        for cp in sends:
            cp.wait_send()
        for mine in started:
            mine.wait()

    return pl.pallas_call(
        body, name=name, out_shape=tuple(jax.ShapeDtypeStruct((N_DEV,) + blk, a.dtype) for blk, a in zip(blocks, arrays)),
        in_specs=[pl.BlockSpec(memory_space=pl.ANY)] * n, out_specs=tuple([pl.BlockSpec(memory_space=pl.ANY)] * n),
        scratch_shapes=[pltpu.SemaphoreType.DMA((n, N_DEV - 1)), pltpu.SemaphoreType.DMA((n, N_DEV - 1)), pltpu.SemaphoreType.DMA((n,))],
    )(*arrays)


def _exchange(name, x, scatter):
    return _exchange_many(name, [x], scatter)[0]


HBM_SPEC = pl.BlockSpec(memory_space=pltpu.HBM)
SEM_SPEC = pl.BlockSpec(memory_space=pltpu.SEMAPHORE)
DATAFLOW = pltpu.SideEffectType.DATAFLOW_SIDE_EFFECTING


def _remote_copies(x_refs, land_refs, send_sems, recv_sems, scatter, numbers=None):
    me = _my_index()
    numbers = list(range(len(x_refs))) if numbers is None else numbers
    out, inc = [], []
    for a in range(len(x_refs)):
        for k in range(1, N_DEV):
            peer = _peer(k)
            pair = numbers[a] * (N_DEV - 1) + k - 1
            sems = dict(send_sem=send_sems.at[pair], recv_sem=recv_sems.at[pair], device_id=peer, device_id_type=MESH)
            out.append(pltpu.make_async_remote_copy(
                src_ref=x_refs[a].at[_index_of(peer)] if scatter else x_refs[a], dst_ref=land_refs[a].at[me], **sems))
            inc.append(pltpu.make_async_remote_copy(
                src_ref=x_refs[a].at[me] if scatter else x_refs[a], dst_ref=land_refs[a].at[_index_of(peer)], **sems))
    return out, inc


def _exchange_start(name, arrays, scatter):
    n = len(arrays)
    blocks = [tuple(a.shape[1:]) if scatter else tuple(a.shape) for a in arrays]

    def body(*refs):
        x_refs, land_refs = refs[:n], refs[n:2 * n]
        send_sems, recv_sems = refs[2 * n], refs[2 * n + 1]
        for cp in _remote_copies(x_refs, land_refs, send_sems, recv_sems, scatter)[0]:
            cp.start()
        refs[-1][...] = jnp.zeros_like(refs[-1])

    sem_type = pltpu.SemaphoreType.DMA((n * (N_DEV - 1),))
    lands =[pltpu.with_memory_space_constraint(lax.empty((N_DEV,) + blk, a.dtype), pltpu.HBM) for blk, a in zip(blocks, arrays)]
    srcs = [pltpu.with_memory_space_constraint(a, pltpu.HBM) for a in arrays]
    res = pl.pallas_call(
        body, name=name,
        out_shape=(sem_type, sem_type, *[pltpu.HBM(a.shape, a.dtype) for a in srcs + lands], jax.ShapeDtypeStruct((8, LANES), F32)),
        in_specs=[HBM_SPEC] * (2 * n), out_specs=(SEM_SPEC, SEM_SPEC, *[HBM_SPEC] * (2 * n), pl.BlockSpec(memory_space=pltpu.VMEM)),
        input_output_aliases={i: 2 + i for i in range(2 * n)},
        compiler_params=pltpu.CompilerParams(has_side_effects=DATAFLOW),
    )(*srcs, *lands)
    return (res[0], res[1], list(res[2:2 + n]), list(res[2 + n:2 + 2 * n])), res[-1]


def _exchange_wait(name, handles, after, scatter, which=None):
    send_sems, recv_sems, srcs, lands = handles
    which = list(range(len(srcs))) if which is None else list(which)
    srcs, lands = [srcs[a] for a in which], [lands[a] for a in which]
    n = len(srcs)

    def body(*refs):
        x_refs, land_refs = refs[:n], refs[n:2 * n]
        out, inc = _remote_copies(x_refs, land_refs, refs[2 * n], refs[2 * n + 1], scatter, which)
        for cp in out:
            cp.wait_send()
        for cp in inc:
            cp.wait_recv()

    res = pl.pallas_call(
        body, name=name, out_shape=tuple(pltpu.HBM(a.shape, a.dtype) for a in srcs + lands),
        in_specs=[HBM_SPEC] * (2 * n) + [SEM_SPEC, SEM_SPEC, pl.BlockSpec(memory_space=pl.ANY)], out_specs=tuple([HBM_SPEC] * (2 * n)),
        input_output_aliases={i: i for i in range(2 * n)},
        compiler_params=pltpu.CompilerParams(has_side_effects=DATAFLOW),
    )(*srcs, *lands, send_sems, recv_sems, after)
    return list(res[n:])


def _pack(arrays, dtype, row_multiple):
    flat = jnp.concatenate([a.astype(dtype).reshape(-1) for a in arrays])
    rows = -(-flat.shape[0] // (LANES * row_multiple)) * row_multiple
    return jnp.pad(flat, (0, rows * LANES - flat.shape[0])).reshape(rows, LANES)


def _unpack(packed, shapes):
    lead = packed.shape[:-2]
    flat = packed.reshape(lead + (-1,))
    out, off = [], 0
    for shp in shapes:
        size = 1
        for d in shp:
            size *= d
        out.append(flat[..., off:off + size].reshape(lead + tuple(shp)))
        off += size
    return out


def _unshard(g8, axis):
    return jnp.concatenate([g8[j] for j in range(N_DEV)], axis=axis)


def _shard8(full, axis):
    n = full.shape[axis] // N_DEV
    return jnp.stack([lax.slice_in_dim(full, j * n, (j + 1) * n, axis=axis) for j in range(N_DEV)])


VECTOR_WEIGHTS = (("pool_b", 1), ("pool_scale", 1), ("conv_w", 2))
REPLICATED_WEIGHTS = ("norm1_g", "norm2_g", "kv_in_g", "ckv_norm_g", "q_norm_g", "conv_b", "final_g")
WEIGHT_ORDER = ("mod_w", "mod_b", "norm1_g", "norm2_g", "pool_w", "pool_b", "pool_scale", "kv_in_g", "w_dkv", "ckv_norm_g", "w_uk",
                "w_uv", "w_dq", "q_norm_g", "w_uq", "w_o", "w_up", "conv_w", "conv_b", "w_down", "final_g")
BIG_ROW_MULTIPLE = 1024
SMALL_ROW_MULTIPLE = 16


def _as_2d(a):
    if a.ndim == 1:
        return a.reshape(-1, LANES)
    return a.reshape(-1, a.shape[-1])


def kernel(x, c, positions, mod_w, mod_b, norm1_g, norm2_g, pool_w, pool_b, pool_scale, kv_in_g, w_dkv, ckv_norm_g, w_uk, w_uv, w_dq, q_norm_g, w_uq, w_o, w_up, conv_w, conv_b, w_down, final_g, loss_target, m_mod_w, m_mod_b, m_norm1_g, m_norm2_g, m_pool_w, m_pool_b, m_pool_scale, m_kv_in_g, m_w_dkv, m_ckv_norm_g, m_w_uk, m_w_uv, m_w_dq, m_q_norm_g, m_w_uq, m_w_o, m_w_up, m_conv_w, m_conv_b, m_w_down, m_final_g, v_mod_w, v_mod_b, v_norm1_g, v_norm2_g, v_pool_w, v_pool_b, v_pool_scale, v_kv_in_g, v_w_dkv, v_ckv_norm_g, v_w_uk, v_w_uv, v_w_dq, v_q_norm_g, v_w_uq, v_w_o, v_w_up, v_conv_w, v_conv_b, v_w_down, v_final_g):
    shard = dict(mod_w=mod_w, mod_b=mod_b, norm1_g=norm1_g, norm2_g=norm2_g, pool_w=pool_w, pool_b=pool_b, pool_scale=pool_scale,
                 kv_in_g=kv_in_g, w_dkv=w_dkv, ckv_norm_g=ckv_norm_g, w_uk=w_uk, w_uv=w_uv, w_dq=w_dq, q_norm_g=q_norm_g, w_uq=w_uq,
                 w_o=w_o, w_up=w_up, conv_w=conv_w, conv_b=conv_b, w_down=w_down, final_g=final_g)
    mom_m = dict(mod_w=m_mod_w, mod_b=m_mod_b, norm1_g=m_norm1_g, norm2_g=m_norm2_g, pool_w=m_pool_w, pool_b=m_pool_b,
                 pool_scale=m_pool_scale, kv_in_g=m_kv_in_g, w_dkv=m_w_dkv, ckv_norm_g=m_ckv_norm_g, w_uk=m_w_uk, w_uv=m_w_uv,
                 w_dq=m_w_dq, q_norm_g=m_q_norm_g, w_uq=m_w_uq, w_o=m_w_o, w_up=m_w_up, conv_w=m_conv_w, conv_b=m_conv_b,
                 w_down=m_w_down, final_g=m_final_g)
    mom_v = dict(mod_w=v_mod_w, mod_b=v_mod_b, norm1_g=v_norm1_g, norm2_g=v_norm2_g, pool_w=v_pool_w, pool_b=v_pool_b,
                 pool_scale=v_pool_scale, kv_in_g=v_kv_in_g, w_dkv=v_w_dkv, ckv_norm_g=v_ckv_norm_g, w_uk=v_w_uk, w_uv=v_w_uv,
                 w_dq=v_w_dq, q_norm_g=v_q_norm_g, w_uq=v_w_uq, w_o=v_w_o, w_up=v_w_up, conv_w=v_conv_w, conv_b=v_conv_b,
                 w_down=v_w_down, final_g=v_final_g)
    me = _my_index()
    d6 = N_MOD * D_MODEL
    mod_cols = d6 // N_DEV

    small_in = [c] + [shard[k] for k, _ in VECTOR_WEIGHTS]
    small_all = _exchange("gather_vectors", _pack(small_in, F32, SMALL_ROW_MULTIPLE), scatter=False)
    parts = _unpack(small_all, [a.shape for a in small_in])
    c_all = jnp.pad(parts[0].reshape(N_DEV, D_MODEL), ((0, N_DEV), (0, 0)))
    vec = {k: _unshard(p, ax) for (k, ax), p in zip(VECTOR_WEIGHTS, parts[1:])}

    my_mod_b = lax.dynamic_slice_in_dim(mod_b, me * mod_cols, mod_cols, axis=1)
    mods_mine = _mods_fwd("mods_fwd", c_all, mod_w, my_mod_b)
    mods_all = _exchange("gather_mods", _pack([mods_mine], F32, SMALL_ROW_MULTIPLE), scatter=False)
    mods_all = _unpack(mods_all, [mods_mine.shape])[0]
    mods = lax.dynamic_index_in_dim(mods_all, me, axis=2, keepdims=False)
    mods = jnp.moveaxis(mods, 0, 1).reshape(DEPTH, d6)

    tabq, tabk = _rope_tables(positions[0])
    half = N_DEV // 2
    up_cols = shard["w_up"].shape[2]
    cat = lambda a, axis, lo=0, hi=N_DEV: jnp.concatenate([a[j] for j in range(lo, hi)], axis=axis)

    def stage_pieces(l):
        out = {"pool_w": shard["pool_w"].astype(BF16)} if l == 0 else {}
        out.update(w_up=shard["w_up"][l].astype(BF16), w_down=shard["w_down"][l].astype(BF16))
        if l >= N_A_LAYERS:
            out.update({k: shard[k][l - N_A_LAYERS].astype(BF16) for k in ("w_dq", "w_uq", "w_o")})
        if l == N_A_LAYERS:
            out.update({k: shard[k].astype(BF16) for k in ("w_dkv", "w_uk", "w_uv")})
        return out

    gathers, pool_all = {}, []

    def start_gather(l, behind=None):
        pieces = stage_pieces(l)
        if behind is not None:
            pieces, _ = lax.optimization_barrier((pieces, behind))
        handles, token = _exchange_start(f"gather_start_{l}", list(pieces.values()), scatter=False)
        gathers[l] = (handles, pieces)
        return token[0, 0]

    def wait_gather(l, keys, after, tag=""):
        handles, pieces = gathers[l]
        which = [list(pieces).index(k) for k in keys]
        lands = _exchange_wait(f"gather_wait_{l}{tag}", handles, after, scatter=False, which=which)
        return dict(zip(keys, own_slot(lands, [pieces[k] for k in keys])))

    def whole_weights(l, got):
        w = dict(norm1_g=norm1_g[l], norm2_g=norm2_g[l], conv_w=vec["conv_w"][l], conv_b=conv_b[l])
        if l == 0:
            pool_all.append(got["pool_w"])
        if l < N_A_LAYERS:
            w.update(pool_w=cat(pool_all[0][:, l], 1), pool_b=vec["pool_b"][l], pool_scale=vec["pool_scale"][l])
        else:
            rope = got["w_uq"][..., QK_NOPE:]
            ext = jnp.concatenate([got["w_uq"][..., :QK_NOPE], rope, _swap_halves(rope)], axis=-1)
            w.update(w_dq=got["w_dq"].reshape(D_MODEL, Q_RANK), w_uq_ext=cat(ext, -1), w_o=got["w_o"].reshape(D_MODEL, D_MODEL),
                     q_norm_g=q_norm_g[l - N_A_LAYERS])
        if l == N_A_LAYERS:
            w.update(w_dkv_ext=_extend_w_dkv(got["w_dkv"].reshape(D_MODEL, KV_RANK + QK_ROPE)), w_uk=cat(got["w_uk"], -1),
                     w_uv=cat(got["w_uv"], -1), kv_in_g=kv_in_g, ckv_norm_g=ckv_norm_g)
        return w

    def own_slot(lands, own):
        return [lax.dynamic_update_index_in_dim(p, o, me, 0) for p, o in zip(lands, own)]

    def fetch(l, after):
        up_parts = lambda g8: (cat(g8, -1, 0, half), cat(g8, -1, half, N_DEV))
        if l == 0:
            start_gather(0, behind=mods)
            got = wait_gather(0, ["pool_w"], mods, "_pool")
            w_up = lambda aft: up_parts(wait_gather(0, ["w_up"], aft, "_up")["w_up"])
            w_down = lambda aft: wait_gather(0, ["w_down"], aft, "_down")["w_down"].reshape(D_FF, D_MODEL)
        else:
            got = wait_gather(l, list(gathers[l][1]), after)
            up, down = up_parts(got["w_up"]), got["w_down"].reshape(D_FF, D_MODEL)
            w_up, w_down = (lambda aft: up), (lambda aft: down)
        w = dict(whole_weights(l, got), w_up=w_up, w_down=w_down)
        return w, (start_gather(l + 1) if l + 1 < DEPTH else 0.0)

    scatters, pending, pool_grads, piece_grads = {}, {}, {}, {}

    def reduce_pieces(l, keys, got):
        for k, p in zip(keys, got):
            piece_grads[(k, l)] = _sum8(f"sum_grads_{k}_{l}", p.reshape(N_DEV, -1, p.shape[-1])).reshape(p.shape[1:])

    def start_scatter(name, sent):
        sent = {k: a.astype(BF16) for k, a in sent.items()}
        handles, token = _exchange_start(f"scatter_start_{name}", list(sent.values()), scatter=True)
        scatters[name] = (handles, list(sent), [lax.dynamic_index_in_dim(a, me, 0, keepdims=False) for a in sent.values()])
        return token[0, 0]

    def finish_scatter(name, l, after):
        handles, keys, own = scatters.pop(name)
        reduce_pieces(l, keys, own_slot(_exchange_wait(f"scatter_wait_{name}", handles, after, scatter=True), own))

    def push(l, part, big, after):
        cut = lambda a, n, axis: jnp.stack([lax.slice_in_dim(a, j * n, (j + 1) * n, axis=axis) for j in range(N_DEV)])
        sent = {}
        if part == "down":
            sent["w_down"] = big["w_down"].reshape(N_DEV, D_FF // N_DEV, D_MODEL)
        elif part == "up":
            sent["w_up"] = jnp.stack([lax.slice_in_dim(big[half_], j * up_cols, (j + 1) * up_cols, axis=1)
                                      for half_ in ("w_up_a", "w_up_v") for j in range(half)])
        elif l < N_A_LAYERS:
            pool_grads[l] = big["pool_w"]
        else:
            ext = cut(big["w_uq_ext"], Q_EXT, 1)
            rope = ext[..., QK_NOPE:QK_HEAD] + _unswap_halves(ext[..., QK_HEAD:])
            sent.update(w_dq=big["w_dq"].reshape(N_DEV, D_MODEL // N_DEV, Q_RANK), w_uq=jnp.concatenate([ext[..., :QK_NOPE], rope], axis=-1),
                        w_o=big["w_o"].reshape(N_DEV, D_MODEL // N_DEV, D_MODEL))
        if part == "mix" and l == N_A_LAYERS:
            sent.update(w_dkv=_fold_w_dkv_grad(big["w_dkv_ext"]).reshape(N_DEV, D_MODEL // N_DEV, KV_RANK + QK_ROPE),
                        w_uk=cut(big["w_uk"], QK_NOPE, 1), w_uv=cut(big["w_uv"], V_HEAD, 1))
        if l == 0 and part != "mix":
            return start_scatter(f"0_{part}", sent)
        if l == 0:
            finish_scatter("1", 1, after)
            pool = _shard8(jnp.stack([pool_grads[a] for a in range(N_A_LAYERS)]), 2).astype(BF16)
            reduce_pieces(0, ["pool_w"], _exchange_many("scatter_pool_grads", [pool], scatter=True))
            return 0.0
        pending.setdefault(l, {}).update(sent)
        if part != "mix":
            return 0.0
        if l + 1 < DEPTH:
            finish_scatter(str(l + 1), l + 1, after)
        return start_scatter(str(l), pending.pop(l))

    loss_row, dx, g, dmods = _forward_backward(x[0], loss_target[0], mods, tabq, tabk, final_g, fetch, push)
    layers_of = lambda k, ls: jnp.stack([piece_grads[(k, l)] for l in ls])
    grads = dict(w_dkv=piece_grads[("w_dkv", N_A_LAYERS)], w_uk=piece_grads[("w_uk", N_A_LAYERS)], w_uv=piece_grads[("w_uv", N_A_LAYERS)])
    for k in ("w_dq", "w_uq", "w_o"):
        grads[k] = layers_of(k, range(N_A_LAYERS, DEPTH))

    small_names = REPLICATED_WEIGHTS + tuple(k for k, _ in VECTOR_WEIGHTS)
    small_out = [dmods] + [g[k] for k in small_names] + [loss_row]
    small_shapes = [a.shape for a in small_out]
    small_got = _exchange("gather_small_grads", _pack(small_out, F32, SMALL_ROW_MULTIPLE), scatter=False)
    summed = _unpack(_sum8("sum_small_grads", small_got), small_shapes)
    grads["mod_b"] = summed[0]
    for k, s in zip(small_names, summed[1:-1]):
        grads[k] = s
    for k, ax in VECTOR_WEIGHTS:
        n = shard[k].shape[ax]
        grads[k] = lax.dynamic_slice_in_dim(grads[k], me * n, n, axis=ax)
    loss = summed[-1][0, 0]
    dmods_all = _unpack(small_got, small_shapes)[0]
    dm_mine = lax.dynamic_slice_in_dim(dmods_all, me * mod_cols, mod_cols, axis=2)
    dm_mine = jnp.pad(jnp.moveaxis(dm_mine, 0, 1), ((0, 0), (0, N_DEV), (0, 0)))
    grads["mod_w"] = _mods_bwd("mods_bwd", c_all, dm_mine)

    delta, new_m, new_v = {}, {}, {}

    def adamw(k):
        shp = shard[k].shape
        grads[k] = grads[k].reshape(shp)
        d_, m_, v_ = _adamw(f"adamw_{k}", _as_2d(shard[k]), _as_2d(grads[k]), _as_2d(mom_m[k]), _as_2d(mom_v[k]))
        delta[k], new_m[k], new_v[k] = d_.reshape(shp), m_.reshape(shp), v_.reshape(shp)

    late = ("w_up", "w_down", "pool_w")
    for k in WEIGHT_ORDER:
        if k not in late:
            adamw(k)
    finish_scatter("0_down", 0, delta["final_g"])
    finish_scatter("0_up", 0, delta["final_g"])
    grads.update(w_up=layers_of("w_up", range(DEPTH)), w_down=layers_of("w_down", range(DEPTH)), pool_w=piece_grads[("pool_w", 0)])
    for k in late:
        adamw(k)
    return (loss, dx[None], *[grads[k] for k in WEIGHT_ORDER], *[delta[k] for k in WEIGHT_ORDER],
            *[new_m[k] for k in WEIGHT_ORDER], *[new_v[k] for k in WEIGHT_ORDER])
```

```python
import functools

import jax
import jax.numpy as jnp
from jax import lax
from jax.experimental import pallas as pl
from jax.experimental.pallas import tpu as pltpu

F32 = jnp.float32
BF16 = jnp.bfloat16

D_MODEL = 1024
DEPTH = 4
N_A_LAYERS = 2
N_B_LAYERS = 2
POOL_WINDOWS = (2, 4, 8, 16)
POOL_GROUP = 256
N_HEADS = 8
QK_NOPE = 128
QK_ROPE = 64
V_HEAD = 128
QK_HEAD = QK_NOPE + QK_ROPE
Q_RANK = 384
KV_RANK = 256
ROPE_THETA = 10000.0
D_FF = 2816
EPS = 1e-6
N_MOD = 6
ADAM_LR = 0.001
ADAM_B1 = 0.9
ADAM_B2 = 0.999
ADAM_EPS = 1e-08
ADAM_WD = 0.01
ADAM_STEP = 10

N_DEV = 8
LANES = 128
Q_EXT = 256
VMEM_LIMIT_BYTES = 48 * 1024 * 1024
MESH = pl.DeviceIdType.MESH
NEG_BIG = -0.7 * float(jnp.finfo(jnp.float32).max)


def _params(sem):
    return pltpu.CompilerParams(dimension_semantics=sem, vmem_limit_bytes=VMEM_LIMIT_BYTES)


def _tile(n, cap):
    if n <= cap:
        return n
    best = None
    for d in range(LANES, cap + 1, LANES):
        if n % d == 0:
            best = d
    assert best is not None, (n, cap)
    return best


def _dot(a, b, dims):
    return lax.dot_general(a, b, (dims, ((), ())), preferred_element_type=F32)


NN = ((1,), (0,))
NT = ((1,), (1,))
TN = ((0,), (0,))


def _mm(name, a, b, mode="nn", out_dtype=BF16, add=None, resid=None, gate=None, rowtab=None,
        tm_cap=1024, tn_cap=1408, tk_cap=1408):
    if mode == "tn":
        kdim, m = a.shape
    else:
        m, kdim = a.shape
    n = b.shape[0] if mode == "nt" else b.shape[1]
    tm, tn, tk = _tile(m, tm_cap), _tile(n, tn_cap), _tile(kdim, tk_cap)
    nk = kdim // tk
    dims = {"nn": NN, "nt": NT, "tn": TN}[mode]
    a_spec = pl.BlockSpec((tk, tm), lambda i, j, k: (k, i)) if mode == "tn" else pl.BlockSpec((tm, tk), lambda i, j, k: (i, k))
    b_spec = pl.BlockSpec((tn, tk), lambda i, j, k: (j, k)) if mode == "nt" else pl.BlockSpec((tk, tn), lambda i, j, k: (k, j))
    o_spec = pl.BlockSpec((tm, tn), lambda i, j, k: (i, j))
    g_spec = pl.BlockSpec((1, tn), lambda i, j, k: (0, j))
    gated = resid is not None

    def body(*refs):
        a_ref, b_ref = refs[0], refs[1]
        acc = refs[-1]
        k = pl.program_id(2)

        @pl.when(k == 0)
        def _():
            acc[...] = jnp.zeros_like(acc)

        acc[...] += _dot(a_ref[...].astype(BF16), b_ref[...].astype(BF16), dims)

        @pl.when(k == nk - 1)
        def _():
            if gated:
                r_ref, g_ref, y_ref, x_ref = refs[2:6]
                y_ref[...] = acc[...]
                x_ref[...] = r_ref[...] + g_ref[...] * acc[...]
            elif add is not None:
                refs[3][...] = (acc[...] + refs[2][...].astype(F32)).astype(out_dtype)
            elif rowtab is not None:
                tab = refs[2][...]
                refs[3][...] = (acc[...] * jnp.concatenate([tab] * (tn // tab.shape[1]), axis=1)).astype(out_dtype)
            else:
                refs[2][...] = acc[...].astype(out_dtype)

    ins, in_specs = [a, b], [a_spec, b_spec]
    if rowtab is not None:
        assert tn % rowtab.shape[1] == 0 and not gated and add is None
        ins.append(rowtab)
        in_specs.append(pl.BlockSpec((tm, rowtab.shape[1]), lambda i, j, k: (i, 0)))
    if gated:
        ins += [resid, gate]
        in_specs += [o_spec, g_spec]
        out_shape = (jax.ShapeDtypeStruct((m, n), F32), jax.ShapeDtypeStruct((m, n), F32))
        out_specs = (o_spec, o_spec)
    else:
        if add is not None:
            ins.append(add)
            in_specs.append(o_spec)
        out_shape = jax.ShapeDtypeStruct((m, n), out_dtype)
        out_specs = o_spec
    return pl.pallas_call(
        body, name=name, grid=(m // tm, n // tn, nk), in_specs=in_specs, out_specs=out_specs, out_shape=out_shape,
        scratch_shapes=[pltpu.VMEM((tm, tn), F32)],
        compiler_params=_params(("parallel", "parallel", "arbitrary")),
    )(*ins)


def _rowwise(name, fn, tiled, bcast, outs, sums=(), tr=512):
    tiled = [t if isinstance(t, tuple) else (t, t.shape[1], 0) for t in tiled]
    s = tiled[0][0].shape[0]
    tr = min(tr, s)
    assert s % tr == 0
    n_t, n_b, n_o = len(tiled), len(bcast), len(outs)

    def body(*refs):
        i = pl.program_id(0)
        vals = [r[...] for r in refs[:n_t + n_b]]
        o_vals, s_vals = fn(*vals)
        for r, v in zip(refs[n_t + n_b:n_t + n_b + n_o], o_vals):
            r[...] = v.astype(r.dtype)
        s_refs = refs[n_t + n_b + n_o:]

        @pl.when(i == 0)
        def _():
            for r in s_refs:
                r[...] = jnp.zeros_like(r)

        for r, v in zip(s_refs, s_vals):
            r[...] += v

    in_specs = [pl.BlockSpec((tr, n), functools.partial(lambda cb, i: (i, cb), cb)) for (_, n, cb) in tiled]
    in_specs += [pl.BlockSpec(b.shape, functools.partial(lambda nd, i: (0,) * nd, b.ndim)) for b in bcast]
    out_specs = [pl.BlockSpec((tr, n), lambda i: (i, 0)) for (n, _) in outs]
    out_specs += [pl.BlockSpec((1, n), lambda i: (0, 0)) for n in sums]
    out_shape = [jax.ShapeDtypeStruct((s, n), dt) for (n, dt) in outs]
    out_shape += [jax.ShapeDtypeStruct((1, n), F32) for n in sums]
    res = pl.pallas_call(
        body, name=name, grid=(s // tr,), in_specs=in_specs, out_specs=tuple(out_specs), out_shape=tuple(out_shape),
        compiler_params=_params(("arbitrary",)),
    )(*[t[0] for t in tiled], *bcast)
    return res


def _colsum(v):
    return jnp.sum(v, axis=0, keepdims=True)


def _rms_fwd(name, x, g, scale=None, shift=None, out_dtype=BF16, ncols=None):
    mod = scale is not None

    def fn(xv, gv, *ss):
        y = xv * lax.rsqrt(jnp.mean(xv * xv, axis=-1, keepdims=True) + EPS) * gv
        if mod:
            y = y * (1.0 + ss[0]) + ss[1]
        return (y,), ()

    n = ncols or x.shape[1]
    return _rowwise(name, fn, [(x, n, 0)], [g] + ([scale, shift] if mod else []), [(n, out_dtype)])[0]


def _rms_bwd(name, x, g, dh, scale=None, dx_in=None, ncols=None, out_dtype=F32):
    mod = scale is not None
    has_in = dx_in is not None

    def fn(*vals):
        xv, dhv = vals[0], vals[1].astype(F32)
        rest = list(vals[2:])
        dxi = rest.pop(0) if has_in else None
        gv = rest.pop(0)
        rstd = lax.rsqrt(jnp.mean(xv * xv, axis=-1, keepdims=True) + EPS)
        xhat = xv * rstd
        sums = []
        if mod:
            sc = rest.pop(0)
            dyn = dhv * (1.0 + sc)
            dshift, dscale = _colsum(dhv), _colsum(dhv * (xhat * gv))
        else:
            dyn = dhv
        dg = _colsum(dyn * xhat)
        dxhat = dyn * gv
        dx = rstd * (dxhat - xhat * jnp.mean(dxhat * xhat, axis=-1, keepdims=True))
        if has_in:
            dx = dx + dxi
        sums = [dg] + ([dshift, dscale] if mod else [])
        return (dx,), sums

    n = ncols or x.shape[1]
    tiled = [(x, n, 0), dh] + ([dx_in] if has_in else [])
    return _rowwise(name, fn, tiled, [g] + ([scale] if mod else []), [(n, out_dtype)], [n] * (3 if mod else 1))


def _gate_bwd(name, dxn, y, g):
    def fn(dv, yv, gv):
        return (gv * dv,), (_colsum(dv * yv),)

    n = dxn.shape[1]
    return _rowwise(name, fn, [dxn, y], [g], [(n, BF16)], [n])


def _loss_head(name, x, g, target):
    n = x.shape[1]

    def fn(xv, tv, gv):
        rstd = lax.rsqrt(jnp.mean(xv * xv, axis=-1, keepdims=True) + EPS)
        xhat = xv * rstd
        err = xhat * gv - tv
        loss = 0.5 * jnp.sum(jnp.sum(err * err, axis=-1, keepdims=True) / n, axis=0, keepdims=True)
        dy = err / n
        dg = _colsum(dy * xhat)
        dxhat = dy * gv
        dx = rstd * (dxhat - xhat * jnp.mean(dxhat * xhat, axis=-1, keepdims=True))
        return (dx,), (dg, jnp.broadcast_to(loss, (1, LANES)))

    return _rowwise(name, fn, [x, target], [g], [(n, F32)], [n, LANES])


def _krope_fwd(name, kv_ext, tabk):
    def fn(xv, tv):
        t = xv * tv
        return (t + pltpu.roll(t, 64, 1),), ()

    return _rowwise(name, fn, [(kv_ext, LANES, 2), tabk], [], [(LANES, BF16)])[0]


def _krope_bwd(name, dkd, tabk):
    def fn(dv, tv):
        return ((dv + pltpu.roll(dv, 64, 1)) * tv,), ()

    return _rowwise(name, fn, [dkd, tabk], [], [(LANES, F32)])[0]


def _adamw(name, w, g, m, v):
    def fn(wv, gv, mv, vv):
        m2 = ADAM_B1 * mv + (1.0 - ADAM_B1) * gv
        v2 = ADAM_B2 * vv + (1.0 - ADAM_B2) * (gv * gv)
        m_hat = m2 / (1.0 - ADAM_B1 ** ADAM_STEP)
        v_hat = v2 / (1.0 - ADAM_B2 ** ADAM_STEP)
        delta = -ADAM_LR * (m_hat / (jnp.sqrt(v_hat) + ADAM_EPS) + ADAM_WD * wv)
        return (delta, m2, v2), ()

    r, c = w.shape
    tr = r
    for cand in (512, 256, 128, 64, 32, 16, 8):
        if r % cand == 0 and r > cand:
            tr = cand
            break
    return _rowwise(name, fn, [w, g, m, v], [], [(c, F32)] * 3, tr=tr)


def _sum8(name, parts):
    _, r, c = parts.shape
    tr = r
    for cand in (2048, 1024, 512, 256, 128, 64, 32, 16):
        if r % cand == 0 and r > cand and cand * c <= 256 * 1024:
            tr = cand
            break

    def body(p_ref, o_ref):
        acc = p_ref[0].astype(F32)
        for k in range(1, N_DEV):
            acc = acc + p_ref[k].astype(F32)
        o_ref[...] = acc

    return pl.pallas_call(
        body, name=name, grid=(r // tr,), in_specs=[pl.BlockSpec((N_DEV, tr, c), lambda i: (0, i, 0))],
        out_specs=pl.BlockSpec((tr, c), lambda i: (i, 0)), out_shape=jax.ShapeDtypeStruct((r, c), F32),
        compiler_params=_params(("parallel",)),
    )(parts)


def _mods_fwd(name, c_all, w, b):
    depth, d, n = w.shape

    def body(c_ref, w_ref, b_ref, o_ref):
        cv = c_ref[...]
        sc = (cv * (1.0 / (1.0 + jnp.exp(-cv)))).astype(BF16)
        o_ref[0] = _dot(sc, w_ref[0].astype(BF16), NN) + b_ref[0]

    return pl.pallas_call(
        body, name=name, grid=(depth,),
        in_specs=[pl.BlockSpec(c_all.shape, lambda l: (0, 0)), pl.BlockSpec((1, d, n), lambda l: (l, 0, 0)),
                  pl.BlockSpec((1, 1, n), lambda l: (l, 0, 0))],
        out_specs=pl.BlockSpec((1, c_all.shape[0], n), lambda l: (l, 0, 0)),
        out_shape=jax.ShapeDtypeStruct((depth, c_all.shape[0], n), F32),
        compiler_params=_params(("parallel",)),
    )(c_all, w, b.reshape(depth, 1, n))


def _mods_bwd(name, c_all, dm):
    depth, rows, n = dm.shape
    d = c_all.shape[1]

    def body(c_ref, dm_ref, o_ref):
        cv = c_ref[...]
        sc = (cv * (1.0 / (1.0 + jnp.exp(-cv)))).astype(BF16)
        o_ref[0] = _dot(sc, dm_ref[0].astype(BF16), TN)

    return pl.pallas_call(
        body, name=name, grid=(depth,),
        in_specs=[pl.BlockSpec(c_all.shape, lambda l: (0, 0)), pl.BlockSpec((1, rows, n), lambda l: (l, 0, 0))],
        out_specs=pl.BlockSpec((1, d, n), lambda l: (l, 0, 0)),
        out_shape=jax.ShapeDtypeStruct((depth, d, n), F32),
        compiler_params=_params(("parallel",)),
    )(c_all, dm)


POOL_TILE = 256


def _split_dot(band, val):
    hi = val.astype(BF16)
    lo = (val - hi.astype(F32)).astype(BF16)
    return _dot(band, hi, NN) + _dot(band, lo, NN)


def _pool_fwd(name, h1, x, pw, pb, ps, g1):
    s, d = h1.shape
    t = POOL_TILE

    def body(hc_ref, hp_ref, x_ref, pw_ref, pb_ref, ps_ref, g_ref, xo_ref, zb_ref, pooled_ref):
        i = pl.program_id(0)
        r = lax.broadcasted_iota(jnp.int32, (t, t), 0)
        j = lax.broadcasted_iota(jnp.int32, (t, t), 1)
        pos = (i * t + lax.broadcasted_iota(jnp.int32, (t, 1), 0) + 1).astype(F32)
        has_prev = (i > 0).astype(F32)
        for grp, w in enumerate(POOL_WINDOWS):
            cs = slice(grp * POOL_GROUP, (grp + 1) * POOL_GROUP)
            hc = hc_ref[:, cs]
            band_cur = ((r - j >= 0) & (r - j < w)).astype(BF16)
            band_prev = (r + t - j < w).astype(BF16)
            ssum = _split_dot(band_cur, hc) + has_prev * _split_dot(band_prev, hp_ref[:, cs])
            pooled = (ssum / jnp.minimum(pos, float(w)) - hc).astype(BF16)
            zb = _dot(pooled, pw_ref[grp], NN) + pb_ref[:, cs]
            xo_ref[:, cs] = x_ref[:, cs] + g_ref[:, cs] * (zb * ps_ref[:, cs])
            zb_ref[:, cs] = zb
            pooled_ref[:, cs] = pooled

    row = pl.BlockSpec((t, d), lambda i: (i, 0))
    vec = pl.BlockSpec((1, d), lambda i: (0, 0))
    return pl.pallas_call(
        body, name=name, grid=(s // t,),
        in_specs=[row, pl.BlockSpec((t, d), lambda i: (jnp.maximum(i - 1, 0), 0)), row,
                  pl.BlockSpec(pw.shape, lambda i: (0, 0, 0)), vec, vec, vec],
        out_specs=(row, row, row),
        out_shape=(jax.ShapeDtypeStruct((s, d), F32), jax.ShapeDtypeStruct((s, d), F32), jax.ShapeDtypeStruct((s, d), BF16)),
        compiler_params=_params(("parallel",)),
    )(h1, h1, x, pw, pb, ps, g1)


def _pool_bwd(name, dxn, zb, pooled, pw, ps, g1):
    s, d = dxn.shape
    t = POOL_TILE
    nt = s // t

    def body(dc_ref, dn_ref, zb_ref, pooled_ref, pw_ref, ps_ref, g_ref, dh_ref, dpw_ref, dpb_ref, dps_ref, dg_ref):
        i = pl.program_id(0)

        @pl.when(i == 0)
        def _():
            dpw_ref[...] = jnp.zeros_like(dpw_ref)
            dpb_ref[...] = jnp.zeros_like(dpb_ref)
            dps_ref[...] = jnp.zeros_like(dps_ref)
            dg_ref[...] = jnp.zeros_like(dg_ref)

        jj = lax.broadcasted_iota(jnp.int32, (t, t), 0)
        rr = lax.broadcasted_iota(jnp.int32, (t, t), 1)
        pos = (i * t + lax.broadcasted_iota(jnp.int32, (t, 1), 0) + 1).astype(F32)
        has_next = (i < nt - 1).astype(F32)
        for grp, w in enumerate(POOL_WINDOWS):
            cs = slice(grp * POOL_GROUP, (grp + 1) * POOL_GROUP)
            gv, psv, zbv, dxc = g_ref[:, cs], ps_ref[:, cs], zb_ref[:, cs], dc_ref[:, cs]
            dg_ref[:, cs] += _colsum(dxc * (zbv * psv))
            dy = gv * dxc
            dps_ref[:, cs] += _colsum(dy * zbv)
            dz = dy * psv
            dpb_ref[:, cs] += _colsum(dz)
            dzb = dz.astype(BF16)
            dpw_ref[grp] += _dot(pooled_ref[:, cs], dzb, TN)
            dp = _dot(dzb, pw_ref[grp], NT)
            dzn = (gv * dn_ref[:, cs] * psv).astype(BF16)
            dpn = _dot(dzn, pw_ref[grp], NT) * (has_next / float(w))
            band_cur = ((rr - jj >= 0) & (rr - jj < w)).astype(BF16)
            band_next = (rr + t - jj < w).astype(BF16)
            dh_ref[:, cs] = _split_dot(band_cur, dp / jnp.minimum(pos, float(w))) + _split_dot(band_next, dpn) - dp

    row = pl.BlockSpec((t, d), lambda i: (i, 0))
    vec = pl.BlockSpec((1, d), lambda i: (0, 0))
    wspec = pl.BlockSpec(pw.shape, lambda i: (0, 0, 0))
    return pl.pallas_call(
        body, name=name, grid=(nt,),
        in_specs=[row, pl.BlockSpec((t, d), lambda i: (jnp.minimum(i + 1, nt - 1), 0)), row, row, wspec, vec, vec],
        out_specs=(row, wspec, vec, vec, vec),
        out_shape=(jax.ShapeDtypeStruct((s, d), F32), jax.ShapeDtypeStruct(pw.shape, F32),
                   jax.ShapeDtypeStruct((1, d), F32), jax.ShapeDtypeStruct((1, d), F32), jax.ShapeDtypeStruct((1, d), F32)),
        compiler_params=_params(("arbitrary",)),
    )(dxn, dxn, zb, pooled, pw, ps, g1)


GLU_TILE = 512
HALO = 16
INV_SQRT2 = 0.7071067811865476
INV_SQRT_2PI = 0.3989422804014327


def _up_glu_fwd(name, h2, wa, wv, cw, cb):
    s, d = h2.shape
    f = wa.shape[1]
    tm, tn = _tile(s, 1024), _tile(f, 1408)

    def body(h_ref, hh_ref, wa_ref, wv_ref, cw_ref, cb_ref, ua_ref, gl_ref, gpv_ref, ge_ref):
        i = pl.program_id(1)
        has_prev = (i > 0).astype(F32)
        a = _dot(h_ref[...], wa_ref[...], NN).astype(BF16)
        v = _dot(h_ref[...], wv_ref[...], NN)
        above = (_dot(hh_ref[...], wa_ref[...], NN) * has_prev).astype(BF16)
        ua_ref[...] = a
        ext = jnp.concatenate([above.astype(F32), a.astype(F32)], axis=0)
        e1 = pltpu.roll(ext, 1, 0)[HALO:]
        e2 = pltpu.roll(ext, 2, 0)[HALO:]
        pre = e2 * cw_ref[0:1, :] + e1 * cw_ref[1:2, :] + ext[HALO:] * cw_ref[2:3, :] + cb_ref[...]
        cdf = 0.5 * (1.0 + lax.erf(pre * INV_SQRT2))
        ge = pre * cdf
        gl_ref[...] = (ge * v).astype(gl_ref.dtype)
        gpv_ref[...] = ((cdf + pre * (INV_SQRT_2PI * jnp.exp(-0.5 * pre * pre))) * v).astype(gpv_ref.dtype)
        ge_ref[...] = ge.astype(ge_ref.dtype)

    blk = pl.BlockSpec((tm, tn), lambda j, i: (i, j))
    wspec = pl.BlockSpec((d, tn), lambda j, i: (0, j))
    return pl.pallas_call(
        body, name=name, grid=(f // tn, s // tm),
        in_specs=[pl.BlockSpec((tm, d), lambda j, i: (i, 0)), pl.BlockSpec((HALO, d), lambda j, i: (jnp.maximum(i * (tm // HALO) - 1, 0), 0)),
                  wspec, wspec, pl.BlockSpec((3, tn), lambda j, i: (0, j)), pl.BlockSpec((1, tn), lambda j, i: (0, j))],
        out_specs=(blk, blk, blk, blk), out_shape=tuple(jax.ShapeDtypeStruct((s, f), BF16) for _ in range(4)),
        compiler_params=_params(("parallel", "parallel")),
    )(h2, h2, wa, wv, cw, cb)


def _down_glu_bwd(name, dy2, wd, ua, gpv, ge, cw):
    s, f = ua.shape
    d = dy2.shape[1]
    t, tf = min(GLU_TILE, s), _tile(f, 1408)
    nt = s // t
    te = t + HALO

    def body(dy_ref, dyn_ref, wd_ref, a_ref, ah_ref, g_ref, gn_ref, ge_ref, cw_ref, da_ref, dv_ref, dcw_ref, dcb_ref):
        i = pl.program_id(1)

        @pl.when(i == 0)
        def _():
            dcw_ref[...] = jnp.zeros_like(dcw_ref)
            dcb_ref[...] = jnp.zeros_like(dcb_ref)

        has_prev = (i > 0).astype(F32)
        has_next = (i < nt - 1).astype(F32)
        wdv = wd_ref[...]
        dgl = _dot(dy_ref[...], wdv, NT)
        dgl_below = _dot(dyn_ref[...], wdv, NT) * has_next
        dpre = jnp.concatenate([dgl * g_ref[...].astype(F32), dgl_below * gn_ref[...].astype(F32)], axis=0)
        c0, c1, c2 = cw_ref[0:1, :], cw_ref[1:2, :], cw_ref[2:3, :]
        up1 = pltpu.roll(dpre, te - 1, 0)
        up2 = pltpu.roll(dpre, te - 2, 0)
        da_ref[...] = (dpre * c2 + up1 * c1 + up2 * c0)[:t].astype(da_ref.dtype)
        dv_ref[...] = (dgl * ge_ref[...].astype(F32)).astype(dv_ref.dtype)
        ext = jnp.concatenate([ah_ref[...].astype(F32) * has_prev, a_ref[...].astype(F32)], axis=0)
        dpt = dpre[:t]
        dcb_ref[...] += _colsum(dpt)
        dcw_ref[0:1, :] += _colsum(pltpu.roll(ext, 2, 0)[HALO:] * dpt)
        dcw_ref[1:2, :] += _colsum(pltpu.roll(ext, 1, 0)[HALO:] * dpt)
        dcw_ref[2:3, :] += _colsum(ext[HALO:] * dpt)

    blk = pl.BlockSpec((t, tf), lambda j, i: (i, j))
    prev = pl.BlockSpec((HALO, tf), lambda j, i: (jnp.maximum(i * (t // HALO) - 1, 0), j))
    below = lambda i: jnp.minimum((i + 1) * (t // HALO), s // HALO - 1)
    w3 = pl.BlockSpec((3, tf), lambda j, i: (0, j))
    w1 = pl.BlockSpec((1, tf), lambda j, i: (0, j))
    return pl.pallas_call(
        body, name=name, grid=(f // tf, nt),
        in_specs=[pl.BlockSpec((t, d), lambda j, i: (i, 0)), pl.BlockSpec((HALO, d), lambda j, i: (below(i), 0)),
                  pl.BlockSpec((tf, d), lambda j, i: (j, 0)), blk, prev, blk, pl.BlockSpec((HALO, tf), lambda j, i: (below(i), j)), blk, w3],
        out_specs=(blk, blk, w3, w1),
        out_shape=(jax.ShapeDtypeStruct((s, f), BF16), jax.ShapeDtypeStruct((s, f), BF16),
                   jax.ShapeDtypeStruct((3, f), F32), jax.ShapeDtypeStruct((1, f), F32)),
        compiler_params=_params(("parallel", "arbitrary")),
    )(dy2, dy2, wd, ua, ua, gpv, gpv, ge, cw)


ATT_TILE = 512
ATT_ROWS = 256
LOG2E = 1.4426950408889634
LN2 = 0.6931471805599453


def _head_blocks_t(a, width):
    s = a.shape[0]
    t = min(ATT_TILE, s)
    return a.reshape(s // t, t, N_HEADS, width).transpose(2, 0, 3, 1)


def _causal_mask(sv, q0, k0):
    row = q0 + lax.broadcasted_iota(jnp.int32, sv.shape, 0)
    col = k0 + lax.broadcasted_iota(jnp.int32, sv.shape, 1)
    return jnp.where(col <= row, sv, NEG_BIG)


def _attn_fwd(name, q_rot, kt4, v_ext):
    s = q_rot.shape[0]
    t = min(ATT_TILE, s)
    nq = s // t

    rq = min(ATT_ROWS, t)

    def body(q_ref, kt_ref, v_ref, o_ref, lse_ref, row_ref, acc_ref, m_ref):
        qi = pl.program_id(1)
        acc_ref[...] = jnp.zeros_like(acc_ref)
        m_ref[...] = jnp.full_like(m_ref, NEG_BIG)

        def step(j, masked):
            v_blk = v_ref[pl.ds(pl.multiple_of(j * t, t), t), :]
            for r in range(t // rq):
                rs = pl.ds(r * rq, rq)
                sv = _dot(q_ref[rs, :], kt_ref[0, j], NN)
                if masked:
                    sv = _causal_mask(sv, r * rq, 0)
                m_prev = m_ref[rs, :]
                m_new = jnp.maximum(m_prev, jnp.max(sv, axis=-1, keepdims=True))
                p = jnp.exp2(sv - m_new).astype(BF16)
                acc_ref[rs, :] = jnp.exp2(m_prev - m_new) * acc_ref[rs, :] + _dot(p, v_blk, NN)
                m_ref[rs, :] = m_new

        def full_step(j, carry):
            step(j, False)
            return carry

        lax.fori_loop(0, qi, full_step, 0)
        step(qi, True)
        l = acc_ref[:, V_HEAD:V_HEAD + 1]
        o_ref[...] = (acc_ref[:, :V_HEAD] / l).astype(o_ref.dtype)
        lse = jnp.broadcast_to(m_ref[...] + jnp.log(l) * LOG2E, lse_ref.shape)
        lse_ref[...] = lse
        row_ref[0, 0] = jnp.transpose(lse)[0:8, :]

    head_q = pl.BlockSpec((t, Q_EXT), lambda h, i: (i, h))
    head_o = pl.BlockSpec((t, V_HEAD), lambda h, i: (i, h))
    return pl.pallas_call(
        body, name=name, grid=(N_HEADS, nq),
        in_specs=[head_q, pl.BlockSpec((1, nq, Q_EXT, t), lambda h, i: (h, 0, 0, 0)), pl.BlockSpec((s, Q_EXT), lambda h, i: (0, h))],
        out_specs=(head_o, head_o, pl.BlockSpec((1, 1, 8, t), lambda h, i: (h, i, 0, 0))),
        out_shape=(jax.ShapeDtypeStruct((s, N_HEADS * V_HEAD), BF16), jax.ShapeDtypeStruct((s, N_HEADS * LANES), F32),
                   jax.ShapeDtypeStruct((N_HEADS, nq, 8, t), F32)),
        scratch_shapes=[pltpu.VMEM((t, Q_EXT), F32), pltpu.VMEM((t, 1), F32)],
        compiler_params=_params(("parallel", "parallel")),
    )(q_rot, kt4, v_ext)


def _attn_dq(name, q_rot, tabq, kt4, kfull, vt4, o, lse, do):
    s = q_rot.shape[0]
    t = min(ATT_TILE, s)
    nq = s // t

    def body(q_ref, tab_ref, kt_ref, k_ref, vt_ref, o_ref, lse_ref, do_ref, dq_ref, delta_ref, acc_ref):
        qi = pl.program_id(1)
        q = q_ref[...]
        dov = do_ref[...]
        delta = jnp.sum(dov.astype(F32) * o_ref[...].astype(F32), axis=-1, keepdims=True)
        lse = lse_ref[:, 0:1]
        acc_ref[...] = jnp.zeros_like(acc_ref)

        def step(j, masked):
            sv = _dot(q, kt_ref[0, j], NN)
            if masked:
                sv = _causal_mask(sv, qi * t, j * t)
            p = jnp.exp2(sv - lse)
            dp = _dot(dov, vt_ref[0, j], NN)
            ds = (p * (dp - delta)).astype(BF16)
            acc_ref[...] += _dot(ds, k_ref[pl.ds(pl.multiple_of(j * t, t), t), :], NN)

        def full_step(j, carry):
            step(j, False)
            return carry

        lax.fori_loop(0, qi, full_step, 0)
        step(qi, True)
        dq_ref[...] = (acc_ref[...] * (tab_ref[...] * LN2)).astype(dq_ref.dtype)
        delta_ref[0, 0] = jnp.transpose(jnp.broadcast_to(delta, (t, LANES)))[0:8, :]

    head_q = pl.BlockSpec((t, Q_EXT), lambda h, i: (i, h))
    head_o = pl.BlockSpec((t, V_HEAD), lambda h, i: (i, h))
    return pl.pallas_call(
        body, name=name, grid=(N_HEADS, nq),
        in_specs=[head_q, pl.BlockSpec((t, Q_EXT), lambda h, i: (i, 0)), pl.BlockSpec((1, nq, Q_EXT, t), lambda h, i: (h, 0, 0, 0)),
                  pl.BlockSpec((s, Q_EXT), lambda h, i: (0, h)), pl.BlockSpec((1, nq, V_HEAD, t), lambda h, i: (h, 0, 0, 0)),
                  head_o, head_o, head_o],
        out_specs=(head_q, pl.BlockSpec((1, 1, 8, t), lambda h, i: (h, i, 0, 0))),
        out_shape=(jax.ShapeDtypeStruct((s, N_HEADS * Q_EXT), BF16), jax.ShapeDtypeStruct((N_HEADS, nq, 8, t), F32)),
        scratch_shapes=[pltpu.VMEM((t, Q_EXT), F32)],
        compiler_params=_params(("parallel", "parallel")),
    )(q_rot, tabq, kt4, kfull, vt4, o, lse, do)


def _attn_dkv(name, kfull, v, qt4, q_rot, dot4, do, lse_row, delta_row, acc_in=None):
    s = kfull.shape[0]
    t = min(ATT_TILE, s)
    nq = s // t
    has_in = acc_in is not None

    def body(*refs):
        k_ref, v_ref, qt_ref, q_ref, dot_ref, do_ref, lse_ref, delta_ref = refs[:8]
        dkn_ref, dkd_ref, dv_ref, acck_ref, accv_ref = refs[-5:]
        kj, h = pl.program_id(0), pl.program_id(1)
        k_blk, v_blk = k_ref[...], v_ref[...]
        acck_ref[...] = jnp.zeros_like(acck_ref)
        accv_ref[...] = jnp.zeros_like(accv_ref)

        def step(i, masked):
            qs = pl.ds(pl.multiple_of(i * t, t), t)
            st = _dot(k_blk, qt_ref[0, i], NN)
            if masked:
                krow = lax.broadcasted_iota(jnp.int32, st.shape, 0)
                qcol = lax.broadcasted_iota(jnp.int32, st.shape, 1)
                st = jnp.where(krow <= qcol, st, NEG_BIG)
            pt = jnp.exp2(st - lse_ref[0, i, 0:1, :])
            accv_ref[...] += _dot(pt.astype(BF16), do_ref[qs, :], NN)
            dpt = _dot(v_blk, dot_ref[0, i], NN)
            dst = (pt * (dpt - delta_ref[0, i, 0:1, :])).astype(BF16)
            acck_ref[...] += _dot(dst, q_ref[qs, :], NN)

        def full_step(i, carry):
            step(i, False)
            return carry

        step(kj, True)
        lax.fori_loop(kj + 1, nq, full_step, 0)
        dk = acck_ref[...] * LN2
        dkn, dkd = dk[:, :QK_NOPE], dk[:, QK_NOPE:]
        if has_in:
            dkn = dkn + refs[8][...]
            dv_ref[...] = accv_ref[...] + refs[10][...]
        else:
            dv_ref[...] = accv_ref[...]
        dkn_ref[...] = dkn

        @pl.when(h == 0)
        def _():
            if has_in:
                dkd_ref[...] = dkd + refs[9][...]
            else:
                dkd_ref[...] = dkd

        @pl.when(h > 0)
        def _():
            dkd_ref[...] += dkd

    kblk = pl.BlockSpec((t, LANES), lambda j, h: (j, h))
    kdblk = pl.BlockSpec((t, LANES), lambda j, h: (j, 0))
    col = pl.BlockSpec((s, LANES), lambda j, h: (0, h))
    stat = pl.BlockSpec((1, nq, 8, t), lambda j, h: (h, 0, 0, 0))
    ins = [kfull, v, qt4, q_rot, dot4, do, lse_row, delta_row]
    in_specs = [pl.BlockSpec((t, Q_EXT), lambda j, h: (j, h)), kblk, pl.BlockSpec((1, nq, Q_EXT, t), lambda j, h: (h, 0, 0, 0)),
                pl.BlockSpec((s, Q_EXT), lambda j, h: (0, h)), pl.BlockSpec((1, nq, V_HEAD, t), lambda j, h: (h, 0, 0, 0)), col, stat, stat]
    if has_in:
        ins += list(acc_in)
        in_specs += [kblk, kdblk, kblk]
    return pl.pallas_call(
        body, name=name, grid=(nq, N_HEADS), in_specs=in_specs, out_specs=(kblk, kdblk, kblk),
        out_shape=(jax.ShapeDtypeStruct((s, N_HEADS * LANES), F32), jax.ShapeDtypeStruct((s, LANES), F32),
                   jax.ShapeDtypeStruct((s, N_HEADS * LANES), F32)),
        scratch_shapes=[pltpu.VMEM((t, Q_EXT), F32), pltpu.VMEM((t, LANES), F32)],
        compiler_params=_params(("parallel", "arbitrary")),
    )(*ins)


def _swap_halves(w):
    half = w.shape[-1] // 2
    return jnp.concatenate([-w[..., half:], w[..., :half]], axis=-1)


def _unswap_halves(g):
    half = g.shape[-1] // 2
    return jnp.concatenate([g[..., half:], -g[..., :half]], axis=-1)


def _extend_w_uq(w):
    r = w.reshape(Q_RANK, N_HEADS, QK_HEAD)
    rope = r[..., QK_NOPE:]
    return jnp.concatenate([r[..., :QK_NOPE], rope, _swap_halves(rope)], axis=-1).reshape(Q_RANK, N_HEADS * Q_EXT)


def _fold_w_uq_grad(g):
    r = g.reshape(Q_RANK, N_HEADS, Q_EXT)
    rope = r[..., QK_NOPE:QK_HEAD] + _unswap_halves(r[..., QK_HEAD:])
    return jnp.concatenate([r[..., :QK_NOPE], rope], axis=-1).reshape(Q_RANK, N_HEADS * QK_HEAD)


def _extend_w_dkv(w):
    return jnp.concatenate([w, _swap_halves(w[:, KV_RANK:])], axis=-1)


def _fold_w_dkv_grad(g):
    rope = g[:, KV_RANK:KV_RANK + QK_ROPE] + _unswap_halves(g[:, KV_RANK + QK_ROPE:])
    return jnp.concatenate([g[:, :KV_RANK], rope], axis=-1)


def _rope_tables(positions):
    inv = 1.0 / (ROPE_THETA ** (jnp.arange(0, QK_ROPE, 2, dtype=F32) / QK_ROPE))
    ang = positions.astype(F32)[:, None] * inv
    cos, sin = jnp.cos(ang), jnp.sin(ang)
    tabk = jnp.concatenate([cos, cos, sin, sin], axis=-1)
    scale = QK_HEAD ** -0.5 * LOG2E
    tabq = jnp.concatenate([jnp.full((positions.shape[0], QK_NOPE), scale, F32), tabk * scale], axis=-1)
    return tabq, tabk


def _forward_backward(x, target, mods, tabq, tabk, final_g, fetch, push):
    row = lambda vec: vec.reshape(1, -1)
    mod = [[row(mods[l, k * D_MODEL:(k + 1) * D_MODEL]) for k in range(N_MOD)] for l in range(DEPTH)]
    saved, weights = [], []
    kv = None
    for l in range(DEPTH):
        w, tok = fetch(l, x)
        sh1, sc1, g1, sh2, sc2, g2 = mod[l]
        sh1 = sh1 + tok
        if l == N_A_LAYERS:
            kvn = _rms_fwd("kvin_fwd", x, row(w["kv_in_g"]))
            kv_ext = _mm("dkv_fwd", kvn, w["w_dkv_ext"], out_dtype=F32)
            ckv = _rms_fwd("ckv_fwd", kv_ext, row(w["ckv_norm_g"]), ncols=KV_RANK)
            kd = _krope_fwd("krope_fwd", kv_ext, tabk)
            kn, v = _mm("uk_fwd", ckv, w["w_uk"]), _mm("uv_fwd", ckv, w["w_uv"])
            heads = lambda a: [a[:, h * LANES:(h + 1) * LANES] for h in range(N_HEADS)]
            kfull = jnp.concatenate([part for kh in heads(kn) for part in (kh, kd)], axis=-1)
            v_ext = jnp.concatenate([part for vh in heads(v) for part in (vh, jnp.ones_like(vh))], axis=-1)
            kv = dict(x=x, kvn=kvn, kv_ext=kv_ext, ckv=ckv, v=v, kfull=kfull, v_ext=v_ext,
                      kt4=_head_blocks_t(kfull, Q_EXT), vt4=_head_blocks_t(v, V_HEAD))
        x_in = x
        if l < N_A_LAYERS:
            h1 = _rms_fwd(f"norm1_fwd_{l}", x, row(w["norm1_g"]), sc1, sh1, out_dtype=F32)
            x_mid, zb, pooled = _pool_fwd(f"pool_fwd_{l}", h1, x, w["pool_w"], row(w["pool_b"]), row(w["pool_scale"]), g1)
            mix = (zb, pooled)
        else:
            h1 = _rms_fwd(f"norm1_fwd_{l}", x, row(w["norm1_g"]), sc1, sh1)
            cq_pre = _mm(f"dq_fwd_{l}", h1, w["w_dq"], out_dtype=F32)
            cq = _rms_fwd(f"qnorm_fwd_{l}", cq_pre, row(w["q_norm_g"]))
            q_rot = _mm(f"uq_fwd_{l}", cq, w["w_uq_ext"], rowtab=tabq)
            o, lse, lse_row = _attn_fwd(f"attn_fwd_{l}", q_rot, kv["kt4"], kv["v_ext"])
            y, x_mid = _mm(f"wo_fwd_{l}", o, w["w_o"], resid=x, gate=g1)
            mix = (h1, cq_pre, cq, q_rot, o, lse, lse_row, y)
        h2 = _rms_fwd(f"norm2_fwd_{l}", x_mid, row(w["norm2_g"]), sc2, sh2)
        w_up_a, w_up_v = w["w_up"](h2)
        ua, gl, gpv, ge = _up_glu_fwd(f"up_glu_fwd_{l}", h2, w_up_a, w_up_v, w["conv_w"], row(w["conv_b"]))
        w_down = w["w_down"](gl)
        y2, x = _mm(f"down_fwd_{l}", gl, w_down, resid=x_mid, gate=g2)
        saved.append((x_in, x_mid, h2, ua, gpv, ge, gl, y2, mix))
        weights.append(dict(w, w_up_a=w_up_a, w_up_v=w_up_v, w_down=w_down))

    dx, dfinal_g, loss = _loss_head("loss_head", x, row(final_g), target)
    g = {"final_g": dfinal_g.reshape(-1)}
    per_layer = {k: [None] * DEPTH for k in ("norm1_g", "norm2_g", "conv_w", "conv_b")}
    per_a = {k: [None] * N_A_LAYERS for k in ("pool_b", "pool_scale")}
    per_b = {k: [None] * N_B_LAYERS for k in ("q_norm_g",)}
    dmods = [None] * DEPTH
    dkv = None
    tok = 0.0
    for l in reversed(range(DEPTH)):
        w, big = weights[l], {}
        sh1, sc1, g1, sh2, sc2, g2 = mod[l]
        g2 = g2 + tok
        x_in, x_mid, h2, ua, gpv, ge, gl, y2, mix = saved[l]
        dy2, dg2 = _gate_bwd(f"gate2_bwd_{l}", dx, y2, g2)
        tok = push(l, "down", dict(w_down=_mm(f"down_wgrad_{l}", gl, dy2, mode="tn", tm_cap=1408)), None)
        da, dv_, dcw, dcb = _down_glu_bwd(f"down_glu_bwd_{l}", dy2, w["w_down"], ua, gpv, ge, w["conv_w"] + tok)
        dh2 = _mm(f"up_a_bwd_{l}", da, w["w_up_a"], mode="nt", out_dtype=F32)
        dh2 = _mm(f"up_v_bwd_{l}", dv_, w["w_up_v"], mode="nt", out_dtype=F32, add=dh2)
        tok = push(l, "up", dict(w_up_a=_mm(f"up_a_wgrad_{l}", h2, da, mode="tn"), w_up_v=_mm(f"up_v_wgrad_{l}", h2, dv_, mode="tn")), None)
        per_layer["conv_w"][l], per_layer["conv_b"][l] = dcw, dcb.reshape(-1)
        dx_mid, dn2, dsh2, dsc2 = _rms_bwd(f"norm2_bwd_{l}", x_mid, row(w["norm2_g"]), dh2, sc2 + tok, dx_in=dx)
        per_layer["norm2_g"][l] = dn2.reshape(-1)
        if l < N_A_LAYERS:
            zb, pooled = mix
            dh1, dpw, dpb, dps, dg1 = _pool_bwd(f"pool_bwd_{l}", dx_mid, zb, pooled, w["pool_w"], row(w["pool_scale"]), g1)
            big["pool_w"] = dpw
            per_a["pool_b"][l], per_a["pool_scale"][l] = dpb.reshape(-1), dps.reshape(-1)
        else:
            j = l - N_A_LAYERS
            h1, cq_pre, cq, q_rot, o, lse, lse_row, y = mix
            dy, dg1 = _gate_bwd(f"gate1_bwd_{l}", dx_mid, y, g1)
            do = _mm(f"wo_bwd_{l}", dy, w["w_o"], mode="nt")
            big["w_o"] = _mm(f"wo_wgrad_{l}", o, dy, mode="tn")
            dq_ext, delta_row = _attn_dq(f"attn_dq_{l}", q_rot, tabq, kv["kt4"], kv["kfull"], kv["vt4"], o, lse, do)
            dkv = _attn_dkv(f"attn_dkv_{l}", kv["kfull"], kv["v"], _head_blocks_t(q_rot, Q_EXT), q_rot, _head_blocks_t(do, V_HEAD), do,
                            lse_row, delta_row, acc_in=dkv)
            dcq = _mm(f"uq_bwd_{l}", dq_ext, w["w_uq_ext"], mode="nt", out_dtype=F32)
            big["w_uq_ext"] = _mm(f"uq_wgrad_{l}", cq, dq_ext, mode="tn", out_dtype=F32)
            dcq_pre, dqn = _rms_bwd(f"qnorm_bwd_{l}", cq_pre, row(w["q_norm_g"]), dcq, out_dtype=BF16)
            per_b["q_norm_g"][j] = dqn.reshape(-1)
            dh1 = _mm(f"dq_bwd_{l}", dcq_pre, w["w_dq"], mode="nt")
            big["w_dq"] = _mm(f"dq_wgrad_{l}", h1, dcq_pre, mode="tn")
        dx, dn1, dsh1, dsc1 = _rms_bwd(f"norm1_bwd_{l}", x_in, row(w["norm1_g"]), dh1, sc1, dx_in=dx_mid)
        per_layer["norm1_g"][l] = dn1.reshape(-1)
        dmods[l] = jnp.concatenate([dsh1, dsc1, dg1, dsh2, dsc2, dg2], axis=-1).reshape(-1)
        if l == N_A_LAYERS:
            dkn, dkd, dv = dkv
            dckv = _mm("uk_bwd", dkn, w["w_uk"], mode="nt", out_dtype=F32)
            dckv = _mm("uv_bwd", dv, w["w_uv"], mode="nt", out_dtype=F32, add=dckv)
            big["w_uk"] = _mm("uk_wgrad", kv["ckv"], dkn, mode="tn")
            big["w_uv"] = _mm("uv_wgrad", kv["ckv"], dv, mode="tn")
            dkr = _krope_bwd("krope_bwd", dkd, tabk)
            dc, dckv_g = _rms_bwd("ckv_bwd", kv["kv_ext"], row(w["ckv_norm_g"]), dckv, ncols=KV_RANK, out_dtype=BF16)
            dkv_ext = jnp.concatenate([dc, dkr.astype(BF16)], axis=-1)
            dkvn = _mm("dkv_bwd", dkv_ext, w["w_dkv_ext"], mode="nt")
            big["w_dkv_ext"] = _mm("dkv_wgrad", kv["kvn"], dkv_ext, mode="tn", out_dtype=F32)
            dx, dkv_in_g = _rms_bwd("kvin_bwd", kv["x"], row(w["kv_in_g"]), dkvn, dx_in=dx)
            g["ckv_norm_g"], g["kv_in_g"] = dckv_g.reshape(-1), dkv_in_g.reshape(-1)
        tok = push(l, "mix", big, dx)
    for group in (per_layer, per_a, per_b):
        for k, vals in group.items():
            g[k] = jnp.stack(vals)
    return loss, dx, g, jnp.stack(dmods)


def _my_index():
    return 4 * lax.axis_index("x") + 2 * lax.axis_index("y") + lax.axis_index("c")


def _peer(k):
    x, y, c = lax.axis_index("x"), lax.axis_index("y"), lax.axis_index("c")
    return (1 - x if k & 4 else x, 1 - y if k & 2 else y, 1 - c if k & 1 else c)


def _index_of(pos):
    return 4 * pos[0] + 2 * pos[1] + pos[2]


def _exchange_many(name, arrays, scatter):
    n = len(arrays)
    blocks = [tuple(a.shape[1:]) if scatter else tuple(a.shape) for a in arrays]

    def body(*refs):
        x_refs, o_refs = refs[:n], refs[n:2 * n]
        send_sems, recv_sems, local_sems = refs[2 * n:]
        me = _my_index()
        started = []
        for a in range(n):
            mine = pltpu.make_async_copy(x_refs[a].at[me] if scatter else x_refs[a], o_refs[a].at[me], local_sems.at[a])
            mine.start()
            started.append(mine)
        sends = []
        for k in range(1, N_DEV):
            peer = _peer(k)
            for a in range(n):
                cp = pltpu.make_async_remote_copy(
                    src_ref=x_refs[a].at[_index_of(peer)] if scatter else x_refs[a], dst_ref=o_refs[a].at[me],
                    send_sem=send_sems.at[a, k - 1], recv_sem=recv_sems.at[a, k - 1], device_id=peer, device_id_type=MESH)
                cp.start()
                sends.append(cp)
        for k in range(1, N_DEV):
            peer = _peer(k)
            for a in range(n):
                pltpu.make_async_remote_copy(
                    src_ref=x_refs[a].at[me] if scatter else x_refs[a], dst_ref=o_refs[a].at[_index_of(peer)],
                    send_sem=send_sems.at[a, k - 1], recv_sem=recv_sems.at[a, k - 1], device_id=peer, device_id_type=MESH).wait_recv()
        for cp in sends:
            cp.wait_send()
        for mine in started:
            mine.wait()

    return pl.pallas_call(
        body, name=name, out_shape=tuple(jax.ShapeDtypeStruct((N_DEV,) + blk, a.dtype) for blk, a in zip(blocks, arrays)),
        in_specs=[pl.BlockSpec(memory_space=pl.ANY)] * n, out_specs=tuple([pl.BlockSpec(memory_space=pl.ANY)] * n),
        scratch_shapes=[pltpu.SemaphoreType.DMA((n, N_DEV - 1)), pltpu.SemaphoreType.DMA((n, N_DEV - 1)), pltpu.SemaphoreType.DMA((n,))],
    )(*arrays)


def _exchange(name, x, scatter):
    return _exchange_many(name, [x], scatter)[0]


HBM_SPEC = pl.BlockSpec(memory_space=pltpu.HBM)
SEM_SPEC = pl.BlockSpec(memory_space=pltpu.SEMAPHORE)
DATAFLOW = pltpu.SideEffectType.DATAFLOW_SIDE_EFFECTING


def _remote_copies(x_refs, land_refs, send_sems, recv_sems, scatter, numbers=None):
    me = _my_index()
    numbers = list(range(len(x_refs))) if numbers is None else numbers
    out, inc = [], []
    for a in range(len(x_refs)):
        for k in range(1, N_DEV):
            peer = _peer(k)
            pair = numbers[a] * (N_DEV - 1) + k - 1
            sems = dict(send_sem=send_sems.at[pair], recv_sem=recv_sems.at[pair], device_id=peer, device_id_type=MESH)
            out.append(pltpu.make_async_remote_copy(
                src_ref=x_refs[a].at[_index_of(peer)] if scatter else x_refs[a], dst_ref=land_refs[a].at[me], **sems))
            inc.append(pltpu.make_async_remote_copy(
                src_ref=x_refs[a].at[me] if scatter else x_refs[a], dst_ref=land_refs[a].at[_index_of(peer)], **sems))
    return out, inc


def _exchange_start(name, arrays, scatter):
    n = len(arrays)
    blocks = [tuple(a.shape[1:]) if scatter else tuple(a.shape) for a in arrays]

    def body(*refs):
        x_refs, land_refs = refs[:n], refs[n:2 * n]
        send_sems, recv_sems = refs[2 * n], refs[2 * n + 1]
        for cp in _remote_copies(x_refs, land_refs, send_sems, recv_sems, scatter)[0]:
            cp.start()
        refs[-1][...] = jnp.zeros_like(refs[-1])

    sem_type = pltpu.SemaphoreType.DMA((n * (N_DEV - 1),))
    lands =[pltpu.with_memory_space_constraint(lax.empty((N_DEV,) + blk, a.dtype), pltpu.HBM) for blk, a in zip(blocks, arrays)]
    srcs = [pltpu.with_memory_space_constraint(a, pltpu.HBM) for a in arrays]
    res = pl.pallas_call(
        body, name=name,
        out_shape=(sem_type, sem_type, *[pltpu.HBM(a.shape, a.dtype) for a in srcs + lands], jax.ShapeDtypeStruct((8, LANES), F32)),
        in_specs=[HBM_SPEC] * (2 * n), out_specs=(SEM_SPEC, SEM_SPEC, *[HBM_SPEC] * (2 * n), pl.BlockSpec(memory_space=pltpu.VMEM)),
        input_output_aliases={i: 2 + i for i in range(2 * n)},
        compiler_params=pltpu.CompilerParams(has_side_effects=DATAFLOW),
    )(*srcs, *lands)
    return (res[0], res[1], list(res[2:2 + n]), list(res[2 + n:2 + 2 * n])), res[-1]


def _exchange_wait(name, handles, after, scatter, which=None):
    send_sems, recv_sems, srcs, lands = handles
    which = list(range(len(srcs))) if which is None else list(which)
    srcs, lands = [srcs[a] for a in which], [lands[a] for a in which]
    n = len(srcs)

    def body(*refs):
        x_refs, land_refs = refs[:n], refs[n:2 * n]
        out, inc = _remote_copies(x_refs, land_refs, refs[2 * n], refs[2 * n + 1], scatter, which)
        for cp in out:
            cp.wait_send()
        for cp in inc:
            cp.wait_recv()

    res = pl.pallas_call(
        body, name=name, out_shape=tuple(pltpu.HBM(a.shape, a.dtype) for a in srcs + lands),
        in_specs=[HBM_SPEC] * (2 * n) + [SEM_SPEC, SEM_SPEC, pl.BlockSpec(memory_space=pl.ANY)], out_specs=tuple([HBM_SPEC] * (2 * n)),
        input_output_aliases={i: i for i in range(2 * n)},
        compiler_params=pltpu.CompilerParams(has_side_effects=DATAFLOW),
    )(*srcs, *lands, send_sems, recv_sems, after)
    return list(res[n:])


def _pack(arrays, dtype, row_multiple):
    flat = jnp.concatenate([a.astype(dtype).reshape(-1) for a in arrays])
    rows = -(-flat.shape[0] // (LANES * row_multiple)) * row_multiple
    return jnp.pad(flat, (0, rows * LANES - flat.shape[0])).reshape(rows, LANES)


def _unpack(packed, shapes):
    lead = packed.shape[:-2]
    flat = packed.reshape(lead + (-1,))
    out, off = [], 0
    for shp in shapes:
        size = 1
        for d in shp:
            size *= d
        out.append(flat[..., off:off + size].reshape(lead + tuple(shp)))
        off += size
    return out


def _unshard(g8, axis):
    return jnp.concatenate([g8[j] for j in range(N_DEV)], axis=axis)


def _shard8(full, axis):
    n = full.shape[axis] // N_DEV
    return jnp.stack([lax.slice_in_dim(full, j * n, (j + 1) * n, axis=axis) for j in range(N_DEV)])


VECTOR_WEIGHTS = (("pool_b", 1), ("pool_scale", 1), ("conv_w", 2))
REPLICATED_WEIGHTS = ("norm1_g", "norm2_g", "kv_in_g", "ckv_norm_g", "q_norm_g", "conv_b", "final_g")
WEIGHT_ORDER = ("mod_w", "mod_b", "norm1_g", "norm2_g", "pool_w", "pool_b", "pool_scale", "kv_in_g", "w_dkv", "ckv_norm_g", "w_uk",
                "w_uv", "w_dq", "q_norm_g", "w_uq", "w_o", "w_up", "conv_w", "conv_b", "w_down", "final_g")
BIG_ROW_MULTIPLE = 1024
SMALL_ROW_MULTIPLE = 16


def _as_2d(a):
    if a.ndim == 1:
        return a.reshape(-1, LANES)
    return a.reshape(-1, a.shape[-1])


def kernel(x, c, positions, mod_w, mod_b, norm1_g, norm2_g, pool_w, pool_b, pool_scale, kv_in_g, w_dkv, ckv_norm_g, w_uk, w_uv, w_dq, q_norm_g, w_uq, w_o, w_up, conv_w, conv_b, w_down, final_g, loss_target, m_mod_w, m_mod_b, m_norm1_g, m_norm2_g, m_pool_w, m_pool_b, m_pool_scale, m_kv_in_g, m_w_dkv, m_ckv_norm_g, m_w_uk, m_w_uv, m_w_dq, m_q_norm_g, m_w_uq, m_w_o, m_w_up, m_conv_w, m_conv_b, m_w_down, m_final_g, v_mod_w, v_mod_b, v_norm1_g, v_norm2_g, v_pool_w, v_pool_b, v_pool_scale, v_kv_in_g, v_w_dkv, v_ckv_norm_g, v_w_uk, v_w_uv, v_w_dq, v_q_norm_g, v_w_uq, v_w_o, v_w_up, v_conv_w, v_conv_b, v_w_down, v_final_g):
    shard = dict(mod_w=mod_w, mod_b=mod_b, norm1_g=norm1_g, norm2_g=norm2_g, pool_w=pool_w, pool_b=pool_b, pool_scale=pool_scale,
                 kv_in_g=kv_in_g, w_dkv=w_dkv, ckv_norm_g=ckv_norm_g, w_uk=w_uk, w_uv=w_uv, w_dq=w_dq, q_norm_g=q_norm_g, w_uq=w_uq,
                 w_o=w_o, w_up=w_up, conv_w=conv_w, conv_b=conv_b, w_down=w_down, final_g=final_g)
    mom_m = dict(mod_w=m_mod_w, mod_b=m_mod_b, norm1_g=m_norm1_g, norm2_g=m_norm2_g, pool_w=m_pool_w, pool_b=m_pool_b,
                 pool_scale=m_pool_scale, kv_in_g=m_kv_in_g, w_dkv=m_w_dkv, ckv_norm_g=m_ckv_norm_g, w_uk=m_w_uk, w_uv=m_w_uv,
                 w_dq=m_w_dq, q_norm_g=m_q_norm_g, w_uq=m_w_uq, w_o=m_w_o, w_up=m_w_up, conv_w=m_conv_w, conv_b=m_conv_b,
                 w_down=m_w_down, final_g=m_final_g)
    mom_v = dict(mod_w=v_mod_w, mod_b=v_mod_b, norm1_g=v_norm1_g, norm2_g=v_norm2_g, pool_w=v_pool_w, pool_b=v_pool_b,
                 pool_scale=v_pool_scale, kv_in_g=v_kv_in_g, w_dkv=v_w_dkv, ckv_norm_g=v_ckv_norm_g, w_uk=v_w_uk, w_uv=v_w_uv,
                 w_dq=v_w_dq, q_norm_g=v_q_norm_g, w_uq=v_w_uq, w_o=v_w_o, w_up=v_w_up, conv_w=v_conv_w, conv_b=v_conv_b,
                 w_down=v_w_down, final_g=v_final_g)
    me = _my_index()
    d6 = N_MOD * D_MODEL
    mod_cols = d6 // N_DEV

    small_in = [c] + [shard[k] for k, _ in VECTOR_WEIGHTS]
    small_all = _exchange("gather_vectors", _pack(small_in, F32, SMALL_ROW_MULTIPLE), scatter=False)
    parts = _unpack(small_all, [a.shape for a in small_in])
    c_all = jnp.pad(parts[0].reshape(N_DEV, D_MODEL), ((0, N_DEV), (0, 0)))
    vec = {k: _unshard(p, ax) for (k, ax), p in zip(VECTOR_WEIGHTS, parts[1:])}

    my_mod_b = lax.dynamic_slice_in_dim(mod_b, me * mod_cols, mod_cols, axis=1)
    mods_mine = _mods_fwd("mods_fwd", c_all, mod_w, my_mod_b)
    mods_all = _exchange("gather_mods", _pack([mods_mine], F32, SMALL_ROW_MULTIPLE), scatter=False)
    mods_all = _unpack(mods_all, [mods_mine.shape])[0]
    mods = lax.dynamic_index_in_dim(mods_all, me, axis=2, keepdims=False)
    mods = jnp.moveaxis(mods, 0, 1).reshape(DEPTH, d6)

    tabq, tabk = _rope_tables(positions[0])
    half = N_DEV // 2
    up_cols = shard["w_up"].shape[2]
    cat = lambda a, axis, lo=0, hi=N_DEV: jnp.concatenate([a[j] for j in range(lo, hi)], axis=axis)

    def stage_pieces(l):
        out = {"pool_w": shard["pool_w"].astype(BF16)} if l == 0 else {}
        if l == N_A_LAYERS:
            out.update({k: shard[k].astype(BF16) for k in ("w_dkv", "w_uk", "w_uv")})
        if l >= N_A_LAYERS:
            out.update({k: shard[k][l - N_A_LAYERS].astype(BF16) for k in ("w_dq", "w_uq", "w_o")})
        out.update(w_up=shard["w_up"][l].astype(BF16), w_down=shard["w_down"][l].astype(BF16))
        return out

    gathers, pool_all = {}, []

    def start_gather(l, behind=None):
        pieces = stage_pieces(l)
        if behind is not None:
            pieces, _ = lax.optimization_barrier((pieces, behind))
        handles, token = _exchange_start(f"gather_start_{l}", list(pieces.values()), scatter=False)
        gathers[l] = (handles, pieces)
        return token[0, 0]

    def wait_gather(l, keys, after, tag=""):
        handles, pieces = gathers[l]
        which = [list(pieces).index(k) for k in keys]
        lands = _exchange_wait(f"gather_wait_{l}{tag}", handles, after, scatter=False, which=which)
        return dict(zip(keys, own_slot(lands, [pieces[k] for k in keys])))

    def whole_weights(l, got):
        w = dict(norm1_g=norm1_g[l], norm2_g=norm2_g[l], conv_w=vec["conv_w"][l], conv_b=conv_b[l])
        if l == 0:
            pool_all.append(got["pool_w"])
        if l < N_A_LAYERS:
            w.update(pool_w=cat(pool_all[0][:, l], 1), pool_b=vec["pool_b"][l], pool_scale=vec["pool_scale"][l])
        else:
            rope = got["w_uq"][..., QK_NOPE:]
            ext = jnp.concatenate([got["w_uq"][..., :QK_NOPE], rope, _swap_halves(rope)], axis=-1)
            w.update(w_dq=got["w_dq"].reshape(D_MODEL, Q_RANK), w_uq_ext=cat(ext, -1), w_o=got["w_o"].reshape(D_MODEL, D_MODEL),
                     q_norm_g=q_norm_g[l - N_A_LAYERS])
        if l == N_A_LAYERS:
            w.update(w_dkv_ext=_extend_w_dkv(got["w_dkv"].reshape(D_MODEL, KV_RANK + QK_ROPE)), w_uk=cat(got["w_uk"], -1),
                     w_uv=cat(got["w_uv"], -1), kv_in_g=kv_in_g, ckv_norm_g=ckv_norm_g)
        return w

    def own_slot(lands, own):
        return [lax.dynamic_update_index_in_dim(p, o, me, 0) for p, o in zip(lands, own)]

    def fetch(l, after):
        up_parts = lambda g8: (cat(g8, -1, 0, half), cat(g8, -1, half, N_DEV))
        if l == 0:
            start_gather(0, behind=mods)
        first = [k for k in gathers[l][1] if k not in ("w_up", "w_down")]
        got = wait_gather(l, first, mods if l == 0 else after, "_mix") if first else {}
        w_up = lambda aft: up_parts(wait_gather(l, ["w_up"], aft, "_up")["w_up"])
        w_down = lambda aft: wait_gather(l, ["w_down"], aft, "_down")["w_down"].reshape(D_FF, D_MODEL)
        w = dict(whole_weights(l, got), w_up=w_up, w_down=w_down)
        return w, (start_gather(l + 1) if l + 1 < DEPTH else 0.0)

    scatters, pending, pool_grads, piece_grads = {}, {}, {}, {}

    def reduce_pieces(l, keys, got):
        for k, p in zip(keys, got):
            piece_grads[(k, l)] = _sum8(f"sum_grads_{k}_{l}", p.reshape(N_DEV, -1, p.shape[-1])).reshape(p.shape[1:])

    def start_scatter(name, sent):
        sent = {k: a.astype(BF16) for k, a in sent.items()}
        handles, token = _exchange_start(f"scatter_start_{name}", list(sent.values()), scatter=True)
        scatters[name] = (handles, list(sent), [lax.dynamic_index_in_dim(a, me, 0, keepdims=False) for a in sent.values()])
        return token[0, 0]

    def finish_scatter(name, l, after):
        handles, keys, own = scatters.pop(name)
        reduce_pieces(l, keys, own_slot(_exchange_wait(f"scatter_wait_{name}", handles, after, scatter=True), own))

    def push(l, part, big, after):
        cut = lambda a, n, axis: jnp.stack([lax.slice_in_dim(a, j * n, (j + 1) * n, axis=axis) for j in range(N_DEV)])
        sent = {}
        if part == "down":
            sent["w_down"] = big["w_down"].reshape(N_DEV, D_FF // N_DEV, D_MODEL)
        elif part == "up":
            sent["w_up"] = jnp.stack([lax.slice_in_dim(big[half_], j * up_cols, (j + 1) * up_cols, axis=1)
                                      for half_ in ("w_up_a", "w_up_v") for j in range(half)])
        elif l < N_A_LAYERS:
            pool_grads[l] = big["pool_w"]
        else:
            ext = cut(big["w_uq_ext"], Q_EXT, 1)
            rope = ext[..., QK_NOPE:QK_HEAD] + _unswap_halves(ext[..., QK_HEAD:])
            sent.update(w_dq=big["w_dq"].reshape(N_DEV, D_MODEL // N_DEV, Q_RANK), w_uq=jnp.concatenate([ext[..., :QK_NOPE], rope], axis=-1),
                        w_o=big["w_o"].reshape(N_DEV, D_MODEL // N_DEV, D_MODEL))
        if part == "mix" and l == N_A_LAYERS:
            sent.update(w_dkv=_fold_w_dkv_grad(big["w_dkv_ext"]).reshape(N_DEV, D_MODEL // N_DEV, KV_RANK + QK_ROPE),
                        w_uk=cut(big["w_uk"], QK_NOPE, 1), w_uv=cut(big["w_uv"], V_HEAD, 1))
        if l == 0 and part != "mix":
            return start_scatter(f"0_{part}", sent)
        if l == 0:
            finish_scatter("1", 1, after)
            pool = _shard8(jnp.stack([pool_grads[a] for a in range(N_A_LAYERS)]), 2).astype(BF16)
            reduce_pieces(0, ["pool_w"], _exchange_many("scatter_pool_grads", [pool], scatter=True))
            return 0.0
        pending.setdefault(l, {}).update(sent)
        if part != "mix":
            return 0.0
        if l + 1 < DEPTH:
            finish_scatter(str(l + 1), l + 1, after)
        return start_scatter(str(l), pending.pop(l))

    loss_row, dx, g, dmods = _forward_backward(x[0], loss_target[0], mods, tabq, tabk, final_g, fetch, push)
    layers_of = lambda k, ls: jnp.stack([piece_grads[(k, l)] for l in ls])
    grads = dict(w_dkv=piece_grads[("w_dkv", N_A_LAYERS)], w_uk=piece_grads[("w_uk", N_A_LAYERS)], w_uv=piece_grads[("w_uv", N_A_LAYERS)])
    for k in ("w_dq", "w_uq", "w_o"):
        grads[k] = layers_of(k, range(N_A_LAYERS, DEPTH))

    small_names = REPLICATED_WEIGHTS + tuple(k for k, _ in VECTOR_WEIGHTS)
    small_out = [dmods] + [g[k] for k in small_names] + [loss_row]
    small_shapes = [a.shape for a in small_out]
    small_got = _exchange("gather_small_grads", _pack(small_out, F32, SMALL_ROW_MULTIPLE), scatter=False)
    summed = _unpack(_sum8("sum_small_grads", small_got), small_shapes)
    grads["mod_b"] = summed[0]
    for k, s in zip(small_names, summed[1:-1]):
        grads[k] = s
    for k, ax in VECTOR_WEIGHTS:
        n = shard[k].shape[ax]
        grads[k] = lax.dynamic_slice_in_dim(grads[k], me * n, n, axis=ax)
    loss = summed[-1][0, 0]
    dmods_all = _unpack(small_got, small_shapes)[0]
    dm_mine = lax.dynamic_slice_in_dim(dmods_all, me * mod_cols, mod_cols, axis=2)
    dm_mine = jnp.pad(jnp.moveaxis(dm_mine, 0, 1), ((0, 0), (0, N_DEV), (0, 0)))
    grads["mod_w"] = _mods_bwd("mods_bwd", c_all, dm_mine)

    delta, new_m, new_v = {}, {}, {}

    def adamw(k):
        shp = shard[k].shape
        grads[k] = grads[k].reshape(shp)
        d_, m_, v_ = _adamw(f"adamw_{k}", _as_2d(shard[k]), _as_2d(grads[k]), _as_2d(mom_m[k]), _as_2d(mom_v[k]))
        delta[k], new_m[k], new_v[k] = d_.reshape(shp), m_.reshape(shp), v_.reshape(shp)

    late = ("w_up", "w_down", "pool_w")
    for k in WEIGHT_ORDER:
        if k not in late:
            adamw(k)
    finish_scatter("0_down", 0, delta["final_g"])
    finish_scatter("0_up", 0, delta["final_g"])
    grads.update(w_up=layers_of("w_up", range(DEPTH)), w_down=layers_of("w_down", range(DEPTH)), pool_w=piece_grads[("pool_w", 0)])
    for k in late:
        adamw(k)
    return (loss, dx[None], *[grads[k] for k in WEIGHT_ORDER], *[delta[k] for k in WEIGHT_ORDER],
            *[new_m[k] for k in WEIGHT_ORDER], *[new_v[k] for k in WEIGHT_ORDER])
```

```python
import functools

import jax
import jax.numpy as jnp
from jax import lax
from jax.experimental import pallas as pl
from jax.experimental.pallas import tpu as pltpu

F32 = jnp.float32
BF16 = jnp.bfloat16

D_MODEL = 1024
DEPTH = 4
N_A_LAYERS = 2
N_B_LAYERS = 2
POOL_WINDOWS = (2, 4, 8, 16)
POOL_GROUP = 256
N_HEADS = 8
QK_NOPE = 128
QK_ROPE = 64
V_HEAD = 128
QK_HEAD = QK_NOPE + QK_ROPE
Q_RANK = 384
KV_RANK = 256
ROPE_THETA = 10000.0
D_FF = 2816
EPS = 1e-6
N_MOD = 6
ADAM_LR = 0.001
ADAM_B1 = 0.9
ADAM_B2 = 0.999
ADAM_EPS = 1e-08
ADAM_WD = 0.01
ADAM_STEP = 10

N_DEV = 8
LANES = 128
Q_EXT = 256
VMEM_LIMIT_BYTES = 48 * 1024 * 1024
MESH = pl.DeviceIdType.MESH
NEG_BIG = -0.7 * float(jnp.finfo(jnp.float32).max)


def _params(sem):
    return pltpu.CompilerParams(dimension_semantics=sem, vmem_limit_bytes=VMEM_LIMIT_BYTES)


def _tile(n, cap):
    if n <= cap:
        return n
    best = None
    for d in range(LANES, cap + 1, LANES):
        if n % d == 0:
            best = d
    assert best is not None, (n, cap)
    return best


def _dot(a, b, dims):
    return lax.dot_general(a, b, (dims, ((), ())), preferred_element_type=F32)


NN = ((1,), (0,))
NT = ((1,), (1,))
TN = ((0,), (0,))


def _mm(name, a, b, mode="nn", out_dtype=BF16, add=None, resid=None, gate=None, rowtab=None,
        tm_cap=1024, tn_cap=1408, tk_cap=1408):
    if mode == "tn":
        kdim, m = a.shape
    else:
        m, kdim = a.shape
    n = b.shape[0] if mode == "nt" else b.shape[1]
    tm, tn, tk = _tile(m, tm_cap), _tile(n, tn_cap), _tile(kdim, tk_cap)
    nk = kdim // tk
    dims = {"nn": NN, "nt": NT, "tn": TN}[mode]
    a_spec = pl.BlockSpec((tk, tm), lambda i, j, k: (k, i)) if mode == "tn" else pl.BlockSpec((tm, tk), lambda i, j, k: (i, k))
    b_spec = pl.BlockSpec((tn, tk), lambda i, j, k: (j, k)) if mode == "nt" else pl.BlockSpec((tk, tn), lambda i, j, k: (k, j))
    o_spec = pl.BlockSpec((tm, tn), lambda i, j, k: (i, j))
    g_spec = pl.BlockSpec((1, tn), lambda i, j, k: (0, j))
    gated = resid is not None

    def body(*refs):
        a_ref, b_ref = refs[0], refs[1]
        acc = refs[-1]
        k = pl.program_id(2)

        @pl.when(k == 0)
        def _():
            acc[...] = jnp.zeros_like(acc)

        acc[...] += _dot(a_ref[...].astype(BF16), b_ref[...].astype(BF16), dims)

        @pl.when(k == nk - 1)
        def _():
            if gated:
                r_ref, g_ref, y_ref, x_ref = refs[2:6]
                y_ref[...] = acc[...]
                x_ref[...] = r_ref[...] + g_ref[...] * acc[...]
            elif add is not None:
                refs[3][...] = (acc[...] + refs[2][...].astype(F32)).astype(out_dtype)
            elif rowtab is not None:
                tab = refs[2][...]
                refs[3][...] = (acc[...] * jnp.concatenate([tab] * (tn // tab.shape[1]), axis=1)).astype(out_dtype)
            else:
                refs[2][...] = acc[...].astype(out_dtype)

    ins, in_specs = [a, b], [a_spec, b_spec]
    if rowtab is not None:
        assert tn % rowtab.shape[1] == 0 and not gated and add is None
        ins.append(rowtab)
        in_specs.append(pl.BlockSpec((tm, rowtab.shape[1]), lambda i, j, k: (i, 0)))
    if gated:
        ins += [resid, gate]
        in_specs += [o_spec, g_spec]
        out_shape = (jax.ShapeDtypeStruct((m, n), F32), jax.ShapeDtypeStruct((m, n), F32))
        out_specs = (o_spec, o_spec)
    else:
        if add is not None:
            ins.append(add)
            in_specs.append(o_spec)
        out_shape = jax.ShapeDtypeStruct((m, n), out_dtype)
        out_specs = o_spec
    return pl.pallas_call(
        body, name=name, grid=(m // tm, n // tn, nk), in_specs=in_specs, out_specs=out_specs, out_shape=out_shape,
        scratch_shapes=[pltpu.VMEM((tm, tn), F32)],
        compiler_params=_params(("parallel", "parallel", "arbitrary")),
    )(*ins)


def _rowwise(name, fn, tiled, bcast, outs, sums=(), tr=512):
    tiled = [t if isinstance(t, tuple) else (t, t.shape[1], 0) for t in tiled]
    s = tiled[0][0].shape[0]
    tr = min(tr, s)
    assert s % tr == 0
    n_t, n_b, n_o = len(tiled), len(bcast), len(outs)

    def body(*refs):
        i = pl.program_id(0)
        vals = [r[...] for r in refs[:n_t + n_b]]
        o_vals, s_vals = fn(*vals)
        for r, v in zip(refs[n_t + n_b:n_t + n_b + n_o], o_vals):
            r[...] = v.astype(r.dtype)
        s_refs = refs[n_t + n_b + n_o:]

        @pl.when(i == 0)
        def _():
            for r in s_refs:
                r[...] = jnp.zeros_like(r)

        for r, v in zip(s_refs, s_vals):
            r[...] += v

    in_specs = [pl.BlockSpec((tr, n), functools.partial(lambda cb, i: (i, cb), cb)) for (_, n, cb) in tiled]
    in_specs += [pl.BlockSpec(b.shape, functools.partial(lambda nd, i: (0,) * nd, b.ndim)) for b in bcast]
    out_specs = [pl.BlockSpec((tr, n), lambda i: (i, 0)) for (n, _) in outs]
    out_specs += [pl.BlockSpec((1, n), lambda i: (0, 0)) for n in sums]
    out_shape = [jax.ShapeDtypeStruct((s, n), dt) for (n, dt) in outs]
    out_shape += [jax.ShapeDtypeStruct((1, n), F32) for n in sums]
    res = pl.pallas_call(
        body, name=name, grid=(s // tr,), in_specs=in_specs, out_specs=tuple(out_specs), out_shape=tuple(out_shape),
        compiler_params=_params(("arbitrary",)),
    )(*[t[0] for t in tiled], *bcast)
    return res


def _colsum(v):
    return jnp.sum(v, axis=0, keepdims=True)


def _rms_fwd(name, x, g, scale=None, shift=None, out_dtype=BF16, ncols=None):
    mod = scale is not None

    def fn(xv, gv, *ss):
        y = xv * lax.rsqrt(jnp.mean(xv * xv, axis=-1, keepdims=True) + EPS) * gv
        if mod:
            y = y * (1.0 + ss[0]) + ss[1]
        return (y,), ()

    n = ncols or x.shape[1]
    return _rowwise(name, fn, [(x, n, 0)], [g] + ([scale, shift] if mod else []), [(n, out_dtype)])[0]


def _rms_bwd(name, x, g, dh, scale=None, dx_in=None, ncols=None, out_dtype=F32):
    mod = scale is not None
    has_in = dx_in is not None

    def fn(*vals):
        xv, dhv = vals[0], vals[1].astype(F32)
        rest = list(vals[2:])
        dxi = rest.pop(0) if has_in else None
        gv = rest.pop(0)
        rstd = lax.rsqrt(jnp.mean(xv * xv, axis=-1, keepdims=True) + EPS)
        xhat = xv * rstd
        sums = []
        if mod:
            sc = rest.pop(0)
            dyn = dhv * (1.0 + sc)
            dshift, dscale = _colsum(dhv), _colsum(dhv * (xhat * gv))
        else:
            dyn = dhv
        dg = _colsum(dyn * xhat)
        dxhat = dyn * gv
        dx = rstd * (dxhat - xhat * jnp.mean(dxhat * xhat, axis=-1, keepdims=True))
        if has_in:
            dx = dx + dxi
        sums = [dg] + ([dshift, dscale] if mod else [])
        return (dx,), sums

    n = ncols or x.shape[1]
    tiled = [(x, n, 0), dh] + ([dx_in] if has_in else [])
    return _rowwise(name, fn, tiled, [g] + ([scale] if mod else []), [(n, out_dtype)], [n] * (3 if mod else 1))


def _gate_bwd(name, dxn, y, g):
    def fn(dv, yv, gv):
        return (gv * dv,), (_colsum(dv * yv),)

    n = dxn.shape[1]
    return _rowwise(name, fn, [dxn, y], [g], [(n, BF16)], [n])


def _loss_head(name, x, g, target):
    n = x.shape[1]

    def fn(xv, tv, gv):
        rstd = lax.rsqrt(jnp.mean(xv * xv, axis=-1, keepdims=True) + EPS)
        xhat = xv * rstd
        err = xhat * gv - tv
        loss = 0.5 * jnp.sum(jnp.sum(err * err, axis=-1, keepdims=True) / n, axis=0, keepdims=True)
        dy = err / n
        dg = _colsum(dy * xhat)
        dxhat = dy * gv
        dx = rstd * (dxhat - xhat * jnp.mean(dxhat * xhat, axis=-1, keepdims=True))
        return (dx,), (dg, jnp.broadcast_to(loss, (1, LANES)))

    return _rowwise(name, fn, [x, target], [g], [(n, F32)], [n, LANES])


def _krope_fwd(name, kv_ext, tabk):
    def fn(xv, tv):
        t = xv * tv
        return (t + pltpu.roll(t, 64, 1),), ()

    return _rowwise(name, fn, [(kv_ext, LANES, 2), tabk], [], [(LANES, BF16)])[0]


def _krope_bwd(name, dkd, tabk):
    def fn(dv, tv):
        return ((dv + pltpu.roll(dv, 64, 1)) * tv,), ()

    return _rowwise(name, fn, [dkd, tabk], [], [(LANES, F32)])[0]


def _adamw(name, w, g, m, v):
    def fn(wv, gv, mv, vv):
        m2 = ADAM_B1 * mv + (1.0 - ADAM_B1) * gv
        v2 = ADAM_B2 * vv + (1.0 - ADAM_B2) * (gv * gv)
        m_hat = m2 / (1.0 - ADAM_B1 ** ADAM_STEP)
        v_hat = v2 / (1.0 - ADAM_B2 ** ADAM_STEP)
        delta = -ADAM_LR * (m_hat / (jnp.sqrt(v_hat) + ADAM_EPS) + ADAM_WD * wv)
        return (delta, m2, v2), ()

    r, c = w.shape
    tr = r
    for cand in (512, 256, 128, 64, 32, 16, 8):
        if r % cand == 0 and r > cand:
            tr = cand
            break
    return _rowwise(name, fn, [w, g, m, v], [], [(c, F32)] * 3, tr=tr)


def _sum8(name, parts):
    _, r, c = parts.shape
    tr = r
    for cand in (2048, 1024, 512, 256, 128, 64, 32, 16):
        if r % cand == 0 and r > cand and cand * c <= 256 * 1024:
            tr = cand
            break

    def body(p_ref, o_ref):
        acc = p_ref[0].astype(F32)
        for k in range(1, N_DEV):
            acc = acc + p_ref[k].astype(F32)
        o_ref[...] = acc

    return pl.pallas_call(
        body, name=name, grid=(r // tr,), in_specs=[pl.BlockSpec((N_DEV, tr, c), lambda i: (0, i, 0))],
        out_specs=pl.BlockSpec((tr, c), lambda i: (i, 0)), out_shape=jax.ShapeDtypeStruct((r, c), F32),
        compiler_params=_params(("parallel",)),
    )(parts)


def _mods_fwd(name, c_all, w, b):
    depth, d, n = w.shape

    def body(c_ref, w_ref, b_ref, o_ref):
        cv = c_ref[...]
        sc = (cv * (1.0 / (1.0 + jnp.exp(-cv)))).astype(BF16)
        o_ref[0] = _dot(sc, w_ref[0].astype(BF16), NN) + b_ref[0]

    return pl.pallas_call(
        body, name=name, grid=(depth,),
        in_specs=[pl.BlockSpec(c_all.shape, lambda l: (0, 0)), pl.BlockSpec((1, d, n), lambda l: (l, 0, 0)),
                  pl.BlockSpec((1, 1, n), lambda l: (l, 0, 0))],
        out_specs=pl.BlockSpec((1, c_all.shape[0], n), lambda l: (l, 0, 0)),
        out_shape=jax.ShapeDtypeStruct((depth, c_all.shape[0], n), F32),
        compiler_params=_params(("parallel",)),
    )(c_all, w, b.reshape(depth, 1, n))


def _mods_bwd(name, c_all, dm):
    depth, rows, n = dm.shape
    d = c_all.shape[1]

    def body(c_ref, dm_ref, o_ref):
        cv = c_ref[...]
        sc = (cv * (1.0 / (1.0 + jnp.exp(-cv)))).astype(BF16)
        o_ref[0] = _dot(sc, dm_ref[0].astype(BF16), TN)

    return pl.pallas_call(
        body, name=name, grid=(depth,),
        in_specs=[pl.BlockSpec(c_all.shape, lambda l: (0, 0)), pl.BlockSpec((1, rows, n), lambda l: (l, 0, 0))],
        out_specs=pl.BlockSpec((1, d, n), lambda l: (l, 0, 0)),
        out_shape=jax.ShapeDtypeStruct((depth, d, n), F32),
        compiler_params=_params(("parallel",)),
    )(c_all, dm)


POOL_TILE = 256


def _split_dot(band, val):
    hi = val.astype(BF16)
    lo = (val - hi.astype(F32)).astype(BF16)
    return _dot(band, hi, NN) + _dot(band, lo, NN)


def _pool_fwd(name, h1, x, pw, pb, ps, g1):
    s, d = h1.shape
    t = POOL_TILE

    def body(hc_ref, hp_ref, x_ref, pw_ref, pb_ref, ps_ref, g_ref, xo_ref, zb_ref, pooled_ref):
        i = pl.program_id(0)
        r = lax.broadcasted_iota(jnp.int32, (t, t), 0)
        j = lax.broadcasted_iota(jnp.int32, (t, t), 1)
        pos = (i * t + lax.broadcasted_iota(jnp.int32, (t, 1), 0) + 1).astype(F32)
        has_prev = (i > 0).astype(F32)
        for grp, w in enumerate(POOL_WINDOWS):
            cs = slice(grp * POOL_GROUP, (grp + 1) * POOL_GROUP)
            hc = hc_ref[:, cs]
            band_cur = ((r - j >= 0) & (r - j < w)).astype(BF16)
            band_prev = (r + t - j < w).astype(BF16)
            ssum = _split_dot(band_cur, hc) + has_prev * _split_dot(band_prev, hp_ref[:, cs])
            pooled = (ssum / jnp.minimum(pos, float(w)) - hc).astype(BF16)
            zb = _dot(pooled, pw_ref[grp], NN) + pb_ref[:, cs]
            xo_ref[:, cs] = x_ref[:, cs] + g_ref[:, cs] * (zb * ps_ref[:, cs])
            zb_ref[:, cs] = zb
            pooled_ref[:, cs] = pooled

    row = pl.BlockSpec((t, d), lambda i: (i, 0))
    vec = pl.BlockSpec((1, d), lambda i: (0, 0))
    return pl.pallas_call(
        body, name=name, grid=(s // t,),
        in_specs=[row, pl.BlockSpec((t, d), lambda i: (jnp.maximum(i - 1, 0), 0)), row,
                  pl.BlockSpec(pw.shape, lambda i: (0, 0, 0)), vec, vec, vec],
        out_specs=(row, row, row),
        out_shape=(jax.ShapeDtypeStruct((s, d), F32), jax.ShapeDtypeStruct((s, d), F32), jax.ShapeDtypeStruct((s, d), BF16)),
        compiler_params=_params(("parallel",)),
    )(h1, h1, x, pw, pb, ps, g1)


def _pool_bwd(name, dxn, zb, pooled, pw, ps, g1):
    s, d = dxn.shape
    t = POOL_TILE
    nt = s // t

    def body(dc_ref, dn_ref, zb_ref, pooled_ref, pw_ref, ps_ref, g_ref, dh_ref, dpw_ref, dpb_ref, dps_ref, dg_ref):
        i = pl.program_id(0)

        @pl.when(i == 0)
        def _():
            dpw_ref[...] = jnp.zeros_like(dpw_ref)
            dpb_ref[...] = jnp.zeros_like(dpb_ref)
            dps_ref[...] = jnp.zeros_like(dps_ref)
            dg_ref[...] = jnp.zeros_like(dg_ref)

        jj = lax.broadcasted_iota(jnp.int32, (t, t), 0)
        rr = lax.broadcasted_iota(jnp.int32, (t, t), 1)
        pos = (i * t + lax.broadcasted_iota(jnp.int32, (t, 1), 0) + 1).astype(F32)
        has_next = (i < nt - 1).astype(F32)
        for grp, w in enumerate(POOL_WINDOWS):
            cs = slice(grp * POOL_GROUP, (grp + 1) * POOL_GROUP)
            gv, psv, zbv, dxc = g_ref[:, cs], ps_ref[:, cs], zb_ref[:, cs], dc_ref[:, cs]
            dg_ref[:, cs] += _colsum(dxc * (zbv * psv))
            dy = gv * dxc
            dps_ref[:, cs] += _colsum(dy * zbv)
            dz = dy * psv
            dpb_ref[:, cs] += _colsum(dz)
            dzb = dz.astype(BF16)
            dpw_ref[grp] += _dot(pooled_ref[:, cs], dzb, TN)
            dp = _dot(dzb, pw_ref[grp], NT)
            dzn = (gv * dn_ref[:, cs] * psv).astype(BF16)
            dpn = _dot(dzn, pw_ref[grp], NT) * (has_next / float(w))
            band_cur = ((rr - jj >= 0) & (rr - jj < w)).astype(BF16)
            band_next = (rr + t - jj < w).astype(BF16)
            dh_ref[:, cs] = _split_dot(band_cur, dp / jnp.minimum(pos, float(w))) + _split_dot(band_next, dpn) - dp

    row = pl.BlockSpec((t, d), lambda i: (i, 0))
    vec = pl.BlockSpec((1, d), lambda i: (0, 0))
    wspec = pl.BlockSpec(pw.shape, lambda i: (0, 0, 0))
    return pl.pallas_call(
        body, name=name, grid=(nt,),
        in_specs=[row, pl.BlockSpec((t, d), lambda i: (jnp.minimum(i + 1, nt - 1), 0)), row, row, wspec, vec, vec],
        out_specs=(row, wspec, vec, vec, vec),
        out_shape=(jax.ShapeDtypeStruct((s, d), F32), jax.ShapeDtypeStruct(pw.shape, F32),
                   jax.ShapeDtypeStruct((1, d), F32), jax.ShapeDtypeStruct((1, d), F32), jax.ShapeDtypeStruct((1, d), F32)),
        compiler_params=_params(("arbitrary",)),
    )(dxn, dxn, zb, pooled, pw, ps, g1)


GLU_TILE = 512
HALO = 16
INV_SQRT2 = 0.7071067811865476
INV_SQRT_2PI = 0.3989422804014327


def _up_glu_fwd(name, h2, wa, wv, cw, cb):
    s, d = h2.shape
    f = wa.shape[1]
    tm, tn = _tile(s, 1024), _tile(f, 1408)

    def body(h_ref, hh_ref, wa_ref, wv_ref, cw_ref, cb_ref, ua_ref, gl_ref, gpv_ref, ge_ref):
        i = pl.program_id(1)
        has_prev = (i > 0).astype(F32)
        a = _dot(h_ref[...], wa_ref[...], NN).astype(BF16)
        v = _dot(h_ref[...], wv_ref[...], NN)
        above = (_dot(hh_ref[...], wa_ref[...], NN) * has_prev).astype(BF16)
        ua_ref[...] = a
        ext = jnp.concatenate([above.astype(F32), a.astype(F32)], axis=0)
        e1 = pltpu.roll(ext, 1, 0)[HALO:]
        e2 = pltpu.roll(ext, 2, 0)[HALO:]
        pre = e2 * cw_ref[0:1, :] + e1 * cw_ref[1:2, :] + ext[HALO:] * cw_ref[2:3, :] + cb_ref[...]
        cdf = 0.5 * (1.0 + lax.erf(pre * INV_SQRT2))
        ge = pre * cdf
        gl_ref[...] = (ge * v).astype(gl_ref.dtype)
        gpv_ref[...] = ((cdf + pre * (INV_SQRT_2PI * jnp.exp(-0.5 * pre * pre))) * v).astype(gpv_ref.dtype)
        ge_ref[...] = ge.astype(ge_ref.dtype)

    blk = pl.BlockSpec((tm, tn), lambda j, i: (i, j))
    wspec = pl.BlockSpec((d, tn), lambda j, i: (0, j))
    return pl.pallas_call(
        body, name=name, grid=(f // tn, s // tm),
        in_specs=[pl.BlockSpec((tm, d), lambda j, i: (i, 0)), pl.BlockSpec((HALO, d), lambda j, i: (jnp.maximum(i * (tm // HALO) - 1, 0), 0)),
                  wspec, wspec, pl.BlockSpec((3, tn), lambda j, i: (0, j)), pl.BlockSpec((1, tn), lambda j, i: (0, j))],
        out_specs=(blk, blk, blk, blk), out_shape=tuple(jax.ShapeDtypeStruct((s, f), BF16) for _ in range(4)),
        compiler_params=_params(("parallel", "parallel")),
    )(h2, h2, wa, wv, cw, cb)


def _down_glu_bwd(name, dy2, wd, ua, gpv, ge, cw):
    s, f = ua.shape
    d = dy2.shape[1]
    t, tf = min(GLU_TILE, s), _tile(f, 1408)
    nt = s // t
    te = t + HALO

    def body(dy_ref, dyn_ref, wd_ref, a_ref, ah_ref, g_ref, gn_ref, ge_ref, cw_ref, da_ref, dv_ref, dcw_ref, dcb_ref):
        i = pl.program_id(1)

        @pl.when(i == 0)
        def _():
            dcw_ref[...] = jnp.zeros_like(dcw_ref)
            dcb_ref[...] = jnp.zeros_like(dcb_ref)

        has_prev = (i > 0).astype(F32)
        has_next = (i < nt - 1).astype(F32)
        wdv = wd_ref[...]
        dgl = _dot(dy_ref[...], wdv, NT)
        dgl_below = _dot(dyn_ref[...], wdv, NT) * has_next
        dpre = jnp.concatenate([dgl * g_ref[...].astype(F32), dgl_below * gn_ref[...].astype(F32)], axis=0)
        c0, c1, c2 = cw_ref[0:1, :], cw_ref[1:2, :], cw_ref[2:3, :]
        up1 = pltpu.roll(dpre, te - 1, 0)
        up2 = pltpu.roll(dpre, te - 2, 0)
        da_ref[...] = (dpre * c2 + up1 * c1 + up2 * c0)[:t].astype(da_ref.dtype)
        dv_ref[...] = (dgl * ge_ref[...].astype(F32)).astype(dv_ref.dtype)
        ext = jnp.concatenate([ah_ref[...].astype(F32) * has_prev, a_ref[...].astype(F32)], axis=0)
        dpt = dpre[:t]
        dcb_ref[...] += _colsum(dpt)
        dcw_ref[0:1, :] += _colsum(pltpu.roll(ext, 2, 0)[HALO:] * dpt)
        dcw_ref[1:2, :] += _colsum(pltpu.roll(ext, 1, 0)[HALO:] * dpt)
        dcw_ref[2:3, :] += _colsum(ext[HALO:] * dpt)

    blk = pl.BlockSpec((t, tf), lambda j, i: (i, j))
    prev = pl.BlockSpec((HALO, tf), lambda j, i: (jnp.maximum(i * (t // HALO) - 1, 0), j))
    below = lambda i: jnp.minimum((i + 1) * (t // HALO), s // HALO - 1)
    w3 = pl.BlockSpec((3, tf), lambda j, i: (0, j))
    w1 = pl.BlockSpec((1, tf), lambda j, i: (0, j))
    return pl.pallas_call(
        body, name=name, grid=(f // tf, nt),
        in_specs=[pl.BlockSpec((t, d), lambda j, i: (i, 0)), pl.BlockSpec((HALO, d), lambda j, i: (below(i), 0)),
                  pl.BlockSpec((tf, d), lambda j, i: (j, 0)), blk, prev, blk, pl.BlockSpec((HALO, tf), lambda j, i: (below(i), j)), blk, w3],
        out_specs=(blk, blk, w3, w1),
        out_shape=(jax.ShapeDtypeStruct((s, f), BF16), jax.ShapeDtypeStruct((s, f), BF16),
                   jax.ShapeDtypeStruct((3, f), F32), jax.ShapeDtypeStruct((1, f), F32)),
        compiler_params=_params(("parallel", "arbitrary")),
    )(dy2, dy2, wd, ua, ua, gpv, gpv, ge, cw)


ATT_TILE = 512
ATT_ROWS = 256
LOG2E = 1.4426950408889634
LN2 = 0.6931471805599453


def _head_blocks_t(a, width):
    s = a.shape[0]
    t = min(ATT_TILE, s)
    return a.reshape(s // t, t, N_HEADS, width).transpose(2, 0, 3, 1)


def _causal_mask(sv, q0, k0):
    row = q0 + lax.broadcasted_iota(jnp.int32, sv.shape, 0)
    col = k0 + lax.broadcasted_iota(jnp.int32, sv.shape, 1)
    return jnp.where(col <= row, sv, NEG_BIG)


def _attn_fwd(name, q_rot, kt4, v_ext):
    s = q_rot.shape[0]
    t = min(ATT_TILE, s)
    nq = s // t

    rq = min(ATT_ROWS, t)

    def body(q_ref, kt_ref, v_ref, o_ref, lse_ref, row_ref, acc_ref, m_ref):
        qi = pl.program_id(1)
        acc_ref[...] = jnp.zeros_like(acc_ref)
        m_ref[...] = jnp.full_like(m_ref, NEG_BIG)

        def step(j, masked):
            v_blk = v_ref[pl.ds(pl.multiple_of(j * t, t), t), :]
            for r in range(t // rq):
                rs = pl.ds(r * rq, rq)
                sv = _dot(q_ref[rs, :], kt_ref[0, j], NN)
                if masked:
                    sv = _causal_mask(sv, r * rq, 0)
                m_prev = m_ref[rs, :]
                m_new = jnp.maximum(m_prev, jnp.max(sv, axis=-1, keepdims=True))
                p = jnp.exp2(sv - m_new).astype(BF16)
                acc_ref[rs, :] = jnp.exp2(m_prev - m_new) * acc_ref[rs, :] + _dot(p, v_blk, NN)
                m_ref[rs, :] = m_new

        def full_step(j, carry):
            step(j, False)
            return carry

        lax.fori_loop(0, qi, full_step, 0)
        step(qi, True)
        l = acc_ref[:, V_HEAD:V_HEAD + 1]
        o_ref[...] = (acc_ref[:, :V_HEAD] / l).astype(o_ref.dtype)
        lse = jnp.broadcast_to(m_ref[...] + jnp.log(l) * LOG2E, lse_ref.shape)
        lse_ref[...] = lse
        row_ref[0, 0] = jnp.transpose(lse)[0:8, :]

    head_q = pl.BlockSpec((t, Q_EXT), lambda h, i: (i, h))
    head_o = pl.BlockSpec((t, V_HEAD), lambda h, i: (i, h))
    return pl.pallas_call(
        body, name=name, grid=(N_HEADS, nq),
        in_specs=[head_q, pl.BlockSpec((1, nq, Q_EXT, t), lambda h, i: (h, 0, 0, 0)), pl.BlockSpec((s, Q_EXT), lambda h, i: (0, h))],
        out_specs=(head_o, head_o, pl.BlockSpec((1, 1, 8, t), lambda h, i: (h, i, 0, 0))),
        out_shape=(jax.ShapeDtypeStruct((s, N_HEADS * V_HEAD), BF16), jax.ShapeDtypeStruct((s, N_HEADS * LANES), F32),
                   jax.ShapeDtypeStruct((N_HEADS, nq, 8, t), F32)),
        scratch_shapes=[pltpu.VMEM((t, Q_EXT), F32), pltpu.VMEM((t, 1), F32)],
        compiler_params=_params(("parallel", "parallel")),
    )(q_rot, kt4, v_ext)


def _attn_dq(name, q_rot, tabq, kt4, kfull, vt4, o, lse, do):
    s = q_rot.shape[0]
    t = min(ATT_TILE, s)
    nq = s // t

    def body(q_ref, tab_ref, kt_ref, k_ref, vt_ref, o_ref, lse_ref, do_ref, dq_ref, delta_ref, acc_ref):
        qi = pl.program_id(1)
        q = q_ref[...]
        dov = do_ref[...]
        delta = jnp.sum(dov.astype(F32) * o_ref[...].astype(F32), axis=-1, keepdims=True)
        lse = lse_ref[:, 0:1]
        acc_ref[...] = jnp.zeros_like(acc_ref)

        def step(j, masked):
            sv = _dot(q, kt_ref[0, j], NN)
            if masked:
                sv = _causal_mask(sv, qi * t, j * t)
            p = jnp.exp2(sv - lse)
            dp = _dot(dov, vt_ref[0, j], NN)
            ds = (p * (dp - delta)).astype(BF16)
            acc_ref[...] += _dot(ds, k_ref[pl.ds(pl.multiple_of(j * t, t), t), :], NN)

        def full_step(j, carry):
            step(j, False)
            return carry

        lax.fori_loop(0, qi, full_step, 0)
        step(qi, True)
        dq_ref[...] = (acc_ref[...] * (tab_ref[...] * LN2)).astype(dq_ref.dtype)
        delta_ref[0, 0] = jnp.transpose(jnp.broadcast_to(delta, (t, LANES)))[0:8, :]

    head_q = pl.BlockSpec((t, Q_EXT), lambda h, i: (i, h))
    head_o = pl.BlockSpec((t, V_HEAD), lambda h, i: (i, h))
    return pl.pallas_call(
        body, name=name, grid=(N_HEADS, nq),
        in_specs=[head_q, pl.BlockSpec((t, Q_EXT), lambda h, i: (i, 0)), pl.BlockSpec((1, nq, Q_EXT, t), lambda h, i: (h, 0, 0, 0)),
                  pl.BlockSpec((s, Q_EXT), lambda h, i: (0, h)), pl.BlockSpec((1, nq, V_HEAD, t), lambda h, i: (h, 0, 0, 0)),
                  head_o, head_o, head_o],
        out_specs=(head_q, pl.BlockSpec((1, 1, 8, t), lambda h, i: (h, i, 0, 0))),
        out_shape=(jax.ShapeDtypeStruct((s, N_HEADS * Q_EXT), BF16), jax.ShapeDtypeStruct((N_HEADS, nq, 8, t), F32)),
        scratch_shapes=[pltpu.VMEM((t, Q_EXT), F32)],
        compiler_params=_params(("parallel", "parallel")),
    )(q_rot, tabq, kt4, kfull, vt4, o, lse, do)


def _attn_dkv(name, kfull, v, qt4, q_rot, dot4, do, lse_row, delta_row, acc_in=None):
    s = kfull.shape[0]
    t = min(ATT_TILE, s)
    nq = s // t
    has_in = acc_in is not None

    def body(*refs):
        k_ref, v_ref, qt_ref, q_ref, dot_ref, do_ref, lse_ref, delta_ref = refs[:8]
        dkn_ref, dkd_ref, dv_ref, acck_ref, accv_ref = refs[-5:]
        kj, h = pl.program_id(0), pl.program_id(1)
        k_blk, v_blk = k_ref[...], v_ref[...]
        acck_ref[...] = jnp.zeros_like(acck_ref)
        accv_ref[...] = jnp.zeros_like(accv_ref)

        def step(i, masked):
            qs = pl.ds(pl.multiple_of(i * t, t), t)
            st = _dot(k_blk, qt_ref[0, i], NN)
            if masked:
                krow = lax.broadcasted_iota(jnp.int32, st.shape, 0)
                qcol = lax.broadcasted_iota(jnp.int32, st.shape, 1)
                st = jnp.where(krow <= qcol, st, NEG_BIG)
            pt = jnp.exp2(st - lse_ref[0, i, 0:1, :])
            accv_ref[...] += _dot(pt.astype(BF16), do_ref[qs, :], NN)
            dpt = _dot(v_blk, dot_ref[0, i], NN)
            dst = (pt * (dpt - delta_ref[0, i, 0:1, :])).astype(BF16)
            acck_ref[...] += _dot(dst, q_ref[qs, :], NN)

        def full_step(i, carry):
            step(i, False)
            return carry

        step(kj, True)
        lax.fori_loop(kj + 1, nq, full_step, 0)
        dk = acck_ref[...] * LN2
        dkn, dkd = dk[:, :QK_NOPE], dk[:, QK_NOPE:]
        if has_in:
            dkn = dkn + refs[8][...]
            dv_ref[...] = accv_ref[...] + refs[10][...]
        else:
            dv_ref[...] = accv_ref[...]
        dkn_ref[...] = dkn

        @pl.when(h == 0)
        def _():
            if has_in:
                dkd_ref[...] = dkd + refs[9][...]
            else:
                dkd_ref[...] = dkd

        @pl.when(h > 0)
        def _():
            dkd_ref[...] += dkd

    kblk = pl.BlockSpec((t, LANES), lambda j, h: (j, h))
    kdblk = pl.BlockSpec((t, LANES), lambda j, h: (j, 0))
    col = pl.BlockSpec((s, LANES), lambda j, h: (0, h))
    stat = pl.BlockSpec((1, nq, 8, t), lambda j, h: (h, 0, 0, 0))
    ins = [kfull, v, qt4, q_rot, dot4, do, lse_row, delta_row]
    in_specs = [pl.BlockSpec((t, Q_EXT), lambda j, h: (j, h)), kblk, pl.BlockSpec((1, nq, Q_EXT, t), lambda j, h: (h, 0, 0, 0)),
                pl.BlockSpec((s, Q_EXT), lambda j, h: (0, h)), pl.BlockSpec((1, nq, V_HEAD, t), lambda j, h: (h, 0, 0, 0)), col, stat, stat]
    if has_in:
        ins += list(acc_in)
        in_specs += [kblk, kdblk, kblk]
    return pl.pallas_call(
        body, name=name, grid=(nq, N_HEADS), in_specs=in_specs, out_specs=(kblk, kdblk, kblk),
        out_shape=(jax.ShapeDtypeStruct((s, N_HEADS * LANES), F32), jax.ShapeDtypeStruct((s, LANES), F32),
                   jax.ShapeDtypeStruct((s, N_HEADS * LANES), F32)),
        scratch_shapes=[pltpu.VMEM((t, Q_EXT), F32), pltpu.VMEM((t, LANES), F32)],
        compiler_params=_params(("parallel", "arbitrary")),
    )(*ins)


def _swap_halves(w):
    half = w.shape[-1] // 2
    return jnp.concatenate([-w[..., half:], w[..., :half]], axis=-1)


def _unswap_halves(g):
    half = g.shape[-1] // 2
    return jnp.concatenate([g[..., half:], -g[..., :half]], axis=-1)


def _extend_w_uq(w):
    r = w.reshape(Q_RANK, N_HEADS, QK_HEAD)
    rope = r[..., QK_NOPE:]
    return jnp.concatenate([r[..., :QK_NOPE], rope, _swap_halves(rope)], axis=-1).reshape(Q_RANK, N_HEADS * Q_EXT)


def _fold_w_uq_grad(g):
    r = g.reshape(Q_RANK, N_HEADS, Q_EXT)
    rope = r[..., QK_NOPE:QK_HEAD] + _unswap_halves(r[..., QK_HEAD:])
    return jnp.concatenate([r[..., :QK_NOPE], rope], axis=-1).reshape(Q_RANK, N_HEADS * QK_HEAD)


def _extend_w_dkv(w):
    return jnp.concatenate([w, _swap_halves(w[:, KV_RANK:])], axis=-1)


def _fold_w_dkv_grad(g):
    rope = g[:, KV_RANK:KV_RANK + QK_ROPE] + _unswap_halves(g[:, KV_RANK + QK_ROPE:])
    return jnp.concatenate([g[:, :KV_RANK], rope], axis=-1)


def _rope_tables(positions):
    inv = 1.0 / (ROPE_THETA ** (jnp.arange(0, QK_ROPE, 2, dtype=F32) / QK_ROPE))
    ang = positions.astype(F32)[:, None] * inv
    cos, sin = jnp.cos(ang), jnp.sin(ang)
    tabk = jnp.concatenate([cos, cos, sin, sin], axis=-1)
    scale = QK_HEAD ** -0.5 * LOG2E
    tabq = jnp.concatenate([jnp.full((positions.shape[0], QK_NOPE), scale, F32), tabk * scale], axis=-1)
    return tabq, tabk


def _forward_backward(x, target, mods, tabq, tabk, final_g, fetch, push):
    row = lambda vec: vec.reshape(1, -1)
    mod = [[row(mods[l, k * D_MODEL:(k + 1) * D_MODEL]) for k in range(N_MOD)] for l in range(DEPTH)]
    saved, weights = [], []
    kv = None
    for l in range(DEPTH):
        w, tok = fetch(l, x)
        sh1, sc1, g1, sh2, sc2, g2 = mod[l]
        sh1 = sh1 + tok
        if l == N_A_LAYERS:
            kvn = _rms_fwd("kvin_fwd", x, row(w["kv_in_g"]))
            kv_ext = _mm("dkv_fwd", kvn, w["w_dkv_ext"], out_dtype=F32)
            ckv = _rms_fwd("ckv_fwd", kv_ext, row(w["ckv_norm_g"]), ncols=KV_RANK)
            kd = _krope_fwd("krope_fwd", kv_ext, tabk)
            kn, v = _mm("uk_fwd", ckv, w["w_uk"]), _mm("uv_fwd", ckv, w["w_uv"])
            heads = lambda a: [a[:, h * LANES:(h + 1) * LANES] for h in range(N_HEADS)]
            kfull = jnp.concatenate([part for kh in heads(kn) for part in (kh, kd)], axis=-1)
            v_ext = jnp.concatenate([part for vh in heads(v) for part in (vh, jnp.ones_like(vh))], axis=-1)
            kv = dict(x=x, kvn=kvn, kv_ext=kv_ext, ckv=ckv, v=v, kfull=kfull, v_ext=v_ext,
                      kt4=_head_blocks_t(kfull, Q_EXT), vt4=_head_blocks_t(v, V_HEAD))
        x_in = x
        if l < N_A_LAYERS:
            h1 = _rms_fwd(f"norm1_fwd_{l}", x, row(w["norm1_g"]), sc1, sh1, out_dtype=F32)
            x_mid, zb, pooled = _pool_fwd(f"pool_fwd_{l}", h1, x, w["pool_w"], row(w["pool_b"]), row(w["pool_scale"]), g1)
            mix = (zb, pooled)
        else:
            h1 = _rms_fwd(f"norm1_fwd_{l}", x, row(w["norm1_g"]), sc1, sh1)
            cq_pre = _mm(f"dq_fwd_{l}", h1, w["w_dq"], out_dtype=F32)
            cq = _rms_fwd(f"qnorm_fwd_{l}", cq_pre, row(w["q_norm_g"]))
            q_rot = _mm(f"uq_fwd_{l}", cq, w["w_uq_ext"], rowtab=tabq)
            o, lse, lse_row = _attn_fwd(f"attn_fwd_{l}", q_rot, kv["kt4"], kv["v_ext"])
            y, x_mid = _mm(f"wo_fwd_{l}", o, w["w_o"], resid=x, gate=g1)
            mix = (h1, cq_pre, cq, q_rot, o, lse, lse_row, y)
        h2 = _rms_fwd(f"norm2_fwd_{l}", x_mid, row(w["norm2_g"]), sc2, sh2)
        w_up_a, w_up_v = w["w_up"](h2)
        ua, gl, gpv, ge = _up_glu_fwd(f"up_glu_fwd_{l}", h2, w_up_a, w_up_v, w["conv_w"], row(w["conv_b"]))
        w_down = w["w_down"](gl)
        y2, x = _mm(f"down_fwd_{l}", gl, w_down, resid=x_mid, gate=g2)
        saved.append((x_in, x_mid, h2, ua, gpv, ge, gl, y2, mix))
        weights.append(dict(w, w_up_a=w_up_a, w_up_v=w_up_v, w_down=w_down))

    dx, dfinal_g, loss = _loss_head("loss_head", x, row(final_g), target)
    g = {"final_g": dfinal_g.reshape(-1)}
    per_layer = {k: [None] * DEPTH for k in ("norm1_g", "norm2_g", "conv_w", "conv_b")}
    per_a = {k: [None] * N_A_LAYERS for k in ("pool_b", "pool_scale")}
    per_b = {k: [None] * N_B_LAYERS for k in ("q_norm_g",)}
    dmods = [None] * DEPTH
    dkv = None
    tok = 0.0
    for l in reversed(range(DEPTH)):
        w, big = weights[l], {}
        sh1, sc1, g1, sh2, sc2, g2 = mod[l]
        g2 = g2 + tok
        x_in, x_mid, h2, ua, gpv, ge, gl, y2, mix = saved[l]
        dy2, dg2 = _gate_bwd(f"gate2_bwd_{l}", dx, y2, g2)
        tok = push(l, "down", dict(w_down=_mm(f"down_wgrad_{l}", gl, dy2, mode="tn", tm_cap=1408)), None)
        da, dv_, dcw, dcb = _down_glu_bwd(f"down_glu_bwd_{l}", dy2, w["w_down"], ua, gpv, ge, w["conv_w"] + tok)
        dh2 = _mm(f"up_a_bwd_{l}", da, w["w_up_a"], mode="nt", out_dtype=F32)
        dh2 = _mm(f"up_v_bwd_{l}", dv_, w["w_up_v"], mode="nt", out_dtype=F32, add=dh2)
        tok = push(l, "up", dict(w_up_a=_mm(f"up_a_wgrad_{l}", h2, da, mode="tn"), w_up_v=_mm(f"up_v_wgrad_{l}", h2, dv_, mode="tn")), None)
        per_layer["conv_w"][l], per_layer["conv_b"][l] = dcw, dcb.reshape(-1)
        dx_mid, dn2, dsh2, dsc2 = _rms_bwd(f"norm2_bwd_{l}", x_mid, row(w["norm2_g"]), dh2, sc2 + tok, dx_in=dx)
        per_layer["norm2_g"][l] = dn2.reshape(-1)
        if l < N_A_LAYERS:
            zb, pooled = mix
            dh1, dpw, dpb, dps, dg1 = _pool_bwd(f"pool_bwd_{l}", dx_mid, zb, pooled, w["pool_w"], row(w["pool_scale"]), g1)
            big["pool_w"] = dpw
            per_a["pool_b"][l], per_a["pool_scale"][l] = dpb.reshape(-1), dps.reshape(-1)
        else:
            j = l - N_A_LAYERS
            h1, cq_pre, cq, q_rot, o, lse, lse_row, y = mix
            dy, dg1 = _gate_bwd(f"gate1_bwd_{l}", dx_mid, y, g1)
            do = _mm(f"wo_bwd_{l}", dy, w["w_o"], mode="nt")
            big["w_o"] = _mm(f"wo_wgrad_{l}", o, dy, mode="tn")
            dq_ext, delta_row = _attn_dq(f"attn_dq_{l}", q_rot, tabq, kv["kt4"], kv["kfull"], kv["vt4"], o, lse, do)
            dkv = _attn_dkv(f"attn_dkv_{l}", kv["kfull"], kv["v"], _head_blocks_t(q_rot, Q_EXT), q_rot, _head_blocks_t(do, V_HEAD), do,
                            lse_row, delta_row, acc_in=dkv)
            dcq = _mm(f"uq_bwd_{l}", dq_ext, w["w_uq_ext"], mode="nt", out_dtype=F32)
            big["w_uq_ext"] = _mm(f"uq_wgrad_{l}", cq, dq_ext, mode="tn", out_dtype=F32)
            dcq_pre, dqn = _rms_bwd(f"qnorm_bwd_{l}", cq_pre, row(w["q_norm_g"]), dcq, out_dtype=BF16)
            per_b["q_norm_g"][j] = dqn.reshape(-1)
            dh1 = _mm(f"dq_bwd_{l}", dcq_pre, w["w_dq"], mode="nt")
            big["w_dq"] = _mm(f"dq_wgrad_{l}", h1, dcq_pre, mode="tn")
        dx, dn1, dsh1, dsc1 = _rms_bwd(f"norm1_bwd_{l}", x_in, row(w["norm1_g"]), dh1, sc1, dx_in=dx_mid)
        per_layer["norm1_g"][l] = dn1.reshape(-1)
        dmods[l] = jnp.concatenate([dsh1, dsc1, dg1, dsh2, dsc2, dg2], axis=-1).reshape(-1)
        if l == N_A_LAYERS:
            dkn, dkd, dv = dkv
            dckv = _mm("uk_bwd", dkn, w["w_uk"], mode="nt", out_dtype=F32)
            dckv = _mm("uv_bwd", dv, w["w_uv"], mode="nt", out_dtype=F32, add=dckv)
            big["w_uk"] = _mm("uk_wgrad", kv["ckv"], dkn, mode="tn")
            big["w_uv"] = _mm("uv_wgrad", kv["ckv"], dv, mode="tn")
            dkr = _krope_bwd("krope_bwd", dkd, tabk)
            dc, dckv_g = _rms_bwd("ckv_bwd", kv["kv_ext"], row(w["ckv_norm_g"]), dckv, ncols=KV_RANK, out_dtype=BF16)
            dkv_ext = jnp.concatenate([dc, dkr.astype(BF16)], axis=-1)
            dkvn = _mm("dkv_bwd", dkv_ext, w["w_dkv_ext"], mode="nt")
            big["w_dkv_ext"] = _mm("dkv_wgrad", kv["kvn"], dkv_ext, mode="tn", out_dtype=F32)
            dx, dkv_in_g = _rms_bwd("kvin_bwd", kv["x"], row(w["kv_in_g"]), dkvn, dx_in=dx)
            g["ckv_norm_g"], g["kv_in_g"] = dckv_g.reshape(-1), dkv_in_g.reshape(-1)
        tok = push(l, "mix", big, dx)
    for group in (per_layer, per_a, per_b):
        for k, vals in group.items():
            g[k] = jnp.stack(vals)
    return loss, dx, g, jnp.stack(dmods)


def _my_index():
    return 4 * lax.axis_index("x") + 2 * lax.axis_index("y") + lax.axis_index("c")


def _peer(k):
    x, y, c = lax.axis_index("x"), lax.axis_index("y"), lax.axis_index("c")
    return (1 - x if k & 4 else x, 1 - y if k & 2 else y, 1 - c if k & 1 else c)


def _index_of(pos):
    return 4 * pos[0] + 2 * pos[1] + pos[2]


def _exchange_many(name, arrays, scatter):
    n = len(arrays)
    blocks = [tuple(a.shape[1:]) if scatter else tuple(a.shape) for a in arrays]

    def body(*refs):
        x_refs, o_refs = refs[:n], refs[n:2 * n]
        send_sems, recv_sems, local_sems = refs[2 * n:]
        me = _my_index()
        started = []
        for a in range(n):
            mine = pltpu.make_async_copy(x_refs[a].at[me] if scatter else x_refs[a], o_refs[a].at[me], local_sems.at[a])
            mine.start()
            started.append(mine)
        sends = []
        for k in range(1, N_DEV):
            peer = _peer(k)
            for a in range(n):
                cp = pltpu.make_async_remote_copy(
                    src_ref=x_refs[a].at[_index_of(peer)] if scatter else x_refs[a], dst_ref=o_refs[a].at[me],
                    send_sem=send_sems.at[a, k - 1], recv_sem=recv_sems.at[a, k - 1], device_id=peer, device_id_type=MESH)
                cp.start()
                sends.append(cp)
        for k in range(1, N_DEV):
            peer = _peer(k)
            for a in range(n):
                pltpu.make_async_remote_copy(
                    src_ref=x_refs[a].at[me] if scatter else x_refs[a], dst_ref=o_refs[a].at[_index_of(peer)],
                    send_sem=send_sems.at[a, k - 1], recv_sem=recv_sems.at[a, k - 1], device_id=peer, device_id_type=MESH).wait_recv()
        for cp in sends:
            cp.wait_send()
        for mine in started:
            mine.wait()

    return pl.pallas_call(
        body, name=name, out_shape=tuple(jax.ShapeDtypeStruct((N_DEV,) + blk, a.dtype) for blk, a in zip(blocks, arrays)),
        in_specs=[pl.BlockSpec(memory_space=pl.ANY)] * n, out_specs=tuple([pl.BlockSpec(memory_space=pl.ANY)] * n),
        scratch_shapes=[pltpu.SemaphoreType.DMA((n, N_DEV - 1)), pltpu.SemaphoreType.DMA((n, N_DEV - 1)), pltpu.SemaphoreType.DMA((n,))],
    )(*arrays)


def _exchange(name, x, scatter):
    return _exchange_many(name, [x], scatter)[0]


HBM_SPEC = pl.BlockSpec(memory_space=pltpu.HBM)
SEM_SPEC = pl.BlockSpec(memory_space=pltpu.SEMAPHORE)
DATAFLOW = pltpu.SideEffectType.DATAFLOW_SIDE_EFFECTING


def _remote_copies(x_refs, land_refs, send_sems, recv_sems, scatter, numbers=None):
    me = _my_index()
    numbers = list(range(len(x_refs))) if numbers is None else numbers
    out, inc = [], []
    for a in range(len(x_refs)):
        for k in range(1, N_DEV):
            peer = _peer(k)
            pair = numbers[a] * (N_DEV - 1) + k - 1
            sems = dict(send_sem=send_sems.at[pair], recv_sem=recv_sems.at[pair], device_id=peer, device_id_type=MESH)
            out.append(pltpu.make_async_remote_copy(
                src_ref=x_refs[a].at[_index_of(peer)] if scatter else x_refs[a], dst_ref=land_refs[a].at[me], **sems))
            inc.append(pltpu.make_async_remote_copy(
                src_ref=x_refs[a].at[me] if scatter else x_refs[a], dst_ref=land_refs[a].at[_index_of(peer)], **sems))
    return out, inc


def _exchange_start(name, arrays, scatter):
    n = len(arrays)
    blocks = [tuple(a.shape[1:]) if scatter else tuple(a.shape) for a in arrays]

    def body(*refs):
        x_refs, land_refs = refs[:n], refs[n:2 * n]
        send_sems, recv_sems = refs[2 * n], refs[2 * n + 1]
        for cp in _remote_copies(x_refs, land_refs, send_sems, recv_sems, scatter)[0]:
            cp.start()
        refs[-1][...] = jnp.zeros_like(refs[-1])

    sem_type = pltpu.SemaphoreType.DMA((n * (N_DEV - 1),))
    lands =[pltpu.with_memory_space_constraint(lax.empty((N_DEV,) + blk, a.dtype), pltpu.HBM) for blk, a in zip(blocks, arrays)]
    srcs = [pltpu.with_memory_space_constraint(a, pltpu.HBM) for a in arrays]
    res = pl.pallas_call(
        body, name=name,
        out_shape=(sem_type, sem_type, *[pltpu.HBM(a.shape, a.dtype) for a in srcs + lands], jax.ShapeDtypeStruct((8, LANES), F32)),
        in_specs=[HBM_SPEC] * (2 * n), out_specs=(SEM_SPEC, SEM_SPEC, *[HBM_SPEC] * (2 * n), pl.BlockSpec(memory_space=pltpu.VMEM)),
        input_output_aliases={i: 2 + i for i in range(2 * n)},
        compiler_params=pltpu.CompilerParams(has_side_effects=DATAFLOW),
    )(*srcs, *lands)
    return (res[0], res[1], list(res[2:2 + n]), list(res[2 + n:2 + 2 * n])), res[-1]


def _exchange_wait(name, handles, after, scatter, which=None):
    send_sems, recv_sems, srcs, lands = handles
    which = list(range(len(srcs))) if which is None else list(which)
    srcs, lands = [srcs[a] for a in which], [lands[a] for a in which]
    n = len(srcs)

    def body(*refs):
        x_refs, land_refs = refs[:n], refs[n:2 * n]
        out, inc = _remote_copies(x_refs, land_refs, refs[2 * n], refs[2 * n + 1], scatter, which)
        for cp in out:
            cp.wait_send()
        for cp in inc:
            cp.wait_recv()

    res = pl.pallas_call(
        body, name=name, out_shape=tuple(pltpu.HBM(a.shape, a.dtype) for a in srcs + lands),
        in_specs=[HBM_SPEC] * (2 * n) + [SEM_SPEC, SEM_SPEC, pl.BlockSpec(memory_space=pl.ANY)], out_specs=tuple([HBM_SPEC] * (2 * n)),
        input_output_aliases={i: i for i in range(2 * n)},
        compiler_params=pltpu.CompilerParams(has_side_effects=DATAFLOW),
    )(*srcs, *lands, send_sems, recv_sems, after)
    return list(res[n:])


def _pack(arrays, dtype, row_multiple):
    flat = jnp.concatenate([a.astype(dtype).reshape(-1) for a in arrays])
    rows = -(-flat.shape[0] // (LANES * row_multiple)) * row_multiple
    return jnp.pad(flat, (0, rows * LANES - flat.shape[0])).reshape(rows, LANES)


def _unpack(packed, shapes):
    lead = packed.shape[:-2]
    flat = packed.reshape(lead + (-1,))
    out, off = [], 0
    for shp in shapes:
        size = 1
        for d in shp:
            size *= d
        out.append(flat[..., off:off + size].reshape(lead + tuple(shp)))
        off += size
    return out


def _unshard(g8, axis):
    return jnp.concatenate([g8[j] for j in range(N_DEV)], axis=axis)


def _shard8(full, axis):
    n = full.shape[axis] // N_DEV
    return jnp.stack([lax.slice_in_dim(full, j * n, (j + 1) * n, axis=axis) for j in range(N_DEV)])


VECTOR_WEIGHTS = (("pool_b", 1), ("pool_scale", 1), ("conv_w", 2))
REPLICATED_WEIGHTS = ("norm1_g", "norm2_g", "kv_in_g", "ckv_norm_g", "q_norm_g", "conv_b", "final_g")
WEIGHT_ORDER = ("mod_w", "mod_b", "norm1_g", "norm2_g", "pool_w", "pool_b", "pool_scale", "kv_in_g", "w_dkv", "ckv_norm_g", "w_uk",
                "w_uv", "w_dq", "q_norm_g", "w_uq", "w_o", "w_up", "conv_w", "conv_b", "w_down", "final_g")
BIG_ROW_MULTIPLE = 1024
SMALL_ROW_MULTIPLE = 16


def _as_2d(a):
    if a.ndim == 1:
        return a.reshape(-1, LANES)
    return a.reshape(-1, a.shape[-1])


def kernel(x, c, positions, mod_w, mod_b, norm1_g, norm2_g, pool_w, pool_b, pool_scale, kv_in_g, w_dkv, ckv_norm_g, w_uk, w_uv, w_dq, q_norm_g, w_uq, w_o, w_up, conv_w, conv_b, w_down, final_g, loss_target, m_mod_w, m_mod_b, m_norm1_g, m_norm2_g, m_pool_w, m_pool_b, m_pool_scale, m_kv_in_g, m_w_dkv, m_ckv_norm_g, m_w_uk, m_w_uv, m_w_dq, m_q_norm_g, m_w_uq, m_w_o, m_w_up, m_conv_w, m_conv_b, m_w_down, m_final_g, v_mod_w, v_mod_b, v_norm1_g, v_norm2_g, v_pool_w, v_pool_b, v_pool_scale, v_kv_in_g, v_w_dkv, v_ckv_norm_g, v_w_uk, v_w_uv, v_w_dq, v_q_norm_g, v_w_uq, v_w_o, v_w_up, v_conv_w, v_conv_b, v_w_down, v_final_g):
    shard = dict(mod_w=mod_w, mod_b=mod_b, norm1_g=norm1_g, norm2_g=norm2_g, pool_w=pool_w, pool_b=pool_b, pool_scale=pool_scale,
                 kv_in_g=kv_in_g, w_dkv=w_dkv, ckv_norm_g=ckv_norm_g, w_uk=w_uk, w_uv=w_uv, w_dq=w_dq, q_norm_g=q_norm_g, w_uq=w_uq,
                 w_o=w_o, w_up=w_up, conv_w=conv_w, conv_b=conv_b, w_down=w_down, final_g=final_g)
    mom_m = dict(mod_w=m_mod_w, mod_b=m_mod_b, norm1_g=m_norm1_g, norm2_g=m_norm2_g, pool_w=m_pool_w, pool_b=m_pool_b,
                 pool_scale=m_pool_scale, kv_in_g=m_kv_in_g, w_dkv=m_w_dkv, ckv_norm_g=m_ckv_norm_g, w_uk=m_w_uk, w_uv=m_w_uv,
                 w_dq=m_w_dq, q_norm_g=m_q_norm_g, w_uq=m_w_uq, w_o=m_w_o, w_up=m_w_up, conv_w=m_conv_w, conv_b=m_conv_b,
                 w_down=m_w_down, final_g=m_final_g)
    mom_v = dict(mod_w=v_mod_w, mod_b=v_mod_b, norm1_g=v_norm1_g, norm2_g=v_norm2_g, pool_w=v_pool_w, pool_b=v_pool_b,
                 pool_scale=v_pool_scale, kv_in_g=v_kv_in_g, w_dkv=v_w_dkv, ckv_norm_g=v_ckv_norm_g, w_uk=v_w_uk, w_uv=v_w_uv,
                 w_dq=v_w_dq, q_norm_g=v_q_norm_g, w_uq=v_w_uq, w_o=v_w_o, w_up=v_w_up, conv_w=v_conv_w, conv_b=v_conv_b,
                 w_down=v_w_down, final_g=v_final_g)
    me = _my_index()
    d6 = N_MOD * D_MODEL
    mod_cols = d6 // N_DEV

    small_in = [c] + [shard[k] for k, _ in VECTOR_WEIGHTS]
    small_all = _exchange("gather_vectors", _pack(small_in, F32, SMALL_ROW_MULTIPLE), scatter=False)
    parts = _unpack(small_all, [a.shape for a in small_in])
    c_all = jnp.pad(parts[0].reshape(N_DEV, D_MODEL), ((0, N_DEV), (0, 0)))
    vec = {k: _unshard(p, ax) for (k, ax), p in zip(VECTOR_WEIGHTS, parts[1:])}

    my_mod_b = lax.dynamic_slice_in_dim(mod_b, me * mod_cols, mod_cols, axis=1)
    mods_mine = _mods_fwd("mods_fwd", c_all, mod_w, my_mod_b)
    mods_all = _exchange("gather_mods", _pack([mods_mine], F32, SMALL_ROW_MULTIPLE), scatter=False)
    mods_all = _unpack(mods_all, [mods_mine.shape])[0]
    mods = lax.dynamic_index_in_dim(mods_all, me, axis=2, keepdims=False)
    mods = jnp.moveaxis(mods, 0, 1).reshape(DEPTH, d6)

    tabq, tabk = _rope_tables(positions[0])
    half = N_DEV // 2
    up_cols = shard["w_up"].shape[2]
    cat = lambda a, axis, lo=0, hi=N_DEV: jnp.concatenate([a[j] for j in range(lo, hi)], axis=axis)

    def stage_pieces(l):
        out = {"pool_w": shard["pool_w"].astype(BF16)} if l == 0 else {}
        if l == N_A_LAYERS:
            out.update({k: shard[k].astype(BF16) for k in ("w_dkv", "w_uk", "w_uv")})
        if l >= N_A_LAYERS:
            out.update({k: shard[k][l - N_A_LAYERS].astype(BF16) for k in ("w_dq", "w_uq", "w_o")})
        out.update(w_up=shard["w_up"][l].astype(BF16), w_down=shard["w_down"][l].astype(BF16))
        return out

    gathers, pool_all = {}, []

    def start_gather(l, behind=None):
        pieces = stage_pieces(l)
        if behind is not None:
            pieces, _ = lax.optimization_barrier((pieces, behind))
        handles, token = _exchange_start(f"gather_start_{l}", list(pieces.values()), scatter=False)
        gathers[l] = (handles, pieces)
        return token[0, 0]

    def wait_gather(l, keys, after, tag=""):
        handles, pieces = gathers[l]
        which = [list(pieces).index(k) for k in keys]
        lands = _exchange_wait(f"gather_wait_{l}{tag}", handles, after, scatter=False, which=which)
        return dict(zip(keys, own_slot(lands, [pieces[k] for k in keys])))

    def whole_weights(l, got):
        w = dict(norm1_g=norm1_g[l], norm2_g=norm2_g[l], conv_w=vec["conv_w"][l], conv_b=conv_b[l])
        if l == 0:
            pool_all.append(got["pool_w"])
        if l < N_A_LAYERS:
            w.update(pool_w=cat(pool_all[0][:, l], 1), pool_b=vec["pool_b"][l], pool_scale=vec["pool_scale"][l])
        else:
            rope = got["w_uq"][..., QK_NOPE:]
            ext = jnp.concatenate([got["w_uq"][..., :QK_NOPE], rope, _swap_halves(rope)], axis=-1)
            w.update(w_dq=got["w_dq"].reshape(D_MODEL, Q_RANK), w_uq_ext=cat(ext, -1), w_o=got["w_o"].reshape(D_MODEL, D_MODEL),
                     q_norm_g=q_norm_g[l - N_A_LAYERS])
        if l == N_A_LAYERS:
            w.update(w_dkv_ext=_extend_w_dkv(got["w_dkv"].reshape(D_MODEL, KV_RANK + QK_ROPE)), w_uk=cat(got["w_uk"], -1),
                     w_uv=cat(got["w_uv"], -1), kv_in_g=kv_in_g, ckv_norm_g=ckv_norm_g)
        return w

    def own_slot(lands, own):
        return [lax.dynamic_update_index_in_dim(p, o, me, 0) for p, o in zip(lands, own)]

    def fetch(l, after):
        up_parts = lambda g8: (cat(g8, -1, 0, half), cat(g8, -1, half, N_DEV))
        if l == 0:
            start_gather(0, behind=mods)
            got = wait_gather(0, ["pool_w"], mods, "_pool")
            w_up = lambda aft: up_parts(wait_gather(0, ["w_up"], aft, "_up")["w_up"])
            w_down = lambda aft: wait_gather(0, ["w_down"], aft, "_down")["w_down"].reshape(D_FF, D_MODEL)
        else:
            got = wait_gather(l, list(gathers[l][1]), after)
            up, down = up_parts(got["w_up"]), got["w_down"].reshape(D_FF, D_MODEL)
            w_up, w_down = (lambda aft: up), (lambda aft: down)
        w = dict(whole_weights(l, got), w_up=w_up, w_down=w_down)
        return w, (start_gather(l + 1) if l + 1 < DEPTH else 0.0)

    scatters, pending, pool_grads, piece_grads = {}, {}, {}, {}

    def reduce_pieces(l, keys, got):
        for k, p in zip(keys, got):
            piece_grads[(k, l)] = _sum8(f"sum_grads_{k}_{l}", p.reshape(N_DEV, -1, p.shape[-1])).reshape(p.shape[1:])

    def start_scatter(name, sent):
        sent = {k: a.astype(BF16) for k, a in sent.items()}
        handles, token = _exchange_start(f"scatter_start_{name}", list(sent.values()), scatter=True)
        scatters[name] = (handles, list(sent), [lax.dynamic_index_in_dim(a, me, 0, keepdims=False) for a in sent.values()])
        return token[0, 0]

    def finish_scatter(name, l, after):
        handles, keys, own = scatters.pop(name)
        reduce_pieces(l, keys, own_slot(_exchange_wait(f"scatter_wait_{name}", handles, after, scatter=True), own))

    def push(l, part, big, after):
        cut = lambda a, n, axis: jnp.stack([lax.slice_in_dim(a, j * n, (j + 1) * n, axis=axis) for j in range(N_DEV)])
        sent = {}
        if part == "down":
            sent["w_down"] = big["w_down"].reshape(N_DEV, D_FF // N_DEV, D_MODEL)
        elif part == "up":
            sent["w_up"] = jnp.stack([lax.slice_in_dim(big[half_], j * up_cols, (j + 1) * up_cols, axis=1)
                                      for half_ in ("w_up_a", "w_up_v") for j in range(half)])
        elif l < N_A_LAYERS:
            pool_grads[l] = big["pool_w"]
        else:
            ext = cut(big["w_uq_ext"], Q_EXT, 1)
            rope = ext[..., QK_NOPE:QK_HEAD] + _unswap_halves(ext[..., QK_HEAD:])
            sent.update(w_dq=big["w_dq"].reshape(N_DEV, D_MODEL // N_DEV, Q_RANK), w_uq=jnp.concatenate([ext[..., :QK_NOPE], rope], axis=-1),
                        w_o=big["w_o"].reshape(N_DEV, D_MODEL // N_DEV, D_MODEL))
        if part == "mix" and l == N_A_LAYERS:
            sent.update(w_dkv=_fold_w_dkv_grad(big["w_dkv_ext"]).reshape(N_DEV, D_MODEL // N_DEV, KV_RANK + QK_ROPE),
                        w_uk=cut(big["w_uk"], QK_NOPE, 1), w_uv=cut(big["w_uv"], V_HEAD, 1))
        if l == 0 and part != "mix":
            return start_scatter(f"0_{part}", sent)
        if l == 0:
            finish_scatter("1", 1, after)
            pool = _shard8(jnp.stack([pool_grads[a] for a in range(N_A_LAYERS)]), 2).astype(BF16)
            reduce_pieces(0, ["pool_w"], _exchange_many("scatter_pool_grads", [pool], scatter=True))
            return 0.0
        pending.setdefault(l, {}).update(sent)
        if part != "mix":
            return 0.0
        if l + 1 < DEPTH:
            finish_scatter(str(l + 1), l + 1, after)
        return start_scatter(str(l), pending.pop(l))

    loss_row, dx, g, dmods = _forward_backward(x[0], loss_target[0], mods, tabq, tabk, final_g, fetch, push)
    layers_of = lambda k, ls: jnp.stack([piece_grads[(k, l)] for l in ls])
    grads = dict(w_dkv=piece_grads[("w_dkv", N_A_LAYERS)], w_uk=piece_grads[("w_uk", N_A_LAYERS)], w_uv=piece_grads[("w_uv", N_A_LAYERS)])
    for k in ("w_dq", "w_uq", "w_o"):
        grads[k] = layers_of(k, range(N_A_LAYERS, DEPTH))

    small_names = REPLICATED_WEIGHTS + tuple(k for k, _ in VECTOR_WEIGHTS)
    small_out = [dmods] + [g[k] for k in small_names] + [loss_row]
    small_shapes = [a.shape for a in small_out]
    small_got = _exchange("gather_small_grads", _pack(small_out, F32, SMALL_ROW_MULTIPLE), scatter=False)
    summed = _unpack(_sum8("sum_small_grads", small_got), small_shapes)
    grads["mod_b"] = summed[0]
    for k, s in zip(small_names, summed[1:-1]):
        grads[k] = s
    for k, ax in VECTOR_WEIGHTS:
        n = shard[k].shape[ax]
        grads[k] = lax.dynamic_slice_in_dim(grads[k], me * n, n, axis=ax)
    loss = summed[-1][0, 0]
    dmods_all = _unpack(small_got, small_shapes)[0]
    dm_mine = lax.dynamic_slice_in_dim(dmods_all, me * mod_cols, mod_cols, axis=2)
    dm_mine = jnp.pad(jnp.moveaxis(dm_mine, 0, 1), ((0, 0), (0, N_DEV), (0, 0)))
    grads["mod_w"] = _mods_bwd("mods_bwd", c_all, dm_mine)

    delta, new_m, new_v = {}, {}, {}

    def adamw(k):
        shp = shard[k].shape
        grads[k] = grads[k].reshape(shp)
        view = (lambda a: jnp.swapaxes(a, 1, 2)) if k == "w_up" else (lambda a: a)
        ops = [view(a) for a in (shard[k], grads[k], mom_m[k], mom_v[k])]
        res = _adamw(f"adamw_{k}", *[_as_2d(a) for a in ops])
        delta[k], new_m[k], new_v[k] = [view(r.reshape(ops[0].shape)) for r in res]
        grads[k] = view(ops[1])

    late = ("w_up", "w_down", "pool_w")
    for k in WEIGHT_ORDER:
        if k not in late:
            adamw(k)
    finish_scatter("0_down", 0, delta["final_g"])
    finish_scatter("0_up", 0, delta["final_g"])
    grads.update(w_up=layers_of("w_up", range(DEPTH)), w_down=layers_of("w_down", range(DEPTH)), pool_w=piece_grads[("pool_w", 0)])
    for k in late:
        adamw(k)
    return (loss, dx[None], *[grads[k] for k in WEIGHT_ORDER], *[delta[k] for k in WEIGHT_ORDER],
            *[new_m[k] for k in WEIGHT_ORDER], *[new_v[k] for k in WEIGHT_ORDER])
```

```python
import functools

import jax
import jax.numpy as jnp
from jax import lax
from jax.experimental import pallas as pl
from jax.experimental.pallas import tpu as pltpu

F32 = jnp.float32
BF16 = jnp.bfloat16

D_MODEL = 1024
DEPTH = 4
N_A_LAYERS = 2
N_B_LAYERS = 2
POOL_WINDOWS = (2, 4, 8, 16)
POOL_GROUP = 256
N_HEADS = 8
QK_NOPE = 128
QK_ROPE = 64
V_HEAD = 128
QK_HEAD = QK_NOPE + QK_ROPE
Q_RANK = 384
KV_RANK = 256
ROPE_THETA = 10000.0
D_FF = 2816
EPS = 1e-6
N_MOD = 6
ADAM_LR = 0.001
ADAM_B1 = 0.9
ADAM_B2 = 0.999
ADAM_EPS = 1e-08
ADAM_WD = 0.01
ADAM_STEP = 10

N_DEV = 8
LANES = 128
Q_EXT = 256
VMEM_LIMIT_BYTES = 48 * 1024 * 1024
MESH = pl.DeviceIdType.MESH
NEG_BIG = -0.7 * float(jnp.finfo(jnp.float32).max)


def _params(sem):
    return pltpu.CompilerParams(dimension_semantics=sem, vmem_limit_bytes=VMEM_LIMIT_BYTES)


def _tile(n, cap):
    if n <= cap:
        return n
    best = None
    for d in range(LANES, cap + 1, LANES):
        if n % d == 0:
            best = d
    assert best is not None, (n, cap)
    return best


def _dot(a, b, dims):
    return lax.dot_general(a, b, (dims, ((), ())), preferred_element_type=F32)


NN = ((1,), (0,))
NT = ((1,), (1,))
TN = ((0,), (0,))


def _mm(name, a, b, mode="nn", out_dtype=BF16, add=None, resid=None, gate=None, rowtab=None,
        tm_cap=1024, tn_cap=1408, tk_cap=1408):
    if mode == "tn":
        kdim, m = a.shape
    else:
        m, kdim = a.shape
    n = b.shape[0] if mode == "nt" else b.shape[1]
    tm, tn, tk = _tile(m, tm_cap), _tile(n, tn_cap), _tile(kdim, tk_cap)
    nk = kdim // tk
    dims = {"nn": NN, "nt": NT, "tn": TN}[mode]
    a_spec = pl.BlockSpec((tk, tm), lambda i, j, k: (k, i)) if mode == "tn" else pl.BlockSpec((tm, tk), lambda i, j, k: (i, k))
    b_spec = pl.BlockSpec((tn, tk), lambda i, j, k: (j, k)) if mode == "nt" else pl.BlockSpec((tk, tn), lambda i, j, k: (k, j))
    o_spec = pl.BlockSpec((tm, tn), lambda i, j, k: (i, j))
    g_spec = pl.BlockSpec((1, tn), lambda i, j, k: (0, j))
    gated = resid is not None

    def body(*refs):
        a_ref, b_ref = refs[0], refs[1]
        acc = refs[-1]
        k = pl.program_id(2)

        @pl.when(k == 0)
        def _():
            acc[...] = jnp.zeros_like(acc)

        acc[...] += _dot(a_ref[...].astype(BF16), b_ref[...].astype(BF16), dims)

        @pl.when(k == nk - 1)
        def _():
            if gated:
                r_ref, g_ref, y_ref, x_ref = refs[2:6]
                y_ref[...] = acc[...]
                x_ref[...] = r_ref[...] + g_ref[...] * acc[...]
            elif add is not None:
                refs[3][...] = (acc[...] + refs[2][...].astype(F32)).astype(out_dtype)
            elif rowtab is not None:
                tab = refs[2][...]
                refs[3][...] = (acc[...] * jnp.concatenate([tab] * (tn // tab.shape[1]), axis=1)).astype(out_dtype)
            else:
                refs[2][...] = acc[...].astype(out_dtype)

    ins, in_specs = [a, b], [a_spec, b_spec]
    if rowtab is not None:
        assert tn % rowtab.shape[1] == 0 and not gated and add is None
        ins.append(rowtab)
        in_specs.append(pl.BlockSpec((tm, rowtab.shape[1]), lambda i, j, k: (i, 0)))
    if gated:
        ins += [resid, gate]
        in_specs += [o_spec, g_spec]
        out_shape = (jax.ShapeDtypeStruct((m, n), F32), jax.ShapeDtypeStruct((m, n), F32))
        out_specs = (o_spec, o_spec)
    else:
        if add is not None:
            ins.append(add)
            in_specs.append(o_spec)
        out_shape = jax.ShapeDtypeStruct((m, n), out_dtype)
        out_specs = o_spec
    return pl.pallas_call(
        body, name=name, grid=(m // tm, n // tn, nk), in_specs=in_specs, out_specs=out_specs, out_shape=out_shape,
        scratch_shapes=[pltpu.VMEM((tm, tn), F32)],
        compiler_params=_params(("parallel", "parallel", "arbitrary")),
    )(*ins)


def _rowwise(name, fn, tiled, bcast, outs, sums=(), tr=512):
    tiled = [t if isinstance(t, tuple) else (t, t.shape[1], 0) for t in tiled]
    s = tiled[0][0].shape[0]
    tr = min(tr, s)
    assert s % tr == 0
    n_t, n_b, n_o = len(tiled), len(bcast), len(outs)

    def body(*refs):
        i = pl.program_id(0)
        vals = [r[...] for r in refs[:n_t + n_b]]
        o_vals, s_vals = fn(*vals)
        for r, v in zip(refs[n_t + n_b:n_t + n_b + n_o], o_vals):
            r[...] = v.astype(r.dtype)
        s_refs = refs[n_t + n_b + n_o:]

        @pl.when(i == 0)
        def _():
            for r in s_refs:
                r[...] = jnp.zeros_like(r)

        for r, v in zip(s_refs, s_vals):
            r[...] += v

    in_specs = [pl.BlockSpec((tr, n), functools.partial(lambda cb, i: (i, cb), cb)) for (_, n, cb) in tiled]
    in_specs += [pl.BlockSpec(b.shape, functools.partial(lambda nd, i: (0,) * nd, b.ndim)) for b in bcast]
    out_specs = [pl.BlockSpec((tr, n), lambda i: (i, 0)) for (n, _) in outs]
    out_specs += [pl.BlockSpec((1, n), lambda i: (0, 0)) for n in sums]
    out_shape = [jax.ShapeDtypeStruct((s, n), dt) for (n, dt) in outs]
    out_shape += [jax.ShapeDtypeStruct((1, n), F32) for n in sums]
    res = pl.pallas_call(
        body, name=name, grid=(s // tr,), in_specs=in_specs, out_specs=tuple(out_specs), out_shape=tuple(out_shape),
        compiler_params=_params(("arbitrary",)),
    )(*[t[0] for t in tiled], *bcast)
    return res


def _colsum(v):
    return jnp.sum(v, axis=0, keepdims=True)


def _rms_fwd(name, x, g, scale=None, shift=None, out_dtype=BF16, ncols=None):
    mod = scale is not None

    def fn(xv, gv, *ss):
        y = xv * lax.rsqrt(jnp.mean(xv * xv, axis=-1, keepdims=True) + EPS) * gv
        if mod:
            y = y * (1.0 + ss[0]) + ss[1]
        return (y,), ()

    n = ncols or x.shape[1]
    return _rowwise(name, fn, [(x, n, 0)], [g] + ([scale, shift] if mod else []), [(n, out_dtype)])[0]


def _rms_bwd(name, x, g, dh, scale=None, dx_in=None, ncols=None, out_dtype=F32):
    mod = scale is not None
    has_in = dx_in is not None

    def fn(*vals):
        xv, dhv = vals[0], vals[1].astype(F32)
        rest = list(vals[2:])
        dxi = rest.pop(0) if has_in else None
        gv = rest.pop(0)
        rstd = lax.rsqrt(jnp.mean(xv * xv, axis=-1, keepdims=True) + EPS)
        xhat = xv * rstd
        sums = []
        if mod:
            sc = rest.pop(0)
            dyn = dhv * (1.0 + sc)
            dshift, dscale = _colsum(dhv), _colsum(dhv * (xhat * gv))
        else:
            dyn = dhv
        dg = _colsum(dyn * xhat)
        dxhat = dyn * gv
        dx = rstd * (dxhat - xhat * jnp.mean(dxhat * xhat, axis=-1, keepdims=True))
        if has_in:
            dx = dx + dxi
        sums = [dg] + ([dshift, dscale] if mod else [])
        return (dx,), sums

    n = ncols or x.shape[1]
    tiled = [(x, n, 0), dh] + ([dx_in] if has_in else [])
    return _rowwise(name, fn, tiled, [g] + ([scale] if mod else []), [(n, out_dtype)], [n] * (3 if mod else 1))


def _gate_bwd(name, dxn, y, g):
    def fn(dv, yv, gv):
        return (gv * dv,), (_colsum(dv * yv),)

    n = dxn.shape[1]
    return _rowwise(name, fn, [dxn, y], [g], [(n, BF16)], [n])


def _loss_head(name, x, g, target):
    n = x.shape[1]

    def fn(xv, tv, gv):
        rstd = lax.rsqrt(jnp.mean(xv * xv, axis=-1, keepdims=True) + EPS)
        xhat = xv * rstd
        err = xhat * gv - tv
        loss = 0.5 * jnp.sum(jnp.sum(err * err, axis=-1, keepdims=True) / n, axis=0, keepdims=True)
        dy = err / n
        dg = _colsum(dy * xhat)
        dxhat = dy * gv
        dx = rstd * (dxhat - xhat * jnp.mean(dxhat * xhat, axis=-1, keepdims=True))
        return (dx,), (dg, jnp.broadcast_to(loss, (1, LANES)))

    return _rowwise(name, fn, [x, target], [g], [(n, F32)], [n, LANES])


def _krope_fwd(name, kv_ext, tabk):
    def fn(xv, tv):
        t = xv * tv
        return (t + pltpu.roll(t, 64, 1),), ()

    return _rowwise(name, fn, [(kv_ext, LANES, 2), tabk], [], [(LANES, BF16)])[0]


def _krope_bwd(name, dkd, tabk):
    def fn(dv, tv):
        d = dv[:, :LANES]
        for h in range(1, N_HEADS):
            d = d + dv[:, h * LANES:(h + 1) * LANES]
        return ((d + pltpu.roll(d, 64, 1)) * tv,), ()

    return _rowwise(name, fn, [dkd, tabk], [], [(LANES, F32)])[0]


def _adamw(name, w, g, m, v):
    def fn(wv, gv, mv, vv):
        m2 = ADAM_B1 * mv + (1.0 - ADAM_B1) * gv
        v2 = ADAM_B2 * vv + (1.0 - ADAM_B2) * (gv * gv)
        m_hat = m2 / (1.0 - ADAM_B1 ** ADAM_STEP)
        v_hat = v2 / (1.0 - ADAM_B2 ** ADAM_STEP)
        delta = -ADAM_LR * (m_hat / (jnp.sqrt(v_hat) + ADAM_EPS) + ADAM_WD * wv)
        return (delta, m2, v2), ()

    r, c = w.shape
    tr = r
    for cand in (512, 256, 128, 64, 32, 16, 8):
        if r % cand == 0 and r > cand:
            tr = cand
            break
    return _rowwise(name, fn, [w, g, m, v], [], [(c, F32)] * 3, tr=tr)


def _sum8(name, parts):
    _, r, c = parts.shape
    tr = r
    for cand in (2048, 1024, 512, 256, 128, 64, 32, 16):
        if r % cand == 0 and r > cand and cand * c <= 256 * 1024:
            tr = cand
            break

    def body(p_ref, o_ref):
        acc = p_ref[0].astype(F32)
        for k in range(1, N_DEV):
            acc = acc + p_ref[k].astype(F32)
        o_ref[...] = acc

    return pl.pallas_call(
        body, name=name, grid=(r // tr,), in_specs=[pl.BlockSpec((N_DEV, tr, c), lambda i: (0, i, 0))],
        out_specs=pl.BlockSpec((tr, c), lambda i: (i, 0)), out_shape=jax.ShapeDtypeStruct((r, c), F32),
        compiler_params=_params(("parallel",)),
    )(parts)


def _mods_fwd(name, c_all, w, b):
    depth, d, n = w.shape

    def body(c_ref, w_ref, b_ref, o_ref):
        cv = c_ref[...]
        sc = (cv * (1.0 / (1.0 + jnp.exp(-cv)))).astype(BF16)
        o_ref[0] = _dot(sc, w_ref[0].astype(BF16), NN) + b_ref[0]

    return pl.pallas_call(
        body, name=name, grid=(depth,),
        in_specs=[pl.BlockSpec(c_all.shape, lambda l: (0, 0)), pl.BlockSpec((1, d, n), lambda l: (l, 0, 0)),
                  pl.BlockSpec((1, 1, n), lambda l: (l, 0, 0))],
        out_specs=pl.BlockSpec((1, c_all.shape[0], n), lambda l: (l, 0, 0)),
        out_shape=jax.ShapeDtypeStruct((depth, c_all.shape[0], n), F32),
        compiler_params=_params(("parallel",)),
    )(c_all, w, b.reshape(depth, 1, n))


def _mods_bwd(name, c_all, dm):
    depth, rows, n = dm.shape
    d = c_all.shape[1]

    def body(c_ref, dm_ref, o_ref):
        cv = c_ref[...]
        sc = (cv * (1.0 / (1.0 + jnp.exp(-cv)))).astype(BF16)
        o_ref[0] = _dot(sc, dm_ref[0].astype(BF16), TN)

    return pl.pallas_call(
        body, name=name, grid=(depth,),
        in_specs=[pl.BlockSpec(c_all.shape, lambda l: (0, 0)), pl.BlockSpec((1, rows, n), lambda l: (l, 0, 0))],
        out_specs=pl.BlockSpec((1, d, n), lambda l: (l, 0, 0)),
        out_shape=jax.ShapeDtypeStruct((depth, d, n), F32),
        compiler_params=_params(("parallel",)),
    )(c_all, dm)


POOL_TILE = 256


def _split_dot(band, val):
    hi = val.astype(BF16)
    lo = (val - hi.astype(F32)).astype(BF16)
    return _dot(band, hi, NN) + _dot(band, lo, NN)


def _pool_fwd(name, h1, x, pw, pb, ps, g1):
    s, d = h1.shape
    t = POOL_TILE

    def body(hc_ref, hp_ref, x_ref, pw_ref, pb_ref, ps_ref, g_ref, xo_ref, zb_ref, pooled_ref):
        i = pl.program_id(0)
        r = lax.broadcasted_iota(jnp.int32, (t, t), 0)
        j = lax.broadcasted_iota(jnp.int32, (t, t), 1)
        pos = (i * t + lax.broadcasted_iota(jnp.int32, (t, 1), 0) + 1).astype(F32)
        has_prev = (i > 0).astype(F32)
        for grp, w in enumerate(POOL_WINDOWS):
            cs = slice(grp * POOL_GROUP, (grp + 1) * POOL_GROUP)
            hc = hc_ref[:, cs]
            band_cur = ((r - j >= 0) & (r - j < w)).astype(BF16)
            band_prev = (r + t - j < w).astype(BF16)
            ssum = _split_dot(band_cur, hc) + has_prev * _split_dot(band_prev, hp_ref[:, cs])
            pooled = (ssum / jnp.minimum(pos, float(w)) - hc).astype(BF16)
            zb = _dot(pooled, pw_ref[grp], NN) + pb_ref[:, cs]
            xo_ref[:, cs] = x_ref[:, cs] + g_ref[:, cs] * (zb * ps_ref[:, cs])
            zb_ref[:, cs] = zb
            pooled_ref[:, cs] = pooled

    row = pl.BlockSpec((t, d), lambda i: (i, 0))
    vec = pl.BlockSpec((1, d), lambda i: (0, 0))
    return pl.pallas_call(
        body, name=name, grid=(s // t,),
        in_specs=[row, pl.BlockSpec((t, d), lambda i: (jnp.maximum(i - 1, 0), 0)), row,
                  pl.BlockSpec(pw.shape, lambda i: (0, 0, 0)), vec, vec, vec],
        out_specs=(row, row, row),
        out_shape=(jax.ShapeDtypeStruct((s, d), F32), jax.ShapeDtypeStruct((s, d), F32), jax.ShapeDtypeStruct((s, d), BF16)),
        compiler_params=_params(("parallel",)),
    )(h1, h1, x, pw, pb, ps, g1)


def _pool_bwd(name, dxn, zb, pooled, pw, ps, g1):
    s, d = dxn.shape
    t = POOL_TILE
    nt = s // t

    def body(dc_ref, dn_ref, zb_ref, pooled_ref, pw_ref, ps_ref, g_ref, dh_ref, dpw_ref, dpb_ref, dps_ref, dg_ref):
        i = pl.program_id(0)

        @pl.when(i == 0)
        def _():
            dpw_ref[...] = jnp.zeros_like(dpw_ref)
            dpb_ref[...] = jnp.zeros_like(dpb_ref)
            dps_ref[...] = jnp.zeros_like(dps_ref)
            dg_ref[...] = jnp.zeros_like(dg_ref)

        jj = lax.broadcasted_iota(jnp.int32, (t, t), 0)
        rr = lax.broadcasted_iota(jnp.int32, (t, t), 1)
        pos = (i * t + lax.broadcasted_iota(jnp.int32, (t, 1), 0) + 1).astype(F32)
        has_next = (i < nt - 1).astype(F32)
        for grp, w in enumerate(POOL_WINDOWS):
            cs = slice(grp * POOL_GROUP, (grp + 1) * POOL_GROUP)
            gv, psv, zbv, dxc = g_ref[:, cs], ps_ref[:, cs], zb_ref[:, cs], dc_ref[:, cs]
            dg_ref[:, cs] += _colsum(dxc * (zbv * psv))
            dy = gv * dxc
            dps_ref[:, cs] += _colsum(dy * zbv)
            dz = dy * psv
            dpb_ref[:, cs] += _colsum(dz)
            dzb = dz.astype(BF16)
            dpw_ref[grp] += _dot(pooled_ref[:, cs], dzb, TN)
            dp = _dot(dzb, pw_ref[grp], NT)
            dzn = (gv * dn_ref[:, cs] * psv).astype(BF16)
            dpn = _dot(dzn, pw_ref[grp], NT) * (has_next / float(w))
            band_cur = ((rr - jj >= 0) & (rr - jj < w)).astype(BF16)
            band_next = (rr + t - jj < w).astype(BF16)
            dh_ref[:, cs] = _split_dot(band_cur, dp / jnp.minimum(pos, float(w))) + _split_dot(band_next, dpn) - dp

    row = pl.BlockSpec((t, d), lambda i: (i, 0))
    vec = pl.BlockSpec((1, d), lambda i: (0, 0))
    wspec = pl.BlockSpec(pw.shape, lambda i: (0, 0, 0))
    return pl.pallas_call(
        body, name=name, grid=(nt,),
        in_specs=[row, pl.BlockSpec((t, d), lambda i: (jnp.minimum(i + 1, nt - 1), 0)), row, row, wspec, vec, vec],
        out_specs=(row, wspec, vec, vec, vec),
        out_shape=(jax.ShapeDtypeStruct((s, d), F32), jax.ShapeDtypeStruct(pw.shape, F32),
                   jax.ShapeDtypeStruct((1, d), F32), jax.ShapeDtypeStruct((1, d), F32), jax.ShapeDtypeStruct((1, d), F32)),
        compiler_params=_params(("arbitrary",)),
    )(dxn, dxn, zb, pooled, pw, ps, g1)


GLU_TILE = 512
HALO = 16
INV_SQRT2 = 0.7071067811865476
INV_SQRT_2PI = 0.3989422804014327


def _up_glu_fwd(name, h2, wa, wv, cw, cb):
    s, d = h2.shape
    f = wa.shape[1]
    tm, tn = _tile(s, 1024), _tile(f, 1408)

    def body(h_ref, hh_ref, wa_ref, wv_ref, cw_ref, cb_ref, ua_ref, gl_ref, gpv_ref, ge_ref):
        i = pl.program_id(1)
        has_prev = (i > 0).astype(F32)
        a = _dot(h_ref[...], wa_ref[...], NN).astype(BF16)
        v = _dot(h_ref[...], wv_ref[...], NN)
        above = (_dot(hh_ref[...], wa_ref[...], NN) * has_prev).astype(BF16)
        ua_ref[...] = a
        ext = jnp.concatenate([above.astype(F32), a.astype(F32)], axis=0)
        e1 = pltpu.roll(ext, 1, 0)[HALO:]
        e2 = pltpu.roll(ext, 2, 0)[HALO:]
        pre = e2 * cw_ref[0:1, :] + e1 * cw_ref[1:2, :] + ext[HALO:] * cw_ref[2:3, :] + cb_ref[...]
        cdf = 0.5 * (1.0 + lax.erf(pre * INV_SQRT2))
        ge = pre * cdf
        gl_ref[...] = (ge * v).astype(gl_ref.dtype)
        gpv_ref[...] = ((cdf + pre * (INV_SQRT_2PI * jnp.exp(-0.5 * pre * pre))) * v).astype(gpv_ref.dtype)
        ge_ref[...] = ge.astype(ge_ref.dtype)

    blk = pl.BlockSpec((tm, tn), lambda j, i: (i, j))
    wspec = pl.BlockSpec((d, tn), lambda j, i: (0, j))
    return pl.pallas_call(
        body, name=name, grid=(f // tn, s // tm),
        in_specs=[pl.BlockSpec((tm, d), lambda j, i: (i, 0)), pl.BlockSpec((HALO, d), lambda j, i: (jnp.maximum(i * (tm // HALO) - 1, 0), 0)),
                  wspec, wspec, pl.BlockSpec((3, tn), lambda j, i: (0, j)), pl.BlockSpec((1, tn), lambda j, i: (0, j))],
        out_specs=(blk, blk, blk, blk), out_shape=tuple(jax.ShapeDtypeStruct((s, f), BF16) for _ in range(4)),
        compiler_params=_params(("parallel", "parallel")),
    )(h2, h2, wa, wv, cw, cb)


def _down_glu_bwd(name, dy2, wd, ua, gpv, ge, cw):
    s, f = ua.shape
    d = dy2.shape[1]
    t, tf = min(GLU_TILE, s), _tile(f, 1408)
    nt = s // t
    te = t + HALO

    def body(dy_ref, dyn_ref, wd_ref, a_ref, ah_ref, g_ref, gn_ref, ge_ref, cw_ref, da_ref, dv_ref, dcw_ref, dcb_ref):
        i = pl.program_id(1)

        @pl.when(i == 0)
        def _():
            dcw_ref[...] = jnp.zeros_like(dcw_ref)
            dcb_ref[...] = jnp.zeros_like(dcb_ref)

        has_prev = (i > 0).astype(F32)
        has_next = (i < nt - 1).astype(F32)
        wdv = wd_ref[...]
        dgl = _dot(dy_ref[...], wdv, NT)
        dgl_below = _dot(dyn_ref[...], wdv, NT) * has_next
        dpre = jnp.concatenate([dgl * g_ref[...].astype(F32), dgl_below * gn_ref[...].astype(F32)], axis=0)
        c0, c1, c2 = cw_ref[0:1, :], cw_ref[1:2, :], cw_ref[2:3, :]
        up1 = pltpu.roll(dpre, te - 1, 0)
        up2 = pltpu.roll(dpre, te - 2, 0)
        da_ref[...] = (dpre * c2 + up1 * c1 + up2 * c0)[:t].astype(da_ref.dtype)
        dv_ref[...] = (dgl * ge_ref[...].astype(F32)).astype(dv_ref.dtype)
        ext = jnp.concatenate([ah_ref[...].astype(F32) * has_prev, a_ref[...].astype(F32)], axis=0)
        dpt = dpre[:t]
        dcb_ref[...] += _colsum(dpt)
        dcw_ref[0:1, :] += _colsum(pltpu.roll(ext, 2, 0)[HALO:] * dpt)
        dcw_ref[1:2, :] += _colsum(pltpu.roll(ext, 1, 0)[HALO:] * dpt)
        dcw_ref[2:3, :] += _colsum(ext[HALO:] * dpt)

    blk = pl.BlockSpec((t, tf), lambda j, i: (i, j))
    prev = pl.BlockSpec((HALO, tf), lambda j, i: (jnp.maximum(i * (t // HALO) - 1, 0), j))
    below = lambda i: jnp.minimum((i + 1) * (t // HALO), s // HALO - 1)
    w3 = pl.BlockSpec((3, tf), lambda j, i: (0, j))
    w1 = pl.BlockSpec((1, tf), lambda j, i: (0, j))
    return pl.pallas_call(
        body, name=name, grid=(f // tf, nt),
        in_specs=[pl.BlockSpec((t, d), lambda j, i: (i, 0)), pl.BlockSpec((HALO, d), lambda j, i: (below(i), 0)),
                  pl.BlockSpec((tf, d), lambda j, i: (j, 0)), blk, prev, blk, pl.BlockSpec((HALO, tf), lambda j, i: (below(i), j)), blk, w3],
        out_specs=(blk, blk, w3, w1),
        out_shape=(jax.ShapeDtypeStruct((s, f), BF16), jax.ShapeDtypeStruct((s, f), BF16),
                   jax.ShapeDtypeStruct((3, f), F32), jax.ShapeDtypeStruct((1, f), F32)),
        compiler_params=_params(("parallel", "arbitrary")),
    )(dy2, dy2, wd, ua, ua, gpv, gpv, ge, cw)


ATT_TILE = 512
ATT_ROWS = 256
LOG2E = 1.4426950408889634
LN2 = 0.6931471805599453


def _head_blocks_t(a, width):
    s = a.shape[0]
    t = min(ATT_TILE, s)
    return a.reshape(s // t, t, N_HEADS, width).transpose(2, 0, 3, 1)


def _causal_mask(sv, q0, k0):
    row = q0 + lax.broadcasted_iota(jnp.int32, sv.shape, 0)
    col = k0 + lax.broadcasted_iota(jnp.int32, sv.shape, 1)
    return jnp.where(col <= row, sv, NEG_BIG)


def _attn_fwd(name, q_rot, kt4, v_ext):
    s = q_rot.shape[0]
    t = min(ATT_TILE, s)
    nq = s // t
    rq = min(ATT_ROWS, t)

    def body(q_ref, kt_ref, v_ref, o_ref, row_ref, acc_ref, m_ref):
        qi = pl.program_id(1)
        acc_ref[...] = jnp.zeros_like(acc_ref)
        m_ref[...] = jnp.full_like(m_ref, NEG_BIG)

        def step(j, masked):
            v_blk = v_ref[pl.ds(pl.multiple_of(j * t, t), t), :]
            for r in range(t // rq):
                rs = pl.ds(r * rq, rq)
                sv = _dot(q_ref[rs, :], kt_ref[0, j], NN)
                if masked:
                    sv = _causal_mask(sv, r * rq, 0)
                m_prev = m_ref[rs, :]
                m_new = jnp.maximum(m_prev, jnp.max(sv, axis=-1, keepdims=True))
                p = jnp.exp2(sv - m_new).astype(BF16)
                acc_ref[rs, :] = jnp.exp2(m_prev - m_new) * acc_ref[rs, :] + _dot(p, v_blk, NN)
                m_ref[rs, :] = m_new

        def full_step(j, carry):
            step(j, False)
            return carry

        lax.fori_loop(0, qi, full_step, 0)
        step(qi, True)
        l = acc_ref[:, V_HEAD:V_HEAD + 1]
        o_ref[...] = (acc_ref[:, :V_HEAD] / l).astype(o_ref.dtype)
        lse = jnp.broadcast_to(m_ref[...] + jnp.log(l) * LOG2E, (t, LANES))
        row_ref[0, 0] = jnp.transpose(lse)[0:8, :]

    head_q = pl.BlockSpec((t, Q_EXT), lambda h, i: (i, h))
    head_o = pl.BlockSpec((t, V_HEAD), lambda h, i: (i, h))
    return pl.pallas_call(
        body, name=name, grid=(N_HEADS, nq),
        in_specs=[head_q, pl.BlockSpec((1, nq, Q_EXT, t), lambda h, i: (h, 0, 0, 0)), pl.BlockSpec((s, Q_EXT), lambda h, i: (0, h))],
        out_specs=(head_o, pl.BlockSpec((1, 1, 8, t), lambda h, i: (h, i, 0, 0))),
        out_shape=(jax.ShapeDtypeStruct((s, N_HEADS * V_HEAD), BF16), jax.ShapeDtypeStruct((N_HEADS, nq, 8, t), F32)),
        scratch_shapes=[pltpu.VMEM((t, Q_EXT), F32), pltpu.VMEM((t, 1), F32)],
        compiler_params=_params(("parallel", "parallel")),
    )(q_rot, kt4, v_ext)


def _attn_delta(name, o, do):
    s = o.shape[0]
    t = min(ATT_TILE, s)

    def body(o_ref, do_ref, delta_ref):
        delta = jnp.sum(do_ref[...].astype(F32) * o_ref[...].astype(F32), axis=-1, keepdims=True)
        delta_ref[0, 0] = jnp.transpose(jnp.broadcast_to(delta, (t, LANES)))[0:8, :]

    head_o = pl.BlockSpec((t, V_HEAD), lambda h, i: (i, h))
    return pl.pallas_call(
        body, name=name, grid=(N_HEADS, s // t), in_specs=[head_o, head_o],
        out_specs=pl.BlockSpec((1, 1, 8, t), lambda h, i: (h, i, 0, 0)),
        out_shape=jax.ShapeDtypeStruct((N_HEADS, s // t, 8, t), F32),
        compiler_params=_params(("parallel", "parallel")),
    )(o, do)


def _attn_bwd(name, kfull, v, qt4, q_rot, dot4, do, lse_row, delta_row, tabq, acc_in=None):
    s = kfull.shape[0]
    t = min(ATT_TILE, s)
    nq = s // t
    has_in = acc_in is not None

    def body(*refs):
        k_ref, v_ref, qt_ref, q_ref, dot_ref, do_ref, lse_ref, delta_ref, tab_ref = refs[:9]
        dq_ref, dkn_ref, dkd_ref, dv_ref, dq_acc_ref, acck_ref, accv_ref = refs[-7:]
        kj = pl.program_id(1)

        @pl.when(kj == 0)
        def _():
            dq_acc_ref[...] = jnp.zeros_like(dq_acc_ref)

        k_blk, v_blk = k_ref[...], v_ref[...]
        acck_ref[...] = jnp.zeros_like(acck_ref)
        accv_ref[...] = jnp.zeros_like(accv_ref)

        def step(i, masked):
            qs = pl.ds(pl.multiple_of(i * t, t), t)
            st = _dot(k_blk, qt_ref[0, i], NN)
            if masked:
                krow = lax.broadcasted_iota(jnp.int32, st.shape, 0)
                qcol = lax.broadcasted_iota(jnp.int32, st.shape, 1)
                st = jnp.where(krow <= qcol, st, NEG_BIG)
            pt = jnp.exp2(st - lse_ref[0, i, 0:1, :])
            accv_ref[...] += _dot(pt.astype(BF16), do_ref[qs, :], NN)
            dpt = _dot(v_blk, dot_ref[0, i], NN)
            dst = (pt * (dpt - delta_ref[0, i, 0:1, :])).astype(BF16)
            acck_ref[...] += _dot(dst, q_ref[qs, :], NN)
            dq_acc_ref[qs, :] += _dot(dst, k_blk, TN)

        def full_step(i, carry):
            step(i, False)
            return carry

        step(kj, True)
        lax.fori_loop(kj + 1, nq, full_step, 0)
        dk = acck_ref[...] * LN2
        if has_in:
            dkn_ref[...] = dk[:, :QK_NOPE] + refs[9][...]
            dkd_ref[...] = dk[:, QK_NOPE:] + refs[10][...]
            dv_ref[...] = accv_ref[...] + refs[11][...]
        else:
            dkn_ref[...] = dk[:, :QK_NOPE]
            dkd_ref[...] = dk[:, QK_NOPE:]
            dv_ref[...] = accv_ref[...]

        @pl.when(kj == nq - 1)
        def _():
            dq_ref[...] = (dq_acc_ref[...] * (tab_ref[...] * LN2)).astype(dq_ref.dtype)

    kblk = pl.BlockSpec((t, LANES), lambda h, j: (j, h))
    col = pl.BlockSpec((s, LANES), lambda h, j: (0, h))
    q_all = pl.BlockSpec((s, Q_EXT), lambda h, j: (0, h))
    stat = pl.BlockSpec((1, nq, 8, t), lambda h, j: (h, 0, 0, 0))
    ins = [kfull, v, qt4, q_rot, dot4, do, lse_row, delta_row, tabq]
    in_specs = [pl.BlockSpec((t, Q_EXT), lambda h, j: (j, h)), kblk, pl.BlockSpec((1, nq, Q_EXT, t), lambda h, j: (h, 0, 0, 0)),
                q_all, pl.BlockSpec((1, nq, V_HEAD, t), lambda h, j: (h, 0, 0, 0)), col, stat, stat,
                pl.BlockSpec((s, Q_EXT), lambda h, j: (0, 0))]
    if has_in:
        ins += list(acc_in)
        in_specs += [kblk, kblk, kblk]
    wide = jax.ShapeDtypeStruct((s, N_HEADS * LANES), F32)
    return pl.pallas_call(
        body, name=name, grid=(N_HEADS, nq), in_specs=in_specs, out_specs=(q_all, kblk, kblk, kblk),
        out_shape=(jax.ShapeDtypeStruct((s, N_HEADS * Q_EXT), BF16), wide, wide, wide),
        scratch_shapes=[pltpu.VMEM((s, Q_EXT), F32), pltpu.VMEM((t, Q_EXT), F32), pltpu.VMEM((t, LANES), F32)],
        compiler_params=_params(("parallel", "arbitrary")),
    )(*ins)


def _swap_halves(w):
    half = w.shape[-1] // 2
    return jnp.concatenate([-w[..., half:], w[..., :half]], axis=-1)


def _unswap_halves(g):
    half = g.shape[-1] // 2
    return jnp.concatenate([g[..., half:], -g[..., :half]], axis=-1)


def _extend_w_uq(w):
    r = w.reshape(Q_RANK, N_HEADS, QK_HEAD)
    rope = r[..., QK_NOPE:]
    return jnp.concatenate([r[..., :QK_NOPE], rope, _swap_halves(rope)], axis=-1).reshape(Q_RANK, N_HEADS * Q_EXT)


def _fold_w_uq_grad(g):
    r = g.reshape(Q_RANK, N_HEADS, Q_EXT)
    rope = r[..., QK_NOPE:QK_HEAD] + _unswap_halves(r[..., QK_HEAD:])
    return jnp.concatenate([r[..., :QK_NOPE], rope], axis=-1).reshape(Q_RANK, N_HEADS * QK_HEAD)


def _extend_w_dkv(w):
    return jnp.concatenate([w, _swap_halves(w[:, KV_RANK:])], axis=-1)


def _fold_w_dkv_grad(g):
    rope = g[:, KV_RANK:KV_RANK + QK_ROPE] + _unswap_halves(g[:, KV_RANK + QK_ROPE:])
    return jnp.concatenate([g[:, :KV_RANK], rope], axis=-1)


def _rope_tables(positions):
    inv = 1.0 / (ROPE_THETA ** (jnp.arange(0, QK_ROPE, 2, dtype=F32) / QK_ROPE))
    ang = positions.astype(F32)[:, None] * inv
    cos, sin = jnp.cos(ang), jnp.sin(ang)
    tabk = jnp.concatenate([cos, cos, sin, sin], axis=-1)
    scale = QK_HEAD ** -0.5 * LOG2E
    tabq = jnp.concatenate([jnp.full((positions.shape[0], QK_NOPE), scale, F32), tabk * scale], axis=-1)
    return tabq, tabk


def _forward_backward(x, target, mods, tabq, tabk, final_g, fetch, push):
    row = lambda vec: vec.reshape(1, -1)
    mod = [[row(mods[l, k * D_MODEL:(k + 1) * D_MODEL]) for k in range(N_MOD)] for l in range(DEPTH)]
    saved, weights = [], []
    kv = None
    for l in range(DEPTH):
        w, tok = fetch(l, x)
        sh1, sc1, g1, sh2, sc2, g2 = mod[l]
        sh1 = sh1 + tok
        if l == N_A_LAYERS:
            kvn = _rms_fwd("kvin_fwd", x, row(w["kv_in_g"]))
            kv_ext = _mm("dkv_fwd", kvn, w["w_dkv_ext"], out_dtype=F32)
            ckv = _rms_fwd("ckv_fwd", kv_ext, row(w["ckv_norm_g"]), ncols=KV_RANK)
            kd = _krope_fwd("krope_fwd", kv_ext, tabk)
            kn, v = _mm("uk_fwd", ckv, w["w_uk"]), _mm("uv_fwd", ckv, w["w_uv"])
            heads = lambda a: [a[:, h * LANES:(h + 1) * LANES] for h in range(N_HEADS)]
            kfull = jnp.concatenate([part for kh in heads(kn) for part in (kh, kd)], axis=-1)
            v_ext = jnp.concatenate([part for vh in heads(v) for part in (vh, jnp.ones_like(vh))], axis=-1)
            kv = dict(x=x, kvn=kvn, kv_ext=kv_ext, ckv=ckv, v=v, kfull=kfull, v_ext=v_ext,
                      kt4=_head_blocks_t(kfull, Q_EXT))
        x_in = x
        if l < N_A_LAYERS:
            h1 = _rms_fwd(f"norm1_fwd_{l}", x, row(w["norm1_g"]), sc1, sh1, out_dtype=F32)
            x_mid, zb, pooled = _pool_fwd(f"pool_fwd_{l}", h1, x, w["pool_w"], row(w["pool_b"]), row(w["pool_scale"]), g1)
            mix = (zb, pooled)
        else:
            h1 = _rms_fwd(f"norm1_fwd_{l}", x, row(w["norm1_g"]), sc1, sh1)
            cq_pre = _mm(f"dq_fwd_{l}", h1, w["w_dq"], out_dtype=F32)
            cq = _rms_fwd(f"qnorm_fwd_{l}", cq_pre, row(w["q_norm_g"]))
            q_rot = _mm(f"uq_fwd_{l}", cq, w["w_uq_ext"], rowtab=tabq)
            o, lse_row = _attn_fwd(f"attn_fwd_{l}", q_rot, kv["kt4"], kv["v_ext"])
            y, x_mid = _mm(f"wo_fwd_{l}", o, w["w_o"], resid=x, gate=g1)
            mix = (h1, cq_pre, cq, q_rot, o, lse_row, y)
        h2 = _rms_fwd(f"norm2_fwd_{l}", x_mid, row(w["norm2_g"]), sc2, sh2)
        w_up_a, w_up_v = w["w_up"](h2)
        ua, gl, gpv, ge = _up_glu_fwd(f"up_glu_fwd_{l}", h2, w_up_a, w_up_v, w["conv_w"], row(w["conv_b"]))
        w_down = w["w_down"](gl)
        y2, x = _mm(f"down_fwd_{l}", gl, w_down, resid=x_mid, gate=g2)
        saved.append((x_in, x_mid, h2, ua, gpv, ge, gl, y2, mix))
        weights.append(dict(w, w_up_a=w_up_a, w_up_v=w_up_v, w_down=w_down))

    dx, dfinal_g, loss = _loss_head("loss_head", x, row(final_g), target)
    g = {"final_g": dfinal_g.reshape(-1)}
    per_layer = {k: [None] * DEPTH for k in ("norm1_g", "norm2_g", "conv_w", "conv_b")}
    per_a = {k: [None] * N_A_LAYERS for k in ("pool_b", "pool_scale")}
    per_b = {k: [None] * N_B_LAYERS for k in ("q_norm_g",)}
    dmods = [None] * DEPTH
    dkv = None
    tok = 0.0
    for l in reversed(range(DEPTH)):
        w, big = weights[l], {}
        sh1, sc1, g1, sh2, sc2, g2 = mod[l]
        g2 = g2 + tok
        x_in, x_mid, h2, ua, gpv, ge, gl, y2, mix = saved[l]
        dy2, dg2 = _gate_bwd(f"gate2_bwd_{l}", dx, y2, g2)
        tok = push(l, "down", dict(w_down=_mm(f"down_wgrad_{l}", gl, dy2, mode="tn", tm_cap=1408)), None)
        da, dv_, dcw, dcb = _down_glu_bwd(f"down_glu_bwd_{l}", dy2, w["w_down"], ua, gpv, ge, w["conv_w"] + tok)
        dh2 = _mm(f"up_a_bwd_{l}", da, w["w_up_a"], mode="nt", out_dtype=F32)
        dh2 = _mm(f"up_v_bwd_{l}", dv_, w["w_up_v"], mode="nt", out_dtype=F32, add=dh2)
        tok = push(l, "up", dict(w_up_a=_mm(f"up_a_wgrad_{l}", h2, da, mode="tn"), w_up_v=_mm(f"up_v_wgrad_{l}", h2, dv_, mode="tn")), None)
        per_layer["conv_w"][l], per_layer["conv_b"][l] = dcw, dcb.reshape(-1)
        dx_mid, dn2, dsh2, dsc2 = _rms_bwd(f"norm2_bwd_{l}", x_mid, row(w["norm2_g"]), dh2, sc2 + tok, dx_in=dx)
        per_layer["norm2_g"][l] = dn2.reshape(-1)
        if l < N_A_LAYERS:
            zb, pooled = mix
            dh1, dpw, dpb, dps, dg1 = _pool_bwd(f"pool_bwd_{l}", dx_mid, zb, pooled, w["pool_w"], row(w["pool_scale"]), g1)
            big["pool_w"] = dpw
            per_a["pool_b"][l], per_a["pool_scale"][l] = dpb.reshape(-1), dps.reshape(-1)
        else:
            j = l - N_A_LAYERS
            h1, cq_pre, cq, q_rot, o, lse_row, y = mix
            dy, dg1 = _gate_bwd(f"gate1_bwd_{l}", dx_mid, y, g1)
            do = _mm(f"wo_bwd_{l}", dy, w["w_o"], mode="nt")
            big["w_o"] = _mm(f"wo_wgrad_{l}", o, dy, mode="tn")
            delta_row = _attn_delta(f"attn_delta_{l}", o, do)
            dq_ext, *dkv = _attn_bwd(f"attn_bwd_{l}", kv["kfull"], kv["v"], _head_blocks_t(q_rot, Q_EXT), q_rot, _head_blocks_t(do, V_HEAD), do,
                                     lse_row, delta_row, tabq, acc_in=dkv)
            dcq = _mm(f"uq_bwd_{l}", dq_ext, w["w_uq_ext"], mode="nt", out_dtype=F32)
            big["w_uq_ext"] = _mm(f"uq_wgrad_{l}", cq, dq_ext, mode="tn", out_dtype=F32)
            dcq_pre, dqn = _rms_bwd(f"qnorm_bwd_{l}", cq_pre, row(w["q_norm_g"]), dcq, out_dtype=BF16)
            per_b["q_norm_g"][j] = dqn.reshape(-1)
            dh1 = _mm(f"dq_bwd_{l}", dcq_pre, w["w_dq"], mode="nt")
            big["w_dq"] = _mm(f"dq_wgrad_{l}", h1, dcq_pre, mode="tn")
        dx, dn1, dsh1, dsc1 = _rms_bwd(f"norm1_bwd_{l}", x_in, row(w["norm1_g"]), dh1, sc1, dx_in=dx_mid)
        per_layer["norm1_g"][l] = dn1.reshape(-1)
        dmods[l] = jnp.concatenate([dsh1, dsc1, dg1, dsh2, dsc2, dg2], axis=-1).reshape(-1)
        if l == N_A_LAYERS:
            dkn, dkd, dv = dkv
            dckv = _mm("uk_bwd", dkn, w["w_uk"], mode="nt", out_dtype=F32)
            dckv = _mm("uv_bwd", dv, w["w_uv"], mode="nt", out_dtype=F32, add=dckv)
            big["w_uk"] = _mm("uk_wgrad", kv["ckv"], dkn, mode="tn")
            big["w_uv"] = _mm("uv_wgrad", kv["ckv"], dv, mode="tn")
            dkr = _krope_bwd("krope_bwd", dkd, tabk)
            dc, dckv_g = _rms_bwd("ckv_bwd", kv["kv_ext"], row(w["ckv_norm_g"]), dckv, ncols=KV_RANK, out_dtype=BF16)
            dkv_ext = jnp.concatenate([dc, dkr.astype(BF16)], axis=-1)
            dkvn = _mm("dkv_bwd", dkv_ext, w["w_dkv_ext"], mode="nt")
            big["w_dkv_ext"] = _mm("dkv_wgrad", kv["kvn"], dkv_ext, mode="tn", out_dtype=F32)
            dx, dkv_in_g = _rms_bwd("kvin_bwd", kv["x"], row(w["kv_in_g"]), dkvn, dx_in=dx)
            g["ckv_norm_g"], g["kv_in_g"] = dckv_g.reshape(-1), dkv_in_g.reshape(-1)
        tok = push(l, "mix", big, dx)
    for group in (per_layer, per_a, per_b):
        for k, vals in group.items():
            g[k] = jnp.stack(vals)
    return loss, dx, g, jnp.stack(dmods)


def _my_index():
    return 4 * lax.axis_index("x") + 2 * lax.axis_index("y") + lax.axis_index("c")


def _peer(k):
    x, y, c = lax.axis_index("x"), lax.axis_index("y"), lax.axis_index("c")
    return (1 - x if k & 4 else x, 1 - y if k & 2 else y, 1 - c if k & 1 else c)


def _index_of(pos):
    return 4 * pos[0] + 2 * pos[1] + pos[2]


def _exchange_many(name, arrays, scatter):
    n = len(arrays)
    blocks = [tuple(a.shape[1:]) if scatter else tuple(a.shape) for a in arrays]

    def body(*refs):
        x_refs, o_refs = refs[:n], refs[n:2 * n]
        send_sems, recv_sems, local_sems = refs[2 * n:]
        me = _my_index()
        started = []
        for a in range(n):
            mine = pltpu.make_async_copy(x_refs[a].at[me] if scatter else x_refs[a], o_refs[a].at[me], local_sems.at[a])
            mine.start()
            started.append(mine)
        sends = []
        for k in range(1, N_DEV):
            peer = _peer(k)
            for a in range(n):
                cp = pltpu.make_async_remote_copy(
                    src_ref=x_refs[a].at[_index_of(peer)] if scatter else x_refs[a], dst_ref=o_refs[a].at[me],
                    send_sem=send_sems.at[a, k - 1], recv_sem=recv_sems.at[a, k - 1], device_id=peer, device_id_type=MESH)
                cp.start()
                sends.append(cp)
        for k in range(1, N_DEV):
            peer = _peer(k)
            for a in range(n):
                pltpu.make_async_remote_copy(
                    src_ref=x_refs[a].at[me] if scatter else x_refs[a], dst_ref=o_refs[a].at[_index_of(peer)],
                    send_sem=send_sems.at[a, k - 1], recv_sem=recv_sems.at[a, k - 1], device_id=peer, device_id_type=MESH).wait_recv()
        for cp in sends:
            cp.wait_send()
        for mine in started:
            mine.wait()

    return pl.pallas_call(
        body, name=name, out_shape=tuple(jax.ShapeDtypeStruct((N_DEV,) + blk, a.dtype) for blk, a in zip(blocks, arrays)),
        in_specs=[pl.BlockSpec(memory_space=pl.ANY)] * n, out_specs=tuple([pl.BlockSpec(memory_space=pl.ANY)] * n),
        scratch_shapes=[pltpu.SemaphoreType.DMA((n, N_DEV - 1)), pltpu.SemaphoreType.DMA((n, N_DEV - 1)), pltpu.SemaphoreType.DMA((n,))],
    )(*arrays)


def _exchange(name, x, scatter):
    return _exchange_many(name, [x], scatter)[0]


HBM_SPEC = pl.BlockSpec(memory_space=pltpu.HBM)
SEM_SPEC = pl.BlockSpec(memory_space=pltpu.SEMAPHORE)
DATAFLOW = pltpu.SideEffectType.DATAFLOW_SIDE_EFFECTING


def _remote_copies(x_refs, land_refs, send_sems, recv_sems, scatter, numbers=None):
    me = _my_index()
    numbers = list(range(len(x_refs))) if numbers is None else numbers
    out, inc = [], []
    for a in range(len(x_refs)):
        for k in range(1, N_DEV):
            peer = _peer(k)
            pair = numbers[a] * (N_DEV - 1) + k - 1
            sems = dict(send_sem=send_sems.at[pair], recv_sem=recv_sems.at[pair], device_id=peer, device_id_type=MESH)
            out.append(pltpu.make_async_remote_copy(
                src_ref=x_refs[a].at[_index_of(peer)] if scatter else x_refs[a], dst_ref=land_refs[a].at[me], **sems))
            inc.append(pltpu.make_async_remote_copy(
                src_ref=x_refs[a].at[me] if scatter else x_refs[a], dst_ref=land_refs[a].at[_index_of(peer)], **sems))
    return out, inc


def _exchange_start(name, arrays, scatter):
    n = len(arrays)
    blocks = [tuple(a.shape[1:]) if scatter else tuple(a.shape) for a in arrays]

    def body(*refs):
        x_refs, land_refs = refs[:n], refs[n:2 * n]
        send_sems, recv_sems = refs[2 * n], refs[2 * n + 1]
        for cp in _remote_copies(x_refs, land_refs, send_sems, recv_sems, scatter)[0]:
            cp.start()
        refs[-1][...] = jnp.zeros_like(refs[-1])

    sem_type = pltpu.SemaphoreType.DMA((n * (N_DEV - 1),))
    lands =[pltpu.with_memory_space_constraint(lax.empty((N_DEV,) + blk, a.dtype), pltpu.HBM) for blk, a in zip(blocks, arrays)]
    srcs = [pltpu.with_memory_space_constraint(a, pltpu.HBM) for a in arrays]
    res = pl.pallas_call(
        body, name=name,
        out_shape=(sem_type, sem_type, *[pltpu.HBM(a.shape, a.dtype) for a in srcs + lands], jax.ShapeDtypeStruct((8, LANES), F32)),
        in_specs=[HBM_SPEC] * (2 * n), out_specs=(SEM_SPEC, SEM_SPEC, *[HBM_SPEC] * (2 * n), pl.BlockSpec(memory_space=pltpu.VMEM)),
        input_output_aliases={i: 2 + i for i in range(2 * n)},
        compiler_params=pltpu.CompilerParams(has_side_effects=DATAFLOW),
    )(*srcs, *lands)
    return (res[0], res[1], list(res[2:2 + n]), list(res[2 + n:2 + 2 * n])), res[-1]


def _exchange_wait(name, handles, after, scatter, which=None):
    send_sems, recv_sems, srcs, lands = handles
    which = list(range(len(srcs))) if which is None else list(which)
    srcs, lands = [srcs[a] for a in which], [lands[a] for a in which]
    n = len(srcs)

    def body(*refs):
        x_refs, land_refs = refs[:n], refs[n:2 * n]
        out, inc = _remote_copies(x_refs, land_refs, refs[2 * n], refs[2 * n + 1], scatter, which)
        for cp in out:
            cp.wait_send()
        for cp in inc:
            cp.wait_recv()

    res = pl.pallas_call(
        body, name=name, out_shape=tuple(pltpu.HBM(a.shape, a.dtype) for a in srcs + lands),
        in_specs=[HBM_SPEC] * (2 * n) + [SEM_SPEC, SEM_SPEC, pl.BlockSpec(memory_space=pl.ANY)], out_specs=tuple([HBM_SPEC] * (2 * n)),
        input_output_aliases={i: i for i in range(2 * n)},
        compiler_params=pltpu.CompilerParams(has_side_effects=DATAFLOW),
    )(*srcs, *lands, send_sems, recv_sems, after)
    return list(res[n:])


def _pack(arrays, dtype, row_multiple):
    flat = jnp.concatenate([a.astype(dtype).reshape(-1) for a in arrays])
    rows = -(-flat.shape[0] // (LANES * row_multiple)) * row_multiple
    return jnp.pad(flat, (0, rows * LANES - flat.shape[0])).reshape(rows, LANES)


def _unpack(packed, shapes):
    lead = packed.shape[:-2]
    flat = packed.reshape(lead + (-1,))
    out, off = [], 0
    for shp in shapes:
        size = 1
        for d in shp:
            size *= d
        out.append(flat[..., off:off + size].reshape(lead + tuple(shp)))
        off += size
    return out


def _unshard(g8, axis):
    return jnp.concatenate([g8[j] for j in range(N_DEV)], axis=axis)


def _shard8(full, axis):
    n = full.shape[axis] // N_DEV
    return jnp.stack([lax.slice_in_dim(full, j * n, (j + 1) * n, axis=axis) for j in range(N_DEV)])


VECTOR_WEIGHTS = (("pool_b", 1), ("pool_scale", 1), ("conv_w", 2))
REPLICATED_WEIGHTS = ("norm1_g", "norm2_g", "kv_in_g", "ckv_norm_g", "q_norm_g", "conv_b", "final_g")
WEIGHT_ORDER = ("mod_w", "mod_b", "norm1_g", "norm2_g", "pool_w", "pool_b", "pool_scale", "kv_in_g", "w_dkv", "ckv_norm_g", "w_uk",
                "w_uv", "w_dq", "q_norm_g", "w_uq", "w_o", "w_up", "conv_w", "conv_b", "w_down", "final_g")
BIG_ROW_MULTIPLE = 1024
SMALL_ROW_MULTIPLE = 16


def _as_2d(a):
    if a.ndim == 1:
        return a.reshape(-1, LANES)
    return a.reshape(-1, a.shape[-1])


def kernel(x, c, positions, mod_w, mod_b, norm1_g, norm2_g, pool_w, pool_b, pool_scale, kv_in_g, w_dkv, ckv_norm_g, w_uk, w_uv, w_dq, q_norm_g, w_uq, w_o, w_up, conv_w, conv_b, w_down, final_g, loss_target, m_mod_w, m_mod_b, m_norm1_g, m_norm2_g, m_pool_w, m_pool_b, m_pool_scale, m_kv_in_g, m_w_dkv, m_ckv_norm_g, m_w_uk, m_w_uv, m_w_dq, m_q_norm_g, m_w_uq, m_w_o, m_w_up, m_conv_w, m_conv_b, m_w_down, m_final_g, v_mod_w, v_mod_b, v_norm1_g, v_norm2_g, v_pool_w, v_pool_b, v_pool_scale, v_kv_in_g, v_w_dkv, v_ckv_norm_g, v_w_uk, v_w_uv, v_w_dq, v_q_norm_g, v_w_uq, v_w_o, v_w_up, v_conv_w, v_conv_b, v_w_down, v_final_g):
    shard = dict(mod_w=mod_w, mod_b=mod_b, norm1_g=norm1_g, norm2_g=norm2_g, pool_w=pool_w, pool_b=pool_b, pool_scale=pool_scale,
                 kv_in_g=kv_in_g, w_dkv=w_dkv, ckv_norm_g=ckv_norm_g, w_uk=w_uk, w_uv=w_uv, w_dq=w_dq, q_norm_g=q_norm_g, w_uq=w_uq,
                 w_o=w_o, w_up=w_up, conv_w=conv_w, conv_b=conv_b, w_down=w_down, final_g=final_g)
    mom_m = dict(mod_w=m_mod_w, mod_b=m_mod_b, norm1_g=m_norm1_g, norm2_g=m_norm2_g, pool_w=m_pool_w, pool_b=m_pool_b,
                 pool_scale=m_pool_scale, kv_in_g=m_kv_in_g, w_dkv=m_w_dkv, ckv_norm_g=m_ckv_norm_g, w_uk=m_w_uk, w_uv=m_w_uv,
                 w_dq=m_w_dq, q_norm_g=m_q_norm_g, w_uq=m_w_uq, w_o=m_w_o, w_up=m_w_up, conv_w=m_conv_w, conv_b=m_conv_b,
                 w_down=m_w_down, final_g=m_final_g)
    mom_v = dict(mod_w=v_mod_w, mod_b=v_mod_b, norm1_g=v_norm1_g, norm2_g=v_norm2_g, pool_w=v_pool_w, pool_b=v_pool_b,
                 pool_scale=v_pool_scale, kv_in_g=v_kv_in_g, w_dkv=v_w_dkv, ckv_norm_g=v_ckv_norm_g, w_uk=v_w_uk, w_uv=v_w_uv,
                 w_dq=v_w_dq, q_norm_g=v_q_norm_g, w_uq=v_w_uq, w_o=v_w_o, w_up=v_w_up, conv_w=v_conv_w, conv_b=v_conv_b,
                 w_down=v_w_down, final_g=v_final_g)
    me = _my_index()
    d6 = N_MOD * D_MODEL
    mod_cols = d6 // N_DEV

    small_in = [c] + [shard[k] for k, _ in VECTOR_WEIGHTS]
    small_all = _exchange("gather_vectors", _pack(small_in, F32, SMALL_ROW_MULTIPLE), scatter=False)
    parts = _unpack(small_all, [a.shape for a in small_in])
    c_all = jnp.pad(parts[0].reshape(N_DEV, D_MODEL), ((0, N_DEV), (0, 0)))
    vec = {k: _unshard(p, ax) for (k, ax), p in zip(VECTOR_WEIGHTS, parts[1:])}

    my_mod_b = lax.dynamic_slice_in_dim(mod_b, me * mod_cols, mod_cols, axis=1)
    mods_mine = _mods_fwd("mods_fwd", c_all, mod_w, my_mod_b)
    mods_all = _exchange("gather_mods", _pack([mods_mine], F32, SMALL_ROW_MULTIPLE), scatter=False)
    mods_all = _unpack(mods_all, [mods_mine.shape])[0]
    mods = lax.dynamic_index_in_dim(mods_all, me, axis=2, keepdims=False)
    mods = jnp.moveaxis(mods, 0, 1).reshape(DEPTH, d6)

    tabq, tabk = _rope_tables(positions[0])
    half = N_DEV // 2
    up_cols = shard["w_up"].shape[2]
    cat = lambda a, axis, lo=0, hi=N_DEV: jnp.concatenate([a[j] for j in range(lo, hi)], axis=axis)

    def stage_pieces(l):
        out = {"pool_w": shard["pool_w"].astype(BF16)} if l == 0 else {}
        if l == N_A_LAYERS:
            out.update({k: shard[k].astype(BF16) for k in ("w_dkv", "w_uk", "w_uv")})
        if l >= N_A_LAYERS:
            out.update({k: shard[k][l - N_A_LAYERS].astype(BF16) for k in ("w_dq", "w_uq", "w_o")})
        out.update(w_up=shard["w_up"][l].astype(BF16), w_down=shard["w_down"][l].astype(BF16))
        return out

    gathers, pool_all = {}, []

    def start_gather(l, behind=None):
        pieces = stage_pieces(l)
        if behind is not None:
            pieces, _ = lax.optimization_barrier((pieces, behind))
        handles, token = _exchange_start(f"gather_start_{l}", list(pieces.values()), scatter=False)
        gathers[l] = (handles, pieces)
        return token[0, 0]

    def wait_gather(l, keys, after, tag=""):
        handles, pieces = gathers[l]
        which = [list(pieces).index(k) for k in keys]
        lands = _exchange_wait(f"gather_wait_{l}{tag}", handles, after, scatter=False, which=which)
        return dict(zip(keys, own_slot(lands, [pieces[k] for k in keys])))

    def whole_weights(l, got):
        w = dict(norm1_g=norm1_g[l], norm2_g=norm2_g[l], conv_w=vec["conv_w"][l], conv_b=conv_b[l])
        if l == 0:
            pool_all.append(got["pool_w"])
        if l < N_A_LAYERS:
            w.update(pool_w=cat(pool_all[0][:, l], 1), pool_b=vec["pool_b"][l], pool_scale=vec["pool_scale"][l])
        else:
            rope = got["w_uq"][..., QK_NOPE:]
            ext = jnp.concatenate([got["w_uq"][..., :QK_NOPE], rope, _swap_halves(rope)], axis=-1)
            w.update(w_dq=got["w_dq"].reshape(D_MODEL, Q_RANK), w_uq_ext=cat(ext, -1), w_o=got["w_o"].reshape(D_MODEL, D_MODEL),
                     q_norm_g=q_norm_g[l - N_A_LAYERS])
        if l == N_A_LAYERS:
            w.update(w_dkv_ext=_extend_w_dkv(got["w_dkv"].reshape(D_MODEL, KV_RANK + QK_ROPE)), w_uk=cat(got["w_uk"], -1),
                     w_uv=cat(got["w_uv"], -1), kv_in_g=kv_in_g, ckv_norm_g=ckv_norm_g)
        return w

    def own_slot(lands, own):
        return [lax.dynamic_update_index_in_dim(p, o, me, 0) for p, o in zip(lands, own)]

    def fetch(l, after):
        up_parts = lambda g8: (cat(g8, -1, 0, half), cat(g8, -1, half, N_DEV))
        if l == 0:
            start_gather(0, behind=mods)
            got = wait_gather(0, ["pool_w"], mods, "_pool")
            w_up = lambda aft: up_parts(wait_gather(0, ["w_up"], aft, "_up")["w_up"])
            w_down = lambda aft: wait_gather(0, ["w_down"], aft, "_down")["w_down"].reshape(D_FF, D_MODEL)
        else:
            got = wait_gather(l, list(gathers[l][1]), after)
            up, down = up_parts(got["w_up"]), got["w_down"].reshape(D_FF, D_MODEL)
            w_up, w_down = (lambda aft: up), (lambda aft: down)
        w = dict(whole_weights(l, got), w_up=w_up, w_down=w_down)
        return w, (start_gather(l + 1) if l + 1 < DEPTH else 0.0)

    scatters, pending, pool_grads, piece_grads = {}, {}, {}, {}

    def reduce_pieces(l, keys, got):
        for k, p in zip(keys, got):
            piece_grads[(k, l)] = _sum8(f"sum_grads_{k}_{l}", p.reshape(N_DEV, -1, p.shape[-1])).reshape(p.shape[1:])

    def start_scatter(name, sent):
        sent = {k: a.astype(BF16) for k, a in sent.items()}
        handles, token = _exchange_start(f"scatter_start_{name}", list(sent.values()), scatter=True)
        scatters[name] = (handles, list(sent), [lax.dynamic_index_in_dim(a, me, 0, keepdims=False) for a in sent.values()])
        return token[0, 0]

    def finish_scatter(name, l, after):
        handles, keys, own = scatters.pop(name)
        reduce_pieces(l, keys, own_slot(_exchange_wait(f"scatter_wait_{name}", handles, after, scatter=True), own))

    def push(l, part, big, after):
        cut = lambda a, n, axis: jnp.stack([lax.slice_in_dim(a, j * n, (j + 1) * n, axis=axis) for j in range(N_DEV)])
        sent = {}
        if part == "down":
            sent["w_down"] = big["w_down"].reshape(N_DEV, D_FF // N_DEV, D_MODEL)
        elif part == "up":
            sent["w_up"] = jnp.stack([lax.slice_in_dim(big[half_], j * up_cols, (j + 1) * up_cols, axis=1)
                                      for half_ in ("w_up_a", "w_up_v") for j in range(half)])
        elif l < N_A_LAYERS:
            pool_grads[l] = big["pool_w"]
        else:
            ext = cut(big["w_uq_ext"], Q_EXT, 1)
            rope = ext[..., QK_NOPE:QK_HEAD] + _unswap_halves(ext[..., QK_HEAD:])
            sent.update(w_dq=big["w_dq"].reshape(N_DEV, D_MODEL // N_DEV, Q_RANK), w_uq=jnp.concatenate([ext[..., :QK_NOPE], rope], axis=-1),
                        w_o=big["w_o"].reshape(N_DEV, D_MODEL // N_DEV, D_MODEL))
        if part == "mix" and l == N_A_LAYERS:
            sent.update(w_dkv=_fold_w_dkv_grad(big["w_dkv_ext"]).reshape(N_DEV, D_MODEL // N_DEV, KV_RANK + QK_ROPE),
                        w_uk=cut(big["w_uk"], QK_NOPE, 1), w_uv=cut(big["w_uv"], V_HEAD, 1))
        if l == 0 and part != "mix":
            return start_scatter(f"0_{part}", sent)
        if l == 0:
            finish_scatter("1", 1, after)
            pool = _shard8(jnp.stack([pool_grads[a] for a in range(N_A_LAYERS)]), 2).astype(BF16)
            reduce_pieces(0, ["pool_w"], _exchange_many("scatter_pool_grads", [pool], scatter=True))
            return 0.0
        pending.setdefault(l, {}).update(sent)
        if part != "mix":
            return 0.0
        if l + 1 < DEPTH:
            finish_scatter(str(l + 1), l + 1, after)
        return start_scatter(str(l), pending.pop(l))

    loss_row, dx, g, dmods = _forward_backward(x[0], loss_target[0], mods, tabq, tabk, final_g, fetch, push)
    layers_of = lambda k, ls: jnp.stack([piece_grads[(k, l)] for l in ls])
    grads = dict(w_dkv=piece_grads[("w_dkv", N_A_LAYERS)], w_uk=piece_grads[("w_uk", N_A_LAYERS)], w_uv=piece_grads[("w_uv", N_A_LAYERS)])
    for k in ("w_dq", "w_uq", "w_o"):
        grads[k] = layers_of(k, range(N_A_LAYERS, DEPTH))

    small_names = REPLICATED_WEIGHTS + tuple(k for k, _ in VECTOR_WEIGHTS)
    small_out = [dmods] + [g[k] for k in small_names] + [loss_row]
    small_shapes = [a.shape for a in small_out]
    small_got = _exchange("gather_small_grads", _pack(small_out, F32, SMALL_ROW_MULTIPLE), scatter=False)
    summed = _unpack(_sum8("sum_small_grads", small_got), small_shapes)
    grads["mod_b"] = summed[0]
    for k, s in zip(small_names, summed[1:-1]):
        grads[k] = s
    for k, ax in VECTOR_WEIGHTS:
        n = shard[k].shape[ax]
        grads[k] = lax.dynamic_slice_in_dim(grads[k], me * n, n, axis=ax)
    loss = summed[-1][0, 0]
    dmods_all = _unpack(small_got, small_shapes)[0]
    dm_mine = lax.dynamic_slice_in_dim(dmods_all, me * mod_cols, mod_cols, axis=2)
    dm_mine = jnp.pad(jnp.moveaxis(dm_mine, 0, 1), ((0, 0), (0, N_DEV), (0, 0)))
    grads["mod_w"] = _mods_bwd("mods_bwd", c_all, dm_mine)

    delta, new_m, new_v = {}, {}, {}

    def adamw(k):
        shp = shard[k].shape
        grads[k] = grads[k].reshape(shp)
        view = (lambda a: jnp.swapaxes(a, 1, 2)) if k == "w_up" else (lambda a: a)
        ops = [view(a) for a in (shard[k], grads[k], mom_m[k], mom_v[k])]
        res = _adamw(f"adamw_{k}", *[_as_2d(a) for a in ops])
        delta[k], new_m[k], new_v[k] = [view(r.reshape(ops[0].shape)) for r in res]
        grads[k] = view(ops[1])

    late = ("w_up", "w_down", "pool_w")
    for k in WEIGHT_ORDER:
        if k not in late:
            adamw(k)
    finish_scatter("0_down", 0, delta["final_g"])
    finish_scatter("0_up", 0, delta["final_g"])
    grads.update(w_up=layers_of("w_up", range(DEPTH)), w_down=layers_of("w_down", range(DEPTH)), pool_w=piece_grads[("pool_w", 0)])
    for k in late:
        adamw(k)
    return (loss, dx[None], *[grads[k] for k in WEIGHT_ORDER], *[delta[k] for k in WEIGHT_ORDER],
            *[new_m[k] for k in WEIGHT_ORDER], *[new_v[k] for k in WEIGHT_ORDER])
```

```python
import functools

import jax
import jax.numpy as jnp
from jax import lax
from jax.experimental import pallas as pl
from jax.experimental.pallas import tpu as pltpu

F32 = jnp.float32
BF16 = jnp.bfloat16

D_MODEL = 1024
DEPTH = 4
N_A_LAYERS = 2
N_B_LAYERS = 2
POOL_WINDOWS = (2, 4, 8, 16)
POOL_GROUP = 256
N_HEADS = 8
QK_NOPE = 128
QK_ROPE = 64
V_HEAD = 128
QK_HEAD = QK_NOPE + QK_ROPE
Q_RANK = 384
KV_RANK = 256
ROPE_THETA = 10000.0
D_FF = 2816
EPS = 1e-6
N_MOD = 6
ADAM_LR = 0.001
ADAM_B1 = 0.9
ADAM_B2 = 0.999
ADAM_EPS = 1e-08
ADAM_WD = 0.01
ADAM_STEP = 10

N_DEV = 8
LANES = 128
Q_EXT = 256
VMEM_LIMIT_BYTES = 48 * 1024 * 1024
MESH = pl.DeviceIdType.MESH
NEG_BIG = -0.7 * float(jnp.finfo(jnp.float32).max)


def _params(sem):
    return pltpu.CompilerParams(dimension_semantics=sem, vmem_limit_bytes=VMEM_LIMIT_BYTES)


def _tile(n, cap):
    if n <= cap:
        return n
    best = None
    for d in range(LANES, cap + 1, LANES):
        if n % d == 0:
            best = d
    assert best is not None, (n, cap)
    return best


def _dot(a, b, dims):
    return lax.dot_general(a, b, (dims, ((), ())), preferred_element_type=F32)


NN = ((1,), (0,))
NT = ((1,), (1,))
TN = ((0,), (0,))


def _mm(name, a, b, mode="nn", out_dtype=BF16, resid=None, gate=None, rowtab=None, second=None,
        tm_cap=1024, tn_cap=1408, tk_cap=1408):
    if mode == "tn":
        kdim, m = a.shape
    else:
        m, kdim = a.shape
    n = b.shape[0] if mode == "nt" else b.shape[1]
    tm, tn, tk = _tile(m, tm_cap), _tile(n, tn_cap), _tile(kdim, tk_cap)
    nk = kdim // tk
    dims = {"nn": NN, "nt": NT, "tn": TN}[mode]
    a_spec = pl.BlockSpec((tk, tm), lambda i, j, k: (k, i)) if mode == "tn" else pl.BlockSpec((tm, tk), lambda i, j, k: (i, k))
    b_spec = pl.BlockSpec((tn, tk), lambda i, j, k: (j, k)) if mode == "nt" else pl.BlockSpec((tk, tn), lambda i, j, k: (k, j))
    o_spec = pl.BlockSpec((tm, tn), lambda i, j, k: (i, j))
    g_spec = pl.BlockSpec((1, tn), lambda i, j, k: (0, j))
    gated = resid is not None

    n_ops = 2 if second is None else 4

    def body(*refs):
        acc = refs[-1]
        k = pl.program_id(2)

        @pl.when(k == 0)
        def _():
            acc[...] = jnp.zeros_like(acc)

        prod = _dot(refs[0][...].astype(BF16), refs[1][...].astype(BF16), dims)
        if second is not None:
            prod = prod + _dot(refs[2][...].astype(BF16), refs[3][...].astype(BF16), dims)
        acc[...] += prod

        @pl.when(k == nk - 1)
        def _():
            rest = refs[n_ops:-1]
            if gated:
                r_ref, g_ref, y_ref, x_ref = rest
                y_ref[...] = acc[...]
                x_ref[...] = r_ref[...] + g_ref[...] * acc[...]
            elif rowtab is not None:
                tab = rest[0][...]
                rest[1][...] = (acc[...] * jnp.concatenate([tab] * (tn // tab.shape[1]), axis=1)).astype(out_dtype)
            else:
                rest[0][...] = acc[...].astype(out_dtype)

    ins, in_specs = [a, b], [a_spec, b_spec]
    if second is not None:
        assert second[0].shape == a.shape and second[1].shape == b.shape
        ins += list(second)
        in_specs += [a_spec, b_spec]
    if rowtab is not None:
        assert tn % rowtab.shape[1] == 0 and not gated
        ins.append(rowtab)
        in_specs.append(pl.BlockSpec((tm, rowtab.shape[1]), lambda i, j, k: (i, 0)))
    if gated:
        ins += [resid, gate]
        in_specs += [o_spec, g_spec]
        out_shape = (jax.ShapeDtypeStruct((m, n), F32), jax.ShapeDtypeStruct((m, n), F32))
        out_specs = (o_spec, o_spec)
    else:
        out_shape = jax.ShapeDtypeStruct((m, n), out_dtype)
        out_specs = o_spec
    return pl.pallas_call(
        body, name=name, grid=(m // tm, n // tn, nk), in_specs=in_specs, out_specs=out_specs, out_shape=out_shape,
        scratch_shapes=[pltpu.VMEM((tm, tn), F32)],
        compiler_params=_params(("parallel", "parallel", "arbitrary")),
    )(*ins)


def _rowwise(name, fn, tiled, bcast, outs, sums=(), tr=512):
    tiled = [t if isinstance(t, tuple) else (t, t.shape[1], 0) for t in tiled]
    s = tiled[0][0].shape[0]
    tr = min(tr, s)
    assert s % tr == 0
    n_t, n_b, n_o = len(tiled), len(bcast), len(outs)

    def body(*refs):
        i = pl.program_id(0)
        vals = [r[...] for r in refs[:n_t + n_b]]
        o_vals, s_vals = fn(*vals)
        for r, v in zip(refs[n_t + n_b:n_t + n_b + n_o], o_vals):
            r[...] = v.astype(r.dtype)
        s_refs = refs[n_t + n_b + n_o:]

        @pl.when(i == 0)
        def _():
            for r in s_refs:
                r[...] = jnp.zeros_like(r)

        for r, v in zip(s_refs, s_vals):
            r[...] += v

    in_specs = [pl.BlockSpec((tr, n), functools.partial(lambda cb, i: (i, cb), cb)) for (_, n, cb) in tiled]
    in_specs += [pl.BlockSpec(b.shape, functools.partial(lambda nd, i: (0,) * nd, b.ndim)) for b in bcast]
    out_specs = [pl.BlockSpec((tr, n), lambda i: (i, 0)) for (n, _) in outs]
    out_specs += [pl.BlockSpec((1, n), lambda i: (0, 0)) for n in sums]
    out_shape = [jax.ShapeDtypeStruct((s, n), dt) for (n, dt) in outs]
    out_shape += [jax.ShapeDtypeStruct((1, n), F32) for n in sums]
    res = pl.pallas_call(
        body, name=name, grid=(s // tr,), in_specs=in_specs, out_specs=tuple(out_specs), out_shape=tuple(out_shape),
        compiler_params=_params(("arbitrary",)),
    )(*[t[0] for t in tiled], *bcast)
    return res


def _colsum(v):
    return jnp.sum(v, axis=0, keepdims=True)


def _rms_fwd(name, x, g, scale=None, shift=None, out_dtype=BF16, ncols=None):
    mod = scale is not None

    def fn(xv, gv, *ss):
        y = xv * lax.rsqrt(jnp.mean(xv * xv, axis=-1, keepdims=True) + EPS) * gv
        if mod:
            y = y * (1.0 + ss[0]) + ss[1]
        return (y,), ()

    n = ncols or x.shape[1]
    return _rowwise(name, fn, [(x, n, 0)], [g] + ([scale, shift] if mod else []), [(n, out_dtype)])[0]


def _rms_bwd(name, x, g, dh, scale=None, dx_in=None, ncols=None, out_dtype=F32):
    mod = scale is not None
    has_in = dx_in is not None

    def fn(*vals):
        xv, dhv = vals[0], vals[1].astype(F32)
        rest = list(vals[2:])
        dxi = rest.pop(0) if has_in else None
        gv = rest.pop(0)
        rstd = lax.rsqrt(jnp.mean(xv * xv, axis=-1, keepdims=True) + EPS)
        xhat = xv * rstd
        sums = []
        if mod:
            sc = rest.pop(0)
            dyn = dhv * (1.0 + sc)
            dshift, dscale = _colsum(dhv), _colsum(dhv * (xhat * gv))
        else:
            dyn = dhv
        dg = _colsum(dyn * xhat)
        dxhat = dyn * gv
        dx = rstd * (dxhat - xhat * jnp.mean(dxhat * xhat, axis=-1, keepdims=True))
        if has_in:
            dx = dx + dxi
        sums = [dg] + ([dshift, dscale] if mod else [])
        return (dx,), sums

    n = ncols or x.shape[1]
    tiled = [(x, n, 0), dh] + ([dx_in] if has_in else [])
    return _rowwise(name, fn, tiled, [g] + ([scale] if mod else []), [(n, out_dtype)], [n] * (3 if mod else 1))


def _gate_bwd(name, dxn, y, g):
    def fn(dv, yv, gv):
        return (gv * dv,), (_colsum(dv * yv),)

    n = dxn.shape[1]
    return _rowwise(name, fn, [dxn, y], [g], [(n, BF16)], [n])


def _loss_head(name, x, g, target):
    n = x.shape[1]

    def fn(xv, tv, gv):
        rstd = lax.rsqrt(jnp.mean(xv * xv, axis=-1, keepdims=True) + EPS)
        xhat = xv * rstd
        err = xhat * gv - tv
        loss = 0.5 * jnp.sum(jnp.sum(err * err, axis=-1, keepdims=True) / n, axis=0, keepdims=True)
        dy = err / n
        dg = _colsum(dy * xhat)
        dxhat = dy * gv
        dx = rstd * (dxhat - xhat * jnp.mean(dxhat * xhat, axis=-1, keepdims=True))
        return (dx,), (dg, jnp.broadcast_to(loss, (1, LANES)))

    return _rowwise(name, fn, [x, target], [g], [(n, F32)], [n, LANES])


def _krope_fwd(name, kv_ext, tabk):
    def fn(xv, tv):
        t = xv * tv
        return (t + pltpu.roll(t, 64, 1),), ()

    return _rowwise(name, fn, [(kv_ext, LANES, 2), tabk], [], [(LANES, BF16)])[0]


def _krope_bwd(name, dkd, tabk):
    def fn(dv, tv):
        d = dv[:, :LANES]
        for h in range(1, N_HEADS):
            d = d + dv[:, h * LANES:(h + 1) * LANES]
        return ((d + pltpu.roll(d, 64, 1)) * tv,), ()

    return _rowwise(name, fn, [dkd, tabk], [], [(LANES, F32)])[0]


def _adamw(name, w, g, m, v):
    def fn(wv, gv, mv, vv):
        m2 = ADAM_B1 * mv + (1.0 - ADAM_B1) * gv
        v2 = ADAM_B2 * vv + (1.0 - ADAM_B2) * (gv * gv)
        m_hat = m2 / (1.0 - ADAM_B1 ** ADAM_STEP)
        v_hat = v2 / (1.0 - ADAM_B2 ** ADAM_STEP)
        delta = -ADAM_LR * (m_hat / (jnp.sqrt(v_hat) + ADAM_EPS) + ADAM_WD * wv)
        return (delta, m2, v2), ()

    r, c = w.shape
    tr = r
    for cand in (512, 256, 128, 64, 32, 16, 8):
        if r % cand == 0 and r > cand:
            tr = cand
            break
    return _rowwise(name, fn, [w, g, m, v], [], [(c, F32)] * 3, tr=tr)


def _sum8(name, parts):
    _, r, c = parts.shape
    tr = r
    for cand in (2048, 1024, 512, 256, 128, 64, 32, 16):
        if r % cand == 0 and r > cand and cand * c <= 256 * 1024:
            tr = cand
            break

    def body(p_ref, o_ref):
        acc = p_ref[0].astype(F32)
        for k in range(1, N_DEV):
            acc = acc + p_ref[k].astype(F32)
        o_ref[...] = acc

    return pl.pallas_call(
        body, name=name, grid=(r // tr,), in_specs=[pl.BlockSpec((N_DEV, tr, c), lambda i: (0, i, 0))],
        out_specs=pl.BlockSpec((tr, c), lambda i: (i, 0)), out_shape=jax.ShapeDtypeStruct((r, c), F32),
        compiler_params=_params(("parallel",)),
    )(parts)


def _mods_fwd(name, c_all, w, b):
    depth, d, n = w.shape

    def body(c_ref, w_ref, b_ref, o_ref):
        cv = c_ref[...]
        sc = (cv * (1.0 / (1.0 + jnp.exp(-cv)))).astype(BF16)
        o_ref[0] = _dot(sc, w_ref[0].astype(BF16), NN) + b_ref[0]

    return pl.pallas_call(
        body, name=name, grid=(depth,),
        in_specs=[pl.BlockSpec(c_all.shape, lambda l: (0, 0)), pl.BlockSpec((1, d, n), lambda l: (l, 0, 0)),
                  pl.BlockSpec((1, 1, n), lambda l: (l, 0, 0))],
        out_specs=pl.BlockSpec((1, c_all.shape[0], n), lambda l: (l, 0, 0)),
        out_shape=jax.ShapeDtypeStruct((depth, c_all.shape[0], n), F32),
        compiler_params=_params(("parallel",)),
    )(c_all, w, b.reshape(depth, 1, n))


def _mods_bwd(name, c_all, dm):
    depth, rows, n = dm.shape
    d = c_all.shape[1]

    def body(c_ref, dm_ref, o_ref):
        cv = c_ref[...]
        sc = (cv * (1.0 / (1.0 + jnp.exp(-cv)))).astype(BF16)
        o_ref[0] = _dot(sc, dm_ref[0].astype(BF16), TN)

    return pl.pallas_call(
        body, name=name, grid=(depth,),
        in_specs=[pl.BlockSpec(c_all.shape, lambda l: (0, 0)), pl.BlockSpec((1, rows, n), lambda l: (l, 0, 0))],
        out_specs=pl.BlockSpec((1, d, n), lambda l: (l, 0, 0)),
        out_shape=jax.ShapeDtypeStruct((depth, d, n), F32),
        compiler_params=_params(("parallel",)),
    )(c_all, dm)


POOL_TILE = 256


def _split_dot(band, val):
    hi = val.astype(BF16)
    lo = (val - hi.astype(F32)).astype(BF16)
    return _dot(band, hi, NN) + _dot(band, lo, NN)


def _pool_fwd(name, h1, x, pw, pb, ps, g1):
    s, d = h1.shape
    t = POOL_TILE

    def body(hc_ref, hp_ref, x_ref, pw_ref, pb_ref, ps_ref, g_ref, xo_ref, zb_ref, pooled_ref):
        i = pl.program_id(0)
        r = lax.broadcasted_iota(jnp.int32, (t, t), 0)
        j = lax.broadcasted_iota(jnp.int32, (t, t), 1)
        pos = (i * t + lax.broadcasted_iota(jnp.int32, (t, 1), 0) + 1).astype(F32)
        has_prev = (i > 0).astype(F32)
        for grp, w in enumerate(POOL_WINDOWS):
            cs = slice(grp * POOL_GROUP, (grp + 1) * POOL_GROUP)
            hc = hc_ref[:, cs]
            band_cur = ((r - j >= 0) & (r - j < w)).astype(BF16)
            band_prev = (r + t - j < w).astype(BF16)
            ssum = _split_dot(band_cur, hc) + has_prev * _split_dot(band_prev, hp_ref[:, cs])
            pooled = (ssum / jnp.minimum(pos, float(w)) - hc).astype(BF16)
            zb = _dot(pooled, pw_ref[grp], NN) + pb_ref[:, cs]
            xo_ref[:, cs] = x_ref[:, cs] + g_ref[:, cs] * (zb * ps_ref[:, cs])
            zb_ref[:, cs] = zb
            pooled_ref[:, cs] = pooled

    row = pl.BlockSpec((t, d), lambda i: (i, 0))
    vec = pl.BlockSpec((1, d), lambda i: (0, 0))
    return pl.pallas_call(
        body, name=name, grid=(s // t,),
        in_specs=[row, pl.BlockSpec((t, d), lambda i: (jnp.maximum(i - 1, 0), 0)), row,
                  pl.BlockSpec(pw.shape, lambda i: (0, 0, 0)), vec, vec, vec],
        out_specs=(row, row, row),
        out_shape=(jax.ShapeDtypeStruct((s, d), F32), jax.ShapeDtypeStruct((s, d), F32), jax.ShapeDtypeStruct((s, d), BF16)),
        compiler_params=_params(("parallel",)),
    )(h1, h1, x, pw, pb, ps, g1)


def _pool_bwd(name, dxn, zb, pooled, pw, ps, g1):
    s, d = dxn.shape
    t = POOL_TILE
    nt = s // t

    def body(dc_ref, dn_ref, zb_ref, pooled_ref, pw_ref, ps_ref, g_ref, dh_ref, dpw_ref, dpb_ref, dps_ref, dg_ref):
        i = pl.program_id(0)

        @pl.when(i == 0)
        def _():
            dpw_ref[...] = jnp.zeros_like(dpw_ref)
            dpb_ref[...] = jnp.zeros_like(dpb_ref)
            dps_ref[...] = jnp.zeros_like(dps_ref)
            dg_ref[...] = jnp.zeros_like(dg_ref)

        jj = lax.broadcasted_iota(jnp.int32, (t, t), 0)
        rr = lax.broadcasted_iota(jnp.int32, (t, t), 1)
        pos = (i * t + lax.broadcasted_iota(jnp.int32, (t, 1), 0) + 1).astype(F32)
        has_next = (i < nt - 1).astype(F32)
        for grp, w in enumerate(POOL_WINDOWS):
            cs = slice(grp * POOL_GROUP, (grp + 1) * POOL_GROUP)
            gv, psv, zbv, dxc = g_ref[:, cs], ps_ref[:, cs], zb_ref[:, cs], dc_ref[:, cs]
            dg_ref[:, cs] += _colsum(dxc * (zbv * psv))
            dy = gv * dxc
            dps_ref[:, cs] += _colsum(dy * zbv)
            dz = dy * psv
            dpb_ref[:, cs] += _colsum(dz)
            dzb = dz.astype(BF16)
            dpw_ref[grp] += _dot(pooled_ref[:, cs], dzb, TN)
            dp = _dot(dzb, pw_ref[grp], NT)
            dzn = (gv * dn_ref[:, cs] * psv).astype(BF16)
            dpn = _dot(dzn, pw_ref[grp], NT) * (has_next / float(w))
            band_cur = ((rr - jj >= 0) & (rr - jj < w)).astype(BF16)
            band_next = (rr + t - jj < w).astype(BF16)
            dh_ref[:, cs] = _split_dot(band_cur, dp / jnp.minimum(pos, float(w))) + _split_dot(band_next, dpn) - dp

    row = pl.BlockSpec((t, d), lambda i: (i, 0))
    vec = pl.BlockSpec((1, d), lambda i: (0, 0))
    wspec = pl.BlockSpec(pw.shape, lambda i: (0, 0, 0))
    return pl.pallas_call(
        body, name=name, grid=(nt,),
        in_specs=[row, pl.BlockSpec((t, d), lambda i: (jnp.minimum(i + 1, nt - 1), 0)), row, row, wspec, vec, vec],
        out_specs=(row, wspec, vec, vec, vec),
        out_shape=(jax.ShapeDtypeStruct((s, d), F32), jax.ShapeDtypeStruct(pw.shape, F32),
                   jax.ShapeDtypeStruct((1, d), F32), jax.ShapeDtypeStruct((1, d), F32), jax.ShapeDtypeStruct((1, d), F32)),
        compiler_params=_params(("arbitrary",)),
    )(dxn, dxn, zb, pooled, pw, ps, g1)


GLU_TILE = 512
HALO = 16
INV_SQRT2 = 0.7071067811865476
INV_SQRT_2PI = 0.3989422804014327


def _up_glu_fwd(name, h2, wa, wv, cw, cb):
    s, d = h2.shape
    f = wa.shape[1]
    tm, tn = _tile(s, 1024), _tile(f, 1408)

    def body(h_ref, hh_ref, wa_ref, wv_ref, cw_ref, cb_ref, ua_ref, gl_ref, gpv_ref, ge_ref):
        i = pl.program_id(1)
        has_prev = (i > 0).astype(F32)
        a = _dot(h_ref[...], wa_ref[...], NN).astype(BF16)
        v = _dot(h_ref[...], wv_ref[...], NN)
        above = (_dot(hh_ref[...], wa_ref[...], NN) * has_prev).astype(BF16)
        ua_ref[...] = a
        ext = jnp.concatenate([above.astype(F32), a.astype(F32)], axis=0)
        e1 = pltpu.roll(ext, 1, 0)[HALO:]
        e2 = pltpu.roll(ext, 2, 0)[HALO:]
        pre = e2 * cw_ref[0:1, :] + e1 * cw_ref[1:2, :] + ext[HALO:] * cw_ref[2:3, :] + cb_ref[...]
        cdf = 0.5 * (1.0 + lax.erf(pre * INV_SQRT2))
        ge = pre * cdf
        gl_ref[...] = (ge * v).astype(gl_ref.dtype)
        gpv_ref[...] = ((cdf + pre * (INV_SQRT_2PI * jnp.exp(-0.5 * pre * pre))) * v).astype(gpv_ref.dtype)
        ge_ref[...] = ge.astype(ge_ref.dtype)

    blk = pl.BlockSpec((tm, tn), lambda j, i: (i, j))
    wspec = pl.BlockSpec((d, tn), lambda j, i: (0, j))
    return pl.pallas_call(
        body, name=name, grid=(f // tn, s // tm),
        in_specs=[pl.BlockSpec((tm, d), lambda j, i: (i, 0)), pl.BlockSpec((HALO, d), lambda j, i: (jnp.maximum(i * (tm // HALO) - 1, 0), 0)),
                  wspec, wspec, pl.BlockSpec((3, tn), lambda j, i: (0, j)), pl.BlockSpec((1, tn), lambda j, i: (0, j))],
        out_specs=(blk, blk, blk, blk), out_shape=tuple(jax.ShapeDtypeStruct((s, f), BF16) for _ in range(4)),
        compiler_params=_params(("parallel", "parallel")),
    )(h2, h2, wa, wv, cw, cb)


def _down_glu_bwd(name, dy2, wd, ua, gpv, ge, cw):
    s, f = ua.shape
    d = dy2.shape[1]
    t, tf = min(GLU_TILE, s), _tile(f, 1408)
    nt = s // t
    te = t + HALO

    def body(dy_ref, dyn_ref, wd_ref, a_ref, ah_ref, g_ref, gn_ref, ge_ref, cw_ref, da_ref, dv_ref, dcw_ref, dcb_ref):
        i = pl.program_id(1)

        @pl.when(i == 0)
        def _():
            dcw_ref[...] = jnp.zeros_like(dcw_ref)
            dcb_ref[...] = jnp.zeros_like(dcb_ref)

        has_prev = (i > 0).astype(F32)
        has_next = (i < nt - 1).astype(F32)
        wdv = wd_ref[...]
        dgl = _dot(dy_ref[...], wdv, NT)
        dgl_below = _dot(dyn_ref[...], wdv, NT) * has_next
        dpre = jnp.concatenate([dgl * g_ref[...].astype(F32), dgl_below * gn_ref[...].astype(F32)], axis=0)
        c0, c1, c2 = cw_ref[0:1, :], cw_ref[1:2, :], cw_ref[2:3, :]
        up1 = pltpu.roll(dpre, te - 1, 0)
        up2 = pltpu.roll(dpre, te - 2, 0)
        da_ref[...] = (dpre * c2 + up1 * c1 + up2 * c0)[:t].astype(da_ref.dtype)
        dv_ref[...] = (dgl * ge_ref[...].astype(F32)).astype(dv_ref.dtype)
        ext = jnp.concatenate([ah_ref[...].astype(F32) * has_prev, a_ref[...].astype(F32)], axis=0)
        dpt = dpre[:t]
        dcb_ref[...] += _colsum(dpt)
        dcw_ref[0:1, :] += _colsum(pltpu.roll(ext, 2, 0)[HALO:] * dpt)
        dcw_ref[1:2, :] += _colsum(pltpu.roll(ext, 1, 0)[HALO:] * dpt)
        dcw_ref[2:3, :] += _colsum(ext[HALO:] * dpt)

    blk = pl.BlockSpec((t, tf), lambda j, i: (i, j))
    prev = pl.BlockSpec((HALO, tf), lambda j, i: (jnp.maximum(i * (t // HALO) - 1, 0), j))
    below = lambda i: jnp.minimum((i + 1) * (t // HALO), s // HALO - 1)
    w3 = pl.BlockSpec((3, tf), lambda j, i: (0, j))
    w1 = pl.BlockSpec((1, tf), lambda j, i: (0, j))
    return pl.pallas_call(
        body, name=name, grid=(f // tf, nt),
        in_specs=[pl.BlockSpec((t, d), lambda j, i: (i, 0)), pl.BlockSpec((HALO, d), lambda j, i: (below(i), 0)),
                  pl.BlockSpec((tf, d), lambda j, i: (j, 0)), blk, prev, blk, pl.BlockSpec((HALO, tf), lambda j, i: (below(i), j)), blk, w3],
        out_specs=(blk, blk, w3, w1),
        out_shape=(jax.ShapeDtypeStruct((s, f), BF16), jax.ShapeDtypeStruct((s, f), BF16),
                   jax.ShapeDtypeStruct((3, f), F32), jax.ShapeDtypeStruct((1, f), F32)),
        compiler_params=_params(("parallel", "arbitrary")),
    )(dy2, dy2, wd, ua, ua, gpv, gpv, ge, cw)


ATT_TILE = 512
ATT_ROWS = 256
LOG2E = 1.4426950408889634
LN2 = 0.6931471805599453


def _head_blocks_t(a, width):
    s = a.shape[0]
    t = min(ATT_TILE, s)
    return a.reshape(s // t, t, N_HEADS, width).transpose(2, 0, 3, 1)


def _causal_mask(sv, q0, k0):
    row = q0 + lax.broadcasted_iota(jnp.int32, sv.shape, 0)
    col = k0 + lax.broadcasted_iota(jnp.int32, sv.shape, 1)
    return jnp.where(col <= row, sv, NEG_BIG)


def _attn_fwd(name, q_rot, kt4, v_ext):
    s = q_rot.shape[0]
    t = min(ATT_TILE, s)
    nq = s // t
    rq = min(ATT_ROWS, t)

    def body(q_ref, kt_ref, v_ref, o_ref, row_ref, acc_ref, m_ref):
        qi = pl.program_id(1)
        acc_ref[...] = jnp.zeros_like(acc_ref)
        m_ref[...] = jnp.full_like(m_ref, NEG_BIG)

        def step(j, masked):
            v_blk = v_ref[pl.ds(pl.multiple_of(j * t, t), t), :]
            for r in range(t // rq):
                rs = pl.ds(r * rq, rq)
                sv = _dot(q_ref[rs, :], kt_ref[0, j], NN)
                if masked:
                    sv = _causal_mask(sv, r * rq, 0)
                m_prev = m_ref[rs, :]
                m_new = jnp.maximum(m_prev, jnp.max(sv, axis=-1, keepdims=True))
                p = jnp.exp2(sv - m_new).astype(BF16)
                acc_ref[rs, :] = jnp.exp2(m_prev - m_new) * acc_ref[rs, :] + _dot(p, v_blk, NN)
                m_ref[rs, :] = m_new

        def full_step(j, carry):
            step(j, False)
            return carry

        lax.fori_loop(0, qi, full_step, 0)
        step(qi, True)
        l = acc_ref[:, V_HEAD:V_HEAD + 1]
        o_ref[...] = (acc_ref[:, :V_HEAD] / l).astype(o_ref.dtype)
        lse = jnp.broadcast_to(m_ref[...] + jnp.log(l) * LOG2E, (t, LANES))
        row_ref[0, 0] = jnp.transpose(lse)[0:8, :]

    head_q = pl.BlockSpec((t, Q_EXT), lambda h, i: (i, h))
    head_o = pl.BlockSpec((t, V_HEAD), lambda h, i: (i, h))
    return pl.pallas_call(
        body, name=name, grid=(N_HEADS, nq),
        in_specs=[head_q, pl.BlockSpec((1, nq, Q_EXT, t), lambda h, i: (h, 0, 0, 0)), pl.BlockSpec((s, Q_EXT), lambda h, i: (0, h))],
        out_specs=(head_o, pl.BlockSpec((1, 1, 8, t), lambda h, i: (h, i, 0, 0))),
        out_shape=(jax.ShapeDtypeStruct((s, N_HEADS * V_HEAD), BF16), jax.ShapeDtypeStruct((N_HEADS, nq, 8, t), F32)),
        scratch_shapes=[pltpu.VMEM((t, Q_EXT), F32), pltpu.VMEM((t, 1), F32)],
        compiler_params=_params(("parallel", "parallel")),
    )(q_rot, kt4, v_ext)


def _attn_delta(name, o, do):
    s = o.shape[0]
    t = min(ATT_TILE, s)

    def body(o_ref, do_ref, delta_ref):
        prod = do_ref[...].astype(F32) * o_ref[...].astype(F32)
        for h in range(N_HEADS):
            delta = jnp.sum(prod[:, h * V_HEAD:(h + 1) * V_HEAD], axis=-1, keepdims=True)
            delta_ref[h, 0] = jnp.transpose(jnp.broadcast_to(delta, (t, LANES)))[0:8, :]

    rows = pl.BlockSpec((t, N_HEADS * V_HEAD), lambda i: (i, 0))
    return pl.pallas_call(
        body, name=name, grid=(s // t,), in_specs=[rows, rows],
        out_specs=pl.BlockSpec((N_HEADS, 1, 8, t), lambda i: (0, i, 0, 0)),
        out_shape=jax.ShapeDtypeStruct((N_HEADS, s // t, 8, t), F32),
        compiler_params=_params(("parallel",)),
    )(o, do)


def _attn_bwd(name, kfull, v, qt4, q_rot, dot4, do, lse_row, delta_row, tabq, acc_in=None):
    s = kfull.shape[0]
    t = min(ATT_TILE, s)
    nq = s // t
    has_in = acc_in is not None

    def body(*refs):
        k_ref, v_ref, qt_ref, q_ref, dot_ref, do_ref, lse_ref, delta_ref, tab_ref = refs[:9]
        dq_ref, dkn_ref, dkd_ref, dv_ref, dq_acc_ref, acck_ref, accv_ref = refs[-7:]
        kj = pl.program_id(1)

        @pl.when(kj == 0)
        def _():
            dq_acc_ref[...] = jnp.zeros_like(dq_acc_ref)

        k_blk, v_blk = k_ref[...], v_ref[...]
        acck_ref[...] = jnp.zeros_like(acck_ref)
        accv_ref[...] = jnp.zeros_like(accv_ref)

        def step(i, masked):
            qs = pl.ds(pl.multiple_of(i * t, t), t)
            st = _dot(k_blk, qt_ref[0, i], NN)
            if masked:
                krow = lax.broadcasted_iota(jnp.int32, st.shape, 0)
                qcol = lax.broadcasted_iota(jnp.int32, st.shape, 1)
                st = jnp.where(krow <= qcol, st, NEG_BIG)
            pt = jnp.exp2(st - lse_ref[0, i, 0:1, :])
            accv_ref[...] += _dot(pt.astype(BF16), do_ref[qs, :], NN)
            dpt = _dot(v_blk, dot_ref[0, i], NN)
            dst = (pt * (dpt - delta_ref[0, i, 0:1, :])).astype(BF16)
            acck_ref[...] += _dot(dst, q_ref[qs, :], NN)
            dq_acc_ref[qs, :] += _dot(dst, k_blk, TN)

        def full_step(i, carry):
            step(i, False)
            return carry

        step(kj, True)
        lax.fori_loop(kj + 1, nq, full_step, 0)
        dk = acck_ref[...] * LN2
        if has_in:
            dkn_ref[...] = dk[:, :QK_NOPE] + refs[9][...]
            dkd_ref[...] = dk[:, QK_NOPE:] + refs[10][...]
            dv_ref[...] = accv_ref[...] + refs[11][...]
        else:
            dkn_ref[...] = dk[:, :QK_NOPE]
            dkd_ref[...] = dk[:, QK_NOPE:]
            dv_ref[...] = accv_ref[...]

        @pl.when(kj == nq - 1)
        def _():
            dq_ref[...] = (dq_acc_ref[...] * (tab_ref[...] * LN2)).astype(dq_ref.dtype)

    kblk = pl.BlockSpec((t, LANES), lambda h, j: (j, h))
    col = pl.BlockSpec((s, LANES), lambda h, j: (0, h))
    q_all = pl.BlockSpec((s, Q_EXT), lambda h, j: (0, h))
    stat = pl.BlockSpec((1, nq, 8, t), lambda h, j: (h, 0, 0, 0))
    ins = [kfull, v, qt4, q_rot, dot4, do, lse_row, delta_row, tabq]
    in_specs = [pl.BlockSpec((t, Q_EXT), lambda h, j: (j, h)), kblk, pl.BlockSpec((1, nq, Q_EXT, t), lambda h, j: (h, 0, 0, 0)),
                q_all, pl.BlockSpec((1, nq, V_HEAD, t), lambda h, j: (h, 0, 0, 0)), col, stat, stat,
                pl.BlockSpec((s, Q_EXT), lambda h, j: (0, 0))]
    if has_in:
        ins += list(acc_in)
        in_specs += [kblk, kblk, kblk]
    wide = jax.ShapeDtypeStruct((s, N_HEADS * LANES), F32)
    return pl.pallas_call(
        body, name=name, grid=(N_HEADS, nq), in_specs=in_specs, out_specs=(q_all, kblk, kblk, kblk),
        out_shape=(jax.ShapeDtypeStruct((s, N_HEADS * Q_EXT), BF16), wide, wide, wide),
        scratch_shapes=[pltpu.VMEM((s, Q_EXT), F32), pltpu.VMEM((t, Q_EXT), F32), pltpu.VMEM((t, LANES), F32)],
        compiler_params=_params(("parallel", "arbitrary")),
    )(*ins)


def _swap_halves(w):
    half = w.shape[-1] // 2
    return jnp.concatenate([-w[..., half:], w[..., :half]], axis=-1)


def _unswap_halves(g):
    half = g.shape[-1] // 2
    return jnp.concatenate([g[..., half:], -g[..., :half]], axis=-1)


def _extend_w_uq(w):
    r = w.reshape(Q_RANK, N_HEADS, QK_HEAD)
    rope = r[..., QK_NOPE:]
    return jnp.concatenate([r[..., :QK_NOPE], rope, _swap_halves(rope)], axis=-1).reshape(Q_RANK, N_HEADS * Q_EXT)


def _fold_w_uq_grad(g):
    r = g.reshape(Q_RANK, N_HEADS, Q_EXT)
    rope = r[..., QK_NOPE:QK_HEAD] + _unswap_halves(r[..., QK_HEAD:])
    return jnp.concatenate([r[..., :QK_NOPE], rope], axis=-1).reshape(Q_RANK, N_HEADS * QK_HEAD)


def _extend_w_dkv(w):
    return jnp.concatenate([w, _swap_halves(w[:, KV_RANK:])], axis=-1)


def _fold_w_dkv_grad(g):
    rope = g[:, KV_RANK:KV_RANK + QK_ROPE] + _unswap_halves(g[:, KV_RANK + QK_ROPE:])
    return jnp.concatenate([g[:, :KV_RANK], rope], axis=-1)


def _rope_tables(positions):
    inv = 1.0 / (ROPE_THETA ** (jnp.arange(0, QK_ROPE, 2, dtype=F32) / QK_ROPE))
    ang = positions.astype(F32)[:, None] * inv
    cos, sin = jnp.cos(ang), jnp.sin(ang)
    tabk = jnp.concatenate([cos, cos, sin, sin], axis=-1)
    scale = QK_HEAD ** -0.5 * LOG2E
    tabq = jnp.concatenate([jnp.full((positions.shape[0], QK_NOPE), scale, F32), tabk * scale], axis=-1)
    return tabq, tabk


def _forward_backward(x, target, mods, tabq, tabk, final_g, fetch, push):
    row = lambda vec: vec.reshape(1, -1)
    mod = [[row(mods[l, k * D_MODEL:(k + 1) * D_MODEL]) for k in range(N_MOD)] for l in range(DEPTH)]
    saved, weights = [], []
    kv = None
    for l in range(DEPTH):
        w, tok = fetch(l, x)
        sh1, sc1, g1, sh2, sc2, g2 = mod[l]
        sh1 = sh1 + tok
        if l == N_A_LAYERS:
            kvn = _rms_fwd("kvin_fwd", x, row(w["kv_in_g"]))
            kv_ext = _mm("dkv_fwd", kvn, w["w_dkv_ext"], out_dtype=F32)
            ckv = _rms_fwd("ckv_fwd", kv_ext, row(w["ckv_norm_g"]), ncols=KV_RANK)
            kd = _krope_fwd("krope_fwd", kv_ext, tabk)
            kn, v = _mm("uk_fwd", ckv, w["w_uk"]), _mm("uv_fwd", ckv, w["w_uv"])
            heads = lambda a: [a[:, h * LANES:(h + 1) * LANES] for h in range(N_HEADS)]
            kfull = jnp.concatenate([part for kh in heads(kn) for part in (kh, kd)], axis=-1)
            v_ext = jnp.concatenate([part for vh in heads(v) for part in (vh, jnp.ones_like(vh))], axis=-1)
            kv = dict(x=x, kvn=kvn, kv_ext=kv_ext, ckv=ckv, v=v, kfull=kfull, v_ext=v_ext,
                      kt4=_head_blocks_t(kfull, Q_EXT))
        x_in = x
        if l < N_A_LAYERS:
            h1 = _rms_fwd(f"norm1_fwd_{l}", x, row(w["norm1_g"]), sc1, sh1, out_dtype=F32)
            x_mid, zb, pooled = _pool_fwd(f"pool_fwd_{l}", h1, x, w["pool_w"], row(w["pool_b"]), row(w["pool_scale"]), g1)
            mix = (zb, pooled)
        else:
            h1 = _rms_fwd(f"norm1_fwd_{l}", x, row(w["norm1_g"]), sc1, sh1)
            cq_pre = _mm(f"dq_fwd_{l}", h1, w["w_dq"], out_dtype=F32)
            cq = _rms_fwd(f"qnorm_fwd_{l}", cq_pre, row(w["q_norm_g"]))
            q_rot = _mm(f"uq_fwd_{l}", cq, w["w_uq_ext"], rowtab=tabq)
            o, lse_row = _attn_fwd(f"attn_fwd_{l}", q_rot, kv["kt4"], kv["v_ext"])
            y, x_mid = _mm(f"wo_fwd_{l}", o, w["w_o"], resid=x, gate=g1)
            mix = (h1, cq_pre, cq, q_rot, o, lse_row, y)
        h2 = _rms_fwd(f"norm2_fwd_{l}", x_mid, row(w["norm2_g"]), sc2, sh2)
        w_up_a, w_up_v = w["w_up"](h2)
        ua, gl, gpv, ge = _up_glu_fwd(f"up_glu_fwd_{l}", h2, w_up_a, w_up_v, w["conv_w"], row(w["conv_b"]))
        w_down = w["w_down"](gl)
        y2, x = _mm(f"down_fwd_{l}", gl, w_down, resid=x_mid, gate=g2)
        saved.append((x_in, x_mid, h2, ua, gpv, ge, gl, y2, mix))
        weights.append(dict(w, w_up_a=w_up_a, w_up_v=w_up_v, w_down=w_down))

    dx, dfinal_g, loss = _loss_head("loss_head", x, row(final_g), target)
    g = {"final_g": dfinal_g.reshape(-1)}
    per_layer = {k: [None] * DEPTH for k in ("norm1_g", "norm2_g", "conv_w", "conv_b")}
    per_a = {k: [None] * N_A_LAYERS for k in ("pool_b", "pool_scale")}
    per_b = {k: [None] * N_B_LAYERS for k in ("q_norm_g",)}
    dmods = [None] * DEPTH
    dkv = None
    tok = 0.0
    for l in reversed(range(DEPTH)):
        w, big = weights[l], {}
        sh1, sc1, g1, sh2, sc2, g2 = mod[l]
        g2 = g2 + tok
        x_in, x_mid, h2, ua, gpv, ge, gl, y2, mix = saved[l]
        dy2, dg2 = _gate_bwd(f"gate2_bwd_{l}", dx, y2, g2)
        tok = push(l, "down", dict(w_down=_mm(f"down_wgrad_{l}", gl, dy2, mode="tn", tm_cap=1408)), None)
        da, dv_, dcw, dcb = _down_glu_bwd(f"down_glu_bwd_{l}", dy2, w["w_down"], ua, gpv, ge, w["conv_w"] + tok)
        dh2 = _mm(f"up_bwd_{l}", da, w["w_up_a"], mode="nt", out_dtype=F32, second=(dv_, w["w_up_v"]))
        tok = push(l, "up", dict(w_up_a=_mm(f"up_a_wgrad_{l}", h2, da, mode="tn"), w_up_v=_mm(f"up_v_wgrad_{l}", h2, dv_, mode="tn")), None)
        per_layer["conv_w"][l], per_layer["conv_b"][l] = dcw, dcb.reshape(-1)
        dx_mid, dn2, dsh2, dsc2 = _rms_bwd(f"norm2_bwd_{l}", x_mid, row(w["norm2_g"]), dh2, sc2 + tok, dx_in=dx)
        per_layer["norm2_g"][l] = dn2.reshape(-1)
        if l < N_A_LAYERS:
            zb, pooled = mix
            dh1, dpw, dpb, dps, dg1 = _pool_bwd(f"pool_bwd_{l}", dx_mid, zb, pooled, w["pool_w"], row(w["pool_scale"]), g1)
            big["pool_w"] = dpw
            per_a["pool_b"][l], per_a["pool_scale"][l] = dpb.reshape(-1), dps.reshape(-1)
        else:
            j = l - N_A_LAYERS
            h1, cq_pre, cq, q_rot, o, lse_row, y = mix
            dy, dg1 = _gate_bwd(f"gate1_bwd_{l}", dx_mid, y, g1)
            do = _mm(f"wo_bwd_{l}", dy, w["w_o"], mode="nt")
            big["w_o"] = _mm(f"wo_wgrad_{l}", o, dy, mode="tn")
            delta_row = _attn_delta(f"attn_delta_{l}", o, do)
            dq_ext, *dkv = _attn_bwd(f"attn_bwd_{l}", kv["kfull"], kv["v"], _head_blocks_t(q_rot, Q_EXT), q_rot, _head_blocks_t(do, V_HEAD), do,
                                     lse_row, delta_row, tabq, acc_in=dkv)
            dcq = _mm(f"uq_bwd_{l}", dq_ext, w["w_uq_ext"], mode="nt", out_dtype=F32)
            big["w_uq_ext"] = _mm(f"uq_wgrad_{l}", cq, dq_ext, mode="tn", out_dtype=F32)
            dcq_pre, dqn = _rms_bwd(f"qnorm_bwd_{l}", cq_pre, row(w["q_norm_g"]), dcq, out_dtype=BF16)
            per_b["q_norm_g"][j] = dqn.reshape(-1)
            dh1 = _mm(f"dq_bwd_{l}", dcq_pre, w["w_dq"], mode="nt")
            big["w_dq"] = _mm(f"dq_wgrad_{l}", h1, dcq_pre, mode="tn")
        dx, dn1, dsh1, dsc1 = _rms_bwd(f"norm1_bwd_{l}", x_in, row(w["norm1_g"]), dh1, sc1, dx_in=dx_mid)
        per_layer["norm1_g"][l] = dn1.reshape(-1)
        dmods[l] = jnp.concatenate([dsh1, dsc1, dg1, dsh2, dsc2, dg2], axis=-1).reshape(-1)
        if l == N_A_LAYERS:
            dkn, dkd, dv = dkv
            dckv = _mm("ukv_bwd", dkn, w["w_uk"], mode="nt", out_dtype=F32, second=(dv, w["w_uv"]))
            big["w_uk"] = _mm("uk_wgrad", kv["ckv"], dkn, mode="tn")
            big["w_uv"] = _mm("uv_wgrad", kv["ckv"], dv, mode="tn")
            dkr = _krope_bwd("krope_bwd", dkd, tabk)
            dc, dckv_g = _rms_bwd("ckv_bwd", kv["kv_ext"], row(w["ckv_norm_g"]), dckv, ncols=KV_RANK, out_dtype=BF16)
            dkv_ext = jnp.concatenate([dc, dkr.astype(BF16)], axis=-1)
            dkvn = _mm("dkv_bwd", dkv_ext, w["w_dkv_ext"], mode="nt")
            big["w_dkv_ext"] = _mm("dkv_wgrad", kv["kvn"], dkv_ext, mode="tn", out_dtype=F32)
            dx, dkv_in_g = _rms_bwd("kvin_bwd", kv["x"], row(w["kv_in_g"]), dkvn, dx_in=dx)
            g["ckv_norm_g"], g["kv_in_g"] = dckv_g.reshape(-1), dkv_in_g.reshape(-1)
        tok = push(l, "mix", big, dx)
    for group in (per_layer, per_a, per_b):
        for k, vals in group.items():
            g[k] = jnp.stack(vals)
    return loss, dx, g, jnp.stack(dmods)


def _my_index():
    return 4 * lax.axis_index("x") + 2 * lax.axis_index("y") + lax.axis_index("c")


def _peer(k):
    x, y, c = lax.axis_index("x"), lax.axis_index("y"), lax.axis_index("c")
    return (1 - x if k & 4 else x, 1 - y if k & 2 else y, 1 - c if k & 1 else c)


def _index_of(pos):
    return 4 * pos[0] + 2 * pos[1] + pos[2]


def _exchange_many(name, arrays, scatter):
    n = len(arrays)
    blocks = [tuple(a.shape[1:]) if scatter else tuple(a.shape) for a in arrays]

    def body(*refs):
        x_refs, o_refs = refs[:n], refs[n:2 * n]
        send_sems, recv_sems, local_sems = refs[2 * n:]
        me = _my_index()
        started = []
        for a in range(n):
            mine = pltpu.make_async_copy(x_refs[a].at[me] if scatter else x_refs[a], o_refs[a].at[me], local_sems.at[a])
            mine.start()
            started.append(mine)
        sends = []
        for k in range(1, N_DEV):
            peer = _peer(k)
            for a in range(n):
                cp = pltpu.make_async_remote_copy(
                    src_ref=x_refs[a].at[_index_of(peer)] if scatter else x_refs[a], dst_ref=o_refs[a].at[me],
                    send_sem=send_sems.at[a, k - 1], recv_sem=recv_sems.at[a, k - 1], device_id=peer, device_id_type=MESH)
                cp.start()
                sends.append(cp)
        for k in range(1, N_DEV):
            peer = _peer(k)
            for a in range(n):
                pltpu.make_async_remote_copy(
                    src_ref=x_refs[a].at[me] if scatter else x_refs[a], dst_ref=o_refs[a].at[_index_of(peer)],
                    send_sem=send_sems.at[a, k - 1], recv_sem=recv_sems.at[a, k - 1], device_id=peer, device_id_type=MESH).wait_recv()
        for cp in sends:
            cp.wait_send()
        for mine in started:
            mine.wait()

    return pl.pallas_call(
        body, name=name, out_shape=tuple(jax.ShapeDtypeStruct((N_DEV,) + blk, a.dtype) for blk, a in zip(blocks, arrays)),
        in_specs=[pl.BlockSpec(memory_space=pl.ANY)] * n, out_specs=tuple([pl.BlockSpec(memory_space=pl.ANY)] * n),
        scratch_shapes=[pltpu.SemaphoreType.DMA((n, N_DEV - 1)), pltpu.SemaphoreType.DMA((n, N_DEV - 1)), pltpu.SemaphoreType.DMA((n,))],
    )(*arrays)


def _exchange(name, x, scatter):
    return _exchange_many(name, [x], scatter)[0]


HBM_SPEC = pl.BlockSpec(memory_space=pltpu.HBM)
SEM_SPEC = pl.BlockSpec(memory_space=pltpu.SEMAPHORE)
DATAFLOW = pltpu.SideEffectType.DATAFLOW_SIDE_EFFECTING


def _remote_copies(x_refs, land_refs, send_sems, recv_sems, scatter, numbers=None):
    me = _my_index()
    numbers = list(range(len(x_refs))) if numbers is None else numbers
    out, inc = [], []
    for a in range(len(x_refs)):
        for k in range(1, N_DEV):
            peer = _peer(k)
            pair = numbers[a] * (N_DEV - 1) + k - 1
            sems = dict(send_sem=send_sems.at[pair], recv_sem=recv_sems.at[pair], device_id=peer, device_id_type=MESH)
            out.append(pltpu.make_async_remote_copy(
                src_ref=x_refs[a].at[_index_of(peer)] if scatter else x_refs[a], dst_ref=land_refs[a].at[me], **sems))
            inc.append(pltpu.make_async_remote_copy(
                src_ref=x_refs[a].at[me] if scatter else x_refs[a], dst_ref=land_refs[a].at[_index_of(peer)], **sems))
    return out, inc


def _exchange_start(name, arrays, scatter):
    n = len(arrays)
    blocks = [tuple(a.shape[1:]) if scatter else tuple(a.shape) for a in arrays]

    def body(*refs):
        x_refs, land_refs = refs[:n], refs[n:2 * n]
        send_sems, recv_sems = refs[2 * n], refs[2 * n + 1]
        for cp in _remote_copies(x_refs, land_refs, send_sems, recv_sems, scatter)[0]:
            cp.start()
        refs[-1][...] = jnp.zeros_like(refs[-1])

    sem_type = pltpu.SemaphoreType.DMA((n * (N_DEV - 1),))
    lands =[pltpu.with_memory_space_constraint(lax.empty((N_DEV,) + blk, a.dtype), pltpu.HBM) for blk, a in zip(blocks, arrays)]
    srcs = [pltpu.with_memory_space_constraint(a, pltpu.HBM) for a in arrays]
    res = pl.pallas_call(
        body, name=name,
        out_shape=(sem_type, sem_type, *[pltpu.HBM(a.shape, a.dtype) for a in srcs + lands], jax.ShapeDtypeStruct((8, LANES), F32)),
        in_specs=[HBM_SPEC] * (2 * n), out_specs=(SEM_SPEC, SEM_SPEC, *[HBM_SPEC] * (2 * n), pl.BlockSpec(memory_space=pltpu.VMEM)),
        input_output_aliases={i: 2 + i for i in range(2 * n)},
        compiler_params=pltpu.CompilerParams(has_side_effects=DATAFLOW),
    )(*srcs, *lands)
    return (res[0], res[1], list(res[2:2 + n]), list(res[2 + n:2 + 2 * n])), res[-1]


def _exchange_wait(name, handles, after, scatter, which=None):
    send_sems, recv_sems, srcs, lands = handles
    which = list(range(len(srcs))) if which is None else list(which)
    srcs, lands = [srcs[a] for a in which], [lands[a] for a in which]
    n = len(srcs)

    def body(*refs):
        x_refs, land_refs = refs[:n], refs[n:2 * n]
        out, inc = _remote_copies(x_refs, land_refs, refs[2 * n], refs[2 * n + 1], scatter, which)
        for cp in out:
            cp.wait_send()
        for cp in inc:
            cp.wait_recv()

    res = pl.pallas_call(
        body, name=name, out_shape=tuple(pltpu.HBM(a.shape, a.dtype) for a in srcs + lands),
        in_specs=[HBM_SPEC] * (2 * n) + [SEM_SPEC, SEM_SPEC, pl.BlockSpec(memory_space=pl.ANY)], out_specs=tuple([HBM_SPEC] * (2 * n)),
        input_output_aliases={i: i for i in range(2 * n)},
        compiler_params=pltpu.CompilerParams(has_side_effects=DATAFLOW),
    )(*srcs, *lands, send_sems, recv_sems, after)
    return list(res[n:])


def _pack(arrays, dtype, row_multiple):
    flat = jnp.concatenate([a.astype(dtype).reshape(-1) for a in arrays])
    rows = -(-flat.shape[0] // (LANES * row_multiple)) * row_multiple
    return jnp.pad(flat, (0, rows * LANES - flat.shape[0])).reshape(rows, LANES)


def _unpack(packed, shapes):
    lead = packed.shape[:-2]
    flat = packed.reshape(lead + (-1,))
    out, off = [], 0
    for shp in shapes:
        size = 1
        for d in shp:
            size *= d
        out.append(flat[..., off:off + size].reshape(lead + tuple(shp)))
        off += size
    return out


def _unshard(g8, axis):
    return jnp.concatenate([g8[j] for j in range(N_DEV)], axis=axis)


def _shard8(full, axis):
    n = full.shape[axis] // N_DEV
    return jnp.stack([lax.slice_in_dim(full, j * n, (j + 1) * n, axis=axis) for j in range(N_DEV)])


VECTOR_WEIGHTS = (("pool_b", 1), ("pool_scale", 1), ("conv_w", 2))
REPLICATED_WEIGHTS = ("norm1_g", "norm2_g", "kv_in_g", "ckv_norm_g", "q_norm_g", "conv_b", "final_g")
WEIGHT_ORDER = ("mod_w", "mod_b", "norm1_g", "norm2_g", "pool_w", "pool_b", "pool_scale", "kv_in_g", "w_dkv", "ckv_norm_g", "w_uk",
                "w_uv", "w_dq", "q_norm_g", "w_uq", "w_o", "w_up", "conv_w", "conv_b", "w_down", "final_g")
BIG_ROW_MULTIPLE = 1024
SMALL_ROW_MULTIPLE = 16


def _as_2d(a):
    if a.ndim == 1:
        return a.reshape(-1, LANES)
    return a.reshape(-1, a.shape[-1])


def kernel(x, c, positions, mod_w, mod_b, norm1_g, norm2_g, pool_w, pool_b, pool_scale, kv_in_g, w_dkv, ckv_norm_g, w_uk, w_uv, w_dq, q_norm_g, w_uq, w_o, w_up, conv_w, conv_b, w_down, final_g, loss_target, m_mod_w, m_mod_b, m_norm1_g, m_norm2_g, m_pool_w, m_pool_b, m_pool_scale, m_kv_in_g, m_w_dkv, m_ckv_norm_g, m_w_uk, m_w_uv, m_w_dq, m_q_norm_g, m_w_uq, m_w_o, m_w_up, m_conv_w, m_conv_b, m_w_down, m_final_g, v_mod_w, v_mod_b, v_norm1_g, v_norm2_g, v_pool_w, v_pool_b, v_pool_scale, v_kv_in_g, v_w_dkv, v_ckv_norm_g, v_w_uk, v_w_uv, v_w_dq, v_q_norm_g, v_w_uq, v_w_o, v_w_up, v_conv_w, v_conv_b, v_w_down, v_final_g):
    shard = dict(mod_w=mod_w, mod_b=mod_b, norm1_g=norm1_g, norm2_g=norm2_g, pool_w=pool_w, pool_b=pool_b, pool_scale=pool_scale,
                 kv_in_g=kv_in_g, w_dkv=w_dkv, ckv_norm_g=ckv_norm_g, w_uk=w_uk, w_uv=w_uv, w_dq=w_dq, q_norm_g=q_norm_g, w_uq=w_uq,
                 w_o=w_o, w_up=w_up, conv_w=conv_w, conv_b=conv_b, w_down=w_down, final_g=final_g)
    mom_m = dict(mod_w=m_mod_w, mod_b=m_mod_b, norm1_g=m_norm1_g, norm2_g=m_norm2_g, pool_w=m_pool_w, pool_b=m_pool_b,
                 pool_scale=m_pool_scale, kv_in_g=m_kv_in_g, w_dkv=m_w_dkv, ckv_norm_g=m_ckv_norm_g, w_uk=m_w_uk, w_uv=m_w_uv,
                 w_dq=m_w_dq, q_norm_g=m_q_norm_g, w_uq=m_w_uq, w_o=m_w_o, w_up=m_w_up, conv_w=m_conv_w, conv_b=m_conv_b,
                 w_down=m_w_down, final_g=m_final_g)
    mom_v = dict(mod_w=v_mod_w, mod_b=v_mod_b, norm1_g=v_norm1_g, norm2_g=v_norm2_g, pool_w=v_pool_w, pool_b=v_pool_b,
                 pool_scale=v_pool_scale, kv_in_g=v_kv_in_g, w_dkv=v_w_dkv, ckv_norm_g=v_ckv_norm_g, w_uk=v_w_uk, w_uv=v_w_uv,
                 w_dq=v_w_dq, q_norm_g=v_q_norm_g, w_uq=v_w_uq, w_o=v_w_o, w_up=v_w_up, conv_w=v_conv_w, conv_b=v_conv_b,
                 w_down=v_w_down, final_g=v_final_g)
    me = _my_index()
    d6 = N_MOD * D_MODEL
    mod_cols = d6 // N_DEV

    small_in = [c] + [shard[k] for k, _ in VECTOR_WEIGHTS]
    small_all = _exchange("gather_vectors", _pack(small_in, F32, SMALL_ROW_MULTIPLE), scatter=False)
    parts = _unpack(small_all, [a.shape for a in small_in])
    c_all = jnp.pad(parts[0].reshape(N_DEV, D_MODEL), ((0, N_DEV), (0, 0)))
    vec = {k: _unshard(p, ax) for (k, ax), p in zip(VECTOR_WEIGHTS, parts[1:])}

    my_mod_b = lax.dynamic_slice_in_dim(mod_b, me * mod_cols, mod_cols, axis=1)
    mods_mine = _mods_fwd("mods_fwd", c_all, mod_w, my_mod_b)
    mods_all = _exchange("gather_mods", _pack([mods_mine], F32, SMALL_ROW_MULTIPLE), scatter=False)
    mods_all = _unpack(mods_all, [mods_mine.shape])[0]
    mods = lax.dynamic_index_in_dim(mods_all, me, axis=2, keepdims=False)
    mods = jnp.moveaxis(mods, 0, 1).reshape(DEPTH, d6)

    tabq, tabk = _rope_tables(positions[0])
    half = N_DEV // 2
    up_cols = shard["w_up"].shape[2]
    cat = lambda a, axis, lo=0, hi=N_DEV: jnp.concatenate([a[j] for j in range(lo, hi)], axis=axis)

    def stage_pieces(l):
        out = {"pool_w": shard["pool_w"].astype(BF16)} if l == 0 else {}
        if l == N_A_LAYERS:
            out.update({k: shard[k].astype(BF16) for k in ("w_dkv", "w_uk", "w_uv")})
        if l >= N_A_LAYERS:
            out.update({k: shard[k][l - N_A_LAYERS].astype(BF16) for k in ("w_dq", "w_uq", "w_o")})
        out.update(w_up=shard["w_up"][l].astype(BF16), w_down=shard["w_down"][l].astype(BF16))
        return out

    gathers, pool_all = {}, []

    def start_gather(l, behind=None):
        pieces = stage_pieces(l)
        if behind is not None:
            pieces, _ = lax.optimization_barrier((pieces, behind))
        handles, token = _exchange_start(f"gather_start_{l}", list(pieces.values()), scatter=False)
        gathers[l] = (handles, pieces)
        return token[0, 0]

    def wait_gather(l, keys, after, tag=""):
        handles, pieces = gathers[l]
        which = [list(pieces).index(k) for k in keys]
        lands = _exchange_wait(f"gather_wait_{l}{tag}", handles, after, scatter=False, which=which)
        return dict(zip(keys, own_slot(lands, [pieces[k] for k in keys])))

    def whole_weights(l, got):
        w = dict(norm1_g=norm1_g[l], norm2_g=norm2_g[l], conv_w=vec["conv_w"][l], conv_b=conv_b[l])
        if l == 0:
            pool_all.append(got["pool_w"])
        if l < N_A_LAYERS:
            w.update(pool_w=cat(pool_all[0][:, l], 1), pool_b=vec["pool_b"][l], pool_scale=vec["pool_scale"][l])
        else:
            rope = got["w_uq"][..., QK_NOPE:]
            ext = jnp.concatenate([got["w_uq"][..., :QK_NOPE], rope, _swap_halves(rope)], axis=-1)
            w.update(w_dq=got["w_dq"].reshape(D_MODEL, Q_RANK), w_uq_ext=cat(ext, -1), w_o=got["w_o"].reshape(D_MODEL, D_MODEL),
                     q_norm_g=q_norm_g[l - N_A_LAYERS])
        if l == N_A_LAYERS:
            w.update(w_dkv_ext=_extend_w_dkv(got["w_dkv"].reshape(D_MODEL, KV_RANK + QK_ROPE)), w_uk=cat(got["w_uk"], -1),
                     w_uv=cat(got["w_uv"], -1), kv_in_g=kv_in_g, ckv_norm_g=ckv_norm_g)
        return w

    def own_slot(lands, own):
        return [lax.dynamic_update_index_in_dim(p, o, me, 0) for p, o in zip(lands, own)]

    def fetch(l, after):
        up_parts = lambda g8: (cat(g8, -1, 0, half), cat(g8, -1, half, N_DEV))
        if l == 0:
            start_gather(0, behind=mods)
            got = wait_gather(0, ["pool_w"], mods, "_pool")
            w_up = lambda aft: up_parts(wait_gather(0, ["w_up"], aft, "_up")["w_up"])
            w_down = lambda aft: wait_gather(0, ["w_down"], aft, "_down")["w_down"].reshape(D_FF, D_MODEL)
        else:
            got = wait_gather(l, list(gathers[l][1]), after)
            up, down = up_parts(got["w_up"]), got["w_down"].reshape(D_FF, D_MODEL)
            w_up, w_down = (lambda aft: up), (lambda aft: down)
        w = dict(whole_weights(l, got), w_up=w_up, w_down=w_down)
        return w, (start_gather(l + 1) if l + 1 < DEPTH else 0.0)

    scatters, pending, pool_grads, piece_grads = {}, {}, {}, {}

    def reduce_pieces(l, keys, got):
        for k, p in zip(keys, got):
            piece_grads[(k, l)] = _sum8(f"sum_grads_{k}_{l}", p.reshape(N_DEV, -1, p.shape[-1])).reshape(p.shape[1:])

    def start_scatter(name, sent):
        sent = {k: a.astype(BF16) for k, a in sent.items()}
        handles, token = _exchange_start(f"scatter_start_{name}", list(sent.values()), scatter=True)
        scatters[name] = (handles, list(sent), [lax.dynamic_index_in_dim(a, me, 0, keepdims=False) for a in sent.values()])
        return token[0, 0]

    def finish_scatter(name, l, after):
        handles, keys, own = scatters.pop(name)
        reduce_pieces(l, keys, own_slot(_exchange_wait(f"scatter_wait_{name}", handles, after, scatter=True), own))

    def push(l, part, big, after):
        cut = lambda a, n, axis: jnp.stack([lax.slice_in_dim(a, j * n, (j + 1) * n, axis=axis) for j in range(N_DEV)])
        sent = {}
        if part == "down":
            sent["w_down"] = big["w_down"].reshape(N_DEV, D_FF // N_DEV, D_MODEL)
        elif part == "up":
            sent["w_up"] = jnp.stack([lax.slice_in_dim(big[half_], j * up_cols, (j + 1) * up_cols, axis=1)
                                      for half_ in ("w_up_a", "w_up_v") for j in range(half)])
        elif l < N_A_LAYERS:
            pool_grads[l] = big["pool_w"]
        else:
            ext = cut(big["w_uq_ext"], Q_EXT, 1)
            rope = ext[..., QK_NOPE:QK_HEAD] + _unswap_halves(ext[..., QK_HEAD:])
            sent.update(w_dq=big["w_dq"].reshape(N_DEV, D_MODEL // N_DEV, Q_RANK), w_uq=jnp.concatenate([ext[..., :QK_NOPE], rope], axis=-1),
                        w_o=big["w_o"].reshape(N_DEV, D_MODEL // N_DEV, D_MODEL))
        if part == "mix" and l == N_A_LAYERS:
            sent.update(w_dkv=_fold_w_dkv_grad(big["w_dkv_ext"]).reshape(N_DEV, D_MODEL // N_DEV, KV_RANK + QK_ROPE),
                        w_uk=cut(big["w_uk"], QK_NOPE, 1), w_uv=cut(big["w_uv"], V_HEAD, 1))
        if l == 0 and part != "mix":
            return start_scatter(f"0_{part}", sent)
        if l == 0:
            finish_scatter("1", 1, after)
            pool = _shard8(jnp.stack([pool_grads[a] for a in range(N_A_LAYERS)]), 2).astype(BF16)
            reduce_pieces(0, ["pool_w"], _exchange_many("scatter_pool_grads", [pool], scatter=True))
            return 0.0
        pending.setdefault(l, {}).update(sent)
        if part != "mix":
            return 0.0
        if l + 1 < DEPTH:
            finish_scatter(str(l + 1), l + 1, after)
        return start_scatter(str(l), pending.pop(l))

    loss_row, dx, g, dmods = _forward_backward(x[0], loss_target[0], mods, tabq, tabk, final_g, fetch, push)
    layers_of = lambda k, ls: jnp.stack([piece_grads[(k, l)] for l in ls])
    grads = dict(w_dkv=piece_grads[("w_dkv", N_A_LAYERS)], w_uk=piece_grads[("w_uk", N_A_LAYERS)], w_uv=piece_grads[("w_uv", N_A_LAYERS)])
    for k in ("w_dq", "w_uq", "w_o"):
        grads[k] = layers_of(k, range(N_A_LAYERS, DEPTH))

    small_names = REPLICATED_WEIGHTS + tuple(k for k, _ in VECTOR_WEIGHTS)
    small_out = [dmods] + [g[k] for k in small_names] + [loss_row]
    small_shapes = [a.shape for a in small_out]
    small_got = _exchange("gather_small_grads", _pack(small_out, F32, SMALL_ROW_MULTIPLE), scatter=False)
    summed = _unpack(_sum8("sum_small_grads", small_got), small_shapes)
    grads["mod_b"] = summed[0]
    for k, s in zip(small_names, summed[1:-1]):
        grads[k] = s
    for k, ax in VECTOR_WEIGHTS:
        n = shard[k].shape[ax]
        grads[k] = lax.dynamic_slice_in_dim(grads[k], me * n, n, axis=ax)
    loss = summed[-1][0, 0]
    dmods_all = _unpack(small_got, small_shapes)[0]
    dm_mine = lax.dynamic_slice_in_dim(dmods_all, me * mod_cols, mod_cols, axis=2)
    dm_mine = jnp.pad(jnp.moveaxis(dm_mine, 0, 1), ((0, 0), (0, N_DEV), (0, 0)))
    grads["mod_w"] = _mods_bwd("mods_bwd", c_all, dm_mine)

    delta, new_m, new_v = {}, {}, {}

    def adamw(k):
        shp = shard[k].shape
        grads[k] = grads[k].reshape(shp)
        view = (lambda a: jnp.swapaxes(a, 1, 2)) if k == "w_up" else (lambda a: a)
        ops = [view(a) for a in (shard[k], grads[k], mom_m[k], mom_v[k])]
        res = _adamw(f"adamw_{k}", *[_as_2d(a) for a in ops])
        delta[k], new_m[k], new_v[k] = [view(r.reshape(ops[0].shape)) for r in res]
        grads[k] = view(ops[1])

    late = ("w_up", "w_down", "pool_w")
    for k in WEIGHT_ORDER:
        if k not in late:
            adamw(k)
    finish_scatter("0_down", 0, delta["final_g"])
    finish_scatter("0_up", 0, delta["final_g"])
    grads.update(w_up=layers_of("w_up", range(DEPTH)), w_down=layers_of("w_down", range(DEPTH)), pool_w=piece_grads[("pool_w", 0)])
    for k in late:
        adamw(k)
    return (loss, dx[None], *[grads[k] for k in WEIGHT_ORDER], *[delta[k] for k in WEIGHT_ORDER],
            *[new_m[k] for k in WEIGHT_ORDER], *[new_v[k] for k in WEIGHT_ORDER])
```

```python
import functools

import jax
import jax.numpy as jnp
from jax import lax
from jax.experimental import pallas as pl
from jax.experimental.pallas import tpu as pltpu

F32 = jnp.float32
BF16 = jnp.bfloat16

D_MODEL = 1024
DEPTH = 4
N_A_LAYERS = 2
N_B_LAYERS = 2
POOL_WINDOWS = (2, 4, 8, 16)
POOL_GROUP = 256
N_HEADS = 8
QK_NOPE = 128
QK_ROPE = 64
V_HEAD = 128
QK_HEAD = QK_NOPE + QK_ROPE
Q_RANK = 384
KV_RANK = 256
ROPE_THETA = 10000.0
D_FF = 2816
EPS = 1e-6
N_MOD = 6
ADAM_LR = 0.001
ADAM_B1 = 0.9
ADAM_B2 = 0.999
ADAM_EPS = 1e-08
ADAM_WD = 0.01
ADAM_STEP = 10

N_DEV = 8
LANES = 128
Q_EXT = 256
VMEM_LIMIT_BYTES = 48 * 1024 * 1024
MESH = pl.DeviceIdType.MESH
NEG_BIG = -0.7 * float(jnp.finfo(jnp.float32).max)


def _params(sem):
    return pltpu.CompilerParams(dimension_semantics=sem, vmem_limit_bytes=VMEM_LIMIT_BYTES)


def _tile(n, cap):
    if n <= cap:
        return n
    best = None
    for d in range(LANES, cap + 1, LANES):
        if n % d == 0:
            best = d
    assert best is not None, (n, cap)
    return best


def _dot(a, b, dims):
    return lax.dot_general(a, b, (dims, ((), ())), preferred_element_type=F32)


NN = ((1,), (0,))
NT = ((1,), (1,))
TN = ((0,), (0,))


def _mm(name, a, b, mode="nn", out_dtype=BF16, resid=None, gate=None, rowtab=None, second=None,
        tm_cap=1024, tn_cap=1408, tk_cap=1408):
    if mode == "tn":
        kdim, m = a.shape
    else:
        m, kdim = a.shape
    n = b.shape[0] if mode == "nt" else b.shape[1]
    tm, tn, tk = _tile(m, tm_cap), _tile(n, tn_cap), _tile(kdim, tk_cap)
    nk = kdim // tk
    dims = {"nn": NN, "nt": NT, "tn": TN}[mode]
    a_spec = pl.BlockSpec((tk, tm), lambda i, j, k: (k, i)) if mode == "tn" else pl.BlockSpec((tm, tk), lambda i, j, k: (i, k))
    b_spec = pl.BlockSpec((tn, tk), lambda i, j, k: (j, k)) if mode == "nt" else pl.BlockSpec((tk, tn), lambda i, j, k: (k, j))
    o_spec = pl.BlockSpec((tm, tn), lambda i, j, k: (i, j))
    g_spec = pl.BlockSpec((1, tn), lambda i, j, k: (0, j))
    gated = resid is not None

    n_ops = 2 if second is None else 4

    def body(*refs):
        acc = refs[-1]
        k = pl.program_id(2)

        @pl.when(k == 0)
        def _():
            acc[...] = jnp.zeros_like(acc)

        prod = _dot(refs[0][...].astype(BF16), refs[1][...].astype(BF16), dims)
        if second is not None:
            prod = prod + _dot(refs[2][...].astype(BF16), refs[3][...].astype(BF16), dims)
        acc[...] += prod

        @pl.when(k == nk - 1)
        def _():
            rest = refs[n_ops:-1]
            if gated:
                r_ref, g_ref, y_ref, x_ref = rest
                y_ref[...] = acc[...]
                x_ref[...] = r_ref[...] + g_ref[...] * acc[...]
            elif rowtab is not None:
                tab = rest[0][...]
                rest[1][...] = (acc[...] * jnp.concatenate([tab] * (tn // tab.shape[1]), axis=1)).astype(out_dtype)
            else:
                rest[0][...] = acc[...].astype(out_dtype)

    ins, in_specs = [a, b], [a_spec, b_spec]
    if second is not None:
        assert second[0].shape == a.shape and second[1].shape == b.shape
        ins += list(second)
        in_specs += [a_spec, b_spec]
    if rowtab is not None:
        assert tn % rowtab.shape[1] == 0 and not gated
        ins.append(rowtab)
        in_specs.append(pl.BlockSpec((tm, rowtab.shape[1]), lambda i, j, k: (i, 0)))
    if gated:
        ins += [resid, gate]
        in_specs += [o_spec, g_spec]
        out_shape = (jax.ShapeDtypeStruct((m, n), F32), jax.ShapeDtypeStruct((m, n), F32))
        out_specs = (o_spec, o_spec)
    else:
        out_shape = jax.ShapeDtypeStruct((m, n), out_dtype)
        out_specs = o_spec
    return pl.pallas_call(
        body, name=name, grid=(m // tm, n // tn, nk), in_specs=in_specs, out_specs=out_specs, out_shape=out_shape,
        scratch_shapes=[pltpu.VMEM((tm, tn), F32)],
        compiler_params=_params(("parallel", "parallel", "arbitrary")),
    )(*ins)


def _rowwise(name, fn, tiled, bcast, outs, sums=(), tr=512):
    tiled = [t if isinstance(t, tuple) else (t, t.shape[1], 0) for t in tiled]
    s = tiled[0][0].shape[0]
    tr = min(tr, s)
    assert s % tr == 0
    n_t, n_b, n_o = len(tiled), len(bcast), len(outs)

    def body(*refs):
        i = pl.program_id(0)
        vals = [r[...] for r in refs[:n_t + n_b]]
        o_vals, s_vals = fn(*vals)
        for r, v in zip(refs[n_t + n_b:n_t + n_b + n_o], o_vals):
            r[...] = v.astype(r.dtype)
        s_refs = refs[n_t + n_b + n_o:]

        @pl.when(i == 0)
        def _():
            for r in s_refs:
                r[...] = jnp.zeros_like(r)

        for r, v in zip(s_refs, s_vals):
            r[...] += v

    in_specs = [pl.BlockSpec((tr, n), functools.partial(lambda cb, i: (i, cb), cb)) for (_, n, cb) in tiled]
    in_specs += [pl.BlockSpec(b.shape, functools.partial(lambda nd, i: (0,) * nd, b.ndim)) for b in bcast]
    out_specs = [pl.BlockSpec((tr, n), lambda i: (i, 0)) for (n, _) in outs]
    out_specs += [pl.BlockSpec((1, n), lambda i: (0, 0)) for n in sums]
    out_shape = [jax.ShapeDtypeStruct((s, n), dt) for (n, dt) in outs]
    out_shape += [jax.ShapeDtypeStruct((1, n), F32) for n in sums]
    res = pl.pallas_call(
        body, name=name, grid=(s // tr,), in_specs=in_specs, out_specs=tuple(out_specs), out_shape=tuple(out_shape),
        compiler_params=_params(("arbitrary",)),
    )(*[t[0] for t in tiled], *bcast)
    return res


def _colsum(v):
    return jnp.sum(v, axis=0, keepdims=True)


def _rms_fwd(name, x, g, scale=None, shift=None, out_dtype=BF16, ncols=None):
    mod = scale is not None

    def fn(xv, gv, *ss):
        y = xv * lax.rsqrt(jnp.mean(xv * xv, axis=-1, keepdims=True) + EPS) * gv
        if mod:
            y = y * (1.0 + ss[0]) + ss[1]
        return (y,), ()

    n = ncols or x.shape[1]
    return _rowwise(name, fn, [(x, n, 0)], [g] + ([scale, shift] if mod else []), [(n, out_dtype)])[0]


def _rms_bwd(name, x, g, dh, scale=None, dx_in=None, ncols=None, out_dtype=F32):
    mod = scale is not None
    has_in = dx_in is not None

    def fn(*vals):
        xv, dhv = vals[0], vals[1].astype(F32)
        rest = list(vals[2:])
        dxi = rest.pop(0) if has_in else None
        gv = rest.pop(0)
        rstd = lax.rsqrt(jnp.mean(xv * xv, axis=-1, keepdims=True) + EPS)
        xhat = xv * rstd
        sums = []
        if mod:
            sc = rest.pop(0)
            dyn = dhv * (1.0 + sc)
            dshift, dscale = _colsum(dhv), _colsum(dhv * (xhat * gv))
        else:
            dyn = dhv
        dg = _colsum(dyn * xhat)
        dxhat = dyn * gv
        dx = rstd * (dxhat - xhat * jnp.mean(dxhat * xhat, axis=-1, keepdims=True))
        if has_in:
            dx = dx + dxi
        sums = [dg] + ([dshift, dscale] if mod else [])
        return (dx,), sums

    n = ncols or x.shape[1]
    tiled = [(x, n, 0), dh] + ([dx_in] if has_in else [])
    return _rowwise(name, fn, tiled, [g] + ([scale] if mod else []), [(n, out_dtype)], [n] * (3 if mod else 1))


def _gate_bwd(name, dxn, y, g):
    def fn(dv, yv, gv):
        return (gv * dv,), (_colsum(dv * yv),)

    n = dxn.shape[1]
    return _rowwise(name, fn, [dxn, y], [g], [(n, BF16)], [n])


def _loss_head(name, x, g, target):
    n = x.shape[1]

    def fn(xv, tv, gv):
        rstd = lax.rsqrt(jnp.mean(xv * xv, axis=-1, keepdims=True) + EPS)
        xhat = xv * rstd
        err = xhat * gv - tv
        loss = 0.5 * jnp.sum(jnp.sum(err * err, axis=-1, keepdims=True) / n, axis=0, keepdims=True)
        dy = err / n
        dg = _colsum(dy * xhat)
        dxhat = dy * gv
        dx = rstd * (dxhat - xhat * jnp.mean(dxhat * xhat, axis=-1, keepdims=True))
        return (dx,), (dg, jnp.broadcast_to(loss, (1, LANES)))

    return _rowwise(name, fn, [x, target], [g], [(n, F32)], [n, LANES])


def _krope_fwd(name, kv_ext, tabk):
    def fn(xv, tv):
        t = xv * tv
        return (t + pltpu.roll(t, 64, 1),), ()

    return _rowwise(name, fn, [(kv_ext, LANES, 2), tabk], [], [(LANES, BF16)])[0]


def _krope_bwd(name, dkd, tabk):
    def fn(dv, tv):
        d = dv[:, :LANES]
        for h in range(1, N_HEADS):
            d = d + dv[:, h * LANES:(h + 1) * LANES]
        return ((d + pltpu.roll(d, 64, 1)) * tv,), ()

    return _rowwise(name, fn, [dkd, tabk], [], [(LANES, F32)])[0]


def _adamw(name, w, g, m, v):
    def fn(wv, gv, mv, vv):
        m2 = ADAM_B1 * mv + (1.0 - ADAM_B1) * gv
        v2 = ADAM_B2 * vv + (1.0 - ADAM_B2) * (gv * gv)
        m_hat = m2 / (1.0 - ADAM_B1 ** ADAM_STEP)
        v_hat = v2 / (1.0 - ADAM_B2 ** ADAM_STEP)
        delta = -ADAM_LR * (m_hat / (jnp.sqrt(v_hat) + ADAM_EPS) + ADAM_WD * wv)
        return (delta, m2, v2), ()

    r, c = w.shape
    tr = r
    for cand in (512, 256, 128, 64, 32, 16, 8):
        if r % cand == 0 and r > cand:
            tr = cand
            break
    return _rowwise(name, fn, [w, g, m, v], [], [(c, F32)] * 3, tr=tr)


def _sum8(name, parts):
    _, r, c = parts.shape
    tr = r
    for cand in (2048, 1024, 512, 256, 128, 64, 32, 16):
        if r % cand == 0 and r > cand and cand * c <= 256 * 1024:
            tr = cand
            break

    def body(p_ref, o_ref):
        acc = p_ref[0].astype(F32)
        for k in range(1, N_DEV):
            acc = acc + p_ref[k].astype(F32)
        o_ref[...] = acc

    return pl.pallas_call(
        body, name=name, grid=(r // tr,), in_specs=[pl.BlockSpec((N_DEV, tr, c), lambda i: (0, i, 0))],
        out_specs=pl.BlockSpec((tr, c), lambda i: (i, 0)), out_shape=jax.ShapeDtypeStruct((r, c), F32),
        compiler_params=_params(("parallel",)),
    )(parts)


def _mods_fwd(name, c_all, w, b):
    depth, d, n = w.shape

    def body(c_ref, w_ref, b_ref, o_ref):
        cv = c_ref[...]
        sc = (cv * (1.0 / (1.0 + jnp.exp(-cv)))).astype(BF16)
        o_ref[0] = _dot(sc, w_ref[0].astype(BF16), NN) + b_ref[0]

    return pl.pallas_call(
        body, name=name, grid=(depth,),
        in_specs=[pl.BlockSpec(c_all.shape, lambda l: (0, 0)), pl.BlockSpec((1, d, n), lambda l: (l, 0, 0)),
                  pl.BlockSpec((1, 1, n), lambda l: (l, 0, 0))],
        out_specs=pl.BlockSpec((1, c_all.shape[0], n), lambda l: (l, 0, 0)),
        out_shape=jax.ShapeDtypeStruct((depth, c_all.shape[0], n), F32),
        compiler_params=_params(("parallel",)),
    )(c_all, w, b.reshape(depth, 1, n))


def _mods_bwd(name, c_all, dm):
    depth, rows, n = dm.shape
    d = c_all.shape[1]

    def body(c_ref, dm_ref, o_ref):
        cv = c_ref[...]
        sc = (cv * (1.0 / (1.0 + jnp.exp(-cv)))).astype(BF16)
        o_ref[0] = _dot(sc, dm_ref[0].astype(BF16), TN)

    return pl.pallas_call(
        body, name=name, grid=(depth,),
        in_specs=[pl.BlockSpec(c_all.shape, lambda l: (0, 0)), pl.BlockSpec((1, rows, n), lambda l: (l, 0, 0))],
        out_specs=pl.BlockSpec((1, d, n), lambda l: (l, 0, 0)),
        out_shape=jax.ShapeDtypeStruct((depth, d, n), F32),
        compiler_params=_params(("parallel",)),
    )(c_all, dm)


POOL_TILE = 256


def _split_dot(band, val):
    hi = val.astype(BF16)
    lo = (val - hi.astype(F32)).astype(BF16)
    return _dot(band, hi, NN) + _dot(band, lo, NN)


def _pool_fwd(name, h1, x, pw, pb, ps, g1):
    s, d = h1.shape
    t = POOL_TILE

    def body(hc_ref, hp_ref, x_ref, pw_ref, pb_ref, ps_ref, g_ref, xo_ref, zb_ref, pooled_ref):
        i = pl.program_id(0)
        r = lax.broadcasted_iota(jnp.int32, (t, t), 0)
        j = lax.broadcasted_iota(jnp.int32, (t, t), 1)
        pos = (i * t + lax.broadcasted_iota(jnp.int32, (t, 1), 0) + 1).astype(F32)
        has_prev = (i > 0).astype(F32)
        for grp, w in enumerate(POOL_WINDOWS):
            cs = slice(grp * POOL_GROUP, (grp + 1) * POOL_GROUP)
            hc = hc_ref[:, cs]
            band_cur = ((r - j >= 0) & (r - j < w)).astype(BF16)
            band_prev = (r + t - j < w).astype(BF16)
            ssum = _split_dot(band_cur, hc) + has_prev * _split_dot(band_prev, hp_ref[:, cs])
            pooled = (ssum / jnp.minimum(pos, float(w)) - hc).astype(BF16)
            zb = _dot(pooled, pw_ref[grp], NN) + pb_ref[:, cs]
            xo_ref[:, cs] = x_ref[:, cs] + g_ref[:, cs] * (zb * ps_ref[:, cs])
            zb_ref[:, cs] = zb
            pooled_ref[:, cs] = pooled

    row = pl.BlockSpec((t, d), lambda i: (i, 0))
    vec = pl.BlockSpec((1, d), lambda i: (0, 0))
    return pl.pallas_call(
        body, name=name, grid=(s // t,),
        in_specs=[row, pl.BlockSpec((t, d), lambda i: (jnp.maximum(i - 1, 0), 0)), row,
                  pl.BlockSpec(pw.shape, lambda i: (0, 0, 0)), vec, vec, vec],
        out_specs=(row, row, row),
        out_shape=(jax.ShapeDtypeStruct((s, d), F32), jax.ShapeDtypeStruct((s, d), F32), jax.ShapeDtypeStruct((s, d), BF16)),
        compiler_params=_params(("parallel",)),
    )(h1, h1, x, pw, pb, ps, g1)


def _pool_bwd(name, dxn, zb, pooled, pw, ps, g1):
    s, d = dxn.shape
    t = POOL_TILE
    nt = s // t

    def body(dc_ref, dn_ref, zb_ref, pooled_ref, pw_ref, ps_ref, g_ref, dh_ref, dpw_ref, dpb_ref, dps_ref, dg_ref):
        i = pl.program_id(0)

        @pl.when(i == 0)
        def _():
            dpw_ref[...] = jnp.zeros_like(dpw_ref)
            dpb_ref[...] = jnp.zeros_like(dpb_ref)
            dps_ref[...] = jnp.zeros_like(dps_ref)
            dg_ref[...] = jnp.zeros_like(dg_ref)

        jj = lax.broadcasted_iota(jnp.int32, (t, t), 0)
        rr = lax.broadcasted_iota(jnp.int32, (t, t), 1)
        pos = (i * t + lax.broadcasted_iota(jnp.int32, (t, 1), 0) + 1).astype(F32)
        has_next = (i < nt - 1).astype(F32)
        for grp, w in enumerate(POOL_WINDOWS):
            cs = slice(grp * POOL_GROUP, (grp + 1) * POOL_GROUP)
            gv, psv, zbv, dxc = g_ref[:, cs], ps_ref[:, cs], zb_ref[:, cs], dc_ref[:, cs]
            dg_ref[:, cs] += _colsum(dxc * (zbv * psv))
            dy = gv * dxc
            dps_ref[:, cs] += _colsum(dy * zbv)
            dz = dy * psv
            dpb_ref[:, cs] += _colsum(dz)
            dzb = dz.astype(BF16)
            dpw_ref[grp] += _dot(pooled_ref[:, cs], dzb, TN)
            dp = _dot(dzb, pw_ref[grp], NT)
            dzn = (gv * dn_ref[:, cs] * psv).astype(BF16)
            dpn = _dot(dzn, pw_ref[grp], NT) * (has_next / float(w))
            band_cur = ((rr - jj >= 0) & (rr - jj < w)).astype(BF16)
            band_next = (rr + t - jj < w).astype(BF16)
            dh_ref[:, cs] = _split_dot(band_cur, dp / jnp.minimum(pos, float(w))) + _split_dot(band_next, dpn) - dp

    row = pl.BlockSpec((t, d), lambda i: (i, 0))
    vec = pl.BlockSpec((1, d), lambda i: (0, 0))
    wspec = pl.BlockSpec(pw.shape, lambda i: (0, 0, 0))
    return pl.pallas_call(
        body, name=name, grid=(nt,),
        in_specs=[row, pl.BlockSpec((t, d), lambda i: (jnp.minimum(i + 1, nt - 1), 0)), row, row, wspec, vec, vec],
        out_specs=(row, wspec, vec, vec, vec),
        out_shape=(jax.ShapeDtypeStruct((s, d), F32), jax.ShapeDtypeStruct(pw.shape, F32),
                   jax.ShapeDtypeStruct((1, d), F32), jax.ShapeDtypeStruct((1, d), F32), jax.ShapeDtypeStruct((1, d), F32)),
        compiler_params=_params(("arbitrary",)),
    )(dxn, dxn, zb, pooled, pw, ps, g1)


GLU_TILE = 512
HALO = 16
INV_SQRT2 = 0.7071067811865476
INV_SQRT_2PI = 0.3989422804014327


def _up_glu_fwd(name, h2, wa, wv, cw, cb):
    s, d = h2.shape
    f = wa.shape[1]
    tm, tn = _tile(s, 1024), _tile(f, 1408)

    def body(h_ref, hh_ref, wa_ref, wv_ref, cw_ref, cb_ref, ua_ref, gl_ref, gpv_ref, ge_ref):
        i = pl.program_id(1)
        has_prev = (i > 0).astype(F32)
        a = _dot(h_ref[...], wa_ref[...], NN).astype(BF16)
        v = _dot(h_ref[...], wv_ref[...], NN)
        above = (_dot(hh_ref[...], wa_ref[...], NN) * has_prev).astype(BF16)
        ua_ref[...] = a
        ext = jnp.concatenate([above.astype(F32), a.astype(F32)], axis=0)
        e1 = pltpu.roll(ext, 1, 0)[HALO:]
        e2 = pltpu.roll(ext, 2, 0)[HALO:]
        pre = e2 * cw_ref[0:1, :] + e1 * cw_ref[1:2, :] + ext[HALO:] * cw_ref[2:3, :] + cb_ref[...]
        cdf = 0.5 * (1.0 + lax.erf(pre * INV_SQRT2))
        ge = pre * cdf
        gl_ref[...] = (ge * v).astype(gl_ref.dtype)
        gpv_ref[...] = ((cdf + pre * (INV_SQRT_2PI * jnp.exp(-0.5 * pre * pre))) * v).astype(gpv_ref.dtype)
        ge_ref[...] = ge.astype(ge_ref.dtype)

    blk = pl.BlockSpec((tm, tn), lambda j, i: (i, j))
    wspec = pl.BlockSpec((d, tn), lambda j, i: (0, j))
    return pl.pallas_call(
        body, name=name, grid=(f // tn, s // tm),
        in_specs=[pl.BlockSpec((tm, d), lambda j, i: (i, 0)), pl.BlockSpec((HALO, d), lambda j, i: (jnp.maximum(i * (tm // HALO) - 1, 0), 0)),
                  wspec, wspec, pl.BlockSpec((3, tn), lambda j, i: (0, j)), pl.BlockSpec((1, tn), lambda j, i: (0, j))],
        out_specs=(blk, blk, blk, blk), out_shape=tuple(jax.ShapeDtypeStruct((s, f), BF16) for _ in range(4)),
        compiler_params=_params(("parallel", "parallel")),
    )(h2, h2, wa, wv, cw, cb)


def _down_glu_bwd(name, dy2, wd, ua, gpv, ge, cw):
    s, f = ua.shape
    d = dy2.shape[1]
    t, tf = min(GLU_TILE, s), _tile(f, 1408)
    nt = s // t
    te = t + HALO

    def body(dy_ref, dyn_ref, wd_ref, a_ref, ah_ref, g_ref, gn_ref, ge_ref, cw_ref, da_ref, dv_ref, dcw_ref, dcb_ref):
        i = pl.program_id(1)

        @pl.when(i == 0)
        def _():
            dcw_ref[...] = jnp.zeros_like(dcw_ref)
            dcb_ref[...] = jnp.zeros_like(dcb_ref)

        has_prev = (i > 0).astype(F32)
        has_next = (i < nt - 1).astype(F32)
        wdv = wd_ref[...]
        dgl = _dot(dy_ref[...], wdv, NT)
        dgl_below = _dot(dyn_ref[...], wdv, NT) * has_next
        dpre = jnp.concatenate([dgl * g_ref[...].astype(F32), dgl_below * gn_ref[...].astype(F32)], axis=0)
        c0, c1, c2 = cw_ref[0:1, :], cw_ref[1:2, :], cw_ref[2:3, :]
        up1 = pltpu.roll(dpre, te - 1, 0)
        up2 = pltpu.roll(dpre, te - 2, 0)
        da_ref[...] = (dpre * c2 + up1 * c1 + up2 * c0)[:t].astype(da_ref.dtype)
        dv_ref[...] = (dgl * ge_ref[...].astype(F32)).astype(dv_ref.dtype)
        ext = jnp.concatenate([ah_ref[...].astype(F32) * has_prev, a_ref[...].astype(F32)], axis=0)
        dpt = dpre[:t]
        dcb_ref[...] += _colsum(dpt)
        dcw_ref[0:1, :] += _colsum(pltpu.roll(ext, 2, 0)[HALO:] * dpt)
        dcw_ref[1:2, :] += _colsum(pltpu.roll(ext, 1, 0)[HALO:] * dpt)
        dcw_ref[2:3, :] += _colsum(ext[HALO:] * dpt)

    blk = pl.BlockSpec((t, tf), lambda j, i: (i, j))
    prev = pl.BlockSpec((HALO, tf), lambda j, i: (jnp.maximum(i * (t // HALO) - 1, 0), j))
    below = lambda i: jnp.minimum((i + 1) * (t // HALO), s // HALO - 1)
    w3 = pl.BlockSpec((3, tf), lambda j, i: (0, j))
    w1 = pl.BlockSpec((1, tf), lambda j, i: (0, j))
    return pl.pallas_call(
        body, name=name, grid=(f // tf, nt),
        in_specs=[pl.BlockSpec((t, d), lambda j, i: (i, 0)), pl.BlockSpec((HALO, d), lambda j, i: (below(i), 0)),
                  pl.BlockSpec((tf, d), lambda j, i: (j, 0)), blk, prev, blk, pl.BlockSpec((HALO, tf), lambda j, i: (below(i), j)), blk, w3],
        out_specs=(blk, blk, w3, w1),
        out_shape=(jax.ShapeDtypeStruct((s, f), BF16), jax.ShapeDtypeStruct((s, f), BF16),
                   jax.ShapeDtypeStruct((3, f), F32), jax.ShapeDtypeStruct((1, f), F32)),
        compiler_params=_params(("parallel", "arbitrary")),
    )(dy2, dy2, wd, ua, ua, gpv, gpv, ge, cw)


ATT_TILE = 512
ATT_ROWS = 256
LOG2E = 1.4426950408889634
LN2 = 0.6931471805599453


def _head_blocks_t(a, width):
    s = a.shape[0]
    t = min(ATT_TILE, s)
    return a.reshape(s // t, t, N_HEADS, width).transpose(2, 0, 3, 1)


def _causal_mask(sv, q0, k0):
    row = q0 + lax.broadcasted_iota(jnp.int32, sv.shape, 0)
    col = k0 + lax.broadcasted_iota(jnp.int32, sv.shape, 1)
    return jnp.where(col <= row, sv, NEG_BIG)


def _attn_fwd(name, q_rot, kt4, v_ext):
    s = q_rot.shape[0]
    t = min(ATT_TILE, s)
    nq = s // t
    rq = min(ATT_ROWS, t)

    def body(q_ref, kt_ref, v_ref, o_ref, row_ref, acc_ref, m_ref):
        qi = pl.program_id(1)
        acc_ref[...] = jnp.zeros_like(acc_ref)
        m_ref[...] = jnp.full_like(m_ref, NEG_BIG)

        def step(j, masked):
            v_blk = v_ref[pl.ds(pl.multiple_of(j * t, t), t), :]
            for r in range(t // rq):
                rs = pl.ds(r * rq, rq)
                sv = _dot(q_ref[rs, :], kt_ref[0, j], NN)
                if masked:
                    sv = _causal_mask(sv, r * rq, 0)
                m_prev = m_ref[rs, :]
                m_new = jnp.maximum(m_prev, jnp.max(sv, axis=-1, keepdims=True))
                p = jnp.exp2(sv - m_new).astype(BF16)
                acc_ref[rs, :] = jnp.exp2(m_prev - m_new) * acc_ref[rs, :] + _dot(p, v_blk, NN)
                m_ref[rs, :] = m_new

        def full_step(j, carry):
            step(j, False)
            return carry

        lax.fori_loop(0, qi, full_step, 0)
        step(qi, True)
        l = acc_ref[:, V_HEAD:V_HEAD + 1]
        o_ref[...] = (acc_ref[:, :V_HEAD] / l).astype(o_ref.dtype)
        lse = jnp.broadcast_to(m_ref[...] + jnp.log(l) * LOG2E, (t, LANES))
        row_ref[0, 0] = jnp.transpose(lse)[0:8, :]

    head_q = pl.BlockSpec((t, Q_EXT), lambda h, i: (i, h))
    head_o = pl.BlockSpec((t, V_HEAD), lambda h, i: (i, h))
    return pl.pallas_call(
        body, name=name, grid=(N_HEADS, nq),
        in_specs=[head_q, pl.BlockSpec((1, nq, Q_EXT, t), lambda h, i: (h, 0, 0, 0)), pl.BlockSpec((s, Q_EXT), lambda h, i: (0, h))],
        out_specs=(head_o, pl.BlockSpec((1, 1, 8, t), lambda h, i: (h, i, 0, 0))),
        out_shape=(jax.ShapeDtypeStruct((s, N_HEADS * V_HEAD), BF16), jax.ShapeDtypeStruct((N_HEADS, nq, 8, t), F32)),
        scratch_shapes=[pltpu.VMEM((t, Q_EXT), F32), pltpu.VMEM((t, 1), F32)],
        compiler_params=_params(("parallel", "parallel")),
    )(q_rot, kt4, v_ext)


def _attn_delta(name, o, do):
    s = o.shape[0]
    t = min(ATT_TILE, s)

    def body(o_ref, do_ref, delta_ref):
        prod = do_ref[...].astype(F32) * o_ref[...].astype(F32)
        for h in range(N_HEADS):
            delta = jnp.sum(prod[:, h * V_HEAD:(h + 1) * V_HEAD], axis=-1, keepdims=True)
            delta_ref[h, 0] = jnp.transpose(jnp.broadcast_to(delta, (t, LANES)))[0:8, :]

    rows = pl.BlockSpec((t, N_HEADS * V_HEAD), lambda i: (i, 0))
    return pl.pallas_call(
        body, name=name, grid=(s // t,), in_specs=[rows, rows],
        out_specs=pl.BlockSpec((N_HEADS, 1, 8, t), lambda i: (0, i, 0, 0)),
        out_shape=jax.ShapeDtypeStruct((N_HEADS, s // t, 8, t), F32),
        compiler_params=_params(("parallel",)),
    )(o, do)


def _attn_bwd(name, kfull, v, qt4, q_rot, dot4, do, lse_row, delta_row, tabq, acc_in=None):
    s = kfull.shape[0]
    t = min(ATT_TILE, s)
    nq = s // t
    has_in = acc_in is not None

    def body(*refs):
        k_ref, v_ref, qt_ref, q_ref, dot_ref, do_ref, lse_ref, delta_ref, tab_ref = refs[:9]
        dq_ref, dkn_ref, dkd_ref, dv_ref, dq_acc_ref, acck_ref, accv_ref = refs[-7:]
        kj = pl.program_id(1)

        @pl.when(kj == 0)
        def _():
            dq_acc_ref[...] = jnp.zeros_like(dq_acc_ref)

        k_blk, v_blk = k_ref[...], v_ref[...]
        acck_ref[...] = jnp.zeros_like(acck_ref)
        accv_ref[...] = jnp.zeros_like(accv_ref)

        def step(i, masked):
            qs = pl.ds(pl.multiple_of(i * t, t), t)
            st = _dot(k_blk, qt_ref[0, i], NN)
            if masked:
                krow = lax.broadcasted_iota(jnp.int32, st.shape, 0)
                qcol = lax.broadcasted_iota(jnp.int32, st.shape, 1)
                st = jnp.where(krow <= qcol, st, NEG_BIG)
            pt = jnp.exp2(st - lse_ref[0, i, 0:1, :])
            accv_ref[...] += _dot(pt.astype(BF16), do_ref[qs, :], NN)
            dpt = _dot(v_blk, dot_ref[0, i], NN)
            dst = (pt * (dpt - delta_ref[0, i, 0:1, :])).astype(BF16)
            acck_ref[...] += _dot(dst, q_ref[qs, :], NN)
            dq_acc_ref[qs, :] += _dot(dst, k_blk, TN)

        def full_step(i, carry):
            step(i, False)
            return carry

        step(kj, True)
        lax.fori_loop(kj + 1, nq, full_step, 0)
        dk = acck_ref[...] * LN2
        if has_in:
            dkn_ref[...] = dk[:, :QK_NOPE] + refs[9][...]
            dkd_ref[...] = dk[:, QK_NOPE:] + refs[10][...]
            dv_ref[...] = accv_ref[...] + refs[11][...]
        else:
            dkn_ref[...] = dk[:, :QK_NOPE]
            dkd_ref[...] = dk[:, QK_NOPE:]
            dv_ref[...] = accv_ref[...]

        @pl.when(kj == nq - 1)
        def _():
            dq_ref[...] = (dq_acc_ref[...] * (tab_ref[...] * LN2)).astype(dq_ref.dtype)

    kblk = pl.BlockSpec((t, LANES), lambda h, j: (j, h))
    col = pl.BlockSpec((s, LANES), lambda h, j: (0, h))
    q_all = pl.BlockSpec((s, Q_EXT), lambda h, j: (0, h))
    stat = pl.BlockSpec((1, nq, 8, t), lambda h, j: (h, 0, 0, 0))
    ins = [kfull, v, qt4, q_rot, dot4, do, lse_row, delta_row, tabq]
    in_specs = [pl.BlockSpec((t, Q_EXT), lambda h, j: (j, h)), kblk, pl.BlockSpec((1, nq, Q_EXT, t), lambda h, j: (h, 0, 0, 0)),
                q_all, pl.BlockSpec((1, nq, V_HEAD, t), lambda h, j: (h, 0, 0, 0)), col, stat, stat,
                pl.BlockSpec((s, Q_EXT), lambda h, j: (0, 0))]
    if has_in:
        ins += list(acc_in)
        in_specs += [kblk, kblk, kblk]
    wide = jax.ShapeDtypeStruct((s, N_HEADS * LANES), F32)
    return pl.pallas_call(
        body, name=name, grid=(N_HEADS, nq), in_specs=in_specs, out_specs=(q_all, kblk, kblk, kblk),
        out_shape=(jax.ShapeDtypeStruct((s, N_HEADS * Q_EXT), BF16), wide, wide, wide),
        scratch_shapes=[pltpu.VMEM((s, Q_EXT), F32), pltpu.VMEM((t, Q_EXT), F32), pltpu.VMEM((t, LANES), F32)],
        compiler_params=_params(("parallel", "arbitrary")),
    )(*ins)


def _swap_halves(w):
    half = w.shape[-1] // 2
    return jnp.concatenate([-w[..., half:], w[..., :half]], axis=-1)


def _unswap_halves(g):
    half = g.shape[-1] // 2
    return jnp.concatenate([g[..., half:], -g[..., :half]], axis=-1)


def _extend_w_dkv(w):
    return jnp.concatenate([w, _swap_halves(w[:, KV_RANK:])], axis=-1)


def _fold_w_dkv_grad(g):
    rope = g[:, KV_RANK:KV_RANK + QK_ROPE] + _unswap_halves(g[:, KV_RANK + QK_ROPE:])
    return jnp.concatenate([g[:, :KV_RANK], rope], axis=-1)


def _rope_tables(positions):
    inv = 1.0 / (ROPE_THETA ** (jnp.arange(0, QK_ROPE, 2, dtype=F32) / QK_ROPE))
    ang = positions.astype(F32)[:, None] * inv
    cos, sin = jnp.cos(ang), jnp.sin(ang)
    tabk = jnp.concatenate([cos, cos, sin, sin], axis=-1)
    scale = QK_HEAD ** -0.5 * LOG2E
    tabq = jnp.concatenate([jnp.full((positions.shape[0], QK_NOPE), scale, F32), tabk * scale], axis=-1)
    return tabq, tabk


def _forward_backward(x, target, mods, tabq, tabk, final_g, fetch, push):
    row = lambda vec: vec.reshape(1, -1)
    mod = [[row(mods[l, k * D_MODEL:(k + 1) * D_MODEL]) for k in range(N_MOD)] for l in range(DEPTH)]
    saved, weights = [], []
    kv = None
    for l in range(DEPTH):
        w, tok = fetch(l, x)
        sh1, sc1, g1, sh2, sc2, g2 = mod[l]
        sh1 = sh1 + tok
        if l == N_A_LAYERS:
            kvn = _rms_fwd("kvin_fwd", x, row(w["kv_in_g"]))
            kv_ext = _mm("dkv_fwd", kvn, w["w_dkv_ext"], out_dtype=F32)
            ckv = _rms_fwd("ckv_fwd", kv_ext, row(w["ckv_norm_g"]), ncols=KV_RANK)
            kd = _krope_fwd("krope_fwd", kv_ext, tabk)
            kn, v = _mm("uk_fwd", ckv, w["w_uk"]), _mm("uv_fwd", ckv, w["w_uv"])
            heads = lambda a: [a[:, h * LANES:(h + 1) * LANES] for h in range(N_HEADS)]
            kfull = jnp.concatenate([part for kh in heads(kn) for part in (kh, kd)], axis=-1)
            v_ext = jnp.concatenate([part for vh in heads(v) for part in (vh, jnp.ones_like(vh))], axis=-1)
            kv = dict(x=x, kvn=kvn, kv_ext=kv_ext, ckv=ckv, v=v, kfull=kfull, v_ext=v_ext,
                      kt4=_head_blocks_t(kfull, Q_EXT))
        x_in = x
        if l < N_A_LAYERS:
            h1 = _rms_fwd(f"norm1_fwd_{l}", x, row(w["norm1_g"]), sc1, sh1, out_dtype=F32)
            x_mid, zb, pooled = _pool_fwd(f"pool_fwd_{l}", h1, x, w["pool_w"], row(w["pool_b"]), row(w["pool_scale"]), g1)
            mix = (zb, pooled)
        else:
            h1 = _rms_fwd(f"norm1_fwd_{l}", x, row(w["norm1_g"]), sc1, sh1)
            cq_pre = _mm(f"dq_fwd_{l}", h1, w["w_dq"], out_dtype=F32)
            cq = _rms_fwd(f"qnorm_fwd_{l}", cq_pre, row(w["q_norm_g"]))
            q_rot = _mm(f"uq_fwd_{l}", cq, w["w_uq_ext"], rowtab=tabq)
            o, lse_row = _attn_fwd(f"attn_fwd_{l}", q_rot, kv["kt4"], kv["v_ext"])
            y, x_mid = _mm(f"wo_fwd_{l}", o, w["w_o"], resid=x, gate=g1)
            mix = (h1, cq_pre, cq, q_rot, o, lse_row, y)
        h2 = _rms_fwd(f"norm2_fwd_{l}", x_mid, row(w["norm2_g"]), sc2, sh2)
        w_up_a, w_up_v = w["w_up"](h2)
        ua, gl, gpv, ge = _up_glu_fwd(f"up_glu_fwd_{l}", h2, w_up_a, w_up_v, w["conv_w"], row(w["conv_b"]))
        w_down = w["w_down"](gl)
        y2, x = _mm(f"down_fwd_{l}", gl, w_down, resid=x_mid, gate=g2)
        saved.append((x_in, x_mid, h2, ua, gpv, ge, gl, y2, mix))
        weights.append(dict(w, w_up_a=w_up_a, w_up_v=w_up_v, w_down=w_down))

    dx, dfinal_g, loss = _loss_head("loss_head", x, row(final_g), target)
    g = {"final_g": dfinal_g.reshape(-1)}
    per_layer = {k: [None] * DEPTH for k in ("norm1_g", "norm2_g", "conv_w", "conv_b")}
    per_a = {k: [None] * N_A_LAYERS for k in ("pool_b", "pool_scale")}
    per_b = {k: [None] * N_B_LAYERS for k in ("q_norm_g",)}
    dmods = [None] * DEPTH
    dkv = None
    tok = 0.0
    for l in reversed(range(DEPTH)):
        w, big = weights[l], {}
        sh1, sc1, g1, sh2, sc2, g2 = mod[l]
        g2 = g2 + tok
        x_in, x_mid, h2, ua, gpv, ge, gl, y2, mix = saved[l]
        dy2, dg2 = _gate_bwd(f"gate2_bwd_{l}", dx, y2, g2)
        tok = push(l, "down", dict(w_down=_mm(f"down_wgrad_{l}", gl, dy2, mode="tn", tm_cap=1408)), None)
        da, dv_, dcw, dcb = _down_glu_bwd(f"down_glu_bwd_{l}", dy2, w["w_down"], ua, gpv, ge, w["conv_w"] + tok)
        dh2 = _mm(f"up_bwd_{l}", da, w["w_up_a"], mode="nt", out_dtype=F32, second=(dv_, w["w_up_v"]))
        tok = push(l, "up", dict(w_up_a=_mm(f"up_a_wgrad_{l}", h2, da, mode="tn"), w_up_v=_mm(f"up_v_wgrad_{l}", h2, dv_, mode="tn")), None)
        per_layer["conv_w"][l], per_layer["conv_b"][l] = dcw, dcb.reshape(-1)
        dx_mid, dn2, dsh2, dsc2 = _rms_bwd(f"norm2_bwd_{l}", x_mid, row(w["norm2_g"]), dh2, sc2 + tok, dx_in=dx)
        per_layer["norm2_g"][l] = dn2.reshape(-1)
        if l < N_A_LAYERS:
            zb, pooled = mix
            dh1, dpw, dpb, dps, dg1 = _pool_bwd(f"pool_bwd_{l}", dx_mid, zb, pooled, w["pool_w"], row(w["pool_scale"]), g1)
            big["pool_w"] = dpw
            per_a["pool_b"][l], per_a["pool_scale"][l] = dpb.reshape(-1), dps.reshape(-1)
        else:
            j = l - N_A_LAYERS
            h1, cq_pre, cq, q_rot, o, lse_row, y = mix
            dy, dg1 = _gate_bwd(f"gate1_bwd_{l}", dx_mid, y, g1)
            do = _mm(f"wo_bwd_{l}", dy, w["w_o"], mode="nt")
            big["w_o"] = _mm(f"wo_wgrad_{l}", o, dy, mode="tn")
            delta_row = _attn_delta(f"attn_delta_{l}", o, do)
            dq_ext, *dkv = _attn_bwd(f"attn_bwd_{l}", kv["kfull"], kv["v"], _head_blocks_t(q_rot, Q_EXT), q_rot, _head_blocks_t(do, V_HEAD), do,
                                     lse_row, delta_row, tabq, acc_in=dkv)
            dcq = _mm(f"uq_bwd_{l}", dq_ext, w["w_uq_ext"], mode="nt", out_dtype=F32)
            big["w_uq_ext"] = _mm(f"uq_wgrad_{l}", cq, dq_ext, mode="tn", out_dtype=F32)
            dcq_pre, dqn = _rms_bwd(f"qnorm_bwd_{l}", cq_pre, row(w["q_norm_g"]), dcq, out_dtype=BF16)
            per_b["q_norm_g"][j] = dqn.reshape(-1)
            dh1 = _mm(f"dq_bwd_{l}", dcq_pre, w["w_dq"], mode="nt")
            big["w_dq"] = _mm(f"dq_wgrad_{l}", h1, dcq_pre, mode="tn")
        dx, dn1, dsh1, dsc1 = _rms_bwd(f"norm1_bwd_{l}", x_in, row(w["norm1_g"]), dh1, sc1, dx_in=dx_mid)
        per_layer["norm1_g"][l] = dn1.reshape(-1)
        dmods[l] = jnp.concatenate([dsh1, dsc1, dg1, dsh2, dsc2, dg2], axis=-1).reshape(-1)
        if l == N_A_LAYERS:
            dkn, dkd, dv = dkv
            dckv = _mm("ukv_bwd", dkn, w["w_uk"], mode="nt", out_dtype=F32, second=(dv, w["w_uv"]))
            big["w_uk"] = _mm("uk_wgrad", kv["ckv"], dkn, mode="tn")
            big["w_uv"] = _mm("uv_wgrad", kv["ckv"], dv, mode="tn")
            dkr = _krope_bwd("krope_bwd", dkd, tabk)
            dc, dckv_g = _rms_bwd("ckv_bwd", kv["kv_ext"], row(w["ckv_norm_g"]), dckv, ncols=KV_RANK, out_dtype=BF16)
            dkv_ext = jnp.concatenate([dc, dkr.astype(BF16)], axis=-1)
            dkvn = _mm("dkv_bwd", dkv_ext, w["w_dkv_ext"], mode="nt")
            big["w_dkv_ext"] = _mm("dkv_wgrad", kv["kvn"], dkv_ext, mode="tn", out_dtype=F32)
            dx, dkv_in_g = _rms_bwd("kvin_bwd", kv["x"], row(w["kv_in_g"]), dkvn, dx_in=dx)
            g["ckv_norm_g"], g["kv_in_g"] = dckv_g.reshape(-1), dkv_in_g.reshape(-1)
        tok = push(l, "mix", big, dx)
    for group in (per_layer, per_a, per_b):
        for k, vals in group.items():
            g[k] = jnp.stack(vals)
    return loss, dx, g, jnp.stack(dmods)


def _my_index():
    return 4 * lax.axis_index("x") + 2 * lax.axis_index("y") + lax.axis_index("c")


def _peer(k):
    x, y, c = lax.axis_index("x"), lax.axis_index("y"), lax.axis_index("c")
    return (1 - x if k & 4 else x, 1 - y if k & 2 else y, 1 - c if k & 1 else c)


def _index_of(pos):
    return 4 * pos[0] + 2 * pos[1] + pos[2]


def _exchange_many(name, arrays, scatter):
    n = len(arrays)
    blocks = [tuple(a.shape[1:]) if scatter else tuple(a.shape) for a in arrays]

    def body(*refs):
        x_refs, o_refs = refs[:n], refs[n:2 * n]
        send_sems, recv_sems, local_sems = refs[2 * n:]
        me = _my_index()
        started = []
        for a in range(n):
            mine = pltpu.make_async_copy(x_refs[a].at[me] if scatter else x_refs[a], o_refs[a].at[me], local_sems.at[a])
            mine.start()
            started.append(mine)
        sends = []
        for k in range(1, N_DEV):
            peer = _peer(k)
            for a in range(n):
                cp = pltpu.make_async_remote_copy(
                    src_ref=x_refs[a].at[_index_of(peer)] if scatter else x_refs[a], dst_ref=o_refs[a].at[me],
                    send_sem=send_sems.at[a, k - 1], recv_sem=recv_sems.at[a, k - 1], device_id=peer, device_id_type=MESH)
                cp.start()
                sends.append(cp)
        for k in range(1, N_DEV):
            peer = _peer(k)
            for a in range(n):
                pltpu.make_async_remote_copy(
                    src_ref=x_refs[a].at[me] if scatter else x_refs[a], dst_ref=o_refs[a].at[_index_of(peer)],
                    send_sem=send_sems.at[a, k - 1], recv_sem=recv_sems.at[a, k - 1], device_id=peer, device_id_type=MESH).wait_recv()
        for cp in sends:
            cp.wait_send()
        for mine in started:
            mine.wait()

    return pl.pallas_call(
        body, name=name, out_shape=tuple(jax.ShapeDtypeStruct((N_DEV,) + blk, a.dtype) for blk, a in zip(blocks, arrays)),
        in_specs=[pl.BlockSpec(memory_space=pl.ANY)] * n, out_specs=tuple([pl.BlockSpec(memory_space=pl.ANY)] * n),
        scratch_shapes=[pltpu.SemaphoreType.DMA((n, N_DEV - 1)), pltpu.SemaphoreType.DMA((n, N_DEV - 1)), pltpu.SemaphoreType.DMA((n,))],
    )(*arrays)


def _exchange(name, x, scatter):
    return _exchange_many(name, [x], scatter)[0]


HBM_SPEC = pl.BlockSpec(memory_space=pltpu.HBM)
SEM_SPEC = pl.BlockSpec(memory_space=pltpu.SEMAPHORE)
DATAFLOW = pltpu.SideEffectType.DATAFLOW_SIDE_EFFECTING


def _remote_copies(x_refs, land_refs, send_sems, recv_sems, scatter, numbers=None):
    me = _my_index()
    numbers = list(range(len(x_refs))) if numbers is None else numbers
    out, inc = [], []
    for a in range(len(x_refs)):
        for k in range(1, N_DEV):
            peer = _peer(k)
            pair = numbers[a] * (N_DEV - 1) + k - 1
            sems = dict(send_sem=send_sems.at[pair], recv_sem=recv_sems.at[pair], device_id=peer, device_id_type=MESH)
            out.append(pltpu.make_async_remote_copy(
                src_ref=x_refs[a].at[_index_of(peer)] if scatter else x_refs[a], dst_ref=land_refs[a].at[me], **sems))
            inc.append(pltpu.make_async_remote_copy(
                src_ref=x_refs[a].at[me] if scatter else x_refs[a], dst_ref=land_refs[a].at[_index_of(peer)], **sems))
    return out, inc


def _exchange_start(name, arrays, scatter):
    n = len(arrays)
    blocks = [tuple(a.shape[1:]) if scatter else tuple(a.shape) for a in arrays]

    def body(*refs):
        x_refs, land_refs = refs[:n], refs[n:2 * n]
        send_sems, recv_sems = refs[2 * n], refs[2 * n + 1]
        for cp in _remote_copies(x_refs, land_refs, send_sems, recv_sems, scatter)[0]:
            cp.start()
        refs[-1][...] = jnp.zeros_like(refs[-1])

    sem_type = pltpu.SemaphoreType.DMA((n * (N_DEV - 1),))
    lands =[pltpu.with_memory_space_constraint(lax.empty((N_DEV,) + blk, a.dtype), pltpu.HBM) for blk, a in zip(blocks, arrays)]
    srcs = [pltpu.with_memory_space_constraint(a, pltpu.HBM) for a in arrays]
    res = pl.pallas_call(
        body, name=name,
        out_shape=(sem_type, sem_type, *[pltpu.HBM(a.shape, a.dtype) for a in srcs + lands], jax.ShapeDtypeStruct((8, LANES), F32)),
        in_specs=[HBM_SPEC] * (2 * n), out_specs=(SEM_SPEC, SEM_SPEC, *[HBM_SPEC] * (2 * n), pl.BlockSpec(memory_space=pltpu.VMEM)),
        input_output_aliases={i: 2 + i for i in range(2 * n)},
        compiler_params=pltpu.CompilerParams(has_side_effects=DATAFLOW),
    )(*srcs, *lands)
    return (res[0], res[1], list(res[2:2 + n]), list(res[2 + n:2 + 2 * n])), res[-1]


def _exchange_wait(name, handles, after, scatter, which=None):
    send_sems, recv_sems, srcs, lands = handles
    which = list(range(len(srcs))) if which is None else list(which)
    srcs, lands = [srcs[a] for a in which], [lands[a] for a in which]
    n = len(srcs)

    def body(*refs):
        x_refs, land_refs = refs[:n], refs[n:2 * n]
        out, inc = _remote_copies(x_refs, land_refs, refs[2 * n], refs[2 * n + 1], scatter, which)
        for cp in out:
            cp.wait_send()
        for cp in inc:
            cp.wait_recv()

    res = pl.pallas_call(
        body, name=name, out_shape=tuple(pltpu.HBM(a.shape, a.dtype) for a in srcs + lands),
        in_specs=[HBM_SPEC] * (2 * n) + [SEM_SPEC, SEM_SPEC, pl.BlockSpec(memory_space=pl.ANY)], out_specs=tuple([HBM_SPEC] * (2 * n)),
        input_output_aliases={i: i for i in range(2 * n)},
        compiler_params=pltpu.CompilerParams(has_side_effects=DATAFLOW),
    )(*srcs, *lands, send_sems, recv_sems, after)
    return list(res[n:])


def _pack(arrays, dtype, row_multiple):
    flat = jnp.concatenate([a.astype(dtype).reshape(-1) for a in arrays])
    rows = -(-flat.shape[0] // (LANES * row_multiple)) * row_multiple
    return jnp.pad(flat, (0, rows * LANES - flat.shape[0])).reshape(rows, LANES)


def _unpack(packed, shapes):
    lead = packed.shape[:-2]
    flat = packed.reshape(lead + (-1,))
    out, off = [], 0
    for shp in shapes:
        size = 1
        for d in shp:
            size *= d
        out.append(flat[..., off:off + size].reshape(lead + tuple(shp)))
        off += size
    return out


def _unshard(g8, axis):
    return jnp.concatenate([g8[j] for j in range(N_DEV)], axis=axis)


def _shard8(full, axis):
    n = full.shape[axis] // N_DEV
    return jnp.stack([lax.slice_in_dim(full, j * n, (j + 1) * n, axis=axis) for j in range(N_DEV)])


VECTOR_WEIGHTS = (("pool_b", 1), ("pool_scale", 1), ("conv_w", 2))
REPLICATED_WEIGHTS = ("norm1_g", "norm2_g", "kv_in_g", "ckv_norm_g", "q_norm_g", "conv_b", "final_g")
WEIGHT_ORDER = ("mod_w", "mod_b", "norm1_g", "norm2_g", "pool_w", "pool_b", "pool_scale", "kv_in_g", "w_dkv", "ckv_norm_g", "w_uk",
                "w_uv", "w_dq", "q_norm_g", "w_uq", "w_o", "w_up", "conv_w", "conv_b", "w_down", "final_g")
SMALL_ROW_MULTIPLE = 16


def _as_2d(a):
    if a.ndim == 1:
        return a.reshape(-1, LANES)
    return a.reshape(-1, a.shape[-1])


def kernel(x, c, positions, mod_w, mod_b, norm1_g, norm2_g, pool_w, pool_b, pool_scale, kv_in_g, w_dkv, ckv_norm_g, w_uk, w_uv, w_dq, q_norm_g, w_uq, w_o, w_up, conv_w, conv_b, w_down, final_g, loss_target, m_mod_w, m_mod_b, m_norm1_g, m_norm2_g, m_pool_w, m_pool_b, m_pool_scale, m_kv_in_g, m_w_dkv, m_ckv_norm_g, m_w_uk, m_w_uv, m_w_dq, m_q_norm_g, m_w_uq, m_w_o, m_w_up, m_conv_w, m_conv_b, m_w_down, m_final_g, v_mod_w, v_mod_b, v_norm1_g, v_norm2_g, v_pool_w, v_pool_b, v_pool_scale, v_kv_in_g, v_w_dkv, v_ckv_norm_g, v_w_uk, v_w_uv, v_w_dq, v_q_norm_g, v_w_uq, v_w_o, v_w_up, v_conv_w, v_conv_b, v_w_down, v_final_g):
    shard = dict(mod_w=mod_w, mod_b=mod_b, norm1_g=norm1_g, norm2_g=norm2_g, pool_w=pool_w, pool_b=pool_b, pool_scale=pool_scale,
                 kv_in_g=kv_in_g, w_dkv=w_dkv, ckv_norm_g=ckv_norm_g, w_uk=w_uk, w_uv=w_uv, w_dq=w_dq, q_norm_g=q_norm_g, w_uq=w_uq,
                 w_o=w_o, w_up=w_up, conv_w=conv_w, conv_b=conv_b, w_down=w_down, final_g=final_g)
    mom_m = dict(mod_w=m_mod_w, mod_b=m_mod_b, norm1_g=m_norm1_g, norm2_g=m_norm2_g, pool_w=m_pool_w, pool_b=m_pool_b,
                 pool_scale=m_pool_scale, kv_in_g=m_kv_in_g, w_dkv=m_w_dkv, ckv_norm_g=m_ckv_norm_g, w_uk=m_w_uk, w_uv=m_w_uv,
                 w_dq=m_w_dq, q_norm_g=m_q_norm_g, w_uq=m_w_uq, w_o=m_w_o, w_up=m_w_up, conv_w=m_conv_w, conv_b=m_conv_b,
                 w_down=m_w_down, final_g=m_final_g)
    mom_v = dict(mod_w=v_mod_w, mod_b=v_mod_b, norm1_g=v_norm1_g, norm2_g=v_norm2_g, pool_w=v_pool_w, pool_b=v_pool_b,
                 pool_scale=v_pool_scale, kv_in_g=v_kv_in_g, w_dkv=v_w_dkv, ckv_norm_g=v_ckv_norm_g, w_uk=v_w_uk, w_uv=v_w_uv,
                 w_dq=v_w_dq, q_norm_g=v_q_norm_g, w_uq=v_w_uq, w_o=v_w_o, w_up=v_w_up, conv_w=v_conv_w, conv_b=v_conv_b,
                 w_down=v_w_down, final_g=v_final_g)
    me = _my_index()
    d6 = N_MOD * D_MODEL
    mod_cols = d6 // N_DEV

    small_in = [c] + [shard[k] for k, _ in VECTOR_WEIGHTS]
    small_all = _exchange("gather_vectors", _pack(small_in, F32, SMALL_ROW_MULTIPLE), scatter=False)
    parts = _unpack(small_all, [a.shape for a in small_in])
    c_all = jnp.pad(parts[0].reshape(N_DEV, D_MODEL), ((0, N_DEV), (0, 0)))
    vec = {k: _unshard(p, ax) for (k, ax), p in zip(VECTOR_WEIGHTS, parts[1:])}

    my_mod_b = lax.dynamic_slice_in_dim(mod_b, me * mod_cols, mod_cols, axis=1)
    mods_mine = _mods_fwd("mods_fwd", c_all, mod_w, my_mod_b)
    mods_all = _exchange("gather_mods", _pack([mods_mine], F32, SMALL_ROW_MULTIPLE), scatter=False)
    mods_all = _unpack(mods_all, [mods_mine.shape])[0]
    mods = lax.dynamic_index_in_dim(mods_all, me, axis=2, keepdims=False)
    mods = jnp.moveaxis(mods, 0, 1).reshape(DEPTH, d6)

    tabq, tabk = _rope_tables(positions[0])
    half = N_DEV // 2
    up_cols = shard["w_up"].shape[2]
    cat = lambda a, axis, lo=0, hi=N_DEV: jnp.concatenate([a[j] for j in range(lo, hi)], axis=axis)

    def stage_pieces(l):
        out = {"pool_w": shard["pool_w"].astype(BF16)} if l == 0 else {}
        if l == N_A_LAYERS:
            out.update({k: shard[k].astype(BF16) for k in ("w_dkv", "w_uk", "w_uv")})
        if l >= N_A_LAYERS:
            out.update({k: shard[k][l - N_A_LAYERS].astype(BF16) for k in ("w_dq", "w_uq", "w_o")})
        out.update(w_up=shard["w_up"][l].astype(BF16), w_down=shard["w_down"][l].astype(BF16))
        return out

    gathers, pool_all = {}, []

    def start_gather(l, behind=None):
        pieces = stage_pieces(l)
        if behind is not None:
            pieces, _ = lax.optimization_barrier((pieces, behind))
        handles, token = _exchange_start(f"gather_start_{l}", list(pieces.values()), scatter=False)
        gathers[l] = (handles, pieces)
        return token[0, 0]

    def wait_gather(l, keys, after, tag=""):
        handles, pieces = gathers[l]
        which = [list(pieces).index(k) for k in keys]
        lands = _exchange_wait(f"gather_wait_{l}{tag}", handles, after, scatter=False, which=which)
        return dict(zip(keys, own_slot(lands, [pieces[k] for k in keys])))

    def whole_weights(l, got):
        w = dict(norm1_g=norm1_g[l], norm2_g=norm2_g[l], conv_w=vec["conv_w"][l], conv_b=conv_b[l])
        if l == 0:
            pool_all.append(got["pool_w"])
        if l < N_A_LAYERS:
            w.update(pool_w=cat(pool_all[0][:, l], 1), pool_b=vec["pool_b"][l], pool_scale=vec["pool_scale"][l])
        else:
            rope = got["w_uq"][..., QK_NOPE:]
            ext = jnp.concatenate([got["w_uq"][..., :QK_NOPE], rope, _swap_halves(rope)], axis=-1)
            w.update(w_dq=got["w_dq"].reshape(D_MODEL, Q_RANK), w_uq_ext=cat(ext, -1), w_o=got["w_o"].reshape(D_MODEL, D_MODEL),
                     q_norm_g=q_norm_g[l - N_A_LAYERS])
        if l == N_A_LAYERS:
            w.update(w_dkv_ext=_extend_w_dkv(got["w_dkv"].reshape(D_MODEL, KV_RANK + QK_ROPE)), w_uk=cat(got["w_uk"], -1),
                     w_uv=cat(got["w_uv"], -1), kv_in_g=kv_in_g, ckv_norm_g=ckv_norm_g)
        return w

    def own_slot(lands, own):
        return [lax.dynamic_update_index_in_dim(p, o, me, 0) for p, o in zip(lands, own)]

    def fetch(l, after):
        up_parts = lambda g8: (cat(g8, -1, 0, half), cat(g8, -1, half, N_DEV))
        if l == 0:
            start_gather(0, behind=mods)
            got = wait_gather(0, ["pool_w"], mods, "_pool")
            w_up = lambda aft: up_parts(wait_gather(0, ["w_up"], aft, "_up")["w_up"])
            w_down = lambda aft: wait_gather(0, ["w_down"], aft, "_down")["w_down"].reshape(D_FF, D_MODEL)
        else:
            got = wait_gather(l, list(gathers[l][1]), after)
            up, down = up_parts(got["w_up"]), got["w_down"].reshape(D_FF, D_MODEL)
            w_up, w_down = (lambda aft: up), (lambda aft: down)
        w = dict(whole_weights(l, got), w_up=w_up, w_down=w_down)
        return w, (start_gather(l + 1) if l + 1 < DEPTH else 0.0)

    scatters, pending, pool_grads, piece_grads = {}, {}, {}, {}

    def reduce_pieces(l, keys, got):
        for k, p in zip(keys, got):
            piece_grads[(k, l)] = _sum8(f"sum_grads_{k}_{l}", p.reshape(N_DEV, -1, p.shape[-1])).reshape(p.shape[1:])

    def start_scatter(name, sent):
        sent = {k: a.astype(BF16) for k, a in sent.items()}
        handles, token = _exchange_start(f"scatter_start_{name}", list(sent.values()), scatter=True)
        scatters[name] = (handles, list(sent), [lax.dynamic_index_in_dim(a, me, 0, keepdims=False) for a in sent.values()])
        return token[0, 0]

    def finish_scatter(name, l, after):
        handles, keys, own = scatters.pop(name)
        reduce_pieces(l, keys, own_slot(_exchange_wait(f"scatter_wait_{name}", handles, after, scatter=True), own))

    def push(l, part, big, after):
        cut = lambda a, n, axis: jnp.stack([lax.slice_in_dim(a, j * n, (j + 1) * n, axis=axis) for j in range(N_DEV)])
        sent = {}
        if part == "down":
            sent["w_down"] = big["w_down"].reshape(N_DEV, D_FF // N_DEV, D_MODEL)
        elif part == "up":
            sent["w_up"] = jnp.stack([lax.slice_in_dim(big[half_], j * up_cols, (j + 1) * up_cols, axis=1)
                                      for half_ in ("w_up_a", "w_up_v") for j in range(half)])
        elif l < N_A_LAYERS:
            pool_grads[l] = big["pool_w"]
        else:
            ext = cut(big["w_uq_ext"], Q_EXT, 1)
            rope = ext[..., QK_NOPE:QK_HEAD] + _unswap_halves(ext[..., QK_HEAD:])
            sent.update(w_dq=big["w_dq"].reshape(N_DEV, D_MODEL // N_DEV, Q_RANK), w_uq=jnp.concatenate([ext[..., :QK_NOPE], rope], axis=-1),
                        w_o=big["w_o"].reshape(N_DEV, D_MODEL // N_DEV, D_MODEL))
        if part == "mix" and l == N_A_LAYERS:
            sent.update(w_dkv=_fold_w_dkv_grad(big["w_dkv_ext"]).reshape(N_DEV, D_MODEL // N_DEV, KV_RANK + QK_ROPE),
                        w_uk=cut(big["w_uk"], QK_NOPE, 1), w_uv=cut(big["w_uv"], V_HEAD, 1))
        if l == 0 and part != "mix":
            return start_scatter(f"0_{part}", sent)
        if l == 0:
            finish_scatter("1", 1, after)
            pool = _shard8(jnp.stack([pool_grads[a] for a in range(N_A_LAYERS)]), 2).astype(BF16)
            reduce_pieces(0, ["pool_w"], _exchange_many("scatter_pool_grads", [pool], scatter=True))
            return 0.0
        pending.setdefault(l, {}).update(sent)
        if part != "mix":
            return 0.0
        if l + 1 < DEPTH:
            finish_scatter(str(l + 1), l + 1, after)
        return start_scatter(str(l), pending.pop(l))

    loss_row, dx, g, dmods = _forward_backward(x[0], loss_target[0], mods, tabq, tabk, final_g, fetch, push)
    layers_of = lambda k, ls: jnp.stack([piece_grads[(k, l)] for l in ls])
    grads = dict(w_dkv=piece_grads[("w_dkv", N_A_LAYERS)], w_uk=piece_grads[("w_uk", N_A_LAYERS)], w_uv=piece_grads[("w_uv", N_A_LAYERS)])
    for k in ("w_dq", "w_uq", "w_o"):
        grads[k] = layers_of(k, range(N_A_LAYERS, DEPTH))

    small_names = REPLICATED_WEIGHTS + tuple(k for k, _ in VECTOR_WEIGHTS)
    small_out = [dmods] + [g[k] for k in small_names] + [loss_row]
    small_shapes = [a.shape for a in small_out]
    small_got = _exchange("gather_small_grads", _pack(small_out, F32, SMALL_ROW_MULTIPLE), scatter=False)
    summed = _unpack(_sum8("sum_small_grads", small_got), small_shapes)
    grads["mod_b"] = summed[0]
    for k, s in zip(small_names, summed[1:-1]):
        grads[k] = s
    for k, ax in VECTOR_WEIGHTS:
        n = shard[k].shape[ax]
        grads[k] = lax.dynamic_slice_in_dim(grads[k], me * n, n, axis=ax)
    loss = summed[-1][0, 0]
    dmods_all = _unpack(small_got, small_shapes)[0]
    dm_mine = lax.dynamic_slice_in_dim(dmods_all, me * mod_cols, mod_cols, axis=2)
    dm_mine = jnp.pad(jnp.moveaxis(dm_mine, 0, 1), ((0, 0), (0, N_DEV), (0, 0)))
    grads["mod_w"] = _mods_bwd("mods_bwd", c_all, dm_mine)

    delta, new_m, new_v = {}, {}, {}

    def adamw(k):
        shp = shard[k].shape
        grads[k] = grads[k].reshape(shp)
        view = (lambda a: jnp.swapaxes(a, 1, 2)) if k == "w_up" else (lambda a: a)
        ops = [view(a) for a in (shard[k], grads[k], mom_m[k], mom_v[k])]
        ops[1] = lax.optimization_barrier(ops[1])
        res = _adamw(f"adamw_{k}", *[_as_2d(a) for a in ops])
        delta[k], new_m[k], new_v[k] = [view(r.reshape(ops[0].shape)) for r in res]
        grads[k] = view(ops[1])

    late = ("w_up", "w_down", "pool_w")
    for k in WEIGHT_ORDER:
        if k not in late:
            adamw(k)
    finish_scatter("0_down", 0, delta["final_g"])
    finish_scatter("0_up", 0, delta["final_g"])
    grads.update(w_up=layers_of("w_up", range(DEPTH)), w_down=layers_of("w_down", range(DEPTH)), pool_w=piece_grads[("pool_w", 0)])
    for k in late:
        adamw(k)
    return (loss, dx[None], *[grads[k] for k in WEIGHT_ORDER], *[delta[k] for k in WEIGHT_ORDER],
            *[new_m[k] for k in WEIGHT_ORDER], *[new_v[k] for k in WEIGHT_ORDER])
```

```python
import functools

import jax
import jax.numpy as jnp
from jax import lax
from jax.experimental import pallas as pl
from jax.experimental.pallas import tpu as pltpu

F32 = jnp.float32
BF16 = jnp.bfloat16

D_MODEL = 1024
DEPTH = 4
N_A_LAYERS = 2
N_B_LAYERS = 2
POOL_WINDOWS = (2, 4, 8, 16)
POOL_GROUP = 256
N_HEADS = 8
QK_NOPE = 128
QK_ROPE = 64
V_HEAD = 128
QK_HEAD = QK_NOPE + QK_ROPE
Q_RANK = 384
KV_RANK = 256
ROPE_THETA = 10000.0
D_FF = 2816
EPS = 1e-6
N_MOD = 6
ADAM_LR = 0.001
ADAM_B1 = 0.9
ADAM_B2 = 0.999
ADAM_EPS = 1e-08
ADAM_WD = 0.01
ADAM_STEP = 10

N_DEV = 8
LANES = 128
Q_EXT = 256
VMEM_LIMIT_BYTES = 48 * 1024 * 1024
MESH = pl.DeviceIdType.MESH
NEG_BIG = -0.7 * float(jnp.finfo(jnp.float32).max)


def _params(sem):
    return pltpu.CompilerParams(dimension_semantics=sem, vmem_limit_bytes=VMEM_LIMIT_BYTES)


def _tile(n, cap):
    if n <= cap:
        return n
    best = None
    for d in range(LANES, cap + 1, LANES):
        if n % d == 0:
            best = d
    assert best is not None, (n, cap)
    return best


def _dot(a, b, dims):
    return lax.dot_general(a, b, (dims, ((), ())), preferred_element_type=F32)


NN = ((1,), (0,))
NT = ((1,), (1,))
TN = ((0,), (0,))


def _mm(name, a, b, mode="nn", out_dtype=BF16, resid=None, gate=None, rowtab=None, second=None,
        tm_cap=1024, tn_cap=1408, tk_cap=1408):
    if mode == "tn":
        kdim, m = a.shape
    else:
        m, kdim = a.shape
    n = b.shape[0] if mode == "nt" else b.shape[1]
    tm, tn, tk = _tile(m, tm_cap), _tile(n, tn_cap), _tile(kdim, tk_cap)
    nk = kdim // tk
    dims = {"nn": NN, "nt": NT, "tn": TN}[mode]
    a_spec = pl.BlockSpec((tk, tm), lambda i, j, k: (k, i)) if mode == "tn" else pl.BlockSpec((tm, tk), lambda i, j, k: (i, k))
    b_spec = pl.BlockSpec((tn, tk), lambda i, j, k: (j, k)) if mode == "nt" else pl.BlockSpec((tk, tn), lambda i, j, k: (k, j))
    o_spec = pl.BlockSpec((tm, tn), lambda i, j, k: (i, j))
    g_spec = pl.BlockSpec((1, tn), lambda i, j, k: (0, j))
    gated = resid is not None

    n_ops = 2 if second is None else 4

    def body(*refs):
        acc = refs[-1]
        k = pl.program_id(2)

        @pl.when(k == 0)
        def _():
            acc[...] = jnp.zeros_like(acc)

        prod = _dot(refs[0][...].astype(BF16), refs[1][...].astype(BF16), dims)
        if second is not None:
            prod = prod + _dot(refs[2][...].astype(BF16), refs[3][...].astype(BF16), dims)
        acc[...] += prod

        @pl.when(k == nk - 1)
        def _():
            rest = refs[n_ops:-1]
            if gated:
                r_ref, g_ref, y_ref, x_ref = rest
                y_ref[...] = acc[...]
                x_ref[...] = r_ref[...] + g_ref[...] * acc[...]
            elif rowtab is not None:
                tab = rest[0][...]
                rest[1][...] = (acc[...] * jnp.concatenate([tab] * (tn // tab.shape[1]), axis=1)).astype(out_dtype)
            else:
                rest[0][...] = acc[...].astype(out_dtype)

    ins, in_specs = [a, b], [a_spec, b_spec]
    if second is not None:
        assert second[0].shape == a.shape and second[1].shape == b.shape
        ins += list(second)
        in_specs += [a_spec, b_spec]
    if rowtab is not None:
        assert tn % rowtab.shape[1] == 0 and not gated
        ins.append(rowtab)
        in_specs.append(pl.BlockSpec((tm, rowtab.shape[1]), lambda i, j, k: (i, 0)))
    if gated:
        ins += [resid, gate]
        in_specs += [o_spec, g_spec]
        out_shape = (jax.ShapeDtypeStruct((m, n), F32), jax.ShapeDtypeStruct((m, n), F32))
        out_specs = (o_spec, o_spec)
    else:
        out_shape = jax.ShapeDtypeStruct((m, n), out_dtype)
        out_specs = o_spec
    return pl.pallas_call(
        body, name=name, grid=(m // tm, n // tn, nk), in_specs=in_specs, out_specs=out_specs, out_shape=out_shape,
        scratch_shapes=[pltpu.VMEM((tm, tn), F32)],
        compiler_params=_params(("parallel", "parallel", "arbitrary")),
    )(*ins)


def _rowwise(name, fn, tiled, bcast, outs, sums=(), tr=512):
    tiled = [t if isinstance(t, tuple) else (t, t.shape[1], 0) for t in tiled]
    s = tiled[0][0].shape[0]
    tr = min(tr, s)
    assert s % tr == 0
    n_t, n_b, n_o = len(tiled), len(bcast), len(outs)

    def body(*refs):
        i = pl.program_id(0)
        vals = [r[...] for r in refs[:n_t + n_b]]
        o_vals, s_vals = fn(*vals)
        for r, v in zip(refs[n_t + n_b:n_t + n_b + n_o], o_vals):
            r[...] = v.astype(r.dtype)
        s_refs = refs[n_t + n_b + n_o:]

        @pl.when(i == 0)
        def _():
            for r in s_refs:
                r[...] = jnp.zeros_like(r)

        for r, v in zip(s_refs, s_vals):
            r[...] += v

    in_specs = [pl.BlockSpec((tr, n), functools.partial(lambda cb, i: (i, cb), cb)) for (_, n, cb) in tiled]
    in_specs += [pl.BlockSpec(b.shape, functools.partial(lambda nd, i: (0,) * nd, b.ndim)) for b in bcast]
    out_specs = [pl.BlockSpec((tr, n), lambda i: (i, 0)) for (n, _) in outs]
    out_specs += [pl.BlockSpec((1, n), lambda i: (0, 0)) for n in sums]
    out_shape = [jax.ShapeDtypeStruct((s, n), dt) for (n, dt) in outs]
    out_shape += [jax.ShapeDtypeStruct((1, n), F32) for n in sums]
    res = pl.pallas_call(
        body, name=name, grid=(s // tr,), in_specs=in_specs, out_specs=tuple(out_specs), out_shape=tuple(out_shape),
        compiler_params=_params(("arbitrary",)),
    )(*[t[0] for t in tiled], *bcast)
    return res


def _colsum(v):
    return jnp.sum(v, axis=0, keepdims=True)


def _rms_fwd(name, x, g, scale=None, shift=None, out_dtype=BF16, ncols=None):
    mod = scale is not None

    def fn(xv, gv, *ss):
        y = xv * lax.rsqrt(jnp.mean(xv * xv, axis=-1, keepdims=True) + EPS) * gv
        if mod:
            y = y * (1.0 + ss[0]) + ss[1]
        return (y,), ()

    n = ncols or x.shape[1]
    return _rowwise(name, fn, [(x, n, 0)], [g] + ([scale, shift] if mod else []), [(n, out_dtype)])[0]


def _rms_bwd(name, x, g, dh, scale=None, dx_in=None, ncols=None, out_dtype=F32):
    mod = scale is not None
    has_in = dx_in is not None

    def fn(*vals):
        xv, dhv = vals[0], vals[1].astype(F32)
        rest = list(vals[2:])
        dxi = rest.pop(0) if has_in else None
        gv = rest.pop(0)
        rstd = lax.rsqrt(jnp.mean(xv * xv, axis=-1, keepdims=True) + EPS)
        xhat = xv * rstd
        sums = []
        if mod:
            sc = rest.pop(0)
            dyn = dhv * (1.0 + sc)
            dshift, dscale = _colsum(dhv), _colsum(dhv * (xhat * gv))
        else:
            dyn = dhv
        dg = _colsum(dyn * xhat)
        dxhat = dyn * gv
        dx = rstd * (dxhat - xhat * jnp.mean(dxhat * xhat, axis=-1, keepdims=True))
        if has_in:
            dx = dx + dxi
        sums = [dg] + ([dshift, dscale] if mod else [])
        return (dx,), sums

    n = ncols or x.shape[1]
    tiled = [(x, n, 0), dh] + ([dx_in] if has_in else [])
    return _rowwise(name, fn, tiled, [g] + ([scale] if mod else []), [(n, out_dtype)], [n] * (3 if mod else 1))


def _gate_bwd(name, dxn, y, g):
    def fn(dv, yv, gv):
        return (gv * dv,), (_colsum(dv * yv),)

    n = dxn.shape[1]
    return _rowwise(name, fn, [dxn, y], [g], [(n, BF16)], [n])


def _loss_head(name, x, g, target):
    n = x.shape[1]

    def fn(xv, tv, gv):
        rstd = lax.rsqrt(jnp.mean(xv * xv, axis=-1, keepdims=True) + EPS)
        xhat = xv * rstd
        err = xhat * gv - tv
        loss = 0.5 * jnp.sum(jnp.sum(err * err, axis=-1, keepdims=True) / n, axis=0, keepdims=True)
        dy = err / n
        dg = _colsum(dy * xhat)
        dxhat = dy * gv
        dx = rstd * (dxhat - xhat * jnp.mean(dxhat * xhat, axis=-1, keepdims=True))
        return (dx,), (dg, jnp.broadcast_to(loss, (1, LANES)))

    return _rowwise(name, fn, [x, target], [g], [(n, F32)], [n, LANES])


def _krope_fwd(name, kv_ext, tabk):
    def fn(xv, tv):
        t = xv * tv
        return (t + pltpu.roll(t, 64, 1),), ()

    return _rowwise(name, fn, [(kv_ext, LANES, 2), tabk], [], [(LANES, BF16)])[0]


def _krope_bwd(name, dkd, tabk):
    def fn(dv, tv):
        d = dv[:, :LANES]
        for h in range(1, N_HEADS):
            d = d + dv[:, h * LANES:(h + 1) * LANES]
        return ((d + pltpu.roll(d, 64, 1)) * tv,), ()

    return _rowwise(name, fn, [dkd, tabk], [], [(LANES, F32)])[0]


def _adamw(name, w, g, m, v):
    def fn(wv, gv, mv, vv):
        m2 = ADAM_B1 * mv + (1.0 - ADAM_B1) * gv
        v2 = ADAM_B2 * vv + (1.0 - ADAM_B2) * (gv * gv)
        m_hat = m2 / (1.0 - ADAM_B1 ** ADAM_STEP)
        v_hat = v2 / (1.0 - ADAM_B2 ** ADAM_STEP)
        delta = -ADAM_LR * (m_hat / (jnp.sqrt(v_hat) + ADAM_EPS) + ADAM_WD * wv)
        return (delta, m2, v2), ()

    r, c = w.shape
    tr = r
    for cand in (512, 256, 128, 64, 32, 16, 8):
        if r % cand == 0 and r > cand:
            tr = cand
            break
    return _rowwise(name, fn, [w, g, m, v], [], [(c, F32)] * 3, tr=tr)


def _sum8(name, parts):
    _, r, c = parts.shape
    tr = r
    for cand in (2048, 1024, 512, 256, 128, 64, 32, 16):
        if r % cand == 0 and r > cand and cand * c <= 256 * 1024:
            tr = cand
            break

    def body(p_ref, o_ref):
        acc = p_ref[0].astype(F32)
        for k in range(1, N_DEV):
            acc = acc + p_ref[k].astype(F32)
        o_ref[...] = acc

    return pl.pallas_call(
        body, name=name, grid=(r // tr,), in_specs=[pl.BlockSpec((N_DEV, tr, c), lambda i: (0, i, 0))],
        out_specs=pl.BlockSpec((tr, c), lambda i: (i, 0)), out_shape=jax.ShapeDtypeStruct((r, c), F32),
        compiler_params=_params(("parallel",)),
    )(parts)


def _mods_fwd(name, c_all, w, b):
    depth, d, n = w.shape

    def body(c_ref, w_ref, b_ref, o_ref):
        cv = c_ref[...]
        sc = (cv * (1.0 / (1.0 + jnp.exp(-cv)))).astype(BF16)
        o_ref[0] = _dot(sc, w_ref[0].astype(BF16), NN) + b_ref[0]

    return pl.pallas_call(
        body, name=name, grid=(depth,),
        in_specs=[pl.BlockSpec(c_all.shape, lambda l: (0, 0)), pl.BlockSpec((1, d, n), lambda l: (l, 0, 0)),
                  pl.BlockSpec((1, 1, n), lambda l: (l, 0, 0))],
        out_specs=pl.BlockSpec((1, c_all.shape[0], n), lambda l: (l, 0, 0)),
        out_shape=jax.ShapeDtypeStruct((depth, c_all.shape[0], n), F32),
        compiler_params=_params(("parallel",)),
    )(c_all, w, b.reshape(depth, 1, n))


def _mods_bwd(name, c_all, dm):
    depth, rows, n = dm.shape
    d = c_all.shape[1]

    def body(c_ref, dm_ref, o_ref):
        cv = c_ref[...]
        sc = (cv * (1.0 / (1.0 + jnp.exp(-cv)))).astype(BF16)
        o_ref[0] = _dot(sc, dm_ref[0].astype(BF16), TN)

    return pl.pallas_call(
        body, name=name, grid=(depth,),
        in_specs=[pl.BlockSpec(c_all.shape, lambda l: (0, 0)), pl.BlockSpec((1, rows, n), lambda l: (l, 0, 0))],
        out_specs=pl.BlockSpec((1, d, n), lambda l: (l, 0, 0)),
        out_shape=jax.ShapeDtypeStruct((depth, d, n), F32),
        compiler_params=_params(("parallel",)),
    )(c_all, dm)


POOL_TILE = 256


def _split_dot(band, val):
    hi = val.astype(BF16)
    lo = (val - hi.astype(F32)).astype(BF16)
    return _dot(band, hi, NN) + _dot(band, lo, NN)


def _pool_fwd(name, h1, x, pw, pb, ps, g1):
    s, d = h1.shape
    t = POOL_TILE

    def body(hc_ref, hp_ref, x_ref, pw_ref, pb_ref, ps_ref, g_ref, xo_ref, zb_ref, pooled_ref):
        i = pl.program_id(0)
        r = lax.broadcasted_iota(jnp.int32, (t, t), 0)
        j = lax.broadcasted_iota(jnp.int32, (t, t), 1)
        pos = (i * t + lax.broadcasted_iota(jnp.int32, (t, 1), 0) + 1).astype(F32)
        has_prev = (i > 0).astype(F32)
        for grp, w in enumerate(POOL_WINDOWS):
            cs = slice(grp * POOL_GROUP, (grp + 1) * POOL_GROUP)
            hc = hc_ref[:, cs]
            band_cur = ((r - j >= 0) & (r - j < w)).astype(BF16)
            band_prev = (r + t - j < w).astype(BF16)
            ssum = _split_dot(band_cur, hc) + has_prev * _split_dot(band_prev, hp_ref[:, cs])
            pooled = (ssum / jnp.minimum(pos, float(w)) - hc).astype(BF16)
            zb = _dot(pooled, pw_ref[grp], NN) + pb_ref[:, cs]
            xo_ref[:, cs] = x_ref[:, cs] + g_ref[:, cs] * (zb * ps_ref[:, cs])
            zb_ref[:, cs] = zb
            pooled_ref[:, cs] = pooled

    row = pl.BlockSpec((t, d), lambda i: (i, 0))
    vec = pl.BlockSpec((1, d), lambda i: (0, 0))
    return pl.pallas_call(
        body, name=name, grid=(s // t,),
        in_specs=[row, pl.BlockSpec((t, d), lambda i: (jnp.maximum(i - 1, 0), 0)), row,
                  pl.BlockSpec(pw.shape, lambda i: (0, 0, 0)), vec, vec, vec],
        out_specs=(row, row, row),
        out_shape=(jax.ShapeDtypeStruct((s, d), F32), jax.ShapeDtypeStruct((s, d), F32), jax.ShapeDtypeStruct((s, d), BF16)),
        compiler_params=_params(("parallel",)),
    )(h1, h1, x, pw, pb, ps, g1)


def _pool_bwd(name, dxn, zb, pooled, pw, ps, g1):
    s, d = dxn.shape
    t = POOL_TILE
    nt = s // t

    def body(dc_ref, dn_ref, zb_ref, pooled_ref, pw_ref, ps_ref, g_ref, dh_ref, dpw_ref, dpb_ref, dps_ref, dg_ref):
        i = pl.program_id(0)

        @pl.when(i == 0)
        def _():
            dpw_ref[...] = jnp.zeros_like(dpw_ref)
            dpb_ref[...] = jnp.zeros_like(dpb_ref)
            dps_ref[...] = jnp.zeros_like(dps_ref)
            dg_ref[...] = jnp.zeros_like(dg_ref)

        jj = lax.broadcasted_iota(jnp.int32, (t, t), 0)
        rr = lax.broadcasted_iota(jnp.int32, (t, t), 1)
        pos = (i * t + lax.broadcasted_iota(jnp.int32, (t, 1), 0) + 1).astype(F32)
        has_next = (i < nt - 1).astype(F32)
        for grp, w in enumerate(POOL_WINDOWS):
            cs = slice(grp * POOL_GROUP, (grp + 1) * POOL_GROUP)
            gv, psv, zbv, dxc = g_ref[:, cs], ps_ref[:, cs], zb_ref[:, cs], dc_ref[:, cs]
            dg_ref[:, cs] += _colsum(dxc * (zbv * psv))
            dy = gv * dxc
            dps_ref[:, cs] += _colsum(dy * zbv)
            dz = dy * psv
            dpb_ref[:, cs] += _colsum(dz)
            dzb = dz.astype(BF16)
            dpw_ref[grp] += _dot(pooled_ref[:, cs], dzb, TN)
            dp = _dot(dzb, pw_ref[grp], NT)
            dzn = (gv * dn_ref[:, cs] * psv).astype(BF16)
            dpn = _dot(dzn, pw_ref[grp], NT) * (has_next / float(w))
            band_cur = ((rr - jj >= 0) & (rr - jj < w)).astype(BF16)
            band_next = (rr + t - jj < w).astype(BF16)
            dh_ref[:, cs] = _split_dot(band_cur, dp / jnp.minimum(pos, float(w))) + _split_dot(band_next, dpn) - dp

    row = pl.BlockSpec((t, d), lambda i: (i, 0))
    vec = pl.BlockSpec((1, d), lambda i: (0, 0))
    wspec = pl.BlockSpec(pw.shape, lambda i: (0, 0, 0))
    return pl.pallas_call(
        body, name=name, grid=(nt,),
        in_specs=[row, pl.BlockSpec((t, d), lambda i: (jnp.minimum(i + 1, nt - 1), 0)), row, row, wspec, vec, vec],
        out_specs=(row, wspec, vec, vec, vec),
        out_shape=(jax.ShapeDtypeStruct((s, d), F32), jax.ShapeDtypeStruct(pw.shape, F32),
                   jax.ShapeDtypeStruct((1, d), F32), jax.ShapeDtypeStruct((1, d), F32), jax.ShapeDtypeStruct((1, d), F32)),
        compiler_params=_params(("arbitrary",)),
    )(dxn, dxn, zb, pooled, pw, ps, g1)


GLU_TILE = 512
HALO = 16
INV_SQRT2 = 0.7071067811865476
INV_SQRT_2PI = 0.3989422804014327


def _up_glu_fwd(name, h2, wa, wv, cw, cb):
    s, d = h2.shape
    f = wa.shape[1]
    tm, tn = _tile(s, 1024), _tile(f, 1408)

    def body(h_ref, hh_ref, wa_ref, wv_ref, cw_ref, cb_ref, ua_ref, gl_ref, gpv_ref, ge_ref):
        i = pl.program_id(1)
        has_prev = (i > 0).astype(F32)
        a = _dot(h_ref[...], wa_ref[...], NN).astype(BF16)
        v = _dot(h_ref[...], wv_ref[...], NN)
        above = (_dot(hh_ref[...], wa_ref[...], NN) * has_prev).astype(BF16)
        ua_ref[...] = a
        ext = jnp.concatenate([above.astype(F32), a.astype(F32)], axis=0)
        e1 = pltpu.roll(ext, 1, 0)[HALO:]
        e2 = pltpu.roll(ext, 2, 0)[HALO:]
        pre = e2 * cw_ref[0:1, :] + e1 * cw_ref[1:2, :] + ext[HALO:] * cw_ref[2:3, :] + cb_ref[...]
        cdf = 0.5 * (1.0 + lax.erf(pre * INV_SQRT2))
        ge = pre * cdf
        gl_ref[...] = (ge * v).astype(gl_ref.dtype)
        gpv_ref[...] = ((cdf + pre * (INV_SQRT_2PI * jnp.exp(-0.5 * pre * pre))) * v).astype(gpv_ref.dtype)
        ge_ref[...] = ge.astype(ge_ref.dtype)

    blk = pl.BlockSpec((tm, tn), lambda j, i: (i, j))
    wspec = pl.BlockSpec((d, tn), lambda j, i: (0, j))
    return pl.pallas_call(
        body, name=name, grid=(f // tn, s // tm),
        in_specs=[pl.BlockSpec((tm, d), lambda j, i: (i, 0)), pl.BlockSpec((HALO, d), lambda j, i: (jnp.maximum(i * (tm // HALO) - 1, 0), 0)),
                  wspec, wspec, pl.BlockSpec((3, tn), lambda j, i: (0, j)), pl.BlockSpec((1, tn), lambda j, i: (0, j))],
        out_specs=(blk, blk, blk, blk), out_shape=tuple(jax.ShapeDtypeStruct((s, f), BF16) for _ in range(4)),
        compiler_params=_params(("parallel", "parallel")),
    )(h2, h2, wa, wv, cw, cb)


def _down_glu_bwd(name, dy2, wd, ua, gpv, ge, cw):
    s, f = ua.shape
    d = dy2.shape[1]
    t, tf = min(GLU_TILE, s), _tile(f, 1408)
    nt = s // t
    te = t + HALO

    def body(dy_ref, dyn_ref, wd_ref, a_ref, ah_ref, g_ref, gn_ref, ge_ref, cw_ref, da_ref, dv_ref, dcw_ref, dcb_ref):
        i = pl.program_id(1)

        @pl.when(i == 0)
        def _():
            dcw_ref[...] = jnp.zeros_like(dcw_ref)
            dcb_ref[...] = jnp.zeros_like(dcb_ref)

        has_prev = (i > 0).astype(F32)
        has_next = (i < nt - 1).astype(F32)
        wdv = wd_ref[...]
        dgl = _dot(dy_ref[...], wdv, NT)
        dgl_below = _dot(dyn_ref[...], wdv, NT) * has_next
        dpre = jnp.concatenate([dgl * g_ref[...].astype(F32), dgl_below * gn_ref[...].astype(F32)], axis=0)
        c0, c1, c2 = cw_ref[0:1, :], cw_ref[1:2, :], cw_ref[2:3, :]
        up1 = pltpu.roll(dpre, te - 1, 0)
        up2 = pltpu.roll(dpre, te - 2, 0)
        da_ref[...] = (dpre * c2 + up1 * c1 + up2 * c0)[:t].astype(da_ref.dtype)
        dv_ref[...] = (dgl * ge_ref[...].astype(F32)).astype(dv_ref.dtype)
        ext = jnp.concatenate([ah_ref[...].astype(F32) * has_prev, a_ref[...].astype(F32)], axis=0)
        dpt = dpre[:t]
        dcb_ref[...] += _colsum(dpt)
        dcw_ref[0:1, :] += _colsum(pltpu.roll(ext, 2, 0)[HALO:] * dpt)
        dcw_ref[1:2, :] += _colsum(pltpu.roll(ext, 1, 0)[HALO:] * dpt)
        dcw_ref[2:3, :] += _colsum(ext[HALO:] * dpt)

    blk = pl.BlockSpec((t, tf), lambda j, i: (i, j))
    prev = pl.BlockSpec((HALO, tf), lambda j, i: (jnp.maximum(i * (t // HALO) - 1, 0), j))
    below = lambda i: jnp.minimum((i + 1) * (t // HALO), s // HALO - 1)
    w3 = pl.BlockSpec((3, tf), lambda j, i: (0, j))
    w1 = pl.BlockSpec((1, tf), lambda j, i: (0, j))
    return pl.pallas_call(
        body, name=name, grid=(f // tf, nt),
        in_specs=[pl.BlockSpec((t, d), lambda j, i: (i, 0)), pl.BlockSpec((HALO, d), lambda j, i: (below(i), 0)),
                  pl.BlockSpec((tf, d), lambda j, i: (j, 0)), blk, prev, blk, pl.BlockSpec((HALO, tf), lambda j, i: (below(i), j)), blk, w3],
        out_specs=(blk, blk, w3, w1),
        out_shape=(jax.ShapeDtypeStruct((s, f), BF16), jax.ShapeDtypeStruct((s, f), BF16),
                   jax.ShapeDtypeStruct((3, f), F32), jax.ShapeDtypeStruct((1, f), F32)),
        compiler_params=_params(("parallel", "arbitrary")),
    )(dy2, dy2, wd, ua, ua, gpv, gpv, ge, cw)


ATT_TILE = 512
ATT_ROWS = 256
ATT_HEADS = 4
LOG2E = 1.4426950408889634
LN2 = 0.6931471805599453


def _head_blocks_t(a, width):
    s = a.shape[0]
    t = min(ATT_TILE, s)
    return a.reshape(s // t, t, N_HEADS, width).transpose(2, 0, 3, 1)


def _causal_mask(sv, q0, k0):
    row = q0 + lax.broadcasted_iota(jnp.int32, sv.shape, 0)
    col = k0 + lax.broadcasted_iota(jnp.int32, sv.shape, 1)
    return jnp.where(col <= row, sv, NEG_BIG)


def _attn_fwd(name, q_rot, kt4, v_ext):
    s = q_rot.shape[0]
    t = min(ATT_TILE, s)
    nq = s // t
    rq = min(ATT_ROWS, t)
    nh = ATT_HEADS

    def body(q_ref, kt_ref, v_ref, o_ref, row_ref, acc_ref, m_ref):
        qi = pl.program_id(1)
        acc_ref[...] = jnp.zeros_like(acc_ref)
        m_ref[...] = jnp.full_like(m_ref, NEG_BIG)

        def step(j, masked):
            for hh in range(nh):
                cols = slice(hh * Q_EXT, (hh + 1) * Q_EXT)
                v_blk = v_ref[pl.ds(pl.multiple_of(j * t, t), t), cols]
                for r in range(t // rq):
                    rs = pl.ds(r * rq, rq)
                    sv = _dot(q_ref[rs, cols], kt_ref[hh, j], NN)
                    if masked:
                        sv = _causal_mask(sv, r * rq, 0)
                    m_prev = m_ref[hh, rs, :]
                    m_new = jnp.maximum(m_prev, jnp.max(sv, axis=-1, keepdims=True))
                    p = jnp.exp2(sv - m_new).astype(BF16)
                    acc_ref[hh, rs, :] = jnp.exp2(m_prev - m_new) * acc_ref[hh, rs, :] + _dot(p, v_blk, NN)
                    m_ref[hh, rs, :] = m_new

        def full_step(j, carry):
            step(j, False)
            return carry

        lax.fori_loop(0, qi, full_step, 0)
        step(qi, True)
        for hh in range(nh):
            l = acc_ref[hh, :, V_HEAD:V_HEAD + 1]
            o_ref[:, hh * V_HEAD:(hh + 1) * V_HEAD] = (acc_ref[hh, :, :V_HEAD] / l).astype(o_ref.dtype)
            lse = jnp.broadcast_to(m_ref[hh] + jnp.log(l) * LOG2E, (t, LANES))
            row_ref[hh, 0] = jnp.transpose(lse)[0:8, :]

    return pl.pallas_call(
        body, name=name, grid=(N_HEADS // nh, nq),
        in_specs=[pl.BlockSpec((t, nh * Q_EXT), lambda h, i: (i, h)), pl.BlockSpec((nh, nq, Q_EXT, t), lambda h, i: (h, 0, 0, 0)),
                  pl.BlockSpec((s, nh * Q_EXT), lambda h, i: (0, h))],
        out_specs=(pl.BlockSpec((t, nh * V_HEAD), lambda h, i: (i, h)), pl.BlockSpec((nh, 1, 8, t), lambda h, i: (h, i, 0, 0))),
        out_shape=(jax.ShapeDtypeStruct((s, N_HEADS * V_HEAD), BF16), jax.ShapeDtypeStruct((N_HEADS, nq, 8, t), F32)),
        scratch_shapes=[pltpu.VMEM((nh, t, Q_EXT), F32), pltpu.VMEM((nh, t, 1), F32)],
        compiler_params=_params(("parallel", "parallel")),
    )(q_rot, kt4, v_ext)


def _attn_delta(name, o, do):
    s = o.shape[0]
    t = min(ATT_TILE, s)

    def body(o_ref, do_ref, delta_ref):
        prod = do_ref[...].astype(F32) * o_ref[...].astype(F32)
        for h in range(N_HEADS):
            delta = jnp.sum(prod[:, h * V_HEAD:(h + 1) * V_HEAD], axis=-1, keepdims=True)
            delta_ref[h, 0] = jnp.transpose(jnp.broadcast_to(delta, (t, LANES)))[0:8, :]

    rows = pl.BlockSpec((t, N_HEADS * V_HEAD), lambda i: (i, 0))
    return pl.pallas_call(
        body, name=name, grid=(s // t,), in_specs=[rows, rows],
        out_specs=pl.BlockSpec((N_HEADS, 1, 8, t), lambda i: (0, i, 0, 0)),
        out_shape=jax.ShapeDtypeStruct((N_HEADS, s // t, 8, t), F32),
        compiler_params=_params(("parallel",)),
    )(o, do)


def _attn_bwd(name, kfull, v, qt4, q_rot, dot4, do, lse_row, delta_row, tabq, acc_in=None):
    s = kfull.shape[0]
    t = min(ATT_TILE, s)
    nq = s // t
    has_in = acc_in is not None

    def body(*refs):
        k_ref, v_ref, qt_ref, q_ref, dot_ref, do_ref, lse_ref, delta_ref, tab_ref = refs[:9]
        dq_ref, dkn_ref, dkd_ref, dv_ref, dq_acc_ref, acck_ref, accv_ref = refs[-7:]
        kj = pl.program_id(1)

        @pl.when(kj == 0)
        def _():
            dq_acc_ref[...] = jnp.zeros_like(dq_acc_ref)

        k_blk, v_blk = k_ref[...], v_ref[...]
        acck_ref[...] = jnp.zeros_like(acck_ref)
        accv_ref[...] = jnp.zeros_like(accv_ref)

        def step(i, masked):
            qs = pl.ds(pl.multiple_of(i * t, t), t)
            st = _dot(k_blk, qt_ref[0, i], NN)
            if masked:
                krow = lax.broadcasted_iota(jnp.int32, st.shape, 0)
                qcol = lax.broadcasted_iota(jnp.int32, st.shape, 1)
                st = jnp.where(krow <= qcol, st, NEG_BIG)
            pt = jnp.exp2(st - lse_ref[0, i, 0:1, :])
            accv_ref[...] += _dot(pt.astype(BF16), do_ref[qs, :], NN)
            dpt = _dot(v_blk, dot_ref[0, i], NN)
            dst = (pt * (dpt - delta_ref[0, i, 0:1, :])).astype(BF16)
            acck_ref[...] += _dot(dst, q_ref[qs, :], NN)
            dq_acc_ref[qs, :] += _dot(dst, k_blk, TN)

        def full_step(i, carry):
            step(i, False)
            return carry

        step(kj, True)
        lax.fori_loop(kj + 1, nq, full_step, 0)
        dk = acck_ref[...] * LN2
        if has_in:
            dkn_ref[...] = dk[:, :QK_NOPE] + refs[9][...]
            dkd_ref[...] = dk[:, QK_NOPE:] + refs[10][...]
            dv_ref[...] = accv_ref[...] + refs[11][...]
        else:
            dkn_ref[...] = dk[:, :QK_NOPE]
            dkd_ref[...] = dk[:, QK_NOPE:]
            dv_ref[...] = accv_ref[...]

        @pl.when(kj == nq - 1)
        def _():
            dq_ref[...] = (dq_acc_ref[...] * (tab_ref[...] * LN2)).astype(dq_ref.dtype)

    kblk = pl.BlockSpec((t, LANES), lambda h, j: (j, h))
    col = pl.BlockSpec((s, LANES), lambda h, j: (0, h))
    q_all = pl.BlockSpec((s, Q_EXT), lambda h, j: (0, h))
    stat = pl.BlockSpec((1, nq, 8, t), lambda h, j: (h, 0, 0, 0))
    ins = [kfull, v, qt4, q_rot, dot4, do, lse_row, delta_row, tabq]
    in_specs = [pl.BlockSpec((t, Q_EXT), lambda h, j: (j, h)), kblk, pl.BlockSpec((1, nq, Q_EXT, t), lambda h, j: (h, 0, 0, 0)),
                q_all, pl.BlockSpec((1, nq, V_HEAD, t), lambda h, j: (h, 0, 0, 0)), col, stat, stat,
                pl.BlockSpec((s, Q_EXT), lambda h, j: (0, 0))]
    if has_in:
        ins += list(acc_in)
        in_specs += [kblk, kblk, kblk]
    wide = jax.ShapeDtypeStruct((s, N_HEADS * LANES), F32)
    return pl.pallas_call(
        body, name=name, grid=(N_HEADS, nq), in_specs=in_specs, out_specs=(q_all, kblk, kblk, kblk),
        out_shape=(jax.ShapeDtypeStruct((s, N_HEADS * Q_EXT), BF16), wide, wide, wide),
        scratch_shapes=[pltpu.VMEM((s, Q_EXT), F32), pltpu.VMEM((t, Q_EXT), F32), pltpu.VMEM((t, LANES), F32)],
        compiler_params=_params(("parallel", "arbitrary")),
    )(*ins)


def _swap_halves(w):
    half = w.shape[-1] // 2
    return jnp.concatenate([-w[..., half:], w[..., :half]], axis=-1)


def _unswap_halves(g):
    half = g.shape[-1] // 2
    return jnp.concatenate([g[..., half:], -g[..., :half]], axis=-1)


def _extend_w_dkv(w):
    return jnp.concatenate([w, _swap_halves(w[:, KV_RANK:])], axis=-1)


def _fold_w_dkv_grad(g):
    rope = g[:, KV_RANK:KV_RANK + QK_ROPE] + _unswap_halves(g[:, KV_RANK + QK_ROPE:])
    return jnp.concatenate([g[:, :KV_RANK], rope], axis=-1)


def _rope_tables(positions):
    inv = 1.0 / (ROPE_THETA ** (jnp.arange(0, QK_ROPE, 2, dtype=F32) / QK_ROPE))
    ang = positions.astype(F32)[:, None] * inv
    cos, sin = jnp.cos(ang), jnp.sin(ang)
    tabk = jnp.concatenate([cos, cos, sin, sin], axis=-1)
    scale = QK_HEAD ** -0.5 * LOG2E
    tabq = jnp.concatenate([jnp.full((positions.shape[0], QK_NOPE), scale, F32), tabk * scale], axis=-1)
    return tabq, tabk


def _forward_backward(x, target, mods, tabq, tabk, final_g, fetch, push):
    row = lambda vec: vec.reshape(1, -1)
    mod = [[row(mods[l, k * D_MODEL:(k + 1) * D_MODEL]) for k in range(N_MOD)] for l in range(DEPTH)]
    saved, weights = [], []
    kv = None
    for l in range(DEPTH):
        w, tok = fetch(l, x)
        sh1, sc1, g1, sh2, sc2, g2 = mod[l]
        sh1 = sh1 + tok
        if l == N_A_LAYERS:
            kvn = _rms_fwd("kvin_fwd", x, row(w["kv_in_g"]))
            kv_ext = _mm("dkv_fwd", kvn, w["w_dkv_ext"], out_dtype=F32)
            ckv = _rms_fwd("ckv_fwd", kv_ext, row(w["ckv_norm_g"]), ncols=KV_RANK)
            kd = _krope_fwd("krope_fwd", kv_ext, tabk)
            kn, v = _mm("uk_fwd", ckv, w["w_uk"]), _mm("uv_fwd", ckv, w["w_uv"])
            heads = lambda a: [a[:, h * LANES:(h + 1) * LANES] for h in range(N_HEADS)]
            kfull = jnp.concatenate([part for kh in heads(kn) for part in (kh, kd)], axis=-1)
            v_ext = jnp.concatenate([part for vh in heads(v) for part in (vh, jnp.ones_like(vh))], axis=-1)
            kv = dict(x=x, kvn=kvn, kv_ext=kv_ext, ckv=ckv, v=v, kfull=kfull, v_ext=v_ext,
                      kt4=_head_blocks_t(kfull, Q_EXT))
        x_in = x
        if l < N_A_LAYERS:
            h1 = _rms_fwd(f"norm1_fwd_{l}", x, row(w["norm1_g"]), sc1, sh1, out_dtype=F32)
            x_mid, zb, pooled = _pool_fwd(f"pool_fwd_{l}", h1, x, w["pool_w"], row(w["pool_b"]), row(w["pool_scale"]), g1)
            mix = (zb, pooled)
        else:
            h1 = _rms_fwd(f"norm1_fwd_{l}", x, row(w["norm1_g"]), sc1, sh1)
            cq_pre = _mm(f"dq_fwd_{l}", h1, w["w_dq"], out_dtype=F32)
            cq = _rms_fwd(f"qnorm_fwd_{l}", cq_pre, row(w["q_norm_g"]))
            q_rot = _mm(f"uq_fwd_{l}", cq, w["w_uq_ext"], rowtab=tabq)
            o, lse_row = _attn_fwd(f"attn_fwd_{l}", q_rot, kv["kt4"], kv["v_ext"])
            y, x_mid = _mm(f"wo_fwd_{l}", o, w["w_o"], resid=x, gate=g1)
            mix = (h1, cq_pre, cq, q_rot, o, lse_row, y)
        h2 = _rms_fwd(f"norm2_fwd_{l}", x_mid, row(w["norm2_g"]), sc2, sh2)
        w_up_a, w_up_v = w["w_up"](h2)
        ua, gl, gpv, ge = _up_glu_fwd(f"up_glu_fwd_{l}", h2, w_up_a, w_up_v, w["conv_w"], row(w["conv_b"]))
        w_down = w["w_down"](gl)
        y2, x = _mm(f"down_fwd_{l}", gl, w_down, resid=x_mid, gate=g2)
        saved.append((x_in, x_mid, h2, ua, gpv, ge, gl, y2, mix))
        weights.append(dict(w, w_up_a=w_up_a, w_up_v=w_up_v, w_down=w_down))

    dx, dfinal_g, loss = _loss_head("loss_head", x, row(final_g), target)
    g = {"final_g": dfinal_g.reshape(-1)}
    per_layer = {k: [None] * DEPTH for k in ("norm1_g", "norm2_g", "conv_w", "conv_b")}
    per_a = {k: [None] * N_A_LAYERS for k in ("pool_b", "pool_scale")}
    per_b = {k: [None] * N_B_LAYERS for k in ("q_norm_g",)}
    dmods = [None] * DEPTH
    dkv = None
    tok = 0.0
    for l in reversed(range(DEPTH)):
        w, big = weights[l], {}
        sh1, sc1, g1, sh2, sc2, g2 = mod[l]
        g2 = g2 + tok
        x_in, x_mid, h2, ua, gpv, ge, gl, y2, mix = saved[l]
        dy2, dg2 = _gate_bwd(f"gate2_bwd_{l}", dx, y2, g2)
        tok = push(l, "down", dict(w_down=_mm(f"down_wgrad_{l}", gl, dy2, mode="tn", tm_cap=1408)), None)
        da, dv_, dcw, dcb = _down_glu_bwd(f"down_glu_bwd_{l}", dy2, w["w_down"], ua, gpv, ge, w["conv_w"] + tok)
        dh2 = _mm(f"up_bwd_{l}", da, w["w_up_a"], mode="nt", out_dtype=F32, second=(dv_, w["w_up_v"]))
        tok = push(l, "up", dict(w_up_a=_mm(f"up_a_wgrad_{l}", h2, da, mode="tn"), w_up_v=_mm(f"up_v_wgrad_{l}", h2, dv_, mode="tn")), None)
        per_layer["conv_w"][l], per_layer["conv_b"][l] = dcw, dcb.reshape(-1)
        dx_mid, dn2, dsh2, dsc2 = _rms_bwd(f"norm2_bwd_{l}", x_mid, row(w["norm2_g"]), dh2, sc2 + tok, dx_in=dx)
        per_layer["norm2_g"][l] = dn2.reshape(-1)
        if l < N_A_LAYERS:
            zb, pooled = mix
            dh1, dpw, dpb, dps, dg1 = _pool_bwd(f"pool_bwd_{l}", dx_mid, zb, pooled, w["pool_w"], row(w["pool_scale"]), g1)
            big["pool_w"] = dpw
            per_a["pool_b"][l], per_a["pool_scale"][l] = dpb.reshape(-1), dps.reshape(-1)
        else:
            j = l - N_A_LAYERS
            h1, cq_pre, cq, q_rot, o, lse_row, y = mix
            dy, dg1 = _gate_bwd(f"gate1_bwd_{l}", dx_mid, y, g1)
            do = _mm(f"wo_bwd_{l}", dy, w["w_o"], mode="nt")
            big["w_o"] = _mm(f"wo_wgrad_{l}", o, dy, mode="tn")
            delta_row = _attn_delta(f"attn_delta_{l}", o, do)
            dq_ext, *dkv = _attn_bwd(f"attn_bwd_{l}", kv["kfull"], kv["v"], _head_blocks_t(q_rot, Q_EXT), q_rot, _head_blocks_t(do, V_HEAD), do,
                                     lse_row, delta_row, tabq, acc_in=dkv)
            dcq = _mm(f"uq_bwd_{l}", dq_ext, w["w_uq_ext"], mode="nt", out_dtype=F32)
            big["w_uq_ext"] = _mm(f"uq_wgrad_{l}", cq, dq_ext, mode="tn", out_dtype=F32)
            dcq_pre, dqn = _rms_bwd(f"qnorm_bwd_{l}", cq_pre, row(w["q_norm_g"]), dcq, out_dtype=BF16)
            per_b["q_norm_g"][j] = dqn.reshape(-1)
            dh1 = _mm(f"dq_bwd_{l}", dcq_pre, w["w_dq"], mode="nt")
            big["w_dq"] = _mm(f"dq_wgrad_{l}", h1, dcq_pre, mode="tn")
        dx, dn1, dsh1, dsc1 = _rms_bwd(f"norm1_bwd_{l}", x_in, row(w["norm1_g"]), dh1, sc1, dx_in=dx_mid)
        per_layer["norm1_g"][l] = dn1.reshape(-1)
        dmods[l] = jnp.concatenate([dsh1, dsc1, dg1, dsh2, dsc2, dg2], axis=-1).reshape(-1)
        if l == N_A_LAYERS:
            dkn, dkd, dv = dkv
            dckv = _mm("ukv_bwd", dkn, w["w_uk"], mode="nt", out_dtype=F32, second=(dv, w["w_uv"]))
            big["w_uk"] = _mm("uk_wgrad", kv["ckv"], dkn, mode="tn")
            big["w_uv"] = _mm("uv_wgrad", kv["ckv"], dv, mode="tn")
            dkr = _krope_bwd("krope_bwd", dkd, tabk)
            dc, dckv_g = _rms_bwd("ckv_bwd", kv["kv_ext"], row(w["ckv_norm_g"]), dckv, ncols=KV_RANK, out_dtype=BF16)
            dkv_ext = jnp.concatenate([dc, dkr.astype(BF16)], axis=-1)
            dkvn = _mm("dkv_bwd", dkv_ext, w["w_dkv_ext"], mode="nt")
            big["w_dkv_ext"] = _mm("dkv_wgrad", kv["kvn"], dkv_ext, mode="tn", out_dtype=F32)
            dx, dkv_in_g = _rms_bwd("kvin_bwd", kv["x"], row(w["kv_in_g"]), dkvn, dx_in=dx)
            g["ckv_norm_g"], g["kv_in_g"] = dckv_g.reshape(-1), dkv_in_g.reshape(-1)
        tok = push(l, "mix", big, dx)
    for group in (per_layer, per_a, per_b):
        for k, vals in group.items():
            g[k] = jnp.stack(vals)
    return loss, dx, g, jnp.stack(dmods)


def _my_index():
    return 4 * lax.axis_index("x") + 2 * lax.axis_index("y") + lax.axis_index("c")


def _peer(k):
    x, y, c = lax.axis_index("x"), lax.axis_index("y"), lax.axis_index("c")
    return (1 - x if k & 4 else x, 1 - y if k & 2 else y, 1 - c if k & 1 else c)


def _index_of(pos):
    return 4 * pos[0] + 2 * pos[1] + pos[2]


def _exchange_many(name, arrays, scatter):
    n = len(arrays)
    blocks = [tuple(a.shape[1:]) if scatter else tuple(a.shape) for a in arrays]

    def body(*refs):
        x_refs, o_refs = refs[:n], refs[n:2 * n]
        send_sems, recv_sems, local_sems = refs[2 * n:]
        me = _my_index()
        started = []
        for a in range(n):
            mine = pltpu.make_async_copy(x_refs[a].at[me] if scatter else x_refs[a], o_refs[a].at[me], local_sems.at[a])
            mine.start()
            started.append(mine)
        sends = []
        for k in range(1, N_DEV):
            peer = _peer(k)
            for a in range(n):
                cp = pltpu.make_async_remote_copy(
                    src_ref=x_refs[a].at[_index_of(peer)] if scatter else x_refs[a], dst_ref=o_refs[a].at[me],
                    send_sem=send_sems.at[a, k - 1], recv_sem=recv_sems.at[a, k - 1], device_id=peer, device_id_type=MESH)
                cp.start()
                sends.append(cp)
        for k in range(1, N_DEV):
            peer = _peer(k)
            for a in range(n):
                pltpu.make_async_remote_copy(
                    src_ref=x_refs[a].at[me] if scatter else x_refs[a], dst_ref=o_refs[a].at[_index_of(peer)],
                    send_sem=send_sems.at[a, k - 1], recv_sem=recv_sems.at[a, k - 1], device_id=peer, device_id_type=MESH).wait_recv()
        for cp in sends:
            cp.wait_send()
        for mine in started:
            mine.wait()

    return pl.pallas_call(
        body, name=name, out_shape=tuple(jax.ShapeDtypeStruct((N_DEV,) + blk, a.dtype) for blk, a in zip(blocks, arrays)),
        in_specs=[pl.BlockSpec(memory_space=pl.ANY)] * n, out_specs=tuple([pl.BlockSpec(memory_space=pl.ANY)] * n),
        scratch_shapes=[pltpu.SemaphoreType.DMA((n, N_DEV - 1)), pltpu.SemaphoreType.DMA((n, N_DEV - 1)), pltpu.SemaphoreType.DMA((n,))],
    )(*arrays)


def _exchange(name, x, scatter):
    return _exchange_many(name, [x], scatter)[0]


HBM_SPEC = pl.BlockSpec(memory_space=pltpu.HBM)
SEM_SPEC = pl.BlockSpec(memory_space=pltpu.SEMAPHORE)
DATAFLOW = pltpu.SideEffectType.DATAFLOW_SIDE_EFFECTING


def _remote_copies(x_refs, land_refs, send_sems, recv_sems, scatter, numbers=None):
    me = _my_index()
    numbers = list(range(len(x_refs))) if numbers is None else numbers
    out, inc = [], []
    for a in range(len(x_refs)):
        for k in range(1, N_DEV):
            peer = _peer(k)
            pair = numbers[a] * (N_DEV - 1) + k - 1
            sems = dict(send_sem=send_sems.at[pair], recv_sem=recv_sems.at[pair], device_id=peer, device_id_type=MESH)
            out.append(pltpu.make_async_remote_copy(
                src_ref=x_refs[a].at[_index_of(peer)] if scatter else x_refs[a], dst_ref=land_refs[a].at[me], **sems))
            inc.append(pltpu.make_async_remote_copy(
                src_ref=x_refs[a].at[me] if scatter else x_refs[a], dst_ref=land_refs[a].at[_index_of(peer)], **sems))
    return out, inc


def _exchange_start(name, arrays, scatter):
    n = len(arrays)
    blocks = [tuple(a.shape[1:]) if scatter else tuple(a.shape) for a in arrays]

    def body(*refs):
        x_refs, land_refs = refs[:n], refs[n:2 * n]
        send_sems, recv_sems = refs[2 * n], refs[2 * n + 1]
        for cp in _remote_copies(x_refs, land_refs, send_sems, recv_sems, scatter)[0]:
            cp.start()
        refs[-1][...] = jnp.zeros_like(refs[-1])

    sem_type = pltpu.SemaphoreType.DMA((n * (N_DEV - 1),))
    lands =[pltpu.with_memory_space_constraint(lax.empty((N_DEV,) + blk, a.dtype), pltpu.HBM) for blk, a in zip(blocks, arrays)]
    srcs = [pltpu.with_memory_space_constraint(a, pltpu.HBM) for a in arrays]
    res = pl.pallas_call(
        body, name=name,
        out_shape=(sem_type, sem_type, *[pltpu.HBM(a.shape, a.dtype) for a in srcs + lands], jax.ShapeDtypeStruct((8, LANES), F32)),
        in_specs=[HBM_SPEC] * (2 * n), out_specs=(SEM_SPEC, SEM_SPEC, *[HBM_SPEC] * (2 * n), pl.BlockSpec(memory_space=pltpu.VMEM)),
        input_output_aliases={i: 2 + i for i in range(2 * n)},
        compiler_params=pltpu.CompilerParams(has_side_effects=DATAFLOW),
    )(*srcs, *lands)
    return (res[0], res[1], list(res[2:2 + n]), list(res[2 + n:2 + 2 * n])), res[-1]


def _exchange_wait(name, handles, after, scatter, which=None):
    send_sems, recv_sems, srcs, lands = handles
    which = list(range(len(srcs))) if which is None else list(which)
    srcs, lands = [srcs[a] for a in which], [lands[a] for a in which]
    n = len(srcs)

    def body(*refs):
        x_refs, land_refs = refs[:n], refs[n:2 * n]
        out, inc = _remote_copies(x_refs, land_refs, refs[2 * n], refs[2 * n + 1], scatter, which)
        for cp in out:
            cp.wait_send()
        for cp in inc:
            cp.wait_recv()

    res = pl.pallas_call(
        body, name=name, out_shape=tuple(pltpu.HBM(a.shape, a.dtype) for a in srcs + lands),
        in_specs=[HBM_SPEC] * (2 * n) + [SEM_SPEC, SEM_SPEC, pl.BlockSpec(memory_space=pl.ANY)], out_specs=tuple([HBM_SPEC] * (2 * n)),
        input_output_aliases={i: i for i in range(2 * n)},
        compiler_params=pltpu.CompilerParams(has_side_effects=DATAFLOW),
    )(*srcs, *lands, send_sems, recv_sems, after)
    return list(res[n:])


def _pack(arrays, dtype, row_multiple):
    flat = jnp.concatenate([a.astype(dtype).reshape(-1) for a in arrays])
    rows = -(-flat.shape[0] // (LANES * row_multiple)) * row_multiple
    return jnp.pad(flat, (0, rows * LANES - flat.shape[0])).reshape(rows, LANES)


def _unpack(packed, shapes):
    lead = packed.shape[:-2]
    flat = packed.reshape(lead + (-1,))
    out, off = [], 0
    for shp in shapes:
        size = 1
        for d in shp:
            size *= d
        out.append(flat[..., off:off + size].reshape(lead + tuple(shp)))
        off += size
    return out


def _unshard(g8, axis):
    return jnp.concatenate([g8[j] for j in range(N_DEV)], axis=axis)


def _shard8(full, axis):
    n = full.shape[axis] // N_DEV
    return jnp.stack([lax.slice_in_dim(full, j * n, (j + 1) * n, axis=axis) for j in range(N_DEV)])


VECTOR_WEIGHTS = (("pool_b", 1), ("pool_scale", 1), ("conv_w", 2))
REPLICATED_WEIGHTS = ("norm1_g", "norm2_g", "kv_in_g", "ckv_norm_g", "q_norm_g", "conv_b", "final_g")
WEIGHT_ORDER = ("mod_w", "mod_b", "norm1_g", "norm2_g", "pool_w", "pool_b", "pool_scale", "kv_in_g", "w_dkv", "ckv_norm_g", "w_uk",
                "w_uv", "w_dq", "q_norm_g", "w_uq", "w_o", "w_up", "conv_w", "conv_b", "w_down", "final_g")
SMALL_ROW_MULTIPLE = 16


def _as_2d(a):
    if a.ndim == 1:
        return a.reshape(-1, LANES)
    return a.reshape(-1, a.shape[-1])


def kernel(x, c, positions, mod_w, mod_b, norm1_g, norm2_g, pool_w, pool_b, pool_scale, kv_in_g, w_dkv, ckv_norm_g, w_uk, w_uv, w_dq, q_norm_g, w_uq, w_o, w_up, conv_w, conv_b, w_down, final_g, loss_target, m_mod_w, m_mod_b, m_norm1_g, m_norm2_g, m_pool_w, m_pool_b, m_pool_scale, m_kv_in_g, m_w_dkv, m_ckv_norm_g, m_w_uk, m_w_uv, m_w_dq, m_q_norm_g, m_w_uq, m_w_o, m_w_up, m_conv_w, m_conv_b, m_w_down, m_final_g, v_mod_w, v_mod_b, v_norm1_g, v_norm2_g, v_pool_w, v_pool_b, v_pool_scale, v_kv_in_g, v_w_dkv, v_ckv_norm_g, v_w_uk, v_w_uv, v_w_dq, v_q_norm_g, v_w_uq, v_w_o, v_w_up, v_conv_w, v_conv_b, v_w_down, v_final_g):
    shard = dict(mod_w=mod_w, mod_b=mod_b, norm1_g=norm1_g, norm2_g=norm2_g, pool_w=pool_w, pool_b=pool_b, pool_scale=pool_scale,
                 kv_in_g=kv_in_g, w_dkv=w_dkv, ckv_norm_g=ckv_norm_g, w_uk=w_uk, w_uv=w_uv, w_dq=w_dq, q_norm_g=q_norm_g, w_uq=w_uq,
                 w_o=w_o, w_up=w_up, conv_w=conv_w, conv_b=conv_b, w_down=w_down, final_g=final_g)
    mom_m = dict(mod_w=m_mod_w, mod_b=m_mod_b, norm1_g=m_norm1_g, norm2_g=m_norm2_g, pool_w=m_pool_w, pool_b=m_pool_b,
                 pool_scale=m_pool_scale, kv_in_g=m_kv_in_g, w_dkv=m_w_dkv, ckv_norm_g=m_ckv_norm_g, w_uk=m_w_uk, w_uv=m_w_uv,
                 w_dq=m_w_dq, q_norm_g=m_q_norm_g, w_uq=m_w_uq, w_o=m_w_o, w_up=m_w_up, conv_w=m_conv_w, conv_b=m_conv_b,
                 w_down=m_w_down, final_g=m_final_g)
    mom_v = dict(mod_w=v_mod_w, mod_b=v_mod_b, norm1_g=v_norm1_g, norm2_g=v_norm2_g, pool_w=v_pool_w, pool_b=v_pool_b,
                 pool_scale=v_pool_scale, kv_in_g=v_kv_in_g, w_dkv=v_w_dkv, ckv_norm_g=v_ckv_norm_g, w_uk=v_w_uk, w_uv=v_w_uv,
                 w_dq=v_w_dq, q_norm_g=v_q_norm_g, w_uq=v_w_uq, w_o=v_w_o, w_up=v_w_up, conv_w=v_conv_w, conv_b=v_conv_b,
                 w_down=v_w_down, final_g=v_final_g)
    me = _my_index()
    d6 = N_MOD * D_MODEL
    mod_cols = d6 // N_DEV

    small_in = [c] + [shard[k] for k, _ in VECTOR_WEIGHTS]
    small_all = _exchange("gather_vectors", _pack(small_in, F32, SMALL_ROW_MULTIPLE), scatter=False)
    parts = _unpack(small_all, [a.shape for a in small_in])
    c_all = jnp.pad(parts[0].reshape(N_DEV, D_MODEL), ((0, N_DEV), (0, 0)))
    vec = {k: _unshard(p, ax) for (k, ax), p in zip(VECTOR_WEIGHTS, parts[1:])}

    my_mod_b = lax.dynamic_slice_in_dim(mod_b, me * mod_cols, mod_cols, axis=1)
    mods_mine = _mods_fwd("mods_fwd", c_all, mod_w, my_mod_b)
    mods_all = _exchange("gather_mods", _pack([mods_mine], F32, SMALL_ROW_MULTIPLE), scatter=False)
    mods_all = _unpack(mods_all, [mods_mine.shape])[0]
    mods = lax.dynamic_index_in_dim(mods_all, me, axis=2, keepdims=False)
    mods = jnp.moveaxis(mods, 0, 1).reshape(DEPTH, d6)

    tabq, tabk = _rope_tables(positions[0])
    half = N_DEV // 2
    up_cols = shard["w_up"].shape[2]
    cat = lambda a, axis, lo=0, hi=N_DEV: jnp.concatenate([a[j] for j in range(lo, hi)], axis=axis)

    def stage_pieces(l):
        out = {"pool_w": shard["pool_w"].astype(BF16)} if l == 0 else {}
        if l == N_A_LAYERS:
            out.update({k: shard[k].astype(BF16) for k in ("w_dkv", "w_uk", "w_uv")})
        if l >= N_A_LAYERS:
            out.update({k: shard[k][l - N_A_LAYERS].astype(BF16) for k in ("w_dq", "w_uq", "w_o")})
        out.update(w_up=shard["w_up"][l].astype(BF16), w_down=shard["w_down"][l].astype(BF16))
        return out

    gathers, pool_all = {}, []

    def start_gather(l, behind=None):
        pieces = stage_pieces(l)
        if behind is not None:
            pieces, _ = lax.optimization_barrier((pieces, behind))
        handles, token = _exchange_start(f"gather_start_{l}", list(pieces.values()), scatter=False)
        gathers[l] = (handles, pieces)
        return token[0, 0]

    def wait_gather(l, keys, after, tag=""):
        handles, pieces = gathers[l]
        which = [list(pieces).index(k) for k in keys]
        lands = _exchange_wait(f"gather_wait_{l}{tag}", handles, after, scatter=False, which=which)
        return dict(zip(keys, own_slot(lands, [pieces[k] for k in keys])))

    def whole_weights(l, got):
        w = dict(norm1_g=norm1_g[l], norm2_g=norm2_g[l], conv_w=vec["conv_w"][l], conv_b=conv_b[l])
        if l == 0:
            pool_all.append(got["pool_w"])
        if l < N_A_LAYERS:
            w.update(pool_w=cat(pool_all[0][:, l], 1), pool_b=vec["pool_b"][l], pool_scale=vec["pool_scale"][l])
        else:
            rope = got["w_uq"][..., QK_NOPE:]
            ext = jnp.concatenate([got["w_uq"][..., :QK_NOPE], rope, _swap_halves(rope)], axis=-1)
            w.update(w_dq=got["w_dq"].reshape(D_MODEL, Q_RANK), w_uq_ext=cat(ext, -1), w_o=got["w_o"].reshape(D_MODEL, D_MODEL),
                     q_norm_g=q_norm_g[l - N_A_LAYERS])
        if l == N_A_LAYERS:
            w.update(w_dkv_ext=_extend_w_dkv(got["w_dkv"].reshape(D_MODEL, KV_RANK + QK_ROPE)), w_uk=cat(got["w_uk"], -1),
                     w_uv=cat(got["w_uv"], -1), kv_in_g=kv_in_g, ckv_norm_g=ckv_norm_g)
        return w

    def own_slot(lands, own):
        return [lax.dynamic_update_index_in_dim(p, o, me, 0) for p, o in zip(lands, own)]

    def fetch(l, after):
        up_parts = lambda g8: (cat(g8, -1, 0, half), cat(g8, -1, half, N_DEV))
        if l == 0:
            start_gather(0, behind=mods)
            got = wait_gather(0, ["pool_w"], mods, "_pool")
            w_up = lambda aft: up_parts(wait_gather(0, ["w_up"], aft, "_up")["w_up"])
            w_down = lambda aft: wait_gather(0, ["w_down"], aft, "_down")["w_down"].reshape(D_FF, D_MODEL)
        else:
            got = wait_gather(l, list(gathers[l][1]), after)
            up, down = up_parts(got["w_up"]), got["w_down"].reshape(D_FF, D_MODEL)
            w_up, w_down = (lambda aft: up), (lambda aft: down)
        w = dict(whole_weights(l, got), w_up=w_up, w_down=w_down)
        return w, (start_gather(l + 1) if l + 1 < DEPTH else 0.0)

    scatters, pending, pool_grads, piece_grads = {}, {}, {}, {}

    def reduce_pieces(l, keys, got):
        for k, p in zip(keys, got):
            piece_grads[(k, l)] = _sum8(f"sum_grads_{k}_{l}", p.reshape(N_DEV, -1, p.shape[-1])).reshape(p.shape[1:])

    def start_scatter(name, sent):
        sent = {k: a.astype(BF16) for k, a in sent.items()}
        handles, token = _exchange_start(f"scatter_start_{name}", list(sent.values()), scatter=True)
        scatters[name] = (handles, list(sent), [lax.dynamic_index_in_dim(a, me, 0, keepdims=False) for a in sent.values()])
        return token[0, 0]

    def finish_scatter(name, l, after):
        handles, keys, own = scatters.pop(name)
        reduce_pieces(l, keys, own_slot(_exchange_wait(f"scatter_wait_{name}", handles, after, scatter=True), own))

    def push(l, part, big, after):
        cut = lambda a, n, axis: jnp.stack([lax.slice_in_dim(a, j * n, (j + 1) * n, axis=axis) for j in range(N_DEV)])
        sent = {}
        if part == "down":
            sent["w_down"] = big["w_down"].reshape(N_DEV, D_FF // N_DEV, D_MODEL)
        elif part == "up":
            sent["w_up"] = jnp.stack([lax.slice_in_dim(big[half_], j * up_cols, (j + 1) * up_cols, axis=1)
                                      for half_ in ("w_up_a", "w_up_v") for j in range(half)])
        elif l < N_A_LAYERS:
            pool_grads[l] = big["pool_w"]
        else:
            ext = cut(big["w_uq_ext"], Q_EXT, 1)
            rope = ext[..., QK_NOPE:QK_HEAD] + _unswap_halves(ext[..., QK_HEAD:])
            sent.update(w_dq=big["w_dq"].reshape(N_DEV, D_MODEL // N_DEV, Q_RANK), w_uq=jnp.concatenate([ext[..., :QK_NOPE], rope], axis=-1),
                        w_o=big["w_o"].reshape(N_DEV, D_MODEL // N_DEV, D_MODEL))
        if part == "mix" and l == N_A_LAYERS:
            sent.update(w_dkv=_fold_w_dkv_grad(big["w_dkv_ext"]).reshape(N_DEV, D_MODEL // N_DEV, KV_RANK + QK_ROPE),
                        w_uk=cut(big["w_uk"], QK_NOPE, 1), w_uv=cut(big["w_uv"], V_HEAD, 1))
        if l == 0 and part != "mix":
            return start_scatter(f"0_{part}", sent)
        if l == 0:
            finish_scatter("1", 1, after)
            pool = _shard8(jnp.stack([pool_grads[a] for a in range(N_A_LAYERS)]), 2).astype(BF16)
            reduce_pieces(0, ["pool_w"], _exchange_many("scatter_pool_grads", [pool], scatter=True))
            return 0.0
        pending.setdefault(l, {}).update(sent)
        if part != "mix":
            return 0.0
        if l + 1 < DEPTH:
            finish_scatter(str(l + 1), l + 1, after)
        return start_scatter(str(l), pending.pop(l))

    loss_row, dx, g, dmods = _forward_backward(x[0], loss_target[0], mods, tabq, tabk, final_g, fetch, push)
    layers_of = lambda k, ls: jnp.stack([piece_grads[(k, l)] for l in ls])
    grads = dict(w_dkv=piece_grads[("w_dkv", N_A_LAYERS)], w_uk=piece_grads[("w_uk", N_A_LAYERS)], w_uv=piece_grads[("w_uv", N_A_LAYERS)])
    for k in ("w_dq", "w_uq", "w_o"):
        grads[k] = layers_of(k, range(N_A_LAYERS, DEPTH))

    small_names = REPLICATED_WEIGHTS + tuple(k for k, _ in VECTOR_WEIGHTS)
    small_out = [dmods] + [g[k] for k in small_names] + [loss_row]
    small_shapes = [a.shape for a in small_out]
    small_got = _exchange("gather_small_grads", _pack(small_out, F32, SMALL_ROW_MULTIPLE), scatter=False)
    summed = _unpack(_sum8("sum_small_grads", small_got), small_shapes)
    grads["mod_b"] = summed[0]
    for k, s in zip(small_names, summed[1:-1]):
        grads[k] = s
    for k, ax in VECTOR_WEIGHTS:
        n = shard[k].shape[ax]
        grads[k] = lax.dynamic_slice_in_dim(grads[k], me * n, n, axis=ax)
    loss = summed[-1][0, 0]
    dmods_all = _unpack(small_got, small_shapes)[0]
    dm_mine = lax.dynamic_slice_in_dim(dmods_all, me * mod_cols, mod_cols, axis=2)
    dm_mine = jnp.pad(jnp.moveaxis(dm_mine, 0, 1), ((0, 0), (0, N_DEV), (0, 0)))
    grads["mod_w"] = _mods_bwd("mods_bwd", c_all, dm_mine)

    delta, new_m, new_v = {}, {}, {}

    def adamw(k):
        shp = shard[k].shape
        grads[k] = grads[k].reshape(shp)
        view = (lambda a: jnp.swapaxes(a, 1, 2)) if k == "w_up" else (lambda a: a)
        ops = [view(a) for a in (shard[k], grads[k], mom_m[k], mom_v[k])]
        ops[1] = lax.optimization_barrier(ops[1])
        res = _adamw(f"adamw_{k}", *[_as_2d(a) for a in ops])
        delta[k], new_m[k], new_v[k] = [view(r.reshape(ops[0].shape)) for r in res]
        grads[k] = view(ops[1])

    late = ("w_up", "w_down", "pool_w")
    for k in WEIGHT_ORDER:
        if k not in late:
            adamw(k)
    finish_scatter("0_down", 0, delta["final_g"])
    finish_scatter("0_up", 0, delta["final_g"])
    grads.update(w_up=layers_of("w_up", range(DEPTH)), w_down=layers_of("w_down", range(DEPTH)), pool_w=piece_grads[("pool_w", 0)])
    for k in late:
        adamw(k)
    return (loss, dx[None], *[grads[k] for k in WEIGHT_ORDER], *[delta[k] for k in WEIGHT_ORDER],
            *[new_m[k] for k in WEIGHT_ORDER], *[new_v[k] for k in WEIGHT_ORDER])
```

```python
import functools

import jax
import jax.numpy as jnp
from jax import lax
from jax.experimental import pallas as pl
from jax.experimental.pallas import tpu as pltpu

F32 = jnp.float32
BF16 = jnp.bfloat16

D_MODEL = 1024
DEPTH = 4
N_A_LAYERS = 2
N_B_LAYERS = 2
POOL_WINDOWS = (2, 4, 8, 16)
POOL_GROUP = 256
N_HEADS = 8
QK_NOPE = 128
QK_ROPE = 64
V_HEAD = 128
QK_HEAD = QK_NOPE + QK_ROPE
Q_RANK = 384
KV_RANK = 256
ROPE_THETA = 10000.0
D_FF = 2816
EPS = 1e-6
N_MOD = 6
ADAM_LR = 0.001
ADAM_B1 = 0.9
ADAM_B2 = 0.999
ADAM_EPS = 1e-08
ADAM_WD = 0.01
ADAM_STEP = 10

N_DEV = 8
LANES = 128
Q_EXT = 256
VMEM_LIMIT_BYTES = 48 * 1024 * 1024
MESH = pl.DeviceIdType.MESH
NEG_BIG = -0.7 * float(jnp.finfo(jnp.float32).max)


def _params(sem):
    return pltpu.CompilerParams(dimension_semantics=sem, vmem_limit_bytes=VMEM_LIMIT_BYTES)


def _tile(n, cap):
    if n <= cap:
        return n
    best = None
    for d in range(LANES, cap + 1, LANES):
        if n % d == 0:
            best = d
    assert best is not None, (n, cap)
    return best


def _dot(a, b, dims):
    return lax.dot_general(a, b, (dims, ((), ())), preferred_element_type=F32)


NN = ((1,), (0,))
NT = ((1,), (1,))
TN = ((0,), (0,))


def _mm(name, a, b, mode="nn", out_dtype=BF16, resid=None, gate=None, rowtab=None, second=None,
        tm_cap=1024, tn_cap=1408, tk_cap=1408):
    if mode == "tn":
        kdim, m = a.shape
    else:
        m, kdim = a.shape
    n = b.shape[0] if mode == "nt" else b.shape[1]
    tm, tn, tk = _tile(m, tm_cap), _tile(n, tn_cap), _tile(kdim, tk_cap)
    nk = kdim // tk
    dims = {"nn": NN, "nt": NT, "tn": TN}[mode]
    a_spec = pl.BlockSpec((tk, tm), lambda i, j, k: (k, i)) if mode == "tn" else pl.BlockSpec((tm, tk), lambda i, j, k: (i, k))
    b_spec = pl.BlockSpec((tn, tk), lambda i, j, k: (j, k)) if mode == "nt" else pl.BlockSpec((tk, tn), lambda i, j, k: (k, j))
    o_spec = pl.BlockSpec((tm, tn), lambda i, j, k: (i, j))
    g_spec = pl.BlockSpec((1, tn), lambda i, j, k: (0, j))
    gated = resid is not None

    n_ops = 2 if second is None else 4

    def body(*refs):
        acc = refs[-1]
        k = pl.program_id(2)

        @pl.when(k == 0)
        def _():
            acc[...] = jnp.zeros_like(acc)

        prod = _dot(refs[0][...].astype(BF16), refs[1][...].astype(BF16), dims)
        if second is not None:
            prod = prod + _dot(refs[2][...].astype(BF16), refs[3][...].astype(BF16), dims)
        acc[...] += prod

        @pl.when(k == nk - 1)
        def _():
            rest = refs[n_ops:-1]
            if gated:
                r_ref, g_ref, y_ref, x_ref = rest
                y_ref[...] = acc[...].astype(y_ref.dtype)
                x_ref[...] = r_ref[...] + g_ref[...] * acc[...]
            elif rowtab is not None:
                tab = rest[0][...]
                rest[1][...] = (acc[...] * jnp.concatenate([tab] * (tn // tab.shape[1]), axis=1)).astype(out_dtype)
            else:
                rest[0][...] = acc[...].astype(out_dtype)

    ins, in_specs = [a, b], [a_spec, b_spec]
    if second is not None:
        assert second[0].shape == a.shape and second[1].shape == b.shape
        ins += list(second)
        in_specs += [a_spec, b_spec]
    if rowtab is not None:
        assert tn % rowtab.shape[1] == 0 and not gated
        ins.append(rowtab)
        in_specs.append(pl.BlockSpec((tm, rowtab.shape[1]), lambda i, j, k: (i, 0)))
    if gated:
        ins += [resid, gate]
        in_specs += [o_spec, g_spec]
        out_shape = (jax.ShapeDtypeStruct((m, n), BF16), jax.ShapeDtypeStruct((m, n), F32))
        out_specs = (o_spec, o_spec)
    else:
        out_shape = jax.ShapeDtypeStruct((m, n), out_dtype)
        out_specs = o_spec
    return pl.pallas_call(
        body, name=name, grid=(m // tm, n // tn, nk), in_specs=in_specs, out_specs=out_specs, out_shape=out_shape,
        scratch_shapes=[pltpu.VMEM((tm, tn), F32)],
        compiler_params=_params(("parallel", "parallel", "arbitrary")),
    )(*ins)


def _rowwise(name, fn, tiled, bcast, outs, sums=(), tr=512):
    tiled = [t if isinstance(t, tuple) else (t, t.shape[1], 0) for t in tiled]
    s = tiled[0][0].shape[0]
    tr = min(tr, s)
    assert s % tr == 0
    n_t, n_b, n_o = len(tiled), len(bcast), len(outs)

    def body(*refs):
        i = pl.program_id(0)
        vals = [r[...] for r in refs[:n_t + n_b]]
        o_vals, s_vals = fn(*vals)
        for r, v in zip(refs[n_t + n_b:n_t + n_b + n_o], o_vals):
            r[...] = v.astype(r.dtype)
        s_refs = refs[n_t + n_b + n_o:]

        @pl.when(i == 0)
        def _():
            for r in s_refs:
                r[...] = jnp.zeros_like(r)

        for r, v in zip(s_refs, s_vals):
            r[...] += v

    in_specs = [pl.BlockSpec((tr, n), functools.partial(lambda cb, i: (i, cb), cb)) for (_, n, cb) in tiled]
    in_specs += [pl.BlockSpec(b.shape, functools.partial(lambda nd, i: (0,) * nd, b.ndim)) for b in bcast]
    out_specs = [pl.BlockSpec((tr, n), lambda i: (i, 0)) for (n, _) in outs]
    out_specs += [pl.BlockSpec((1, n), lambda i: (0, 0)) for n in sums]
    out_shape = [jax.ShapeDtypeStruct((s, n), dt) for (n, dt) in outs]
    out_shape += [jax.ShapeDtypeStruct((1, n), F32) for n in sums]
    res = pl.pallas_call(
        body, name=name, grid=(s // tr,), in_specs=in_specs, out_specs=tuple(out_specs), out_shape=tuple(out_shape),
        compiler_params=_params(("arbitrary",)),
    )(*[t[0] for t in tiled], *bcast)
    return res


def _colsum(v):
    return jnp.sum(v, axis=0, keepdims=True)


def _rms_fwd(name, x, g, scale=None, shift=None, out_dtype=BF16, ncols=None):
    mod = scale is not None

    def fn(xv, gv, *ss):
        y = xv * lax.rsqrt(jnp.mean(xv * xv, axis=-1, keepdims=True) + EPS) * gv
        if mod:
            y = y * (1.0 + ss[0]) + ss[1]
        return (y,), ()

    n = ncols or x.shape[1]
    return _rowwise(name, fn, [(x, n, 0)], [g] + ([scale, shift] if mod else []), [(n, out_dtype)])[0]


def _rms_bwd(name, x, g, dh, scale=None, dx_in=None, ncols=None, out_dtype=F32):
    mod = scale is not None
    has_in = dx_in is not None

    def fn(*vals):
        xv, dhv = vals[0], vals[1].astype(F32)
        rest = list(vals[2:])
        dxi = rest.pop(0) if has_in else None
        gv = rest.pop(0)
        rstd = lax.rsqrt(jnp.mean(xv * xv, axis=-1, keepdims=True) + EPS)
        xhat = xv * rstd
        sums = []
        if mod:
            sc = rest.pop(0)
            dyn = dhv * (1.0 + sc)
            dshift, dscale = _colsum(dhv), _colsum(dhv * (xhat * gv))
        else:
            dyn = dhv
        dg = _colsum(dyn * xhat)
        dxhat = dyn * gv
        dx = rstd * (dxhat - xhat * jnp.mean(dxhat * xhat, axis=-1, keepdims=True))
        if has_in:
            dx = dx + dxi
        sums = [dg] + ([dshift, dscale] if mod else [])
        return (dx,), sums

    n = ncols or x.shape[1]
    tiled = [(x, n, 0), dh] + ([dx_in] if has_in else [])
    return _rowwise(name, fn, tiled, [g] + ([scale] if mod else []), [(n, out_dtype)], [n] * (3 if mod else 1))


def _gate_bwd(name, dxn, y, g):
    def fn(dv, yv, gv):
        return (gv * dv,), (_colsum(dv * yv.astype(F32)),)

    n = dxn.shape[1]
    return _rowwise(name, fn, [dxn, y], [g], [(n, BF16)], [n])


def _loss_head(name, x, g, target):
    n = x.shape[1]

    def fn(xv, tv, gv):
        rstd = lax.rsqrt(jnp.mean(xv * xv, axis=-1, keepdims=True) + EPS)
        xhat = xv * rstd
        err = xhat * gv - tv
        loss = 0.5 * jnp.sum(jnp.sum(err * err, axis=-1, keepdims=True) / n, axis=0, keepdims=True)
        dy = err / n
        dg = _colsum(dy * xhat)
        dxhat = dy * gv
        dx = rstd * (dxhat - xhat * jnp.mean(dxhat * xhat, axis=-1, keepdims=True))
        return (dx,), (dg, jnp.broadcast_to(loss, (1, LANES)))

    return _rowwise(name, fn, [x, target], [g], [(n, F32)], [n, LANES])


def _krope_fwd(name, kv_ext, tabk):
    def fn(xv, tv):
        t = xv * tv
        return (t + pltpu.roll(t, 64, 1),), ()

    return _rowwise(name, fn, [(kv_ext, LANES, 2), tabk], [], [(LANES, BF16)])[0]


def _krope_bwd(name, dkd, tabk):
    def fn(dv, tv):
        d = dv[:, :LANES]
        for h in range(1, N_HEADS):
            d = d + dv[:, h * LANES:(h + 1) * LANES]
        return ((d + pltpu.roll(d, 64, 1)) * tv,), ()

    return _rowwise(name, fn, [dkd, tabk], [], [(LANES, F32)])[0]


def _adamw(name, w, g, m, v):
    def fn(wv, gv, mv, vv):
        m2 = ADAM_B1 * mv + (1.0 - ADAM_B1) * gv
        v2 = ADAM_B2 * vv + (1.0 - ADAM_B2) * (gv * gv)
        m_hat = m2 / (1.0 - ADAM_B1 ** ADAM_STEP)
        v_hat = v2 / (1.0 - ADAM_B2 ** ADAM_STEP)
        delta = -ADAM_LR * (m_hat / (jnp.sqrt(v_hat) + ADAM_EPS) + ADAM_WD * wv)
        return (delta, m2, v2), ()

    r, c = w.shape
    tr = r
    for cand in (512, 256, 128, 64, 32, 16, 8):
        if r % cand == 0 and r > cand:
            tr = cand
            break
    return _rowwise(name, fn, [w, g, m, v], [], [(c, F32)] * 3, tr=tr)


def _sum8(name, parts):
    _, r, c = parts.shape
    tr = r
    for cand in (2048, 1024, 512, 256, 128, 64, 32, 16):
        if r % cand == 0 and r > cand and cand * c <= 256 * 1024:
            tr = cand
            break

    def body(p_ref, o_ref):
        acc = p_ref[0].astype(F32)
        for k in range(1, N_DEV):
            acc = acc + p_ref[k].astype(F32)
        o_ref[...] = acc

    return pl.pallas_call(
        body, name=name, grid=(r // tr,), in_specs=[pl.BlockSpec((N_DEV, tr, c), lambda i: (0, i, 0))],
        out_specs=pl.BlockSpec((tr, c), lambda i: (i, 0)), out_shape=jax.ShapeDtypeStruct((r, c), F32),
        compiler_params=_params(("parallel",)),
    )(parts)


def _mods_fwd(name, c_all, w, b):
    depth, d, n = w.shape

    def body(c_ref, w_ref, b_ref, o_ref):
        cv = c_ref[...]
        sc = (cv * (1.0 / (1.0 + jnp.exp(-cv)))).astype(BF16)
        o_ref[0] = _dot(sc, w_ref[0].astype(BF16), NN) + b_ref[0]

    return pl.pallas_call(
        body, name=name, grid=(depth,),
        in_specs=[pl.BlockSpec(c_all.shape, lambda l: (0, 0)), pl.BlockSpec((1, d, n), lambda l: (l, 0, 0)),
                  pl.BlockSpec((1, 1, n), lambda l: (l, 0, 0))],
        out_specs=pl.BlockSpec((1, c_all.shape[0], n), lambda l: (l, 0, 0)),
        out_shape=jax.ShapeDtypeStruct((depth, c_all.shape[0], n), F32),
        compiler_params=_params(("parallel",)),
    )(c_all, w, b.reshape(depth, 1, n))


def _mods_bwd(name, c_all, dm):
    depth, rows, n = dm.shape
    d = c_all.shape[1]

    def body(c_ref, dm_ref, o_ref):
        cv = c_ref[...]
        sc = (cv * (1.0 / (1.0 + jnp.exp(-cv)))).astype(BF16)
        o_ref[0] = _dot(sc, dm_ref[0].astype(BF16), TN)

    return pl.pallas_call(
        body, name=name, grid=(depth,),
        in_specs=[pl.BlockSpec(c_all.shape, lambda l: (0, 0)), pl.BlockSpec((1, rows, n), lambda l: (l, 0, 0))],
        out_specs=pl.BlockSpec((1, d, n), lambda l: (l, 0, 0)),
        out_shape=jax.ShapeDtypeStruct((depth, d, n), F32),
        compiler_params=_params(("parallel",)),
    )(c_all, dm)


POOL_TILE = 256


def _split_dot(band, val):
    hi = val.astype(BF16)
    lo = (val - hi.astype(F32)).astype(BF16)
    return _dot(band, hi, NN) + _dot(band, lo, NN)


def _pool_fwd(name, h1, x, pw, pb, ps, g1):
    s, d = h1.shape
    t = POOL_TILE

    def body(hc_ref, hp_ref, x_ref, pw_ref, pb_ref, ps_ref, g_ref, xo_ref, zb_ref, pooled_ref):
        i = pl.program_id(0)
        r = lax.broadcasted_iota(jnp.int32, (t, t), 0)
        j = lax.broadcasted_iota(jnp.int32, (t, t), 1)
        pos = (i * t + lax.broadcasted_iota(jnp.int32, (t, 1), 0) + 1).astype(F32)
        has_prev = (i > 0).astype(F32)
        for grp, w in enumerate(POOL_WINDOWS):
            cs = slice(grp * POOL_GROUP, (grp + 1) * POOL_GROUP)
            hc = hc_ref[:, cs]
            band_cur = ((r - j >= 0) & (r - j < w)).astype(BF16)
            band_prev = (r + t - j < w).astype(BF16)
            ssum = _split_dot(band_cur, hc) + has_prev * _split_dot(band_prev, hp_ref[:, cs])
            pooled = (ssum / jnp.minimum(pos, float(w)) - hc).astype(BF16)
            zb = _dot(pooled, pw_ref[grp], NN) + pb_ref[:, cs]
            xo_ref[:, cs] = x_ref[:, cs] + g_ref[:, cs] * (zb * ps_ref[:, cs])
            zb_ref[:, cs] = zb
            pooled_ref[:, cs] = pooled

    row = pl.BlockSpec((t, d), lambda i: (i, 0))
    vec = pl.BlockSpec((1, d), lambda i: (0, 0))
    return pl.pallas_call(
        body, name=name, grid=(s // t,),
        in_specs=[row, pl.BlockSpec((t, d), lambda i: (jnp.maximum(i - 1, 0), 0)), row,
                  pl.BlockSpec(pw.shape, lambda i: (0, 0, 0)), vec, vec, vec],
        out_specs=(row, row, row),
        out_shape=(jax.ShapeDtypeStruct((s, d), F32), jax.ShapeDtypeStruct((s, d), F32), jax.ShapeDtypeStruct((s, d), BF16)),
        compiler_params=_params(("parallel",)),
    )(h1, h1, x, pw, pb, ps, g1)


def _pool_bwd(name, dxn, zb, pooled, pw, ps, g1):
    s, d = dxn.shape
    t = POOL_TILE
    nt = s // t

    def body(dc_ref, dn_ref, zb_ref, pooled_ref, pw_ref, ps_ref, g_ref, dh_ref, dpw_ref, dpb_ref, dps_ref, dg_ref):
        i = pl.program_id(0)

        @pl.when(i == 0)
        def _():
            dpw_ref[...] = jnp.zeros_like(dpw_ref)
            dpb_ref[...] = jnp.zeros_like(dpb_ref)
            dps_ref[...] = jnp.zeros_like(dps_ref)
            dg_ref[...] = jnp.zeros_like(dg_ref)

        jj = lax.broadcasted_iota(jnp.int32, (t, t), 0)
        rr = lax.broadcasted_iota(jnp.int32, (t, t), 1)
        pos = (i * t + lax.broadcasted_iota(jnp.int32, (t, 1), 0) + 1).astype(F32)
        has_next = (i < nt - 1).astype(F32)
        for grp, w in enumerate(POOL_WINDOWS):
            cs = slice(grp * POOL_GROUP, (grp + 1) * POOL_GROUP)
            gv, psv, zbv, dxc = g_ref[:, cs], ps_ref[:, cs], zb_ref[:, cs], dc_ref[:, cs]
            dg_ref[:, cs] += _colsum(dxc * (zbv * psv))
            dy = gv * dxc
            dps_ref[:, cs] += _colsum(dy * zbv)
            dz = dy * psv
            dpb_ref[:, cs] += _colsum(dz)
            dzb = dz.astype(BF16)
            dpw_ref[grp] += _dot(pooled_ref[:, cs], dzb, TN)
            dp = _dot(dzb, pw_ref[grp], NT)
            dzn = (gv * dn_ref[:, cs] * psv).astype(BF16)
            dpn = _dot(dzn, pw_ref[grp], NT) * (has_next / float(w))
            band_cur = ((rr - jj >= 0) & (rr - jj < w)).astype(BF16)
            band_next = (rr + t - jj < w).astype(BF16)
            dh_ref[:, cs] = _split_dot(band_cur, dp / jnp.minimum(pos, float(w))) + _split_dot(band_next, dpn) - dp

    row = pl.BlockSpec((t, d), lambda i: (i, 0))
    vec = pl.BlockSpec((1, d), lambda i: (0, 0))
    wspec = pl.BlockSpec(pw.shape, lambda i: (0, 0, 0))
    return pl.pallas_call(
        body, name=name, grid=(nt,),
        in_specs=[row, pl.BlockSpec((t, d), lambda i: (jnp.minimum(i + 1, nt - 1), 0)), row, row, wspec, vec, vec],
        out_specs=(row, wspec, vec, vec, vec),
        out_shape=(jax.ShapeDtypeStruct((s, d), F32), jax.ShapeDtypeStruct(pw.shape, F32),
                   jax.ShapeDtypeStruct((1, d), F32), jax.ShapeDtypeStruct((1, d), F32), jax.ShapeDtypeStruct((1, d), F32)),
        compiler_params=_params(("arbitrary",)),
    )(dxn, dxn, zb, pooled, pw, ps, g1)


GLU_TILE = 512
HALO = 16
INV_SQRT2 = 0.7071067811865476
INV_SQRT_2PI = 0.3989422804014327


def _up_glu_fwd(name, h2, wa, wv, cw, cb):
    s, d = h2.shape
    f = wa.shape[1]
    tm, tn = _tile(s, 1024), _tile(f, 1408)

    def body(h_ref, hh_ref, wa_ref, wv_ref, cw_ref, cb_ref, ua_ref, gl_ref, gpv_ref, ge_ref):
        i = pl.program_id(1)
        has_prev = (i > 0).astype(F32)
        a = _dot(h_ref[...], wa_ref[...], NN).astype(BF16)
        v = _dot(h_ref[...], wv_ref[...], NN)
        above = (_dot(hh_ref[...], wa_ref[...], NN) * has_prev).astype(BF16)
        ua_ref[...] = a
        ext = jnp.concatenate([above.astype(F32), a.astype(F32)], axis=0)
        e1 = pltpu.roll(ext, 1, 0)[HALO:]
        e2 = pltpu.roll(ext, 2, 0)[HALO:]
        pre = e2 * cw_ref[0:1, :] + e1 * cw_ref[1:2, :] + ext[HALO:] * cw_ref[2:3, :] + cb_ref[...]
        cdf = 0.5 * (1.0 + lax.erf(pre * INV_SQRT2))
        ge = pre * cdf
        gl_ref[...] = (ge * v).astype(gl_ref.dtype)
        gpv_ref[...] = ((cdf + pre * (INV_SQRT_2PI * jnp.exp(-0.5 * pre * pre))) * v).astype(gpv_ref.dtype)
        ge_ref[...] = ge.astype(ge_ref.dtype)

    blk = pl.BlockSpec((tm, tn), lambda j, i: (i, j))
    wspec = pl.BlockSpec((d, tn), lambda j, i: (0, j))
    return pl.pallas_call(
        body, name=name, grid=(f // tn, s // tm),
        in_specs=[pl.BlockSpec((tm, d), lambda j, i: (i, 0)), pl.BlockSpec((HALO, d), lambda j, i: (jnp.maximum(i * (tm // HALO) - 1, 0), 0)),
                  wspec, wspec, pl.BlockSpec((3, tn), lambda j, i: (0, j)), pl.BlockSpec((1, tn), lambda j, i: (0, j))],
        out_specs=(blk, blk, blk, blk), out_shape=tuple(jax.ShapeDtypeStruct((s, f), BF16) for _ in range(4)),
        compiler_params=_params(("parallel", "parallel")),
    )(h2, h2, wa, wv, cw, cb)


def _down_glu_bwd(name, dy2, wd, ua, gpv, ge, cw):
    s, f = ua.shape
    d = dy2.shape[1]
    t, tf = min(GLU_TILE, s), _tile(f, 1408)
    nt = s // t
    te = t + HALO

    def body(dy_ref, dyn_ref, wd_ref, a_ref, ah_ref, g_ref, gn_ref, ge_ref, cw_ref, da_ref, dv_ref, dcw_ref, dcb_ref):
        i = pl.program_id(1)

        @pl.when(i == 0)
        def _():
            dcw_ref[...] = jnp.zeros_like(dcw_ref)
            dcb_ref[...] = jnp.zeros_like(dcb_ref)

        has_prev = (i > 0).astype(F32)
        has_next = (i < nt - 1).astype(F32)
        wdv = wd_ref[...]
        dgl = _dot(dy_ref[...], wdv, NT)
        dgl_below = _dot(dyn_ref[...], wdv, NT) * has_next
        dpre = jnp.concatenate([dgl * g_ref[...].astype(F32), dgl_below * gn_ref[...].astype(F32)], axis=0)
        c0, c1, c2 = cw_ref[0:1, :], cw_ref[1:2, :], cw_ref[2:3, :]
        up1 = pltpu.roll(dpre, te - 1, 0)
        up2 = pltpu.roll(dpre, te - 2, 0)
        da_ref[...] = (dpre * c2 + up1 * c1 + up2 * c0)[:t].astype(da_ref.dtype)
        dv_ref[...] = (dgl * ge_ref[...].astype(F32)).astype(dv_ref.dtype)
        ext = jnp.concatenate([ah_ref[...].astype(F32) * has_prev, a_ref[...].astype(F32)], axis=0)
        dpt = dpre[:t]
        dcb_ref[...] += _colsum(dpt)
        dcw_ref[0:1, :] += _colsum(pltpu.roll(ext, 2, 0)[HALO:] * dpt)
        dcw_ref[1:2, :] += _colsum(pltpu.roll(ext, 1, 0)[HALO:] * dpt)
        dcw_ref[2:3, :] += _colsum(ext[HALO:] * dpt)

    blk = pl.BlockSpec((t, tf), lambda j, i: (i, j))
    prev = pl.BlockSpec((HALO, tf), lambda j, i: (jnp.maximum(i * (t // HALO) - 1, 0), j))
    below = lambda i: jnp.minimum((i + 1) * (t // HALO), s // HALO - 1)
    w3 = pl.BlockSpec((3, tf), lambda j, i: (0, j))
    w1 = pl.BlockSpec((1, tf), lambda j, i: (0, j))
    return pl.pallas_call(
        body, name=name, grid=(f // tf, nt),
        in_specs=[pl.BlockSpec((t, d), lambda j, i: (i, 0)), pl.BlockSpec((HALO, d), lambda j, i: (below(i), 0)),
                  pl.BlockSpec((tf, d), lambda j, i: (j, 0)), blk, prev, blk, pl.BlockSpec((HALO, tf), lambda j, i: (below(i), j)), blk, w3],
        out_specs=(blk, blk, w3, w1),
        out_shape=(jax.ShapeDtypeStruct((s, f), BF16), jax.ShapeDtypeStruct((s, f), BF16),
                   jax.ShapeDtypeStruct((3, f), F32), jax.ShapeDtypeStruct((1, f), F32)),
        compiler_params=_params(("parallel", "arbitrary")),
    )(dy2, dy2, wd, ua, ua, gpv, gpv, ge, cw)


ATT_TILE = 512
ATT_ROWS = 256
ATT_HEADS = 4
LOG2E = 1.4426950408889634
LN2 = 0.6931471805599453


def _head_blocks_t(a, width):
    s = a.shape[0]
    t = min(ATT_TILE, s)
    return a.reshape(s // t, t, N_HEADS, width).transpose(2, 0, 3, 1)


def _causal_mask(sv, q0, k0):
    row = q0 + lax.broadcasted_iota(jnp.int32, sv.shape, 0)
    col = k0 + lax.broadcasted_iota(jnp.int32, sv.shape, 1)
    return jnp.where(col <= row, sv, NEG_BIG)


def _attn_fwd(name, q_rot, kt4, v_ext):
    s = q_rot.shape[0]
    t = min(ATT_TILE, s)
    nq = s // t
    rq = min(ATT_ROWS, t)
    nh = ATT_HEADS

    def body(q_ref, kt_ref, v_ref, o_ref, row_ref, acc_ref, m_ref):
        qi = pl.program_id(1)
        acc_ref[...] = jnp.zeros_like(acc_ref)
        m_ref[...] = jnp.full_like(m_ref, NEG_BIG)

        def step(j, masked):
            for hh in range(nh):
                cols = slice(hh * Q_EXT, (hh + 1) * Q_EXT)
                v_blk = v_ref[pl.ds(pl.multiple_of(j * t, t), t), cols]
                for r in range(t // rq):
                    rs = pl.ds(r * rq, rq)
                    sv = _dot(q_ref[rs, cols], kt_ref[hh, j], NN)
                    if masked:
                        sv = _causal_mask(sv, r * rq, 0)
                    m_prev = m_ref[hh, rs, :]
                    m_new = jnp.maximum(m_prev, jnp.max(sv, axis=-1, keepdims=True))
                    p = jnp.exp2(sv - m_new).astype(BF16)
                    acc_ref[hh, rs, :] = jnp.exp2(m_prev - m_new) * acc_ref[hh, rs, :] + _dot(p, v_blk, NN)
                    m_ref[hh, rs, :] = m_new

        def full_step(j, carry):
            step(j, False)
            return carry

        lax.fori_loop(0, qi, full_step, 0)
        step(qi, True)
        for hh in range(nh):
            l = acc_ref[hh, :, V_HEAD:V_HEAD + 1]
            o_ref[:, hh * V_HEAD:(hh + 1) * V_HEAD] = (acc_ref[hh, :, :V_HEAD] / l).astype(o_ref.dtype)
            lse = jnp.broadcast_to(m_ref[hh] + jnp.log(l) * LOG2E, (t, LANES))
            row_ref[hh, 0] = jnp.transpose(lse)[0:8, :]

    return pl.pallas_call(
        body, name=name, grid=(N_HEADS // nh, nq),
        in_specs=[pl.BlockSpec((t, nh * Q_EXT), lambda h, i: (i, h)), pl.BlockSpec((nh, nq, Q_EXT, t), lambda h, i: (h, 0, 0, 0)),
                  pl.BlockSpec((s, nh * Q_EXT), lambda h, i: (0, h))],
        out_specs=(pl.BlockSpec((t, nh * V_HEAD), lambda h, i: (i, h)), pl.BlockSpec((nh, 1, 8, t), lambda h, i: (h, i, 0, 0))),
        out_shape=(jax.ShapeDtypeStruct((s, N_HEADS * V_HEAD), BF16), jax.ShapeDtypeStruct((N_HEADS, nq, 8, t), F32)),
        scratch_shapes=[pltpu.VMEM((nh, t, Q_EXT), F32), pltpu.VMEM((nh, t, 1), F32)],
        compiler_params=_params(("parallel", "parallel")),
    )(q_rot, kt4, v_ext)


def _attn_delta(name, o, do):
    s = o.shape[0]
    t = min(ATT_TILE, s)

    def body(o_ref, do_ref, delta_ref):
        prod = do_ref[...].astype(F32) * o_ref[...].astype(F32)
        for h in range(N_HEADS):
            delta = jnp.sum(prod[:, h * V_HEAD:(h + 1) * V_HEAD], axis=-1, keepdims=True)
            delta_ref[h, 0] = jnp.transpose(jnp.broadcast_to(delta, (t, LANES)))[0:8, :]

    rows = pl.BlockSpec((t, N_HEADS * V_HEAD), lambda i: (i, 0))
    return pl.pallas_call(
        body, name=name, grid=(s // t,), in_specs=[rows, rows],
        out_specs=pl.BlockSpec((N_HEADS, 1, 8, t), lambda i: (0, i, 0, 0)),
        out_shape=jax.ShapeDtypeStruct((N_HEADS, s // t, 8, t), F32),
        compiler_params=_params(("parallel",)),
    )(o, do)


def _attn_bwd(name, kfull, v, qt4, q_rot, dot4, do, lse_row, delta_row, tabq, acc_in=None):
    s = kfull.shape[0]
    t = min(ATT_TILE, s)
    nq = s // t
    has_in = acc_in is not None

    def body(*refs):
        k_ref, v_ref, qt_ref, q_ref, dot_ref, do_ref, lse_ref, delta_ref, tab_ref = refs[:9]
        dq_ref, dkn_ref, dkd_ref, dv_ref, dq_acc_ref, acck_ref, accv_ref = refs[-7:]
        kj = pl.program_id(1)

        @pl.when(kj == 0)
        def _():
            dq_acc_ref[...] = jnp.zeros_like(dq_acc_ref)

        k_blk, v_blk = k_ref[...], v_ref[...]
        acck_ref[...] = jnp.zeros_like(acck_ref)
        accv_ref[...] = jnp.zeros_like(accv_ref)

        def step(i, masked):
            qs = pl.ds(pl.multiple_of(i * t, t), t)
            st = _dot(k_blk, qt_ref[0, i], NN)
            if masked:
                krow = lax.broadcasted_iota(jnp.int32, st.shape, 0)
                qcol = lax.broadcasted_iota(jnp.int32, st.shape, 1)
                st = jnp.where(krow <= qcol, st, NEG_BIG)
            pt = jnp.exp2(st - lse_ref[0, i, 0:1, :])
            accv_ref[...] += _dot(pt.astype(BF16), do_ref[qs, :], NN)
            dpt = _dot(v_blk, dot_ref[0, i], NN)
            dst = (pt * (dpt - delta_ref[0, i, 0:1, :])).astype(BF16)
            acck_ref[...] += _dot(dst, q_ref[qs, :], NN)
            dq_acc_ref[qs, :] += _dot(dst, k_blk, TN)

        def full_step(i, carry):
            step(i, False)
            return carry

        step(kj, True)
        lax.fori_loop(kj + 1, nq, full_step, 0)
        dk = acck_ref[...] * LN2
        if has_in:
            dkn_ref[...] = dk[:, :QK_NOPE] + refs[9][...]
            dkd_ref[...] = dk[:, QK_NOPE:] + refs[10][...]
            dv_ref[...] = accv_ref[...] + refs[11][...]
        else:
            dkn_ref[...] = dk[:, :QK_NOPE]
            dkd_ref[...] = dk[:, QK_NOPE:]
            dv_ref[...] = accv_ref[...]

        @pl.when(kj == nq - 1)
        def _():
            dq_ref[...] = (dq_acc_ref[...] * (tab_ref[...] * LN2)).astype(dq_ref.dtype)

    kblk = pl.BlockSpec((t, LANES), lambda h, j: (j, h))
    col = pl.BlockSpec((s, LANES), lambda h, j: (0, h))
    q_all = pl.BlockSpec((s, Q_EXT), lambda h, j: (0, h))
    stat = pl.BlockSpec((1, nq, 8, t), lambda h, j: (h, 0, 0, 0))
    ins = [kfull, v, qt4, q_rot, dot4, do, lse_row, delta_row, tabq]
    in_specs = [pl.BlockSpec((t, Q_EXT), lambda h, j: (j, h)), kblk, pl.BlockSpec((1, nq, Q_EXT, t), lambda h, j: (h, 0, 0, 0)),
                q_all, pl.BlockSpec((1, nq, V_HEAD, t), lambda h, j: (h, 0, 0, 0)), col, stat, stat,
                pl.BlockSpec((s, Q_EXT), lambda h, j: (0, 0))]
    if has_in:
        ins += list(acc_in)
        in_specs += [kblk, kblk, kblk]
    wide = jax.ShapeDtypeStruct((s, N_HEADS * LANES), F32)
    return pl.pallas_call(
        body, name=name, grid=(N_HEADS, nq), in_specs=in_specs, out_specs=(q_all, kblk, kblk, kblk),
        out_shape=(jax.ShapeDtypeStruct((s, N_HEADS * Q_EXT), BF16), wide, wide, wide),
        scratch_shapes=[pltpu.VMEM((s, Q_EXT), F32), pltpu.VMEM((t, Q_EXT), F32), pltpu.VMEM((t, LANES), F32)],
        compiler_params=_params(("parallel", "arbitrary")),
    )(*ins)


def _swap_halves(w):
    half = w.shape[-1] // 2
    return jnp.concatenate([-w[..., half:], w[..., :half]], axis=-1)


def _unswap_halves(g):
    half = g.shape[-1] // 2
    return jnp.concatenate([g[..., half:], -g[..., :half]], axis=-1)


def _extend_w_dkv(w):
    return jnp.concatenate([w, _swap_halves(w[:, KV_RANK:])], axis=-1)


def _fold_w_dkv_grad(g):
    rope = g[:, KV_RANK:KV_RANK + QK_ROPE] + _unswap_halves(g[:, KV_RANK + QK_ROPE:])
    return jnp.concatenate([g[:, :KV_RANK], rope], axis=-1)


def _rope_tables(positions):
    inv = 1.0 / (ROPE_THETA ** (jnp.arange(0, QK_ROPE, 2, dtype=F32) / QK_ROPE))
    ang = positions.astype(F32)[:, None] * inv
    cos, sin = jnp.cos(ang), jnp.sin(ang)
    tabk = jnp.concatenate([cos, cos, sin, sin], axis=-1)
    scale = QK_HEAD ** -0.5 * LOG2E
    tabq = jnp.concatenate([jnp.full((positions.shape[0], QK_NOPE), scale, F32), tabk * scale], axis=-1)
    return tabq, tabk


def _forward_backward(x, target, mods, tabq, tabk, final_g, fetch, push):
    row = lambda vec: vec.reshape(1, -1)
    mod = [[row(mods[l, k * D_MODEL:(k + 1) * D_MODEL]) for k in range(N_MOD)] for l in range(DEPTH)]
    saved, weights = [], []
    kv = None
    for l in range(DEPTH):
        w, tok = fetch(l, x)
        sh1, sc1, g1, sh2, sc2, g2 = mod[l]
        sh1 = sh1 + tok
        if l == N_A_LAYERS:
            kvn = _rms_fwd("kvin_fwd", x, row(w["kv_in_g"]))
            kv_ext = _mm("dkv_fwd", kvn, w["w_dkv_ext"], out_dtype=F32)
            ckv = _rms_fwd("ckv_fwd", kv_ext, row(w["ckv_norm_g"]), ncols=KV_RANK)
            kd = _krope_fwd("krope_fwd", kv_ext, tabk)
            kn, v = _mm("uk_fwd", ckv, w["w_uk"]), _mm("uv_fwd", ckv, w["w_uv"])
            heads = lambda a: [a[:, h * LANES:(h + 1) * LANES] for h in range(N_HEADS)]
            kfull = jnp.concatenate([part for kh in heads(kn) for part in (kh, kd)], axis=-1)
            v_ext = jnp.concatenate([part for vh in heads(v) for part in (vh, jnp.ones_like(vh))], axis=-1)
            kv = dict(x=x, kvn=kvn, kv_ext=kv_ext, ckv=ckv, v=v, kfull=kfull, v_ext=v_ext,
                      kt4=_head_blocks_t(kfull, Q_EXT))
        x_in = x
        if l < N_A_LAYERS:
            h1 = _rms_fwd(f"norm1_fwd_{l}", x, row(w["norm1_g"]), sc1, sh1, out_dtype=F32)
            x_mid, zb, pooled = _pool_fwd(f"pool_fwd_{l}", h1, x, w["pool_w"], row(w["pool_b"]), row(w["pool_scale"]), g1)
            mix = (zb, pooled)
        else:
            h1 = _rms_fwd(f"norm1_fwd_{l}", x, row(w["norm1_g"]), sc1, sh1)
            cq_pre = _mm(f"dq_fwd_{l}", h1, w["w_dq"], out_dtype=F32)
            cq = _rms_fwd(f"qnorm_fwd_{l}", cq_pre, row(w["q_norm_g"]))
            q_rot = _mm(f"uq_fwd_{l}", cq, w["w_uq_ext"], rowtab=tabq)
            o, lse_row = _attn_fwd(f"attn_fwd_{l}", q_rot, kv["kt4"], kv["v_ext"])
            y, x_mid = _mm(f"wo_fwd_{l}", o, w["w_o"], resid=x, gate=g1)
            mix = (h1, cq_pre, cq, q_rot, o, lse_row, y)
        h2 = _rms_fwd(f"norm2_fwd_{l}", x_mid, row(w["norm2_g"]), sc2, sh2)
        w_up_a, w_up_v = w["w_up"](h2)
        ua, gl, gpv, ge = _up_glu_fwd(f"up_glu_fwd_{l}", h2, w_up_a, w_up_v, w["conv_w"], row(w["conv_b"]))
        w_down = w["w_down"](gl)
        y2, x = _mm(f"down_fwd_{l}", gl, w_down, resid=x_mid, gate=g2)
        saved.append((x_in, x_mid, h2, ua, gpv, ge, gl, y2, mix))
        weights.append(dict(w, w_up_a=w_up_a, w_up_v=w_up_v, w_down=w_down))

    dx, dfinal_g, loss = _loss_head("loss_head", x, row(final_g), target)
    g = {"final_g": dfinal_g.reshape(-1)}
    per_layer = {k: [None] * DEPTH for k in ("norm1_g", "norm2_g", "conv_w", "conv_b")}
    per_a = {k: [None] * N_A_LAYERS for k in ("pool_b", "pool_scale")}
    per_b = {k: [None] * N_B_LAYERS for k in ("q_norm_g",)}
    dmods = [None] * DEPTH
    dkv = None
    tok = 0.0
    for l in reversed(range(DEPTH)):
        w, big = weights[l], {}
        sh1, sc1, g1, sh2, sc2, g2 = mod[l]
        g2 = g2 + tok
        x_in, x_mid, h2, ua, gpv, ge, gl, y2, mix = saved[l]
        dy2, dg2 = _gate_bwd(f"gate2_bwd_{l}", dx, y2, g2)
        tok = push(l, "down", dict(w_down=_mm(f"down_wgrad_{l}", gl, dy2, mode="tn", tm_cap=1408)), None)
        da, dv_, dcw, dcb = _down_glu_bwd(f"down_glu_bwd_{l}", dy2, w["w_down"], ua, gpv, ge, w["conv_w"] + tok)
        dh2 = _mm(f"up_bwd_{l}", da, w["w_up_a"], mode="nt", out_dtype=F32, second=(dv_, w["w_up_v"]))
        tok = push(l, "up", dict(w_up_a=_mm(f"up_a_wgrad_{l}", h2, da, mode="tn"), w_up_v=_mm(f"up_v_wgrad_{l}", h2, dv_, mode="tn")), None)
        per_layer["conv_w"][l], per_layer["conv_b"][l] = dcw, dcb.reshape(-1)
        dx_mid, dn2, dsh2, dsc2 = _rms_bwd(f"norm2_bwd_{l}", x_mid, row(w["norm2_g"]), dh2, sc2 + tok, dx_in=dx)
        per_layer["norm2_g"][l] = dn2.reshape(-1)
        if l < N_A_LAYERS:
            zb, pooled = mix
            dh1, dpw, dpb, dps, dg1 = _pool_bwd(f"pool_bwd_{l}", dx_mid, zb, pooled, w["pool_w"], row(w["pool_scale"]), g1)
            big["pool_w"] = dpw
            per_a["pool_b"][l], per_a["pool_scale"][l] = dpb.reshape(-1), dps.reshape(-1)
        else:
            j = l - N_A_LAYERS
            h1, cq_pre, cq, q_rot, o, lse_row, y = mix
            dy, dg1 = _gate_bwd(f"gate1_bwd_{l}", dx_mid, y, g1)
            do = _mm(f"wo_bwd_{l}", dy, w["w_o"], mode="nt")
            big["w_o"] = _mm(f"wo_wgrad_{l}", o, dy, mode="tn")
            delta_row = _attn_delta(f"attn_delta_{l}", o, do)
            dq_ext, *dkv = _attn_bwd(f"attn_bwd_{l}", kv["kfull"], kv["v"], _head_blocks_t(q_rot, Q_EXT), q_rot, _head_blocks_t(do, V_HEAD), do,
                                     lse_row, delta_row, tabq, acc_in=dkv)
            dcq = _mm(f"uq_bwd_{l}", dq_ext, w["w_uq_ext"], mode="nt", out_dtype=F32)
            big["w_uq_ext"] = _mm(f"uq_wgrad_{l}", cq, dq_ext, mode="tn", out_dtype=F32)
            dcq_pre, dqn = _rms_bwd(f"qnorm_bwd_{l}", cq_pre, row(w["q_norm_g"]), dcq, out_dtype=BF16)
            per_b["q_norm_g"][j] = dqn.reshape(-1)
            dh1 = _mm(f"dq_bwd_{l}", dcq_pre, w["w_dq"], mode="nt")
            big["w_dq"] = _mm(f"dq_wgrad_{l}", h1, dcq_pre, mode="tn")
        dx, dn1, dsh1, dsc1 = _rms_bwd(f"norm1_bwd_{l}", x_in, row(w["norm1_g"]), dh1, sc1, dx_in=dx_mid)
        per_layer["norm1_g"][l] = dn1.reshape(-1)
        dmods[l] = jnp.concatenate([dsh1, dsc1, dg1, dsh2, dsc2, dg2], axis=-1).reshape(-1)
        if l == N_A_LAYERS:
            dkn, dkd, dv = dkv
            dckv = _mm("ukv_bwd", dkn, w["w_uk"], mode="nt", out_dtype=F32, second=(dv, w["w_uv"]))
            big["w_uk"] = _mm("uk_wgrad", kv["ckv"], dkn, mode="tn")
            big["w_uv"] = _mm("uv_wgrad", kv["ckv"], dv, mode="tn")
            dkr = _krope_bwd("krope_bwd", dkd, tabk)
            dc, dckv_g = _rms_bwd("ckv_bwd", kv["kv_ext"], row(w["ckv_norm_g"]), dckv, ncols=KV_RANK, out_dtype=BF16)
            dkv_ext = jnp.concatenate([dc, dkr.astype(BF16)], axis=-1)
            dkvn = _mm("dkv_bwd", dkv_ext, w["w_dkv_ext"], mode="nt")
            big["w_dkv_ext"] = _mm("dkv_wgrad", kv["kvn"], dkv_ext, mode="tn", out_dtype=F32)
            dx, dkv_in_g = _rms_bwd("kvin_bwd", kv["x"], row(w["kv_in_g"]), dkvn, dx_in=dx)
            g["ckv_norm_g"], g["kv_in_g"] = dckv_g.reshape(-1), dkv_in_g.reshape(-1)
        tok = push(l, "mix", big, dx)
    for group in (per_layer, per_a, per_b):
        for k, vals in group.items():
            g[k] = jnp.stack(vals)
    return loss, dx, g, jnp.stack(dmods)


def _my_index():
    return 4 * lax.axis_index("x") + 2 * lax.axis_index("y") + lax.axis_index("c")


def _peer(k):
    x, y, c = lax.axis_index("x"), lax.axis_index("y"), lax.axis_index("c")
    return (1 - x if k & 4 else x, 1 - y if k & 2 else y, 1 - c if k & 1 else c)


def _index_of(pos):
    return 4 * pos[0] + 2 * pos[1] + pos[2]


def _exchange_many(name, arrays, scatter):
    n = len(arrays)
    blocks = [tuple(a.shape[1:]) if scatter else tuple(a.shape) for a in arrays]

    def body(*refs):
        x_refs, o_refs = refs[:n], refs[n:2 * n]
        send_sems, recv_sems, local_sems = refs[2 * n:]
        me = _my_index()
        started = []
        for a in range(n):
            mine = pltpu.make_async_copy(x_refs[a].at[me] if scatter else x_refs[a], o_refs[a].at[me], local_sems.at[a])
            mine.start()
            started.append(mine)
        sends = []
        for k in range(1, N_DEV):
            peer = _peer(k)
            for a in range(n):
                cp = pltpu.make_async_remote_copy(
                    src_ref=x_refs[a].at[_index_of(peer)] if scatter else x_refs[a], dst_ref=o_refs[a].at[me],
                    send_sem=send_sems.at[a, k - 1], recv_sem=recv_sems.at[a, k - 1], device_id=peer, device_id_type=MESH)
                cp.start()
                sends.append(cp)
        for k in range(1, N_DEV):
            peer = _peer(k)
            for a in range(n):
                pltpu.make_async_remote_copy(
                    src_ref=x_refs[a].at[me] if scatter else x_refs[a], dst_ref=o_refs[a].at[_index_of(peer)],
                    send_sem=send_sems.at[a, k - 1], recv_sem=recv_sems.at[a, k - 1], device_id=peer, device_id_type=MESH).wait_recv()
        for cp in sends:
            cp.wait_send()
        for mine in started:
            mine.wait()

    return pl.pallas_call(
        body, name=name, out_shape=tuple(jax.ShapeDtypeStruct((N_DEV,) + blk, a.dtype) for blk, a in zip(blocks, arrays)),
        in_specs=[pl.BlockSpec(memory_space=pl.ANY)] * n, out_specs=tuple([pl.BlockSpec(memory_space=pl.ANY)] * n),
        scratch_shapes=[pltpu.SemaphoreType.DMA((n, N_DEV - 1)), pltpu.SemaphoreType.DMA((n, N_DEV - 1)), pltpu.SemaphoreType.DMA((n,))],
    )(*arrays)


def _exchange(name, x, scatter):
    return _exchange_many(name, [x], scatter)[0]


HBM_SPEC = pl.BlockSpec(memory_space=pltpu.HBM)
SEM_SPEC = pl.BlockSpec(memory_space=pltpu.SEMAPHORE)
DATAFLOW = pltpu.SideEffectType.DATAFLOW_SIDE_EFFECTING


def _remote_copies(x_refs, land_refs, send_sems, recv_sems, scatter, numbers=None):
    me = _my_index()
    numbers = list(range(len(x_refs))) if numbers is None else numbers
    out, inc = [], []
    for a in range(len(x_refs)):
        for k in range(1, N_DEV):
            peer = _peer(k)
            pair = numbers[a] * (N_DEV - 1) + k - 1
            sems = dict(send_sem=send_sems.at[pair], recv_sem=recv_sems.at[pair], device_id=peer, device_id_type=MESH)
            out.append(pltpu.make_async_remote_copy(
                src_ref=x_refs[a].at[_index_of(peer)] if scatter else x_refs[a], dst_ref=land_refs[a].at[me], **sems))
            inc.append(pltpu.make_async_remote_copy(
                src_ref=x_refs[a].at[me] if scatter else x_refs[a], dst_ref=land_refs[a].at[_index_of(peer)], **sems))
    return out, inc


def _exchange_start(name, arrays, scatter):
    n = len(arrays)
    blocks = [tuple(a.shape[1:]) if scatter else tuple(a.shape) for a in arrays]

    def body(*refs):
        x_refs, land_refs = refs[:n], refs[n:2 * n]
        send_sems, recv_sems = refs[2 * n], refs[2 * n + 1]
        for cp in _remote_copies(x_refs, land_refs, send_sems, recv_sems, scatter)[0]:
            cp.start()
        refs[-1][...] = jnp.zeros_like(refs[-1])

    sem_type = pltpu.SemaphoreType.DMA((n * (N_DEV - 1),))
    lands =[pltpu.with_memory_space_constraint(lax.empty((N_DEV,) + blk, a.dtype), pltpu.HBM) for blk, a in zip(blocks, arrays)]
    srcs = [pltpu.with_memory_space_constraint(a, pltpu.HBM) for a in arrays]
    res = pl.pallas_call(
        body, name=name,
        out_shape=(sem_type, sem_type, *[pltpu.HBM(a.shape, a.dtype) for a in srcs + lands], jax.ShapeDtypeStruct((8, LANES), F32)),
        in_specs=[HBM_SPEC] * (2 * n), out_specs=(SEM_SPEC, SEM_SPEC, *[HBM_SPEC] * (2 * n), pl.BlockSpec(memory_space=pltpu.VMEM)),
        input_output_aliases={i: 2 + i for i in range(2 * n)},
        compiler_params=pltpu.CompilerParams(has_side_effects=DATAFLOW),
    )(*srcs, *lands)
    return (res[0], res[1], list(res[2:2 + n]), list(res[2 + n:2 + 2 * n])), res[-1]


def _exchange_wait(name, handles, after, scatter, which=None):
    send_sems, recv_sems, srcs, lands = handles
    which = list(range(len(srcs))) if which is None else list(which)
    srcs, lands = [srcs[a] for a in which], [lands[a] for a in which]
    n = len(srcs)

    def body(*refs):
        x_refs, land_refs = refs[:n], refs[n:2 * n]
        out, inc = _remote_copies(x_refs, land_refs, refs[2 * n], refs[2 * n + 1], scatter, which)
        for cp in out:
            cp.wait_send()
        for cp in inc:
            cp.wait_recv()

    res = pl.pallas_call(
        body, name=name, out_shape=tuple(pltpu.HBM(a.shape, a.dtype) for a in srcs + lands),
        in_specs=[HBM_SPEC] * (2 * n) + [SEM_SPEC, SEM_SPEC, pl.BlockSpec(memory_space=pl.ANY)], out_specs=tuple([HBM_SPEC] * (2 * n)),
        input_output_aliases={i: i for i in range(2 * n)},
        compiler_params=pltpu.CompilerParams(has_side_effects=DATAFLOW),
    )(*srcs, *lands, send_sems, recv_sems, after)
    return list(res[n:])


def _pack(arrays, dtype, row_multiple):
    flat = jnp.concatenate([a.astype(dtype).reshape(-1) for a in arrays])
    rows = -(-flat.shape[0] // (LANES * row_multiple)) * row_multiple
    return jnp.pad(flat, (0, rows * LANES - flat.shape[0])).reshape(rows, LANES)


def _unpack(packed, shapes):
    lead = packed.shape[:-2]
    flat = packed.reshape(lead + (-1,))
    out, off = [], 0
    for shp in shapes:
        size = 1
        for d in shp:
            size *= d
        out.append(flat[..., off:off + size].reshape(lead + tuple(shp)))
        off += size
    return out


def _unshard(g8, axis):
    return jnp.concatenate([g8[j] for j in range(N_DEV)], axis=axis)


def _shard8(full, axis):
    n = full.shape[axis] // N_DEV
    return jnp.stack([lax.slice_in_dim(full, j * n, (j + 1) * n, axis=axis) for j in range(N_DEV)])


VECTOR_WEIGHTS = (("pool_b", 1), ("pool_scale", 1), ("conv_w", 2))
REPLICATED_WEIGHTS = ("norm1_g", "norm2_g", "kv_in_g", "ckv_norm_g", "q_norm_g", "conv_b", "final_g")
WEIGHT_ORDER = ("mod_w", "mod_b", "norm1_g", "norm2_g", "pool_w", "pool_b", "pool_scale", "kv_in_g", "w_dkv", "ckv_norm_g", "w_uk",
                "w_uv", "w_dq", "q_norm_g", "w_uq", "w_o", "w_up", "conv_w", "conv_b", "w_down", "final_g")
SMALL_ROW_MULTIPLE = 16


def _as_2d(a):
    if a.ndim == 1:
        return a.reshape(-1, LANES)
    return a.reshape(-1, a.shape[-1])


def kernel(x, c, positions, mod_w, mod_b, norm1_g, norm2_g, pool_w, pool_b, pool_scale, kv_in_g, w_dkv, ckv_norm_g, w_uk, w_uv, w_dq, q_norm_g, w_uq, w_o, w_up, conv_w, conv_b, w_down, final_g, loss_target, m_mod_w, m_mod_b, m_norm1_g, m_norm2_g, m_pool_w, m_pool_b, m_pool_scale, m_kv_in_g, m_w_dkv, m_ckv_norm_g, m_w_uk, m_w_uv, m_w_dq, m_q_norm_g, m_w_uq, m_w_o, m_w_up, m_conv_w, m_conv_b, m_w_down, m_final_g, v_mod_w, v_mod_b, v_norm1_g, v_norm2_g, v_pool_w, v_pool_b, v_pool_scale, v_kv_in_g, v_w_dkv, v_ckv_norm_g, v_w_uk, v_w_uv, v_w_dq, v_q_norm_g, v_w_uq, v_w_o, v_w_up, v_conv_w, v_conv_b, v_w_down, v_final_g):
    shard = dict(mod_w=mod_w, mod_b=mod_b, norm1_g=norm1_g, norm2_g=norm2_g, pool_w=pool_w, pool_b=pool_b, pool_scale=pool_scale,
                 kv_in_g=kv_in_g, w_dkv=w_dkv, ckv_norm_g=ckv_norm_g, w_uk=w_uk, w_uv=w_uv, w_dq=w_dq, q_norm_g=q_norm_g, w_uq=w_uq,
                 w_o=w_o, w_up=w_up, conv_w=conv_w, conv_b=conv_b, w_down=w_down, final_g=final_g)
    mom_m = dict(mod_w=m_mod_w, mod_b=m_mod_b, norm1_g=m_norm1_g, norm2_g=m_norm2_g, pool_w=m_pool_w, pool_b=m_pool_b,
                 pool_scale=m_pool_scale, kv_in_g=m_kv_in_g, w_dkv=m_w_dkv, ckv_norm_g=m_ckv_norm_g, w_uk=m_w_uk, w_uv=m_w_uv,
                 w_dq=m_w_dq, q_norm_g=m_q_norm_g, w_uq=m_w_uq, w_o=m_w_o, w_up=m_w_up, conv_w=m_conv_w, conv_b=m_conv_b,
                 w_down=m_w_down, final_g=m_final_g)
    mom_v = dict(mod_w=v_mod_w, mod_b=v_mod_b, norm1_g=v_norm1_g, norm2_g=v_norm2_g, pool_w=v_pool_w, pool_b=v_pool_b,
                 pool_scale=v_pool_scale, kv_in_g=v_kv_in_g, w_dkv=v_w_dkv, ckv_norm_g=v_ckv_norm_g, w_uk=v_w_uk, w_uv=v_w_uv,
                 w_dq=v_w_dq, q_norm_g=v_q_norm_g, w_uq=v_w_uq, w_o=v_w_o, w_up=v_w_up, conv_w=v_conv_w, conv_b=v_conv_b,
                 w_down=v_w_down, final_g=v_final_g)
    me = _my_index()
    d6 = N_MOD * D_MODEL
    mod_cols = d6 // N_DEV

    small_in = [c] + [shard[k] for k, _ in VECTOR_WEIGHTS]
    small_all = _exchange("gather_vectors", _pack(small_in, F32, SMALL_ROW_MULTIPLE), scatter=False)
    parts = _unpack(small_all, [a.shape for a in small_in])
    c_all = jnp.pad(parts[0].reshape(N_DEV, D_MODEL), ((0, N_DEV), (0, 0)))
    vec = {k: _unshard(p, ax) for (k, ax), p in zip(VECTOR_WEIGHTS, parts[1:])}

    my_mod_b = lax.dynamic_slice_in_dim(mod_b, me * mod_cols, mod_cols, axis=1)
    mods_mine = _mods_fwd("mods_fwd", c_all, mod_w, my_mod_b)
    mods_all = _exchange("gather_mods", _pack([mods_mine], F32, SMALL_ROW_MULTIPLE), scatter=False)
    mods_all = _unpack(mods_all, [mods_mine.shape])[0]
    mods = lax.dynamic_index_in_dim(mods_all, me, axis=2, keepdims=False)
    mods = jnp.moveaxis(mods, 0, 1).reshape(DEPTH, d6)

    tabq, tabk = _rope_tables(positions[0])
    half = N_DEV // 2
    up_cols = shard["w_up"].shape[2]
    cat = lambda a, axis, lo=0, hi=N_DEV: jnp.concatenate([a[j] for j in range(lo, hi)], axis=axis)

    def stage_pieces(l):
        out = {"pool_w": shard["pool_w"].astype(BF16)} if l == 0 else {}
        if l == N_A_LAYERS:
            out.update({k: shard[k].astype(BF16) for k in ("w_dkv", "w_uk", "w_uv")})
        if l >= N_A_LAYERS:
            out.update({k: shard[k][l - N_A_LAYERS].astype(BF16) for k in ("w_dq", "w_uq", "w_o")})
        out.update(w_up=shard["w_up"][l].astype(BF16), w_down=shard["w_down"][l].astype(BF16))
        return out

    gathers, pool_all = {}, []

    def start_gather(l, behind=None):
        pieces = stage_pieces(l)
        if behind is not None:
            pieces, _ = lax.optimization_barrier((pieces, behind))
        handles, token = _exchange_start(f"gather_start_{l}", list(pieces.values()), scatter=False)
        gathers[l] = (handles, pieces)
        return token[0, 0]

    def wait_gather(l, keys, after, tag=""):
        handles, pieces = gathers[l]
        which = [list(pieces).index(k) for k in keys]
        lands = _exchange_wait(f"gather_wait_{l}{tag}", handles, after, scatter=False, which=which)
        return dict(zip(keys, own_slot(lands, [pieces[k] for k in keys])))

    def whole_weights(l, got):
        w = dict(norm1_g=norm1_g[l], norm2_g=norm2_g[l], conv_w=vec["conv_w"][l], conv_b=conv_b[l])
        if l == 0:
            pool_all.append(got["pool_w"])
        if l < N_A_LAYERS:
            w.update(pool_w=cat(pool_all[0][:, l], 1), pool_b=vec["pool_b"][l], pool_scale=vec["pool_scale"][l])
        else:
            rope = got["w_uq"][..., QK_NOPE:]
            ext = jnp.concatenate([got["w_uq"][..., :QK_NOPE], rope, _swap_halves(rope)], axis=-1)
            w.update(w_dq=got["w_dq"].reshape(D_MODEL, Q_RANK), w_uq_ext=cat(ext, -1), w_o=got["w_o"].reshape(D_MODEL, D_MODEL),
                     q_norm_g=q_norm_g[l - N_A_LAYERS])
        if l == N_A_LAYERS:
            w.update(w_dkv_ext=_extend_w_dkv(got["w_dkv"].reshape(D_MODEL, KV_RANK + QK_ROPE)), w_uk=cat(got["w_uk"], -1),
                     w_uv=cat(got["w_uv"], -1), kv_in_g=kv_in_g, ckv_norm_g=ckv_norm_g)
        return w

    def own_slot(lands, own):
        return [lax.dynamic_update_index_in_dim(p, o, me, 0) for p, o in zip(lands, own)]

    def fetch(l, after):
        up_parts = lambda g8: (cat(g8, -1, 0, half), cat(g8, -1, half, N_DEV))
        if l == 0:
            start_gather(0, behind=mods)
            got = wait_gather(0, ["pool_w"], mods, "_pool")
            w_up = lambda aft: up_parts(wait_gather(0, ["w_up"], aft, "_up")["w_up"])
            w_down = lambda aft: wait_gather(0, ["w_down"], aft, "_down")["w_down"].reshape(D_FF, D_MODEL)
        else:
            got = wait_gather(l, list(gathers[l][1]), after)
            up, down = up_parts(got["w_up"]), got["w_down"].reshape(D_FF, D_MODEL)
            w_up, w_down = (lambda aft: up), (lambda aft: down)
        w = dict(whole_weights(l, got), w_up=w_up, w_down=w_down)
        return w, (start_gather(l + 1) if l + 1 < DEPTH else 0.0)

    scatters, pending, pool_grads, piece_grads = {}, {}, {}, {}

    def reduce_pieces(l, keys, got):
        for k, p in zip(keys, got):
            piece_grads[(k, l)] = _sum8(f"sum_grads_{k}_{l}", p.reshape(N_DEV, -1, p.shape[-1])).reshape(p.shape[1:])

    def start_scatter(name, sent):
        sent = {k: a.astype(BF16) for k, a in sent.items()}
        handles, token = _exchange_start(f"scatter_start_{name}", list(sent.values()), scatter=True)
        scatters[name] = (handles, list(sent), [lax.dynamic_index_in_dim(a, me, 0, keepdims=False) for a in sent.values()])
        return token[0, 0]

    def finish_scatter(name, l, after):
        handles, keys, own = scatters.pop(name)
        reduce_pieces(l, keys, own_slot(_exchange_wait(f"scatter_wait_{name}", handles, after, scatter=True), own))

    def push(l, part, big, after):
        cut = lambda a, n, axis: jnp.stack([lax.slice_in_dim(a, j * n, (j + 1) * n, axis=axis) for j in range(N_DEV)])
        sent = {}
        if part == "down":
            sent["w_down"] = big["w_down"].reshape(N_DEV, D_FF // N_DEV, D_MODEL)
        elif part == "up":
            sent["w_up"] = jnp.stack([lax.slice_in_dim(big[half_], j * up_cols, (j + 1) * up_cols, axis=1)
                                      for half_ in ("w_up_a", "w_up_v") for j in range(half)])
        elif l < N_A_LAYERS:
            pool_grads[l] = big["pool_w"]
        else:
            ext = cut(big["w_uq_ext"], Q_EXT, 1)
            rope = ext[..., QK_NOPE:QK_HEAD] + _unswap_halves(ext[..., QK_HEAD:])
            sent.update(w_dq=big["w_dq"].reshape(N_DEV, D_MODEL // N_DEV, Q_RANK), w_uq=jnp.concatenate([ext[..., :QK_NOPE], rope], axis=-1),
                        w_o=big["w_o"].reshape(N_DEV, D_MODEL // N_DEV, D_MODEL))
        if part == "mix" and l == N_A_LAYERS:
            sent.update(w_dkv=_fold_w_dkv_grad(big["w_dkv_ext"]).reshape(N_DEV, D_MODEL // N_DEV, KV_RANK + QK_ROPE),
                        w_uk=cut(big["w_uk"], QK_NOPE, 1), w_uv=cut(big["w_uv"], V_HEAD, 1))
        if l == 0 and part != "mix":
            return start_scatter(f"0_{part}", sent)
        if l == 0:
            finish_scatter("1", 1, after)
            pool = _shard8(jnp.stack([pool_grads[a] for a in range(N_A_LAYERS)]), 2).astype(BF16)
            reduce_pieces(0, ["pool_w"], _exchange_many("scatter_pool_grads", [pool], scatter=True))
            return 0.0
        pending.setdefault(l, {}).update(sent)
        if part != "mix":
            return 0.0
        if l + 1 < DEPTH:
            finish_scatter(str(l + 1), l + 1, after)
        return start_scatter(str(l), pending.pop(l))

    loss_row, dx, g, dmods = _forward_backward(x[0], loss_target[0], mods, tabq, tabk, final_g, fetch, push)
    layers_of = lambda k, ls: jnp.stack([piece_grads[(k, l)] for l in ls])
    grads = dict(w_dkv=piece_grads[("w_dkv", N_A_LAYERS)], w_uk=piece_grads[("w_uk", N_A_LAYERS)], w_uv=piece_grads[("w_uv", N_A_LAYERS)])
    for k in ("w_dq", "w_uq", "w_o"):
        grads[k] = layers_of(k, range(N_A_LAYERS, DEPTH))

    small_names = REPLICATED_WEIGHTS + tuple(k for k, _ in VECTOR_WEIGHTS)
    small_out = [dmods] + [g[k] for k in small_names] + [loss_row]
    small_shapes = [a.shape for a in small_out]
    small_got = _exchange("gather_small_grads", _pack(small_out, F32, SMALL_ROW_MULTIPLE), scatter=False)
    summed = _unpack(_sum8("sum_small_grads", small_got), small_shapes)
    grads["mod_b"] = summed[0]
    for k, s in zip(small_names, summed[1:-1]):
        grads[k] = s
    for k, ax in VECTOR_WEIGHTS:
        n = shard[k].shape[ax]
        grads[k] = lax.dynamic_slice_in_dim(grads[k], me * n, n, axis=ax)
    loss = summed[-1][0, 0]
    dmods_all = _unpack(small_got, small_shapes)[0]
    dm_mine = lax.dynamic_slice_in_dim(dmods_all, me * mod_cols, mod_cols, axis=2)
    dm_mine = jnp.pad(jnp.moveaxis(dm_mine, 0, 1), ((0, 0), (0, N_DEV), (0, 0)))
    grads["mod_w"] = _mods_bwd("mods_bwd", c_all, dm_mine)

    delta, new_m, new_v = {}, {}, {}

    def adamw(k):
        shp = shard[k].shape
        grads[k] = grads[k].reshape(shp)
        view = (lambda a: jnp.swapaxes(a, 1, 2)) if k == "w_up" else (lambda a: a)
        ops = [view(a) for a in (shard[k], grads[k], mom_m[k], mom_v[k])]
        ops[1] = lax.optimization_barrier(ops[1])
        res = _adamw(f"adamw_{k}", *[_as_2d(a) for a in ops])
        delta[k], new_m[k], new_v[k] = [view(r.reshape(ops[0].shape)) for r in res]
        grads[k] = view(ops[1])

    late = ("w_up", "w_down", "pool_w")
    for k in WEIGHT_ORDER:
        if k not in late:
            adamw(k)
    finish_scatter("0_down", 0, delta["final_g"])
    finish_scatter("0_up", 0, delta["final_g"])
    grads.update(w_up=layers_of("w_up", range(DEPTH)), w_down=layers_of("w_down", range(DEPTH)), pool_w=piece_grads[("pool_w", 0)])
    for k in late:
        adamw(k)
    return (loss, dx[None], *[grads[k] for k in WEIGHT_ORDER], *[delta[k] for k in WEIGHT_ORDER],
            *[new_m[k] for k in WEIGHT_ORDER], *[new_v[k] for k in WEIGHT_ORDER])
```

```python
import functools

import jax
import jax.numpy as jnp
from jax import lax
from jax.experimental import pallas as pl
from jax.experimental.pallas import tpu as pltpu

F32 = jnp.float32
BF16 = jnp.bfloat16

D_MODEL = 1024
DEPTH = 4
N_A_LAYERS = 2
N_B_LAYERS = 2
POOL_WINDOWS = (2, 4, 8, 16)
POOL_GROUP = 256
N_HEADS = 8
QK_NOPE = 128
QK_ROPE = 64
V_HEAD = 128
QK_HEAD = QK_NOPE + QK_ROPE
Q_RANK = 384
KV_RANK = 256
ROPE_THETA = 10000.0
D_FF = 2816
EPS = 1e-6
N_MOD = 6
ADAM_LR = 0.001
ADAM_B1 = 0.9
ADAM_B2 = 0.999
ADAM_EPS = 1e-08
ADAM_WD = 0.01
ADAM_STEP = 10

N_DEV = 8
LANES = 128
Q_EXT = 256
VMEM_LIMIT_BYTES = 48 * 1024 * 1024
MESH = pl.DeviceIdType.MESH
NEG_BIG = -0.7 * float(jnp.finfo(jnp.float32).max)


def _params(sem):
    return pltpu.CompilerParams(dimension_semantics=sem, vmem_limit_bytes=VMEM_LIMIT_BYTES)


def _tile(n, cap):
    if n <= cap:
        return n
    best = None
    for d in range(LANES, cap + 1, LANES):
        if n % d == 0:
            best = d
    assert best is not None, (n, cap)
    return best


def _dot(a, b, dims):
    return lax.dot_general(a, b, (dims, ((), ())), preferred_element_type=F32)


NN = ((1,), (0,))
NT = ((1,), (1,))
TN = ((0,), (0,))


def _mm(name, a, b, mode="nn", out_dtype=BF16, resid=None, gate=None, rowtab=None, second=None,
        tm_cap=1024, tn_cap=1408, tk_cap=1408):
    if mode == "tn":
        kdim, m = a.shape
    else:
        m, kdim = a.shape
    n = b.shape[0] if mode == "nt" else b.shape[1]
    tm, tn, tk = _tile(m, tm_cap), _tile(n, tn_cap), _tile(kdim, tk_cap)
    nk = kdim // tk
    dims = {"nn": NN, "nt": NT, "tn": TN}[mode]
    a_spec = pl.BlockSpec((tk, tm), lambda i, j, k: (k, i)) if mode == "tn" else pl.BlockSpec((tm, tk), lambda i, j, k: (i, k))
    b_spec = pl.BlockSpec((tn, tk), lambda i, j, k: (j, k)) if mode == "nt" else pl.BlockSpec((tk, tn), lambda i, j, k: (k, j))
    o_spec = pl.BlockSpec((tm, tn), lambda i, j, k: (i, j))
    g_spec = pl.BlockSpec((1, tn), lambda i, j, k: (0, j))
    gated = resid is not None

    n_ops = 2 if second is None else 4

    def body(*refs):
        acc = refs[-1]
        k = pl.program_id(2)

        @pl.when(k == 0)
        def _():
            acc[...] = jnp.zeros_like(acc)

        prod = _dot(refs[0][...].astype(BF16), refs[1][...].astype(BF16), dims)
        if second is not None:
            prod = prod + _dot(refs[2][...].astype(BF16), refs[3][...].astype(BF16), dims)
        acc[...] += prod

        @pl.when(k == nk - 1)
        def _():
            rest = refs[n_ops:-1]
            if gated:
                r_ref, g_ref, y_ref, x_ref = rest
                y_ref[...] = acc[...]
                x_ref[...] = r_ref[...] + g_ref[...] * acc[...]
            elif rowtab is not None:
                tab = rest[0][...]
                rest[1][...] = (acc[...] * jnp.concatenate([tab] * (tn // tab.shape[1]), axis=1)).astype(out_dtype)
            else:
                rest[0][...] = acc[...].astype(out_dtype)

    ins, in_specs = [a, b], [a_spec, b_spec]
    if second is not None:
        assert second[0].shape == a.shape and second[1].shape == b.shape
        ins += list(second)
        in_specs += [a_spec, b_spec]
    if rowtab is not None:
        assert tn % rowtab.shape[1] == 0 and not gated
        ins.append(rowtab)
        in_specs.append(pl.BlockSpec((tm, rowtab.shape[1]), lambda i, j, k: (i, 0)))
    if gated:
        ins += [resid, gate]
        in_specs += [o_spec, g_spec]
        out_shape = (jax.ShapeDtypeStruct((m, n), F32), jax.ShapeDtypeStruct((m, n), F32))
        out_specs = (o_spec, o_spec)
    else:
        out_shape = jax.ShapeDtypeStruct((m, n), out_dtype)
        out_specs = o_spec
    return pl.pallas_call(
        body, name=name, grid=(m // tm, n // tn, nk), in_specs=in_specs, out_specs=out_specs, out_shape=out_shape,
        scratch_shapes=[pltpu.VMEM((tm, tn), F32)],
        compiler_params=_params(("parallel", "parallel", "arbitrary")),
    )(*ins)


def _rowwise(name, fn, tiled, bcast, outs, sums=(), tr=512):
    tiled = [t if isinstance(t, tuple) else (t, t.shape[1], 0) for t in tiled]
    s = tiled[0][0].shape[0]
    tr = min(tr, s)
    assert s % tr == 0
    n_t, n_b, n_o = len(tiled), len(bcast), len(outs)

    def body(*refs):
        i = pl.program_id(0)
        vals = [r[...] for r in refs[:n_t + n_b]]
        o_vals, s_vals = fn(*vals)
        for r, v in zip(refs[n_t + n_b:n_t + n_b + n_o], o_vals):
            r[...] = v.astype(r.dtype)
        s_refs = refs[n_t + n_b + n_o:]

        @pl.when(i == 0)
        def _():
            for r in s_refs:
                r[...] = jnp.zeros_like(r)

        for r, v in zip(s_refs, s_vals):
            r[...] += v

    in_specs = [pl.BlockSpec((tr, n), functools.partial(lambda cb, i: (i, cb), cb)) for (_, n, cb) in tiled]
    in_specs += [pl.BlockSpec(b.shape, functools.partial(lambda nd, i: (0,) * nd, b.ndim)) for b in bcast]
    out_specs = [pl.BlockSpec((tr, n), lambda i: (i, 0)) for (n, _) in outs]
    out_specs += [pl.BlockSpec((1, n), lambda i: (0, 0)) for n in sums]
    out_shape = [jax.ShapeDtypeStruct((s, n), dt) for (n, dt) in outs]
    out_shape += [jax.ShapeDtypeStruct((1, n), F32) for n in sums]
    res = pl.pallas_call(
        body, name=name, grid=(s // tr,), in_specs=in_specs, out_specs=tuple(out_specs), out_shape=tuple(out_shape),
        compiler_params=_params(("arbitrary",)),
    )(*[t[0] for t in tiled], *bcast)
    return res


def _colsum(v):
    return jnp.sum(v, axis=0, keepdims=True)


def _rms_fwd(name, x, g, scale=None, shift=None, out_dtype=BF16, ncols=None):
    mod = scale is not None

    def fn(xv, gv, *ss):
        y = xv * lax.rsqrt(jnp.mean(xv * xv, axis=-1, keepdims=True) + EPS) * gv
        if mod:
            y = y * (1.0 + ss[0]) + ss[1]
        return (y,), ()

    n = ncols or x.shape[1]
    return _rowwise(name, fn, [(x, n, 0)], [g] + ([scale, shift] if mod else []), [(n, out_dtype)])[0]


def _rms_bwd(name, x, g, dh, scale=None, dx_in=None, ncols=None, out_dtype=F32):
    mod = scale is not None
    has_in = dx_in is not None

    def fn(*vals):
        xv, dhv = vals[0], vals[1].astype(F32)
        rest = list(vals[2:])
        dxi = rest.pop(0) if has_in else None
        gv = rest.pop(0)
        rstd = lax.rsqrt(jnp.mean(xv * xv, axis=-1, keepdims=True) + EPS)
        xhat = xv * rstd
        sums = []
        if mod:
            sc = rest.pop(0)
            dyn = dhv * (1.0 + sc)
            dshift, dscale = _colsum(dhv), _colsum(dhv * (xhat * gv))
        else:
            dyn = dhv
        dg = _colsum(dyn * xhat)
        dxhat = dyn * gv
        dx = rstd * (dxhat - xhat * jnp.mean(dxhat * xhat, axis=-1, keepdims=True))
        if has_in:
            dx = dx + dxi
        sums = [dg] + ([dshift, dscale] if mod else [])
        return (dx,), sums

    n = ncols or x.shape[1]
    tiled = [(x, n, 0), dh] + ([dx_in] if has_in else [])
    return _rowwise(name, fn, tiled, [g] + ([scale] if mod else []), [(n, out_dtype)], [n] * (3 if mod else 1))


def _gate_bwd(name, dxn, y, g):
    def fn(dv, yv, gv):
        return (gv * dv,), (_colsum(dv * yv),)

    n = dxn.shape[1]
    return _rowwise(name, fn, [dxn, y], [g], [(n, BF16)], [n])


def _loss_head(name, x, g, target):
    n = x.shape[1]

    def fn(xv, tv, gv):
        rstd = lax.rsqrt(jnp.mean(xv * xv, axis=-1, keepdims=True) + EPS)
        xhat = xv * rstd
        err = xhat * gv - tv
        loss = 0.5 * jnp.sum(jnp.sum(err * err, axis=-1, keepdims=True) / n, axis=0, keepdims=True)
        dy = err / n
        dg = _colsum(dy * xhat)
        dxhat = dy * gv
        dx = rstd * (dxhat - xhat * jnp.mean(dxhat * xhat, axis=-1, keepdims=True))
        return (dx,), (dg, jnp.broadcast_to(loss, (1, LANES)))

    return _rowwise(name, fn, [x, target], [g], [(n, F32)], [n, LANES])


def _krope_fwd(name, kv_ext, tabk):
    def fn(xv, tv):
        t = xv * tv
        return (t + pltpu.roll(t, 64, 1),), ()

    return _rowwise(name, fn, [(kv_ext, LANES, 2), tabk], [], [(LANES, BF16)])[0]


def _krope_bwd(name, dkd, tabk):
    def fn(dv, tv):
        d = dv[:, :LANES]
        for h in range(1, N_HEADS):
            d = d + dv[:, h * LANES:(h + 1) * LANES]
        return ((d + pltpu.roll(d, 64, 1)) * tv,), ()

    return _rowwise(name, fn, [dkd, tabk], [], [(LANES, F32)])[0]


def _adamw(name, w, g, m, v):
    def fn(wv, gv, mv, vv):
        m2 = ADAM_B1 * mv + (1.0 - ADAM_B1) * gv
        v2 = ADAM_B2 * vv + (1.0 - ADAM_B2) * (gv * gv)
        m_hat = m2 / (1.0 - ADAM_B1 ** ADAM_STEP)
        v_hat = v2 / (1.0 - ADAM_B2 ** ADAM_STEP)
        delta = -ADAM_LR * (m_hat / (jnp.sqrt(v_hat) + ADAM_EPS) + ADAM_WD * wv)
        return (delta, m2, v2), ()

    r, c = w.shape
    tr = r
    for cand in (512, 256, 128, 64, 32, 16, 8):
        if r % cand == 0 and r > cand:
            tr = cand
            break
    return _rowwise(name, fn, [w, g, m, v], [], [(c, F32)] * 3, tr=tr)


def _sum8(name, parts):
    _, r, c = parts.shape
    tr = r
    for cand in (2048, 1024, 512, 256, 128, 64, 32, 16):
        if r % cand == 0 and r > cand and cand * c <= 256 * 1024:
            tr = cand
            break

    def body(p_ref, o_ref):
        acc = p_ref[0].astype(F32)
        for k in range(1, N_DEV):
            acc = acc + p_ref[k].astype(F32)
        o_ref[...] = acc

    return pl.pallas_call(
        body, name=name, grid=(r // tr,), in_specs=[pl.BlockSpec((N_DEV, tr, c), lambda i: (0, i, 0))],
        out_specs=pl.BlockSpec((tr, c), lambda i: (i, 0)), out_shape=jax.ShapeDtypeStruct((r, c), F32),
        compiler_params=_params(("parallel",)),
    )(parts)


def _mods_fwd(name, c_all, w, b):
    depth, d, n = w.shape

    def body(c_ref, w_ref, b_ref, o_ref):
        cv = c_ref[...]
        sc = (cv * (1.0 / (1.0 + jnp.exp(-cv)))).astype(BF16)
        o_ref[0] = _dot(sc, w_ref[0].astype(BF16), NN) + b_ref[0]

    return pl.pallas_call(
        body, name=name, grid=(depth,),
        in_specs=[pl.BlockSpec(c_all.shape, lambda l: (0, 0)), pl.BlockSpec((1, d, n), lambda l: (l, 0, 0)),
                  pl.BlockSpec((1, 1, n), lambda l: (l, 0, 0))],
        out_specs=pl.BlockSpec((1, c_all.shape[0], n), lambda l: (l, 0, 0)),
        out_shape=jax.ShapeDtypeStruct((depth, c_all.shape[0], n), F32),
        compiler_params=_params(("parallel",)),
    )(c_all, w, b.reshape(depth, 1, n))


def _mods_bwd(name, c_all, dm):
    depth, rows, n = dm.shape
    d = c_all.shape[1]

    def body(c_ref, dm_ref, o_ref):
        cv = c_ref[...]
        sc = (cv * (1.0 / (1.0 + jnp.exp(-cv)))).astype(BF16)
        o_ref[0] = _dot(sc, dm_ref[0].astype(BF16), TN)

    return pl.pallas_call(
        body, name=name, grid=(depth,),
        in_specs=[pl.BlockSpec(c_all.shape, lambda l: (0, 0)), pl.BlockSpec((1, rows, n), lambda l: (l, 0, 0))],
        out_specs=pl.BlockSpec((1, d, n), lambda l: (l, 0, 0)),
        out_shape=jax.ShapeDtypeStruct((depth, d, n), F32),
        compiler_params=_params(("parallel",)),
    )(c_all, dm)


POOL_TILE = 256


def _split_dot(band, val):
    hi = val.astype(BF16)
    lo = (val - hi.astype(F32)).astype(BF16)
    return _dot(band, hi, NN) + _dot(band, lo, NN)


def _pool_fwd(name, h1, x, pw, pb, ps, g1):
    s, d = h1.shape
    t = POOL_TILE

    def body(hc_ref, hp_ref, x_ref, pw_ref, pb_ref, ps_ref, g_ref, xo_ref, zb_ref, pooled_ref):
        i = pl.program_id(0)
        r = lax.broadcasted_iota(jnp.int32, (t, t), 0)
        j = lax.broadcasted_iota(jnp.int32, (t, t), 1)
        pos = (i * t + lax.broadcasted_iota(jnp.int32, (t, 1), 0) + 1).astype(F32)
        has_prev = (i > 0).astype(F32)
        for grp, w in enumerate(POOL_WINDOWS):
            cs = slice(grp * POOL_GROUP, (grp + 1) * POOL_GROUP)
            hc = hc_ref[:, cs]
            band_cur = ((r - j >= 0) & (r - j < w)).astype(BF16)
            band_prev = (r + t - j < w).astype(BF16)
            ssum = _split_dot(band_cur, hc) + has_prev * _split_dot(band_prev, hp_ref[:, cs])
            pooled = (ssum / jnp.minimum(pos, float(w)) - hc).astype(BF16)
            zb = _dot(pooled, pw_ref[grp], NN) + pb_ref[:, cs]
            xo_ref[:, cs] = x_ref[:, cs] + g_ref[:, cs] * (zb * ps_ref[:, cs])
            zb_ref[:, cs] = zb
            pooled_ref[:, cs] = pooled

    row = pl.BlockSpec((t, d), lambda i: (i, 0))
    vec = pl.BlockSpec((1, d), lambda i: (0, 0))
    return pl.pallas_call(
        body, name=name, grid=(s // t,),
        in_specs=[row, pl.BlockSpec((t, d), lambda i: (jnp.maximum(i - 1, 0), 0)), row,
                  pl.BlockSpec(pw.shape, lambda i: (0, 0, 0)), vec, vec, vec],
        out_specs=(row, row, row),
        out_shape=(jax.ShapeDtypeStruct((s, d), F32), jax.ShapeDtypeStruct((s, d), F32), jax.ShapeDtypeStruct((s, d), BF16)),
        compiler_params=_params(("parallel",)),
    )(h1, h1, x, pw, pb, ps, g1)


def _pool_bwd(name, dxn, zb, pooled, pw, ps, g1):
    s, d = dxn.shape
    t = POOL_TILE
    nt = s // t

    def body(dc_ref, dn_ref, zb_ref, pooled_ref, pw_ref, ps_ref, g_ref, dh_ref, dpw_ref, dpb_ref, dps_ref, dg_ref):
        i = pl.program_id(0)

        @pl.when(i == 0)
        def _():
            dpw_ref[...] = jnp.zeros_like(dpw_ref)
            dpb_ref[...] = jnp.zeros_like(dpb_ref)
            dps_ref[...] = jnp.zeros_like(dps_ref)
            dg_ref[...] = jnp.zeros_like(dg_ref)

        jj = lax.broadcasted_iota(jnp.int32, (t, t), 0)
        rr = lax.broadcasted_iota(jnp.int32, (t, t), 1)
        pos = (i * t + lax.broadcasted_iota(jnp.int32, (t, 1), 0) + 1).astype(F32)
        has_next = (i < nt - 1).astype(F32)
        for grp, w in enumerate(POOL_WINDOWS):
            cs = slice(grp * POOL_GROUP, (grp + 1) * POOL_GROUP)
            gv, psv, zbv, dxc = g_ref[:, cs], ps_ref[:, cs], zb_ref[:, cs], dc_ref[:, cs]
            dg_ref[:, cs] += _colsum(dxc * (zbv * psv))
            dy = gv * dxc
            dps_ref[:, cs] += _colsum(dy * zbv)
            dz = dy * psv
            dpb_ref[:, cs] += _colsum(dz)
            dzb = dz.astype(BF16)
            dpw_ref[grp] += _dot(pooled_ref[:, cs], dzb, TN)
            dp = _dot(dzb, pw_ref[grp], NT)
            dzn = (gv * dn_ref[:, cs] * psv).astype(BF16)
            dpn = _dot(dzn, pw_ref[grp], NT) * (has_next / float(w))
            band_cur = ((rr - jj >= 0) & (rr - jj < w)).astype(BF16)
            band_next = (rr + t - jj < w).astype(BF16)
            dh_ref[:, cs] = _split_dot(band_cur, dp / jnp.minimum(pos, float(w))) + _split_dot(band_next, dpn) - dp

    row = pl.BlockSpec((t, d), lambda i: (i, 0))
    vec = pl.BlockSpec((1, d), lambda i: (0, 0))
    wspec = pl.BlockSpec(pw.shape, lambda i: (0, 0, 0))
    return pl.pallas_call(
        body, name=name, grid=(nt,),
        in_specs=[row, pl.BlockSpec((t, d), lambda i: (jnp.minimum(i + 1, nt - 1), 0)), row, row, wspec, vec, vec],
        out_specs=(row, wspec, vec, vec, vec),
        out_shape=(jax.ShapeDtypeStruct((s, d), F32), jax.ShapeDtypeStruct(pw.shape, F32),
                   jax.ShapeDtypeStruct((1, d), F32), jax.ShapeDtypeStruct((1, d), F32), jax.ShapeDtypeStruct((1, d), F32)),
        compiler_params=_params(("arbitrary",)),
    )(dxn, dxn, zb, pooled, pw, ps, g1)


GLU_TILE = 512
HALO = 16
INV_SQRT2 = 0.7071067811865476
INV_SQRT_2PI = 0.3989422804014327


def _up_glu_fwd(name, h2, wa, wv, cw, cb):
    s, d = h2.shape
    f = wa.shape[1]
    tm, tn = _tile(s, 1024), _tile(f, 1408)

    def body(h_ref, hh_ref, wa_ref, wv_ref, cw_ref, cb_ref, ua_ref, gl_ref, gpv_ref, ge_ref):
        i = pl.program_id(1)
        has_prev = (i > 0).astype(F32)
        a = _dot(h_ref[...], wa_ref[...], NN).astype(BF16)
        v = _dot(h_ref[...], wv_ref[...], NN)
        above = (_dot(hh_ref[...], wa_ref[...], NN) * has_prev).astype(BF16)
        ua_ref[...] = a
        ext = jnp.concatenate([above.astype(F32), a.astype(F32)], axis=0)
        e1 = pltpu.roll(ext, 1, 0)[HALO:]
        e2 = pltpu.roll(ext, 2, 0)[HALO:]
        pre = e2 * cw_ref[0:1, :] + e1 * cw_ref[1:2, :] + ext[HALO:] * cw_ref[2:3, :] + cb_ref[...]
        cdf = 0.5 * (1.0 + lax.erf(pre * INV_SQRT2))
        ge = pre * cdf
        gl_ref[...] = (ge * v).astype(gl_ref.dtype)
        gpv_ref[...] = ((cdf + pre * (INV_SQRT_2PI * jnp.exp(-0.5 * pre * pre))) * v).astype(gpv_ref.dtype)
        ge_ref[...] = ge.astype(ge_ref.dtype)

    blk = pl.BlockSpec((tm, tn), lambda j, i: (i, j))
    wspec = pl.BlockSpec((d, tn), lambda j, i: (0, j))
    return pl.pallas_call(
        body, name=name, grid=(f // tn, s // tm),
        in_specs=[pl.BlockSpec((tm, d), lambda j, i: (i, 0)), pl.BlockSpec((HALO, d), lambda j, i: (jnp.maximum(i * (tm // HALO) - 1, 0), 0)),
                  wspec, wspec, pl.BlockSpec((3, tn), lambda j, i: (0, j)), pl.BlockSpec((1, tn), lambda j, i: (0, j))],
        out_specs=(blk, blk, blk, blk), out_shape=tuple(jax.ShapeDtypeStruct((s, f), BF16) for _ in range(4)),
        compiler_params=_params(("parallel", "parallel")),
    )(h2, h2, wa, wv, cw, cb)


def _down_glu_bwd(name, dy2, wd, ua, gpv, ge, cw):
    s, f = ua.shape
    d = dy2.shape[1]
    t, tf = min(GLU_TILE, s), _tile(f, 1408)
    nt = s // t
    te = t + HALO

    def body(dy_ref, dyn_ref, wd_ref, a_ref, ah_ref, g_ref, gn_ref, ge_ref, cw_ref, da_ref, dv_ref, dcw_ref, dcb_ref):
        i = pl.program_id(1)

        @pl.when(i == 0)
        def _():
            dcw_ref[...] = jnp.zeros_like(dcw_ref)
            dcb_ref[...] = jnp.zeros_like(dcb_ref)

        has_prev = (i > 0).astype(F32)
        has_next = (i < nt - 1).astype(F32)
        wdv = wd_ref[...]
        dgl = _dot(dy_ref[...], wdv, NT)
        dgl_below = _dot(dyn_ref[...], wdv, NT) * has_next
        dpre = jnp.concatenate([dgl * g_ref[...].astype(F32), dgl_below * gn_ref[...].astype(F32)], axis=0)
        c0, c1, c2 = cw_ref[0:1, :], cw_ref[1:2, :], cw_ref[2:3, :]
        up1 = pltpu.roll(dpre, te - 1, 0)
        up2 = pltpu.roll(dpre, te - 2, 0)
        da_ref[...] = (dpre * c2 + up1 * c1 + up2 * c0)[:t].astype(da_ref.dtype)
        dv_ref[...] = (dgl * ge_ref[...].astype(F32)).astype(dv_ref.dtype)
        ext = jnp.concatenate([ah_ref[...].astype(F32) * has_prev, a_ref[...].astype(F32)], axis=0)
        dpt = dpre[:t]
        dcb_ref[...] += _colsum(dpt)
        dcw_ref[0:1, :] += _colsum(pltpu.roll(ext, 2, 0)[HALO:] * dpt)
        dcw_ref[1:2, :] += _colsum(pltpu.roll(ext, 1, 0)[HALO:] * dpt)
        dcw_ref[2:3, :] += _colsum(ext[HALO:] * dpt)

    blk = pl.BlockSpec((t, tf), lambda j, i: (i, j))
    prev = pl.BlockSpec((HALO, tf), lambda j, i: (jnp.maximum(i * (t // HALO) - 1, 0), j))
    below = lambda i: jnp.minimum((i + 1) * (t // HALO), s // HALO - 1)
    w3 = pl.BlockSpec((3, tf), lambda j, i: (0, j))
    w1 = pl.BlockSpec((1, tf), lambda j, i: (0, j))
    return pl.pallas_call(
        body, name=name, grid=(f // tf, nt),
        in_specs=[pl.BlockSpec((t, d), lambda j, i: (i, 0)), pl.BlockSpec((HALO, d), lambda j, i: (below(i), 0)),
                  pl.BlockSpec((tf, d), lambda j, i: (j, 0)), blk, prev, blk, pl.BlockSpec((HALO, tf), lambda j, i: (below(i), j)), blk, w3],
        out_specs=(blk, blk, w3, w1),
        out_shape=(jax.ShapeDtypeStruct((s, f), BF16), jax.ShapeDtypeStruct((s, f), BF16),
                   jax.ShapeDtypeStruct((3, f), F32), jax.ShapeDtypeStruct((1, f), F32)),
        compiler_params=_params(("parallel", "arbitrary")),
    )(dy2, dy2, wd, ua, ua, gpv, gpv, ge, cw)


ATT_TILE = 512
ATT_ROWS = 256
ATT_HEADS = 4
ATT_BWD_HEADS = 2
ATT_BWD_VMEM_BYTES = 58 * 1024 * 1024
LOG2E = 1.4426950408889634
LN2 = 0.6931471805599453


def _head_blocks_t(a, width):
    s = a.shape[0]
    t = min(ATT_TILE, s)
    return a.reshape(s // t, t, N_HEADS, width).transpose(2, 0, 3, 1)


def _causal_mask(sv, q0, k0):
    row = q0 + lax.broadcasted_iota(jnp.int32, sv.shape, 0)
    col = k0 + lax.broadcasted_iota(jnp.int32, sv.shape, 1)
    return jnp.where(col <= row, sv, NEG_BIG)


def _attn_fwd(name, q_rot, kt4, v_ext):
    s = q_rot.shape[0]
    t = min(ATT_TILE, s)
    nq = s // t
    rq = min(ATT_ROWS, t)
    nh = ATT_HEADS

    def body(q_ref, kt_ref, v_ref, o_ref, row_ref, acc_ref, m_ref):
        qi = pl.program_id(1)
        acc_ref[...] = jnp.zeros_like(acc_ref)
        m_ref[...] = jnp.full_like(m_ref, NEG_BIG)

        def step(j, masked):
            for hh in range(nh):
                cols = slice(hh * Q_EXT, (hh + 1) * Q_EXT)
                v_blk = v_ref[pl.ds(pl.multiple_of(j * t, t), t), cols]
                for r in range(t // rq):
                    rs = pl.ds(r * rq, rq)
                    sv = _dot(q_ref[rs, cols], kt_ref[hh, j], NN)
                    if masked:
                        sv = _causal_mask(sv, r * rq, 0)
                    m_prev = m_ref[hh, rs, :]
                    m_new = jnp.maximum(m_prev, jnp.max(sv, axis=-1, keepdims=True))
                    p = jnp.exp2(sv - m_new).astype(BF16)
                    acc_ref[hh, rs, :] = jnp.exp2(m_prev - m_new) * acc_ref[hh, rs, :] + _dot(p, v_blk, NN)
                    m_ref[hh, rs, :] = m_new

        def full_step(j, carry):
            step(j, False)
            return carry

        lax.fori_loop(0, qi, full_step, 0)
        step(qi, True)
        for hh in range(nh):
            l = acc_ref[hh, :, V_HEAD:V_HEAD + 1]
            o_ref[:, hh * V_HEAD:(hh + 1) * V_HEAD] = (acc_ref[hh, :, :V_HEAD] / l).astype(o_ref.dtype)
            lse = jnp.broadcast_to(m_ref[hh] + jnp.log(l) * LOG2E, (t, LANES))
            row_ref[hh, 0] = jnp.transpose(lse)[0:8, :]

    return pl.pallas_call(
        body, name=name, grid=(N_HEADS // nh, nq),
        in_specs=[pl.BlockSpec((t, nh * Q_EXT), lambda h, i: (i, h)), pl.BlockSpec((nh, nq, Q_EXT, t), lambda h, i: (h, 0, 0, 0)),
                  pl.BlockSpec((s, nh * Q_EXT), lambda h, i: (0, h))],
        out_specs=(pl.BlockSpec((t, nh * V_HEAD), lambda h, i: (i, h)), pl.BlockSpec((nh, 1, 8, t), lambda h, i: (h, i, 0, 0))),
        out_shape=(jax.ShapeDtypeStruct((s, N_HEADS * V_HEAD), BF16), jax.ShapeDtypeStruct((N_HEADS, nq, 8, t), F32)),
        scratch_shapes=[pltpu.VMEM((nh, t, Q_EXT), F32), pltpu.VMEM((nh, t, 1), F32)],
        compiler_params=_params(("parallel", "parallel")),
    )(q_rot, kt4, v_ext)


def _attn_delta(name, o, do):
    s = o.shape[0]
    t = min(ATT_TILE, s)

    def body(o_ref, do_ref, delta_ref):
        prod = do_ref[...].astype(F32) * o_ref[...].astype(F32)
        for h in range(N_HEADS):
            delta = jnp.sum(prod[:, h * V_HEAD:(h + 1) * V_HEAD], axis=-1, keepdims=True)
            delta_ref[h, 0] = jnp.transpose(jnp.broadcast_to(delta, (t, LANES)))[0:8, :]

    rows = pl.BlockSpec((t, N_HEADS * V_HEAD), lambda i: (i, 0))
    return pl.pallas_call(
        body, name=name, grid=(s // t,), in_specs=[rows, rows],
        out_specs=pl.BlockSpec((N_HEADS, 1, 8, t), lambda i: (0, i, 0, 0)),
        out_shape=jax.ShapeDtypeStruct((N_HEADS, s // t, 8, t), F32),
        compiler_params=_params(("parallel",)),
    )(o, do)


def _attn_bwd(name, kfull, v, qt4, q_rot, dot4, do, lse_row, delta_row, tabq, acc_in=None):
    s = kfull.shape[0]
    t = min(ATT_TILE, s)
    nq = s // t
    nh = ATT_BWD_HEADS
    has_in = acc_in is not None

    def body(*refs):
        k_ref, v_ref, qt_ref, q_ref, dot_ref, do_ref, lse_ref, delta_ref, tab_ref = refs[:9]
        dq_ref, dkn_ref, dkd_ref, dv_ref, dq_acc_ref, acck_ref, accv_ref = refs[-7:]
        kj = pl.program_id(1)

        @pl.when(kj == 0)
        def _():
            dq_acc_ref[...] = jnp.zeros_like(dq_acc_ref)

        acck_ref[...] = jnp.zeros_like(acck_ref)
        accv_ref[...] = jnp.zeros_like(accv_ref)

        def step(i, masked):
            qs = pl.ds(pl.multiple_of(i * t, t), t)
            for hh in range(nh):
                qc = slice(hh * Q_EXT, (hh + 1) * Q_EXT)
                vc = slice(hh * LANES, (hh + 1) * LANES)
                k_blk = k_ref[:, qc]
                st = _dot(k_blk, qt_ref[hh, i], NN)
                if masked:
                    krow = lax.broadcasted_iota(jnp.int32, st.shape, 0)
                    qcol = lax.broadcasted_iota(jnp.int32, st.shape, 1)
                    st = jnp.where(krow <= qcol, st, NEG_BIG)
                pt = jnp.exp2(st - lse_ref[hh, i, 0:1, :])
                accv_ref[hh] += _dot(pt.astype(BF16), do_ref[qs, vc], NN)
                dpt = _dot(v_ref[:, vc], dot_ref[hh, i], NN)
                dst = (pt * (dpt - delta_ref[hh, i, 0:1, :])).astype(BF16)
                acck_ref[hh] += _dot(dst, q_ref[qs, qc], NN)
                dq_acc_ref[hh, qs, :] += _dot(dst, k_blk, TN)

        def full_step(i, carry):
            step(i, False)
            return carry

        step(kj, True)
        lax.fori_loop(kj + 1, nq, full_step, 0)
        for hh in range(nh):
            vc = slice(hh * LANES, (hh + 1) * LANES)
            dk = acck_ref[hh] * LN2
            dkn, dkd, dv = dk[:, :QK_NOPE], dk[:, QK_NOPE:], accv_ref[hh]
            if has_in:
                dkn, dkd, dv = dkn + refs[9][:, vc], dkd + refs[10][:, vc], dv + refs[11][:, vc]
            dkn_ref[:, vc], dkd_ref[:, vc], dv_ref[:, vc] = dkn, dkd, dv

        @pl.when(kj == nq - 1)
        def _():
            for hh in range(nh):
                dq_ref[:, hh * Q_EXT:(hh + 1) * Q_EXT] = (dq_acc_ref[hh] * (tab_ref[...] * LN2)).astype(dq_ref.dtype)

    kblk = pl.BlockSpec((t, nh * LANES), lambda h, j: (j, h))
    col = pl.BlockSpec((s, nh * LANES), lambda h, j: (0, h))
    q_all = pl.BlockSpec((s, nh * Q_EXT), lambda h, j: (0, h))
    stat = pl.BlockSpec((nh, nq, 8, t), lambda h, j: (h, 0, 0, 0))
    ins = [kfull, v, qt4, q_rot, dot4, do, lse_row, delta_row, tabq]
    in_specs = [pl.BlockSpec((t, nh * Q_EXT), lambda h, j: (j, h)), kblk, pl.BlockSpec((nh, nq, Q_EXT, t), lambda h, j: (h, 0, 0, 0)),
                q_all, pl.BlockSpec((nh, nq, V_HEAD, t), lambda h, j: (h, 0, 0, 0)), col, stat, stat,
                pl.BlockSpec((s, Q_EXT), lambda h, j: (0, 0))]
    if has_in:
        ins += list(acc_in)
        in_specs += [kblk, kblk, kblk]
    wide = jax.ShapeDtypeStruct((s, N_HEADS * LANES), F32)
    return pl.pallas_call(
        body, name=name, grid=(N_HEADS // nh, nq), in_specs=in_specs, out_specs=(q_all, kblk, kblk, kblk),
        out_shape=(jax.ShapeDtypeStruct((s, N_HEADS * Q_EXT), BF16), wide, wide, wide),
        scratch_shapes=[pltpu.VMEM((nh, s, Q_EXT), F32), pltpu.VMEM((nh, t, Q_EXT), F32), pltpu.VMEM((nh, t, LANES), F32)],
        compiler_params=pltpu.CompilerParams(dimension_semantics=("parallel", "arbitrary"), vmem_limit_bytes=ATT_BWD_VMEM_BYTES),
    )(*ins)


def _swap_halves(w):
    half = w.shape[-1] // 2
    return jnp.concatenate([-w[..., half:], w[..., :half]], axis=-1)


def _unswap_halves(g):
    half = g.shape[-1] // 2
    return jnp.concatenate([g[..., half:], -g[..., :half]], axis=-1)


def _extend_w_dkv(w):
    return jnp.concatenate([w, _swap_halves(w[:, KV_RANK:])], axis=-1)


def _fold_w_dkv_grad(g):
    rope = g[:, KV_RANK:KV_RANK + QK_ROPE] + _unswap_halves(g[:, KV_RANK + QK_ROPE:])
    return jnp.concatenate([g[:, :KV_RANK], rope], axis=-1)


def _rope_tables(positions):
    inv = 1.0 / (ROPE_THETA ** (jnp.arange(0, QK_ROPE, 2, dtype=F32) / QK_ROPE))
    ang = positions.astype(F32)[:, None] * inv
    cos, sin = jnp.cos(ang), jnp.sin(ang)
    tabk = jnp.concatenate([cos, cos, sin, sin], axis=-1)
    scale = QK_HEAD ** -0.5 * LOG2E
    tabq = jnp.concatenate([jnp.full((positions.shape[0], QK_NOPE), scale, F32), tabk * scale], axis=-1)
    return tabq, tabk


def _forward_backward(x, target, mods, tabq, tabk, final_g, fetch, push):
    row = lambda vec: vec.reshape(1, -1)
    mod = [[row(mods[l, k * D_MODEL:(k + 1) * D_MODEL]) for k in range(N_MOD)] for l in range(DEPTH)]
    saved, weights = [], []
    kv = None
    for l in range(DEPTH):
        w, tok = fetch(l, x)
        sh1, sc1, g1, sh2, sc2, g2 = mod[l]
        sh1 = sh1 + tok
        if l == N_A_LAYERS:
            kvn = _rms_fwd("kvin_fwd", x, row(w["kv_in_g"]))
            kv_ext = _mm("dkv_fwd", kvn, w["w_dkv_ext"], out_dtype=F32)
            ckv = _rms_fwd("ckv_fwd", kv_ext, row(w["ckv_norm_g"]), ncols=KV_RANK)
            kd = _krope_fwd("krope_fwd", kv_ext, tabk)
            kn, v = _mm("uk_fwd", ckv, w["w_uk"]), _mm("uv_fwd", ckv, w["w_uv"])
            heads = lambda a: [a[:, h * LANES:(h + 1) * LANES] for h in range(N_HEADS)]
            kfull = jnp.concatenate([part for kh in heads(kn) for part in (kh, kd)], axis=-1)
            v_ext = jnp.concatenate([part for vh in heads(v) for part in (vh, jnp.ones_like(vh))], axis=-1)
            kv = dict(x=x, kvn=kvn, kv_ext=kv_ext, ckv=ckv, v=v, kfull=kfull, v_ext=v_ext,
                      kt4=_head_blocks_t(kfull, Q_EXT))
        x_in = x
        if l < N_A_LAYERS:
            h1 = _rms_fwd(f"norm1_fwd_{l}", x, row(w["norm1_g"]), sc1, sh1, out_dtype=F32)
            x_mid, zb, pooled = _pool_fwd(f"pool_fwd_{l}", h1, x, w["pool_w"], row(w["pool_b"]), row(w["pool_scale"]), g1)
            mix = (zb, pooled)
        else:
            h1 = _rms_fwd(f"norm1_fwd_{l}", x, row(w["norm1_g"]), sc1, sh1)
            cq_pre = _mm(f"dq_fwd_{l}", h1, w["w_dq"], out_dtype=F32)
            cq = _rms_fwd(f"qnorm_fwd_{l}", cq_pre, row(w["q_norm_g"]))
            q_rot = _mm(f"uq_fwd_{l}", cq, w["w_uq_ext"], rowtab=tabq)
            o, lse_row = _attn_fwd(f"attn_fwd_{l}", q_rot, kv["kt4"], kv["v_ext"])
            y, x_mid = _mm(f"wo_fwd_{l}", o, w["w_o"], resid=x, gate=g1)
            mix = (h1, cq_pre, cq, q_rot, o, lse_row, y)
        h2 = _rms_fwd(f"norm2_fwd_{l}", x_mid, row(w["norm2_g"]), sc2, sh2)
        w_up_a, w_up_v = w["w_up"](h2)
        ua, gl, gpv, ge = _up_glu_fwd(f"up_glu_fwd_{l}", h2, w_up_a, w_up_v, w["conv_w"], row(w["conv_b"]))
        w_down = w["w_down"](gl)
        y2, x = _mm(f"down_fwd_{l}", gl, w_down, resid=x_mid, gate=g2)
        saved.append((x_in, x_mid, h2, ua, gpv, ge, gl, y2, mix))
        weights.append(dict(w, w_up_a=w_up_a, w_up_v=w_up_v, w_down=w_down))

    dx, dfinal_g, loss = _loss_head("loss_head", x, row(final_g), target)
    g = {"final_g": dfinal_g.reshape(-1)}
    per_layer = {k: [None] * DEPTH for k in ("norm1_g", "norm2_g", "conv_w", "conv_b")}
    per_a = {k: [None] * N_A_LAYERS for k in ("pool_b", "pool_scale")}
    per_b = {k: [None] * N_B_LAYERS for k in ("q_norm_g",)}
    dmods = [None] * DEPTH
    dkv = None
    tok = 0.0
    for l in reversed(range(DEPTH)):
        w, big = weights[l], {}
        sh1, sc1, g1, sh2, sc2, g2 = mod[l]
        g2 = g2 + tok
        x_in, x_mid, h2, ua, gpv, ge, gl, y2, mix = saved[l]
        dy2, dg2 = _gate_bwd(f"gate2_bwd_{l}", dx, y2, g2)
        tok = push(l, "down", dict(w_down=_mm(f"down_wgrad_{l}", gl, dy2, mode="tn", tm_cap=1408)), None)
        da, dv_, dcw, dcb = _down_glu_bwd(f"down_glu_bwd_{l}", dy2, w["w_down"], ua, gpv, ge, w["conv_w"] + tok)
        dh2 = _mm(f"up_bwd_{l}", da, w["w_up_a"], mode="nt", out_dtype=F32, second=(dv_, w["w_up_v"]))
        tok = push(l, "up", dict(w_up_a=_mm(f"up_a_wgrad_{l}", h2, da, mode="tn"), w_up_v=_mm(f"up_v_wgrad_{l}", h2, dv_, mode="tn")), None)
        per_layer["conv_w"][l], per_layer["conv_b"][l] = dcw, dcb.reshape(-1)
        dx_mid, dn2, dsh2, dsc2 = _rms_bwd(f"norm2_bwd_{l}", x_mid, row(w["norm2_g"]), dh2, sc2 + tok, dx_in=dx)
        per_layer["norm2_g"][l] = dn2.reshape(-1)
        if l < N_A_LAYERS:
            zb, pooled = mix
            dh1, dpw, dpb, dps, dg1 = _pool_bwd(f"pool_bwd_{l}", dx_mid, zb, pooled, w["pool_w"], row(w["pool_scale"]), g1)
            big["pool_w"] = dpw
            per_a["pool_b"][l], per_a["pool_scale"][l] = dpb.reshape(-1), dps.reshape(-1)
        else:
            j = l - N_A_LAYERS
            h1, cq_pre, cq, q_rot, o, lse_row, y = mix
            dy, dg1 = _gate_bwd(f"gate1_bwd_{l}", dx_mid, y, g1)
            do = _mm(f"wo_bwd_{l}", dy, w["w_o"], mode="nt")
            big["w_o"] = _mm(f"wo_wgrad_{l}", o, dy, mode="tn")
            delta_row = _attn_delta(f"attn_delta_{l}", o, do)
            dq_ext, *dkv = _attn_bwd(f"attn_bwd_{l}", kv["kfull"], kv["v"], _head_blocks_t(q_rot, Q_EXT), q_rot, _head_blocks_t(do, V_HEAD), do,
                                     lse_row, delta_row, tabq, acc_in=dkv)
            dcq = _mm(f"uq_bwd_{l}", dq_ext, w["w_uq_ext"], mode="nt", out_dtype=F32)
            big["w_uq_ext"] = _mm(f"uq_wgrad_{l}", cq, dq_ext, mode="tn", out_dtype=F32)
            dcq_pre, dqn = _rms_bwd(f"qnorm_bwd_{l}", cq_pre, row(w["q_norm_g"]), dcq, out_dtype=BF16)
            per_b["q_norm_g"][j] = dqn.reshape(-1)
            dh1 = _mm(f"dq_bwd_{l}", dcq_pre, w["w_dq"], mode="nt")
            big["w_dq"] = _mm(f"dq_wgrad_{l}", h1, dcq_pre, mode="tn")
        dx, dn1, dsh1, dsc1 = _rms_bwd(f"norm1_bwd_{l}", x_in, row(w["norm1_g"]), dh1, sc1, dx_in=dx_mid)
        per_layer["norm1_g"][l] = dn1.reshape(-1)
        dmods[l] = jnp.concatenate([dsh1, dsc1, dg1, dsh2, dsc2, dg2], axis=-1).reshape(-1)
        if l == N_A_LAYERS:
            dkn, dkd, dv = dkv
            dckv = _mm("ukv_bwd", dkn, w["w_uk"], mode="nt", out_dtype=F32, second=(dv, w["w_uv"]))
            big["w_uk"] = _mm("uk_wgrad", kv["ckv"], dkn, mode="tn")
            big["w_uv"] = _mm("uv_wgrad", kv["ckv"], dv, mode="tn")
            dkr = _krope_bwd("krope_bwd", dkd, tabk)
            dc, dckv_g = _rms_bwd("ckv_bwd", kv["kv_ext"], row(w["ckv_norm_g"]), dckv, ncols=KV_RANK, out_dtype=BF16)
            dkv_ext = jnp.concatenate([dc, dkr.astype(BF16)], axis=-1)
            dkvn = _mm("dkv_bwd", dkv_ext, w["w_dkv_ext"], mode="nt")
            big["w_dkv_ext"] = _mm("dkv_wgrad", kv["kvn"], dkv_ext, mode="tn", out_dtype=F32)
            dx, dkv_in_g = _rms_bwd("kvin_bwd", kv["x"], row(w["kv_in_g"]), dkvn, dx_in=dx)
            g["ckv_norm_g"], g["kv_in_g"] = dckv_g.reshape(-1), dkv_in_g.reshape(-1)
        tok = push(l, "mix", big, dx)
    for group in (per_layer, per_a, per_b):
        for k, vals in group.items():
            g[k] = jnp.stack(vals)
    return loss, dx, g, jnp.stack(dmods)


def _my_index():
    return 4 * lax.axis_index("x") + 2 * lax.axis_index("y") + lax.axis_index("c")


def _peer(k):
    x, y, c = lax.axis_index("x"), lax.axis_index("y"), lax.axis_index("c")
    return (1 - x if k & 4 else x, 1 - y if k & 2 else y, 1 - c if k & 1 else c)


def _index_of(pos):
    return 4 * pos[0] + 2 * pos[1] + pos[2]


def _exchange_many(name, arrays, scatter):
    n = len(arrays)
    blocks = [tuple(a.shape[1:]) if scatter else tuple(a.shape) for a in arrays]

    def body(*refs):
        x_refs, o_refs = refs[:n], refs[n:2 * n]
        send_sems, recv_sems, local_sems = refs[2 * n:]
        me = _my_index()
        started = []
        for a in range(n):
            mine = pltpu.make_async_copy(x_refs[a].at[me] if scatter else x_refs[a], o_refs[a].at[me], local_sems.at[a])
            mine.start()
            started.append(mine)
        sends = []
        for k in range(1, N_DEV):
            peer = _peer(k)
            for a in range(n):
                cp = pltpu.make_async_remote_copy(
                    src_ref=x_refs[a].at[_index_of(peer)] if scatter else x_refs[a], dst_ref=o_refs[a].at[me],
                    send_sem=send_sems.at[a, k - 1], recv_sem=recv_sems.at[a, k - 1], device_id=peer, device_id_type=MESH)
                cp.start()
                sends.append(cp)
        for k in range(1, N_DEV):
            peer = _peer(k)
            for a in range(n):
                pltpu.make_async_remote_copy(
                    src_ref=x_refs[a].at[me] if scatter else x_refs[a], dst_ref=o_refs[a].at[_index_of(peer)],
                    send_sem=send_sems.at[a, k - 1], recv_sem=recv_sems.at[a, k - 1], device_id=peer, device_id_type=MESH).wait_recv()
        for cp in sends:
            cp.wait_send()
        for mine in started:
            mine.wait()

    return pl.pallas_call(
        body, name=name, out_shape=tuple(jax.ShapeDtypeStruct((N_DEV,) + blk, a.dtype) for blk, a in zip(blocks, arrays)),
        in_specs=[pl.BlockSpec(memory_space=pl.ANY)] * n, out_specs=tuple([pl.BlockSpec(memory_space=pl.ANY)] * n),
        scratch_shapes=[pltpu.SemaphoreType.DMA((n, N_DEV - 1)), pltpu.SemaphoreType.DMA((n, N_DEV - 1)), pltpu.SemaphoreType.DMA((n,))],
    )(*arrays)


def _exchange(name, x, scatter):
    return _exchange_many(name, [x], scatter)[0]


HBM_SPEC = pl.BlockSpec(memory_space=pltpu.HBM)
SEM_SPEC = pl.BlockSpec(memory_space=pltpu.SEMAPHORE)
DATAFLOW = pltpu.SideEffectType.DATAFLOW_SIDE_EFFECTING


def _remote_copies(x_refs, land_refs, send_sems, recv_sems, scatter, numbers=None):
    me = _my_index()
    numbers = list(range(len(x_refs))) if numbers is None else numbers
    out, inc = [], []
    for a in range(len(x_refs)):
        for k in range(1, N_DEV):
            peer = _peer(k)
            pair = numbers[a] * (N_DEV - 1) + k - 1
            sems = dict(send_sem=send_sems.at[pair], recv_sem=recv_sems.at[pair], device_id=peer, device_id_type=MESH)
            out.append(pltpu.make_async_remote_copy(
                src_ref=x_refs[a].at[_index_of(peer)] if scatter else x_refs[a], dst_ref=land_refs[a].at[me], **sems))
            inc.append(pltpu.make_async_remote_copy(
                src_ref=x_refs[a].at[me] if scatter else x_refs[a], dst_ref=land_refs[a].at[_index_of(peer)], **sems))
    return out, inc


def _exchange_start(name, arrays, scatter):
    n = len(arrays)
    blocks = [tuple(a.shape[1:]) if scatter else tuple(a.shape) for a in arrays]

    def body(*refs):
        x_refs, land_refs = refs[:n], refs[n:2 * n]
        send_sems, recv_sems = refs[2 * n], refs[2 * n + 1]
        for cp in _remote_copies(x_refs, land_refs, send_sems, recv_sems, scatter)[0]:
            cp.start()
        refs[-1][...] = jnp.zeros_like(refs[-1])

    sem_type = pltpu.SemaphoreType.DMA((n * (N_DEV - 1),))
    lands =[pltpu.with_memory_space_constraint(lax.empty((N_DEV,) + blk, a.dtype), pltpu.HBM) for blk, a in zip(blocks, arrays)]
    srcs = [pltpu.with_memory_space_constraint(a, pltpu.HBM) for a in arrays]
    res = pl.pallas_call(
        body, name=name,
        out_shape=(sem_type, sem_type, *[pltpu.HBM(a.shape, a.dtype) for a in srcs + lands], jax.ShapeDtypeStruct((8, LANES), F32)),
        in_specs=[HBM_SPEC] * (2 * n), out_specs=(SEM_SPEC, SEM_SPEC, *[HBM_SPEC] * (2 * n), pl.BlockSpec(memory_space=pltpu.VMEM)),
        input_output_aliases={i: 2 + i for i in range(2 * n)},
        compiler_params=pltpu.CompilerParams(has_side_effects=DATAFLOW),
    )(*srcs, *lands)
    return (res[0], res[1], list(res[2:2 + n]), list(res[2 + n:2 + 2 * n])), res[-1]


def _exchange_wait(name, handles, after, scatter, which=None):
    send_sems, recv_sems, srcs, lands = handles
    which = list(range(len(srcs))) if which is None else list(which)
    srcs, lands = [srcs[a] for a in which], [lands[a] for a in which]
    n = len(srcs)

    def body(*refs):
        x_refs, land_refs = refs[:n], refs[n:2 * n]
        out, inc = _remote_copies(x_refs, land_refs, refs[2 * n], refs[2 * n + 1], scatter, which)
        for cp in out:
            cp.wait_send()
        for cp in inc:
            cp.wait_recv()

    res = pl.pallas_call(
        body, name=name, out_shape=tuple(pltpu.HBM(a.shape, a.dtype) for a in srcs + lands),
        in_specs=[HBM_SPEC] * (2 * n) + [SEM_SPEC, SEM_SPEC, pl.BlockSpec(memory_space=pl.ANY)], out_specs=tuple([HBM_SPEC] * (2 * n)),
        input_output_aliases={i: i for i in range(2 * n)},
        compiler_params=pltpu.CompilerParams(has_side_effects=DATAFLOW),
    )(*srcs, *lands, send_sems, recv_sems, after)
    return list(res[n:])


def _pack(arrays, dtype, row_multiple):
    flat = jnp.concatenate([a.astype(dtype).reshape(-1) for a in arrays])
    rows = -(-flat.shape[0] // (LANES * row_multiple)) * row_multiple
    return jnp.pad(flat, (0, rows * LANES - flat.shape[0])).reshape(rows, LANES)


def _unpack(packed, shapes):
    lead = packed.shape[:-2]
    flat = packed.reshape(lead + (-1,))
    out, off = [], 0
    for shp in shapes:
        size = 1
        for d in shp:
            size *= d
        out.append(flat[..., off:off + size].reshape(lead + tuple(shp)))
        off += size
    return out


def _unshard(g8, axis):
    return jnp.concatenate([g8[j] for j in range(N_DEV)], axis=axis)


def _shard8(full, axis):
    n = full.shape[axis] // N_DEV
    return jnp.stack([lax.slice_in_dim(full, j * n, (j + 1) * n, axis=axis) for j in range(N_DEV)])


VECTOR_WEIGHTS = (("pool_b", 1), ("pool_scale", 1), ("conv_w", 2))
REPLICATED_WEIGHTS = ("norm1_g", "norm2_g", "kv_in_g", "ckv_norm_g", "q_norm_g", "conv_b", "final_g")
WEIGHT_ORDER = ("mod_w", "mod_b", "norm1_g", "norm2_g", "pool_w", "pool_b", "pool_scale", "kv_in_g", "w_dkv", "ckv_norm_g", "w_uk",
                "w_uv", "w_dq", "q_norm_g", "w_uq", "w_o", "w_up", "conv_w", "conv_b", "w_down", "final_g")
SMALL_ROW_MULTIPLE = 16


def _as_2d(a):
    if a.ndim == 1:
        return a.reshape(-1, LANES)
    return a.reshape(-1, a.shape[-1])


def kernel(x, c, positions, mod_w, mod_b, norm1_g, norm2_g, pool_w, pool_b, pool_scale, kv_in_g, w_dkv, ckv_norm_g, w_uk, w_uv, w_dq, q_norm_g, w_uq, w_o, w_up, conv_w, conv_b, w_down, final_g, loss_target, m_mod_w, m_mod_b, m_norm1_g, m_norm2_g, m_pool_w, m_pool_b, m_pool_scale, m_kv_in_g, m_w_dkv, m_ckv_norm_g, m_w_uk, m_w_uv, m_w_dq, m_q_norm_g, m_w_uq, m_w_o, m_w_up, m_conv_w, m_conv_b, m_w_down, m_final_g, v_mod_w, v_mod_b, v_norm1_g, v_norm2_g, v_pool_w, v_pool_b, v_pool_scale, v_kv_in_g, v_w_dkv, v_ckv_norm_g, v_w_uk, v_w_uv, v_w_dq, v_q_norm_g, v_w_uq, v_w_o, v_w_up, v_conv_w, v_conv_b, v_w_down, v_final_g):
    shard = dict(mod_w=mod_w, mod_b=mod_b, norm1_g=norm1_g, norm2_g=norm2_g, pool_w=pool_w, pool_b=pool_b, pool_scale=pool_scale,
                 kv_in_g=kv_in_g, w_dkv=w_dkv, ckv_norm_g=ckv_norm_g, w_uk=w_uk, w_uv=w_uv, w_dq=w_dq, q_norm_g=q_norm_g, w_uq=w_uq,
                 w_o=w_o, w_up=w_up, conv_w=conv_w, conv_b=conv_b, w_down=w_down, final_g=final_g)
    mom_m = dict(mod_w=m_mod_w, mod_b=m_mod_b, norm1_g=m_norm1_g, norm2_g=m_norm2_g, pool_w=m_pool_w, pool_b=m_pool_b,
                 pool_scale=m_pool_scale, kv_in_g=m_kv_in_g, w_dkv=m_w_dkv, ckv_norm_g=m_ckv_norm_g, w_uk=m_w_uk, w_uv=m_w_uv,
                 w_dq=m_w_dq, q_norm_g=m_q_norm_g, w_uq=m_w_uq, w_o=m_w_o, w_up=m_w_up, conv_w=m_conv_w, conv_b=m_conv_b,
                 w_down=m_w_down, final_g=m_final_g)
    mom_v = dict(mod_w=v_mod_w, mod_b=v_mod_b, norm1_g=v_norm1_g, norm2_g=v_norm2_g, pool_w=v_pool_w, pool_b=v_pool_b,
                 pool_scale=v_pool_scale, kv_in_g=v_kv_in_g, w_dkv=v_w_dkv, ckv_norm_g=v_ckv_norm_g, w_uk=v_w_uk, w_uv=v_w_uv,
                 w_dq=v_w_dq, q_norm_g=v_q_norm_g, w_uq=v_w_uq, w_o=v_w_o, w_up=v_w_up, conv_w=v_conv_w, conv_b=v_conv_b,
                 w_down=v_w_down, final_g=v_final_g)
    me = _my_index()
    d6 = N_MOD * D_MODEL
    mod_cols = d6 // N_DEV

    small_in = [c] + [shard[k] for k, _ in VECTOR_WEIGHTS]
    small_all = _exchange("gather_vectors", _pack(small_in, F32, SMALL_ROW_MULTIPLE), scatter=False)
    parts = _unpack(small_all, [a.shape for a in small_in])
    c_all = jnp.pad(parts[0].reshape(N_DEV, D_MODEL), ((0, N_DEV), (0, 0)))
    vec = {k: _unshard(p, ax) for (k, ax), p in zip(VECTOR_WEIGHTS, parts[1:])}

    my_mod_b = lax.dynamic_slice_in_dim(mod_b, me * mod_cols, mod_cols, axis=1)
    mods_mine = _mods_fwd("mods_fwd", c_all, mod_w, my_mod_b)
    mods_all = _exchange("gather_mods", _pack([mods_mine], F32, SMALL_ROW_MULTIPLE), scatter=False)
    mods_all = _unpack(mods_all, [mods_mine.shape])[0]
    mods = lax.dynamic_index_in_dim(mods_all, me, axis=2, keepdims=False)
    mods = jnp.moveaxis(mods, 0, 1).reshape(DEPTH, d6)

    tabq, tabk = _rope_tables(positions[0])
    half = N_DEV // 2
    up_cols = shard["w_up"].shape[2]
    cat = lambda a, axis, lo=0, hi=N_DEV: jnp.concatenate([a[j] for j in range(lo, hi)], axis=axis)

    def stage_pieces(l):
        out = {"pool_w": shard["pool_w"].astype(BF16)} if l == 0 else {}
        if l == N_A_LAYERS:
            out.update({k: shard[k].astype(BF16) for k in ("w_dkv", "w_uk", "w_uv")})
        if l >= N_A_LAYERS:
            out.update({k: shard[k][l - N_A_LAYERS].astype(BF16) for k in ("w_dq", "w_uq", "w_o")})
        out.update(w_up=shard["w_up"][l].astype(BF16), w_down=shard["w_down"][l].astype(BF16))
        return out

    gathers, pool_all = {}, []

    def start_gather(l, behind=None):
        pieces = stage_pieces(l)
        if behind is not None:
            pieces, _ = lax.optimization_barrier((pieces, behind))
        handles, token = _exchange_start(f"gather_start_{l}", list(pieces.values()), scatter=False)
        gathers[l] = (handles, pieces)
        return token[0, 0]

    def wait_gather(l, keys, after, tag=""):
        handles, pieces = gathers[l]
        which = [list(pieces).index(k) for k in keys]
        lands = _exchange_wait(f"gather_wait_{l}{tag}", handles, after, scatter=False, which=which)
        return dict(zip(keys, own_slot(lands, [pieces[k] for k in keys])))

    def whole_weights(l, got):
        w = dict(norm1_g=norm1_g[l], norm2_g=norm2_g[l], conv_w=vec["conv_w"][l], conv_b=conv_b[l])
        if l == 0:
            pool_all.append(got["pool_w"])
        if l < N_A_LAYERS:
            w.update(pool_w=cat(pool_all[0][:, l], 1), pool_b=vec["pool_b"][l], pool_scale=vec["pool_scale"][l])
        else:
            rope = got["w_uq"][..., QK_NOPE:]
            ext = jnp.concatenate([got["w_uq"][..., :QK_NOPE], rope, _swap_halves(rope)], axis=-1)
            w.update(w_dq=got["w_dq"].reshape(D_MODEL, Q_RANK), w_uq_ext=cat(ext, -1), w_o=got["w_o"].reshape(D_MODEL, D_MODEL),
                     q_norm_g=q_norm_g[l - N_A_LAYERS])
        if l == N_A_LAYERS:
            w.update(w_dkv_ext=_extend_w_dkv(got["w_dkv"].reshape(D_MODEL, KV_RANK + QK_ROPE)), w_uk=cat(got["w_uk"], -1),
                     w_uv=cat(got["w_uv"], -1), kv_in_g=kv_in_g, ckv_norm_g=ckv_norm_g)
        return w

    def own_slot(lands, own):
        return [lax.dynamic_update_index_in_dim(p, o, me, 0) for p, o in zip(lands, own)]

    def fetch(l, after):
        up_parts = lambda g8: (cat(g8, -1, 0, half), cat(g8, -1, half, N_DEV))
        if l == 0:
            start_gather(0, behind=mods)
            got = wait_gather(0, ["pool_w"], mods, "_pool")
            w_up = lambda aft: up_parts(wait_gather(0, ["w_up"], aft, "_up")["w_up"])
            w_down = lambda aft: wait_gather(0, ["w_down"], aft, "_down")["w_down"].reshape(D_FF, D_MODEL)
        else:
            got = wait_gather(l, list(gathers[l][1]), after)
            up, down = up_parts(got["w_up"]), got["w_down"].reshape(D_FF, D_MODEL)
            w_up, w_down = (lambda aft: up), (lambda aft: down)
        w = dict(whole_weights(l, got), w_up=w_up, w_down=w_down)
        return w, (start_gather(l + 1) if l + 1 < DEPTH else 0.0)

    scatters, pending, pool_grads, piece_grads = {}, {}, {}, {}

    def reduce_pieces(l, keys, got):
        for k, p in zip(keys, got):
            piece_grads[(k, l)] = _sum8(f"sum_grads_{k}_{l}", p.reshape(N_DEV, -1, p.shape[-1])).reshape(p.shape[1:])

    def start_scatter(name, sent):
        sent = {k: a.astype(BF16) for k, a in sent.items()}
        handles, token = _exchange_start(f"scatter_start_{name}", list(sent.values()), scatter=True)
        scatters[name] = (handles, list(sent), [lax.dynamic_index_in_dim(a, me, 0, keepdims=False) for a in sent.values()])
        return token[0, 0]

    def finish_scatter(name, l, after):
        handles, keys, own = scatters.pop(name)
        reduce_pieces(l, keys, own_slot(_exchange_wait(f"scatter_wait_{name}", handles, after, scatter=True), own))

    def push(l, part, big, after):
        cut = lambda a, n, axis: jnp.stack([lax.slice_in_dim(a, j * n, (j + 1) * n, axis=axis) for j in range(N_DEV)])
        sent = {}
        if part == "down":
            sent["w_down"] = big["w_down"].reshape(N_DEV, D_FF // N_DEV, D_MODEL)
        elif part == "up":
            sent["w_up"] = jnp.stack([lax.slice_in_dim(big[half_], j * up_cols, (j + 1) * up_cols, axis=1)
                                      for half_ in ("w_up_a", "w_up_v") for j in range(half)])
        elif l < N_A_LAYERS:
            pool_grads[l] = big["pool_w"]
        else:
            ext = cut(big["w_uq_ext"], Q_EXT, 1)
            rope = ext[..., QK_NOPE:QK_HEAD] + _unswap_halves(ext[..., QK_HEAD:])
            sent.update(w_dq=big["w_dq"].reshape(N_DEV, D_MODEL // N_DEV, Q_RANK), w_uq=jnp.concatenate([ext[..., :QK_NOPE], rope], axis=-1),
                        w_o=big["w_o"].reshape(N_DEV, D_MODEL // N_DEV, D_MODEL))
        if part == "mix" and l == N_A_LAYERS:
            sent.update(w_dkv=_fold_w_dkv_grad(big["w_dkv_ext"]).reshape(N_DEV, D_MODEL // N_DEV, KV_RANK + QK_ROPE),
                        w_uk=cut(big["w_uk"], QK_NOPE, 1), w_uv=cut(big["w_uv"], V_HEAD, 1))
        if l == 0 and part != "mix":
            return start_scatter(f"0_{part}", sent)
        if l == 0:
            finish_scatter("1", 1, after)
            pool = _shard8(jnp.stack([pool_grads[a] for a in range(N_A_LAYERS)]), 2).astype(BF16)
            reduce_pieces(0, ["pool_w"], _exchange_many("scatter_pool_grads", [pool], scatter=True))
            return 0.0
        pending.setdefault(l, {}).update(sent)
        if part != "mix":
            return 0.0
        if l + 1 < DEPTH:
            finish_scatter(str(l + 1), l + 1, after)
        return start_scatter(str(l), pending.pop(l))

    loss_row, dx, g, dmods = _forward_backward(x[0], loss_target[0], mods, tabq, tabk, final_g, fetch, push)
    layers_of = lambda k, ls: jnp.stack([piece_grads[(k, l)] for l in ls])
    grads = dict(w_dkv=piece_grads[("w_dkv", N_A_LAYERS)], w_uk=piece_grads[("w_uk", N_A_LAYERS)], w_uv=piece_grads[("w_uv", N_A_LAYERS)])
    for k in ("w_dq", "w_uq", "w_o"):
        grads[k] = layers_of(k, range(N_A_LAYERS, DEPTH))

    small_names = REPLICATED_WEIGHTS + tuple(k for k, _ in VECTOR_WEIGHTS)
    small_out = [dmods] + [g[k] for k in small_names] + [loss_row]
    small_shapes = [a.shape for a in small_out]
    small_got = _exchange("gather_small_grads", _pack(small_out, F32, SMALL_ROW_MULTIPLE), scatter=False)
    summed = _unpack(_sum8("sum_small_grads", small_got), small_shapes)
    grads["mod_b"] = summed[0]
    for k, s in zip(small_names, summed[1:-1]):
        grads[k] = s
    for k, ax in VECTOR_WEIGHTS:
        n = shard[k].shape[ax]
        grads[k] = lax.dynamic_slice_in_dim(grads[k], me * n, n, axis=ax)
    loss = summed[-1][0, 0]
    dmods_all = _unpack(small_got, small_shapes)[0]
    dm_mine = lax.dynamic_slice_in_dim(dmods_all, me * mod_cols, mod_cols, axis=2)
    dm_mine = jnp.pad(jnp.moveaxis(dm_mine, 0, 1), ((0, 0), (0, N_DEV), (0, 0)))
    grads["mod_w"] = _mods_bwd("mods_bwd", c_all, dm_mine)

    delta, new_m, new_v = {}, {}, {}

    def adamw(k):
        shp = shard[k].shape
        grads[k] = grads[k].reshape(shp)
        view = (lambda a: jnp.swapaxes(a, 1, 2)) if k == "w_up" else (lambda a: a)
        ops = [view(a) for a in (shard[k], grads[k], mom_m[k], mom_v[k])]
        ops[1] = lax.optimization_barrier(ops[1])
        res = _adamw(f"adamw_{k}", *[_as_2d(a) for a in ops])
        delta[k], new_m[k], new_v[k] = [view(r.reshape(ops[0].shape)) for r in res]
        grads[k] = view(ops[1])

    late = ("w_up", "w_down", "pool_w")
    for k in WEIGHT_ORDER:
        if k not in late:
            adamw(k)
    finish_scatter("0_down", 0, delta["final_g"])
    finish_scatter("0_up", 0, delta["final_g"])
    grads.update(w_up=layers_of("w_up", range(DEPTH)), w_down=layers_of("w_down", range(DEPTH)), pool_w=piece_grads[("pool_w", 0)])
    for k in late:
        adamw(k)
    return (loss, dx[None], *[grads[k] for k in WEIGHT_ORDER], *[delta[k] for k in WEIGHT_ORDER],
            *[new_m[k] for k in WEIGHT_ORDER], *[new_v[k] for k in WEIGHT_ORDER])
```

```python
import functools

import jax
import jax.numpy as jnp
from jax import lax
from jax.experimental import pallas as pl
from jax.experimental.pallas import tpu as pltpu

F32 = jnp.float32
BF16 = jnp.bfloat16

D_MODEL = 1024
DEPTH = 4
N_A_LAYERS = 2
N_B_LAYERS = 2
POOL_WINDOWS = (2, 4, 8, 16)
POOL_GROUP = 256
N_HEADS = 8
QK_NOPE = 128
QK_ROPE = 64
V_HEAD = 128
QK_HEAD = QK_NOPE + QK_ROPE
Q_RANK = 384
KV_RANK = 256
ROPE_THETA = 10000.0
D_FF = 2816
EPS = 1e-6
N_MOD = 6
ADAM_LR = 0.001
ADAM_B1 = 0.9
ADAM_B2 = 0.999
ADAM_EPS = 1e-08
ADAM_WD = 0.01
ADAM_STEP = 10

N_DEV = 8
LANES = 128
Q_EXT = 256
VMEM_LIMIT_BYTES = 48 * 1024 * 1024
MESH = pl.DeviceIdType.MESH
NEG_BIG = -0.7 * float(jnp.finfo(jnp.float32).max)


def _params(sem):
    return pltpu.CompilerParams(dimension_semantics=sem, vmem_limit_bytes=VMEM_LIMIT_BYTES)


def _tile(n, cap):
    if n <= cap:
        return n
    best = None
    for d in range(LANES, cap + 1, LANES):
        if n % d == 0:
            best = d
    assert best is not None, (n, cap)
    return best


def _dot(a, b, dims):
    return lax.dot_general(a, b, (dims, ((), ())), preferred_element_type=F32)


NN = ((1,), (0,))
NT = ((1,), (1,))
TN = ((0,), (0,))


def _mm(name, a, b, mode="nn", out_dtype=BF16, resid=None, gate=None, rowtab=None, second=None, a_scale=None, gate_grad=None,
        tm_cap=1024, tn_cap=1408, tk_cap=1408):
    if mode == "tn":
        kdim, m = a.shape
    else:
        m, kdim = a.shape
    n = b.shape[0] if mode == "nt" else b.shape[1]
    tm, tn, tk = _tile(m, tm_cap), _tile(n, tn_cap), _tile(kdim, tk_cap)
    nk = kdim // tk
    dims = {"nn": NN, "nt": NT, "tn": TN}[mode]
    a_spec = pl.BlockSpec((tk, tm), lambda i, j, k: (k, i)) if mode == "tn" else pl.BlockSpec((tm, tk), lambda i, j, k: (i, k))
    b_spec = pl.BlockSpec((tn, tk), lambda i, j, k: (j, k)) if mode == "nt" else pl.BlockSpec((tk, tn), lambda i, j, k: (k, j))
    o_spec = pl.BlockSpec((tm, tn), lambda i, j, k: (i, j))
    g_spec = pl.BlockSpec((1, tn), lambda i, j, k: (0, j))
    gated = resid is not None
    assert sum(x is not None for x in (resid, rowtab, gate_grad)) <= 1
    n_ops = 2 if second is None else 4
    n_extra = 1 if a_scale is not None else 0

    def body(*refs):
        acc = refs[-1]
        i, k = pl.program_id(0), pl.program_id(2)

        @pl.when(k == 0)
        def _():
            acc[...] = jnp.zeros_like(acc)

        av = refs[0][...]
        if a_scale is not None:
            av = av.astype(F32) * refs[n_ops][...]
        prod = _dot(av.astype(BF16), refs[1][...].astype(BF16), dims)
        if second is not None:
            prod = prod + _dot(refs[2][...].astype(BF16), refs[3][...].astype(BF16), dims)
        acc[...] += prod
        rest = refs[n_ops + n_extra:-1]

        if gate_grad is not None:
            @pl.when((i == 0) & (k == 0))
            def _():
                rest[3][...] = jnp.zeros_like(rest[3])

        @pl.when(k == nk - 1)
        def _():
            if gated:
                r_ref, g_ref, x_ref = rest
                x_ref[...] = r_ref[...] + g_ref[...] * acc[...]
            elif rowtab is not None:
                tab = rest[0][...]
                rest[1][...] = (acc[...] * jnp.concatenate([tab] * (tn // tab.shape[1]), axis=1)).astype(out_dtype)
            elif gate_grad is not None:
                w_ref, g_ref, o_ref, dg_ref = rest
                o_ref[...] = (acc[...] * g_ref[...]).astype(out_dtype)
                dg_ref[...] += _colsum(w_ref[...].astype(F32) * acc[...])
            else:
                rest[0][...] = acc[...].astype(out_dtype)

    ins, in_specs = [a, b], [a_spec, b_spec]
    if second is not None:
        assert second[0].shape == a.shape and second[1].shape == b.shape
        ins += list(second)
        in_specs += [a_spec, b_spec]
    if a_scale is not None:
        assert mode != "tn"
        ins.append(a_scale)
        in_specs.append(pl.BlockSpec((1, tk), lambda i, j, k: (0, k)))
    out_shape, out_specs = jax.ShapeDtypeStruct((m, n), out_dtype), o_spec
    sem = ("parallel", "parallel", "arbitrary")
    if rowtab is not None:
        assert tn % rowtab.shape[1] == 0
        ins.append(rowtab)
        in_specs.append(pl.BlockSpec((tm, rowtab.shape[1]), lambda i, j, k: (i, 0)))
    if gated:
        ins += [resid, gate]
        in_specs += [o_spec, g_spec]
        out_shape = jax.ShapeDtypeStruct((m, n), F32)
    if gate_grad is not None:
        assert mode == "tn" and tn == n
        ins += list(gate_grad)
        in_specs += [o_spec, g_spec]
        out_shape = (out_shape, jax.ShapeDtypeStruct((1, n), F32))
        out_specs = (o_spec, g_spec)
        sem = ("arbitrary", "arbitrary", "arbitrary")
    return pl.pallas_call(
        body, name=name, grid=(m // tm, n // tn, nk), in_specs=in_specs, out_specs=out_specs, out_shape=out_shape,
        scratch_shapes=[pltpu.VMEM((tm, tn), F32)],
        compiler_params=_params(sem),
    )(*ins)


def _rowwise(name, fn, tiled, bcast, outs, sums=(), tr=512):
    tiled = [t if isinstance(t, tuple) else (t, t.shape[1], 0) for t in tiled]
    s = tiled[0][0].shape[0]
    tr = min(tr, s)
    assert s % tr == 0
    n_t, n_b, n_o = len(tiled), len(bcast), len(outs)

    def body(*refs):
        i = pl.program_id(0)
        vals = [r[...] for r in refs[:n_t + n_b]]
        o_vals, s_vals = fn(*vals)
        for r, v in zip(refs[n_t + n_b:n_t + n_b + n_o], o_vals):
            r[...] = v.astype(r.dtype)
        s_refs = refs[n_t + n_b + n_o:]

        @pl.when(i == 0)
        def _():
            for r in s_refs:
                r[...] = jnp.zeros_like(r)

        for r, v in zip(s_refs, s_vals):
            r[...] += v

    in_specs = [pl.BlockSpec((tr, n), functools.partial(lambda cb, i: (i, cb), cb)) for (_, n, cb) in tiled]
    in_specs += [pl.BlockSpec(b.shape, functools.partial(lambda nd, i: (0,) * nd, b.ndim)) for b in bcast]
    out_specs = [pl.BlockSpec((tr, n), lambda i: (i, 0)) for (n, _) in outs]
    out_specs += [pl.BlockSpec((1, n), lambda i: (0, 0)) for n in sums]
    out_shape = [jax.ShapeDtypeStruct((s, n), dt) for (n, dt) in outs]
    out_shape += [jax.ShapeDtypeStruct((1, n), F32) for n in sums]
    res = pl.pallas_call(
        body, name=name, grid=(s // tr,), in_specs=in_specs, out_specs=tuple(out_specs), out_shape=tuple(out_shape),
        compiler_params=_params(("arbitrary",)),
    )(*[t[0] for t in tiled], *bcast)
    return res


def _colsum(v):
    return jnp.sum(v, axis=0, keepdims=True)


def _rms_fwd(name, x, g, scale=None, shift=None, out_dtype=BF16, ncols=None):
    mod = scale is not None

    def fn(xv, gv, *ss):
        y = xv * lax.rsqrt(jnp.mean(xv * xv, axis=-1, keepdims=True) + EPS) * gv
        if mod:
            y = y * (1.0 + ss[0]) + ss[1]
        return (y,), ()

    n = ncols or x.shape[1]
    return _rowwise(name, fn, [(x, n, 0)], [g] + ([scale, shift] if mod else []), [(n, out_dtype)])[0]


def _rms_bwd(name, x, g, dh, scale=None, dx_in=None, ncols=None, out_dtype=F32):
    mod = scale is not None
    has_in = dx_in is not None

    def fn(*vals):
        xv, dhv = vals[0], vals[1].astype(F32)
        rest = list(vals[2:])
        dxi = rest.pop(0) if has_in else None
        gv = rest.pop(0)
        rstd = lax.rsqrt(jnp.mean(xv * xv, axis=-1, keepdims=True) + EPS)
        xhat = xv * rstd
        sums = []
        if mod:
            sc = rest.pop(0)
            dyn = dhv * (1.0 + sc)
            dshift, dscale = _colsum(dhv), _colsum(dhv * (xhat * gv))
        else:
            dyn = dhv
        dg = _colsum(dyn * xhat)
        dxhat = dyn * gv
        dx = rstd * (dxhat - xhat * jnp.mean(dxhat * xhat, axis=-1, keepdims=True))
        if has_in:
            dx = dx + dxi
        sums = [dg] + ([dshift, dscale] if mod else [])
        return (dx,), sums

    n = ncols or x.shape[1]
    tiled = [(x, n, 0), dh] + ([dx_in] if has_in else [])
    return _rowwise(name, fn, tiled, [g] + ([scale] if mod else []), [(n, out_dtype)], [n] * (3 if mod else 1))


def _loss_head(name, x, g, target):
    n = x.shape[1]

    def fn(xv, tv, gv):
        rstd = lax.rsqrt(jnp.mean(xv * xv, axis=-1, keepdims=True) + EPS)
        xhat = xv * rstd
        err = xhat * gv - tv
        loss = 0.5 * jnp.sum(jnp.sum(err * err, axis=-1, keepdims=True) / n, axis=0, keepdims=True)
        dy = err / n
        dg = _colsum(dy * xhat)
        dxhat = dy * gv
        dx = rstd * (dxhat - xhat * jnp.mean(dxhat * xhat, axis=-1, keepdims=True))
        return (dx,), (dg, jnp.broadcast_to(loss, (1, LANES)))

    return _rowwise(name, fn, [x, target], [g], [(n, F32)], [n, LANES])


def _krope_fwd(name, kv_ext, tabk):
    def fn(xv, tv):
        t = xv * tv
        return (t + pltpu.roll(t, 64, 1),), ()

    return _rowwise(name, fn, [(kv_ext, LANES, 2), tabk], [], [(LANES, BF16)])[0]


def _krope_bwd(name, dkd, tabk):
    def fn(dv, tv):
        d = dv[:, :LANES]
        for h in range(1, N_HEADS):
            d = d + dv[:, h * LANES:(h + 1) * LANES]
        return ((d + pltpu.roll(d, 64, 1)) * tv,), ()

    return _rowwise(name, fn, [dkd, tabk], [], [(LANES, F32)])[0]


def _adamw(name, w, g, m, v):
    def fn(wv, gv, mv, vv):
        m2 = ADAM_B1 * mv + (1.0 - ADAM_B1) * gv
        v2 = ADAM_B2 * vv + (1.0 - ADAM_B2) * (gv * gv)
        m_hat = m2 / (1.0 - ADAM_B1 ** ADAM_STEP)
        v_hat = v2 / (1.0 - ADAM_B2 ** ADAM_STEP)
        delta = -ADAM_LR * (m_hat / (jnp.sqrt(v_hat) + ADAM_EPS) + ADAM_WD * wv)
        return (delta, m2, v2), ()

    r, c = w.shape
    tr = r
    for cand in (512, 256, 128, 64, 32, 16, 8):
        if r % cand == 0 and r > cand:
            tr = cand
            break
    return _rowwise(name, fn, [w, g, m, v], [], [(c, F32)] * 3, tr=tr)


def _sum8(name, parts):
    _, r, c = parts.shape
    tr = r
    for cand in (2048, 1024, 512, 256, 128, 64, 32, 16):
        if r % cand == 0 and r > cand and cand * c <= 256 * 1024:
            tr = cand
            break

    def body(p_ref, o_ref):
        acc = p_ref[0].astype(F32)
        for k in range(1, N_DEV):
            acc = acc + p_ref[k].astype(F32)
        o_ref[...] = acc

    return pl.pallas_call(
        body, name=name, grid=(r // tr,), in_specs=[pl.BlockSpec((N_DEV, tr, c), lambda i: (0, i, 0))],
        out_specs=pl.BlockSpec((tr, c), lambda i: (i, 0)), out_shape=jax.ShapeDtypeStruct((r, c), F32),
        compiler_params=_params(("parallel",)),
    )(parts)


def _mods_fwd(name, c_all, w, b):
    depth, d, n = w.shape

    def body(c_ref, w_ref, b_ref, o_ref):
        cv = c_ref[...]
        sc = (cv * (1.0 / (1.0 + jnp.exp(-cv)))).astype(BF16)
        o_ref[0] = _dot(sc, w_ref[0].astype(BF16), NN) + b_ref[0]

    return pl.pallas_call(
        body, name=name, grid=(depth,),
        in_specs=[pl.BlockSpec(c_all.shape, lambda l: (0, 0)), pl.BlockSpec((1, d, n), lambda l: (l, 0, 0)),
                  pl.BlockSpec((1, 1, n), lambda l: (l, 0, 0))],
        out_specs=pl.BlockSpec((1, c_all.shape[0], n), lambda l: (l, 0, 0)),
        out_shape=jax.ShapeDtypeStruct((depth, c_all.shape[0], n), F32),
        compiler_params=_params(("parallel",)),
    )(c_all, w, b.reshape(depth, 1, n))


def _mods_bwd(name, c_all, dm):
    depth, rows, n = dm.shape
    d = c_all.shape[1]

    def body(c_ref, dm_ref, o_ref):
        cv = c_ref[...]
        sc = (cv * (1.0 / (1.0 + jnp.exp(-cv)))).astype(BF16)
        o_ref[0] = _dot(sc, dm_ref[0].astype(BF16), TN)

    return pl.pallas_call(
        body, name=name, grid=(depth,),
        in_specs=[pl.BlockSpec(c_all.shape, lambda l: (0, 0)), pl.BlockSpec((1, rows, n), lambda l: (l, 0, 0))],
        out_specs=pl.BlockSpec((1, d, n), lambda l: (l, 0, 0)),
        out_shape=jax.ShapeDtypeStruct((depth, d, n), F32),
        compiler_params=_params(("parallel",)),
    )(c_all, dm)


POOL_TILE = 256


def _split_dot(band, val):
    hi = val.astype(BF16)
    lo = (val - hi.astype(F32)).astype(BF16)
    return _dot(band, hi, NN) + _dot(band, lo, NN)


def _pool_fwd(name, h1, x, pw, pb, ps, g1):
    s, d = h1.shape
    t = POOL_TILE

    def body(hc_ref, hp_ref, x_ref, pw_ref, pb_ref, ps_ref, g_ref, xo_ref, zb_ref, pooled_ref):
        i = pl.program_id(0)
        r = lax.broadcasted_iota(jnp.int32, (t, t), 0)
        j = lax.broadcasted_iota(jnp.int32, (t, t), 1)
        pos = (i * t + lax.broadcasted_iota(jnp.int32, (t, 1), 0) + 1).astype(F32)
        has_prev = (i > 0).astype(F32)
        for grp, w in enumerate(POOL_WINDOWS):
            cs = slice(grp * POOL_GROUP, (grp + 1) * POOL_GROUP)
            hc = hc_ref[:, cs]
            band_cur = ((r - j >= 0) & (r - j < w)).astype(BF16)
            band_prev = (r + t - j < w).astype(BF16)
            ssum = _split_dot(band_cur, hc) + has_prev * _split_dot(band_prev, hp_ref[:, cs])
            pooled = (ssum / jnp.minimum(pos, float(w)) - hc).astype(BF16)
            zb = _dot(pooled, pw_ref[grp], NN) + pb_ref[:, cs]
            xo_ref[:, cs] = x_ref[:, cs] + g_ref[:, cs] * (zb * ps_ref[:, cs])
            zb_ref[:, cs] = zb
            pooled_ref[:, cs] = pooled

    row = pl.BlockSpec((t, d), lambda i: (i, 0))
    vec = pl.BlockSpec((1, d), lambda i: (0, 0))
    return pl.pallas_call(
        body, name=name, grid=(s // t,),
        in_specs=[row, pl.BlockSpec((t, d), lambda i: (jnp.maximum(i - 1, 0), 0)), row,
                  pl.BlockSpec(pw.shape, lambda i: (0, 0, 0)), vec, vec, vec],
        out_specs=(row, row, row),
        out_shape=(jax.ShapeDtypeStruct((s, d), F32), jax.ShapeDtypeStruct((s, d), F32), jax.ShapeDtypeStruct((s, d), BF16)),
        compiler_params=_params(("parallel",)),
    )(h1, h1, x, pw, pb, ps, g1)


def _pool_bwd(name, dxn, zb, pooled, pw, ps, g1):
    s, d = dxn.shape
    t = POOL_TILE
    nt = s // t

    def body(dc_ref, dn_ref, zb_ref, pooled_ref, pw_ref, ps_ref, g_ref, dh_ref, dpw_ref, dpb_ref, dps_ref, dg_ref):
        i = pl.program_id(0)

        @pl.when(i == 0)
        def _():
            dpw_ref[...] = jnp.zeros_like(dpw_ref)
            dpb_ref[...] = jnp.zeros_like(dpb_ref)
            dps_ref[...] = jnp.zeros_like(dps_ref)
            dg_ref[...] = jnp.zeros_like(dg_ref)

        jj = lax.broadcasted_iota(jnp.int32, (t, t), 0)
        rr = lax.broadcasted_iota(jnp.int32, (t, t), 1)
        pos = (i * t + lax.broadcasted_iota(jnp.int32, (t, 1), 0) + 1).astype(F32)
        has_next = (i < nt - 1).astype(F32)
        for grp, w in enumerate(POOL_WINDOWS):
            cs = slice(grp * POOL_GROUP, (grp + 1) * POOL_GROUP)
            gv, psv, zbv, dxc = g_ref[:, cs], ps_ref[:, cs], zb_ref[:, cs], dc_ref[:, cs]
            dg_ref[:, cs] += _colsum(dxc * (zbv * psv))
            dy = gv * dxc
            dps_ref[:, cs] += _colsum(dy * zbv)
            dz = dy * psv
            dpb_ref[:, cs] += _colsum(dz)
            dzb = dz.astype(BF16)
            dpw_ref[grp] += _dot(pooled_ref[:, cs], dzb, TN)
            dp = _dot(dzb, pw_ref[grp], NT)
            dzn = (gv * dn_ref[:, cs] * psv).astype(BF16)
            dpn = _dot(dzn, pw_ref[grp], NT) * (has_next / float(w))
            band_cur = ((rr - jj >= 0) & (rr - jj < w)).astype(BF16)
            band_next = (rr + t - jj < w).astype(BF16)
            dh_ref[:, cs] = _split_dot(band_cur, dp / jnp.minimum(pos, float(w))) + _split_dot(band_next, dpn) - dp

    row = pl.BlockSpec((t, d), lambda i: (i, 0))
    vec = pl.BlockSpec((1, d), lambda i: (0, 0))
    wspec = pl.BlockSpec(pw.shape, lambda i: (0, 0, 0))
    return pl.pallas_call(
        body, name=name, grid=(nt,),
        in_specs=[row, pl.BlockSpec((t, d), lambda i: (jnp.minimum(i + 1, nt - 1), 0)), row, row, wspec, vec, vec],
        out_specs=(row, wspec, vec, vec, vec),
        out_shape=(jax.ShapeDtypeStruct((s, d), F32), jax.ShapeDtypeStruct(pw.shape, F32),
                   jax.ShapeDtypeStruct((1, d), F32), jax.ShapeDtypeStruct((1, d), F32), jax.ShapeDtypeStruct((1, d), F32)),
        compiler_params=_params(("arbitrary",)),
    )(dxn, dxn, zb, pooled, pw, ps, g1)


GLU_TILE = 512
HALO = 16
INV_SQRT2 = 0.7071067811865476
INV_SQRT_2PI = 0.3989422804014327


def _up_glu_fwd(name, h2, wa, wv, cw, cb):
    s, d = h2.shape
    f = wa.shape[1]
    tm, tn = _tile(s, 1024), _tile(f, 1408)

    def body(h_ref, hh_ref, wa_ref, wv_ref, cw_ref, cb_ref, ua_ref, gl_ref, gpv_ref, ge_ref):
        i = pl.program_id(1)
        has_prev = (i > 0).astype(F32)
        a = _dot(h_ref[...], wa_ref[...], NN).astype(BF16)
        v = _dot(h_ref[...], wv_ref[...], NN)
        above = (_dot(hh_ref[...], wa_ref[...], NN) * has_prev).astype(BF16)
        ua_ref[...] = a
        ext = jnp.concatenate([above.astype(F32), a.astype(F32)], axis=0)
        e1 = pltpu.roll(ext, 1, 0)[HALO:]
        e2 = pltpu.roll(ext, 2, 0)[HALO:]
        pre = e2 * cw_ref[0:1, :] + e1 * cw_ref[1:2, :] + ext[HALO:] * cw_ref[2:3, :] + cb_ref[...]
        cdf = 0.5 * (1.0 + lax.erf(pre * INV_SQRT2))
        ge = pre * cdf
        gl_ref[...] = (ge * v).astype(gl_ref.dtype)
        gpv_ref[...] = ((cdf + pre * (INV_SQRT_2PI * jnp.exp(-0.5 * pre * pre))) * v).astype(gpv_ref.dtype)
        ge_ref[...] = ge.astype(ge_ref.dtype)

    blk = pl.BlockSpec((tm, tn), lambda j, i: (i, j))
    wspec = pl.BlockSpec((d, tn), lambda j, i: (0, j))
    return pl.pallas_call(
        body, name=name, grid=(f // tn, s // tm),
        in_specs=[pl.BlockSpec((tm, d), lambda j, i: (i, 0)), pl.BlockSpec((HALO, d), lambda j, i: (jnp.maximum(i * (tm // HALO) - 1, 0), 0)),
                  wspec, wspec, pl.BlockSpec((3, tn), lambda j, i: (0, j)), pl.BlockSpec((1, tn), lambda j, i: (0, j))],
        out_specs=(blk, blk, blk, blk), out_shape=tuple(jax.ShapeDtypeStruct((s, f), BF16) for _ in range(4)),
        compiler_params=_params(("parallel", "parallel")),
    )(h2, h2, wa, wv, cw, cb)


def _down_glu_bwd(name, dx, gate, wd, ua, gpv, ge, cw):
    s, f = ua.shape
    d = dx.shape[1]
    t, tf = min(GLU_TILE, s), _tile(f, 1408)
    nt = s // t
    te = t + HALO

    def body(dy_ref, dyn_ref, gate_ref, wd_ref, a_ref, ah_ref, g_ref, gn_ref, ge_ref, cw_ref, da_ref, dv_ref, dcw_ref, dcb_ref):
        i = pl.program_id(1)

        @pl.when(i == 0)
        def _():
            dcw_ref[...] = jnp.zeros_like(dcw_ref)
            dcb_ref[...] = jnp.zeros_like(dcb_ref)

        has_prev = (i > 0).astype(F32)
        has_next = (i < nt - 1).astype(F32)
        wdv = wd_ref[...]
        dgl = _dot((dy_ref[...] * gate_ref[...]).astype(BF16), wdv, NT)
        dgl_below = _dot((dyn_ref[...] * gate_ref[...]).astype(BF16), wdv, NT) * has_next
        dpre = jnp.concatenate([dgl * g_ref[...].astype(F32), dgl_below * gn_ref[...].astype(F32)], axis=0)
        c0, c1, c2 = cw_ref[0:1, :], cw_ref[1:2, :], cw_ref[2:3, :]
        up1 = pltpu.roll(dpre, te - 1, 0)
        up2 = pltpu.roll(dpre, te - 2, 0)
        da_ref[...] = (dpre * c2 + up1 * c1 + up2 * c0)[:t].astype(da_ref.dtype)
        dv_ref[...] = (dgl * ge_ref[...].astype(F32)).astype(dv_ref.dtype)
        ext = jnp.concatenate([ah_ref[...].astype(F32) * has_prev, a_ref[...].astype(F32)], axis=0)
        dpt = dpre[:t]
        dcb_ref[...] += _colsum(dpt)
        dcw_ref[0:1, :] += _colsum(pltpu.roll(ext, 2, 0)[HALO:] * dpt)
        dcw_ref[1:2, :] += _colsum(pltpu.roll(ext, 1, 0)[HALO:] * dpt)
        dcw_ref[2:3, :] += _colsum(ext[HALO:] * dpt)

    blk = pl.BlockSpec((t, tf), lambda j, i: (i, j))
    prev = pl.BlockSpec((HALO, tf), lambda j, i: (jnp.maximum(i * (t // HALO) - 1, 0), j))
    below = lambda i: jnp.minimum((i + 1) * (t // HALO), s // HALO - 1)
    w3 = pl.BlockSpec((3, tf), lambda j, i: (0, j))
    w1 = pl.BlockSpec((1, tf), lambda j, i: (0, j))
    return pl.pallas_call(
        body, name=name, grid=(f // tf, nt),
        in_specs=[pl.BlockSpec((t, d), lambda j, i: (i, 0)), pl.BlockSpec((HALO, d), lambda j, i: (below(i), 0)),
                  pl.BlockSpec((1, d), lambda j, i: (0, 0)), pl.BlockSpec((tf, d), lambda j, i: (j, 0)), blk, prev, blk,
                  pl.BlockSpec((HALO, tf), lambda j, i: (below(i), j)), blk, w3],
        out_specs=(blk, blk, w3, w1),
        out_shape=(jax.ShapeDtypeStruct((s, f), BF16), jax.ShapeDtypeStruct((s, f), BF16),
                   jax.ShapeDtypeStruct((3, f), F32), jax.ShapeDtypeStruct((1, f), F32)),
        compiler_params=_params(("parallel", "arbitrary")),
    )(dx, dx, gate, wd, ua, ua, gpv, gpv, ge, cw)


ATT_TILE = 512
ATT_ROWS = 256
ATT_HEADS = 4
ATT_BWD_HEADS = 2
ATT_BWD_VMEM_BYTES = 58 * 1024 * 1024
LOG2E = 1.4426950408889634
LN2 = 0.6931471805599453


def _head_blocks_t(a, width):
    s = a.shape[0]
    t = min(ATT_TILE, s)
    return a.reshape(s // t, t, N_HEADS, width).transpose(2, 0, 3, 1)


def _causal_mask(sv, q0, k0):
    row = q0 + lax.broadcasted_iota(jnp.int32, sv.shape, 0)
    col = k0 + lax.broadcasted_iota(jnp.int32, sv.shape, 1)
    return jnp.where(col <= row, sv, NEG_BIG)


def _attn_fwd(name, q_rot, kt4, v_ext):
    s = q_rot.shape[0]
    t = min(ATT_TILE, s)
    nq = s // t
    rq = min(ATT_ROWS, t)
    nh = ATT_HEADS

    def body(q_ref, kt_ref, v_ref, o_ref, row_ref, acc_ref, m_ref):
        qi = pl.program_id(1)
        acc_ref[...] = jnp.zeros_like(acc_ref)
        m_ref[...] = jnp.full_like(m_ref, NEG_BIG)

        def step(j, masked):
            for hh in range(nh):
                cols = slice(hh * Q_EXT, (hh + 1) * Q_EXT)
                v_blk = v_ref[pl.ds(pl.multiple_of(j * t, t), t), cols]
                for r in range(t // rq):
                    rs = pl.ds(r * rq, rq)
                    sv = _dot(q_ref[rs, cols], kt_ref[hh, j], NN)
                    if masked:
                        sv = _causal_mask(sv, r * rq, 0)
                    m_prev = m_ref[hh, rs, :]
                    m_new = jnp.maximum(m_prev, jnp.max(sv, axis=-1, keepdims=True))
                    p = jnp.exp2(sv - m_new).astype(BF16)
                    acc_ref[hh, rs, :] = jnp.exp2(m_prev - m_new) * acc_ref[hh, rs, :] + _dot(p, v_blk, NN)
                    m_ref[hh, rs, :] = m_new

        def full_step(j, carry):
            step(j, False)
            return carry

        lax.fori_loop(0, qi, full_step, 0)
        step(qi, True)
        for hh in range(nh):
            l = acc_ref[hh, :, V_HEAD:V_HEAD + 1]
            o_ref[:, hh * V_HEAD:(hh + 1) * V_HEAD] = (acc_ref[hh, :, :V_HEAD] / l).astype(o_ref.dtype)
            lse = jnp.broadcast_to(m_ref[hh] + jnp.log(l) * LOG2E, (t, LANES))
            row_ref[hh, 0] = jnp.transpose(lse)[0:8, :]

    return pl.pallas_call(
        body, name=name, grid=(N_HEADS // nh, nq),
        in_specs=[pl.BlockSpec((t, nh * Q_EXT), lambda h, i: (i, h)), pl.BlockSpec((nh, nq, Q_EXT, t), lambda h, i: (h, 0, 0, 0)),
                  pl.BlockSpec((s, nh * Q_EXT), lambda h, i: (0, h))],
        out_specs=(pl.BlockSpec((t, nh * V_HEAD), lambda h, i: (i, h)), pl.BlockSpec((nh, 1, 8, t), lambda h, i: (h, i, 0, 0))),
        out_shape=(jax.ShapeDtypeStruct((s, N_HEADS * V_HEAD), BF16), jax.ShapeDtypeStruct((N_HEADS, nq, 8, t), F32)),
        scratch_shapes=[pltpu.VMEM((nh, t, Q_EXT), F32), pltpu.VMEM((nh, t, 1), F32)],
        compiler_params=_params(("parallel", "parallel")),
    )(q_rot, kt4, v_ext)


def _attn_delta(name, o, do):
    s = o.shape[0]
    t = min(ATT_TILE, s)

    def body(o_ref, do_ref, delta_ref):
        prod = do_ref[...].astype(F32) * o_ref[...].astype(F32)
        for h in range(N_HEADS):
            delta = jnp.sum(prod[:, h * V_HEAD:(h + 1) * V_HEAD], axis=-1, keepdims=True)
            delta_ref[h, 0] = jnp.transpose(jnp.broadcast_to(delta, (t, LANES)))[0:8, :]

    rows = pl.BlockSpec((t, N_HEADS * V_HEAD), lambda i: (i, 0))
    return pl.pallas_call(
        body, name=name, grid=(s // t,), in_specs=[rows, rows],
        out_specs=pl.BlockSpec((N_HEADS, 1, 8, t), lambda i: (0, i, 0, 0)),
        out_shape=jax.ShapeDtypeStruct((N_HEADS, s // t, 8, t), F32),
        compiler_params=_params(("parallel",)),
    )(o, do)


def _attn_bwd(name, kfull, v, qt4, q_rot, dot4, do, lse_row, delta_row, tabq, acc_in=None):
    s = kfull.shape[0]
    t = min(ATT_TILE, s)
    nq = s // t
    nh = ATT_BWD_HEADS
    has_in = acc_in is not None

    def body(*refs):
        k_ref, v_ref, qt_ref, q_ref, dot_ref, do_ref, lse_ref, delta_ref, tab_ref = refs[:9]
        dq_ref, dkn_ref, dkd_ref, dv_ref, dq_acc_ref, acck_ref, accv_ref = refs[-7:]
        kj = pl.program_id(1)

        @pl.when(kj == 0)
        def _():
            dq_acc_ref[...] = jnp.zeros_like(dq_acc_ref)

        acck_ref[...] = jnp.zeros_like(acck_ref)
        accv_ref[...] = jnp.zeros_like(accv_ref)

        def step(i, masked):
            qs = pl.ds(pl.multiple_of(i * t, t), t)
            for hh in range(nh):
                qc = slice(hh * Q_EXT, (hh + 1) * Q_EXT)
                vc = slice(hh * LANES, (hh + 1) * LANES)
                k_blk = k_ref[:, qc]
                st = _dot(k_blk, qt_ref[hh, i], NN)
                if masked:
                    krow = lax.broadcasted_iota(jnp.int32, st.shape, 0)
                    qcol = lax.broadcasted_iota(jnp.int32, st.shape, 1)
                    st = jnp.where(krow <= qcol, st, NEG_BIG)
                pt = jnp.exp2(st - lse_ref[hh, i, 0:1, :])
                accv_ref[hh] += _dot(pt.astype(BF16), do_ref[qs, vc], NN)
                dpt = _dot(v_ref[:, vc], dot_ref[hh, i], NN)
                dst = (pt * (dpt - delta_ref[hh, i, 0:1, :])).astype(BF16)
                acck_ref[hh] += _dot(dst, q_ref[qs, qc], NN)
                dq_acc_ref[hh, qs, :] += _dot(dst, k_blk, TN)

        def full_step(i, carry):
            step(i, False)
            return carry

        step(kj, True)
        lax.fori_loop(kj + 1, nq, full_step, 0)
        for hh in range(nh):
            vc = slice(hh * LANES, (hh + 1) * LANES)
            dk = acck_ref[hh] * LN2
            dkn, dkd, dv = dk[:, :QK_NOPE], dk[:, QK_NOPE:], accv_ref[hh]
            if has_in:
                dkn, dkd, dv = dkn + refs[9][:, vc], dkd + refs[10][:, vc], dv + refs[11][:, vc]
            dkn_ref[:, vc], dkd_ref[:, vc], dv_ref[:, vc] = dkn, dkd, dv

        @pl.when(kj == nq - 1)
        def _():
            for hh in range(nh):
                dq_ref[:, hh * Q_EXT:(hh + 1) * Q_EXT] = (dq_acc_ref[hh] * (tab_ref[...] * LN2)).astype(dq_ref.dtype)

    kblk = pl.BlockSpec((t, nh * LANES), lambda h, j: (j, h))
    col = pl.BlockSpec((s, nh * LANES), lambda h, j: (0, h))
    q_all = pl.BlockSpec((s, nh * Q_EXT), lambda h, j: (0, h))
    stat = pl.BlockSpec((nh, nq, 8, t), lambda h, j: (h, 0, 0, 0))
    ins = [kfull, v, qt4, q_rot, dot4, do, lse_row, delta_row, tabq]
    in_specs = [pl.BlockSpec((t, nh * Q_EXT), lambda h, j: (j, h)), kblk, pl.BlockSpec((nh, nq, Q_EXT, t), lambda h, j: (h, 0, 0, 0)),
                q_all, pl.BlockSpec((nh, nq, V_HEAD, t), lambda h, j: (h, 0, 0, 0)), col, stat, stat,
                pl.BlockSpec((s, Q_EXT), lambda h, j: (0, 0))]
    if has_in:
        ins += list(acc_in)
        in_specs += [kblk, kblk, kblk]
    wide = jax.ShapeDtypeStruct((s, N_HEADS * LANES), F32)
    return pl.pallas_call(
        body, name=name, grid=(N_HEADS // nh, nq), in_specs=in_specs, out_specs=(q_all, kblk, kblk, kblk),
        out_shape=(jax.ShapeDtypeStruct((s, N_HEADS * Q_EXT), BF16), wide, wide, wide),
        scratch_shapes=[pltpu.VMEM((nh, s, Q_EXT), F32), pltpu.VMEM((nh, t, Q_EXT), F32), pltpu.VMEM((nh, t, LANES), F32)],
        compiler_params=pltpu.CompilerParams(dimension_semantics=("parallel", "arbitrary"), vmem_limit_bytes=ATT_BWD_VMEM_BYTES),
    )(*ins)


def _swap_halves(w):
    half = w.shape[-1] // 2
    return jnp.concatenate([-w[..., half:], w[..., :half]], axis=-1)


def _unswap_halves(g):
    half = g.shape[-1] // 2
    return jnp.concatenate([g[..., half:], -g[..., :half]], axis=-1)


def _extend_w_dkv(w):
    return jnp.concatenate([w, _swap_halves(w[:, KV_RANK:])], axis=-1)


def _fold_w_dkv_grad(g):
    rope = g[:, KV_RANK:KV_RANK + QK_ROPE] + _unswap_halves(g[:, KV_RANK + QK_ROPE:])
    return jnp.concatenate([g[:, :KV_RANK], rope], axis=-1)


def _rope_tables(positions):
    inv = 1.0 / (ROPE_THETA ** (jnp.arange(0, QK_ROPE, 2, dtype=F32) / QK_ROPE))
    ang = positions.astype(F32)[:, None] * inv
    cos, sin = jnp.cos(ang), jnp.sin(ang)
    tabk = jnp.concatenate([cos, cos, sin, sin], axis=-1)
    scale = QK_HEAD ** -0.5 * LOG2E
    tabq = jnp.concatenate([jnp.full((positions.shape[0], QK_NOPE), scale, F32), tabk * scale], axis=-1)
    return tabq, tabk


def _forward_backward(x, target, mods, tabq, tabk, final_g, fetch, push):
    row = lambda vec: vec.reshape(1, -1)
    mod = [[row(mods[l, k * D_MODEL:(k + 1) * D_MODEL]) for k in range(N_MOD)] for l in range(DEPTH)]
    saved, weights = [], []
    kv = None
    for l in range(DEPTH):
        w, tok = fetch(l, x)
        sh1, sc1, g1, sh2, sc2, g2 = mod[l]
        sh1 = sh1 + tok
        if l == N_A_LAYERS:
            kvn = _rms_fwd("kvin_fwd", x, row(w["kv_in_g"]))
            kv_ext = _mm("dkv_fwd", kvn, w["w_dkv_ext"], out_dtype=F32)
            ckv = _rms_fwd("ckv_fwd", kv_ext, row(w["ckv_norm_g"]), ncols=KV_RANK)
            kd = _krope_fwd("krope_fwd", kv_ext, tabk)
            kn, v = _mm("uk_fwd", ckv, w["w_uk"]), _mm("uv_fwd", ckv, w["w_uv"])
            heads = lambda a: [a[:, h * LANES:(h + 1) * LANES] for h in range(N_HEADS)]
            kfull = jnp.concatenate([part for kh in heads(kn) for part in (kh, kd)], axis=-1)
            v_ext = jnp.concatenate([part for vh in heads(v) for part in (vh, jnp.ones_like(vh))], axis=-1)
            kv = dict(x=x, kvn=kvn, kv_ext=kv_ext, ckv=ckv, v=v, kfull=kfull, v_ext=v_ext,
                      kt4=_head_blocks_t(kfull, Q_EXT))
        x_in = x
        if l < N_A_LAYERS:
            h1 = _rms_fwd(f"norm1_fwd_{l}", x, row(w["norm1_g"]), sc1, sh1, out_dtype=F32)
            x_mid, zb, pooled = _pool_fwd(f"pool_fwd_{l}", h1, x, w["pool_w"], row(w["pool_b"]), row(w["pool_scale"]), g1)
            mix = (zb, pooled)
        else:
            h1 = _rms_fwd(f"norm1_fwd_{l}", x, row(w["norm1_g"]), sc1, sh1)
            cq_pre = _mm(f"dq_fwd_{l}", h1, w["w_dq"], out_dtype=F32)
            cq = _rms_fwd(f"qnorm_fwd_{l}", cq_pre, row(w["q_norm_g"]))
            q_rot = _mm(f"uq_fwd_{l}", cq, w["w_uq_ext"], rowtab=tabq)
            o, lse_row = _attn_fwd(f"attn_fwd_{l}", q_rot, kv["kt4"], kv["v_ext"])
            x_mid = _mm(f"wo_fwd_{l}", o, w["w_o"], resid=x, gate=g1)
            mix = (h1, cq_pre, cq, q_rot, o, lse_row)
        h2 = _rms_fwd(f"norm2_fwd_{l}", x_mid, row(w["norm2_g"]), sc2, sh2)
        w_up_a, w_up_v = w["w_up"](h2)
        ua, gl, gpv, ge = _up_glu_fwd(f"up_glu_fwd_{l}", h2, w_up_a, w_up_v, w["conv_w"], row(w["conv_b"]))
        w_down = w["w_down"](gl)
        x = _mm(f"down_fwd_{l}", gl, w_down, resid=x_mid, gate=g2)
        saved.append((x_in, x_mid, h2, ua, gpv, ge, gl, mix))
        weights.append(dict(w, w_up_a=w_up_a, w_up_v=w_up_v, w_down=w_down))

    dx, dfinal_g, loss = _loss_head("loss_head", x, row(final_g), target)
    g = {"final_g": dfinal_g.reshape(-1)}
    per_layer = {k: [None] * DEPTH for k in ("norm1_g", "norm2_g", "conv_w", "conv_b")}
    per_a = {k: [None] * N_A_LAYERS for k in ("pool_b", "pool_scale")}
    per_b = {k: [None] * N_B_LAYERS for k in ("q_norm_g",)}
    dmods = [None] * DEPTH
    dkv = None
    tok = 0.0
    for l in reversed(range(DEPTH)):
        w, big = weights[l], {}
        sh1, sc1, g1, sh2, sc2, g2 = mod[l]
        g2 = g2 + tok
        x_in, x_mid, h2, ua, gpv, ge, gl, mix = saved[l]
        dw_down, dg2 = _mm(f"down_wgrad_{l}", gl, dx, mode="tn", tm_cap=1408, gate_grad=(w["w_down"], g2))
        tok = push(l, "down", dict(w_down=dw_down), None)
        da, dv_, dcw, dcb = _down_glu_bwd(f"down_glu_bwd_{l}", dx, g2, w["w_down"], ua, gpv, ge, w["conv_w"] + tok)
        dh2 = _mm(f"up_bwd_{l}", da, w["w_up_a"], mode="nt", out_dtype=F32, second=(dv_, w["w_up_v"]))
        tok = push(l, "up", dict(w_up_a=_mm(f"up_a_wgrad_{l}", h2, da, mode="tn"), w_up_v=_mm(f"up_v_wgrad_{l}", h2, dv_, mode="tn")), None)
        per_layer["conv_w"][l], per_layer["conv_b"][l] = dcw, dcb.reshape(-1)
        dx_mid, dn2, dsh2, dsc2 = _rms_bwd(f"norm2_bwd_{l}", x_mid, row(w["norm2_g"]), dh2, sc2 + tok, dx_in=dx)
        per_layer["norm2_g"][l] = dn2.reshape(-1)
        if l < N_A_LAYERS:
            zb, pooled = mix
            dh1, dpw, dpb, dps, dg1 = _pool_bwd(f"pool_bwd_{l}", dx_mid, zb, pooled, w["pool_w"], row(w["pool_scale"]), g1)
            big["pool_w"] = dpw
            per_a["pool_b"][l], per_a["pool_scale"][l] = dpb.reshape(-1), dps.reshape(-1)
        else:
            j = l - N_A_LAYERS
            h1, cq_pre, cq, q_rot, o, lse_row = mix
            do = _mm(f"wo_bwd_{l}", dx_mid, w["w_o"], mode="nt", a_scale=g1)
            big["w_o"], dg1 = _mm(f"wo_wgrad_{l}", o, dx_mid, mode="tn", gate_grad=(w["w_o"], g1))
            delta_row = _attn_delta(f"attn_delta_{l}", o, do)
            dq_ext, *dkv = _attn_bwd(f"attn_bwd_{l}", kv["kfull"], kv["v"], _head_blocks_t(q_rot, Q_EXT), q_rot, _head_blocks_t(do, V_HEAD), do,
                                     lse_row, delta_row, tabq, acc_in=dkv)
            dcq = _mm(f"uq_bwd_{l}", dq_ext, w["w_uq_ext"], mode="nt", out_dtype=F32)
            big["w_uq_ext"] = _mm(f"uq_wgrad_{l}", cq, dq_ext, mode="tn", out_dtype=F32)
            dcq_pre, dqn = _rms_bwd(f"qnorm_bwd_{l}", cq_pre, row(w["q_norm_g"]), dcq, out_dtype=BF16)
            per_b["q_norm_g"][j] = dqn.reshape(-1)
            dh1 = _mm(f"dq_bwd_{l}", dcq_pre, w["w_dq"], mode="nt")
            big["w_dq"] = _mm(f"dq_wgrad_{l}", h1, dcq_pre, mode="tn")
        dx, dn1, dsh1, dsc1 = _rms_bwd(f"norm1_bwd_{l}", x_in, row(w["norm1_g"]), dh1, sc1, dx_in=dx_mid)
        per_layer["norm1_g"][l] = dn1.reshape(-1)
        dmods[l] = jnp.concatenate([dsh1, dsc1, dg1, dsh2, dsc2, dg2], axis=-1).reshape(-1)
        if l == N_A_LAYERS:
            dkn, dkd, dv = dkv
            dckv = _mm("ukv_bwd", dkn, w["w_uk"], mode="nt", out_dtype=F32, second=(dv, w["w_uv"]))
            big["w_uk"] = _mm("uk_wgrad", kv["ckv"], dkn, mode="tn")
            big["w_uv"] = _mm("uv_wgrad", kv["ckv"], dv, mode="tn")
            dkr = _krope_bwd("krope_bwd", dkd, tabk)
            dc, dckv_g = _rms_bwd("ckv_bwd", kv["kv_ext"], row(w["ckv_norm_g"]), dckv, ncols=KV_RANK, out_dtype=BF16)
            dkv_ext = jnp.concatenate([dc, dkr.astype(BF16)], axis=-1)
            dkvn = _mm("dkv_bwd", dkv_ext, w["w_dkv_ext"], mode="nt")
            big["w_dkv_ext"] = _mm("dkv_wgrad", kv["kvn"], dkv_ext, mode="tn", out_dtype=F32)
            dx, dkv_in_g = _rms_bwd("kvin_bwd", kv["x"], row(w["kv_in_g"]), dkvn, dx_in=dx)
            g["ckv_norm_g"], g["kv_in_g"] = dckv_g.reshape(-1), dkv_in_g.reshape(-1)
        tok = push(l, "mix", big, dx)
    for group in (per_layer, per_a, per_b):
        for k, vals in group.items():
            g[k] = jnp.stack(vals)
    return loss, dx, g, jnp.stack(dmods)


def _my_index():
    return 4 * lax.axis_index("x") + 2 * lax.axis_index("y") + lax.axis_index("c")


def _peer(k):
    x, y, c = lax.axis_index("x"), lax.axis_index("y"), lax.axis_index("c")
    return (1 - x if k & 4 else x, 1 - y if k & 2 else y, 1 - c if k & 1 else c)


def _index_of(pos):
    return 4 * pos[0] + 2 * pos[1] + pos[2]


def _exchange_many(name, arrays, scatter):
    n = len(arrays)
    blocks = [tuple(a.shape[1:]) if scatter else tuple(a.shape) for a in arrays]

    def body(*refs):
        x_refs, o_refs = refs[:n], refs[n:2 * n]
        send_sems, recv_sems, local_sems = refs[2 * n:]
        me = _my_index()
        started = []
        for a in range(n):
            mine = pltpu.make_async_copy(x_refs[a].at[me] if scatter else x_refs[a], o_refs[a].at[me], local_sems.at[a])
            mine.start()
            started.append(mine)
        sends = []
        for k in range(1, N_DEV):
            peer = _peer(k)
            for a in range(n):
                cp = pltpu.make_async_remote_copy(
                    src_ref=x_refs[a].at[_index_of(peer)] if scatter else x_refs[a], dst_ref=o_refs[a].at[me],
                    send_sem=send_sems.at[a, k - 1], recv_sem=recv_sems.at[a, k - 1], device_id=peer, device_id_type=MESH)
                cp.start()
                sends.append(cp)
        for k in range(1, N_DEV):
            peer = _peer(k)
            for a in range(n):
                pltpu.make_async_remote_copy(
                    src_ref=x_refs[a].at[me] if scatter else x_refs[a], dst_ref=o_refs[a].at[_index_of(peer)],
                    send_sem=send_sems.at[a, k - 1], recv_sem=recv_sems.at[a, k - 1], device_id=peer, device_id_type=MESH).wait_recv()
        for cp in sends:
            cp.wait_send()
        for mine in started:
            mine.wait()

    return pl.pallas_call(
        body, name=name, out_shape=tuple(jax.ShapeDtypeStruct((N_DEV,) + blk, a.dtype) for blk, a in zip(blocks, arrays)),
        in_specs=[pl.BlockSpec(memory_space=pl.ANY)] * n, out_specs=tuple([pl.BlockSpec(memory_space=pl.ANY)] * n),
        scratch_shapes=[pltpu.SemaphoreType.DMA((n, N_DEV - 1)), pltpu.SemaphoreType.DMA((n, N_DEV - 1)), pltpu.SemaphoreType.DMA((n,))],
    )(*arrays)


def _exchange(name, x, scatter):
    return _exchange_many(name, [x], scatter)[0]


HBM_SPEC = pl.BlockSpec(memory_space=pltpu.HBM)
SEM_SPEC = pl.BlockSpec(memory_space=pltpu.SEMAPHORE)
DATAFLOW = pltpu.SideEffectType.DATAFLOW_SIDE_EFFECTING


def _remote_copies(x_refs, land_refs, send_sems, recv_sems, scatter, numbers=None):
    me = _my_index()
    numbers = list(range(len(x_refs))) if numbers is None else numbers
    out, inc = [], []
    for a in range(len(x_refs)):
        for k in range(1, N_DEV):
            peer = _peer(k)
            pair = numbers[a] * (N_DEV - 1) + k - 1
            sems = dict(send_sem=send_sems.at[pair], recv_sem=recv_sems.at[pair], device_id=peer, device_id_type=MESH)
            out.append(pltpu.make_async_remote_copy(
                src_ref=x_refs[a].at[_index_of(peer)] if scatter else x_refs[a], dst_ref=land_refs[a].at[me], **sems))
            inc.append(pltpu.make_async_remote_copy(
                src_ref=x_refs[a].at[me] if scatter else x_refs[a], dst_ref=land_refs[a].at[_index_of(peer)], **sems))
    return out, inc


def _exchange_start(name, arrays, scatter):
    n = len(arrays)
    blocks = [tuple(a.shape[1:]) if scatter else tuple(a.shape) for a in arrays]

    def body(*refs):
        x_refs, land_refs = refs[:n], refs[n:2 * n]
        send_sems, recv_sems = refs[2 * n], refs[2 * n + 1]
        for cp in _remote_copies(x_refs, land_refs, send_sems, recv_sems, scatter)[0]:
            cp.start()
        refs[-1][...] = jnp.zeros_like(refs[-1])

    sem_type = pltpu.SemaphoreType.DMA((n * (N_DEV - 1),))
    lands =[pltpu.with_memory_space_constraint(lax.empty((N_DEV,) + blk, a.dtype), pltpu.HBM) for blk, a in zip(blocks, arrays)]
    srcs = [pltpu.with_memory_space_constraint(a, pltpu.HBM) for a in arrays]
    res = pl.pallas_call(
        body, name=name,
        out_shape=(sem_type, sem_type, *[pltpu.HBM(a.shape, a.dtype) for a in srcs + lands], jax.ShapeDtypeStruct((8, LANES), F32)),
        in_specs=[HBM_SPEC] * (2 * n), out_specs=(SEM_SPEC, SEM_SPEC, *[HBM_SPEC] * (2 * n), pl.BlockSpec(memory_space=pltpu.VMEM)),
        input_output_aliases={i: 2 + i for i in range(2 * n)},
        compiler_params=pltpu.CompilerParams(has_side_effects=DATAFLOW),
    )(*srcs, *lands)
    return (res[0], res[1], list(res[2:2 + n]), list(res[2 + n:2 + 2 * n])), res[-1]


def _exchange_wait(name, handles, after, scatter, which=None):
    send_sems, recv_sems, srcs, lands = handles
    which = list(range(len(srcs))) if which is None else list(which)
    srcs, lands = [srcs[a] for a in which], [lands[a] for a in which]
    n = len(srcs)

    def body(*refs):
        x_refs, land_refs = refs[:n], refs[n:2 * n]
        out, inc = _remote_copies(x_refs, land_refs, refs[2 * n], refs[2 * n + 1], scatter, which)
        for cp in out:
            cp.wait_send()
        for cp in inc:
            cp.wait_recv()

    res = pl.pallas_call(
        body, name=name, out_shape=tuple(pltpu.HBM(a.shape, a.dtype) for a in srcs + lands),
        in_specs=[HBM_SPEC] * (2 * n) + [SEM_SPEC, SEM_SPEC, pl.BlockSpec(memory_space=pl.ANY)], out_specs=tuple([HBM_SPEC] * (2 * n)),
        input_output_aliases={i: i for i in range(2 * n)},
        compiler_params=pltpu.CompilerParams(has_side_effects=DATAFLOW),
    )(*srcs, *lands, send_sems, recv_sems, after)
    return list(res[n:])


def _pack(arrays, dtype, row_multiple):
    flat = jnp.concatenate([a.astype(dtype).reshape(-1) for a in arrays])
    rows = -(-flat.shape[0] // (LANES * row_multiple)) * row_multiple
    return jnp.pad(flat, (0, rows * LANES - flat.shape[0])).reshape(rows, LANES)


def _unpack(packed, shapes):
    lead = packed.shape[:-2]
    flat = packed.reshape(lead + (-1,))
    out, off = [], 0
    for shp in shapes:
        size = 1
        for d in shp:
            size *= d
        out.append(flat[..., off:off + size].reshape(lead + tuple(shp)))
        off += size
    return out


def _unshard(g8, axis):
    return jnp.concatenate([g8[j] for j in range(N_DEV)], axis=axis)


def _shard8(full, axis):
    n = full.shape[axis] // N_DEV
    return jnp.stack([lax.slice_in_dim(full, j * n, (j + 1) * n, axis=axis) for j in range(N_DEV)])


VECTOR_WEIGHTS = (("pool_b", 1), ("pool_scale", 1), ("conv_w", 2))
REPLICATED_WEIGHTS = ("norm1_g", "norm2_g", "kv_in_g", "ckv_norm_g", "q_norm_g", "conv_b", "final_g")
WEIGHT_ORDER = ("mod_w", "mod_b", "norm1_g", "norm2_g", "pool_w", "pool_b", "pool_scale", "kv_in_g", "w_dkv", "ckv_norm_g", "w_uk",
                "w_uv", "w_dq", "q_norm_g", "w_uq", "w_o", "w_up", "conv_w", "conv_b", "w_down", "final_g")
SMALL_ROW_MULTIPLE = 16


def _as_2d(a):
    if a.ndim == 1:
        return a.reshape(-1, LANES)
    return a.reshape(-1, a.shape[-1])


def kernel(x, c, positions, mod_w, mod_b, norm1_g, norm2_g, pool_w, pool_b, pool_scale, kv_in_g, w_dkv, ckv_norm_g, w_uk, w_uv, w_dq, q_norm_g, w_uq, w_o, w_up, conv_w, conv_b, w_down, final_g, loss_target, m_mod_w, m_mod_b, m_norm1_g, m_norm2_g, m_pool_w, m_pool_b, m_pool_scale, m_kv_in_g, m_w_dkv, m_ckv_norm_g, m_w_uk, m_w_uv, m_w_dq, m_q_norm_g, m_w_uq, m_w_o, m_w_up, m_conv_w, m_conv_b, m_w_down, m_final_g, v_mod_w, v_mod_b, v_norm1_g, v_norm2_g, v_pool_w, v_pool_b, v_pool_scale, v_kv_in_g, v_w_dkv, v_ckv_norm_g, v_w_uk, v_w_uv, v_w_dq, v_q_norm_g, v_w_uq, v_w_o, v_w_up, v_conv_w, v_conv_b, v_w_down, v_final_g):
    shard = dict(mod_w=mod_w, mod_b=mod_b, norm1_g=norm1_g, norm2_g=norm2_g, pool_w=pool_w, pool_b=pool_b, pool_scale=pool_scale,
                 kv_in_g=kv_in_g, w_dkv=w_dkv, ckv_norm_g=ckv_norm_g, w_uk=w_uk, w_uv=w_uv, w_dq=w_dq, q_norm_g=q_norm_g, w_uq=w_uq,
                 w_o=w_o, w_up=w_up, conv_w=conv_w, conv_b=conv_b, w_down=w_down, final_g=final_g)
    mom_m = dict(mod_w=m_mod_w, mod_b=m_mod_b, norm1_g=m_norm1_g, norm2_g=m_norm2_g, pool_w=m_pool_w, pool_b=m_pool_b,
                 pool_scale=m_pool_scale, kv_in_g=m_kv_in_g, w_dkv=m_w_dkv, ckv_norm_g=m_ckv_norm_g, w_uk=m_w_uk, w_uv=m_w_uv,
                 w_dq=m_w_dq, q_norm_g=m_q_norm_g, w_uq=m_w_uq, w_o=m_w_o, w_up=m_w_up, conv_w=m_conv_w, conv_b=m_conv_b,
                 w_down=m_w_down, final_g=m_final_g)
    mom_v = dict(mod_w=v_mod_w, mod_b=v_mod_b, norm1_g=v_norm1_g, norm2_g=v_norm2_g, pool_w=v_pool_w, pool_b=v_pool_b,
                 pool_scale=v_pool_scale, kv_in_g=v_kv_in_g, w_dkv=v_w_dkv, ckv_norm_g=v_ckv_norm_g, w_uk=v_w_uk, w_uv=v_w_uv,
                 w_dq=v_w_dq, q_norm_g=v_q_norm_g, w_uq=v_w_uq, w_o=v_w_o, w_up=v_w_up, conv_w=v_conv_w, conv_b=v_conv_b,
                 w_down=v_w_down, final_g=v_final_g)
    me = _my_index()
    d6 = N_MOD * D_MODEL
    mod_cols = d6 // N_DEV

    small_in = [c] + [shard[k] for k, _ in VECTOR_WEIGHTS]
    small_all = _exchange("gather_vectors", _pack(small_in, F32, SMALL_ROW_MULTIPLE), scatter=False)
    parts = _unpack(small_all, [a.shape for a in small_in])
    c_all = jnp.pad(parts[0].reshape(N_DEV, D_MODEL), ((0, N_DEV), (0, 0)))
    vec = {k: _unshard(p, ax) for (k, ax), p in zip(VECTOR_WEIGHTS, parts[1:])}

    my_mod_b = lax.dynamic_slice_in_dim(mod_b, me * mod_cols, mod_cols, axis=1)
    mods_mine = _mods_fwd("mods_fwd", c_all, mod_w, my_mod_b)
    mods_all = _exchange("gather_mods", _pack([mods_mine], F32, SMALL_ROW_MULTIPLE), scatter=False)
    mods_all = _unpack(mods_all, [mods_mine.shape])[0]
    mods = lax.dynamic_index_in_dim(mods_all, me, axis=2, keepdims=False)
    mods = jnp.moveaxis(mods, 0, 1).reshape(DEPTH, d6)

    tabq, tabk = _rope_tables(positions[0])
    half = N_DEV // 2
    up_cols = shard["w_up"].shape[2]
    cat = lambda a, axis, lo=0, hi=N_DEV: jnp.concatenate([a[j] for j in range(lo, hi)], axis=axis)

    def stage_pieces(l):
        out = {"pool_w": shard["pool_w"].astype(BF16)} if l == 0 else {}
        if l == N_A_LAYERS:
            out.update({k: shard[k].astype(BF16) for k in ("w_dkv", "w_uk", "w_uv")})
        if l >= N_A_LAYERS:
            out.update({k: shard[k][l - N_A_LAYERS].astype(BF16) for k in ("w_dq", "w_uq", "w_o")})
        out.update(w_up=shard["w_up"][l].astype(BF16), w_down=shard["w_down"][l].astype(BF16))
        return out

    gathers, pool_all = {}, []

    def start_gather(l, behind=None):
        pieces = stage_pieces(l)
        if behind is not None:
            pieces, _ = lax.optimization_barrier((pieces, behind))
        handles, token = _exchange_start(f"gather_start_{l}", list(pieces.values()), scatter=False)
        gathers[l] = (handles, pieces)
        return token[0, 0]

    def wait_gather(l, keys, after, tag=""):
        handles, pieces = gathers[l]
        which = [list(pieces).index(k) for k in keys]
        lands = _exchange_wait(f"gather_wait_{l}{tag}", handles, after, scatter=False, which=which)
        return dict(zip(keys, own_slot(lands, [pieces[k] for k in keys])))

    def whole_weights(l, got):
        w = dict(norm1_g=norm1_g[l], norm2_g=norm2_g[l], conv_w=vec["conv_w"][l], conv_b=conv_b[l])
        if l == 0:
            pool_all.append(got["pool_w"])
        if l < N_A_LAYERS:
            w.update(pool_w=cat(pool_all[0][:, l], 1), pool_b=vec["pool_b"][l], pool_scale=vec["pool_scale"][l])
        else:
            rope = got["w_uq"][..., QK_NOPE:]
            ext = jnp.concatenate([got["w_uq"][..., :QK_NOPE], rope, _swap_halves(rope)], axis=-1)
            w.update(w_dq=got["w_dq"].reshape(D_MODEL, Q_RANK), w_uq_ext=cat(ext, -1), w_o=got["w_o"].reshape(D_MODEL, D_MODEL),
                     q_norm_g=q_norm_g[l - N_A_LAYERS])
        if l == N_A_LAYERS:
            w.update(w_dkv_ext=_extend_w_dkv(got["w_dkv"].reshape(D_MODEL, KV_RANK + QK_ROPE)), w_uk=cat(got["w_uk"], -1),
                     w_uv=cat(got["w_uv"], -1), kv_in_g=kv_in_g, ckv_norm_g=ckv_norm_g)
        return w

    def own_slot(lands, own):
        return [lax.dynamic_update_index_in_dim(p, o, me, 0) for p, o in zip(lands, own)]

    def fetch(l, after):
        up_parts = lambda g8: (cat(g8, -1, 0, half), cat(g8, -1, half, N_DEV))
        if l == 0:
            start_gather(0, behind=mods)
            got = wait_gather(0, ["pool_w"], mods, "_pool")
            w_up = lambda aft: up_parts(wait_gather(0, ["w_up"], aft, "_up")["w_up"])
            w_down = lambda aft: wait_gather(0, ["w_down"], aft, "_down")["w_down"].reshape(D_FF, D_MODEL)
        else:
            got = wait_gather(l, list(gathers[l][1]), after)
            up, down = up_parts(got["w_up"]), got["w_down"].reshape(D_FF, D_MODEL)
            w_up, w_down = (lambda aft: up), (lambda aft: down)
        w = dict(whole_weights(l, got), w_up=w_up, w_down=w_down)
        return w, (start_gather(l + 1) if l + 1 < DEPTH else 0.0)

    scatters, pending, pool_grads, piece_grads = {}, {}, {}, {}

    def reduce_pieces(l, keys, got):
        for k, p in zip(keys, got):
            piece_grads[(k, l)] = _sum8(f"sum_grads_{k}_{l}", p.reshape(N_DEV, -1, p.shape[-1])).reshape(p.shape[1:])

    def start_scatter(name, sent):
        sent = {k: a.astype(BF16) for k, a in sent.items()}
        handles, token = _exchange_start(f"scatter_start_{name}", list(sent.values()), scatter=True)
        scatters[name] = (handles, list(sent), [lax.dynamic_index_in_dim(a, me, 0, keepdims=False) for a in sent.values()])
        return token[0, 0]

    def finish_scatter(name, l, after):
        handles, keys, own = scatters.pop(name)
        reduce_pieces(l, keys, own_slot(_exchange_wait(f"scatter_wait_{name}", handles, after, scatter=True), own))

    def push(l, part, big, after):
        cut = lambda a, n, axis: jnp.stack([lax.slice_in_dim(a, j * n, (j + 1) * n, axis=axis) for j in range(N_DEV)])
        sent = {}
        if part == "down":
            sent["w_down"] = big["w_down"].reshape(N_DEV, D_FF // N_DEV, D_MODEL)
        elif part == "up":
            sent["w_up"] = jnp.stack([lax.slice_in_dim(big[half_], j * up_cols, (j + 1) * up_cols, axis=1)
                                      for half_ in ("w_up_a", "w_up_v") for j in range(half)])
        elif l < N_A_LAYERS:
            pool_grads[l] = big["pool_w"]
        else:
            ext = cut(big["w_uq_ext"], Q_EXT, 1)
            rope = ext[..., QK_NOPE:QK_HEAD] + _unswap_halves(ext[..., QK_HEAD:])
            sent.update(w_dq=big["w_dq"].reshape(N_DEV, D_MODEL // N_DEV, Q_RANK), w_uq=jnp.concatenate([ext[..., :QK_NOPE], rope], axis=-1),
                        w_o=big["w_o"].reshape(N_DEV, D_MODEL // N_DEV, D_MODEL))
        if part == "mix" and l == N_A_LAYERS:
            sent.update(w_dkv=_fold_w_dkv_grad(big["w_dkv_ext"]).reshape(N_DEV, D_MODEL // N_DEV, KV_RANK + QK_ROPE),
                        w_uk=cut(big["w_uk"], QK_NOPE, 1), w_uv=cut(big["w_uv"], V_HEAD, 1))
        if l == 0 and part != "mix":
            return start_scatter(f"0_{part}", sent)
        if l == 0:
            finish_scatter("1", 1, after)
            pool = _shard8(jnp.stack([pool_grads[a] for a in range(N_A_LAYERS)]), 2).astype(BF16)
            reduce_pieces(0, ["pool_w"], _exchange_many("scatter_pool_grads", [pool], scatter=True))
            return 0.0
        pending.setdefault(l, {}).update(sent)
        if part != "mix":
            return 0.0
        if l + 1 < DEPTH:
            finish_scatter(str(l + 1), l + 1, after)
        return start_scatter(str(l), pending.pop(l))

    loss_row, dx, g, dmods = _forward_backward(x[0], loss_target[0], mods, tabq, tabk, final_g, fetch, push)
    layers_of = lambda k, ls: jnp.stack([piece_grads[(k, l)] for l in ls])
    grads = dict(w_dkv=piece_grads[("w_dkv", N_A_LAYERS)], w_uk=piece_grads[("w_uk", N_A_LAYERS)], w_uv=piece_grads[("w_uv", N_A_LAYERS)])
    for k in ("w_dq", "w_uq", "w_o"):
        grads[k] = layers_of(k, range(N_A_LAYERS, DEPTH))

    small_names = REPLICATED_WEIGHTS + tuple(k for k, _ in VECTOR_WEIGHTS)
    small_out = [dmods] + [g[k] for k in small_names] + [loss_row]
    small_shapes = [a.shape for a in small_out]
    small_got = _exchange("gather_small_grads", _pack(small_out, F32, SMALL_ROW_MULTIPLE), scatter=False)
    summed = _unpack(_sum8("sum_small_grads", small_got), small_shapes)
    grads["mod_b"] = summed[0]
    for k, s in zip(small_names, summed[1:-1]):
        grads[k] = s
    for k, ax in VECTOR_WEIGHTS:
        n = shard[k].shape[ax]
        grads[k] = lax.dynamic_slice_in_dim(grads[k], me * n, n, axis=ax)
    loss = summed[-1][0, 0]
    dmods_all = _unpack(small_got, small_shapes)[0]
    dm_mine = lax.dynamic_slice_in_dim(dmods_all, me * mod_cols, mod_cols, axis=2)
    dm_mine = jnp.pad(jnp.moveaxis(dm_mine, 0, 1), ((0, 0), (0, N_DEV), (0, 0)))
    grads["mod_w"] = _mods_bwd("mods_bwd", c_all, dm_mine)

    delta, new_m, new_v = {}, {}, {}

    def adamw(k):
        shp = shard[k].shape
        grads[k] = grads[k].reshape(shp)
        view = (lambda a: jnp.swapaxes(a, 1, 2)) if k == "w_up" else (lambda a: a)
        ops = [view(a) for a in (shard[k], grads[k], mom_m[k], mom_v[k])]
        ops[1] = lax.optimization_barrier(ops[1])
        res = _adamw(f"adamw_{k}", *[_as_2d(a) for a in ops])
        delta[k], new_m[k], new_v[k] = [view(r.reshape(ops[0].shape)) for r in res]
        grads[k] = view(ops[1])

    late = ("w_up", "w_down", "pool_w")
    for k in WEIGHT_ORDER:
        if k not in late:
            adamw(k)
    finish_scatter("0_down", 0, delta["final_g"])
    finish_scatter("0_up", 0, delta["final_g"])
    grads.update(w_up=layers_of("w_up", range(DEPTH)), w_down=layers_of("w_down", range(DEPTH)), pool_w=piece_grads[("pool_w", 0)])
    for k in late:
        adamw(k)
    return (loss, dx[None], *[grads[k] for k in WEIGHT_ORDER], *[delta[k] for k in WEIGHT_ORDER],
            *[new_m[k] for k in WEIGHT_ORDER], *[new_v[k] for k in WEIGHT_ORDER])
```

```python
import functools

import jax
import jax.numpy as jnp
from jax import lax
from jax.experimental import pallas as pl
from jax.experimental.pallas import tpu as pltpu

F32 = jnp.float32
BF16 = jnp.bfloat16

D_MODEL = 1024
DEPTH = 4
N_A_LAYERS = 2
N_B_LAYERS = 2
POOL_WINDOWS = (2, 4, 8, 16)
POOL_GROUP = 256
N_HEADS = 8
QK_NOPE = 128
QK_ROPE = 64
V_HEAD = 128
QK_HEAD = QK_NOPE + QK_ROPE
Q_RANK = 384
KV_RANK = 256
ROPE_THETA = 10000.0
D_FF = 2816
EPS = 1e-6
N_MOD = 6
ADAM_LR = 0.001
ADAM_B1 = 0.9
ADAM_B2 = 0.999
ADAM_EPS = 1e-08
ADAM_WD = 0.01
ADAM_STEP = 10

N_DEV = 8
LANES = 128
Q_EXT = 256
VMEM_LIMIT_BYTES = 48 * 1024 * 1024
MESH = pl.DeviceIdType.MESH
NEG_BIG = -0.7 * float(jnp.finfo(jnp.float32).max)


def _params(sem):
    return pltpu.CompilerParams(dimension_semantics=sem, vmem_limit_bytes=VMEM_LIMIT_BYTES)


def _tile(n, cap):
    if n <= cap:
        return n
    best = None
    for d in range(LANES, cap + 1, LANES):
        if n % d == 0:
            best = d
    assert best is not None, (n, cap)
    return best


def _dot(a, b, dims):
    return lax.dot_general(a, b, (dims, ((), ())), preferred_element_type=F32)


NN = ((1,), (0,))
NT = ((1,), (1,))
TN = ((0,), (0,))


def _modulated_rmsnorm(xv, gv, scale, shift):
    return xv * lax.rsqrt(jnp.mean(xv * xv, axis=-1, keepdims=True) + EPS) * gv * (1.0 + scale) + shift


def _mm(name, a, b, mode="nn", out_dtype=BF16, resid=None, gate=None, norm=None, rowtab=None, second=None, a_scale=None,
        gate_grad=None, tm_cap=1024, tn_cap=1408, tk_cap=1408):
    if mode == "tn":
        kdim, m = a.shape
    else:
        m, kdim = a.shape
    n = b.shape[0] if mode == "nt" else b.shape[1]
    tm, tn, tk = _tile(m, tm_cap), _tile(n, tn_cap), _tile(kdim, tk_cap)
    nk = kdim // tk
    dims = {"nn": NN, "nt": NT, "tn": TN}[mode]
    a_spec = pl.BlockSpec((tk, tm), lambda i, j, k: (k, i)) if mode == "tn" else pl.BlockSpec((tm, tk), lambda i, j, k: (i, k))
    b_spec = pl.BlockSpec((tn, tk), lambda i, j, k: (j, k)) if mode == "nt" else pl.BlockSpec((tk, tn), lambda i, j, k: (k, j))
    o_spec = pl.BlockSpec((tm, tn), lambda i, j, k: (i, j))
    g_spec = pl.BlockSpec((1, tn), lambda i, j, k: (0, j))
    gated = resid is not None
    assert sum(x is not None for x in (resid, rowtab, gate_grad)) <= 1
    n_ops = 2 if second is None else 4
    n_extra = 1 if a_scale is not None else 0

    def body(*refs):
        acc = refs[-1]
        i, k = pl.program_id(0), pl.program_id(2)

        @pl.when(k == 0)
        def _():
            acc[...] = jnp.zeros_like(acc)

        av = refs[0][...]
        if a_scale is not None:
            av = av.astype(F32) * refs[n_ops][...]
        prod = _dot(av.astype(BF16), refs[1][...].astype(BF16), dims)
        if second is not None:
            prod = prod + _dot(refs[2][...].astype(BF16), refs[3][...].astype(BF16), dims)
        acc[...] += prod
        rest = refs[n_ops + n_extra:-1]

        if gate_grad is not None:
            @pl.when((i == 0) & (k == 0))
            def _():
                rest[3][...] = jnp.zeros_like(rest[3])

        @pl.when(k == nk - 1)
        def _():
            if gated and norm is not None:
                r_ref, g_ref, ng_ref, sc_ref, sh_ref, x_ref, h_ref = rest
                xn = r_ref[...] + g_ref[...] * acc[...]
                x_ref[...] = xn
                h_ref[...] = _modulated_rmsnorm(xn, ng_ref[...], sc_ref[...], sh_ref[...]).astype(h_ref.dtype)
            elif gated:
                r_ref, g_ref, x_ref = rest
                x_ref[...] = r_ref[...] + g_ref[...] * acc[...]
            elif rowtab is not None:
                tab = rest[0][...]
                rest[1][...] = (acc[...] * jnp.concatenate([tab] * (tn // tab.shape[1]), axis=1)).astype(out_dtype)
            elif gate_grad is not None:
                w_ref, g_ref, o_ref, dg_ref = rest
                o_ref[...] = (acc[...] * g_ref[...]).astype(out_dtype)
                dg_ref[...] += _colsum(w_ref[...].astype(F32) * acc[...])
            else:
                rest[0][...] = acc[...].astype(out_dtype)

    ins, in_specs = [a, b], [a_spec, b_spec]
    if second is not None:
        assert second[0].shape == a.shape and second[1].shape == b.shape
        ins += list(second)
        in_specs += [a_spec, b_spec]
    if a_scale is not None:
        assert mode != "tn"
        ins.append(a_scale)
        in_specs.append(pl.BlockSpec((1, tk), lambda i, j, k: (0, k)))
    out_shape, out_specs = jax.ShapeDtypeStruct((m, n), out_dtype), o_spec
    sem = ("parallel", "parallel", "arbitrary")
    if rowtab is not None:
        assert tn % rowtab.shape[1] == 0
        ins.append(rowtab)
        in_specs.append(pl.BlockSpec((tm, rowtab.shape[1]), lambda i, j, k: (i, 0)))
    if gated:
        ins += [resid, gate]
        in_specs += [o_spec, g_spec]
        out_shape = jax.ShapeDtypeStruct((m, n), F32)
    if norm is not None:
        assert gated and tn == n
        ins += list(norm[:3])
        in_specs += [g_spec] * 3
        out_shape = (out_shape, jax.ShapeDtypeStruct((m, n), norm[3]))
        out_specs = (o_spec, o_spec)
    if gate_grad is not None:
        assert mode == "tn" and tn == n
        ins += list(gate_grad)
        in_specs += [o_spec, g_spec]
        out_shape = (out_shape, jax.ShapeDtypeStruct((1, n), F32))
        out_specs = (o_spec, g_spec)
        sem = ("arbitrary", "arbitrary", "arbitrary")
    return pl.pallas_call(
        body, name=name, grid=(m // tm, n // tn, nk), in_specs=in_specs, out_specs=out_specs, out_shape=out_shape,
        scratch_shapes=[pltpu.VMEM((tm, tn), F32)],
        compiler_params=_params(sem),
    )(*ins)


def _rowwise(name, fn, tiled, bcast, outs, sums=(), tr=512):
    tiled = [t if isinstance(t, tuple) else (t, t.shape[1], 0) for t in tiled]
    s = tiled[0][0].shape[0]
    tr = min(tr, s)
    assert s % tr == 0
    n_t, n_b, n_o = len(tiled), len(bcast), len(outs)

    def body(*refs):
        i = pl.program_id(0)
        vals = [r[...] for r in refs[:n_t + n_b]]
        o_vals, s_vals = fn(*vals)
        for r, v in zip(refs[n_t + n_b:n_t + n_b + n_o], o_vals):
            r[...] = v.astype(r.dtype)
        s_refs = refs[n_t + n_b + n_o:]

        @pl.when(i == 0)
        def _():
            for r in s_refs:
                r[...] = jnp.zeros_like(r)

        for r, v in zip(s_refs, s_vals):
            r[...] += v

    in_specs = [pl.BlockSpec((tr, n), functools.partial(lambda cb, i: (i, cb), cb)) for (_, n, cb) in tiled]
    in_specs += [pl.BlockSpec(b.shape, functools.partial(lambda nd, i: (0,) * nd, b.ndim)) for b in bcast]
    out_specs = [pl.BlockSpec((tr, n), lambda i: (i, 0)) for (n, _) in outs]
    out_specs += [pl.BlockSpec((1, n), lambda i: (0, 0)) for n in sums]
    out_shape = [jax.ShapeDtypeStruct((s, n), dt) for (n, dt) in outs]
    out_shape += [jax.ShapeDtypeStruct((1, n), F32) for n in sums]
    res = pl.pallas_call(
        body, name=name, grid=(s // tr,), in_specs=in_specs, out_specs=tuple(out_specs), out_shape=tuple(out_shape),
        compiler_params=_params(("arbitrary",)),
    )(*[t[0] for t in tiled], *bcast)
    return res


def _colsum(v):
    return jnp.sum(v, axis=0, keepdims=True)


def _rms_fwd(name, x, g, scale=None, shift=None, out_dtype=BF16, ncols=None):
    mod = scale is not None

    def fn(xv, gv, *ss):
        if mod:
            return (_modulated_rmsnorm(xv, gv, ss[0], ss[1]),), ()
        return (xv * lax.rsqrt(jnp.mean(xv * xv, axis=-1, keepdims=True) + EPS) * gv,), ()

    n = ncols or x.shape[1]
    return _rowwise(name, fn, [(x, n, 0)], [g] + ([scale, shift] if mod else []), [(n, out_dtype)])[0]


def _rms_bwd(name, x, g, dh, scale=None, dx_in=None, ncols=None, out_dtype=F32):
    mod = scale is not None
    has_in = dx_in is not None

    def fn(*vals):
        xv, dhv = vals[0], vals[1].astype(F32)
        rest = list(vals[2:])
        dxi = rest.pop(0) if has_in else None
        gv = rest.pop(0)
        rstd = lax.rsqrt(jnp.mean(xv * xv, axis=-1, keepdims=True) + EPS)
        xhat = xv * rstd
        sums = []
        if mod:
            sc = rest.pop(0)
            dyn = dhv * (1.0 + sc)
            dshift, dscale = _colsum(dhv), _colsum(dhv * (xhat * gv))
        else:
            dyn = dhv
        dg = _colsum(dyn * xhat)
        dxhat = dyn * gv
        dx = rstd * (dxhat - xhat * jnp.mean(dxhat * xhat, axis=-1, keepdims=True))
        if has_in:
            dx = dx + dxi
        sums = [dg] + ([dshift, dscale] if mod else [])
        return (dx,), sums

    n = ncols or x.shape[1]
    tiled = [(x, n, 0), dh] + ([dx_in] if has_in else [])
    return _rowwise(name, fn, tiled, [g] + ([scale] if mod else []), [(n, out_dtype)], [n] * (3 if mod else 1))


def _loss_head(name, x, g, target):
    n = x.shape[1]

    def fn(xv, tv, gv):
        rstd = lax.rsqrt(jnp.mean(xv * xv, axis=-1, keepdims=True) + EPS)
        xhat = xv * rstd
        err = xhat * gv - tv
        loss = 0.5 * jnp.sum(jnp.sum(err * err, axis=-1, keepdims=True) / n, axis=0, keepdims=True)
        dy = err / n
        dg = _colsum(dy * xhat)
        dxhat = dy * gv
        dx = rstd * (dxhat - xhat * jnp.mean(dxhat * xhat, axis=-1, keepdims=True))
        return (dx,), (dg, jnp.broadcast_to(loss, (1, LANES)))

    return _rowwise(name, fn, [x, target], [g], [(n, F32)], [n, LANES])


def _krope_fwd(name, kv_ext, tabk):
    def fn(xv, tv):
        t = xv * tv
        return (t + pltpu.roll(t, 64, 1),), ()

    return _rowwise(name, fn, [(kv_ext, LANES, 2), tabk], [], [(LANES, BF16)])[0]


def _krope_bwd(name, dkd, tabk):
    def fn(dv, tv):
        d = dv[:, :LANES]
        for h in range(1, N_HEADS):
            d = d + dv[:, h * LANES:(h + 1) * LANES]
        return ((d + pltpu.roll(d, 64, 1)) * tv,), ()

    return _rowwise(name, fn, [dkd, tabk], [], [(LANES, F32)])[0]


def _adamw(name, w, g, m, v):
    def fn(wv, gv, mv, vv):
        m2 = ADAM_B1 * mv + (1.0 - ADAM_B1) * gv
        v2 = ADAM_B2 * vv + (1.0 - ADAM_B2) * (gv * gv)
        m_hat = m2 / (1.0 - ADAM_B1 ** ADAM_STEP)
        v_hat = v2 / (1.0 - ADAM_B2 ** ADAM_STEP)
        delta = -ADAM_LR * (m_hat / (jnp.sqrt(v_hat) + ADAM_EPS) + ADAM_WD * wv)
        return (delta, m2, v2), ()

    r, c = w.shape
    tr = r
    for cand in (512, 256, 128, 64, 32, 16, 8):
        if r % cand == 0 and r > cand:
            tr = cand
            break
    return _rowwise(name, fn, [w, g, m, v], [], [(c, F32)] * 3, tr=tr)


def _sum8(name, parts):
    _, r, c = parts.shape
    tr = r
    for cand in (2048, 1024, 512, 256, 128, 64, 32, 16):
        if r % cand == 0 and r > cand and cand * c <= 256 * 1024:
            tr = cand
            break

    def body(p_ref, o_ref):
        acc = p_ref[0].astype(F32)
        for k in range(1, N_DEV):
            acc = acc + p_ref[k].astype(F32)
        o_ref[...] = acc

    return pl.pallas_call(
        body, name=name, grid=(r // tr,), in_specs=[pl.BlockSpec((N_DEV, tr, c), lambda i: (0, i, 0))],
        out_specs=pl.BlockSpec((tr, c), lambda i: (i, 0)), out_shape=jax.ShapeDtypeStruct((r, c), F32),
        compiler_params=_params(("parallel",)),
    )(parts)


def _mods_fwd(name, c_all, w, b):
    depth, d, n = w.shape

    def body(c_ref, w_ref, b_ref, o_ref):
        cv = c_ref[...]
        sc = (cv * (1.0 / (1.0 + jnp.exp(-cv)))).astype(BF16)
        o_ref[0] = _dot(sc, w_ref[0].astype(BF16), NN) + b_ref[0]

    return pl.pallas_call(
        body, name=name, grid=(depth,),
        in_specs=[pl.BlockSpec(c_all.shape, lambda l: (0, 0)), pl.BlockSpec((1, d, n), lambda l: (l, 0, 0)),
                  pl.BlockSpec((1, 1, n), lambda l: (l, 0, 0))],
        out_specs=pl.BlockSpec((1, c_all.shape[0], n), lambda l: (l, 0, 0)),
        out_shape=jax.ShapeDtypeStruct((depth, c_all.shape[0], n), F32),
        compiler_params=_params(("parallel",)),
    )(c_all, w, b.reshape(depth, 1, n))


def _mods_bwd(name, c_all, dm):
    depth, rows, n = dm.shape
    d = c_all.shape[1]

    def body(c_ref, dm_ref, o_ref):
        cv = c_ref[...]
        sc = (cv * (1.0 / (1.0 + jnp.exp(-cv)))).astype(BF16)
        o_ref[0] = _dot(sc, dm_ref[0].astype(BF16), TN)

    return pl.pallas_call(
        body, name=name, grid=(depth,),
        in_specs=[pl.BlockSpec(c_all.shape, lambda l: (0, 0)), pl.BlockSpec((1, rows, n), lambda l: (l, 0, 0))],
        out_specs=pl.BlockSpec((1, d, n), lambda l: (l, 0, 0)),
        out_shape=jax.ShapeDtypeStruct((depth, d, n), F32),
        compiler_params=_params(("parallel",)),
    )(c_all, dm)


POOL_TILE = 256


def _split_dot(band, val):
    hi = val.astype(BF16)
    lo = (val - hi.astype(F32)).astype(BF16)
    return _dot(band, hi, NN) + _dot(band, lo, NN)


def _pool_fwd(name, h1, x, pw, pb, ps, g1, norm):
    s, d = h1.shape
    t = POOL_TILE

    def body(hc_ref, hp_ref, x_ref, pw_ref, pb_ref, ps_ref, g_ref, ng_ref, sc_ref, sh_ref, xo_ref, zb_ref, pooled_ref, h2_ref):
        i = pl.program_id(0)
        r = lax.broadcasted_iota(jnp.int32, (t, t), 0)
        j = lax.broadcasted_iota(jnp.int32, (t, t), 1)
        pos = (i * t + lax.broadcasted_iota(jnp.int32, (t, 1), 0) + 1).astype(F32)
        has_prev = (i > 0).astype(F32)
        for grp, w in enumerate(POOL_WINDOWS):
            cs = slice(grp * POOL_GROUP, (grp + 1) * POOL_GROUP)
            hc = hc_ref[:, cs]
            band_cur = ((r - j >= 0) & (r - j < w)).astype(BF16)
            band_prev = (r + t - j < w).astype(BF16)
            ssum = _split_dot(band_cur, hc) + has_prev * _split_dot(band_prev, hp_ref[:, cs])
            pooled = (ssum / jnp.minimum(pos, float(w)) - hc).astype(BF16)
            zb = _dot(pooled, pw_ref[grp], NN) + pb_ref[:, cs]
            xo_ref[:, cs] = x_ref[:, cs] + g_ref[:, cs] * (zb * ps_ref[:, cs])
            zb_ref[:, cs] = zb
            pooled_ref[:, cs] = pooled
        h2_ref[...] = _modulated_rmsnorm(xo_ref[...], ng_ref[...], sc_ref[...], sh_ref[...]).astype(h2_ref.dtype)

    row = pl.BlockSpec((t, d), lambda i: (i, 0))
    vec = pl.BlockSpec((1, d), lambda i: (0, 0))
    return pl.pallas_call(
        body, name=name, grid=(s // t,),
        in_specs=[row, pl.BlockSpec((t, d), lambda i: (jnp.maximum(i - 1, 0), 0)), row,
                  pl.BlockSpec(pw.shape, lambda i: (0, 0, 0)), vec, vec, vec, vec, vec, vec],
        out_specs=(row, row, row, row),
        out_shape=(jax.ShapeDtypeStruct((s, d), F32), jax.ShapeDtypeStruct((s, d), F32), jax.ShapeDtypeStruct((s, d), BF16),
                   jax.ShapeDtypeStruct((s, d), BF16)),
        compiler_params=_params(("parallel",)),
    )(h1, h1, x, pw, pb, ps, g1, *norm)


def _pool_bwd(name, dxn, zb, pooled, pw, ps, g1):
    s, d = dxn.shape
    t = POOL_TILE
    nt = s // t

    def body(dc_ref, dn_ref, zb_ref, pooled_ref, pw_ref, ps_ref, g_ref, dh_ref, dpw_ref, dpb_ref, dps_ref, dg_ref):
        i = pl.program_id(0)

        @pl.when(i == 0)
        def _():
            dpw_ref[...] = jnp.zeros_like(dpw_ref)
            dpb_ref[...] = jnp.zeros_like(dpb_ref)
            dps_ref[...] = jnp.zeros_like(dps_ref)
            dg_ref[...] = jnp.zeros_like(dg_ref)

        jj = lax.broadcasted_iota(jnp.int32, (t, t), 0)
        rr = lax.broadcasted_iota(jnp.int32, (t, t), 1)
        pos = (i * t + lax.broadcasted_iota(jnp.int32, (t, 1), 0) + 1).astype(F32)
        has_next = (i < nt - 1).astype(F32)
        for grp, w in enumerate(POOL_WINDOWS):
            cs = slice(grp * POOL_GROUP, (grp + 1) * POOL_GROUP)
            gv, psv, zbv, dxc = g_ref[:, cs], ps_ref[:, cs], zb_ref[:, cs], dc_ref[:, cs]
            dg_ref[:, cs] += _colsum(dxc * (zbv * psv))
            dy = gv * dxc
            dps_ref[:, cs] += _colsum(dy * zbv)
            dz = dy * psv
            dpb_ref[:, cs] += _colsum(dz)
            dzb = dz.astype(BF16)
            dpw_ref[grp] += _dot(pooled_ref[:, cs], dzb, TN)
            dp = _dot(dzb, pw_ref[grp], NT)
            dzn = (gv * dn_ref[:, cs] * psv).astype(BF16)
            dpn = _dot(dzn, pw_ref[grp], NT) * (has_next / float(w))
            band_cur = ((rr - jj >= 0) & (rr - jj < w)).astype(BF16)
            band_next = (rr + t - jj < w).astype(BF16)
            dh_ref[:, cs] = _split_dot(band_cur, dp / jnp.minimum(pos, float(w))) + _split_dot(band_next, dpn) - dp

    row = pl.BlockSpec((t, d), lambda i: (i, 0))
    vec = pl.BlockSpec((1, d), lambda i: (0, 0))
    wspec = pl.BlockSpec(pw.shape, lambda i: (0, 0, 0))
    return pl.pallas_call(
        body, name=name, grid=(nt,),
        in_specs=[row, pl.BlockSpec((t, d), lambda i: (jnp.minimum(i + 1, nt - 1), 0)), row, row, wspec, vec, vec],
        out_specs=(row, wspec, vec, vec, vec),
        out_shape=(jax.ShapeDtypeStruct((s, d), F32), jax.ShapeDtypeStruct(pw.shape, F32),
                   jax.ShapeDtypeStruct((1, d), F32), jax.ShapeDtypeStruct((1, d), F32), jax.ShapeDtypeStruct((1, d), F32)),
        compiler_params=_params(("arbitrary",)),
    )(dxn, dxn, zb, pooled, pw, ps, g1)


GLU_TILE = 512
HALO = 16
INV_SQRT2 = 0.7071067811865476
INV_SQRT_2PI = 0.3989422804014327


def _up_glu_fwd(name, h2, wa, wv, cw, cb):
    s, d = h2.shape
    f = wa.shape[1]
    tm, tn = _tile(s, 1024), _tile(f, 1408)

    def body(h_ref, hh_ref, wa_ref, wv_ref, cw_ref, cb_ref, ua_ref, gl_ref, gpv_ref, ge_ref):
        i = pl.program_id(1)
        has_prev = (i > 0).astype(F32)
        a = _dot(h_ref[...], wa_ref[...], NN).astype(BF16)
        v = _dot(h_ref[...], wv_ref[...], NN)
        above = (_dot(hh_ref[...], wa_ref[...], NN) * has_prev).astype(BF16)
        ua_ref[...] = a
        ext = jnp.concatenate([above.astype(F32), a.astype(F32)], axis=0)
        e1 = pltpu.roll(ext, 1, 0)[HALO:]
        e2 = pltpu.roll(ext, 2, 0)[HALO:]
        pre = e2 * cw_ref[0:1, :] + e1 * cw_ref[1:2, :] + ext[HALO:] * cw_ref[2:3, :] + cb_ref[...]
        cdf = 0.5 * (1.0 + lax.erf(pre * INV_SQRT2))
        ge = pre * cdf
        gl_ref[...] = (ge * v).astype(gl_ref.dtype)
        gpv_ref[...] = ((cdf + pre * (INV_SQRT_2PI * jnp.exp(-0.5 * pre * pre))) * v).astype(gpv_ref.dtype)
        ge_ref[...] = ge.astype(ge_ref.dtype)

    blk = pl.BlockSpec((tm, tn), lambda j, i: (i, j))
    wspec = pl.BlockSpec((d, tn), lambda j, i: (0, j))
    return pl.pallas_call(
        body, name=name, grid=(f // tn, s // tm),
        in_specs=[pl.BlockSpec((tm, d), lambda j, i: (i, 0)), pl.BlockSpec((HALO, d), lambda j, i: (jnp.maximum(i * (tm // HALO) - 1, 0), 0)),
                  wspec, wspec, pl.BlockSpec((3, tn), lambda j, i: (0, j)), pl.BlockSpec((1, tn), lambda j, i: (0, j))],
        out_specs=(blk, blk, blk, blk), out_shape=tuple(jax.ShapeDtypeStruct((s, f), BF16) for _ in range(4)),
        compiler_params=_params(("parallel", "parallel")),
    )(h2, h2, wa, wv, cw, cb)


def _down_glu_bwd(name, dx, gate, wd, ua, gpv, ge, cw):
    s, f = ua.shape
    d = dx.shape[1]
    t, tf = min(GLU_TILE, s), _tile(f, 1408)
    nt = s // t
    te = t + HALO

    def body(dy_ref, dyn_ref, gate_ref, wd_ref, a_ref, ah_ref, g_ref, gn_ref, ge_ref, cw_ref, da_ref, dv_ref, dcw_ref, dcb_ref):
        i = pl.program_id(1)

        @pl.when(i == 0)
        def _():
            dcw_ref[...] = jnp.zeros_like(dcw_ref)
            dcb_ref[...] = jnp.zeros_like(dcb_ref)

        has_prev = (i > 0).astype(F32)
        has_next = (i < nt - 1).astype(F32)
        wdv = wd_ref[...]
        dgl = _dot((dy_ref[...] * gate_ref[...]).astype(BF16), wdv, NT)
        dgl_below = _dot((dyn_ref[...] * gate_ref[...]).astype(BF16), wdv, NT) * has_next
        dpre = jnp.concatenate([dgl * g_ref[...].astype(F32), dgl_below * gn_ref[...].astype(F32)], axis=0)
        c0, c1, c2 = cw_ref[0:1, :], cw_ref[1:2, :], cw_ref[2:3, :]
        up1 = pltpu.roll(dpre, te - 1, 0)
        up2 = pltpu.roll(dpre, te - 2, 0)
        da_ref[...] = (dpre * c2 + up1 * c1 + up2 * c0)[:t].astype(da_ref.dtype)
        dv_ref[...] = (dgl * ge_ref[...].astype(F32)).astype(dv_ref.dtype)
        ext = jnp.concatenate([ah_ref[...].astype(F32) * has_prev, a_ref[...].astype(F32)], axis=0)
        dpt = dpre[:t]
        dcb_ref[...] += _colsum(dpt)
        dcw_ref[0:1, :] += _colsum(pltpu.roll(ext, 2, 0)[HALO:] * dpt)
        dcw_ref[1:2, :] += _colsum(pltpu.roll(ext, 1, 0)[HALO:] * dpt)
        dcw_ref[2:3, :] += _colsum(ext[HALO:] * dpt)

    blk = pl.BlockSpec((t, tf), lambda j, i: (i, j))
    prev = pl.BlockSpec((HALO, tf), lambda j, i: (jnp.maximum(i * (t // HALO) - 1, 0), j))
    below = lambda i: jnp.minimum((i + 1) * (t // HALO), s // HALO - 1)
    w3 = pl.BlockSpec((3, tf), lambda j, i: (0, j))
    w1 = pl.BlockSpec((1, tf), lambda j, i: (0, j))
    return pl.pallas_call(
        body, name=name, grid=(f // tf, nt),
        in_specs=[pl.BlockSpec((t, d), lambda j, i: (i, 0)), pl.BlockSpec((HALO, d), lambda j, i: (below(i), 0)),
                  pl.BlockSpec((1, d), lambda j, i: (0, 0)), pl.BlockSpec((tf, d), lambda j, i: (j, 0)), blk, prev, blk,
                  pl.BlockSpec((HALO, tf), lambda j, i: (below(i), j)), blk, w3],
        out_specs=(blk, blk, w3, w1),
        out_shape=(jax.ShapeDtypeStruct((s, f), BF16), jax.ShapeDtypeStruct((s, f), BF16),
                   jax.ShapeDtypeStruct((3, f), F32), jax.ShapeDtypeStruct((1, f), F32)),
        compiler_params=_params(("parallel", "arbitrary")),
    )(dx, dx, gate, wd, ua, ua, gpv, gpv, ge, cw)


ATT_TILE = 512
ATT_ROWS = 256
ATT_HEADS = 4
ATT_BWD_HEADS = 2
ATT_BWD_VMEM_BYTES = 58 * 1024 * 1024
LOG2E = 1.4426950408889634
LN2 = 0.6931471805599453


def _head_blocks_t(a, width):
    s = a.shape[0]
    t = min(ATT_TILE, s)
    return a.reshape(s // t, t, N_HEADS, width).transpose(2, 0, 3, 1)


def _causal_mask(sv, q0, k0):
    row = q0 + lax.broadcasted_iota(jnp.int32, sv.shape, 0)
    col = k0 + lax.broadcasted_iota(jnp.int32, sv.shape, 1)
    return jnp.where(col <= row, sv, NEG_BIG)


def _attn_fwd(name, q_rot, kt4, v_ext):
    s = q_rot.shape[0]
    t = min(ATT_TILE, s)
    nq = s // t
    rq = min(ATT_ROWS, t)
    nh = ATT_HEADS

    def body(q_ref, kt_ref, v_ref, o_ref, row_ref, acc_ref, m_ref):
        qi = pl.program_id(1)
        acc_ref[...] = jnp.zeros_like(acc_ref)
        m_ref[...] = jnp.full_like(m_ref, NEG_BIG)

        def step(j, masked):
            for hh in range(nh):
                cols = slice(hh * Q_EXT, (hh + 1) * Q_EXT)
                v_blk = v_ref[pl.ds(pl.multiple_of(j * t, t), t), cols]
                for r in range(t // rq):
                    rs = pl.ds(r * rq, rq)
                    sv = _dot(q_ref[rs, cols], kt_ref[hh, j], NN)
                    if masked:
                        sv = _causal_mask(sv, r * rq, 0)
                    m_prev = m_ref[hh, rs, :]
                    m_new = jnp.maximum(m_prev, jnp.max(sv, axis=-1, keepdims=True))
                    p = jnp.exp2(sv - m_new).astype(BF16)
                    acc_ref[hh, rs, :] = jnp.exp2(m_prev - m_new) * acc_ref[hh, rs, :] + _dot(p, v_blk, NN)
                    m_ref[hh, rs, :] = m_new

        def full_step(j, carry):
            step(j, False)
            return carry

        lax.fori_loop(0, qi, full_step, 0)
        step(qi, True)
        for hh in range(nh):
            l = acc_ref[hh, :, V_HEAD:V_HEAD + 1]
            o_ref[:, hh * V_HEAD:(hh + 1) * V_HEAD] = (acc_ref[hh, :, :V_HEAD] / l).astype(o_ref.dtype)
            lse = jnp.broadcast_to(m_ref[hh] + jnp.log(l) * LOG2E, (t, LANES))
            row_ref[hh, 0] = jnp.transpose(lse)[0:8, :]

    return pl.pallas_call(
        body, name=name, grid=(N_HEADS // nh, nq),
        in_specs=[pl.BlockSpec((t, nh * Q_EXT), lambda h, i: (i, h)), pl.BlockSpec((nh, nq, Q_EXT, t), lambda h, i: (h, 0, 0, 0)),
                  pl.BlockSpec((s, nh * Q_EXT), lambda h, i: (0, h))],
        out_specs=(pl.BlockSpec((t, nh * V_HEAD), lambda h, i: (i, h)), pl.BlockSpec((nh, 1, 8, t), lambda h, i: (h, i, 0, 0))),
        out_shape=(jax.ShapeDtypeStruct((s, N_HEADS * V_HEAD), BF16), jax.ShapeDtypeStruct((N_HEADS, nq, 8, t), F32)),
        scratch_shapes=[pltpu.VMEM((nh, t, Q_EXT), F32), pltpu.VMEM((nh, t, 1), F32)],
        compiler_params=_params(("parallel", "parallel")),
    )(q_rot, kt4, v_ext)


def _attn_delta(name, o, do):
    s = o.shape[0]
    t = min(ATT_TILE, s)

    def body(o_ref, do_ref, delta_ref):
        prod = do_ref[...].astype(F32) * o_ref[...].astype(F32)
        for h in range(N_HEADS):
            delta = jnp.sum(prod[:, h * V_HEAD:(h + 1) * V_HEAD], axis=-1, keepdims=True)
            delta_ref[h, 0] = jnp.transpose(jnp.broadcast_to(delta, (t, LANES)))[0:8, :]

    rows = pl.BlockSpec((t, N_HEADS * V_HEAD), lambda i: (i, 0))
    return pl.pallas_call(
        body, name=name, grid=(s // t,), in_specs=[rows, rows],
        out_specs=pl.BlockSpec((N_HEADS, 1, 8, t), lambda i: (0, i, 0, 0)),
        out_shape=jax.ShapeDtypeStruct((N_HEADS, s // t, 8, t), F32),
        compiler_params=_params(("parallel",)),
    )(o, do)


def _attn_bwd(name, kfull, v, qt4, q_rot, dot4, do, lse_row, delta_row, tabq, acc_in=None):
    s = kfull.shape[0]
    t = min(ATT_TILE, s)
    nq = s // t
    nh = ATT_BWD_HEADS
    has_in = acc_in is not None

    def body(*refs):
        k_ref, v_ref, qt_ref, q_ref, dot_ref, do_ref, lse_ref, delta_ref, tab_ref = refs[:9]
        dq_ref, dkn_ref, dkd_ref, dv_ref, dq_acc_ref, acck_ref, accv_ref = refs[-7:]
        kj = pl.program_id(1)

        @pl.when(kj == 0)
        def _():
            dq_acc_ref[...] = jnp.zeros_like(dq_acc_ref)

        acck_ref[...] = jnp.zeros_like(acck_ref)
        accv_ref[...] = jnp.zeros_like(accv_ref)

        def step(i, masked):
            qs = pl.ds(pl.multiple_of(i * t, t), t)
            for hh in range(nh):
                qc = slice(hh * Q_EXT, (hh + 1) * Q_EXT)
                vc = slice(hh * LANES, (hh + 1) * LANES)
                k_blk = k_ref[:, qc]
                st = _dot(k_blk, qt_ref[hh, i], NN)
                if masked:
                    krow = lax.broadcasted_iota(jnp.int32, st.shape, 0)
                    qcol = lax.broadcasted_iota(jnp.int32, st.shape, 1)
                    st = jnp.where(krow <= qcol, st, NEG_BIG)
                pt = jnp.exp2(st - lse_ref[hh, i, 0:1, :])
                accv_ref[hh] += _dot(pt.astype(BF16), do_ref[qs, vc], NN)
                dpt = _dot(v_ref[:, vc], dot_ref[hh, i], NN)
                dst = (pt * (dpt - delta_ref[hh, i, 0:1, :])).astype(BF16)
                acck_ref[hh] += _dot(dst, q_ref[qs, qc], NN)
                dq_acc_ref[hh, qs, :] += _dot(dst, k_blk, TN)

        def full_step(i, carry):
            step(i, False)
            return carry

        step(kj, True)
        lax.fori_loop(kj + 1, nq, full_step, 0)
        for hh in range(nh):
            vc = slice(hh * LANES, (hh + 1) * LANES)
            dk = acck_ref[hh] * LN2
            dkn, dkd, dv = dk[:, :QK_NOPE], dk[:, QK_NOPE:], accv_ref[hh]
            if has_in:
                dkn, dkd, dv = dkn + refs[9][:, vc], dkd + refs[10][:, vc], dv + refs[11][:, vc]
            dkn_ref[:, vc], dkd_ref[:, vc], dv_ref[:, vc] = dkn, dkd, dv

        @pl.when(kj == nq - 1)
        def _():
            for hh in range(nh):
                dq_ref[:, hh * Q_EXT:(hh + 1) * Q_EXT] = (dq_acc_ref[hh] * (tab_ref[...] * LN2)).astype(dq_ref.dtype)

    kblk = pl.BlockSpec((t, nh * LANES), lambda h, j: (j, h))
    col = pl.BlockSpec((s, nh * LANES), lambda h, j: (0, h))
    q_all = pl.BlockSpec((s, nh * Q_EXT), lambda h, j: (0, h))
    stat = pl.BlockSpec((nh, nq, 8, t), lambda h, j: (h, 0, 0, 0))
    ins = [kfull, v, qt4, q_rot, dot4, do, lse_row, delta_row, tabq]
    in_specs = [pl.BlockSpec((t, nh * Q_EXT), lambda h, j: (j, h)), kblk, pl.BlockSpec((nh, nq, Q_EXT, t), lambda h, j: (h, 0, 0, 0)),
                q_all, pl.BlockSpec((nh, nq, V_HEAD, t), lambda h, j: (h, 0, 0, 0)), col, stat, stat,
                pl.BlockSpec((s, Q_EXT), lambda h, j: (0, 0))]
    if has_in:
        ins += list(acc_in)
        in_specs += [kblk, kblk, kblk]
    wide = jax.ShapeDtypeStruct((s, N_HEADS * LANES), F32)
    return pl.pallas_call(
        body, name=name, grid=(N_HEADS // nh, nq), in_specs=in_specs, out_specs=(q_all, kblk, kblk, kblk),
        out_shape=(jax.ShapeDtypeStruct((s, N_HEADS * Q_EXT), BF16), wide, wide, wide),
        scratch_shapes=[pltpu.VMEM((nh, s, Q_EXT), F32), pltpu.VMEM((nh, t, Q_EXT), F32), pltpu.VMEM((nh, t, LANES), F32)],
        compiler_params=pltpu.CompilerParams(dimension_semantics=("parallel", "arbitrary"), vmem_limit_bytes=ATT_BWD_VMEM_BYTES),
    )(*ins)


def _swap_halves(w):
    half = w.shape[-1] // 2
    return jnp.concatenate([-w[..., half:], w[..., :half]], axis=-1)


def _unswap_halves(g):
    half = g.shape[-1] // 2
    return jnp.concatenate([g[..., half:], -g[..., :half]], axis=-1)


def _extend_w_dkv(w):
    return jnp.concatenate([w, _swap_halves(w[:, KV_RANK:])], axis=-1)


def _fold_w_dkv_grad(g):
    rope = g[:, KV_RANK:KV_RANK + QK_ROPE] + _unswap_halves(g[:, KV_RANK + QK_ROPE:])
    return jnp.concatenate([g[:, :KV_RANK], rope], axis=-1)


def _rope_tables(positions):
    inv = 1.0 / (ROPE_THETA ** (jnp.arange(0, QK_ROPE, 2, dtype=F32) / QK_ROPE))
    ang = positions.astype(F32)[:, None] * inv
    cos, sin = jnp.cos(ang), jnp.sin(ang)
    tabk = jnp.concatenate([cos, cos, sin, sin], axis=-1)
    scale = QK_HEAD ** -0.5 * LOG2E
    tabq = jnp.concatenate([jnp.full((positions.shape[0], QK_NOPE), scale, F32), tabk * scale], axis=-1)
    return tabq, tabk


def _forward_backward(x, target, mods, tabq, tabk, norm1_all, final_g, fetch, push):
    row = lambda vec: vec.reshape(1, -1)
    mod = [[row(mods[l, k * D_MODEL:(k + 1) * D_MODEL]) for k in range(N_MOD)] for l in range(DEPTH)]
    saved, weights = [], []
    kv = h1 = None
    for l in range(DEPTH):
        w, tok = fetch(l, x)
        sh1, sc1, g1, sh2, sc2, g2 = mod[l]
        g1 = g1 + tok
        norm2 = (row(w["norm2_g"]), sc2, sh2)
        if l == N_A_LAYERS:
            kvn = _rms_fwd("kvin_fwd", x, row(w["kv_in_g"]))
            kv_ext = _mm("dkv_fwd", kvn, w["w_dkv_ext"], out_dtype=F32)
            ckv = _rms_fwd("ckv_fwd", kv_ext, row(w["ckv_norm_g"]), ncols=KV_RANK)
            kd = _krope_fwd("krope_fwd", kv_ext, tabk)
            kn, v = _mm("uk_fwd", ckv, w["w_uk"]), _mm("uv_fwd", ckv, w["w_uv"])
            heads = lambda a: [a[:, h * LANES:(h + 1) * LANES] for h in range(N_HEADS)]
            kfull = jnp.concatenate([part for kh in heads(kn) for part in (kh, kd)], axis=-1)
            v_ext = jnp.concatenate([part for vh in heads(v) for part in (vh, jnp.ones_like(vh))], axis=-1)
            kv = dict(x=x, kvn=kvn, kv_ext=kv_ext, ckv=ckv, v=v, kfull=kfull, v_ext=v_ext,
                      kt4=_head_blocks_t(kfull, Q_EXT))
        x_in = x
        if l == 0:
            h1 = _rms_fwd("norm1_fwd_0", x, row(norm1_all[0]), sc1, sh1, out_dtype=F32)
        if l < N_A_LAYERS:
            x_mid, zb, pooled, h2 = _pool_fwd(f"pool_fwd_{l}", h1, x, w["pool_w"], row(w["pool_b"]), row(w["pool_scale"]), g1, norm2)
            mix = (zb, pooled)
        else:
            cq_pre = _mm(f"dq_fwd_{l}", h1, w["w_dq"], out_dtype=F32)
            cq = _rms_fwd(f"qnorm_fwd_{l}", cq_pre, row(w["q_norm_g"]))
            q_rot = _mm(f"uq_fwd_{l}", cq, w["w_uq_ext"], rowtab=tabq)
            o, lse_row = _attn_fwd(f"attn_fwd_{l}", q_rot, kv["kt4"], kv["v_ext"])
            x_mid, h2 = _mm(f"wo_fwd_{l}", o, w["w_o"], resid=x, gate=g1, norm=norm2 + (BF16,))
            mix = (h1, cq_pre, cq, q_rot, o, lse_row)
        w_up_a, w_up_v = w["w_up"](h2)
        ua, gl, gpv, ge = _up_glu_fwd(f"up_glu_fwd_{l}", h2, w_up_a, w_up_v, w["conv_w"], row(w["conv_b"]))
        w_down = w["w_down"](gl)
        if l + 1 < DEPTH:
            nxt = (row(norm1_all[l + 1]), mod[l + 1][1], mod[l + 1][0], F32 if l + 1 < N_A_LAYERS else BF16)
            x, h1 = _mm(f"down_fwd_{l}", gl, w_down, resid=x_mid, gate=g2, norm=nxt)
        else:
            x = _mm(f"down_fwd_{l}", gl, w_down, resid=x_mid, gate=g2)
        saved.append((x_in, x_mid, h2, ua, gpv, ge, gl, mix))
        weights.append(dict(w, w_up_a=w_up_a, w_up_v=w_up_v, w_down=w_down))

    dx, dfinal_g, loss = _loss_head("loss_head", x, row(final_g), target)
    g = {"final_g": dfinal_g.reshape(-1)}
    per_layer = {k: [None] * DEPTH for k in ("norm1_g", "norm2_g", "conv_w", "conv_b")}
    per_a = {k: [None] * N_A_LAYERS for k in ("pool_b", "pool_scale")}
    per_b = {k: [None] * N_B_LAYERS for k in ("q_norm_g",)}
    dmods = [None] * DEPTH
    dkv = None
    tok = 0.0
    for l in reversed(range(DEPTH)):
        w, big = weights[l], {}
        sh1, sc1, g1, sh2, sc2, g2 = mod[l]
        g2 = g2 + tok
        x_in, x_mid, h2, ua, gpv, ge, gl, mix = saved[l]
        dw_down, dg2 = _mm(f"down_wgrad_{l}", gl, dx, mode="tn", tm_cap=1408, gate_grad=(w["w_down"], g2))
        tok = push(l, "down", dict(w_down=dw_down), None)
        da, dv_, dcw, dcb = _down_glu_bwd(f"down_glu_bwd_{l}", dx, g2, w["w_down"], ua, gpv, ge, w["conv_w"] + tok)
        dh2 = _mm(f"up_bwd_{l}", da, w["w_up_a"], mode="nt", out_dtype=F32, second=(dv_, w["w_up_v"]))
        tok = push(l, "up", dict(w_up_a=_mm(f"up_a_wgrad_{l}", h2, da, mode="tn"), w_up_v=_mm(f"up_v_wgrad_{l}", h2, dv_, mode="tn")), None)
        per_layer["conv_w"][l], per_layer["conv_b"][l] = dcw, dcb.reshape(-1)
        dx_mid, dn2, dsh2, dsc2 = _rms_bwd(f"norm2_bwd_{l}", x_mid, row(w["norm2_g"]), dh2, sc2 + tok, dx_in=dx)
        per_layer["norm2_g"][l] = dn2.reshape(-1)
        if l < N_A_LAYERS:
            zb, pooled = mix
            dh1, dpw, dpb, dps, dg1 = _pool_bwd(f"pool_bwd_{l}", dx_mid, zb, pooled, w["pool_w"], row(w["pool_scale"]), g1)
            big["pool_w"] = dpw
            per_a["pool_b"][l], per_a["pool_scale"][l] = dpb.reshape(-1), dps.reshape(-1)
        else:
            j = l - N_A_LAYERS
            h1, cq_pre, cq, q_rot, o, lse_row = mix
            do = _mm(f"wo_bwd_{l}", dx_mid, w["w_o"], mode="nt", a_scale=g1)
            big["w_o"], dg1 = _mm(f"wo_wgrad_{l}", o, dx_mid, mode="tn", gate_grad=(w["w_o"], g1))
            delta_row = _attn_delta(f"attn_delta_{l}", o, do)
            dq_ext, *dkv = _attn_bwd(f"attn_bwd_{l}", kv["kfull"], kv["v"], _head_blocks_t(q_rot, Q_EXT), q_rot, _head_blocks_t(do, V_HEAD), do,
                                     lse_row, delta_row, tabq, acc_in=dkv)
            dcq = _mm(f"uq_bwd_{l}", dq_ext, w["w_uq_ext"], mode="nt", out_dtype=F32)
            big["w_uq_ext"] = _mm(f"uq_wgrad_{l}", cq, dq_ext, mode="tn", out_dtype=F32)
            dcq_pre, dqn = _rms_bwd(f"qnorm_bwd_{l}", cq_pre, row(w["q_norm_g"]), dcq, out_dtype=BF16)
            per_b["q_norm_g"][j] = dqn.reshape(-1)
            dh1 = _mm(f"dq_bwd_{l}", dcq_pre, w["w_dq"], mode="nt")
            big["w_dq"] = _mm(f"dq_wgrad_{l}", h1, dcq_pre, mode="tn")
        dx, dn1, dsh1, dsc1 = _rms_bwd(f"norm1_bwd_{l}", x_in, row(w["norm1_g"]), dh1, sc1, dx_in=dx_mid)
        per_layer["norm1_g"][l] = dn1.reshape(-1)
        dmods[l] = jnp.concatenate([dsh1, dsc1, dg1, dsh2, dsc2, dg2], axis=-1).reshape(-1)
        if l == N_A_LAYERS:
            dkn, dkd, dv = dkv
            dckv = _mm("ukv_bwd", dkn, w["w_uk"], mode="nt", out_dtype=F32, second=(dv, w["w_uv"]))
            big["w_uk"] = _mm("uk_wgrad", kv["ckv"], dkn, mode="tn")
            big["w_uv"] = _mm("uv_wgrad", kv["ckv"], dv, mode="tn")
            dkr = _krope_bwd("krope_bwd", dkd, tabk)
            dc, dckv_g = _rms_bwd("ckv_bwd", kv["kv_ext"], row(w["ckv_norm_g"]), dckv, ncols=KV_RANK, out_dtype=BF16)
            dkv_ext = jnp.concatenate([dc, dkr.astype(BF16)], axis=-1)
            dkvn = _mm("dkv_bwd", dkv_ext, w["w_dkv_ext"], mode="nt")
            big["w_dkv_ext"] = _mm("dkv_wgrad", kv["kvn"], dkv_ext, mode="tn", out_dtype=F32)
            dx, dkv_in_g = _rms_bwd("kvin_bwd", kv["x"], row(w["kv_in_g"]), dkvn, dx_in=dx)
            g["ckv_norm_g"], g["kv_in_g"] = dckv_g.reshape(-1), dkv_in_g.reshape(-1)
        tok = push(l, "mix", big, dx)
    for group in (per_layer, per_a, per_b):
        for k, vals in group.items():
            g[k] = jnp.stack(vals)
    return loss, dx, g, jnp.stack(dmods)


def _my_index():
    return 4 * lax.axis_index("x") + 2 * lax.axis_index("y") + lax.axis_index("c")


def _peer(k):
    x, y, c = lax.axis_index("x"), lax.axis_index("y"), lax.axis_index("c")
    return (1 - x if k & 4 else x, 1 - y if k & 2 else y, 1 - c if k & 1 else c)


def _index_of(pos):
    return 4 * pos[0] + 2 * pos[1] + pos[2]


def _exchange_many(name, arrays, scatter):
    n = len(arrays)
    blocks = [tuple(a.shape[1:]) if scatter else tuple(a.shape) for a in arrays]

    def body(*refs):
        x_refs, o_refs = refs[:n], refs[n:2 * n]
        send_sems, recv_sems, local_sems = refs[2 * n:]
        me = _my_index()
        started = []
        for a in range(n):
            mine = pltpu.make_async_copy(x_refs[a].at[me] if scatter else x_refs[a], o_refs[a].at[me], local_sems.at[a])
            mine.start()
            started.append(mine)
        sends = []
        for k in range(1, N_DEV):
            peer = _peer(k)
            for a in range(n):
                cp = pltpu.make_async_remote_copy(
                    src_ref=x_refs[a].at[_index_of(peer)] if scatter else x_refs[a], dst_ref=o_refs[a].at[me],
                    send_sem=send_sems.at[a, k - 1], recv_sem=recv_sems.at[a, k - 1], device_id=peer, device_id_type=MESH)
                cp.start()
                sends.append(cp)
        for k in range(1, N_DEV):
            peer = _peer(k)
            for a in range(n):
                pltpu.make_async_remote_copy(
                    src_ref=x_refs[a].at[me] if scatter else x_refs[a], dst_ref=o_refs[a].at[_index_of(peer)],
                    send_sem=send_sems.at[a, k - 1], recv_sem=recv_sems.at[a, k - 1], device_id=peer, device_id_type=MESH).wait_recv()
        for cp in sends:
            cp.wait_send()
        for mine in started:
            mine.wait()

    return pl.pallas_call(
        body, name=name, out_shape=tuple(jax.ShapeDtypeStruct((N_DEV,) + blk, a.dtype) for blk, a in zip(blocks, arrays)),
        in_specs=[pl.BlockSpec(memory_space=pl.ANY)] * n, out_specs=tuple([pl.BlockSpec(memory_space=pl.ANY)] * n),
        scratch_shapes=[pltpu.SemaphoreType.DMA((n, N_DEV - 1)), pltpu.SemaphoreType.DMA((n, N_DEV - 1)), pltpu.SemaphoreType.DMA((n,))],
    )(*arrays)


def _exchange(name, x, scatter):
    return _exchange_many(name, [x], scatter)[0]


HBM_SPEC = pl.BlockSpec(memory_space=pltpu.HBM)
SEM_SPEC = pl.BlockSpec(memory_space=pltpu.SEMAPHORE)
DATAFLOW = pltpu.SideEffectType.DATAFLOW_SIDE_EFFECTING


def _remote_copies(x_refs, land_refs, send_sems, recv_sems, scatter, numbers=None):
    me = _my_index()
    numbers = list(range(len(x_refs))) if numbers is None else numbers
    out, inc = [], []
    for a in range(len(x_refs)):
        for k in range(1, N_DEV):
            peer = _peer(k)
            pair = numbers[a] * (N_DEV - 1) + k - 1
            sems = dict(send_sem=send_sems.at[pair], recv_sem=recv_sems.at[pair], device_id=peer, device_id_type=MESH)
            out.append(pltpu.make_async_remote_copy(
                src_ref=x_refs[a].at[_index_of(peer)] if scatter else x_refs[a], dst_ref=land_refs[a].at[me], **sems))
            inc.append(pltpu.make_async_remote_copy(
                src_ref=x_refs[a].at[me] if scatter else x_refs[a], dst_ref=land_refs[a].at[_index_of(peer)], **sems))
    return out, inc


def _exchange_start(name, arrays, scatter):
    n = len(arrays)
    blocks = [tuple(a.shape[1:]) if scatter else tuple(a.shape) for a in arrays]

    def body(*refs):
        x_refs, land_refs = refs[:n], refs[n:2 * n]
        send_sems, recv_sems = refs[2 * n], refs[2 * n + 1]
        for cp in _remote_copies(x_refs, land_refs, send_sems, recv_sems, scatter)[0]:
            cp.start()
        refs[-1][...] = jnp.zeros_like(refs[-1])

    sem_type = pltpu.SemaphoreType.DMA((n * (N_DEV - 1),))
    lands =[pltpu.with_memory_space_constraint(lax.empty((N_DEV,) + blk, a.dtype), pltpu.HBM) for blk, a in zip(blocks, arrays)]
    srcs = [pltpu.with_memory_space_constraint(a, pltpu.HBM) for a in arrays]
    res = pl.pallas_call(
        body, name=name,
        out_shape=(sem_type, sem_type, *[pltpu.HBM(a.shape, a.dtype) for a in srcs + lands], jax.ShapeDtypeStruct((8, LANES), F32)),
        in_specs=[HBM_SPEC] * (2 * n), out_specs=(SEM_SPEC, SEM_SPEC, *[HBM_SPEC] * (2 * n), pl.BlockSpec(memory_space=pltpu.VMEM)),
        input_output_aliases={i: 2 + i for i in range(2 * n)},
        compiler_params=pltpu.CompilerParams(has_side_effects=DATAFLOW),
    )(*srcs, *lands)
    return (res[0], res[1], list(res[2:2 + n]), list(res[2 + n:2 + 2 * n])), res[-1]


def _exchange_wait(name, handles, after, scatter, which=None):
    send_sems, recv_sems, srcs, lands = handles
    which = list(range(len(srcs))) if which is None else list(which)
    srcs, lands = [srcs[a] for a in which], [lands[a] for a in which]
    n = len(srcs)

    def body(*refs):
        x_refs, land_refs = refs[:n], refs[n:2 * n]
        out, inc = _remote_copies(x_refs, land_refs, refs[2 * n], refs[2 * n + 1], scatter, which)
        for cp in out:
            cp.wait_send()
        for cp in inc:
            cp.wait_recv()

    res = pl.pallas_call(
        body, name=name, out_shape=tuple(pltpu.HBM(a.shape, a.dtype) for a in srcs + lands),
        in_specs=[HBM_SPEC] * (2 * n) + [SEM_SPEC, SEM_SPEC, pl.BlockSpec(memory_space=pl.ANY)], out_specs=tuple([HBM_SPEC] * (2 * n)),
        input_output_aliases={i: i for i in range(2 * n)},
        compiler_params=pltpu.CompilerParams(has_side_effects=DATAFLOW),
    )(*srcs, *lands, send_sems, recv_sems, after)
    return list(res[n:])


def _pack(arrays, dtype, row_multiple):
    flat = jnp.concatenate([a.astype(dtype).reshape(-1) for a in arrays])
    rows = -(-flat.shape[0] // (LANES * row_multiple)) * row_multiple
    return jnp.pad(flat, (0, rows * LANES - flat.shape[0])).reshape(rows, LANES)


def _unpack(packed, shapes):
    lead = packed.shape[:-2]
    flat = packed.reshape(lead + (-1,))
    out, off = [], 0
    for shp in shapes:
        size = 1
        for d in shp:
            size *= d
        out.append(flat[..., off:off + size].reshape(lead + tuple(shp)))
        off += size
    return out


def _unshard(g8, axis):
    return jnp.concatenate([g8[j] for j in range(N_DEV)], axis=axis)


def _shard8(full, axis):
    n = full.shape[axis] // N_DEV
    return jnp.stack([lax.slice_in_dim(full, j * n, (j + 1) * n, axis=axis) for j in range(N_DEV)])


VECTOR_WEIGHTS = (("pool_b", 1), ("pool_scale", 1), ("conv_w", 2))
REPLICATED_WEIGHTS = ("norm1_g", "norm2_g", "kv_in_g", "ckv_norm_g", "q_norm_g", "conv_b", "final_g")
WEIGHT_ORDER = ("mod_w", "mod_b", "norm1_g", "norm2_g", "pool_w", "pool_b", "pool_scale", "kv_in_g", "w_dkv", "ckv_norm_g", "w_uk",
                "w_uv", "w_dq", "q_norm_g", "w_uq", "w_o", "w_up", "conv_w", "conv_b", "w_down", "final_g")
SMALL_ROW_MULTIPLE = 16


def _as_2d(a):
    if a.ndim == 1:
        return a.reshape(-1, LANES)
    return a.reshape(-1, a.shape[-1])


def kernel(x, c, positions, mod_w, mod_b, norm1_g, norm2_g, pool_w, pool_b, pool_scale, kv_in_g, w_dkv, ckv_norm_g, w_uk, w_uv, w_dq, q_norm_g, w_uq, w_o, w_up, conv_w, conv_b, w_down, final_g, loss_target, m_mod_w, m_mod_b, m_norm1_g, m_norm2_g, m_pool_w, m_pool_b, m_pool_scale, m_kv_in_g, m_w_dkv, m_ckv_norm_g, m_w_uk, m_w_uv, m_w_dq, m_q_norm_g, m_w_uq, m_w_o, m_w_up, m_conv_w, m_conv_b, m_w_down, m_final_g, v_mod_w, v_mod_b, v_norm1_g, v_norm2_g, v_pool_w, v_pool_b, v_pool_scale, v_kv_in_g, v_w_dkv, v_ckv_norm_g, v_w_uk, v_w_uv, v_w_dq, v_q_norm_g, v_w_uq, v_w_o, v_w_up, v_conv_w, v_conv_b, v_w_down, v_final_g):
    shard = dict(mod_w=mod_w, mod_b=mod_b, norm1_g=norm1_g, norm2_g=norm2_g, pool_w=pool_w, pool_b=pool_b, pool_scale=pool_scale,
                 kv_in_g=kv_in_g, w_dkv=w_dkv, ckv_norm_g=ckv_norm_g, w_uk=w_uk, w_uv=w_uv, w_dq=w_dq, q_norm_g=q_norm_g, w_uq=w_uq,
                 w_o=w_o, w_up=w_up, conv_w=conv_w, conv_b=conv_b, w_down=w_down, final_g=final_g)
    mom_m = dict(mod_w=m_mod_w, mod_b=m_mod_b, norm1_g=m_norm1_g, norm2_g=m_norm2_g, pool_w=m_pool_w, pool_b=m_pool_b,
                 pool_scale=m_pool_scale, kv_in_g=m_kv_in_g, w_dkv=m_w_dkv, ckv_norm_g=m_ckv_norm_g, w_uk=m_w_uk, w_uv=m_w_uv,
                 w_dq=m_w_dq, q_norm_g=m_q_norm_g, w_uq=m_w_uq, w_o=m_w_o, w_up=m_w_up, conv_w=m_conv_w, conv_b=m_conv_b,
                 w_down=m_w_down, final_g=m_final_g)
    mom_v = dict(mod_w=v_mod_w, mod_b=v_mod_b, norm1_g=v_norm1_g, norm2_g=v_norm2_g, pool_w=v_pool_w, pool_b=v_pool_b,
                 pool_scale=v_pool_scale, kv_in_g=v_kv_in_g, w_dkv=v_w_dkv, ckv_norm_g=v_ckv_norm_g, w_uk=v_w_uk, w_uv=v_w_uv,
                 w_dq=v_w_dq, q_norm_g=v_q_norm_g, w_uq=v_w_uq, w_o=v_w_o, w_up=v_w_up, conv_w=v_conv_w, conv_b=v_conv_b,
                 w_down=v_w_down, final_g=v_final_g)
    me = _my_index()
    d6 = N_MOD * D_MODEL
    mod_cols = d6 // N_DEV

    small_in = [c] + [shard[k] for k, _ in VECTOR_WEIGHTS]
    small_all = _exchange("gather_vectors", _pack(small_in, F32, SMALL_ROW_MULTIPLE), scatter=False)
    parts = _unpack(small_all, [a.shape for a in small_in])
    c_all = jnp.pad(parts[0].reshape(N_DEV, D_MODEL), ((0, N_DEV), (0, 0)))
    vec = {k: _unshard(p, ax) for (k, ax), p in zip(VECTOR_WEIGHTS, parts[1:])}

    my_mod_b = lax.dynamic_slice_in_dim(mod_b, me * mod_cols, mod_cols, axis=1)
    mods_mine = _mods_fwd("mods_fwd", c_all, mod_w, my_mod_b)
    mods_all = _exchange("gather_mods", _pack([mods_mine], F32, SMALL_ROW_MULTIPLE), scatter=False)
    mods_all = _unpack(mods_all, [mods_mine.shape])[0]
    mods = lax.dynamic_index_in_dim(mods_all, me, axis=2, keepdims=False)
    mods = jnp.moveaxis(mods, 0, 1).reshape(DEPTH, d6)

    tabq, tabk = _rope_tables(positions[0])
    half = N_DEV // 2
    up_cols = shard["w_up"].shape[2]
    cat = lambda a, axis, lo=0, hi=N_DEV: jnp.concatenate([a[j] for j in range(lo, hi)], axis=axis)

    def stage_pieces(l):
        out = {"pool_w": shard["pool_w"].astype(BF16)} if l == 0 else {}
        if l == N_A_LAYERS:
            out.update({k: shard[k].astype(BF16) for k in ("w_dkv", "w_uk", "w_uv")})
        if l >= N_A_LAYERS:
            out.update({k: shard[k][l - N_A_LAYERS].astype(BF16) for k in ("w_dq", "w_uq", "w_o")})
        out.update(w_up=shard["w_up"][l].astype(BF16), w_down=shard["w_down"][l].astype(BF16))
        return out

    gathers, pool_all = {}, []

    def start_gather(l, behind=None):
        pieces = stage_pieces(l)
        if behind is not None:
            pieces, _ = lax.optimization_barrier((pieces, behind))
        handles, token = _exchange_start(f"gather_start_{l}", list(pieces.values()), scatter=False)
        gathers[l] = (handles, pieces)
        return token[0, 0]

    def wait_gather(l, keys, after, tag=""):
        handles, pieces = gathers[l]
        which = [list(pieces).index(k) for k in keys]
        lands = _exchange_wait(f"gather_wait_{l}{tag}", handles, after, scatter=False, which=which)
        return dict(zip(keys, own_slot(lands, [pieces[k] for k in keys])))

    def whole_weights(l, got):
        w = dict(norm1_g=norm1_g[l], norm2_g=norm2_g[l], conv_w=vec["conv_w"][l], conv_b=conv_b[l])
        if l == 0:
            pool_all.append(got["pool_w"])
        if l < N_A_LAYERS:
            w.update(pool_w=cat(pool_all[0][:, l], 1), pool_b=vec["pool_b"][l], pool_scale=vec["pool_scale"][l])
        else:
            rope = got["w_uq"][..., QK_NOPE:]
            ext = jnp.concatenate([got["w_uq"][..., :QK_NOPE], rope, _swap_halves(rope)], axis=-1)
            w.update(w_dq=got["w_dq"].reshape(D_MODEL, Q_RANK), w_uq_ext=cat(ext, -1), w_o=got["w_o"].reshape(D_MODEL, D_MODEL),
                     q_norm_g=q_norm_g[l - N_A_LAYERS])
        if l == N_A_LAYERS:
            w.update(w_dkv_ext=_extend_w_dkv(got["w_dkv"].reshape(D_MODEL, KV_RANK + QK_ROPE)), w_uk=cat(got["w_uk"], -1),
                     w_uv=cat(got["w_uv"], -1), kv_in_g=kv_in_g, ckv_norm_g=ckv_norm_g)
        return w

    def own_slot(lands, own):
        return [lax.dynamic_update_index_in_dim(p, o, me, 0) for p, o in zip(lands, own)]

    def fetch(l, after):
        up_parts = lambda g8: (cat(g8, -1, 0, half), cat(g8, -1, half, N_DEV))
        if l == 0:
            start_gather(0, behind=mods)
            got = wait_gather(0, ["pool_w"], mods, "_pool")
            w_up = lambda aft: up_parts(wait_gather(0, ["w_up"], aft, "_up")["w_up"])
            w_down = lambda aft: wait_gather(0, ["w_down"], aft, "_down")["w_down"].reshape(D_FF, D_MODEL)
        else:
            got = wait_gather(l, list(gathers[l][1]), after)
            up, down = up_parts(got["w_up"]), got["w_down"].reshape(D_FF, D_MODEL)
            w_up, w_down = (lambda aft: up), (lambda aft: down)
        w = dict(whole_weights(l, got), w_up=w_up, w_down=w_down)
        return w, (start_gather(l + 1) if l + 1 < DEPTH else 0.0)

    scatters, pending, pool_grads, piece_grads = {}, {}, {}, {}

    def reduce_pieces(l, keys, got):
        for k, p in zip(keys, got):
            piece_grads[(k, l)] = _sum8(f"sum_grads_{k}_{l}", p.reshape(N_DEV, -1, p.shape[-1])).reshape(p.shape[1:])

    def start_scatter(name, sent):
        sent = {k: a.astype(BF16) for k, a in sent.items()}
        handles, token = _exchange_start(f"scatter_start_{name}", list(sent.values()), scatter=True)
        scatters[name] = (handles, list(sent), [lax.dynamic_index_in_dim(a, me, 0, keepdims=False) for a in sent.values()])
        return token[0, 0]

    def finish_scatter(name, l, after):
        handles, keys, own = scatters.pop(name)
        reduce_pieces(l, keys, own_slot(_exchange_wait(f"scatter_wait_{name}", handles, after, scatter=True), own))

    def push(l, part, big, after):
        cut = lambda a, n, axis: jnp.stack([lax.slice_in_dim(a, j * n, (j + 1) * n, axis=axis) for j in range(N_DEV)])
        sent = {}
        if part == "down":
            sent["w_down"] = big["w_down"].reshape(N_DEV, D_FF // N_DEV, D_MODEL)
        elif part == "up":
            sent["w_up"] = jnp.stack([lax.slice_in_dim(big[half_], j * up_cols, (j + 1) * up_cols, axis=1)
                                      for half_ in ("w_up_a", "w_up_v") for j in range(half)])
        elif l < N_A_LAYERS:
            pool_grads[l] = big["pool_w"]
        else:
            ext = cut(big["w_uq_ext"], Q_EXT, 1)
            rope = ext[..., QK_NOPE:QK_HEAD] + _unswap_halves(ext[..., QK_HEAD:])
            sent.update(w_dq=big["w_dq"].reshape(N_DEV, D_MODEL // N_DEV, Q_RANK), w_uq=jnp.concatenate([ext[..., :QK_NOPE], rope], axis=-1),
                        w_o=big["w_o"].reshape(N_DEV, D_MODEL // N_DEV, D_MODEL))
        if part == "mix" and l == N_A_LAYERS:
            sent.update(w_dkv=_fold_w_dkv_grad(big["w_dkv_ext"]).reshape(N_DEV, D_MODEL // N_DEV, KV_RANK + QK_ROPE),
                        w_uk=cut(big["w_uk"], QK_NOPE, 1), w_uv=cut(big["w_uv"], V_HEAD, 1))
        if l == 0 and part != "mix":
            return start_scatter(f"0_{part}", sent)
        if l == 0:
            finish_scatter("1", 1, after)
            pool = _shard8(jnp.stack([pool_grads[a] for a in range(N_A_LAYERS)]), 2).astype(BF16)
            reduce_pieces(0, ["pool_w"], _exchange_many("scatter_pool_grads", [pool], scatter=True))
            return 0.0
        pending.setdefault(l, {}).update(sent)
        if part != "mix":
            return 0.0
        if l + 1 < DEPTH:
            finish_scatter(str(l + 1), l + 1, after)
        return start_scatter(str(l), pending.pop(l))

    loss_row, dx, g, dmods = _forward_backward(x[0], loss_target[0], mods, tabq, tabk, norm1_g, final_g, fetch, push)
    layers_of = lambda k, ls: jnp.stack([piece_grads[(k, l)] for l in ls])
    grads = dict(w_dkv=piece_grads[("w_dkv", N_A_LAYERS)], w_uk=piece_grads[("w_uk", N_A_LAYERS)], w_uv=piece_grads[("w_uv", N_A_LAYERS)])
    for k in ("w_dq", "w_uq", "w_o"):
        grads[k] = layers_of(k, range(N_A_LAYERS, DEPTH))

    small_names = REPLICATED_WEIGHTS + tuple(k for k, _ in VECTOR_WEIGHTS)
    small_out = [dmods] + [g[k] for k in small_names] + [loss_row]
    small_shapes = [a.shape for a in small_out]
    small_got = _exchange("gather_small_grads", _pack(small_out, F32, SMALL_ROW_MULTIPLE), scatter=False)
    summed = _unpack(_sum8("sum_small_grads", small_got), small_shapes)
    grads["mod_b"] = summed[0]
    for k, s in zip(small_names, summed[1:-1]):
        grads[k] = s
    for k, ax in VECTOR_WEIGHTS:
        n = shard[k].shape[ax]
        grads[k] = lax.dynamic_slice_in_dim(grads[k], me * n, n, axis=ax)
    loss = summed[-1][0, 0]
    dmods_all = _unpack(small_got, small_shapes)[0]
    dm_mine = lax.dynamic_slice_in_dim(dmods_all, me * mod_cols, mod_cols, axis=2)
    dm_mine = jnp.pad(jnp.moveaxis(dm_mine, 0, 1), ((0, 0), (0, N_DEV), (0, 0)))
    grads["mod_w"] = _mods_bwd("mods_bwd", c_all, dm_mine)

    delta, new_m, new_v = {}, {}, {}

    def adamw(k):
        shp = shard[k].shape
        grads[k] = grads[k].reshape(shp)
        view = (lambda a: jnp.swapaxes(a, 1, 2)) if k == "w_up" else (lambda a: a)
        ops = [view(a) for a in (shard[k], grads[k], mom_m[k], mom_v[k])]
        ops[1] = lax.optimization_barrier(ops[1])
        res = _adamw(f"adamw_{k}", *[_as_2d(a) for a in ops])
        delta[k], new_m[k], new_v[k] = [view(r.reshape(ops[0].shape)) for r in res]
        grads[k] = view(ops[1])

    late = ("w_up", "w_down", "pool_w")
    for k in WEIGHT_ORDER:
        if k not in late:
            adamw(k)
    finish_scatter("0_down", 0, delta["final_g"])
    finish_scatter("0_up", 0, delta["final_g"])
    grads.update(w_up=layers_of("w_up", range(DEPTH)), w_down=layers_of("w_down", range(DEPTH)), pool_w=piece_grads[("pool_w", 0)])
    for k in late:
        adamw(k)
    return (loss, dx[None], *[grads[k] for k in WEIGHT_ORDER], *[delta[k] for k in WEIGHT_ORDER],
            *[new_m[k] for k in WEIGHT_ORDER], *[new_v[k] for k in WEIGHT_ORDER])
```

```python
import functools

import jax
import jax.numpy as jnp
from jax import lax
from jax.experimental import pallas as pl
from jax.experimental.pallas import tpu as pltpu

F32 = jnp.float32
BF16 = jnp.bfloat16

D_MODEL = 1024
DEPTH = 4
N_A_LAYERS = 2
N_B_LAYERS = 2
POOL_WINDOWS = (2, 4, 8, 16)
POOL_GROUP = 256
N_HEADS = 8
QK_NOPE = 128
QK_ROPE = 64
V_HEAD = 128
QK_HEAD = QK_NOPE + QK_ROPE
Q_RANK = 384
KV_RANK = 256
ROPE_THETA = 10000.0
D_FF = 2816
EPS = 1e-6
N_MOD = 6
ADAM_LR = 0.001
ADAM_B1 = 0.9
ADAM_B2 = 0.999
ADAM_EPS = 1e-08
ADAM_WD = 0.01
ADAM_STEP = 10

N_DEV = 8
LANES = 128
Q_EXT = 256
VMEM_LIMIT_BYTES = 48 * 1024 * 1024
MESH = pl.DeviceIdType.MESH
NEG_BIG = -0.7 * float(jnp.finfo(jnp.float32).max)


def _params(sem):
    return pltpu.CompilerParams(dimension_semantics=sem, vmem_limit_bytes=VMEM_LIMIT_BYTES)


def _tile(n, cap):
    if n <= cap:
        return n
    best = None
    for d in range(LANES, cap + 1, LANES):
        if n % d == 0:
            best = d
    assert best is not None, (n, cap)
    return best


def _dot(a, b, dims):
    return lax.dot_general(a, b, (dims, ((), ())), preferred_element_type=F32)


NN = ((1,), (0,))
NT = ((1,), (1,))
TN = ((0,), (0,))


def _modulated_rmsnorm(xv, gv, scale, shift):
    return xv * lax.rsqrt(jnp.mean(xv * xv, axis=-1, keepdims=True) + EPS) * gv * (1.0 + scale) + shift


def _mm(name, a, b, mode="nn", out_dtype=BF16, resid=None, gate=None, norm=None, rowtab=None, second=None, a_scale=None,
        gate_grad=None, tm_cap=1024, tn_cap=1408, tk_cap=1408):
    if mode == "tn":
        kdim, m = a.shape
    else:
        m, kdim = a.shape
    n = b.shape[0] if mode == "nt" else b.shape[1]
    tm, tn, tk = _tile(m, tm_cap), _tile(n, tn_cap), _tile(kdim, tk_cap)
    nk = kdim // tk
    dims = {"nn": NN, "nt": NT, "tn": TN}[mode]
    a_spec = pl.BlockSpec((tk, tm), lambda i, j, k: (k, i)) if mode == "tn" else pl.BlockSpec((tm, tk), lambda i, j, k: (i, k))
    b_spec = pl.BlockSpec((tn, tk), lambda i, j, k: (j, k)) if mode == "nt" else pl.BlockSpec((tk, tn), lambda i, j, k: (k, j))
    o_spec = pl.BlockSpec((tm, tn), lambda i, j, k: (i, j))
    g_spec = pl.BlockSpec((1, tn), lambda i, j, k: (0, j))
    gated = resid is not None
    assert sum(x is not None for x in (resid, rowtab, gate_grad)) <= 1
    n_ops = 2 if second is None else 4
    n_extra = 1 if a_scale is not None else 0

    def body(*refs):
        acc = refs[-1]
        i, k = pl.program_id(0), pl.program_id(2)

        @pl.when(k == 0)
        def _():
            acc[...] = jnp.zeros_like(acc)

        av = refs[0][...]
        if a_scale is not None:
            av = av.astype(F32) * refs[n_ops][...]
        prod = _dot(av.astype(BF16), refs[1][...].astype(BF16), dims)
        if second is not None:
            prod = prod + _dot(refs[2][...].astype(BF16), refs[3][...].astype(BF16), dims)
        acc[...] += prod
        rest = refs[n_ops + n_extra:-1]

        if gate_grad is not None:
            @pl.when((i == 0) & (k == 0))
            def _():
                rest[3][...] = jnp.zeros_like(rest[3])

        @pl.when(k == nk - 1)
        def _():
            if gated and norm is not None:
                r_ref, g_ref, ng_ref, sc_ref, sh_ref, x_ref, h_ref = rest
                xn = r_ref[...] + g_ref[...] * acc[...]
                x_ref[...] = xn
                h_ref[...] = _modulated_rmsnorm(xn, ng_ref[...], sc_ref[...], sh_ref[...]).astype(h_ref.dtype)
            elif gated:
                r_ref, g_ref, x_ref = rest
                x_ref[...] = r_ref[...] + g_ref[...] * acc[...]
            elif rowtab is not None:
                tab = rest[0][...]
                rest[1][...] = (acc[...] * jnp.concatenate([tab] * (tn // tab.shape[1]), axis=1)).astype(out_dtype)
            elif gate_grad is not None:
                w_ref, g_ref, o_ref, dg_ref = rest
                o_ref[...] = (acc[...] * g_ref[...]).astype(out_dtype)
                dg_ref[...] += _colsum(w_ref[...].astype(F32) * acc[...])
            else:
                rest[0][...] = acc[...].astype(out_dtype)

    ins, in_specs = [a, b], [a_spec, b_spec]
    if second is not None:
        assert second[0].shape == a.shape and second[1].shape == b.shape
        ins += list(second)
        in_specs += [a_spec, b_spec]
    if a_scale is not None:
        assert mode != "tn"
        ins.append(a_scale)
        in_specs.append(pl.BlockSpec((1, tk), lambda i, j, k: (0, k)))
    out_shape, out_specs = jax.ShapeDtypeStruct((m, n), out_dtype), o_spec
    sem = ("parallel", "parallel", "arbitrary")
    if rowtab is not None:
        assert tn % rowtab.shape[1] == 0
        ins.append(rowtab)
        in_specs.append(pl.BlockSpec((tm, rowtab.shape[1]), lambda i, j, k: (i, 0)))
    if gated:
        ins += [resid, gate]
        in_specs += [o_spec, g_spec]
        out_shape = jax.ShapeDtypeStruct((m, n), F32)
    if norm is not None:
        assert gated and tn == n
        ins += list(norm[:3])
        in_specs += [g_spec] * 3
        out_shape = (out_shape, jax.ShapeDtypeStruct((m, n), norm[3]))
        out_specs = (o_spec, o_spec)
    if gate_grad is not None:
        assert mode == "tn" and tn == n
        ins += list(gate_grad)
        in_specs += [o_spec, g_spec]
        out_shape = (out_shape, jax.ShapeDtypeStruct((1, n), F32))
        out_specs = (o_spec, g_spec)
        sem = ("arbitrary", "arbitrary", "arbitrary")
    return pl.pallas_call(
        body, name=name, grid=(m // tm, n // tn, nk), in_specs=in_specs, out_specs=out_specs, out_shape=out_shape,
        scratch_shapes=[pltpu.VMEM((tm, tn), F32)],
        compiler_params=_params(sem),
    )(*ins)


def _rowwise(name, fn, tiled, bcast, outs, sums=(), tr=512):
    tiled = [t if isinstance(t, tuple) else (t, t.shape[1], 0) for t in tiled]
    s = tiled[0][0].shape[0]
    tr = min(tr, s)
    assert s % tr == 0
    n_t, n_b, n_o = len(tiled), len(bcast), len(outs)

    def body(*refs):
        i = pl.program_id(0)
        vals = [r[...] for r in refs[:n_t + n_b]]
        o_vals, s_vals = fn(*vals)
        for r, v in zip(refs[n_t + n_b:n_t + n_b + n_o], o_vals):
            r[...] = v.astype(r.dtype)
        s_refs = refs[n_t + n_b + n_o:]

        @pl.when(i == 0)
        def _():
            for r in s_refs:
                r[...] = jnp.zeros_like(r)

        for r, v in zip(s_refs, s_vals):
            r[...] += v

    in_specs = [pl.BlockSpec((tr, n), functools.partial(lambda cb, i: (i, cb), cb)) for (_, n, cb) in tiled]
    in_specs += [pl.BlockSpec(b.shape, functools.partial(lambda nd, i: (0,) * nd, b.ndim)) for b in bcast]
    out_specs = [pl.BlockSpec((tr, n), lambda i: (i, 0)) for (n, _) in outs]
    out_specs += [pl.BlockSpec((1, n), lambda i: (0, 0)) for n in sums]
    out_shape = [jax.ShapeDtypeStruct((s, n), dt) for (n, dt) in outs]
    out_shape += [jax.ShapeDtypeStruct((1, n), F32) for n in sums]
    res = pl.pallas_call(
        body, name=name, grid=(s // tr,), in_specs=in_specs, out_specs=tuple(out_specs), out_shape=tuple(out_shape),
        compiler_params=_params(("arbitrary",)),
    )(*[t[0] for t in tiled], *bcast)
    return res


def _colsum(v):
    return jnp.sum(v, axis=0, keepdims=True)


def _rms_fwd(name, x, g, scale=None, shift=None, out_dtype=BF16, ncols=None):
    mod = scale is not None

    def fn(xv, gv, *ss):
        if mod:
            return (_modulated_rmsnorm(xv, gv, ss[0], ss[1]),), ()
        return (xv * lax.rsqrt(jnp.mean(xv * xv, axis=-1, keepdims=True) + EPS) * gv,), ()

    n = ncols or x.shape[1]
    return _rowwise(name, fn, [(x, n, 0)], [g] + ([scale, shift] if mod else []), [(n, out_dtype)])[0]


def _rms_bwd(name, x, g, dh, scale=None, dx_in=None, ncols=None, out_dtype=F32):
    mod = scale is not None
    has_in = dx_in is not None

    def fn(*vals):
        xv, dhv = vals[0], vals[1].astype(F32)
        rest = list(vals[2:])
        dxi = rest.pop(0) if has_in else None
        gv = rest.pop(0)
        rstd = lax.rsqrt(jnp.mean(xv * xv, axis=-1, keepdims=True) + EPS)
        xhat = xv * rstd
        sums = []
        if mod:
            sc = rest.pop(0)
            dyn = dhv * (1.0 + sc)
            dshift, dscale = _colsum(dhv), _colsum(dhv * (xhat * gv))
        else:
            dyn = dhv
        dg = _colsum(dyn * xhat)
        dxhat = dyn * gv
        dx = rstd * (dxhat - xhat * jnp.mean(dxhat * xhat, axis=-1, keepdims=True))
        if has_in:
            dx = dx + dxi
        sums = [dg] + ([dshift, dscale] if mod else [])
        return (dx,), sums

    n = ncols or x.shape[1]
    tiled = [(x, n, 0), dh] + ([dx_in] if has_in else [])
    return _rowwise(name, fn, tiled, [g] + ([scale] if mod else []), [(n, out_dtype)], [n] * (3 if mod else 1))


def _loss_head(name, x, g, target):
    n = x.shape[1]

    def fn(xv, tv, gv):
        rstd = lax.rsqrt(jnp.mean(xv * xv, axis=-1, keepdims=True) + EPS)
        xhat = xv * rstd
        err = xhat * gv - tv
        loss = 0.5 * jnp.sum(jnp.sum(err * err, axis=-1, keepdims=True) / n, axis=0, keepdims=True)
        dy = err / n
        dg = _colsum(dy * xhat)
        dxhat = dy * gv
        dx = rstd * (dxhat - xhat * jnp.mean(dxhat * xhat, axis=-1, keepdims=True))
        return (dx,), (dg, jnp.broadcast_to(loss, (1, LANES)))

    return _rowwise(name, fn, [x, target], [g], [(n, F32)], [n, LANES])


def _krope_fwd(name, kv_ext, tabk):
    def fn(xv, tv):
        t = xv * tv
        return (t + pltpu.roll(t, 64, 1),), ()

    return _rowwise(name, fn, [(kv_ext, LANES, 2), tabk], [], [(LANES, BF16)])[0]


def _krope_bwd(name, dkd, tabk):
    def fn(dv, tv):
        d = dv[:, :LANES]
        for h in range(1, N_HEADS):
            d = d + dv[:, h * LANES:(h + 1) * LANES]
        return ((d + pltpu.roll(d, 64, 1)) * tv,), ()

    return _rowwise(name, fn, [dkd, tabk], [], [(LANES, F32)])[0]


def _adamw(name, w, g, m, v):
    def fn(wv, gv, mv, vv):
        m2 = ADAM_B1 * mv + (1.0 - ADAM_B1) * gv
        v2 = ADAM_B2 * vv + (1.0 - ADAM_B2) * (gv * gv)
        m_hat = m2 / (1.0 - ADAM_B1 ** ADAM_STEP)
        v_hat = v2 / (1.0 - ADAM_B2 ** ADAM_STEP)
        delta = -ADAM_LR * (m_hat / (jnp.sqrt(v_hat) + ADAM_EPS) + ADAM_WD * wv)
        return (delta, m2, v2), ()

    r, c = w.shape
    tr = r
    for cand in (512, 256, 128, 64, 32, 16, 8):
        if r % cand == 0 and r > cand:
            tr = cand
            break
    return _rowwise(name, fn, [w, g, m, v], [], [(c, F32)] * 3, tr=tr)


def _sum8(name, parts):
    _, r, c = parts.shape
    tr = r
    for cand in (2048, 1024, 512, 256, 128, 64, 32, 16):
        if r % cand == 0 and r > cand and cand * c <= 256 * 1024:
            tr = cand
            break

    def body(p_ref, o_ref):
        acc = p_ref[0].astype(F32)
        for k in range(1, N_DEV):
            acc = acc + p_ref[k].astype(F32)
        o_ref[...] = acc

    return pl.pallas_call(
        body, name=name, grid=(r // tr,), in_specs=[pl.BlockSpec((N_DEV, tr, c), lambda i: (0, i, 0))],
        out_specs=pl.BlockSpec((tr, c), lambda i: (i, 0)), out_shape=jax.ShapeDtypeStruct((r, c), F32),
        compiler_params=_params(("parallel",)),
    )(parts)


def _mods_fwd(name, c_all, w, b):
    depth, d, n = w.shape

    def body(c_ref, w_ref, b_ref, o_ref):
        cv = c_ref[...]
        sc = (cv * (1.0 / (1.0 + jnp.exp(-cv)))).astype(BF16)
        o_ref[0] = _dot(sc, w_ref[0].astype(BF16), NN) + b_ref[0]

    return pl.pallas_call(
        body, name=name, grid=(depth,),
        in_specs=[pl.BlockSpec(c_all.shape, lambda l: (0, 0)), pl.BlockSpec((1, d, n), lambda l: (l, 0, 0)),
                  pl.BlockSpec((1, 1, n), lambda l: (l, 0, 0))],
        out_specs=pl.BlockSpec((1, c_all.shape[0], n), lambda l: (l, 0, 0)),
        out_shape=jax.ShapeDtypeStruct((depth, c_all.shape[0], n), F32),
        compiler_params=_params(("parallel",)),
    )(c_all, w, b.reshape(depth, 1, n))


def _mods_bwd(name, c_all, dm):
    depth, rows, n = dm.shape
    d = c_all.shape[1]

    def body(c_ref, dm_ref, o_ref):
        cv = c_ref[...]
        sc = (cv * (1.0 / (1.0 + jnp.exp(-cv)))).astype(BF16)
        o_ref[0] = _dot(sc, dm_ref[0].astype(BF16), TN)

    return pl.pallas_call(
        body, name=name, grid=(depth,),
        in_specs=[pl.BlockSpec(c_all.shape, lambda l: (0, 0)), pl.BlockSpec((1, rows, n), lambda l: (l, 0, 0))],
        out_specs=pl.BlockSpec((1, d, n), lambda l: (l, 0, 0)),
        out_shape=jax.ShapeDtypeStruct((depth, d, n), F32),
        compiler_params=_params(("parallel",)),
    )(c_all, dm)


POOL_TILE = 256


def _split_dot(band, val):
    hi = val.astype(BF16)
    lo = (val - hi.astype(F32)).astype(BF16)
    return _dot(band, hi, NN) + _dot(band, lo, NN)


def _pool_fwd(name, h1, x, pw, pb, ps, g1, norm):
    s, d = h1.shape
    t = POOL_TILE

    def body(hc_ref, hp_ref, x_ref, pw_ref, pb_ref, ps_ref, g_ref, ng_ref, sc_ref, sh_ref, xo_ref, zb_ref, pooled_ref, h2_ref):
        i = pl.program_id(0)
        r = lax.broadcasted_iota(jnp.int32, (t, t), 0)
        j = lax.broadcasted_iota(jnp.int32, (t, t), 1)
        pos = (i * t + lax.broadcasted_iota(jnp.int32, (t, 1), 0) + 1).astype(F32)
        has_prev = (i > 0).astype(F32)
        for grp, w in enumerate(POOL_WINDOWS):
            cs = slice(grp * POOL_GROUP, (grp + 1) * POOL_GROUP)
            hc = hc_ref[:, cs]
            band_cur = ((r - j >= 0) & (r - j < w)).astype(BF16)
            band_prev = (r + t - j < w).astype(BF16)
            ssum = _split_dot(band_cur, hc) + has_prev * _split_dot(band_prev, hp_ref[:, cs])
            pooled = (ssum / jnp.minimum(pos, float(w)) - hc).astype(BF16)
            zb = _dot(pooled, pw_ref[grp], NN) + pb_ref[:, cs]
            xo_ref[:, cs] = x_ref[:, cs] + g_ref[:, cs] * (zb * ps_ref[:, cs])
            zb_ref[:, cs] = zb
            pooled_ref[:, cs] = pooled
        h2_ref[...] = _modulated_rmsnorm(xo_ref[...], ng_ref[...], sc_ref[...], sh_ref[...]).astype(h2_ref.dtype)

    row = pl.BlockSpec((t, d), lambda i: (i, 0))
    vec = pl.BlockSpec((1, d), lambda i: (0, 0))
    return pl.pallas_call(
        body, name=name, grid=(s // t,),
        in_specs=[row, pl.BlockSpec((t, d), lambda i: (jnp.maximum(i - 1, 0), 0)), row,
                  pl.BlockSpec(pw.shape, lambda i: (0, 0, 0)), vec, vec, vec, vec, vec, vec],
        out_specs=(row, row, row, row),
        out_shape=(jax.ShapeDtypeStruct((s, d), F32), jax.ShapeDtypeStruct((s, d), F32), jax.ShapeDtypeStruct((s, d), BF16),
                   jax.ShapeDtypeStruct((s, d), BF16)),
        compiler_params=_params(("parallel",)),
    )(h1, h1, x, pw, pb, ps, g1, *norm)


def _pool_bwd(name, dxn, zb, pooled, pw, ps, g1):
    s, d = dxn.shape
    t = POOL_TILE
    nt = s // t

    def body(dc_ref, dn_ref, zb_ref, pooled_ref, pw_ref, ps_ref, g_ref, dh_ref, dpw_ref, dpb_ref, dps_ref, dg_ref):
        i = pl.program_id(0)

        @pl.when(i == 0)
        def _():
            dpw_ref[...] = jnp.zeros_like(dpw_ref)
            dpb_ref[...] = jnp.zeros_like(dpb_ref)
            dps_ref[...] = jnp.zeros_like(dps_ref)
            dg_ref[...] = jnp.zeros_like(dg_ref)

        jj = lax.broadcasted_iota(jnp.int32, (t, t), 0)
        rr = lax.broadcasted_iota(jnp.int32, (t, t), 1)
        pos = (i * t + lax.broadcasted_iota(jnp.int32, (t, 1), 0) + 1).astype(F32)
        has_next = (i < nt - 1).astype(F32)
        for grp, w in enumerate(POOL_WINDOWS):
            cs = slice(grp * POOL_GROUP, (grp + 1) * POOL_GROUP)
            gv, psv, zbv, dxc = g_ref[:, cs], ps_ref[:, cs], zb_ref[:, cs], dc_ref[:, cs]
            dg_ref[:, cs] += _colsum(dxc * (zbv * psv))
            dy = gv * dxc
            dps_ref[:, cs] += _colsum(dy * zbv)
            dz = dy * psv
            dpb_ref[:, cs] += _colsum(dz)
            dzb = dz.astype(BF16)
            dpw_ref[grp] += _dot(pooled_ref[:, cs], dzb, TN)
            dp = _dot(dzb, pw_ref[grp], NT)
            dzn = (gv * dn_ref[:, cs] * psv).astype(BF16)
            dpn = _dot(dzn, pw_ref[grp], NT) * (has_next / float(w))
            band_cur = ((rr - jj >= 0) & (rr - jj < w)).astype(BF16)
            band_next = (rr + t - jj < w).astype(BF16)
            dh_ref[:, cs] = _split_dot(band_cur, dp / jnp.minimum(pos, float(w))) + _split_dot(band_next, dpn) - dp

    row = pl.BlockSpec((t, d), lambda i: (i, 0))
    vec = pl.BlockSpec((1, d), lambda i: (0, 0))
    wspec = pl.BlockSpec(pw.shape, lambda i: (0, 0, 0))
    return pl.pallas_call(
        body, name=name, grid=(nt,),
        in_specs=[row, pl.BlockSpec((t, d), lambda i: (jnp.minimum(i + 1, nt - 1), 0)), row, row, wspec, vec, vec],
        out_specs=(row, wspec, vec, vec, vec),
        out_shape=(jax.ShapeDtypeStruct((s, d), F32), jax.ShapeDtypeStruct(pw.shape, F32),
                   jax.ShapeDtypeStruct((1, d), F32), jax.ShapeDtypeStruct((1, d), F32), jax.ShapeDtypeStruct((1, d), F32)),
        compiler_params=_params(("arbitrary",)),
    )(dxn, dxn, zb, pooled, pw, ps, g1)


GLU_TILE = 512
HALO = 16
INV_SQRT2 = 0.7071067811865476
INV_SQRT_2PI = 0.3989422804014327


def _up_glu_fwd(name, h2, wt, cw, cb):
    s, d = h2.shape
    f = wt.shape[0] // 2
    tm, tn = _tile(s, GLU_TILE), _tile(f, 1408)

    def body(h_ref, hh_ref, wa_ref, wv_ref, cw_ref, cb_ref, ua_ref, gl_ref, gpv_ref, ge_ref):
        i = pl.program_id(1)
        has_prev = (i > 0).astype(F32)
        a = _dot(h_ref[...], wa_ref[...], NT).astype(BF16)
        v = _dot(h_ref[...], wv_ref[...], NT)
        above = (_dot(hh_ref[...], wa_ref[...], NT) * has_prev).astype(BF16)
        ua_ref[...] = a
        ext = jnp.concatenate([above.astype(F32), a.astype(F32)], axis=0)
        e1 = pltpu.roll(ext, 1, 0)[HALO:]
        e2 = pltpu.roll(ext, 2, 0)[HALO:]
        pre = e2 * cw_ref[0:1, :] + e1 * cw_ref[1:2, :] + ext[HALO:] * cw_ref[2:3, :] + cb_ref[...]
        cdf = 0.5 * (1.0 + lax.erf(pre * INV_SQRT2))
        ge = pre * cdf
        gl_ref[...] = (ge * v).astype(gl_ref.dtype)
        gpv_ref[...] = ((cdf + pre * (INV_SQRT_2PI * jnp.exp(-0.5 * pre * pre))) * v).astype(gpv_ref.dtype)
        ge_ref[...] = ge.astype(ge_ref.dtype)

    blk = pl.BlockSpec((tm, tn), lambda j, i: (i, j))
    return pl.pallas_call(
        body, name=name, grid=(f // tn, s // tm),
        in_specs=[pl.BlockSpec((tm, d), lambda j, i: (i, 0)), pl.BlockSpec((HALO, d), lambda j, i: (jnp.maximum(i * (tm // HALO) - 1, 0), 0)),
                  pl.BlockSpec((tn, d), lambda j, i: (j, 0)), pl.BlockSpec((tn, d), lambda j, i: (j + f // tn, 0)),
                  pl.BlockSpec((3, tn), lambda j, i: (0, j)), pl.BlockSpec((1, tn), lambda j, i: (0, j))],
        out_specs=(blk, blk, blk, blk), out_shape=tuple(jax.ShapeDtypeStruct((s, f), BF16) for _ in range(4)),
        compiler_params=_params(("parallel", "parallel")),
    )(h2, h2, wt, wt, cw, cb)


def _down_glu_bwd(name, dx, gate, wd, ua, gpv, ge, cw):
    s, f = ua.shape
    d = dx.shape[1]
    t, tf = min(GLU_TILE, s), _tile(f, 1408)
    nt = s // t
    te = t + HALO

    def body(dy_ref, dyn_ref, gate_ref, wd_ref, a_ref, ah_ref, g_ref, gn_ref, ge_ref, cw_ref, da_ref, dv_ref, dcw_ref, dcb_ref):
        i = pl.program_id(1)

        @pl.when(i == 0)
        def _():
            dcw_ref[...] = jnp.zeros_like(dcw_ref)
            dcb_ref[...] = jnp.zeros_like(dcb_ref)

        has_prev = (i > 0).astype(F32)
        has_next = (i < nt - 1).astype(F32)
        wdv = wd_ref[...]
        dgl = _dot((dy_ref[...] * gate_ref[...]).astype(BF16), wdv, NT)
        dgl_below = _dot((dyn_ref[...] * gate_ref[...]).astype(BF16), wdv, NT) * has_next
        dpre = jnp.concatenate([dgl * g_ref[...].astype(F32), dgl_below * gn_ref[...].astype(F32)], axis=0)
        c0, c1, c2 = cw_ref[0:1, :], cw_ref[1:2, :], cw_ref[2:3, :]
        up1 = pltpu.roll(dpre, te - 1, 0)
        up2 = pltpu.roll(dpre, te - 2, 0)
        da_ref[...] = (dpre * c2 + up1 * c1 + up2 * c0)[:t].astype(da_ref.dtype)
        dv_ref[...] = (dgl * ge_ref[...].astype(F32)).astype(dv_ref.dtype)
        ext = jnp.concatenate([ah_ref[...].astype(F32) * has_prev, a_ref[...].astype(F32)], axis=0)
        dpt = dpre[:t]
        dcb_ref[...] += _colsum(dpt)
        dcw_ref[0:1, :] += _colsum(pltpu.roll(ext, 2, 0)[HALO:] * dpt)
        dcw_ref[1:2, :] += _colsum(pltpu.roll(ext, 1, 0)[HALO:] * dpt)
        dcw_ref[2:3, :] += _colsum(ext[HALO:] * dpt)

    blk = pl.BlockSpec((t, tf), lambda j, i: (i, j))
    prev = pl.BlockSpec((HALO, tf), lambda j, i: (jnp.maximum(i * (t // HALO) - 1, 0), j))
    below = lambda i: jnp.minimum((i + 1) * (t // HALO), s // HALO - 1)
    w3 = pl.BlockSpec((3, tf), lambda j, i: (0, j))
    w1 = pl.BlockSpec((1, tf), lambda j, i: (0, j))
    return pl.pallas_call(
        body, name=name, grid=(f // tf, nt),
        in_specs=[pl.BlockSpec((t, d), lambda j, i: (i, 0)), pl.BlockSpec((HALO, d), lambda j, i: (below(i), 0)),
                  pl.BlockSpec((1, d), lambda j, i: (0, 0)), pl.BlockSpec((tf, d), lambda j, i: (j, 0)), blk, prev, blk,
                  pl.BlockSpec((HALO, tf), lambda j, i: (below(i), j)), blk, w3],
        out_specs=(blk, blk, w3, w1),
        out_shape=(jax.ShapeDtypeStruct((s, f), BF16), jax.ShapeDtypeStruct((s, f), BF16),
                   jax.ShapeDtypeStruct((3, f), F32), jax.ShapeDtypeStruct((1, f), F32)),
        compiler_params=_params(("parallel", "arbitrary")),
    )(dx, dx, gate, wd, ua, ua, gpv, gpv, ge, cw)


ATT_TILE = 512
ATT_ROWS = 256
ATT_HEADS = 4
ATT_BWD_HEADS = 2
ATT_BWD_VMEM_BYTES = 58 * 1024 * 1024
LOG2E = 1.4426950408889634
LN2 = 0.6931471805599453


def _head_blocks_t(a, width):
    s = a.shape[0]
    t = min(ATT_TILE, s)
    return a.reshape(s // t, t, N_HEADS, width).transpose(2, 0, 3, 1)


def _causal_mask(sv, q0, k0):
    row = q0 + lax.broadcasted_iota(jnp.int32, sv.shape, 0)
    col = k0 + lax.broadcasted_iota(jnp.int32, sv.shape, 1)
    return jnp.where(col <= row, sv, NEG_BIG)


def _attn_fwd(name, q_rot, kt4, v_ext):
    s = q_rot.shape[0]
    t = min(ATT_TILE, s)
    nq = s // t
    rq = min(ATT_ROWS, t)
    nh = ATT_HEADS

    def body(q_ref, kt_ref, v_ref, o_ref, row_ref, acc_ref, m_ref):
        qi = pl.program_id(1)
        acc_ref[...] = jnp.zeros_like(acc_ref)
        m_ref[...] = jnp.full_like(m_ref, NEG_BIG)

        def step(j, masked):
            for hh in range(nh):
                cols = slice(hh * Q_EXT, (hh + 1) * Q_EXT)
                v_blk = v_ref[pl.ds(pl.multiple_of(j * t, t), t), cols]
                for r in range(t // rq):
                    rs = pl.ds(r * rq, rq)
                    sv = _dot(q_ref[rs, cols], kt_ref[hh, j], NN)
                    if masked:
                        sv = _causal_mask(sv, r * rq, 0)
                    m_prev = m_ref[hh, rs, :]
                    m_new = jnp.maximum(m_prev, jnp.max(sv, axis=-1, keepdims=True))
                    p = jnp.exp2(sv - m_new).astype(BF16)
                    acc_ref[hh, rs, :] = jnp.exp2(m_prev - m_new) * acc_ref[hh, rs, :] + _dot(p, v_blk, NN)
                    m_ref[hh, rs, :] = m_new

        def full_step(j, carry):
            step(j, False)
            return carry

        lax.fori_loop(0, qi, full_step, 0)
        step(qi, True)
        for hh in range(nh):
            l = acc_ref[hh, :, V_HEAD:V_HEAD + 1]
            o_ref[:, hh * V_HEAD:(hh + 1) * V_HEAD] = (acc_ref[hh, :, :V_HEAD] / l).astype(o_ref.dtype)
            lse = jnp.broadcast_to(m_ref[hh] + jnp.log(l) * LOG2E, (t, LANES))
            row_ref[hh, 0] = jnp.transpose(lse)[0:8, :]

    return pl.pallas_call(
        body, name=name, grid=(N_HEADS // nh, nq),
        in_specs=[pl.BlockSpec((t, nh * Q_EXT), lambda h, i: (i, h)), pl.BlockSpec((nh, nq, Q_EXT, t), lambda h, i: (h, 0, 0, 0)),
                  pl.BlockSpec((s, nh * Q_EXT), lambda h, i: (0, h))],
        out_specs=(pl.BlockSpec((t, nh * V_HEAD), lambda h, i: (i, h)), pl.BlockSpec((nh, 1, 8, t), lambda h, i: (h, i, 0, 0))),
        out_shape=(jax.ShapeDtypeStruct((s, N_HEADS * V_HEAD), BF16), jax.ShapeDtypeStruct((N_HEADS, nq, 8, t), F32)),
        scratch_shapes=[pltpu.VMEM((nh, t, Q_EXT), F32), pltpu.VMEM((nh, t, 1), F32)],
        compiler_params=_params(("parallel", "parallel")),
    )(q_rot, kt4, v_ext)


def _attn_delta(name, o, do):
    s = o.shape[0]
    t = min(ATT_TILE, s)

    def body(o_ref, do_ref, delta_ref):
        prod = do_ref[...].astype(F32) * o_ref[...].astype(F32)
        for h in range(N_HEADS):
            delta = jnp.sum(prod[:, h * V_HEAD:(h + 1) * V_HEAD], axis=-1, keepdims=True)
            delta_ref[h, 0] = jnp.transpose(jnp.broadcast_to(delta, (t, LANES)))[0:8, :]

    rows = pl.BlockSpec((t, N_HEADS * V_HEAD), lambda i: (i, 0))
    return pl.pallas_call(
        body, name=name, grid=(s // t,), in_specs=[rows, rows],
        out_specs=pl.BlockSpec((N_HEADS, 1, 8, t), lambda i: (0, i, 0, 0)),
        out_shape=jax.ShapeDtypeStruct((N_HEADS, s // t, 8, t), F32),
        compiler_params=_params(("parallel",)),
    )(o, do)


def _attn_bwd(name, kfull, v, qt4, q_rot, dot4, do, lse_row, delta_row, tabq, acc_in=None):
    s = kfull.shape[0]
    t = min(ATT_TILE, s)
    nq = s // t
    nh = ATT_BWD_HEADS
    has_in = acc_in is not None

    def body(*refs):
        k_ref, v_ref, qt_ref, q_ref, dot_ref, do_ref, lse_ref, delta_ref, tab_ref = refs[:9]
        dq_ref, dkn_ref, dkd_ref, dv_ref, dq_acc_ref, acck_ref, accv_ref = refs[-7:]
        kj = pl.program_id(1)

        @pl.when(kj == 0)
        def _():
            dq_acc_ref[...] = jnp.zeros_like(dq_acc_ref)

        acck_ref[...] = jnp.zeros_like(acck_ref)
        accv_ref[...] = jnp.zeros_like(accv_ref)

        def step(i, masked):
            qs = pl.ds(pl.multiple_of(i * t, t), t)
            for hh in range(nh):
                qc = slice(hh * Q_EXT, (hh + 1) * Q_EXT)
                vc = slice(hh * LANES, (hh + 1) * LANES)
                k_blk = k_ref[:, qc]
                st = _dot(k_blk, qt_ref[hh, i], NN)
                if masked:
                    krow = lax.broadcasted_iota(jnp.int32, st.shape, 0)
                    qcol = lax.broadcasted_iota(jnp.int32, st.shape, 1)
                    st = jnp.where(krow <= qcol, st, NEG_BIG)
                pt = jnp.exp2(st - lse_ref[hh, i, 0:1, :])
                accv_ref[hh] += _dot(pt.astype(BF16), do_ref[qs, vc], NN)
                dpt = _dot(v_ref[:, vc], dot_ref[hh, i], NN)
                dst = (pt * (dpt - delta_ref[hh, i, 0:1, :])).astype(BF16)
                acck_ref[hh] += _dot(dst, q_ref[qs, qc], NN)
                dq_acc_ref[hh, qs, :] += _dot(dst, k_blk, TN)

        def full_step(i, carry):
            step(i, False)
            return carry

        step(kj, True)
        lax.fori_loop(kj + 1, nq, full_step, 0)
        for hh in range(nh):
            vc = slice(hh * LANES, (hh + 1) * LANES)
            dk = acck_ref[hh] * LN2
            dkn, dkd, dv = dk[:, :QK_NOPE], dk[:, QK_NOPE:], accv_ref[hh]
            if has_in:
                dkn, dkd, dv = dkn + refs[9][:, vc], dkd + refs[10][:, vc], dv + refs[11][:, vc]
            dkn_ref[:, vc], dkd_ref[:, vc], dv_ref[:, vc] = dkn, dkd, dv

        @pl.when(kj == nq - 1)
        def _():
            for hh in range(nh):
                dq_ref[:, hh * Q_EXT:(hh + 1) * Q_EXT] = (dq_acc_ref[hh] * (tab_ref[...] * LN2)).astype(dq_ref.dtype)

    kblk = pl.BlockSpec((t, nh * LANES), lambda h, j: (j, h))
    col = pl.BlockSpec((s, nh * LANES), lambda h, j: (0, h))
    q_all = pl.BlockSpec((s, nh * Q_EXT), lambda h, j: (0, h))
    stat = pl.BlockSpec((nh, nq, 8, t), lambda h, j: (h, 0, 0, 0))
    ins = [kfull, v, qt4, q_rot, dot4, do, lse_row, delta_row, tabq]
    in_specs = [pl.BlockSpec((t, nh * Q_EXT), lambda h, j: (j, h)), kblk, pl.BlockSpec((nh, nq, Q_EXT, t), lambda h, j: (h, 0, 0, 0)),
                q_all, pl.BlockSpec((nh, nq, V_HEAD, t), lambda h, j: (h, 0, 0, 0)), col, stat, stat,
                pl.BlockSpec((s, Q_EXT), lambda h, j: (0, 0))]
    if has_in:
        ins += list(acc_in)
        in_specs += [kblk, kblk, kblk]
    wide = jax.ShapeDtypeStruct((s, N_HEADS * LANES), F32)
    return pl.pallas_call(
        body, name=name, grid=(N_HEADS // nh, nq), in_specs=in_specs, out_specs=(q_all, kblk, kblk, kblk),
        out_shape=(jax.ShapeDtypeStruct((s, N_HEADS * Q_EXT), BF16), wide, wide, wide),
        scratch_shapes=[pltpu.VMEM((nh, s, Q_EXT), F32), pltpu.VMEM((nh, t, Q_EXT), F32), pltpu.VMEM((nh, t, LANES), F32)],
        compiler_params=pltpu.CompilerParams(dimension_semantics=("parallel", "arbitrary"), vmem_limit_bytes=ATT_BWD_VMEM_BYTES),
    )(*ins)


def _swap_halves(w):
    half = w.shape[-1] // 2
    return jnp.concatenate([-w[..., half:], w[..., :half]], axis=-1)


def _unswap_halves(g):
    half = g.shape[-1] // 2
    return jnp.concatenate([g[..., half:], -g[..., :half]], axis=-1)


def _extend_w_dkv(w):
    return jnp.concatenate([w, _swap_halves(w[:, KV_RANK:])], axis=-1)


def _fold_w_dkv_grad(g):
    rope = g[:, KV_RANK:KV_RANK + QK_ROPE] + _unswap_halves(g[:, KV_RANK + QK_ROPE:])
    return jnp.concatenate([g[:, :KV_RANK], rope], axis=-1)


def _rope_tables(positions):
    inv = 1.0 / (ROPE_THETA ** (jnp.arange(0, QK_ROPE, 2, dtype=F32) / QK_ROPE))
    ang = positions.astype(F32)[:, None] * inv
    cos, sin = jnp.cos(ang), jnp.sin(ang)
    tabk = jnp.concatenate([cos, cos, sin, sin], axis=-1)
    scale = QK_HEAD ** -0.5 * LOG2E
    tabq = jnp.concatenate([jnp.full((positions.shape[0], QK_NOPE), scale, F32), tabk * scale], axis=-1)
    return tabq, tabk


def _forward_backward(x, target, mods, tabq, tabk, norm1_all, final_g, fetch, push):
    row = lambda vec: vec.reshape(1, -1)
    mod = [[row(mods[l, k * D_MODEL:(k + 1) * D_MODEL]) for k in range(N_MOD)] for l in range(DEPTH)]
    saved, weights = [], []
    kv = h1 = None
    for l in range(DEPTH):
        w, tok = fetch(l, x)
        sh1, sc1, g1, sh2, sc2, g2 = mod[l]
        g1 = g1 + tok
        norm2 = (row(w["norm2_g"]), sc2, sh2)
        if l == N_A_LAYERS:
            kvn = _rms_fwd("kvin_fwd", x, row(w["kv_in_g"]))
            kv_ext = _mm("dkv_fwd", kvn, w["w_dkv_ext"], out_dtype=F32)
            ckv = _rms_fwd("ckv_fwd", kv_ext, row(w["ckv_norm_g"]), ncols=KV_RANK)
            kd = _krope_fwd("krope_fwd", kv_ext, tabk)
            kn, v = _mm("uk_fwd", ckv, w["w_uk"]), _mm("uv_fwd", ckv, w["w_uv"])
            heads = lambda a: [a[:, h * LANES:(h + 1) * LANES] for h in range(N_HEADS)]
            kfull = jnp.concatenate([part for kh in heads(kn) for part in (kh, kd)], axis=-1)
            v_ext = jnp.concatenate([part for vh in heads(v) for part in (vh, jnp.ones_like(vh))], axis=-1)
            kv = dict(x=x, kvn=kvn, kv_ext=kv_ext, ckv=ckv, v=v, kfull=kfull, v_ext=v_ext,
                      kt4=_head_blocks_t(kfull, Q_EXT))
        x_in = x
        if l == 0:
            h1 = _rms_fwd("norm1_fwd_0", x, row(norm1_all[0]), sc1, sh1, out_dtype=F32)
        if l < N_A_LAYERS:
            x_mid, zb, pooled, h2 = _pool_fwd(f"pool_fwd_{l}", h1, x, w["pool_w"], row(w["pool_b"]), row(w["pool_scale"]), g1, norm2)
            mix = (zb, pooled)
        else:
            cq_pre = _mm(f"dq_fwd_{l}", h1, w["w_dq"], out_dtype=F32)
            cq = _rms_fwd(f"qnorm_fwd_{l}", cq_pre, row(w["q_norm_g"]))
            q_rot = _mm(f"uq_fwd_{l}", cq, w["w_uq_ext"], rowtab=tabq)
            o, lse_row = _attn_fwd(f"attn_fwd_{l}", q_rot, kv["kt4"], kv["v_ext"])
            x_mid, h2 = _mm(f"wo_fwd_{l}", o, w["w_o"], resid=x, gate=g1, norm=norm2 + (BF16,))
            mix = (h1, cq_pre, cq, q_rot, o, lse_row)
        w_up_t = w["w_up"](h2)
        ua, gl, gpv, ge = _up_glu_fwd(f"up_glu_fwd_{l}", h2, w_up_t, w["conv_w"], row(w["conv_b"]))
        w_down = w["w_down"](gl)
        if l + 1 < DEPTH:
            nxt = (row(norm1_all[l + 1]), mod[l + 1][1], mod[l + 1][0], F32 if l + 1 < N_A_LAYERS else BF16)
            x, h1 = _mm(f"down_fwd_{l}", gl, w_down, resid=x_mid, gate=g2, norm=nxt)
        else:
            x = _mm(f"down_fwd_{l}", gl, w_down, resid=x_mid, gate=g2)
        saved.append((x_in, x_mid, h2, ua, gpv, ge, gl, mix))
        weights.append(dict(w, w_up_t=w_up_t, w_down=w_down))

    dx, dfinal_g, loss = _loss_head("loss_head", x, row(final_g), target)
    g = {"final_g": dfinal_g.reshape(-1)}
    per_layer = {k: [None] * DEPTH for k in ("norm1_g", "norm2_g", "conv_w", "conv_b")}
    per_a = {k: [None] * N_A_LAYERS for k in ("pool_b", "pool_scale")}
    per_b = {k: [None] * N_B_LAYERS for k in ("q_norm_g",)}
    dmods = [None] * DEPTH
    dkv = None
    tok = 0.0
    for l in reversed(range(DEPTH)):
        w, big = weights[l], {}
        sh1, sc1, g1, sh2, sc2, g2 = mod[l]
        g2 = g2 + tok
        x_in, x_mid, h2, ua, gpv, ge, gl, mix = saved[l]
        dw_down, dg2 = _mm(f"down_wgrad_{l}", gl, dx, mode="tn", tm_cap=1408, gate_grad=(w["w_down"], g2))
        tok = push(l, "down", dict(w_down=dw_down), None)
        da, dv_, dcw, dcb = _down_glu_bwd(f"down_glu_bwd_{l}", dx, g2, w["w_down"], ua, gpv, ge, w["conv_w"] + tok)
        dh2 = _mm(f"up_bwd_{l}", da, w["w_up_t"][:D_FF], out_dtype=F32, second=(dv_, w["w_up_t"][D_FF:]))
        tok = push(l, "up", dict(w_up_t_a=_mm(f"up_a_wgrad_{l}", da, h2, mode="tn", tm_cap=1408),
                                 w_up_t_v=_mm(f"up_v_wgrad_{l}", dv_, h2, mode="tn", tm_cap=1408)), None)
        per_layer["conv_w"][l], per_layer["conv_b"][l] = dcw, dcb.reshape(-1)
        dx_mid, dn2, dsh2, dsc2 = _rms_bwd(f"norm2_bwd_{l}", x_mid, row(w["norm2_g"]), dh2, sc2 + tok, dx_in=dx)
        per_layer["norm2_g"][l] = dn2.reshape(-1)
        if l < N_A_LAYERS:
            zb, pooled = mix
            dh1, dpw, dpb, dps, dg1 = _pool_bwd(f"pool_bwd_{l}", dx_mid, zb, pooled, w["pool_w"], row(w["pool_scale"]), g1)
            big["pool_w"] = dpw
            per_a["pool_b"][l], per_a["pool_scale"][l] = dpb.reshape(-1), dps.reshape(-1)
        else:
            j = l - N_A_LAYERS
            h1, cq_pre, cq, q_rot, o, lse_row = mix
            do = _mm(f"wo_bwd_{l}", dx_mid, w["w_o"], mode="nt", a_scale=g1)
            big["w_o"], dg1 = _mm(f"wo_wgrad_{l}", o, dx_mid, mode="tn", gate_grad=(w["w_o"], g1))
            delta_row = _attn_delta(f"attn_delta_{l}", o, do)
            dq_ext, *dkv = _attn_bwd(f"attn_bwd_{l}", kv["kfull"], kv["v"], _head_blocks_t(q_rot, Q_EXT), q_rot, _head_blocks_t(do, V_HEAD), do,
                                     lse_row, delta_row, tabq, acc_in=dkv)
            dcq = _mm(f"uq_bwd_{l}", dq_ext, w["w_uq_ext"], mode="nt", out_dtype=F32)
            big["w_uq_ext"] = _mm(f"uq_wgrad_{l}", cq, dq_ext, mode="tn", out_dtype=F32)
            dcq_pre, dqn = _rms_bwd(f"qnorm_bwd_{l}", cq_pre, row(w["q_norm_g"]), dcq, out_dtype=BF16)
            per_b["q_norm_g"][j] = dqn.reshape(-1)
            dh1 = _mm(f"dq_bwd_{l}", dcq_pre, w["w_dq"], mode="nt")
            big["w_dq"] = _mm(f"dq_wgrad_{l}", h1, dcq_pre, mode="tn")
        dx, dn1, dsh1, dsc1 = _rms_bwd(f"norm1_bwd_{l}", x_in, row(w["norm1_g"]), dh1, sc1, dx_in=dx_mid)
        per_layer["norm1_g"][l] = dn1.reshape(-1)
        dmods[l] = jnp.concatenate([dsh1, dsc1, dg1, dsh2, dsc2, dg2], axis=-1).reshape(-1)
        if l == N_A_LAYERS:
            dkn, dkd, dv = dkv
            dckv = _mm("ukv_bwd", dkn, w["w_uk"], mode="nt", out_dtype=F32, second=(dv, w["w_uv"]))
            big["w_uk"] = _mm("uk_wgrad", kv["ckv"], dkn, mode="tn")
            big["w_uv"] = _mm("uv_wgrad", kv["ckv"], dv, mode="tn")
            dkr = _krope_bwd("krope_bwd", dkd, tabk)
            dc, dckv_g = _rms_bwd("ckv_bwd", kv["kv_ext"], row(w["ckv_norm_g"]), dckv, ncols=KV_RANK, out_dtype=BF16)
            dkv_ext = jnp.concatenate([dc, dkr.astype(BF16)], axis=-1)
            dkvn = _mm("dkv_bwd", dkv_ext, w["w_dkv_ext"], mode="nt")
            big["w_dkv_ext"] = _mm("dkv_wgrad", kv["kvn"], dkv_ext, mode="tn", out_dtype=F32)
            dx, dkv_in_g = _rms_bwd("kvin_bwd", kv["x"], row(w["kv_in_g"]), dkvn, dx_in=dx)
            g["ckv_norm_g"], g["kv_in_g"] = dckv_g.reshape(-1), dkv_in_g.reshape(-1)
        tok = push(l, "mix", big, dx)
    for group in (per_layer, per_a, per_b):
        for k, vals in group.items():
            g[k] = jnp.stack(vals)
    return loss, dx, g, jnp.stack(dmods)


def _my_index():
    return 4 * lax.axis_index("x") + 2 * lax.axis_index("y") + lax.axis_index("c")


def _peer(k):
    x, y, c = lax.axis_index("x"), lax.axis_index("y"), lax.axis_index("c")
    return (1 - x if k & 4 else x, 1 - y if k & 2 else y, 1 - c if k & 1 else c)


def _index_of(pos):
    return 4 * pos[0] + 2 * pos[1] + pos[2]


def _exchange_many(name, arrays, scatter):
    n = len(arrays)
    blocks = [tuple(a.shape[1:]) if scatter else tuple(a.shape) for a in arrays]

    def body(*refs):
        x_refs, o_refs = refs[:n], refs[n:2 * n]
        send_sems, recv_sems, local_sems = refs[2 * n:]
        me = _my_index()
        started = []
        for a in range(n):
            mine = pltpu.make_async_copy(x_refs[a].at[me] if scatter else x_refs[a], o_refs[a].at[me], local_sems.at[a])
            mine.start()
            started.append(mine)
        sends = []
        for k in range(1, N_DEV):
            peer = _peer(k)
            for a in range(n):
                cp = pltpu.make_async_remote_copy(
                    src_ref=x_refs[a].at[_index_of(peer)] if scatter else x_refs[a], dst_ref=o_refs[a].at[me],
                    send_sem=send_sems.at[a, k - 1], recv_sem=recv_sems.at[a, k - 1], device_id=peer, device_id_type=MESH)
                cp.start()
                sends.append(cp)
        for k in range(1, N_DEV):
            peer = _peer(k)
            for a in range(n):
                pltpu.make_async_remote_copy(
                    src_ref=x_refs[a].at[me] if scatter else x_refs[a], dst_ref=o_refs[a].at[_index_of(peer)],
                    send_sem=send_sems.at[a, k - 1], recv_sem=recv_sems.at[a, k - 1], device_id=peer, device_id_type=MESH).wait_recv()
        for cp in sends:
            cp.wait_send()
        for mine in started:
            mine.wait()

    return pl.pallas_call(
        body, name=name, out_shape=tuple(jax.ShapeDtypeStruct((N_DEV,) + blk, a.dtype) for blk, a in zip(blocks, arrays)),
        in_specs=[pl.BlockSpec(memory_space=pl.ANY)] * n, out_specs=tuple([pl.BlockSpec(memory_space=pl.ANY)] * n),
        scratch_shapes=[pltpu.SemaphoreType.DMA((n, N_DEV - 1)), pltpu.SemaphoreType.DMA((n, N_DEV - 1)), pltpu.SemaphoreType.DMA((n,))],
    )(*arrays)


def _exchange(name, x, scatter):
    return _exchange_many(name, [x], scatter)[0]


HBM_SPEC = pl.BlockSpec(memory_space=pltpu.HBM)
SEM_SPEC = pl.BlockSpec(memory_space=pltpu.SEMAPHORE)
DATAFLOW = pltpu.SideEffectType.DATAFLOW_SIDE_EFFECTING


def _remote_copies(x_refs, land_refs, send_sems, recv_sems, scatter, numbers=None):
    me = _my_index()
    numbers = list(range(len(x_refs))) if numbers is None else numbers
    out, inc = [], []
    for a in range(len(x_refs)):
        for k in range(1, N_DEV):
            peer = _peer(k)
            pair = numbers[a] * (N_DEV - 1) + k - 1
            sems = dict(send_sem=send_sems.at[pair], recv_sem=recv_sems.at[pair], device_id=peer, device_id_type=MESH)
            out.append(pltpu.make_async_remote_copy(
                src_ref=x_refs[a].at[_index_of(peer)] if scatter else x_refs[a], dst_ref=land_refs[a].at[me], **sems))
            inc.append(pltpu.make_async_remote_copy(
                src_ref=x_refs[a].at[me] if scatter else x_refs[a], dst_ref=land_refs[a].at[_index_of(peer)], **sems))
    return out, inc


def _exchange_start(name, arrays, scatter):
    n = len(arrays)
    blocks = [tuple(a.shape[1:]) if scatter else tuple(a.shape) for a in arrays]

    def body(*refs):
        x_refs, land_refs = refs[:n], refs[n:2 * n]
        send_sems, recv_sems = refs[2 * n], refs[2 * n + 1]
        for cp in _remote_copies(x_refs, land_refs, send_sems, recv_sems, scatter)[0]:
            cp.start()
        refs[-1][...] = jnp.zeros_like(refs[-1])

    sem_type = pltpu.SemaphoreType.DMA((n * (N_DEV - 1),))
    lands =[pltpu.with_memory_space_constraint(lax.empty((N_DEV,) + blk, a.dtype), pltpu.HBM) for blk, a in zip(blocks, arrays)]
    srcs = [pltpu.with_memory_space_constraint(a, pltpu.HBM) for a in arrays]
    res = pl.pallas_call(
        body, name=name,
        out_shape=(sem_type, sem_type, *[pltpu.HBM(a.shape, a.dtype) for a in srcs + lands], jax.ShapeDtypeStruct((8, LANES), F32)),
        in_specs=[HBM_SPEC] * (2 * n), out_specs=(SEM_SPEC, SEM_SPEC, *[HBM_SPEC] * (2 * n), pl.BlockSpec(memory_space=pltpu.VMEM)),
        input_output_aliases={i: 2 + i for i in range(2 * n)},
        compiler_params=pltpu.CompilerParams(has_side_effects=DATAFLOW),
    )(*srcs, *lands)
    return (res[0], res[1], list(res[2:2 + n]), list(res[2 + n:2 + 2 * n])), res[-1]


def _exchange_wait(name, handles, after, scatter, which=None):
    send_sems, recv_sems, srcs, lands = handles
    which = list(range(len(srcs))) if which is None else list(which)
    srcs, lands = [srcs[a] for a in which], [lands[a] for a in which]
    n = len(srcs)

    def body(*refs):
        x_refs, land_refs = refs[:n], refs[n:2 * n]
        out, inc = _remote_copies(x_refs, land_refs, refs[2 * n], refs[2 * n + 1], scatter, which)
        for cp in out:
            cp.wait_send()
        for cp in inc:
            cp.wait_recv()

    res = pl.pallas_call(
        body, name=name, out_shape=tuple(pltpu.HBM(a.shape, a.dtype) for a in srcs + lands),
        in_specs=[HBM_SPEC] * (2 * n) + [SEM_SPEC, SEM_SPEC, pl.BlockSpec(memory_space=pl.ANY)], out_specs=tuple([HBM_SPEC] * (2 * n)),
        input_output_aliases={i: i for i in range(2 * n)},
        compiler_params=pltpu.CompilerParams(has_side_effects=DATAFLOW),
    )(*srcs, *lands, send_sems, recv_sems, after)
    return list(res[n:])


def _pack(arrays, dtype, row_multiple):
    flat = jnp.concatenate([a.astype(dtype).reshape(-1) for a in arrays])
    rows = -(-flat.shape[0] // (LANES * row_multiple)) * row_multiple
    return jnp.pad(flat, (0, rows * LANES - flat.shape[0])).reshape(rows, LANES)


def _unpack(packed, shapes):
    lead = packed.shape[:-2]
    flat = packed.reshape(lead + (-1,))
    out, off = [], 0
    for shp in shapes:
        size = 1
        for d in shp:
            size *= d
        out.append(flat[..., off:off + size].reshape(lead + tuple(shp)))
        off += size
    return out


def _unshard(g8, axis):
    return jnp.concatenate([g8[j] for j in range(N_DEV)], axis=axis)


def _shard8(full, axis):
    n = full.shape[axis] // N_DEV
    return jnp.stack([lax.slice_in_dim(full, j * n, (j + 1) * n, axis=axis) for j in range(N_DEV)])


VECTOR_WEIGHTS = (("pool_b", 1), ("pool_scale", 1), ("conv_w", 2))
REPLICATED_WEIGHTS = ("norm1_g", "norm2_g", "kv_in_g", "ckv_norm_g", "q_norm_g", "conv_b", "final_g")
WEIGHT_ORDER = ("mod_w", "mod_b", "norm1_g", "norm2_g", "pool_w", "pool_b", "pool_scale", "kv_in_g", "w_dkv", "ckv_norm_g", "w_uk",
                "w_uv", "w_dq", "q_norm_g", "w_uq", "w_o", "w_up", "conv_w", "conv_b", "w_down", "final_g")
SMALL_ROW_MULTIPLE = 16


def _as_2d(a):
    if a.ndim == 1:
        return a.reshape(-1, LANES)
    return a.reshape(-1, a.shape[-1])


def kernel(x, c, positions, mod_w, mod_b, norm1_g, norm2_g, pool_w, pool_b, pool_scale, kv_in_g, w_dkv, ckv_norm_g, w_uk, w_uv, w_dq, q_norm_g, w_uq, w_o, w_up, conv_w, conv_b, w_down, final_g, loss_target, m_mod_w, m_mod_b, m_norm1_g, m_norm2_g, m_pool_w, m_pool_b, m_pool_scale, m_kv_in_g, m_w_dkv, m_ckv_norm_g, m_w_uk, m_w_uv, m_w_dq, m_q_norm_g, m_w_uq, m_w_o, m_w_up, m_conv_w, m_conv_b, m_w_down, m_final_g, v_mod_w, v_mod_b, v_norm1_g, v_norm2_g, v_pool_w, v_pool_b, v_pool_scale, v_kv_in_g, v_w_dkv, v_ckv_norm_g, v_w_uk, v_w_uv, v_w_dq, v_q_norm_g, v_w_uq, v_w_o, v_w_up, v_conv_w, v_conv_b, v_w_down, v_final_g):
    shard = dict(mod_w=mod_w, mod_b=mod_b, norm1_g=norm1_g, norm2_g=norm2_g, pool_w=pool_w, pool_b=pool_b, pool_scale=pool_scale,
                 kv_in_g=kv_in_g, w_dkv=w_dkv, ckv_norm_g=ckv_norm_g, w_uk=w_uk, w_uv=w_uv, w_dq=w_dq, q_norm_g=q_norm_g, w_uq=w_uq,
                 w_o=w_o, w_up=w_up, conv_w=conv_w, conv_b=conv_b, w_down=w_down, final_g=final_g)
    mom_m = dict(mod_w=m_mod_w, mod_b=m_mod_b, norm1_g=m_norm1_g, norm2_g=m_norm2_g, pool_w=m_pool_w, pool_b=m_pool_b,
                 pool_scale=m_pool_scale, kv_in_g=m_kv_in_g, w_dkv=m_w_dkv, ckv_norm_g=m_ckv_norm_g, w_uk=m_w_uk, w_uv=m_w_uv,
                 w_dq=m_w_dq, q_norm_g=m_q_norm_g, w_uq=m_w_uq, w_o=m_w_o, w_up=m_w_up, conv_w=m_conv_w, conv_b=m_conv_b,
                 w_down=m_w_down, final_g=m_final_g)
    mom_v = dict(mod_w=v_mod_w, mod_b=v_mod_b, norm1_g=v_norm1_g, norm2_g=v_norm2_g, pool_w=v_pool_w, pool_b=v_pool_b,
                 pool_scale=v_pool_scale, kv_in_g=v_kv_in_g, w_dkv=v_w_dkv, ckv_norm_g=v_ckv_norm_g, w_uk=v_w_uk, w_uv=v_w_uv,
                 w_dq=v_w_dq, q_norm_g=v_q_norm_g, w_uq=v_w_uq, w_o=v_w_o, w_up=v_w_up, conv_w=v_conv_w, conv_b=v_conv_b,
                 w_down=v_w_down, final_g=v_final_g)
    me = _my_index()
    d6 = N_MOD * D_MODEL
    mod_cols = d6 // N_DEV

    small_in = [c] + [shard[k] for k, _ in VECTOR_WEIGHTS]
    small_all = _exchange("gather_vectors", _pack(small_in, F32, SMALL_ROW_MULTIPLE), scatter=False)
    parts = _unpack(small_all, [a.shape for a in small_in])
    c_all = jnp.pad(parts[0].reshape(N_DEV, D_MODEL), ((0, N_DEV), (0, 0)))
    vec = {k: _unshard(p, ax) for (k, ax), p in zip(VECTOR_WEIGHTS, parts[1:])}

    my_mod_b = lax.dynamic_slice_in_dim(mod_b, me * mod_cols, mod_cols, axis=1)
    mods_mine = _mods_fwd("mods_fwd", c_all, mod_w, my_mod_b)
    mods_all = _exchange("gather_mods", _pack([mods_mine], F32, SMALL_ROW_MULTIPLE), scatter=False)
    mods_all = _unpack(mods_all, [mods_mine.shape])[0]
    mods = lax.dynamic_index_in_dim(mods_all, me, axis=2, keepdims=False)
    mods = jnp.moveaxis(mods, 0, 1).reshape(DEPTH, d6)

    tabq, tabk = _rope_tables(positions[0])
    half = N_DEV // 2
    up_view = lambda a: jnp.swapaxes(a, 1, 2)
    w_up_t = up_view(shard["w_up"])
    up_cols = w_up_t.shape[1]
    cat = lambda a, axis, lo=0, hi=N_DEV: jnp.concatenate([a[j] for j in range(lo, hi)], axis=axis)

    def stage_pieces(l):
        out = {"pool_w": shard["pool_w"].astype(BF16)} if l == 0 else {}
        if l == N_A_LAYERS:
            out.update({k: shard[k].astype(BF16) for k in ("w_dkv", "w_uk", "w_uv")})
        if l >= N_A_LAYERS:
            out.update({k: shard[k][l - N_A_LAYERS].astype(BF16) for k in ("w_dq", "w_uq", "w_o")})
        out.update(w_up=w_up_t[l].astype(BF16), w_down=shard["w_down"][l].astype(BF16))
        return out

    gathers, pool_all = {}, []

    def start_gather(l, behind=None):
        pieces = stage_pieces(l)
        if behind is not None:
            pieces, _ = lax.optimization_barrier((pieces, behind))
        handles, token = _exchange_start(f"gather_start_{l}", list(pieces.values()), scatter=False)
        gathers[l] = (handles, pieces)
        return token[0, 0]

    def wait_gather(l, keys, after, tag=""):
        handles, pieces = gathers[l]
        which = [list(pieces).index(k) for k in keys]
        lands = _exchange_wait(f"gather_wait_{l}{tag}", handles, after, scatter=False, which=which)
        return dict(zip(keys, own_slot(lands, [pieces[k] for k in keys])))

    def whole_weights(l, got):
        w = dict(norm1_g=norm1_g[l], norm2_g=norm2_g[l], conv_w=vec["conv_w"][l], conv_b=conv_b[l])
        if l == 0:
            pool_all.append(got["pool_w"])
        if l < N_A_LAYERS:
            w.update(pool_w=cat(pool_all[0][:, l], 1), pool_b=vec["pool_b"][l], pool_scale=vec["pool_scale"][l])
        else:
            rope = got["w_uq"][..., QK_NOPE:]
            ext = jnp.concatenate([got["w_uq"][..., :QK_NOPE], rope, _swap_halves(rope)], axis=-1)
            w.update(w_dq=got["w_dq"].reshape(D_MODEL, Q_RANK), w_uq_ext=cat(ext, -1), w_o=got["w_o"].reshape(D_MODEL, D_MODEL),
                     q_norm_g=q_norm_g[l - N_A_LAYERS])
        if l == N_A_LAYERS:
            w.update(w_dkv_ext=_extend_w_dkv(got["w_dkv"].reshape(D_MODEL, KV_RANK + QK_ROPE)), w_uk=cat(got["w_uk"], -1),
                     w_uv=cat(got["w_uv"], -1), kv_in_g=kv_in_g, ckv_norm_g=ckv_norm_g)
        return w

    def own_slot(lands, own):
        return [lax.dynamic_update_index_in_dim(p, o, me, 0) for p, o in zip(lands, own)]

    def fetch(l, after):
        up_parts = lambda g8: g8.reshape(N_DEV * up_cols, D_MODEL)
        if l == 0:
            start_gather(0, behind=mods)
            got = wait_gather(0, ["pool_w"], mods, "_pool")
            w_up = lambda aft: up_parts(wait_gather(0, ["w_up"], aft, "_up")["w_up"])
            w_down = lambda aft: wait_gather(0, ["w_down"], aft, "_down")["w_down"].reshape(D_FF, D_MODEL)
        else:
            got = wait_gather(l, list(gathers[l][1]), after)
            up, down = up_parts(got["w_up"]), got["w_down"].reshape(D_FF, D_MODEL)
            w_up, w_down = (lambda aft: up), (lambda aft: down)
        w = dict(whole_weights(l, got), w_up=w_up, w_down=w_down)
        return w, (start_gather(l + 1) if l + 1 < DEPTH else 0.0)

    scatters, pending, pool_grads, piece_grads = {}, {}, {}, {}

    def reduce_pieces(l, keys, got):
        for k, p in zip(keys, got):
            piece_grads[(k, l)] = _sum8(f"sum_grads_{k}_{l}", p.reshape(N_DEV, -1, p.shape[-1])).reshape(p.shape[1:])

    def start_scatter(name, sent):
        sent = {k: a.astype(BF16) for k, a in sent.items()}
        handles, token = _exchange_start(f"scatter_start_{name}", list(sent.values()), scatter=True)
        scatters[name] = (handles, list(sent), [lax.dynamic_index_in_dim(a, me, 0, keepdims=False) for a in sent.values()])
        return token[0, 0]

    def finish_scatter(name, l, after):
        handles, keys, own = scatters.pop(name)
        reduce_pieces(l, keys, own_slot(_exchange_wait(f"scatter_wait_{name}", handles, after, scatter=True), own))

    def push(l, part, big, after):
        cut = lambda a, n, axis: jnp.stack([lax.slice_in_dim(a, j * n, (j + 1) * n, axis=axis) for j in range(N_DEV)])
        sent = {}
        if part == "down":
            sent["w_down"] = big["w_down"].reshape(N_DEV, D_FF // N_DEV, D_MODEL)
        elif part == "up":
            sent["w_up"] = jnp.concatenate([big[part].reshape(half, up_cols, D_MODEL) for part in ("w_up_t_a", "w_up_t_v")])
        elif l < N_A_LAYERS:
            pool_grads[l] = big["pool_w"]
        else:
            ext = cut(big["w_uq_ext"], Q_EXT, 1)
            rope = ext[..., QK_NOPE:QK_HEAD] + _unswap_halves(ext[..., QK_HEAD:])
            sent.update(w_dq=big["w_dq"].reshape(N_DEV, D_MODEL // N_DEV, Q_RANK), w_uq=jnp.concatenate([ext[..., :QK_NOPE], rope], axis=-1),
                        w_o=big["w_o"].reshape(N_DEV, D_MODEL // N_DEV, D_MODEL))
        if part == "mix" and l == N_A_LAYERS:
            sent.update(w_dkv=_fold_w_dkv_grad(big["w_dkv_ext"]).reshape(N_DEV, D_MODEL // N_DEV, KV_RANK + QK_ROPE),
                        w_uk=cut(big["w_uk"], QK_NOPE, 1), w_uv=cut(big["w_uv"], V_HEAD, 1))
        if l == 0 and part != "mix":
            return start_scatter(f"0_{part}", sent)
        if l == 0:
            finish_scatter("1", 1, after)
            pool = _shard8(jnp.stack([pool_grads[a] for a in range(N_A_LAYERS)]), 2).astype(BF16)
            reduce_pieces(0, ["pool_w"], _exchange_many("scatter_pool_grads", [pool], scatter=True))
            return 0.0
        pending.setdefault(l, {}).update(sent)
        if part != "mix":
            return 0.0
        if l + 1 < DEPTH:
            finish_scatter(str(l + 1), l + 1, after)
        return start_scatter(str(l), pending.pop(l))

    loss_row, dx, g, dmods = _forward_backward(x[0], loss_target[0], mods, tabq, tabk, norm1_g, final_g, fetch, push)
    layers_of = lambda k, ls: jnp.stack([piece_grads[(k, l)] for l in ls])
    grads = dict(w_dkv=piece_grads[("w_dkv", N_A_LAYERS)], w_uk=piece_grads[("w_uk", N_A_LAYERS)], w_uv=piece_grads[("w_uv", N_A_LAYERS)])
    for k in ("w_dq", "w_uq", "w_o"):
        grads[k] = layers_of(k, range(N_A_LAYERS, DEPTH))

    small_names = REPLICATED_WEIGHTS + tuple(k for k, _ in VECTOR_WEIGHTS)
    small_out = [dmods] + [g[k] for k in small_names] + [loss_row]
    small_shapes = [a.shape for a in small_out]
    small_got = _exchange("gather_small_grads", _pack(small_out, F32, SMALL_ROW_MULTIPLE), scatter=False)
    summed = _unpack(_sum8("sum_small_grads", small_got), small_shapes)
    grads["mod_b"] = summed[0]
    for k, s in zip(small_names, summed[1:-1]):
        grads[k] = s
    for k, ax in VECTOR_WEIGHTS:
        n = shard[k].shape[ax]
        grads[k] = lax.dynamic_slice_in_dim(grads[k], me * n, n, axis=ax)
    loss = summed[-1][0, 0]
    dmods_all = _unpack(small_got, small_shapes)[0]
    dm_mine = lax.dynamic_slice_in_dim(dmods_all, me * mod_cols, mod_cols, axis=2)
    dm_mine = jnp.pad(jnp.moveaxis(dm_mine, 0, 1), ((0, 0), (0, N_DEV), (0, 0)))
    grads["mod_w"] = _mods_bwd("mods_bwd", c_all, dm_mine)

    delta, new_m, new_v = {}, {}, {}

    def adamw(k):
        if k == "w_up":
            ops = [w_up_t, grads[k], up_view(mom_m[k]), up_view(mom_v[k])]
            res = _adamw(f"adamw_{k}", *[_as_2d(a) for a in ops])
            grads[k], delta[k], new_m[k], new_v[k] = [up_view(r.reshape(w_up_t.shape)) for r in (ops[1],) + tuple(res)]
            return
        shp = shard[k].shape
        grads[k] = grads[k].reshape(shp)
        res = _adamw(f"adamw_{k}", _as_2d(shard[k]), _as_2d(grads[k]), _as_2d(mom_m[k]), _as_2d(mom_v[k]))
        delta[k], new_m[k], new_v[k] = [r.reshape(shp) for r in res]

    late = ("w_up", "w_down", "pool_w")
    for k in WEIGHT_ORDER:
        if k not in late:
            adamw(k)
    finish_scatter("0_down", 0, delta["final_g"])
    finish_scatter("0_up", 0, delta["final_g"])
    grads.update(w_up=layers_of("w_up", range(DEPTH)), w_down=layers_of("w_down", range(DEPTH)), pool_w=piece_grads[("pool_w", 0)])
    for k in late:
        adamw(k)
    return (loss, dx[None], *[grads[k] for k in WEIGHT_ORDER], *[delta[k] for k in WEIGHT_ORDER],
            *[new_m[k] for k in WEIGHT_ORDER], *[new_v[k] for k in WEIGHT_ORDER])
```

```python
import functools

import jax
import jax.numpy as jnp
from jax import lax
from jax.experimental import pallas as pl
from jax.experimental.pallas import tpu as pltpu

F32 = jnp.float32
BF16 = jnp.bfloat16

D_MODEL = 1024
DEPTH = 4
N_A_LAYERS = 2
N_B_LAYERS = 2
POOL_WINDOWS = (2, 4, 8, 16)
POOL_GROUP = 256
N_HEADS = 8
QK_NOPE = 128
QK_ROPE = 64
V_HEAD = 128
QK_HEAD = QK_NOPE + QK_ROPE
Q_RANK = 384
KV_RANK = 256
ROPE_THETA = 10000.0
D_FF = 2816
EPS = 1e-6
N_MOD = 6
ADAM_LR = 0.001
ADAM_B1 = 0.9
ADAM_B2 = 0.999
ADAM_EPS = 1e-08
ADAM_WD = 0.01
ADAM_STEP = 10

N_DEV = 8
LANES = 128
Q_EXT = 256
VMEM_LIMIT_BYTES = 48 * 1024 * 1024
MESH = pl.DeviceIdType.MESH
NEG_BIG = -0.7 * float(jnp.finfo(jnp.float32).max)


def _params(sem):
    return pltpu.CompilerParams(dimension_semantics=sem, vmem_limit_bytes=VMEM_LIMIT_BYTES)


def _tile(n, cap):
    if n <= cap:
        return n
    best = None
    for d in range(LANES, cap + 1, LANES):
        if n % d == 0:
            best = d
    assert best is not None, (n, cap)
    return best


def _dot(a, b, dims):
    return lax.dot_general(a, b, (dims, ((), ())), preferred_element_type=F32)


NN = ((1,), (0,))
NT = ((1,), (1,))
TN = ((0,), (0,))


def _modulated_rmsnorm(xv, gv, scale, shift):
    return xv * lax.rsqrt(jnp.mean(xv * xv, axis=-1, keepdims=True) + EPS) * gv * (1.0 + scale) + shift


def _mm(name, a, b, mode="nn", out_dtype=BF16, resid=None, gate=None, norm=None, rowtab=None, second=None, a_scale=None,
        gate_grad=None, tm_cap=1024, tn_cap=1408, tk_cap=1408):
    if mode == "tn":
        kdim, m = a.shape
    else:
        m, kdim = a.shape
    n = b.shape[0] if mode == "nt" else b.shape[1]
    tm, tn, tk = _tile(m, tm_cap), _tile(n, tn_cap), _tile(kdim, tk_cap)
    nk = kdim // tk
    dims = {"nn": NN, "nt": NT, "tn": TN}[mode]
    a_spec = pl.BlockSpec((tk, tm), lambda i, j, k: (k, i)) if mode == "tn" else pl.BlockSpec((tm, tk), lambda i, j, k: (i, k))
    b_spec = pl.BlockSpec((tn, tk), lambda i, j, k: (j, k)) if mode == "nt" else pl.BlockSpec((tk, tn), lambda i, j, k: (k, j))
    o_spec = pl.BlockSpec((tm, tn), lambda i, j, k: (i, j))
    g_spec = pl.BlockSpec((1, tn), lambda i, j, k: (0, j))
    gated = resid is not None
    assert sum(x is not None for x in (resid, rowtab, gate_grad)) <= 1
    n_ops = 2 if second is None else 4
    n_extra = 1 if a_scale is not None else 0

    def body(*refs):
        acc = refs[-1]
        i, k = pl.program_id(0), pl.program_id(2)

        @pl.when(k == 0)
        def _():
            acc[...] = jnp.zeros_like(acc)

        av = refs[0][...]
        if a_scale is not None:
            av = av.astype(F32) * refs[n_ops][...]
        prod = _dot(av.astype(BF16), refs[1][...].astype(BF16), dims)
        if second is not None:
            prod = prod + _dot(refs[2][...].astype(BF16), refs[3][...].astype(BF16), dims)
        acc[...] += prod
        rest = refs[n_ops + n_extra:-1]

        if gate_grad is not None:
            @pl.when((i == 0) & (k == 0))
            def _():
                rest[3][...] = jnp.zeros_like(rest[3])

        @pl.when(k == nk - 1)
        def _():
            if gated and norm is not None:
                r_ref, g_ref, ng_ref, sc_ref, sh_ref, x_ref, h_ref = rest
                xn = r_ref[...] + g_ref[...] * acc[...]
                x_ref[...] = xn
                h_ref[...] = _modulated_rmsnorm(xn, ng_ref[...], sc_ref[...], sh_ref[...]).astype(h_ref.dtype)
            elif gated:
                r_ref, g_ref, x_ref = rest
                x_ref[...] = r_ref[...] + g_ref[...] * acc[...]
            elif rowtab is not None:
                tab = rest[0][...]
                rest[1][...] = (acc[...] * jnp.concatenate([tab] * (tn // tab.shape[1]), axis=1)).astype(out_dtype)
            elif gate_grad is not None:
                w_ref, g_ref, o_ref, dg_ref = rest
                o_ref[...] = (acc[...] * g_ref[...]).astype(out_dtype)
                dg_ref[...] += _colsum(w_ref[...].astype(F32) * acc[...])
            else:
                rest[0][...] = acc[...].astype(out_dtype)

    ins, in_specs = [a, b], [a_spec, b_spec]
    if second is not None:
        assert second[0].shape == a.shape and second[1].shape == b.shape
        ins += list(second)
        in_specs += [a_spec, b_spec]
    if a_scale is not None:
        assert mode != "tn"
        ins.append(a_scale)
        in_specs.append(pl.BlockSpec((1, tk), lambda i, j, k: (0, k)))
    out_shape, out_specs = jax.ShapeDtypeStruct((m, n), out_dtype), o_spec
    sem = ("parallel", "parallel", "arbitrary")
    if rowtab is not None:
        assert tn % rowtab.shape[1] == 0
        ins.append(rowtab)
        in_specs.append(pl.BlockSpec((tm, rowtab.shape[1]), lambda i, j, k: (i, 0)))
    if gated:
        ins += [resid, gate]
        in_specs += [o_spec, g_spec]
        out_shape = jax.ShapeDtypeStruct((m, n), F32)
    if norm is not None:
        assert gated and tn == n
        ins += list(norm[:3])
        in_specs += [g_spec] * 3
        out_shape = (out_shape, jax.ShapeDtypeStruct((m, n), norm[3]))
        out_specs = (o_spec, o_spec)
    if gate_grad is not None:
        assert mode == "tn" and tn == n
        ins += list(gate_grad)
        in_specs += [o_spec, g_spec]
        out_shape = (out_shape, jax.ShapeDtypeStruct((1, n), F32))
        out_specs = (o_spec, g_spec)
        sem = ("arbitrary", "arbitrary", "arbitrary")
    return pl.pallas_call(
        body, name=name, grid=(m // tm, n // tn, nk), in_specs=in_specs, out_specs=out_specs, out_shape=out_shape,
        scratch_shapes=[pltpu.VMEM((tm, tn), F32)],
        compiler_params=_params(sem),
    )(*ins)


def _rowwise(name, fn, tiled, bcast, outs, sums=(), tr=512):
    tiled = [t if isinstance(t, tuple) else (t, t.shape[1], 0) for t in tiled]
    s = tiled[0][0].shape[0]
    tr = min(tr, s)
    assert s % tr == 0
    n_t, n_b, n_o = len(tiled), len(bcast), len(outs)

    def body(*refs):
        i = pl.program_id(0)
        vals = [r[...] for r in refs[:n_t + n_b]]
        o_vals, s_vals = fn(*vals)
        for r, v in zip(refs[n_t + n_b:n_t + n_b + n_o], o_vals):
            r[...] = v.astype(r.dtype)
        s_refs = refs[n_t + n_b + n_o:]

        @pl.when(i == 0)
        def _():
            for r in s_refs:
                r[...] = jnp.zeros_like(r)

        for r, v in zip(s_refs, s_vals):
            r[...] += v

    in_specs = [pl.BlockSpec((tr, n), functools.partial(lambda cb, i: (i, cb), cb)) for (_, n, cb) in tiled]
    in_specs += [pl.BlockSpec(b.shape, functools.partial(lambda nd, i: (0,) * nd, b.ndim)) for b in bcast]
    out_specs = [pl.BlockSpec((tr, n), lambda i: (i, 0)) for (n, _) in outs]
    out_specs += [pl.BlockSpec((1, n), lambda i: (0, 0)) for n in sums]
    out_shape = [jax.ShapeDtypeStruct((s, n), dt) for (n, dt) in outs]
    out_shape += [jax.ShapeDtypeStruct((1, n), F32) for n in sums]
    res = pl.pallas_call(
        body, name=name, grid=(s // tr,), in_specs=in_specs, out_specs=tuple(out_specs), out_shape=tuple(out_shape),
        compiler_params=_params(("arbitrary",)),
    )(*[t[0] for t in tiled], *bcast)
    return res


def _colsum(v):
    return jnp.sum(v, axis=0, keepdims=True)


def _rms_fwd(name, x, g, scale=None, shift=None, out_dtype=BF16, ncols=None):
    mod = scale is not None

    def fn(xv, gv, *ss):
        if mod:
            return (_modulated_rmsnorm(xv, gv, ss[0], ss[1]),), ()
        return (xv * lax.rsqrt(jnp.mean(xv * xv, axis=-1, keepdims=True) + EPS) * gv,), ()

    n = ncols or x.shape[1]
    return _rowwise(name, fn, [(x, n, 0)], [g] + ([scale, shift] if mod else []), [(n, out_dtype)])[0]


def _rms_bwd(name, x, g, dh, scale=None, dx_in=None, ncols=None, out_dtype=F32):
    mod = scale is not None
    has_in = dx_in is not None

    def fn(*vals):
        xv, dhv = vals[0], vals[1].astype(F32)
        rest = list(vals[2:])
        dxi = rest.pop(0) if has_in else None
        gv = rest.pop(0)
        rstd = lax.rsqrt(jnp.mean(xv * xv, axis=-1, keepdims=True) + EPS)
        xhat = xv * rstd
        sums = []
        if mod:
            sc = rest.pop(0)
            dyn = dhv * (1.0 + sc)
            dshift, dscale = _colsum(dhv), _colsum(dhv * (xhat * gv))
        else:
            dyn = dhv
        dg = _colsum(dyn * xhat)
        dxhat = dyn * gv
        dx = rstd * (dxhat - xhat * jnp.mean(dxhat * xhat, axis=-1, keepdims=True))
        if has_in:
            dx = dx + dxi
        sums = [dg] + ([dshift, dscale] if mod else [])
        return (dx,), sums

    n = ncols or x.shape[1]
    tiled = [(x, n, 0), dh] + ([dx_in] if has_in else [])
    return _rowwise(name, fn, tiled, [g] + ([scale] if mod else []), [(n, out_dtype)], [n] * (3 if mod else 1))


def _loss_head(name, x, g, target):
    n = x.shape[1]

    def fn(xv, tv, gv):
        rstd = lax.rsqrt(jnp.mean(xv * xv, axis=-1, keepdims=True) + EPS)
        xhat = xv * rstd
        err = xhat * gv - tv
        loss = 0.5 * jnp.sum(jnp.sum(err * err, axis=-1, keepdims=True) / n, axis=0, keepdims=True)
        dy = err / n
        dg = _colsum(dy * xhat)
        dxhat = dy * gv
        dx = rstd * (dxhat - xhat * jnp.mean(dxhat * xhat, axis=-1, keepdims=True))
        return (dx,), (dg, jnp.broadcast_to(loss, (1, LANES)))

    return _rowwise(name, fn, [x, target], [g], [(n, F32)], [n, LANES])


def _krope_fwd(name, kv_ext, tabk):
    def fn(xv, tv):
        t = xv * tv
        return (t + pltpu.roll(t, 64, 1),), ()

    return _rowwise(name, fn, [(kv_ext, LANES, 2), tabk], [], [(LANES, BF16)])[0]


def _krope_bwd(name, dkd, tabk):
    def fn(dv, tv):
        d = dv[:, :LANES]
        for h in range(1, N_HEADS):
            d = d + dv[:, h * LANES:(h + 1) * LANES]
        return ((d + pltpu.roll(d, 64, 1)) * tv,), ()

    return _rowwise(name, fn, [dkd, tabk], [], [(LANES, F32)])[0]


def _adamw(name, w, g, m, v):
    def fn(wv, gv, mv, vv):
        m2 = ADAM_B1 * mv + (1.0 - ADAM_B1) * gv
        v2 = ADAM_B2 * vv + (1.0 - ADAM_B2) * (gv * gv)
        m_hat = m2 / (1.0 - ADAM_B1 ** ADAM_STEP)
        v_hat = v2 / (1.0 - ADAM_B2 ** ADAM_STEP)
        delta = -ADAM_LR * (m_hat / (jnp.sqrt(v_hat) + ADAM_EPS) + ADAM_WD * wv)
        return (delta, m2, v2), ()

    r, c = w.shape
    tr = r
    for cand in (512, 256, 128, 64, 32, 16, 8):
        if r % cand == 0 and r > cand:
            tr = cand
            break
    return _rowwise(name, fn, [w, g, m, v], [], [(c, F32)] * 3, tr=tr)


def _sum8(name, parts):
    _, r, c = parts.shape
    tr = r
    for cand in (2048, 1024, 512, 256, 128, 64, 32, 16):
        if r % cand == 0 and r > cand and cand * c <= 256 * 1024:
            tr = cand
            break

    def body(p_ref, o_ref):
        acc = p_ref[0].astype(F32)
        for k in range(1, N_DEV):
            acc = acc + p_ref[k].astype(F32)
        o_ref[...] = acc

    return pl.pallas_call(
        body, name=name, grid=(r // tr,), in_specs=[pl.BlockSpec((N_DEV, tr, c), lambda i: (0, i, 0))],
        out_specs=pl.BlockSpec((tr, c), lambda i: (i, 0)), out_shape=jax.ShapeDtypeStruct((r, c), F32),
        compiler_params=_params(("parallel",)),
    )(parts)


def _mods_fwd(name, c_all, w, b):
    depth, d, n = w.shape

    def body(c_ref, w_ref, b_ref, o_ref):
        cv = c_ref[...]
        sc = (cv * (1.0 / (1.0 + jnp.exp(-cv)))).astype(BF16)
        o_ref[0] = _dot(sc, w_ref[0].astype(BF16), NN) + b_ref[0]

    return pl.pallas_call(
        body, name=name, grid=(depth,),
        in_specs=[pl.BlockSpec(c_all.shape, lambda l: (0, 0)), pl.BlockSpec((1, d, n), lambda l: (l, 0, 0)),
                  pl.BlockSpec((1, 1, n), lambda l: (l, 0, 0))],
        out_specs=pl.BlockSpec((1, c_all.shape[0], n), lambda l: (l, 0, 0)),
        out_shape=jax.ShapeDtypeStruct((depth, c_all.shape[0], n), F32),
        compiler_params=_params(("parallel",)),
    )(c_all, w, b.reshape(depth, 1, n))


def _mods_bwd(name, c_all, dm):
    depth, rows, n = dm.shape
    d = c_all.shape[1]

    def body(c_ref, dm_ref, o_ref):
        cv = c_ref[...]
        sc = (cv * (1.0 / (1.0 + jnp.exp(-cv)))).astype(BF16)
        o_ref[0] = _dot(sc, dm_ref[0].astype(BF16), TN)

    return pl.pallas_call(
        body, name=name, grid=(depth,),
        in_specs=[pl.BlockSpec(c_all.shape, lambda l: (0, 0)), pl.BlockSpec((1, rows, n), lambda l: (l, 0, 0))],
        out_specs=pl.BlockSpec((1, d, n), lambda l: (l, 0, 0)),
        out_shape=jax.ShapeDtypeStruct((depth, d, n), F32),
        compiler_params=_params(("parallel",)),
    )(c_all, dm)


POOL_TILE = 256


def _split_dot(band, val):
    hi = val.astype(BF16)
    lo = (val - hi.astype(F32)).astype(BF16)
    return _dot(band, hi, NN) + _dot(band, lo, NN)


def _pool_fwd(name, h1, x, pw, pb, ps, g1, norm):
    s, d = h1.shape
    t = POOL_TILE

    def body(hc_ref, hp_ref, x_ref, pw_ref, pb_ref, ps_ref, g_ref, ng_ref, sc_ref, sh_ref, xo_ref, zb_ref, pooled_ref, h2_ref):
        i = pl.program_id(0)
        r = lax.broadcasted_iota(jnp.int32, (t, t), 0)
        j = lax.broadcasted_iota(jnp.int32, (t, t), 1)
        pos = (i * t + lax.broadcasted_iota(jnp.int32, (t, 1), 0) + 1).astype(F32)
        has_prev = (i > 0).astype(F32)
        for grp, w in enumerate(POOL_WINDOWS):
            cs = slice(grp * POOL_GROUP, (grp + 1) * POOL_GROUP)
            hc = hc_ref[:, cs]
            band_cur = ((r - j >= 0) & (r - j < w)).astype(BF16)
            band_prev = (r + t - j < w).astype(BF16)
            ssum = _split_dot(band_cur, hc) + has_prev * _split_dot(band_prev, hp_ref[:, cs])
            pooled = (ssum / jnp.minimum(pos, float(w)) - hc).astype(BF16)
            zb = _dot(pooled, pw_ref[grp], NN) + pb_ref[:, cs]
            xo_ref[:, cs] = x_ref[:, cs] + g_ref[:, cs] * (zb * ps_ref[:, cs])
            zb_ref[:, cs] = zb
            pooled_ref[:, cs] = pooled
        h2_ref[...] = _modulated_rmsnorm(xo_ref[...], ng_ref[...], sc_ref[...], sh_ref[...]).astype(h2_ref.dtype)

    row = pl.BlockSpec((t, d), lambda i: (i, 0))
    vec = pl.BlockSpec((1, d), lambda i: (0, 0))
    return pl.pallas_call(
        body, name=name, grid=(s // t,),
        in_specs=[row, pl.BlockSpec((t, d), lambda i: (jnp.maximum(i - 1, 0), 0)), row,
                  pl.BlockSpec(pw.shape, lambda i: (0, 0, 0)), vec, vec, vec, vec, vec, vec],
        out_specs=(row, row, row, row),
        out_shape=(jax.ShapeDtypeStruct((s, d), F32), jax.ShapeDtypeStruct((s, d), F32), jax.ShapeDtypeStruct((s, d), BF16),
                   jax.ShapeDtypeStruct((s, d), BF16)),
        compiler_params=_params(("parallel",)),
    )(h1, h1, x, pw, pb, ps, g1, *norm)


def _pool_bwd(name, dxn, zb, pooled, pw, ps, g1):
    s, d = dxn.shape
    t = POOL_TILE
    nt = s // t

    def body(dc_ref, dn_ref, zb_ref, pooled_ref, pw_ref, ps_ref, g_ref, dh_ref, dpw_ref, dpb_ref, dps_ref, dg_ref):
        i = pl.program_id(0)

        @pl.when(i == 0)
        def _():
            dpw_ref[...] = jnp.zeros_like(dpw_ref)
            dpb_ref[...] = jnp.zeros_like(dpb_ref)
            dps_ref[...] = jnp.zeros_like(dps_ref)
            dg_ref[...] = jnp.zeros_like(dg_ref)

        jj = lax.broadcasted_iota(jnp.int32, (t, t), 0)
        rr = lax.broadcasted_iota(jnp.int32, (t, t), 1)
        pos = (i * t + lax.broadcasted_iota(jnp.int32, (t, 1), 0) + 1).astype(F32)
        has_next = (i < nt - 1).astype(F32)
        for grp, w in enumerate(POOL_WINDOWS):
            cs = slice(grp * POOL_GROUP, (grp + 1) * POOL_GROUP)
            gv, psv, zbv, dxc = g_ref[:, cs], ps_ref[:, cs], zb_ref[:, cs], dc_ref[:, cs]
            dg_ref[:, cs] += _colsum(dxc * (zbv * psv))
            dy = gv * dxc
            dps_ref[:, cs] += _colsum(dy * zbv)
            dz = dy * psv
            dpb_ref[:, cs] += _colsum(dz)
            dzb = dz.astype(BF16)
            dpw_ref[grp] += _dot(pooled_ref[:, cs], dzb, TN)
            dp = _dot(dzb, pw_ref[grp], NT)
            dzn = (gv * dn_ref[:, cs] * psv).astype(BF16)
            dpn = _dot(dzn, pw_ref[grp], NT) * (has_next / float(w))
            band_cur = ((rr - jj >= 0) & (rr - jj < w)).astype(BF16)
            band_next = (rr + t - jj < w).astype(BF16)
            dh_ref[:, cs] = _split_dot(band_cur, dp / jnp.minimum(pos, float(w))) + _split_dot(band_next, dpn) - dp

    row = pl.BlockSpec((t, d), lambda i: (i, 0))
    vec = pl.BlockSpec((1, d), lambda i: (0, 0))
    wspec = pl.BlockSpec(pw.shape, lambda i: (0, 0, 0))
    return pl.pallas_call(
        body, name=name, grid=(nt,),
        in_specs=[row, pl.BlockSpec((t, d), lambda i: (jnp.minimum(i + 1, nt - 1), 0)), row, row, wspec, vec, vec],
        out_specs=(row, wspec, vec, vec, vec),
        out_shape=(jax.ShapeDtypeStruct((s, d), F32), jax.ShapeDtypeStruct(pw.shape, F32),
                   jax.ShapeDtypeStruct((1, d), F32), jax.ShapeDtypeStruct((1, d), F32), jax.ShapeDtypeStruct((1, d), F32)),
        compiler_params=_params(("arbitrary",)),
    )(dxn, dxn, zb, pooled, pw, ps, g1)


GLU_TILE = 512
HALO = 16
INV_SQRT2 = 0.7071067811865476
INV_SQRT_2PI = 0.3989422804014327


def _up_glu_fwd(name, h2, wt, cw, cb):
    s, d = h2.shape
    f = wt.shape[0] // 2
    tm, tn = _tile(s, GLU_TILE), _tile(f, 1408)

    def body(h_ref, hh_ref, wa_ref, wv_ref, cw_ref, cb_ref, ua_ref, gl_ref, gpv_ref, ge_ref):
        i = pl.program_id(1)
        has_prev = (i > 0).astype(F32)
        a = _dot(h_ref[...], wa_ref[...], NT).astype(BF16)
        v = _dot(h_ref[...], wv_ref[...], NT)
        above = (_dot(hh_ref[...], wa_ref[...], NT) * has_prev).astype(BF16)
        ua_ref[...] = a
        ext = jnp.concatenate([above.astype(F32), a.astype(F32)], axis=0)
        e1 = pltpu.roll(ext, 1, 0)[HALO:]
        e2 = pltpu.roll(ext, 2, 0)[HALO:]
        pre = e2 * cw_ref[0:1, :] + e1 * cw_ref[1:2, :] + ext[HALO:] * cw_ref[2:3, :] + cb_ref[...]
        cdf = 0.5 * (1.0 + lax.erf(pre * INV_SQRT2))
        ge = pre * cdf
        gl_ref[...] = (ge * v).astype(gl_ref.dtype)
        gpv_ref[...] = ((cdf + pre * (INV_SQRT_2PI * jnp.exp(-0.5 * pre * pre))) * v).astype(gpv_ref.dtype)
        ge_ref[...] = ge.astype(ge_ref.dtype)

    blk = pl.BlockSpec((tm, tn), lambda j, i: (i, j))
    return pl.pallas_call(
        body, name=name, grid=(f // tn, s // tm),
        in_specs=[pl.BlockSpec((tm, d), lambda j, i: (i, 0)), pl.BlockSpec((HALO, d), lambda j, i: (jnp.maximum(i * (tm // HALO) - 1, 0), 0)),
                  pl.BlockSpec((tn, d), lambda j, i: (j, 0)), pl.BlockSpec((tn, d), lambda j, i: (j + f // tn, 0)),
                  pl.BlockSpec((3, tn), lambda j, i: (0, j)), pl.BlockSpec((1, tn), lambda j, i: (0, j))],
        out_specs=(blk, blk, blk, blk), out_shape=tuple(jax.ShapeDtypeStruct((s, f), BF16) for _ in range(4)),
        compiler_params=_params(("parallel", "parallel")),
    )(h2, h2, wt, wt, cw, cb)


def _down_glu_bwd(name, dx, gate, wd, ua, gpv, ge, cw):
    s, f = ua.shape
    d = dx.shape[1]
    t, tf = min(GLU_TILE, s), _tile(f, 1408)
    nt = s // t
    te = t + HALO

    def body(dy_ref, dyn_ref, gate_ref, wd_ref, a_ref, ah_ref, g_ref, gn_ref, ge_ref, cw_ref, da_ref, dv_ref, dcw_ref, dcb_ref):
        i = pl.program_id(1)

        @pl.when(i == 0)
        def _():
            dcw_ref[...] = jnp.zeros_like(dcw_ref)
            dcb_ref[...] = jnp.zeros_like(dcb_ref)

        has_prev = (i > 0).astype(F32)
        has_next = (i < nt - 1).astype(F32)
        wdv = wd_ref[...]
        dgl = _dot((dy_ref[...] * gate_ref[...]).astype(BF16), wdv, NT)
        dgl_below = _dot((dyn_ref[...] * gate_ref[...]).astype(BF16), wdv, NT) * has_next
        dpre = jnp.concatenate([dgl * g_ref[...].astype(F32), dgl_below * gn_ref[...].astype(F32)], axis=0)
        c0, c1, c2 = cw_ref[0:1, :], cw_ref[1:2, :], cw_ref[2:3, :]
        up1 = pltpu.roll(dpre, te - 1, 0)
        up2 = pltpu.roll(dpre, te - 2, 0)
        da_ref[...] = (dpre * c2 + up1 * c1 + up2 * c0)[:t].astype(da_ref.dtype)
        dv_ref[...] = (dgl * ge_ref[...].astype(F32)).astype(dv_ref.dtype)
        ext = jnp.concatenate([ah_ref[...].astype(F32) * has_prev, a_ref[...].astype(F32)], axis=0)
        dpt = dpre[:t]
        dcb_ref[...] += _colsum(dpt)
        dcw_ref[0:1, :] += _colsum(pltpu.roll(ext, 2, 0)[HALO:] * dpt)
        dcw_ref[1:2, :] += _colsum(pltpu.roll(ext, 1, 0)[HALO:] * dpt)
        dcw_ref[2:3, :] += _colsum(ext[HALO:] * dpt)

    blk = pl.BlockSpec((t, tf), lambda j, i: (i, j))
    prev = pl.BlockSpec((HALO, tf), lambda j, i: (jnp.maximum(i * (t // HALO) - 1, 0), j))
    below = lambda i: jnp.minimum((i + 1) * (t // HALO), s // HALO - 1)
    w3 = pl.BlockSpec((3, tf), lambda j, i: (0, j))
    w1 = pl.BlockSpec((1, tf), lambda j, i: (0, j))
    return pl.pallas_call(
        body, name=name, grid=(f // tf, nt),
        in_specs=[pl.BlockSpec((t, d), lambda j, i: (i, 0)), pl.BlockSpec((HALO, d), lambda j, i: (below(i), 0)),
                  pl.BlockSpec((1, d), lambda j, i: (0, 0)), pl.BlockSpec((tf, d), lambda j, i: (j, 0)), blk, prev, blk,
                  pl.BlockSpec((HALO, tf), lambda j, i: (below(i), j)), blk, w3],
        out_specs=(blk, blk, w3, w1),
        out_shape=(jax.ShapeDtypeStruct((s, f), BF16), jax.ShapeDtypeStruct((s, f), BF16),
                   jax.ShapeDtypeStruct((3, f), F32), jax.ShapeDtypeStruct((1, f), F32)),
        compiler_params=_params(("parallel", "arbitrary")),
    )(dx, dx, gate, wd, ua, ua, gpv, gpv, ge, cw)


ATT_TILE = 512
ATT_ROWS = 256
ATT_HEADS = 4
ATT_BWD_HEADS = 2
ATT_BWD_VMEM_BYTES = 58 * 1024 * 1024
LOG2E = 1.4426950408889634
LN2 = 0.6931471805599453


def _head_blocks_t(a, width):
    s = a.shape[0]
    t = min(ATT_TILE, s)
    return a.reshape(s // t, t, N_HEADS, width).transpose(2, 0, 3, 1)


def _causal_mask(sv, q0, k0):
    row = q0 + lax.broadcasted_iota(jnp.int32, sv.shape, 0)
    col = k0 + lax.broadcasted_iota(jnp.int32, sv.shape, 1)
    return jnp.where(col <= row, sv, NEG_BIG)


def _attn_fwd(name, q_rot, kt4, v_ext):
    s = q_rot.shape[0]
    t = min(ATT_TILE, s)
    nq = s // t
    rq = min(ATT_ROWS, t)
    nh = ATT_HEADS

    def body(q_ref, kt_ref, v_ref, o_ref, row_ref, acc_ref, m_ref):
        qi = pl.program_id(1)
        acc_ref[...] = jnp.zeros_like(acc_ref)
        m_ref[...] = jnp.full_like(m_ref, NEG_BIG)

        def step(j, masked):
            for hh in range(nh):
                cols = slice(hh * Q_EXT, (hh + 1) * Q_EXT)
                v_blk = v_ref[pl.ds(pl.multiple_of(j * t, t), t), cols]
                for r in range(t // rq):
                    rs = pl.ds(r * rq, rq)
                    sv = _dot(q_ref[rs, cols], kt_ref[hh, j], NN)
                    if masked:
                        sv = _causal_mask(sv, r * rq, 0)
                    m_prev = m_ref[hh, rs, :]
                    m_new = jnp.maximum(m_prev, jnp.max(sv, axis=-1, keepdims=True))
                    p = jnp.exp2(sv - m_new).astype(BF16)
                    acc_ref[hh, rs, :] = jnp.exp2(m_prev - m_new) * acc_ref[hh, rs, :] + _dot(p, v_blk, NN)
                    m_ref[hh, rs, :] = m_new

        def full_step(j, carry):
            step(j, False)
            return carry

        lax.fori_loop(0, qi, full_step, 0)
        step(qi, True)
        for hh in range(nh):
            l = acc_ref[hh, :, V_HEAD:V_HEAD + 1]
            o_ref[:, hh * V_HEAD:(hh + 1) * V_HEAD] = (acc_ref[hh, :, :V_HEAD] / l).astype(o_ref.dtype)
            lse = jnp.broadcast_to(m_ref[hh] + jnp.log(l) * LOG2E, (t, LANES))
            row_ref[hh, 0] = jnp.transpose(lse)[0:8, :]

    return pl.pallas_call(
        body, name=name, grid=(N_HEADS // nh, nq),
        in_specs=[pl.BlockSpec((t, nh * Q_EXT), lambda h, i: (i, h)), pl.BlockSpec((nh, nq, Q_EXT, t), lambda h, i: (h, 0, 0, 0)),
                  pl.BlockSpec((s, nh * Q_EXT), lambda h, i: (0, h))],
        out_specs=(pl.BlockSpec((t, nh * V_HEAD), lambda h, i: (i, h)), pl.BlockSpec((nh, 1, 8, t), lambda h, i: (h, i, 0, 0))),
        out_shape=(jax.ShapeDtypeStruct((s, N_HEADS * V_HEAD), BF16), jax.ShapeDtypeStruct((N_HEADS, nq, 8, t), F32)),
        scratch_shapes=[pltpu.VMEM((nh, t, Q_EXT), F32), pltpu.VMEM((nh, t, 1), F32)],
        compiler_params=_params(("parallel", "parallel")),
    )(q_rot, kt4, v_ext)


def _attn_delta(name, o, do):
    s = o.shape[0]
    t = min(ATT_TILE, s)

    def body(o_ref, do_ref, delta_ref):
        prod = do_ref[...].astype(F32) * o_ref[...].astype(F32)
        for h in range(N_HEADS):
            delta = jnp.sum(prod[:, h * V_HEAD:(h + 1) * V_HEAD], axis=-1, keepdims=True)
            delta_ref[h, 0] = jnp.transpose(jnp.broadcast_to(delta, (t, LANES)))[0:8, :]

    rows = pl.BlockSpec((t, N_HEADS * V_HEAD), lambda i: (i, 0))
    return pl.pallas_call(
        body, name=name, grid=(s // t,), in_specs=[rows, rows],
        out_specs=pl.BlockSpec((N_HEADS, 1, 8, t), lambda i: (0, i, 0, 0)),
        out_shape=jax.ShapeDtypeStruct((N_HEADS, s // t, 8, t), F32),
        compiler_params=_params(("parallel",)),
    )(o, do)


def _attn_bwd(name, kfull, v, qt4, q_rot, dot4, do, lse_row, delta_row, tabq, acc_in=None):
    s = kfull.shape[0]
    t = min(ATT_TILE, s)
    nq = s // t
    nh = ATT_BWD_HEADS
    has_in = acc_in is not None

    def body(*refs):
        k_ref, v_ref, qt_ref, q_ref, dot_ref, do_ref, lse_ref, delta_ref, tab_ref = refs[:9]
        dq_ref, dkn_ref, dkd_ref, dv_ref, dq_acc_ref, acck_ref, accv_ref = refs[-7:]
        kj = pl.program_id(1)

        @pl.when(kj == 0)
        def _():
            dq_acc_ref[...] = jnp.zeros_like(dq_acc_ref)

        acck_ref[...] = jnp.zeros_like(acck_ref)
        accv_ref[...] = jnp.zeros_like(accv_ref)

        def step(i, masked):
            qs = pl.ds(pl.multiple_of(i * t, t), t)
            for hh in range(nh):
                qc = slice(hh * Q_EXT, (hh + 1) * Q_EXT)
                vc = slice(hh * LANES, (hh + 1) * LANES)
                k_blk = k_ref[:, qc]
                st = _dot(k_blk, qt_ref[hh, i], NN)
                if masked:
                    krow = lax.broadcasted_iota(jnp.int32, st.shape, 0)
                    qcol = lax.broadcasted_iota(jnp.int32, st.shape, 1)
                    st = jnp.where(krow <= qcol, st, NEG_BIG)
                pt = jnp.exp2(st - lse_ref[hh, i, 0:1, :])
                accv_ref[hh] += _dot(pt.astype(BF16), do_ref[qs, vc], NN)
                dpt = _dot(v_ref[:, vc], dot_ref[hh, i], NN)
                dst = (pt * (dpt - delta_ref[hh, i, 0:1, :])).astype(BF16)
                acck_ref[hh] += _dot(dst, q_ref[qs, qc], NN)
                dq_acc_ref[hh, qs, :] += _dot(dst, k_blk, TN)

        def full_step(i, carry):
            step(i, False)
            return carry

        step(kj, True)
        lax.fori_loop(kj + 1, nq, full_step, 0)
        for hh in range(nh):
            vc = slice(hh * LANES, (hh + 1) * LANES)
            dk = acck_ref[hh] * LN2
            dkn, dkd, dv = dk[:, :QK_NOPE], dk[:, QK_NOPE:], accv_ref[hh]
            if has_in:
                dkn, dkd, dv = dkn + refs[9][:, vc], dkd + refs[10][:, vc], dv + refs[11][:, vc]
            dkn_ref[:, vc], dkd_ref[:, vc], dv_ref[:, vc] = dkn, dkd, dv

        @pl.when(kj == nq - 1)
        def _():
            for hh in range(nh):
                dq_ref[:, hh * Q_EXT:(hh + 1) * Q_EXT] = (dq_acc_ref[hh] * (tab_ref[...] * LN2)).astype(dq_ref.dtype)

    kblk = pl.BlockSpec((t, nh * LANES), lambda h, j: (j, h))
    col = pl.BlockSpec((s, nh * LANES), lambda h, j: (0, h))
    q_all = pl.BlockSpec((s, nh * Q_EXT), lambda h, j: (0, h))
    stat = pl.BlockSpec((nh, nq, 8, t), lambda h, j: (h, 0, 0, 0))
    ins = [kfull, v, qt4, q_rot, dot4, do, lse_row, delta_row, tabq]
    in_specs = [pl.BlockSpec((t, nh * Q_EXT), lambda h, j: (j, h)), kblk, pl.BlockSpec((nh, nq, Q_EXT, t), lambda h, j: (h, 0, 0, 0)),
                q_all, pl.BlockSpec((nh, nq, V_HEAD, t), lambda h, j: (h, 0, 0, 0)), col, stat, stat,
                pl.BlockSpec((s, Q_EXT), lambda h, j: (0, 0))]
    if has_in:
        ins += list(acc_in)
        in_specs += [kblk, kblk, kblk]
    wide = jax.ShapeDtypeStruct((s, N_HEADS * LANES), F32)
    return pl.pallas_call(
        body, name=name, grid=(N_HEADS // nh, nq), in_specs=in_specs, out_specs=(q_all, kblk, kblk, kblk),
        out_shape=(jax.ShapeDtypeStruct((s, N_HEADS * Q_EXT), BF16), wide, wide, wide),
        scratch_shapes=[pltpu.VMEM((nh, s, Q_EXT), F32), pltpu.VMEM((nh, t, Q_EXT), F32), pltpu.VMEM((nh, t, LANES), F32)],
        compiler_params=pltpu.CompilerParams(dimension_semantics=("parallel", "arbitrary"), vmem_limit_bytes=ATT_BWD_VMEM_BYTES),
    )(*ins)


def _swap_halves(w):
    half = w.shape[-1] // 2
    return jnp.concatenate([-w[..., half:], w[..., :half]], axis=-1)


def _unswap_halves(g):
    half = g.shape[-1] // 2
    return jnp.concatenate([g[..., half:], -g[..., :half]], axis=-1)


def _extend_w_dkv(w):
    return jnp.concatenate([w, _swap_halves(w[:, KV_RANK:])], axis=-1)


def _fold_w_dkv_grad(g):
    rope = g[:, KV_RANK:KV_RANK + QK_ROPE] + _unswap_halves(g[:, KV_RANK + QK_ROPE:])
    return jnp.concatenate([g[:, :KV_RANK], rope], axis=-1)


def _rope_tables(positions):
    inv = 1.0 / (ROPE_THETA ** (jnp.arange(0, QK_ROPE, 2, dtype=F32) / QK_ROPE))
    ang = positions.astype(F32)[:, None] * inv
    cos, sin = jnp.cos(ang), jnp.sin(ang)
    tabk = jnp.concatenate([cos, cos, sin, sin], axis=-1)
    scale = QK_HEAD ** -0.5 * LOG2E
    tabq = jnp.concatenate([jnp.full((positions.shape[0], QK_NOPE), scale, F32), tabk * scale], axis=-1)
    return tabq, tabk


def _forward_backward(x, target, mods, tabq, tabk, norm1_all, final_g, fetch, push):
    row = lambda vec: vec.reshape(1, -1)
    mod = [[row(mods[l, k * D_MODEL:(k + 1) * D_MODEL]) for k in range(N_MOD)] for l in range(DEPTH)]
    saved, weights = [], []
    kv = h1 = None
    for l in range(DEPTH):
        w, tok = fetch(l, x)
        sh1, sc1, g1, sh2, sc2, g2 = mod[l]
        g1 = g1 + tok
        norm2 = (row(w["norm2_g"]), sc2, sh2)
        if l == N_A_LAYERS:
            kvn = _rms_fwd("kvin_fwd", x, row(w["kv_in_g"]))
            kv_ext = _mm("dkv_fwd", kvn, w["w_dkv_ext"], out_dtype=F32)
            ckv = _rms_fwd("ckv_fwd", kv_ext, row(w["ckv_norm_g"]), ncols=KV_RANK)
            kd = _krope_fwd("krope_fwd", kv_ext, tabk)
            kn, v = _mm("uk_fwd", ckv, w["w_uk"]), _mm("uv_fwd", ckv, w["w_uv"])
            heads = lambda a: [a[:, h * LANES:(h + 1) * LANES] for h in range(N_HEADS)]
            kfull = jnp.concatenate([part for kh in heads(kn) for part in (kh, kd)], axis=-1)
            v_ext = jnp.concatenate([part for vh in heads(v) for part in (vh, jnp.ones_like(vh))], axis=-1)
            kv = dict(x=x, kvn=kvn, kv_ext=kv_ext, ckv=ckv, v=v, kfull=kfull, v_ext=v_ext,
                      kt4=_head_blocks_t(kfull, Q_EXT))
        x_in = x
        if l == 0:
            h1 = _rms_fwd("norm1_fwd_0", x, row(norm1_all[0]), sc1, sh1, out_dtype=F32)
        if l < N_A_LAYERS:
            x_mid, zb, pooled, h2 = _pool_fwd(f"pool_fwd_{l}", h1, x, w["pool_w"], row(w["pool_b"]), row(w["pool_scale"]), g1, norm2)
            mix = (zb, pooled)
        else:
            cq_pre = _mm(f"dq_fwd_{l}", h1, w["w_dq"], out_dtype=F32)
            cq = _rms_fwd(f"qnorm_fwd_{l}", cq_pre, row(w["q_norm_g"]))
            q_rot = _mm(f"uq_fwd_{l}", cq, w["w_uq_ext"], rowtab=tabq)
            o, lse_row = _attn_fwd(f"attn_fwd_{l}", q_rot, kv["kt4"], kv["v_ext"])
            x_mid, h2 = _mm(f"wo_fwd_{l}", o, w["w_o"], resid=x, gate=g1, norm=norm2 + (BF16,))
            mix = (h1, cq_pre, cq, q_rot, o, lse_row)
        w_up_t = w["w_up"](h2)
        ua, gl, gpv, ge = _up_glu_fwd(f"up_glu_fwd_{l}", h2, w_up_t, w["conv_w"], row(w["conv_b"]))
        w_down = w["w_down"](gl)
        if l + 1 < DEPTH:
            nxt = (row(norm1_all[l + 1]), mod[l + 1][1], mod[l + 1][0], F32 if l + 1 < N_A_LAYERS else BF16)
            x, h1 = _mm(f"down_fwd_{l}", gl, w_down, resid=x_mid, gate=g2, norm=nxt)
        else:
            x = _mm(f"down_fwd_{l}", gl, w_down, resid=x_mid, gate=g2)
        saved.append((x_in, x_mid, h2, ua, gpv, ge, gl, mix))
        weights.append(dict(w, w_up_t=w_up_t, w_down=w_down))

    dx, dfinal_g, loss = _loss_head("loss_head", x, row(final_g), target)
    g = {"final_g": dfinal_g.reshape(-1)}
    per_layer = {k: [None] * DEPTH for k in ("norm1_g", "norm2_g", "conv_w", "conv_b")}
    per_a = {k: [None] * N_A_LAYERS for k in ("pool_b", "pool_scale")}
    per_b = {k: [None] * N_B_LAYERS for k in ("q_norm_g",)}
    dmods = [None] * DEPTH
    dkv = None
    tok = 0.0
    for l in reversed(range(DEPTH)):
        w, big = weights[l], {}
        sh1, sc1, g1, sh2, sc2, g2 = mod[l]
        g2 = g2 + tok
        x_in, x_mid, h2, ua, gpv, ge, gl, mix = saved[l]
        dw_down, dg2 = _mm(f"down_wgrad_{l}", gl, dx, mode="tn", tm_cap=1408, gate_grad=(w["w_down"], g2))
        tok = push(l, "down", dict(w_down=dw_down), None)
        da, dv_, dcw, dcb = _down_glu_bwd(f"down_glu_bwd_{l}", dx, g2, w["w_down"], ua, gpv, ge, w["conv_w"] + tok)
        dh2 = _mm(f"up_bwd_{l}", da, w["w_up_t"][:D_FF], out_dtype=F32, second=(dv_, w["w_up_t"][D_FF:]))
        tok = push(l, "up", dict(w_up_t_a=_mm(f"up_a_wgrad_{l}", da, h2, mode="tn", tm_cap=1408),
                                 w_up_t_v=_mm(f"up_v_wgrad_{l}", dv_, h2, mode="tn", tm_cap=1408)), None)
        per_layer["conv_w"][l], per_layer["conv_b"][l] = dcw, dcb.reshape(-1)
        dx_mid, dn2, dsh2, dsc2 = _rms_bwd(f"norm2_bwd_{l}", x_mid, row(w["norm2_g"]), dh2, sc2 + tok, dx_in=dx)
        per_layer["norm2_g"][l] = dn2.reshape(-1)
        if l < N_A_LAYERS:
            zb, pooled = mix
            dh1, dpw, dpb, dps, dg1 = _pool_bwd(f"pool_bwd_{l}", dx_mid, zb, pooled, w["pool_w"], row(w["pool_scale"]), g1)
            big["pool_w"] = dpw
            per_a["pool_b"][l], per_a["pool_scale"][l] = dpb.reshape(-1), dps.reshape(-1)
        else:
            j = l - N_A_LAYERS
            h1, cq_pre, cq, q_rot, o, lse_row = mix
            do = _mm(f"wo_bwd_{l}", dx_mid, w["w_o"], mode="nt", a_scale=g1)
            big["w_o"], dg1 = _mm(f"wo_wgrad_{l}", o, dx_mid, mode="tn", gate_grad=(w["w_o"], g1))
            delta_row = _attn_delta(f"attn_delta_{l}", o, do)
            dq_ext, *dkv = _attn_bwd(f"attn_bwd_{l}", kv["kfull"], kv["v"], _head_blocks_t(q_rot, Q_EXT), q_rot, _head_blocks_t(do, V_HEAD), do,
                                     lse_row, delta_row, tabq, acc_in=dkv)
            dcq = _mm(f"uq_bwd_{l}", dq_ext, w["w_uq_ext"], mode="nt", out_dtype=F32)
            big["w_uq_ext"] = _mm(f"uq_wgrad_{l}", cq, dq_ext, mode="tn", out_dtype=F32)
            dcq_pre, dqn = _rms_bwd(f"qnorm_bwd_{l}", cq_pre, row(w["q_norm_g"]), dcq, out_dtype=BF16)
            per_b["q_norm_g"][j] = dqn.reshape(-1)
            dh1 = _mm(f"dq_bwd_{l}", dcq_pre, w["w_dq"], mode="nt")
            big["w_dq"] = _mm(f"dq_wgrad_{l}", h1, dcq_pre, mode="tn")
        dx, dn1, dsh1, dsc1 = _rms_bwd(f"norm1_bwd_{l}", x_in, row(w["norm1_g"]), dh1, sc1, dx_in=dx_mid)
        per_layer["norm1_g"][l] = dn1.reshape(-1)
        dmods[l] = jnp.concatenate([dsh1, dsc1, dg1, dsh2, dsc2, dg2], axis=-1).reshape(-1)
        if l == N_A_LAYERS:
            dkn, dkd, dv = dkv
            dckv = _mm("ukv_bwd", dkn, w["w_uk"], mode="nt", out_dtype=F32, second=(dv, w["w_uv"]))
            big["w_uk"] = _mm("uk_wgrad", kv["ckv"], dkn, mode="tn")
            big["w_uv"] = _mm("uv_wgrad", kv["ckv"], dv, mode="tn")
            dkr = _krope_bwd("krope_bwd", dkd, tabk)
            dc, dckv_g = _rms_bwd("ckv_bwd", kv["kv_ext"], row(w["ckv_norm_g"]), dckv, ncols=KV_RANK, out_dtype=BF16)
            dkv_ext = jnp.concatenate([dc, dkr.astype(BF16)], axis=-1)
            dkvn = _mm("dkv_bwd", dkv_ext, w["w_dkv_ext"], mode="nt")
            big["w_dkv_ext"] = _mm("dkv_wgrad", kv["kvn"], dkv_ext, mode="tn", out_dtype=F32)
            dx, dkv_in_g = _rms_bwd("kvin_bwd", kv["x"], row(w["kv_in_g"]), dkvn, dx_in=dx)
            g["ckv_norm_g"], g["kv_in_g"] = dckv_g.reshape(-1), dkv_in_g.reshape(-1)
        tok = push(l, "mix", big, dx)
    for group in (per_layer, per_a, per_b):
        for k, vals in group.items():
            g[k] = jnp.stack(vals)
    return loss, dx, g, jnp.stack(dmods)


def _my_index():
    return 4 * lax.axis_index("x") + 2 * lax.axis_index("y") + lax.axis_index("c")


def _peer(k):
    x, y, c = lax.axis_index("x"), lax.axis_index("y"), lax.axis_index("c")
    return (1 - x if k & 4 else x, 1 - y if k & 2 else y, 1 - c if k & 1 else c)


def _index_of(pos):
    return 4 * pos[0] + 2 * pos[1] + pos[2]


def _exchange_many(name, arrays, scatter):
    n = len(arrays)
    blocks = [tuple(a.shape[1:]) if scatter else tuple(a.shape) for a in arrays]

    def body(*refs):
        x_refs, o_refs = refs[:n], refs[n:2 * n]
        send_sems, recv_sems, local_sems = refs[2 * n:]
        me = _my_index()
        started = []
        for a in range(n):
            mine = pltpu.make_async_copy(x_refs[a].at[me] if scatter else x_refs[a], o_refs[a].at[me], local_sems.at[a])
            mine.start()
            started.append(mine)
        sends = []
        for k in range(1, N_DEV):
            peer = _peer(k)
            for a in range(n):
                cp = pltpu.make_async_remote_copy(
                    src_ref=x_refs[a].at[_index_of(peer)] if scatter else x_refs[a], dst_ref=o_refs[a].at[me],
                    send_sem=send_sems.at[a, k - 1], recv_sem=recv_sems.at[a, k - 1], device_id=peer, device_id_type=MESH)
                cp.start()
                sends.append(cp)
        for k in range(1, N_DEV):
            peer = _peer(k)
            for a in range(n):
                pltpu.make_async_remote_copy(
                    src_ref=x_refs[a].at[me] if scatter else x_refs[a], dst_ref=o_refs[a].at[_index_of(peer)],
                    send_sem=send_sems.at[a, k - 1], recv_sem=recv_sems.at[a, k - 1], device_id=peer, device_id_type=MESH).wait_recv()
        for cp in sends:
            cp.wait_send()
        for mine in started:
            mine.wait()

    return pl.pallas_call(
        body, name=name, out_shape=tuple(jax.ShapeDtypeStruct((N_DEV,) + blk, a.dtype) for blk, a in zip(blocks, arrays)),
        in_specs=[pl.BlockSpec(memory_space=pl.ANY)] * n, out_specs=tuple([pl.BlockSpec(memory_space=pl.ANY)] * n),
        scratch_shapes=[pltpu.SemaphoreType.DMA((n, N_DEV - 1)), pltpu.SemaphoreType.DMA((n, N_DEV - 1)), pltpu.SemaphoreType.DMA((n,))],
    )(*arrays)


def _exchange(name, x, scatter):
    return _exchange_many(name, [x], scatter)[0]


HBM_SPEC = pl.BlockSpec(memory_space=pltpu.HBM)
SEM_SPEC = pl.BlockSpec(memory_space=pltpu.SEMAPHORE)
DATAFLOW = pltpu.SideEffectType.DATAFLOW_SIDE_EFFECTING


def _remote_copies(x_refs, land_refs, send_sems, recv_sems, scatter, numbers=None):
    me = _my_index()
    numbers = list(range(len(x_refs))) if numbers is None else numbers
    out, inc = [], []
    for a in range(len(x_refs)):
        for k in range(1, N_DEV):
            peer = _peer(k)
            pair = numbers[a] * (N_DEV - 1) + k - 1
            sems = dict(send_sem=send_sems.at[pair], recv_sem=recv_sems.at[pair], device_id=peer, device_id_type=MESH)
            out.append(pltpu.make_async_remote_copy(
                src_ref=x_refs[a].at[_index_of(peer)] if scatter else x_refs[a], dst_ref=land_refs[a].at[me], **sems))
            inc.append(pltpu.make_async_remote_copy(
                src_ref=x_refs[a].at[me] if scatter else x_refs[a], dst_ref=land_refs[a].at[_index_of(peer)], **sems))
    return out, inc


def _exchange_start(name, arrays, scatter):
    n = len(arrays)
    blocks = [tuple(a.shape[1:]) if scatter else tuple(a.shape) for a in arrays]

    def body(*refs):
        x_refs, land_refs = refs[:n], refs[n:2 * n]
        send_sems, recv_sems = refs[2 * n], refs[2 * n + 1]
        for cp in _remote_copies(x_refs, land_refs, send_sems, recv_sems, scatter)[0]:
            cp.start()
        refs[-1][...] = jnp.zeros_like(refs[-1])

    sem_type = pltpu.SemaphoreType.DMA((n * (N_DEV - 1),))
    lands =[pltpu.with_memory_space_constraint(lax.empty((N_DEV,) + blk, a.dtype), pltpu.HBM) for blk, a in zip(blocks, arrays)]
    srcs = [pltpu.with_memory_space_constraint(a, pltpu.HBM) for a in arrays]
    res = pl.pallas_call(
        body, name=name,
        out_shape=(sem_type, sem_type, *[pltpu.HBM(a.shape, a.dtype) for a in srcs + lands], jax.ShapeDtypeStruct((8, LANES), F32)),
        in_specs=[HBM_SPEC] * (2 * n), out_specs=(SEM_SPEC, SEM_SPEC, *[HBM_SPEC] * (2 * n), pl.BlockSpec(memory_space=pltpu.VMEM)),
        input_output_aliases={i: 2 + i for i in range(2 * n)},
        compiler_params=pltpu.CompilerParams(has_side_effects=DATAFLOW),
    )(*srcs, *lands)
    return (res[0], res[1], list(res[2:2 + n]), list(res[2 + n:2 + 2 * n])), res[-1]


def _exchange_wait(name, handles, after, scatter, which=None):
    send_sems, recv_sems, srcs, lands = handles
    which = list(range(len(srcs))) if which is None else list(which)
    srcs, lands = [srcs[a] for a in which], [lands[a] for a in which]
    n = len(srcs)

    def body(*refs):
        x_refs, land_refs = refs[:n], refs[n:2 * n]
        out, inc = _remote_copies(x_refs, land_refs, refs[2 * n], refs[2 * n + 1], scatter, which)
        for cp in out:
            cp.wait_send()
        for cp in inc:
            cp.wait_recv()

    res = pl.pallas_call(
        body, name=name, out_shape=tuple(pltpu.HBM(a.shape, a.dtype) for a in srcs + lands),
        in_specs=[HBM_SPEC] * (2 * n) + [SEM_SPEC, SEM_SPEC, pl.BlockSpec(memory_space=pl.ANY)], out_specs=tuple([HBM_SPEC] * (2 * n)),
        input_output_aliases={i: i for i in range(2 * n)},
        compiler_params=pltpu.CompilerParams(has_side_effects=DATAFLOW),
    )(*srcs, *lands, send_sems, recv_sems, after)
    return list(res[n:])


def _pack(arrays, dtype, row_multiple):
    flat = jnp.concatenate([a.astype(dtype).reshape(-1) for a in arrays])
    rows = -(-flat.shape[0] // (LANES * row_multiple)) * row_multiple
    return jnp.pad(flat, (0, rows * LANES - flat.shape[0])).reshape(rows, LANES)


def _unpack(packed, shapes):
    lead = packed.shape[:-2]
    flat = packed.reshape(lead + (-1,))
    out, off = [], 0
    for shp in shapes:
        size = 1
        for d in shp:
            size *= d
        out.append(flat[..., off:off + size].reshape(lead + tuple(shp)))
        off += size
    return out


def _unshard(g8, axis):
    return jnp.concatenate([g8[j] for j in range(N_DEV)], axis=axis)


def _shard8(full, axis):
    n = full.shape[axis] // N_DEV
    return jnp.stack([lax.slice_in_dim(full, j * n, (j + 1) * n, axis=axis) for j in range(N_DEV)])


VECTOR_WEIGHTS = (("pool_b", 1), ("pool_scale", 1), ("conv_w", 2))
REPLICATED_WEIGHTS = ("norm1_g", "norm2_g", "kv_in_g", "ckv_norm_g", "q_norm_g", "conv_b", "final_g")
WEIGHT_ORDER = ("mod_w", "mod_b", "norm1_g", "norm2_g", "pool_w", "pool_b", "pool_scale", "kv_in_g", "w_dkv", "ckv_norm_g", "w_uk",
                "w_uv", "w_dq", "q_norm_g", "w_uq", "w_o", "w_up", "conv_w", "conv_b", "w_down", "final_g")
SMALL_ROW_MULTIPLE = 16


def _as_2d(a):
    if a.ndim == 1:
        return a.reshape(-1, LANES)
    return a.reshape(-1, a.shape[-1])


def kernel(x, c, positions, mod_w, mod_b, norm1_g, norm2_g, pool_w, pool_b, pool_scale, kv_in_g, w_dkv, ckv_norm_g, w_uk, w_uv, w_dq, q_norm_g, w_uq, w_o, w_up, conv_w, conv_b, w_down, final_g, loss_target, m_mod_w, m_mod_b, m_norm1_g, m_norm2_g, m_pool_w, m_pool_b, m_pool_scale, m_kv_in_g, m_w_dkv, m_ckv_norm_g, m_w_uk, m_w_uv, m_w_dq, m_q_norm_g, m_w_uq, m_w_o, m_w_up, m_conv_w, m_conv_b, m_w_down, m_final_g, v_mod_w, v_mod_b, v_norm1_g, v_norm2_g, v_pool_w, v_pool_b, v_pool_scale, v_kv_in_g, v_w_dkv, v_ckv_norm_g, v_w_uk, v_w_uv, v_w_dq, v_q_norm_g, v_w_uq, v_w_o, v_w_up, v_conv_w, v_conv_b, v_w_down, v_final_g):
    shard = dict(mod_w=mod_w, mod_b=mod_b, norm1_g=norm1_g, norm2_g=norm2_g, pool_w=pool_w, pool_b=pool_b, pool_scale=pool_scale,
                 kv_in_g=kv_in_g, w_dkv=w_dkv, ckv_norm_g=ckv_norm_g, w_uk=w_uk, w_uv=w_uv, w_dq=w_dq, q_norm_g=q_norm_g, w_uq=w_uq,
                 w_o=w_o, w_up=w_up, conv_w=conv_w, conv_b=conv_b, w_down=w_down, final_g=final_g)
    mom_m = dict(mod_w=m_mod_w, mod_b=m_mod_b, norm1_g=m_norm1_g, norm2_g=m_norm2_g, pool_w=m_pool_w, pool_b=m_pool_b,
                 pool_scale=m_pool_scale, kv_in_g=m_kv_in_g, w_dkv=m_w_dkv, ckv_norm_g=m_ckv_norm_g, w_uk=m_w_uk, w_uv=m_w_uv,
                 w_dq=m_w_dq, q_norm_g=m_q_norm_g, w_uq=m_w_uq, w_o=m_w_o, w_up=m_w_up, conv_w=m_conv_w, conv_b=m_conv_b,
                 w_down=m_w_down, final_g=m_final_g)
    mom_v = dict(mod_w=v_mod_w, mod_b=v_mod_b, norm1_g=v_norm1_g, norm2_g=v_norm2_g, pool_w=v_pool_w, pool_b=v_pool_b,
                 pool_scale=v_pool_scale, kv_in_g=v_kv_in_g, w_dkv=v_w_dkv, ckv_norm_g=v_ckv_norm_g, w_uk=v_w_uk, w_uv=v_w_uv,
                 w_dq=v_w_dq, q_norm_g=v_q_norm_g, w_uq=v_w_uq, w_o=v_w_o, w_up=v_w_up, conv_w=v_conv_w, conv_b=v_conv_b,
                 w_down=v_w_down, final_g=v_final_g)
    me = _my_index()
    d6 = N_MOD * D_MODEL
    mod_cols = d6 // N_DEV

    small_in = [c] + [shard[k] for k, _ in VECTOR_WEIGHTS]
    small_all = _exchange("gather_vectors", _pack(small_in, F32, SMALL_ROW_MULTIPLE), scatter=False)
    parts = _unpack(small_all, [a.shape for a in small_in])
    c_all = jnp.pad(parts[0].reshape(N_DEV, D_MODEL), ((0, N_DEV), (0, 0)))
    vec = {k: _unshard(p, ax) for (k, ax), p in zip(VECTOR_WEIGHTS, parts[1:])}

    my_mod_b = lax.dynamic_slice_in_dim(mod_b, me * mod_cols, mod_cols, axis=1)
    mods_mine = _mods_fwd("mods_fwd", c_all, mod_w, my_mod_b)
    mods_all = _exchange("gather_mods", _pack([mods_mine], F32, SMALL_ROW_MULTIPLE), scatter=False)
    mods_all = _unpack(mods_all, [mods_mine.shape])[0]
    mods = lax.dynamic_index_in_dim(mods_all, me, axis=2, keepdims=False)
    mods = jnp.moveaxis(mods, 0, 1).reshape(DEPTH, d6)

    tabq, tabk = _rope_tables(positions[0])
    half = N_DEV // 2
    up_view = lambda a: jnp.swapaxes(a, 1, 2)
    w_up_t = up_view(shard["w_up"])
    up_cols = w_up_t.shape[1]
    cat = lambda a, axis, lo=0, hi=N_DEV: jnp.concatenate([a[j] for j in range(lo, hi)], axis=axis)

    def stage_pieces(l):
        out = {"pool_w": shard["pool_w"].astype(BF16)} if l == 0 else {}
        if l == N_A_LAYERS:
            out.update({k: shard[k].astype(BF16) for k in ("w_dkv", "w_uk", "w_uv")})
        if l >= N_A_LAYERS:
            out.update({k: shard[k][l - N_A_LAYERS].astype(BF16) for k in ("w_dq", "w_uq", "w_o")})
        out.update(w_up=w_up_t[l].astype(BF16), w_down=shard["w_down"][l].astype(BF16))
        return out

    gathers, pool_all = {}, []

    def start_gather(l, behind=None):
        pieces = stage_pieces(l)
        if behind is not None:
            pieces, _ = lax.optimization_barrier((pieces, behind))
        handles, token = _exchange_start(f"gather_start_{l}", list(pieces.values()), scatter=False)
        gathers[l] = (handles, pieces)
        return token[0, 0]

    def wait_gather(l, keys, after, tag=""):
        handles, pieces = gathers[l]
        which = [list(pieces).index(k) for k in keys]
        lands = _exchange_wait(f"gather_wait_{l}{tag}", handles, after, scatter=False, which=which)
        return dict(zip(keys, own_slot(lands, [pieces[k] for k in keys])))

    def whole_weights(l, got):
        w = dict(norm1_g=norm1_g[l], norm2_g=norm2_g[l], conv_w=vec["conv_w"][l], conv_b=conv_b[l])
        if l == 0:
            pool_all.append(got["pool_w"])
        if l < N_A_LAYERS:
            w.update(pool_w=cat(pool_all[0][:, l], 1), pool_b=vec["pool_b"][l], pool_scale=vec["pool_scale"][l])
        else:
            rope = got["w_uq"][..., QK_NOPE:]
            ext = jnp.concatenate([got["w_uq"][..., :QK_NOPE], rope, _swap_halves(rope)], axis=-1)
            w.update(w_dq=got["w_dq"].reshape(D_MODEL, Q_RANK), w_uq_ext=cat(ext, -1), w_o=got["w_o"].reshape(D_MODEL, D_MODEL),
                     q_norm_g=q_norm_g[l - N_A_LAYERS])
        if l == N_A_LAYERS:
            w.update(w_dkv_ext=_extend_w_dkv(got["w_dkv"].reshape(D_MODEL, KV_RANK + QK_ROPE)), w_uk=cat(got["w_uk"], -1),
                     w_uv=cat(got["w_uv"], -1), kv_in_g=kv_in_g, ckv_norm_g=ckv_norm_g)
        return w

    def own_slot(lands, own):
        return [lax.dynamic_update_index_in_dim(p, o, me, 0) for p, o in zip(lands, own)]

    def fetch(l, after):
        up_parts = lambda g8: g8.reshape(N_DEV * up_cols, D_MODEL)
        if l == 0:
            start_gather(0, behind=mods)
        first = [k for k in gathers[l][1] if k not in ("w_up", "w_down")]
        got = wait_gather(l, first, mods if l == 0 else after, "_mix") if first else {}
        early = l + 1 < DEPTH and l + 1 > N_A_LAYERS
        tok = start_gather(l + 1) if early else 0.0

        def w_down(aft):
            down = wait_gather(l, ["w_down"], aft, "_down")["w_down"].reshape(D_FF, D_MODEL)
            if l + 1 < DEPTH and not early:
                start_gather(l + 1, behind=down)
            return down

        w = dict(whole_weights(l, got), w_up=lambda aft: up_parts(wait_gather(l, ["w_up"], aft, "_up")["w_up"]), w_down=w_down)
        return w, tok

    scatters, pending, pool_grads, piece_grads = {}, {}, {}, {}

    def reduce_pieces(l, keys, got):
        for k, p in zip(keys, got):
            piece_grads[(k, l)] = _sum8(f"sum_grads_{k}_{l}", p.reshape(N_DEV, -1, p.shape[-1])).reshape(p.shape[1:])

    def start_scatter(name, sent):
        sent = {k: a.astype(BF16) for k, a in sent.items()}
        handles, token = _exchange_start(f"scatter_start_{name}", list(sent.values()), scatter=True)
        scatters[name] = (handles, list(sent), [lax.dynamic_index_in_dim(a, me, 0, keepdims=False) for a in sent.values()])
        return token[0, 0]

    def finish_scatter(name, l, after):
        handles, keys, own = scatters.pop(name)
        reduce_pieces(l, keys, own_slot(_exchange_wait(f"scatter_wait_{name}", handles, after, scatter=True), own))

    def push(l, part, big, after):
        cut = lambda a, n, axis: jnp.stack([lax.slice_in_dim(a, j * n, (j + 1) * n, axis=axis) for j in range(N_DEV)])
        sent = {}
        if part == "down":
            sent["w_down"] = big["w_down"].reshape(N_DEV, D_FF // N_DEV, D_MODEL)
        elif part == "up":
            sent["w_up"] = jnp.concatenate([big[part].reshape(half, up_cols, D_MODEL) for part in ("w_up_t_a", "w_up_t_v")])
        elif l < N_A_LAYERS:
            pool_grads[l] = big["pool_w"]
        else:
            ext = cut(big["w_uq_ext"], Q_EXT, 1)
            rope = ext[..., QK_NOPE:QK_HEAD] + _unswap_halves(ext[..., QK_HEAD:])
            sent.update(w_dq=big["w_dq"].reshape(N_DEV, D_MODEL // N_DEV, Q_RANK), w_uq=jnp.concatenate([ext[..., :QK_NOPE], rope], axis=-1),
                        w_o=big["w_o"].reshape(N_DEV, D_MODEL // N_DEV, D_MODEL))
        if part == "mix" and l == N_A_LAYERS:
            sent.update(w_dkv=_fold_w_dkv_grad(big["w_dkv_ext"]).reshape(N_DEV, D_MODEL // N_DEV, KV_RANK + QK_ROPE),
                        w_uk=cut(big["w_uk"], QK_NOPE, 1), w_uv=cut(big["w_uv"], V_HEAD, 1))
        if l == 0 and part != "mix":
            return start_scatter(f"0_{part}", sent)
        if l == 0:
            finish_scatter("1", 1, after)
            pool = _shard8(jnp.stack([pool_grads[a] for a in range(N_A_LAYERS)]), 2).astype(BF16)
            reduce_pieces(0, ["pool_w"], _exchange_many("scatter_pool_grads", [pool], scatter=True))
            return 0.0
        pending.setdefault(l, {}).update(sent)
        if part != "mix":
            return 0.0
        if l + 1 < DEPTH:
            finish_scatter(str(l + 1), l + 1, after)
        return start_scatter(str(l), pending.pop(l))

    loss_row, dx, g, dmods = _forward_backward(x[0], loss_target[0], mods, tabq, tabk, norm1_g, final_g, fetch, push)
    layers_of = lambda k, ls: jnp.stack([piece_grads[(k, l)] for l in ls])
    grads = dict(w_dkv=piece_grads[("w_dkv", N_A_LAYERS)], w_uk=piece_grads[("w_uk", N_A_LAYERS)], w_uv=piece_grads[("w_uv", N_A_LAYERS)])
    for k in ("w_dq", "w_uq", "w_o"):
        grads[k] = layers_of(k, range(N_A_LAYERS, DEPTH))

    small_names = REPLICATED_WEIGHTS + tuple(k for k, _ in VECTOR_WEIGHTS)
    small_out = [dmods] + [g[k] for k in small_names] + [loss_row]
    small_shapes = [a.shape for a in small_out]
    small_got = _exchange("gather_small_grads", _pack(small_out, F32, SMALL_ROW_MULTIPLE), scatter=False)
    summed = _unpack(_sum8("sum_small_grads", small_got), small_shapes)
    grads["mod_b"] = summed[0]
    for k, s in zip(small_names, summed[1:-1]):
        grads[k] = s
    for k, ax in VECTOR_WEIGHTS:
        n = shard[k].shape[ax]
        grads[k] = lax.dynamic_slice_in_dim(grads[k], me * n, n, axis=ax)
    loss = summed[-1][0, 0]
    dmods_all = _unpack(small_got, small_shapes)[0]
    dm_mine = lax.dynamic_slice_in_dim(dmods_all, me * mod_cols, mod_cols, axis=2)
    dm_mine = jnp.pad(jnp.moveaxis(dm_mine, 0, 1), ((0, 0), (0, N_DEV), (0, 0)))
    grads["mod_w"] = _mods_bwd("mods_bwd", c_all, dm_mine)

    delta, new_m, new_v = {}, {}, {}

    def adamw(k):
        if k == "w_up":
            ops = [w_up_t, grads[k], up_view(mom_m[k]), up_view(mom_v[k])]
            res = _adamw(f"adamw_{k}", *[_as_2d(a) for a in ops])
            grads[k], delta[k], new_m[k], new_v[k] = [up_view(r.reshape(w_up_t.shape)) for r in (ops[1],) + tuple(res)]
            return
        shp = shard[k].shape
        grads[k] = grads[k].reshape(shp)
        res = _adamw(f"adamw_{k}", _as_2d(shard[k]), _as_2d(grads[k]), _as_2d(mom_m[k]), _as_2d(mom_v[k]))
        delta[k], new_m[k], new_v[k] = [r.reshape(shp) for r in res]

    late = ("w_up", "w_down", "pool_w")
    for k in WEIGHT_ORDER:
        if k not in late:
            adamw(k)
    finish_scatter("0_down", 0, delta["final_g"])
    finish_scatter("0_up", 0, delta["final_g"])
    grads.update(w_up=layers_of("w_up", range(DEPTH)), w_down=layers_of("w_down", range(DEPTH)), pool_w=piece_grads[("pool_w", 0)])
    for k in late:
        adamw(k)
    return (loss, dx[None], *[grads[k] for k in WEIGHT_ORDER], *[delta[k] for k in WEIGHT_ORDER],
            *[new_m[k] for k in WEIGHT_ORDER], *[new_v[k] for k in WEIGHT_ORDER])
```

```python
import functools

import jax
import jax.numpy as jnp
from jax import lax
from jax.experimental import pallas as pl
from jax.experimental.pallas import tpu as pltpu

F32 = jnp.float32
BF16 = jnp.bfloat16

D_MODEL = 1024
DEPTH = 4
N_A_LAYERS = 2
N_B_LAYERS = 2
POOL_WINDOWS = (2, 4, 8, 16)
POOL_GROUP = 256
N_HEADS = 8
QK_NOPE = 128
QK_ROPE = 64
V_HEAD = 128
QK_HEAD = QK_NOPE + QK_ROPE
Q_RANK = 384
KV_RANK = 256
ROPE_THETA = 10000.0
D_FF = 2816
EPS = 1e-6
N_MOD = 6
ADAM_LR = 0.001
ADAM_B1 = 0.9
ADAM_B2 = 0.999
ADAM_EPS = 1e-08
ADAM_WD = 0.01
ADAM_STEP = 10

N_DEV = 8
LANES = 128
Q_EXT = 256
VMEM_LIMIT_BYTES = 48 * 1024 * 1024
MESH = pl.DeviceIdType.MESH
NEG_BIG = -0.7 * float(jnp.finfo(jnp.float32).max)


def _params(sem):
    return pltpu.CompilerParams(dimension_semantics=sem, vmem_limit_bytes=VMEM_LIMIT_BYTES)


def _tile(n, cap):
    if n <= cap:
        return n
    best = None
    for d in range(LANES, cap + 1, LANES):
        if n % d == 0:
            best = d
    assert best is not None, (n, cap)
    return best


def _dot(a, b, dims):
    return lax.dot_general(a, b, (dims, ((), ())), preferred_element_type=F32)


NN = ((1,), (0,))
NT = ((1,), (1,))
TN = ((0,), (0,))


def _modulated_rmsnorm(xv, gv, scale, shift):
    return xv * lax.rsqrt(jnp.mean(xv * xv, axis=-1, keepdims=True) + EPS) * gv * (1.0 + scale) + shift


def _mm(name, a, b, mode="nn", out_dtype=BF16, resid=None, gate=None, norm=None, rowtab=None, second=None, a_scale=None,
        gate_grad=None, tm_cap=1024, tn_cap=1408, tk_cap=1408):
    if mode == "tn":
        kdim, m = a.shape
    else:
        m, kdim = a.shape
    n = b.shape[0] if mode == "nt" else b.shape[1]
    tm, tn, tk = _tile(m, tm_cap), _tile(n, tn_cap), _tile(kdim, tk_cap)
    nk = kdim // tk
    dims = {"nn": NN, "nt": NT, "tn": TN}[mode]
    a_spec = pl.BlockSpec((tk, tm), lambda i, j, k: (k, i)) if mode == "tn" else pl.BlockSpec((tm, tk), lambda i, j, k: (i, k))
    b_spec = pl.BlockSpec((tn, tk), lambda i, j, k: (j, k)) if mode == "nt" else pl.BlockSpec((tk, tn), lambda i, j, k: (k, j))
    o_spec = pl.BlockSpec((tm, tn), lambda i, j, k: (i, j))
    g_spec = pl.BlockSpec((1, tn), lambda i, j, k: (0, j))
    gated = resid is not None
    assert sum(x is not None for x in (resid, rowtab, gate_grad)) <= 1
    n_ops = 2 if second is None else 4
    n_extra = 1 if a_scale is not None else 0

    def body(*refs):
        acc = refs[-1]
        i, k = pl.program_id(0), pl.program_id(2)

        @pl.when(k == 0)
        def _():
            acc[...] = jnp.zeros_like(acc)

        av = refs[0][...]
        if a_scale is not None:
            av = av.astype(F32) * refs[n_ops][...]
        prod = _dot(av.astype(BF16), refs[1][...].astype(BF16), dims)
        if second is not None:
            prod = prod + _dot(refs[2][...].astype(BF16), refs[3][...].astype(BF16), dims)
        acc[...] += prod
        rest = refs[n_ops + n_extra:-1]

        if gate_grad is not None:
            @pl.when((i == 0) & (k == 0))
            def _():
                rest[3][...] = jnp.zeros_like(rest[3])

        @pl.when(k == nk - 1)
        def _():
            if gated and norm is not None:
                r_ref, g_ref, ng_ref, sc_ref, sh_ref, x_ref, h_ref = rest
                xn = r_ref[...] + g_ref[...] * acc[...]
                x_ref[...] = xn
                h_ref[...] = _modulated_rmsnorm(xn, ng_ref[...], sc_ref[...], sh_ref[...]).astype(h_ref.dtype)
            elif gated:
                r_ref, g_ref, x_ref = rest
                x_ref[...] = r_ref[...] + g_ref[...] * acc[...]
            elif rowtab is not None:
                tab = rest[0][...]
                rest[1][...] = (acc[...] * jnp.concatenate([tab] * (tn // tab.shape[1]), axis=1)).astype(out_dtype)
            elif gate_grad is not None:
                w_ref, g_ref, o_ref, dg_ref = rest
                o_ref[...] = (acc[...] * g_ref[...]).astype(out_dtype)
                dg_ref[...] += _colsum(w_ref[...].astype(F32) * acc[...])
            else:
                rest[0][...] = acc[...].astype(out_dtype)

    ins, in_specs = [a, b], [a_spec, b_spec]
    if second is not None:
        assert second[0].shape == a.shape and second[1].shape == b.shape
        ins += list(second)
        in_specs += [a_spec, b_spec]
    if a_scale is not None:
        assert mode != "tn"
        ins.append(a_scale)
        in_specs.append(pl.BlockSpec((1, tk), lambda i, j, k: (0, k)))
    out_shape, out_specs = jax.ShapeDtypeStruct((m, n), out_dtype), o_spec
    sem = ("parallel", "parallel", "arbitrary")
    if rowtab is not None:
        assert tn % rowtab.shape[1] == 0
        ins.append(rowtab)
        in_specs.append(pl.BlockSpec((tm, rowtab.shape[1]), lambda i, j, k: (i, 0)))
    if gated:
        ins += [resid, gate]
        in_specs += [o_spec, g_spec]
        out_shape = jax.ShapeDtypeStruct((m, n), F32)
    if norm is not None:
        assert gated and tn == n
        ins += list(norm[:3])
        in_specs += [g_spec] * 3
        out_shape = (out_shape, jax.ShapeDtypeStruct((m, n), norm[3]))
        out_specs = (o_spec, o_spec)
    if gate_grad is not None:
        assert mode == "tn" and tn == n
        ins += list(gate_grad)
        in_specs += [o_spec, g_spec]
        out_shape = (out_shape, jax.ShapeDtypeStruct((1, n), F32))
        out_specs = (o_spec, g_spec)
        sem = ("arbitrary", "arbitrary", "arbitrary")
    return pl.pallas_call(
        body, name=name, grid=(m // tm, n // tn, nk), in_specs=in_specs, out_specs=out_specs, out_shape=out_shape,
        scratch_shapes=[pltpu.VMEM((tm, tn), F32)],
        compiler_params=_params(sem),
    )(*ins)


def _rowwise(name, fn, tiled, bcast, outs, sums=(), tr=512):
    tiled = [t if isinstance(t, tuple) else (t, t.shape[1], 0) for t in tiled]
    s = tiled[0][0].shape[0]
    tr = min(tr, s)
    assert s % tr == 0
    n_t, n_b, n_o = len(tiled), len(bcast), len(outs)

    def body(*refs):
        i = pl.program_id(0)
        vals = [r[...] for r in refs[:n_t + n_b]]
        o_vals, s_vals = fn(*vals)
        for r, v in zip(refs[n_t + n_b:n_t + n_b + n_o], o_vals):
            r[...] = v.astype(r.dtype)
        s_refs = refs[n_t + n_b + n_o:]

        @pl.when(i == 0)
        def _():
            for r in s_refs:
                r[...] = jnp.zeros_like(r)

        for r, v in zip(s_refs, s_vals):
            r[...] += v

    in_specs = [pl.BlockSpec((tr, n), functools.partial(lambda cb, i: (i, cb), cb)) for (_, n, cb) in tiled]
    in_specs += [pl.BlockSpec(b.shape, functools.partial(lambda nd, i: (0,) * nd, b.ndim)) for b in bcast]
    out_specs = [pl.BlockSpec((tr, n), lambda i: (i, 0)) for (n, _) in outs]
    out_specs += [pl.BlockSpec((1, n), lambda i: (0, 0)) for n in sums]
    out_shape = [jax.ShapeDtypeStruct((s, n), dt) for (n, dt) in outs]
    out_shape += [jax.ShapeDtypeStruct((1, n), F32) for n in sums]
    res = pl.pallas_call(
        body, name=name, grid=(s // tr,), in_specs=in_specs, out_specs=tuple(out_specs), out_shape=tuple(out_shape),
        compiler_params=_params(("arbitrary",)),
    )(*[t[0] for t in tiled], *bcast)
    return res


def _colsum(v):
    return jnp.sum(v, axis=0, keepdims=True)


def _rms_fwd(name, x, g, scale=None, shift=None, out_dtype=BF16, ncols=None):
    mod = scale is not None

    def fn(xv, gv, *ss):
        if mod:
            return (_modulated_rmsnorm(xv, gv, ss[0], ss[1]),), ()
        return (xv * lax.rsqrt(jnp.mean(xv * xv, axis=-1, keepdims=True) + EPS) * gv,), ()

    n = ncols or x.shape[1]
    return _rowwise(name, fn, [(x, n, 0)], [g] + ([scale, shift] if mod else []), [(n, out_dtype)])[0]


def _rms_bwd(name, x, g, dh, scale=None, dx_in=None, ncols=None, out_dtype=F32):
    mod = scale is not None
    has_in = dx_in is not None

    def fn(*vals):
        xv, dhv = vals[0], vals[1].astype(F32)
        rest = list(vals[2:])
        dxi = rest.pop(0) if has_in else None
        gv = rest.pop(0)
        rstd = lax.rsqrt(jnp.mean(xv * xv, axis=-1, keepdims=True) + EPS)
        xhat = xv * rstd
        sums = []
        if mod:
            sc = rest.pop(0)
            dyn = dhv * (1.0 + sc)
            dshift, dscale = _colsum(dhv), _colsum(dhv * (xhat * gv))
        else:
            dyn = dhv
        dg = _colsum(dyn * xhat)
        dxhat = dyn * gv
        dx = rstd * (dxhat - xhat * jnp.mean(dxhat * xhat, axis=-1, keepdims=True))
        if has_in:
            dx = dx + dxi
        sums = [dg] + ([dshift, dscale] if mod else [])
        return (dx,), sums

    n = ncols or x.shape[1]
    tiled = [(x, n, 0), dh] + ([dx_in] if has_in else [])
    return _rowwise(name, fn, tiled, [g] + ([scale] if mod else []), [(n, out_dtype)], [n] * (3 if mod else 1))


def _loss_head(name, x, g, target):
    n = x.shape[1]

    def fn(xv, tv, gv):
        rstd = lax.rsqrt(jnp.mean(xv * xv, axis=-1, keepdims=True) + EPS)
        xhat = xv * rstd
        err = xhat * gv - tv
        loss = 0.5 * jnp.sum(jnp.sum(err * err, axis=-1, keepdims=True) / n, axis=0, keepdims=True)
        dy = err / n
        dg = _colsum(dy * xhat)
        dxhat = dy * gv
        dx = rstd * (dxhat - xhat * jnp.mean(dxhat * xhat, axis=-1, keepdims=True))
        return (dx,), (dg, jnp.broadcast_to(loss, (1, LANES)))

    return _rowwise(name, fn, [x, target], [g], [(n, F32)], [n, LANES])


def _krope_fwd(name, kv_ext, tabk):
    def fn(xv, tv):
        t = xv * tv
        return (t + pltpu.roll(t, 64, 1),), ()

    return _rowwise(name, fn, [(kv_ext, LANES, 2), tabk], [], [(LANES, BF16)])[0]


def _krope_bwd(name, dkd, tabk):
    def fn(dv, tv):
        d = dv[:, :LANES]
        for h in range(1, N_HEADS):
            d = d + dv[:, h * LANES:(h + 1) * LANES]
        return ((d + pltpu.roll(d, 64, 1)) * tv,), ()

    return _rowwise(name, fn, [dkd, tabk], [], [(LANES, F32)])[0]


def _adamw(name, w, g, m, v):
    def fn(wv, gv, mv, vv):
        m2 = ADAM_B1 * mv + (1.0 - ADAM_B1) * gv
        v2 = ADAM_B2 * vv + (1.0 - ADAM_B2) * (gv * gv)
        m_hat = m2 / (1.0 - ADAM_B1 ** ADAM_STEP)
        v_hat = v2 / (1.0 - ADAM_B2 ** ADAM_STEP)
        delta = -ADAM_LR * (m_hat / (jnp.sqrt(v_hat) + ADAM_EPS) + ADAM_WD * wv)
        return (delta, m2, v2), ()

    r, c = w.shape
    tr = r
    for cand in (512, 256, 128, 64, 32, 16, 8):
        if r % cand == 0 and r > cand:
            tr = cand
            break
    return _rowwise(name, fn, [w, g, m, v], [], [(c, F32)] * 3, tr=tr)


def _sum8(name, parts):
    _, r, c = parts.shape
    tr = r
    for cand in (2048, 1024, 512, 256, 128, 64, 32, 16):
        if r % cand == 0 and r > cand and cand * c <= 256 * 1024:
            tr = cand
            break

    def body(p_ref, o_ref):
        acc = p_ref[0].astype(F32)
        for k in range(1, N_DEV):
            acc = acc + p_ref[k].astype(F32)
        o_ref[...] = acc

    return pl.pallas_call(
        body, name=name, grid=(r // tr,), in_specs=[pl.BlockSpec((N_DEV, tr, c), lambda i: (0, i, 0))],
        out_specs=pl.BlockSpec((tr, c), lambda i: (i, 0)), out_shape=jax.ShapeDtypeStruct((r, c), F32),
        compiler_params=_params(("parallel",)),
    )(parts)


def _mods_fwd(name, c_all, w, b):
    depth, d, n = w.shape

    def body(c_ref, w_ref, b_ref, o_ref):
        cv = c_ref[...]
        sc = (cv * (1.0 / (1.0 + jnp.exp(-cv)))).astype(BF16)
        o_ref[0] = _dot(sc, w_ref[0].astype(BF16), NN) + b_ref[0]

    return pl.pallas_call(
        body, name=name, grid=(depth,),
        in_specs=[pl.BlockSpec(c_all.shape, lambda l: (0, 0)), pl.BlockSpec((1, d, n), lambda l: (l, 0, 0)),
                  pl.BlockSpec((1, 1, n), lambda l: (l, 0, 0))],
        out_specs=pl.BlockSpec((1, c_all.shape[0], n), lambda l: (l, 0, 0)),
        out_shape=jax.ShapeDtypeStruct((depth, c_all.shape[0], n), F32),
        compiler_params=_params(("parallel",)),
    )(c_all, w, b.reshape(depth, 1, n))


def _mods_bwd(name, c_all, dm):
    depth, rows, n = dm.shape
    d = c_all.shape[1]

    def body(c_ref, dm_ref, o_ref):
        cv = c_ref[...]
        sc = (cv * (1.0 / (1.0 + jnp.exp(-cv)))).astype(BF16)
        o_ref[0] = _dot(sc, dm_ref[0].astype(BF16), TN)

    return pl.pallas_call(
        body, name=name, grid=(depth,),
        in_specs=[pl.BlockSpec(c_all.shape, lambda l: (0, 0)), pl.BlockSpec((1, rows, n), lambda l: (l, 0, 0))],
        out_specs=pl.BlockSpec((1, d, n), lambda l: (l, 0, 0)),
        out_shape=jax.ShapeDtypeStruct((depth, d, n), F32),
        compiler_params=_params(("parallel",)),
    )(c_all, dm)


POOL_TILE = 256


def _split_dot(band, val):
    hi = val.astype(BF16)
    lo = (val - hi.astype(F32)).astype(BF16)
    return _dot(band, hi, NN) + _dot(band, lo, NN)


def _pool_fwd(name, h1, x, pw, pb, ps, g1, norm):
    s, d = h1.shape
    t = POOL_TILE

    def body(hc_ref, hp_ref, x_ref, pw_ref, pb_ref, ps_ref, g_ref, ng_ref, sc_ref, sh_ref, xo_ref, zb_ref, pooled_ref, h2_ref):
        i = pl.program_id(0)
        r = lax.broadcasted_iota(jnp.int32, (t, t), 0)
        j = lax.broadcasted_iota(jnp.int32, (t, t), 1)
        pos = (i * t + lax.broadcasted_iota(jnp.int32, (t, 1), 0) + 1).astype(F32)
        has_prev = (i > 0).astype(F32)
        for grp, w in enumerate(POOL_WINDOWS):
            cs = slice(grp * POOL_GROUP, (grp + 1) * POOL_GROUP)
            hc = hc_ref[:, cs]
            band_cur = ((r - j >= 0) & (r - j < w)).astype(BF16)
            band_prev = (r + t - j < w).astype(BF16)
            ssum = _split_dot(band_cur, hc) + has_prev * _split_dot(band_prev, hp_ref[:, cs])
            pooled = (ssum / jnp.minimum(pos, float(w)) - hc).astype(BF16)
            zb = _dot(pooled, pw_ref[grp], NN) + pb_ref[:, cs]
            xo_ref[:, cs] = x_ref[:, cs] + g_ref[:, cs] * (zb * ps_ref[:, cs])
            zb_ref[:, cs] = zb
            pooled_ref[:, cs] = pooled
        h2_ref[...] = _modulated_rmsnorm(xo_ref[...], ng_ref[...], sc_ref[...], sh_ref[...]).astype(h2_ref.dtype)

    row = pl.BlockSpec((t, d), lambda i: (i, 0))
    vec = pl.BlockSpec((1, d), lambda i: (0, 0))
    return pl.pallas_call(
        body, name=name, grid=(s // t,),
        in_specs=[row, pl.BlockSpec((t, d), lambda i: (jnp.maximum(i - 1, 0), 0)), row,
                  pl.BlockSpec(pw.shape, lambda i: (0, 0, 0)), vec, vec, vec, vec, vec, vec],
        out_specs=(row, row, row, row),
        out_shape=(jax.ShapeDtypeStruct((s, d), F32), jax.ShapeDtypeStruct((s, d), F32), jax.ShapeDtypeStruct((s, d), BF16),
                   jax.ShapeDtypeStruct((s, d), BF16)),
        compiler_params=_params(("parallel",)),
    )(h1, h1, x, pw, pb, ps, g1, *norm)


def _pool_bwd(name, dxn, zb, pooled, pw, ps, g1):
    s, d = dxn.shape
    t = POOL_TILE
    nt = s // t

    def body(dc_ref, dn_ref, zb_ref, pooled_ref, pw_ref, ps_ref, g_ref, dh_ref, dpw_ref, dpb_ref, dps_ref, dg_ref):
        i = pl.program_id(0)

        @pl.when(i == 0)
        def _():
            dpw_ref[...] = jnp.zeros_like(dpw_ref)
            dpb_ref[...] = jnp.zeros_like(dpb_ref)
            dps_ref[...] = jnp.zeros_like(dps_ref)
            dg_ref[...] = jnp.zeros_like(dg_ref)

        jj = lax.broadcasted_iota(jnp.int32, (t, t), 0)
        rr = lax.broadcasted_iota(jnp.int32, (t, t), 1)
        pos = (i * t + lax.broadcasted_iota(jnp.int32, (t, 1), 0) + 1).astype(F32)
        has_next = (i < nt - 1).astype(F32)
        for grp, w in enumerate(POOL_WINDOWS):
            cs = slice(grp * POOL_GROUP, (grp + 1) * POOL_GROUP)
            gv, psv, zbv, dxc = g_ref[:, cs], ps_ref[:, cs], zb_ref[:, cs], dc_ref[:, cs]
            dg_ref[:, cs] += _colsum(dxc * (zbv * psv))
            dy = gv * dxc
            dps_ref[:, cs] += _colsum(dy * zbv)
            dz = dy * psv
            dpb_ref[:, cs] += _colsum(dz)
            dzb = dz.astype(BF16)
            dpw_ref[grp] += _dot(pooled_ref[:, cs], dzb, TN)
            dp = _dot(dzb, pw_ref[grp], NT)
            dzn = (gv * dn_ref[:, cs] * psv).astype(BF16)
            dpn = _dot(dzn, pw_ref[grp], NT) * (has_next / float(w))
            band_cur = ((rr - jj >= 0) & (rr - jj < w)).astype(BF16)
            band_next = (rr + t - jj < w).astype(BF16)
            dh_ref[:, cs] = _split_dot(band_cur, dp / jnp.minimum(pos, float(w))) + _split_dot(band_next, dpn) - dp

    row = pl.BlockSpec((t, d), lambda i: (i, 0))
    vec = pl.BlockSpec((1, d), lambda i: (0, 0))
    wspec = pl.BlockSpec(pw.shape, lambda i: (0, 0, 0))
    return pl.pallas_call(
        body, name=name, grid=(nt,),
        in_specs=[row, pl.BlockSpec((t, d), lambda i: (jnp.minimum(i + 1, nt - 1), 0)), row, row, wspec, vec, vec],
        out_specs=(row, wspec, vec, vec, vec),
        out_shape=(jax.ShapeDtypeStruct((s, d), F32), jax.ShapeDtypeStruct(pw.shape, F32),
                   jax.ShapeDtypeStruct((1, d), F32), jax.ShapeDtypeStruct((1, d), F32), jax.ShapeDtypeStruct((1, d), F32)),
        compiler_params=_params(("arbitrary",)),
    )(dxn, dxn, zb, pooled, pw, ps, g1)


GLU_TILE = 512
HALO = 16
INV_SQRT2 = 0.7071067811865476
INV_SQRT_2PI = 0.3989422804014327


def _up_glu_fwd(name, h2, wt, cw, cb):
    s, d = h2.shape
    f = wt.shape[0] // 2
    tm, tn = _tile(s, GLU_TILE), _tile(f, 1408)

    def body(h_ref, hh_ref, wa_ref, wv_ref, cw_ref, cb_ref, ua_ref, gl_ref, gpv_ref, ge_ref):
        i = pl.program_id(1)
        has_prev = (i > 0).astype(F32)
        a = _dot(h_ref[...], wa_ref[...], NT).astype(BF16)
        v = _dot(h_ref[...], wv_ref[...], NT)
        above = (_dot(hh_ref[...], wa_ref[...], NT) * has_prev).astype(BF16)
        ua_ref[...] = a
        ext = jnp.concatenate([above.astype(F32), a.astype(F32)], axis=0)
        e1 = pltpu.roll(ext, 1, 0)[HALO:]
        e2 = pltpu.roll(ext, 2, 0)[HALO:]
        pre = e2 * cw_ref[0:1, :] + e1 * cw_ref[1:2, :] + ext[HALO:] * cw_ref[2:3, :] + cb_ref[...]
        cdf = 0.5 * (1.0 + lax.erf(pre * INV_SQRT2))
        ge = pre * cdf
        gl_ref[...] = (ge * v).astype(gl_ref.dtype)
        gpv_ref[...] = ((cdf + pre * (INV_SQRT_2PI * jnp.exp(-0.5 * pre * pre))) * v).astype(gpv_ref.dtype)
        ge_ref[...] = ge.astype(ge_ref.dtype)

    blk = pl.BlockSpec((tm, tn), lambda j, i: (i, j))
    return pl.pallas_call(
        body, name=name, grid=(f // tn, s // tm),
        in_specs=[pl.BlockSpec((tm, d), lambda j, i: (i, 0)), pl.BlockSpec((HALO, d), lambda j, i: (jnp.maximum(i * (tm // HALO) - 1, 0), 0)),
                  pl.BlockSpec((tn, d), lambda j, i: (j, 0)), pl.BlockSpec((tn, d), lambda j, i: (j + f // tn, 0)),
                  pl.BlockSpec((3, tn), lambda j, i: (0, j)), pl.BlockSpec((1, tn), lambda j, i: (0, j))],
        out_specs=(blk, blk, blk, blk), out_shape=tuple(jax.ShapeDtypeStruct((s, f), BF16) for _ in range(4)),
        compiler_params=_params(("parallel", "parallel")),
    )(h2, h2, wt, wt, cw, cb)


def _down_glu_bwd(name, dx, gate, wd, ua, gpv, ge, cw):
    s, f = ua.shape
    d = dx.shape[1]
    t, tf = min(GLU_TILE, s), _tile(f, 1408)
    nt = s // t
    te = t + HALO

    def body(dy_ref, dyn_ref, gate_ref, wd_ref, a_ref, ah_ref, g_ref, gn_ref, ge_ref, cw_ref, da_ref, dv_ref, dcw_ref, dcb_ref):
        i = pl.program_id(1)

        @pl.when(i == 0)
        def _():
            dcw_ref[...] = jnp.zeros_like(dcw_ref)
            dcb_ref[...] = jnp.zeros_like(dcb_ref)

        has_prev = (i > 0).astype(F32)
        has_next = (i < nt - 1).astype(F32)
        wdv = wd_ref[...]
        dgl = _dot((dy_ref[...] * gate_ref[...]).astype(BF16), wdv, NT)
        dgl_below = _dot((dyn_ref[...] * gate_ref[...]).astype(BF16), wdv, NT) * has_next
        dpre = jnp.concatenate([dgl * g_ref[...].astype(F32), dgl_below * gn_ref[...].astype(F32)], axis=0)
        c0, c1, c2 = cw_ref[0:1, :], cw_ref[1:2, :], cw_ref[2:3, :]
        up1 = pltpu.roll(dpre, te - 1, 0)
        up2 = pltpu.roll(dpre, te - 2, 0)
        da_ref[...] = (dpre * c2 + up1 * c1 + up2 * c0)[:t].astype(da_ref.dtype)
        dv_ref[...] = (dgl * ge_ref[...].astype(F32)).astype(dv_ref.dtype)
        ext = jnp.concatenate([ah_ref[...].astype(F32) * has_prev, a_ref[...].astype(F32)], axis=0)
        dpt = dpre[:t]
        dcb_ref[...] += _colsum(dpt)
        dcw_ref[0:1, :] += _colsum(pltpu.roll(ext, 2, 0)[HALO:] * dpt)
        dcw_ref[1:2, :] += _colsum(pltpu.roll(ext, 1, 0)[HALO:] * dpt)
        dcw_ref[2:3, :] += _colsum(ext[HALO:] * dpt)

    blk = pl.BlockSpec((t, tf), lambda j, i: (i, j))
    prev = pl.BlockSpec((HALO, tf), lambda j, i: (jnp.maximum(i * (t // HALO) - 1, 0), j))
    below = lambda i: jnp.minimum((i + 1) * (t // HALO), s // HALO - 1)
    w3 = pl.BlockSpec((3, tf), lambda j, i: (0, j))
    w1 = pl.BlockSpec((1, tf), lambda j, i: (0, j))
    return pl.pallas_call(
        body, name=name, grid=(f // tf, nt),
        in_specs=[pl.BlockSpec((t, d), lambda j, i: (i, 0)), pl.BlockSpec((HALO, d), lambda j, i: (below(i), 0)),
                  pl.BlockSpec((1, d), lambda j, i: (0, 0)), pl.BlockSpec((tf, d), lambda j, i: (j, 0)), blk, prev, blk,
                  pl.BlockSpec((HALO, tf), lambda j, i: (below(i), j)), blk, w3],
        out_specs=(blk, blk, w3, w1),
        out_shape=(jax.ShapeDtypeStruct((s, f), BF16), jax.ShapeDtypeStruct((s, f), BF16),
                   jax.ShapeDtypeStruct((3, f), F32), jax.ShapeDtypeStruct((1, f), F32)),
        compiler_params=_params(("parallel", "arbitrary")),
    )(dx, dx, gate, wd, ua, ua, gpv, gpv, ge, cw)


ATT_TILE = 512
ATT_ROWS = 256
ATT_HEADS = 4
ATT_BWD_HEADS = 2
ATT_BWD_VMEM_BYTES = 58 * 1024 * 1024
LOG2E = 1.4426950408889634
LN2 = 0.6931471805599453


def _head_blocks_t(a, width):
    s = a.shape[0]
    t = min(ATT_TILE, s)
    return a.reshape(s // t, t, N_HEADS, width).transpose(2, 0, 3, 1)


def _causal_mask(sv, q0, k0):
    row = q0 + lax.broadcasted_iota(jnp.int32, sv.shape, 0)
    col = k0 + lax.broadcasted_iota(jnp.int32, sv.shape, 1)
    return jnp.where(col <= row, sv, NEG_BIG)


def _attn_fwd(name, q_rot, kt4, v_ext):
    s = q_rot.shape[0]
    t = min(ATT_TILE, s)
    nq = s // t
    rq = min(ATT_ROWS, t)
    nh = ATT_HEADS

    def body(q_ref, kt_ref, v_ref, o_ref, row_ref, acc_ref, m_ref):
        qi = pl.program_id(1)
        acc_ref[...] = jnp.zeros_like(acc_ref)
        m_ref[...] = jnp.full_like(m_ref, NEG_BIG)

        def step(j, masked):
            for hh in range(nh):
                cols = slice(hh * Q_EXT, (hh + 1) * Q_EXT)
                v_blk = v_ref[pl.ds(pl.multiple_of(j * t, t), t), cols]
                for r in range(t // rq):
                    rs = pl.ds(r * rq, rq)
                    sv = _dot(q_ref[rs, cols], kt_ref[hh, j], NN)
                    if masked:
                        sv = _causal_mask(sv, r * rq, 0)
                    m_prev = m_ref[hh, rs, :]
                    m_new = jnp.maximum(m_prev, jnp.max(sv, axis=-1, keepdims=True))
                    p = jnp.exp2(sv - m_new).astype(BF16)
                    acc_ref[hh, rs, :] = jnp.exp2(m_prev - m_new) * acc_ref[hh, rs, :] + _dot(p, v_blk, NN)
                    m_ref[hh, rs, :] = m_new

        def full_step(j, carry):
            step(j, False)
            return carry

        lax.fori_loop(0, qi, full_step, 0)
        step(qi, True)
        for hh in range(nh):
            l = acc_ref[hh, :, V_HEAD:V_HEAD + 1]
            o_ref[:, hh * V_HEAD:(hh + 1) * V_HEAD] = (acc_ref[hh, :, :V_HEAD] / l).astype(o_ref.dtype)
            lse = jnp.broadcast_to(m_ref[hh] + jnp.log(l) * LOG2E, (t, LANES))
            row_ref[hh, 0] = jnp.transpose(lse)[0:8, :]

    return pl.pallas_call(
        body, name=name, grid=(N_HEADS // nh, nq),
        in_specs=[pl.BlockSpec((t, nh * Q_EXT), lambda h, i: (i, h)), pl.BlockSpec((nh, nq, Q_EXT, t), lambda h, i: (h, 0, 0, 0)),
                  pl.BlockSpec((s, nh * Q_EXT), lambda h, i: (0, h))],
        out_specs=(pl.BlockSpec((t, nh * V_HEAD), lambda h, i: (i, h)), pl.BlockSpec((nh, 1, 8, t), lambda h, i: (h, i, 0, 0))),
        out_shape=(jax.ShapeDtypeStruct((s, N_HEADS * V_HEAD), BF16), jax.ShapeDtypeStruct((N_HEADS, nq, 8, t), F32)),
        scratch_shapes=[pltpu.VMEM((nh, t, Q_EXT), F32), pltpu.VMEM((nh, t, 1), F32)],
        compiler_params=_params(("parallel", "parallel")),
    )(q_rot, kt4, v_ext)


def _attn_delta(name, o, do):
    s = o.shape[0]
    t = min(ATT_TILE, s)

    def body(o_ref, do_ref, delta_ref):
        prod = do_ref[...].astype(F32) * o_ref[...].astype(F32)
        for h in range(N_HEADS):
            delta = jnp.sum(prod[:, h * V_HEAD:(h + 1) * V_HEAD], axis=-1, keepdims=True)
            delta_ref[h, 0] = jnp.transpose(jnp.broadcast_to(delta, (t, LANES)))[0:8, :]

    rows = pl.BlockSpec((t, N_HEADS * V_HEAD), lambda i: (i, 0))
    return pl.pallas_call(
        body, name=name, grid=(s // t,), in_specs=[rows, rows],
        out_specs=pl.BlockSpec((N_HEADS, 1, 8, t), lambda i: (0, i, 0, 0)),
        out_shape=jax.ShapeDtypeStruct((N_HEADS, s // t, 8, t), F32),
        compiler_params=_params(("parallel",)),
    )(o, do)


def _attn_bwd(name, kfull, v, qt4, q_rot, dot4, do, lse_row, delta_row, tabq, acc_in=None):
    s = kfull.shape[0]
    t = min(ATT_TILE, s)
    nq = s // t
    nh = ATT_BWD_HEADS
    has_in = acc_in is not None

    def body(*refs):
        k_ref, v_ref, qt_ref, q_ref, dot_ref, do_ref, lse_ref, delta_ref, tab_ref = refs[:9]
        dq_ref, dkn_ref, dkd_ref, dv_ref, dq_acc_ref, acck_ref, accv_ref = refs[-7:]
        kj = pl.program_id(1)

        @pl.when(kj == 0)
        def _():
            dq_acc_ref[...] = jnp.zeros_like(dq_acc_ref)

        acck_ref[...] = jnp.zeros_like(acck_ref)
        accv_ref[...] = jnp.zeros_like(accv_ref)

        def step(i, masked):
            qs = pl.ds(pl.multiple_of(i * t, t), t)
            for hh in range(nh):
                qc = slice(hh * Q_EXT, (hh + 1) * Q_EXT)
                vc = slice(hh * LANES, (hh + 1) * LANES)
                k_blk = k_ref[:, qc]
                st = _dot(k_blk, qt_ref[hh, i], NN)
                if masked:
                    krow = lax.broadcasted_iota(jnp.int32, st.shape, 0)
                    qcol = lax.broadcasted_iota(jnp.int32, st.shape, 1)
                    st = jnp.where(krow <= qcol, st, NEG_BIG)
                pt = jnp.exp2(st - lse_ref[hh, i, 0:1, :])
                accv_ref[hh] += _dot(pt.astype(BF16), do_ref[qs, vc], NN)
                dpt = _dot(v_ref[:, vc], dot_ref[hh, i], NN)
                dst = (pt * (dpt - delta_ref[hh, i, 0:1, :])).astype(BF16)
                acck_ref[hh] += _dot(dst, q_ref[qs, qc], NN)
                dq_acc_ref[hh, qs, :] += _dot(dst, k_blk, TN)

        def full_step(i, carry):
            step(i, False)
            return carry

        step(kj, True)
        lax.fori_loop(kj + 1, nq, full_step, 0)
        for hh in range(nh):
            vc = slice(hh * LANES, (hh + 1) * LANES)
            dk = acck_ref[hh] * LN2
            dkn, dkd, dv = dk[:, :QK_NOPE], dk[:, QK_NOPE:], accv_ref[hh]
            if has_in:
                dkn, dkd, dv = dkn + refs[9][:, vc], dkd + refs[10][:, vc], dv + refs[11][:, vc]
            dkn_ref[:, vc], dkd_ref[:, vc], dv_ref[:, vc] = dkn, dkd, dv

        @pl.when(kj == nq - 1)
        def _():
            for hh in range(nh):
                dq_ref[:, hh * Q_EXT:(hh + 1) * Q_EXT] = (dq_acc_ref[hh] * (tab_ref[...] * LN2)).astype(dq_ref.dtype)

    kblk = pl.BlockSpec((t, nh * LANES), lambda h, j: (j, h))
    col = pl.BlockSpec((s, nh * LANES), lambda h, j: (0, h))
    q_all = pl.BlockSpec((s, nh * Q_EXT), lambda h, j: (0, h))
    stat = pl.BlockSpec((nh, nq, 8, t), lambda h, j: (h, 0, 0, 0))
    ins = [kfull, v, qt4, q_rot, dot4, do, lse_row, delta_row, tabq]
    in_specs = [pl.BlockSpec((t, nh * Q_EXT), lambda h, j: (j, h)), kblk, pl.BlockSpec((nh, nq, Q_EXT, t), lambda h, j: (h, 0, 0, 0)),
                q_all, pl.BlockSpec((nh, nq, V_HEAD, t), lambda h, j: (h, 0, 0, 0)), col, stat, stat,
                pl.BlockSpec((s, Q_EXT), lambda h, j: (0, 0))]
    if has_in:
        ins += list(acc_in)
        in_specs += [kblk, kblk, kblk]
    wide = jax.ShapeDtypeStruct((s, N_HEADS * LANES), F32)
    return pl.pallas_call(
        body, name=name, grid=(N_HEADS // nh, nq), in_specs=in_specs, out_specs=(q_all, kblk, kblk, kblk),
        out_shape=(jax.ShapeDtypeStruct((s, N_HEADS * Q_EXT), BF16), wide, wide, wide),
        scratch_shapes=[pltpu.VMEM((nh, s, Q_EXT), F32), pltpu.VMEM((nh, t, Q_EXT), F32), pltpu.VMEM((nh, t, LANES), F32)],
        compiler_params=pltpu.CompilerParams(dimension_semantics=("parallel", "arbitrary"), vmem_limit_bytes=ATT_BWD_VMEM_BYTES),
    )(*ins)


def _swap_halves(w):
    half = w.shape[-1] // 2
    return jnp.concatenate([-w[..., half:], w[..., :half]], axis=-1)


def _unswap_halves(g):
    half = g.shape[-1] // 2
    return jnp.concatenate([g[..., half:], -g[..., :half]], axis=-1)


def _extend_w_dkv(w):
    return jnp.concatenate([w, _swap_halves(w[:, KV_RANK:])], axis=-1)


def _fold_w_dkv_grad(g):
    rope = g[:, KV_RANK:KV_RANK + QK_ROPE] + _unswap_halves(g[:, KV_RANK + QK_ROPE:])
    return jnp.concatenate([g[:, :KV_RANK], rope], axis=-1)


def _rope_tables(positions):
    inv = 1.0 / (ROPE_THETA ** (jnp.arange(0, QK_ROPE, 2, dtype=F32) / QK_ROPE))
    ang = positions.astype(F32)[:, None] * inv
    cos, sin = jnp.cos(ang), jnp.sin(ang)
    tabk = jnp.concatenate([cos, cos, sin, sin], axis=-1)
    scale = QK_HEAD ** -0.5 * LOG2E
    tabq = jnp.concatenate([jnp.full((positions.shape[0], QK_NOPE), scale, F32), tabk * scale], axis=-1)
    return tabq, tabk


def _forward_backward(x, target, mods, tabq, tabk, norm1_all, final_g, fetch, push):
    row = lambda vec: vec.reshape(1, -1)
    mod = [[row(mods[l, k * D_MODEL:(k + 1) * D_MODEL]) for k in range(N_MOD)] for l in range(DEPTH)]
    saved, weights = [], []
    kv = h1 = None
    for l in range(DEPTH):
        w, tok = fetch(l, x)
        sh1, sc1, g1, sh2, sc2, g2 = mod[l]
        g1 = g1 + tok
        norm2 = (row(w["norm2_g"]), sc2, sh2)
        if l == N_A_LAYERS:
            kvn = _rms_fwd("kvin_fwd", x, row(w["kv_in_g"]))
            kv_ext = _mm("dkv_fwd", kvn, w["w_dkv_ext"], out_dtype=F32)
            ckv = _rms_fwd("ckv_fwd", kv_ext, row(w["ckv_norm_g"]), ncols=KV_RANK)
            kd = _krope_fwd("krope_fwd", kv_ext, tabk)
            kn, v = _mm("uk_fwd", ckv, w["w_uk"]), _mm("uv_fwd", ckv, w["w_uv"])
            heads = lambda a: [a[:, h * LANES:(h + 1) * LANES] for h in range(N_HEADS)]
            kfull = jnp.concatenate([part for kh in heads(kn) for part in (kh, kd)], axis=-1)
            v_ext = jnp.concatenate([part for vh in heads(v) for part in (vh, jnp.ones_like(vh))], axis=-1)
            kv = dict(x=x, kvn=kvn, kv_ext=kv_ext, ckv=ckv, v=v, kfull=kfull, v_ext=v_ext,
                      kt4=_head_blocks_t(kfull, Q_EXT))
        x_in = x
        if l == 0:
            h1 = _rms_fwd("norm1_fwd_0", x, row(norm1_all[0]), sc1, sh1, out_dtype=F32)
        if l < N_A_LAYERS:
            x_mid, zb, pooled, h2 = _pool_fwd(f"pool_fwd_{l}", h1, x, w["pool_w"], row(w["pool_b"]), row(w["pool_scale"]), g1, norm2)
            mix = (zb, pooled)
        else:
            cq_pre = _mm(f"dq_fwd_{l}", h1, w["w_dq"], out_dtype=F32)
            cq = _rms_fwd(f"qnorm_fwd_{l}", cq_pre, row(w["q_norm_g"]))
            q_rot = _mm(f"uq_fwd_{l}", cq, w["w_uq_ext"], rowtab=tabq)
            o, lse_row = _attn_fwd(f"attn_fwd_{l}", q_rot, kv["kt4"], kv["v_ext"])
            x_mid, h2 = _mm(f"wo_fwd_{l}", o, w["w_o"], resid=x, gate=g1, norm=norm2 + (BF16,))
            mix = (h1, cq_pre, cq, q_rot, o, lse_row)
        w_up_t = w["w_up"](h2)
        ua, gl, gpv, ge = _up_glu_fwd(f"up_glu_fwd_{l}", h2, w_up_t, w["conv_w"], row(w["conv_b"]))
        w_down = w["w_down"](gl)
        if l + 1 < DEPTH:
            nxt = (row(norm1_all[l + 1]), mod[l + 1][1], mod[l + 1][0], F32 if l + 1 < N_A_LAYERS else BF16)
            x, h1 = _mm(f"down_fwd_{l}", gl, w_down, resid=x_mid, gate=g2, norm=nxt)
        else:
            x = _mm(f"down_fwd_{l}", gl, w_down, resid=x_mid, gate=g2)
        saved.append((x_in, x_mid, h2, ua, gpv, ge, gl, mix))
        weights.append(dict(w, w_up_t=w_up_t, w_down=w_down))

    dx, dfinal_g, loss = _loss_head("loss_head", x, row(final_g), target)
    g = {"final_g": dfinal_g.reshape(-1)}
    per_layer = {k: [None] * DEPTH for k in ("norm1_g", "norm2_g", "conv_w", "conv_b")}
    per_a = {k: [None] * N_A_LAYERS for k in ("pool_b", "pool_scale")}
    per_b = {k: [None] * N_B_LAYERS for k in ("q_norm_g",)}
    dmods = [None] * DEPTH
    dkv = None
    tok = 0.0
    for l in reversed(range(DEPTH)):
        w, big = weights[l], {}
        sh1, sc1, g1, sh2, sc2, g2 = mod[l]
        g2 = g2 + tok
        x_in, x_mid, h2, ua, gpv, ge, gl, mix = saved[l]
        dw_down, dg2 = _mm(f"down_wgrad_{l}", gl, dx, mode="tn", tm_cap=1408, gate_grad=(w["w_down"], g2))
        tok = push(l, "down", dict(w_down=dw_down), None)
        da, dv_, dcw, dcb = _down_glu_bwd(f"down_glu_bwd_{l}", dx, g2, w["w_down"], ua, gpv, ge, w["conv_w"] + tok)
        dh2 = _mm(f"up_bwd_{l}", da, w["w_up_t"][:D_FF], out_dtype=F32, second=(dv_, w["w_up_t"][D_FF:]))
        tok = push(l, "up", dict(w_up_t_a=_mm(f"up_a_wgrad_{l}", da, h2, mode="tn", tm_cap=1408),
                                 w_up_t_v=_mm(f"up_v_wgrad_{l}", dv_, h2, mode="tn", tm_cap=1408)), None)
        per_layer["conv_w"][l], per_layer["conv_b"][l] = dcw, dcb.reshape(-1)
        dx_mid, dn2, dsh2, dsc2 = _rms_bwd(f"norm2_bwd_{l}", x_mid, row(w["norm2_g"]), dh2, sc2 + tok, dx_in=dx)
        per_layer["norm2_g"][l] = dn2.reshape(-1)
        if l < N_A_LAYERS:
            zb, pooled = mix
            dh1, dpw, dpb, dps, dg1 = _pool_bwd(f"pool_bwd_{l}", dx_mid, zb, pooled, w["pool_w"], row(w["pool_scale"]), g1)
            big["pool_w"] = dpw
            per_a["pool_b"][l], per_a["pool_scale"][l] = dpb.reshape(-1), dps.reshape(-1)
        else:
            j = l - N_A_LAYERS
            h1, cq_pre, cq, q_rot, o, lse_row = mix
            do = _mm(f"wo_bwd_{l}", dx_mid, w["w_o"], mode="nt", a_scale=g1)
            big["w_o"], dg1 = _mm(f"wo_wgrad_{l}", o, dx_mid, mode="tn", gate_grad=(w["w_o"], g1))
            delta_row = _attn_delta(f"attn_delta_{l}", o, do)
            dq_ext, *dkv = _attn_bwd(f"attn_bwd_{l}", kv["kfull"], kv["v"], _head_blocks_t(q_rot, Q_EXT), q_rot, _head_blocks_t(do, V_HEAD), do,
                                     lse_row, delta_row, tabq, acc_in=dkv)
            dcq = _mm(f"uq_bwd_{l}", dq_ext, w["w_uq_ext"], mode="nt", out_dtype=F32)
            big["w_uq_ext"] = _mm(f"uq_wgrad_{l}", cq, dq_ext, mode="tn", out_dtype=F32)
            dcq_pre, dqn = _rms_bwd(f"qnorm_bwd_{l}", cq_pre, row(w["q_norm_g"]), dcq, out_dtype=BF16)
            per_b["q_norm_g"][j] = dqn.reshape(-1)
            dh1 = _mm(f"dq_bwd_{l}", dcq_pre, w["w_dq"], mode="nt")
            big["w_dq"] = _mm(f"dq_wgrad_{l}", h1, dcq_pre, mode="tn")
        dx, dn1, dsh1, dsc1 = _rms_bwd(f"norm1_bwd_{l}", x_in, row(w["norm1_g"]), dh1, sc1, dx_in=dx_mid)
        per_layer["norm1_g"][l] = dn1.reshape(-1)
        dmods[l] = jnp.concatenate([dsh1, dsc1, dg1, dsh2, dsc2, dg2], axis=-1).reshape(-1)
        if l == N_A_LAYERS:
            dkn, dkd, dv = dkv
            dckv = _mm("ukv_bwd", dkn, w["w_uk"], mode="nt", out_dtype=F32, second=(dv, w["w_uv"]))
            big["w_uk"] = _mm("uk_wgrad", kv["ckv"], dkn, mode="tn")
            big["w_uv"] = _mm("uv_wgrad", kv["ckv"], dv, mode="tn")
            dkr = _krope_bwd("krope_bwd", dkd, tabk)
            dc, dckv_g = _rms_bwd("ckv_bwd", kv["kv_ext"], row(w["ckv_norm_g"]), dckv, ncols=KV_RANK, out_dtype=BF16)
            dkv_ext = jnp.concatenate([dc, dkr.astype(BF16)], axis=-1)
            dkvn = _mm("dkv_bwd", dkv_ext, w["w_dkv_ext"], mode="nt")
            big["w_dkv_ext"] = _mm("dkv_wgrad", kv["kvn"], dkv_ext, mode="tn", out_dtype=F32)
            dx, dkv_in_g = _rms_bwd("kvin_bwd", kv["x"], row(w["kv_in_g"]), dkvn, dx_in=dx)
            g["ckv_norm_g"], g["kv_in_g"] = dckv_g.reshape(-1), dkv_in_g.reshape(-1)
        tok = push(l, "mix", big, dx)
    for group in (per_layer, per_a, per_b):
        for k, vals in group.items():
            g[k] = jnp.stack(vals)
    return loss, dx, g, jnp.stack(dmods)


def _my_index():
    return 4 * lax.axis_index("x") + 2 * lax.axis_index("y") + lax.axis_index("c")


def _peer(k):
    x, y, c = lax.axis_index("x"), lax.axis_index("y"), lax.axis_index("c")
    return (1 - x if k & 4 else x, 1 - y if k & 2 else y, 1 - c if k & 1 else c)


def _index_of(pos):
    return 4 * pos[0] + 2 * pos[1] + pos[2]


def _exchange_many(name, arrays, scatter):
    n = len(arrays)
    blocks = [tuple(a.shape[1:]) if scatter else tuple(a.shape) for a in arrays]

    def body(*refs):
        x_refs, o_refs = refs[:n], refs[n:2 * n]
        send_sems, recv_sems, local_sems = refs[2 * n:]
        me = _my_index()
        started = []
        for a in range(n):
            mine = pltpu.make_async_copy(x_refs[a].at[me] if scatter else x_refs[a], o_refs[a].at[me], local_sems.at[a])
            mine.start()
            started.append(mine)
        sends = []
        for k in range(1, N_DEV):
            peer = _peer(k)
            for a in range(n):
                cp = pltpu.make_async_remote_copy(
                    src_ref=x_refs[a].at[_index_of(peer)] if scatter else x_refs[a], dst_ref=o_refs[a].at[me],
                    send_sem=send_sems.at[a, k - 1], recv_sem=recv_sems.at[a, k - 1], device_id=peer, device_id_type=MESH)
                cp.start()
                sends.append(cp)
        for k in range(1, N_DEV):
            peer = _peer(k)
            for a in range(n):
                pltpu.make_async_remote_copy(
                    src_ref=x_refs[a].at[me] if scatter else x_refs[a], dst_ref=o_refs[a].at[_index_of(peer)],
                    send_sem=send_sems.at[a, k - 1], recv_sem=recv_sems.at[a, k - 1], device_id=peer, device_id_type=MESH).wait_recv()
        for cp in sends:
            cp.wait_send()
        for mine in started:
            mine.wait()

    return pl.pallas_call(
        body, name=name, out_shape=tuple(jax.ShapeDtypeStruct((N_DEV,) + blk, a.dtype) for blk, a in zip(blocks, arrays)),
        in_specs=[pl.BlockSpec(memory_space=pl.ANY)] * n, out_specs=tuple([pl.BlockSpec(memory_space=pl.ANY)] * n),
        scratch_shapes=[pltpu.SemaphoreType.DMA((n, N_DEV - 1)), pltpu.SemaphoreType.DMA((n, N_DEV - 1)), pltpu.SemaphoreType.DMA((n,))],
    )(*arrays)


def _exchange(name, x, scatter):
    return _exchange_many(name, [x], scatter)[0]


HBM_SPEC = pl.BlockSpec(memory_space=pltpu.HBM)
SEM_SPEC = pl.BlockSpec(memory_space=pltpu.SEMAPHORE)
DATAFLOW = pltpu.SideEffectType.DATAFLOW_SIDE_EFFECTING


def _remote_copies(x_refs, land_refs, send_sems, recv_sems, scatter, numbers=None):
    me = _my_index()
    numbers = list(range(len(x_refs))) if numbers is None else numbers
    out, inc = [], []
    for a in range(len(x_refs)):
        for k in range(1, N_DEV):
            peer = _peer(k)
            pair = numbers[a] * (N_DEV - 1) + k - 1
            sems = dict(send_sem=send_sems.at[pair], recv_sem=recv_sems.at[pair], device_id=peer, device_id_type=MESH)
            out.append(pltpu.make_async_remote_copy(
                src_ref=x_refs[a].at[_index_of(peer)] if scatter else x_refs[a], dst_ref=land_refs[a].at[me], **sems))
            inc.append(pltpu.make_async_remote_copy(
                src_ref=x_refs[a].at[me] if scatter else x_refs[a], dst_ref=land_refs[a].at[_index_of(peer)], **sems))
    return out, inc


def _exchange_start(name, arrays, scatter):
    n = len(arrays)
    blocks = [tuple(a.shape[1:]) if scatter else tuple(a.shape) for a in arrays]

    def body(*refs):
        x_refs, land_refs = refs[:n], refs[n:2 * n]
        send_sems, recv_sems = refs[2 * n], refs[2 * n + 1]
        for cp in _remote_copies(x_refs, land_refs, send_sems, recv_sems, scatter)[0]:
            cp.start()
        refs[-1][...] = jnp.zeros_like(refs[-1])

    sem_type = pltpu.SemaphoreType.DMA((n * (N_DEV - 1),))
    lands =[pltpu.with_memory_space_constraint(lax.empty((N_DEV,) + blk, a.dtype), pltpu.HBM) for blk, a in zip(blocks, arrays)]
    srcs = [pltpu.with_memory_space_constraint(a, pltpu.HBM) for a in arrays]
    res = pl.pallas_call(
        body, name=name,
        out_shape=(sem_type, sem_type, *[pltpu.HBM(a.shape, a.dtype) for a in srcs + lands], jax.ShapeDtypeStruct((8, LANES), F32)),
        in_specs=[HBM_SPEC] * (2 * n), out_specs=(SEM_SPEC, SEM_SPEC, *[HBM_SPEC] * (2 * n), pl.BlockSpec(memory_space=pltpu.VMEM)),
        input_output_aliases={i: 2 + i for i in range(2 * n)},
        compiler_params=pltpu.CompilerParams(has_side_effects=DATAFLOW),
    )(*srcs, *lands)
    return (res[0], res[1], list(res[2:2 + n]), list(res[2 + n:2 + 2 * n])), res[-1]


def _exchange_wait(name, handles, after, scatter, which=None):
    send_sems, recv_sems, srcs, lands = handles
    which = list(range(len(srcs))) if which is None else list(which)
    srcs, lands = [srcs[a] for a in which], [lands[a] for a in which]
    n = len(srcs)

    def body(*refs):
        x_refs, land_refs = refs[:n], refs[n:2 * n]
        out, inc = _remote_copies(x_refs, land_refs, refs[2 * n], refs[2 * n + 1], scatter, which)
        for cp in out:
            cp.wait_send()
        for cp in inc:
            cp.wait_recv()

    res = pl.pallas_call(
        body, name=name, out_shape=tuple(pltpu.HBM(a.shape, a.dtype) for a in srcs + lands),
        in_specs=[HBM_SPEC] * (2 * n) + [SEM_SPEC, SEM_SPEC, pl.BlockSpec(memory_space=pl.ANY)], out_specs=tuple([HBM_SPEC] * (2 * n)),
        input_output_aliases={i: i for i in range(2 * n)},
        compiler_params=pltpu.CompilerParams(has_side_effects=DATAFLOW),
    )(*srcs, *lands, send_sems, recv_sems, after)
    return list(res[n:])


def _pack(arrays, dtype, row_multiple):
    flat = jnp.concatenate([a.astype(dtype).reshape(-1) for a in arrays])
    rows = -(-flat.shape[0] // (LANES * row_multiple)) * row_multiple
    return jnp.pad(flat, (0, rows * LANES - flat.shape[0])).reshape(rows, LANES)


def _unpack(packed, shapes):
    lead = packed.shape[:-2]
    flat = packed.reshape(lead + (-1,))
    out, off = [], 0
    for shp in shapes:
        size = 1
        for d in shp:
            size *= d
        out.append(flat[..., off:off + size].reshape(lead + tuple(shp)))
        off += size
    return out


def _unshard(g8, axis):
    return jnp.concatenate([g8[j] for j in range(N_DEV)], axis=axis)


def _shard8(full, axis):
    n = full.shape[axis] // N_DEV
    return jnp.stack([lax.slice_in_dim(full, j * n, (j + 1) * n, axis=axis) for j in range(N_DEV)])


VECTOR_WEIGHTS = (("pool_b", 1), ("pool_scale", 1), ("conv_w", 2))
REPLICATED_WEIGHTS = ("norm1_g", "norm2_g", "kv_in_g", "ckv_norm_g", "q_norm_g", "conv_b", "final_g")
WEIGHT_ORDER = ("mod_w", "mod_b", "norm1_g", "norm2_g", "pool_w", "pool_b", "pool_scale", "kv_in_g", "w_dkv", "ckv_norm_g", "w_uk",
                "w_uv", "w_dq", "q_norm_g", "w_uq", "w_o", "w_up", "conv_w", "conv_b", "w_down", "final_g")
SMALL_ROW_MULTIPLE = 16


def _as_2d(a):
    if a.ndim == 1:
        return a.reshape(-1, LANES)
    return a.reshape(-1, a.shape[-1])


def kernel(x, c, positions, mod_w, mod_b, norm1_g, norm2_g, pool_w, pool_b, pool_scale, kv_in_g, w_dkv, ckv_norm_g, w_uk, w_uv, w_dq, q_norm_g, w_uq, w_o, w_up, conv_w, conv_b, w_down, final_g, loss_target, m_mod_w, m_mod_b, m_norm1_g, m_norm2_g, m_pool_w, m_pool_b, m_pool_scale, m_kv_in_g, m_w_dkv, m_ckv_norm_g, m_w_uk, m_w_uv, m_w_dq, m_q_norm_g, m_w_uq, m_w_o, m_w_up, m_conv_w, m_conv_b, m_w_down, m_final_g, v_mod_w, v_mod_b, v_norm1_g, v_norm2_g, v_pool_w, v_pool_b, v_pool_scale, v_kv_in_g, v_w_dkv, v_ckv_norm_g, v_w_uk, v_w_uv, v_w_dq, v_q_norm_g, v_w_uq, v_w_o, v_w_up, v_conv_w, v_conv_b, v_w_down, v_final_g):
    shard = dict(mod_w=mod_w, mod_b=mod_b, norm1_g=norm1_g, norm2_g=norm2_g, pool_w=pool_w, pool_b=pool_b, pool_scale=pool_scale,
                 kv_in_g=kv_in_g, w_dkv=w_dkv, ckv_norm_g=ckv_norm_g, w_uk=w_uk, w_uv=w_uv, w_dq=w_dq, q_norm_g=q_norm_g, w_uq=w_uq,
                 w_o=w_o, w_up=w_up, conv_w=conv_w, conv_b=conv_b, w_down=w_down, final_g=final_g)
    mom_m = dict(mod_w=m_mod_w, mod_b=m_mod_b, norm1_g=m_norm1_g, norm2_g=m_norm2_g, pool_w=m_pool_w, pool_b=m_pool_b,
                 pool_scale=m_pool_scale, kv_in_g=m_kv_in_g, w_dkv=m_w_dkv, ckv_norm_g=m_ckv_norm_g, w_uk=m_w_uk, w_uv=m_w_uv,
                 w_dq=m_w_dq, q_norm_g=m_q_norm_g, w_uq=m_w_uq, w_o=m_w_o, w_up=m_w_up, conv_w=m_conv_w, conv_b=m_conv_b,
                 w_down=m_w_down, final_g=m_final_g)
    mom_v = dict(mod_w=v_mod_w, mod_b=v_mod_b, norm1_g=v_norm1_g, norm2_g=v_norm2_g, pool_w=v_pool_w, pool_b=v_pool_b,
                 pool_scale=v_pool_scale, kv_in_g=v_kv_in_g, w_dkv=v_w_dkv, ckv_norm_g=v_ckv_norm_g, w_uk=v_w_uk, w_uv=v_w_uv,
                 w_dq=v_w_dq, q_norm_g=v_q_norm_g, w_uq=v_w_uq, w_o=v_w_o, w_up=v_w_up, conv_w=v_conv_w, conv_b=v_conv_b,
                 w_down=v_w_down, final_g=v_final_g)
    me = _my_index()
    d6 = N_MOD * D_MODEL
    mod_cols = d6 // N_DEV

    small_in = [c] + [shard[k] for k, _ in VECTOR_WEIGHTS]
    small_all = _exchange("gather_vectors", _pack(small_in, F32, SMALL_ROW_MULTIPLE), scatter=False)
    parts = _unpack(small_all, [a.shape for a in small_in])
    c_all = jnp.pad(parts[0].reshape(N_DEV, D_MODEL), ((0, N_DEV), (0, 0)))
    vec = {k: _unshard(p, ax) for (k, ax), p in zip(VECTOR_WEIGHTS, parts[1:])}

    my_mod_b = lax.dynamic_slice_in_dim(mod_b, me * mod_cols, mod_cols, axis=1)
    mods_mine = _mods_fwd("mods_fwd", c_all, mod_w, my_mod_b)
    mods_all = _exchange("gather_mods", _pack([mods_mine], F32, SMALL_ROW_MULTIPLE), scatter=False)
    mods_all = _unpack(mods_all, [mods_mine.shape])[0]
    mods = lax.dynamic_index_in_dim(mods_all, me, axis=2, keepdims=False)
    mods = jnp.moveaxis(mods, 0, 1).reshape(DEPTH, d6)

    tabq, tabk = _rope_tables(positions[0])
    half = N_DEV // 2
    up_view = lambda a: jnp.swapaxes(a, 1, 2)
    w_up_t = up_view(shard["w_up"])
    up_cols = w_up_t.shape[1]
    cat = lambda a, axis, lo=0, hi=N_DEV: jnp.concatenate([a[j] for j in range(lo, hi)], axis=axis)

    def stage_pieces(l):
        out = {"pool_w": shard["pool_w"].astype(BF16)} if l == 0 else {}
        if l == N_A_LAYERS:
            out.update({k: shard[k].astype(BF16) for k in ("w_dkv", "w_uk", "w_uv")})
        if l >= N_A_LAYERS:
            out.update({k: shard[k][l - N_A_LAYERS].astype(BF16) for k in ("w_dq", "w_uq", "w_o")})
        out.update(w_up=w_up_t[l].astype(BF16), w_down=shard["w_down"][l].astype(BF16))
        return out

    gathers, pool_all = {}, []

    def start_gather(behind):
        pieces = {(k, s): p for s in range(DEPTH) for k, p in stage_pieces(s).items()}
        pieces, _ = lax.optimization_barrier((pieces, behind))
        handles, token = _exchange_start("gather_start", list(pieces.values()), scatter=False)
        gathers["all"] = (handles, pieces)
        return token[0, 0]

    def wait_gather(l, keys, after, tag=""):
        handles, pieces = gathers["all"]
        which = [list(pieces).index((k, l)) for k in keys]
        lands = _exchange_wait(f"gather_wait_{l}{tag}", handles, after, scatter=False, which=which)
        return dict(zip(keys, own_slot(lands, [pieces[(k, l)] for k in keys])))

    def whole_weights(l, got):
        w = dict(norm1_g=norm1_g[l], norm2_g=norm2_g[l], conv_w=vec["conv_w"][l], conv_b=conv_b[l])
        if l == 0:
            pool_all.append(got["pool_w"])
        if l < N_A_LAYERS:
            w.update(pool_w=cat(pool_all[0][:, l], 1), pool_b=vec["pool_b"][l], pool_scale=vec["pool_scale"][l])
        else:
            rope = got["w_uq"][..., QK_NOPE:]
            ext = jnp.concatenate([got["w_uq"][..., :QK_NOPE], rope, _swap_halves(rope)], axis=-1)
            w.update(w_dq=got["w_dq"].reshape(D_MODEL, Q_RANK), w_uq_ext=cat(ext, -1), w_o=got["w_o"].reshape(D_MODEL, D_MODEL),
                     q_norm_g=q_norm_g[l - N_A_LAYERS])
        if l == N_A_LAYERS:
            w.update(w_dkv_ext=_extend_w_dkv(got["w_dkv"].reshape(D_MODEL, KV_RANK + QK_ROPE)), w_uk=cat(got["w_uk"], -1),
                     w_uv=cat(got["w_uv"], -1), kv_in_g=kv_in_g, ckv_norm_g=ckv_norm_g)
        return w

    def own_slot(lands, own):
        return [lax.dynamic_update_index_in_dim(p, o, me, 0) for p, o in zip(lands, own)]

    def fetch(l, after):
        up_parts = lambda g8: g8.reshape(N_DEV * up_cols, D_MODEL)
        tok = start_gather(mods) if l == 0 else 0.0
        first = [k for (k, s) in gathers["all"][1] if s == l and k not in ("w_up", "w_down")]
        got = wait_gather(l, first, mods if l == 0 else after, "_mix") if first else {}
        w_up = lambda aft: up_parts(wait_gather(l, ["w_up"], aft, "_up")["w_up"])
        w_down = lambda aft: wait_gather(l, ["w_down"], aft, "_down")["w_down"].reshape(D_FF, D_MODEL)
        return dict(whole_weights(l, got), w_up=w_up, w_down=w_down), tok

    scatters, pending, pool_grads, piece_grads = {}, {}, {}, {}

    def reduce_pieces(l, keys, got):
        for k, p in zip(keys, got):
            piece_grads[(k, l)] = _sum8(f"sum_grads_{k}_{l}", p.reshape(N_DEV, -1, p.shape[-1])).reshape(p.shape[1:])

    def start_scatter(name, sent):
        sent = {k: a.astype(BF16) for k, a in sent.items()}
        handles, token = _exchange_start(f"scatter_start_{name}", list(sent.values()), scatter=True)
        scatters[name] = (handles, list(sent), [lax.dynamic_index_in_dim(a, me, 0, keepdims=False) for a in sent.values()])
        return token[0, 0]

    def finish_scatter(name, l, after):
        handles, keys, own = scatters.pop(name)
        reduce_pieces(l, keys, own_slot(_exchange_wait(f"scatter_wait_{name}", handles, after, scatter=True), own))

    def push(l, part, big, after):
        cut = lambda a, n, axis: jnp.stack([lax.slice_in_dim(a, j * n, (j + 1) * n, axis=axis) for j in range(N_DEV)])
        sent = {}
        if part == "down":
            sent["w_down"] = big["w_down"].reshape(N_DEV, D_FF // N_DEV, D_MODEL)
        elif part == "up":
            sent["w_up"] = jnp.concatenate([big[part].reshape(half, up_cols, D_MODEL) for part in ("w_up_t_a", "w_up_t_v")])
        elif l < N_A_LAYERS:
            pool_grads[l] = big["pool_w"]
        else:
            ext = cut(big["w_uq_ext"], Q_EXT, 1)
            rope = ext[..., QK_NOPE:QK_HEAD] + _unswap_halves(ext[..., QK_HEAD:])
            sent.update(w_dq=big["w_dq"].reshape(N_DEV, D_MODEL // N_DEV, Q_RANK), w_uq=jnp.concatenate([ext[..., :QK_NOPE], rope], axis=-1),
                        w_o=big["w_o"].reshape(N_DEV, D_MODEL // N_DEV, D_MODEL))
        if part == "mix" and l == N_A_LAYERS:
            sent.update(w_dkv=_fold_w_dkv_grad(big["w_dkv_ext"]).reshape(N_DEV, D_MODEL // N_DEV, KV_RANK + QK_ROPE),
                        w_uk=cut(big["w_uk"], QK_NOPE, 1), w_uv=cut(big["w_uv"], V_HEAD, 1))
        if l == 0 and part != "mix":
            return start_scatter(f"0_{part}", sent)
        if l == 0:
            finish_scatter("1", 1, after)
            pool = _shard8(jnp.stack([pool_grads[a] for a in range(N_A_LAYERS)]), 2).astype(BF16)
            reduce_pieces(0, ["pool_w"], _exchange_many("scatter_pool_grads", [pool], scatter=True))
            return 0.0
        pending.setdefault(l, {}).update(sent)
        if part != "mix":
            return 0.0
        if l + 1 < DEPTH:
            finish_scatter(str(l + 1), l + 1, after)
        return start_scatter(str(l), pending.pop(l))

    loss_row, dx, g, dmods = _forward_backward(x[0], loss_target[0], mods, tabq, tabk, norm1_g, final_g, fetch, push)
    layers_of = lambda k, ls: jnp.stack([piece_grads[(k, l)] for l in ls])
    grads = dict(w_dkv=piece_grads[("w_dkv", N_A_LAYERS)], w_uk=piece_grads[("w_uk", N_A_LAYERS)], w_uv=piece_grads[("w_uv", N_A_LAYERS)])
    for k in ("w_dq", "w_uq", "w_o"):
        grads[k] = layers_of(k, range(N_A_LAYERS, DEPTH))

    small_names = REPLICATED_WEIGHTS + tuple(k for k, _ in VECTOR_WEIGHTS)
    small_out = [dmods] + [g[k] for k in small_names] + [loss_row]
    small_shapes = [a.shape for a in small_out]
    small_got = _exchange("gather_small_grads", _pack(small_out, F32, SMALL_ROW_MULTIPLE), scatter=False)
    summed = _unpack(_sum8("sum_small_grads", small_got), small_shapes)
    grads["mod_b"] = summed[0]
    for k, s in zip(small_names, summed[1:-1]):
        grads[k] = s
    for k, ax in VECTOR_WEIGHTS:
        n = shard[k].shape[ax]
        grads[k] = lax.dynamic_slice_in_dim(grads[k], me * n, n, axis=ax)
    loss = summed[-1][0, 0]
    dmods_all = _unpack(small_got, small_shapes)[0]
    dm_mine = lax.dynamic_slice_in_dim(dmods_all, me * mod_cols, mod_cols, axis=2)
    dm_mine = jnp.pad(jnp.moveaxis(dm_mine, 0, 1), ((0, 0), (0, N_DEV), (0, 0)))
    grads["mod_w"] = _mods_bwd("mods_bwd", c_all, dm_mine)

    delta, new_m, new_v = {}, {}, {}

    def adamw(k):
        if k == "w_up":
            ops = [w_up_t, grads[k], up_view(mom_m[k]), up_view(mom_v[k])]
            res = _adamw(f"adamw_{k}", *[_as_2d(a) for a in ops])
            grads[k], delta[k], new_m[k], new_v[k] = [up_view(r.reshape(w_up_t.shape)) for r in (ops[1],) + tuple(res)]
            return
        shp = shard[k].shape
        grads[k] = grads[k].reshape(shp)
        res = _adamw(f"adamw_{k}", _as_2d(shard[k]), _as_2d(grads[k]), _as_2d(mom_m[k]), _as_2d(mom_v[k]))
        delta[k], new_m[k], new_v[k] = [r.reshape(shp) for r in res]

    late = ("w_up", "w_down", "pool_w")
    for k in WEIGHT_ORDER:
        if k not in late:
            adamw(k)
    finish_scatter("0_down", 0, delta["final_g"])
    finish_scatter("0_up", 0, delta["final_g"])
    grads.update(w_up=layers_of("w_up", range(DEPTH)), w_down=layers_of("w_down", range(DEPTH)), pool_w=piece_grads[("pool_w", 0)])
    for k in late:
        adamw(k)
    return (loss, dx[None], *[grads[k] for k in WEIGHT_ORDER], *[delta[k] for k in WEIGHT_ORDER],
            *[new_m[k] for k in WEIGHT_ORDER], *[new_v[k] for k in WEIGHT_ORDER])
```

```python
import functools

import jax
import jax.numpy as jnp
from jax import lax
from jax.experimental import pallas as pl
from jax.experimental.pallas import tpu as pltpu

F32 = jnp.float32
BF16 = jnp.bfloat16

D_MODEL = 1024
DEPTH = 4
N_A_LAYERS = 2
N_B_LAYERS = 2
POOL_WINDOWS = (2, 4, 8, 16)
POOL_GROUP = 256
N_HEADS = 8
QK_NOPE = 128
QK_ROPE = 64
V_HEAD = 128
QK_HEAD = QK_NOPE + QK_ROPE
Q_RANK = 384
KV_RANK = 256
ROPE_THETA = 10000.0
D_FF = 2816
EPS = 1e-6
N_MOD = 6
ADAM_LR = 0.001
ADAM_B1 = 0.9
ADAM_B2 = 0.999
ADAM_EPS = 1e-08
ADAM_WD = 0.01
ADAM_STEP = 10

N_DEV = 8
LANES = 128
Q_EXT = 256
VMEM_LIMIT_BYTES = 48 * 1024 * 1024
MESH = pl.DeviceIdType.MESH
NEG_BIG = -0.7 * float(jnp.finfo(jnp.float32).max)


def _params(sem):
    return pltpu.CompilerParams(dimension_semantics=sem, vmem_limit_bytes=VMEM_LIMIT_BYTES)


def _tile(n, cap):
    if n <= cap:
        return n
    best = None
    for d in range(LANES, cap + 1, LANES):
        if n % d == 0:
            best = d
    assert best is not None, (n, cap)
    return best


def _dot(a, b, dims):
    return lax.dot_general(a, b, (dims, ((), ())), preferred_element_type=F32)


NN = ((1,), (0,))
NT = ((1,), (1,))
TN = ((0,), (0,))


def _modulated_rmsnorm(xv, gv, scale, shift):
    return xv * lax.rsqrt(jnp.mean(xv * xv, axis=-1, keepdims=True) + EPS) * gv * (1.0 + scale) + shift


def _mm(name, a, b, mode="nn", out_dtype=BF16, resid=None, gate=None, norm=None, rowtab=None, second=None, a_scale=None,
        gate_grad=None, tm_cap=1024, tn_cap=1408, tk_cap=1408):
    if mode == "tn":
        kdim, m = a.shape
    else:
        m, kdim = a.shape
    n = b.shape[0] if mode == "nt" else b.shape[1]
    tm, tn, tk = _tile(m, tm_cap), _tile(n, tn_cap), _tile(kdim, tk_cap)
    nk = kdim // tk
    dims = {"nn": NN, "nt": NT, "tn": TN}[mode]
    a_spec = pl.BlockSpec((tk, tm), lambda i, j, k: (k, i)) if mode == "tn" else pl.BlockSpec((tm, tk), lambda i, j, k: (i, k))
    b_spec = pl.BlockSpec((tn, tk), lambda i, j, k: (j, k)) if mode == "nt" else pl.BlockSpec((tk, tn), lambda i, j, k: (k, j))
    o_spec = pl.BlockSpec((tm, tn), lambda i, j, k: (i, j))
    g_spec = pl.BlockSpec((1, tn), lambda i, j, k: (0, j))
    gated = resid is not None
    assert sum(x is not None for x in (resid, rowtab, gate_grad)) <= 1
    n_ops = 2 if second is None else 4
    n_extra = 1 if a_scale is not None else 0

    def body(*refs):
        acc = refs[-1]
        i, k = pl.program_id(0), pl.program_id(2)

        @pl.when(k == 0)
        def _():
            acc[...] = jnp.zeros_like(acc)

        av = refs[0][...]
        if a_scale is not None:
            av = av.astype(F32) * refs[n_ops][...]
        prod = _dot(av.astype(BF16), refs[1][...].astype(BF16), dims)
        if second is not None:
            prod = prod + _dot(refs[2][...].astype(BF16), refs[3][...].astype(BF16), dims)
        acc[...] += prod
        rest = refs[n_ops + n_extra:-1]

        if gate_grad is not None:
            @pl.when((i == 0) & (k == 0))
            def _():
                rest[3][...] = jnp.zeros_like(rest[3])

        @pl.when(k == nk - 1)
        def _():
            if gated and norm is not None:
                r_ref, g_ref, ng_ref, sc_ref, sh_ref, x_ref, h_ref = rest
                xn = r_ref[...] + g_ref[...] * acc[...]
                x_ref[...] = xn
                h_ref[...] = _modulated_rmsnorm(xn, ng_ref[...], sc_ref[...], sh_ref[...]).astype(h_ref.dtype)
            elif gated:
                r_ref, g_ref, x_ref = rest
                x_ref[...] = r_ref[...] + g_ref[...] * acc[...]
            elif rowtab is not None:
                tab = rest[0][...]
                rest[1][...] = (acc[...] * jnp.concatenate([tab] * (tn // tab.shape[1]), axis=1)).astype(out_dtype)
            elif gate_grad is not None:
                w_ref, g_ref, o_ref, dg_ref = rest
                o_ref[...] = (acc[...] * g_ref[...]).astype(out_dtype)
                dg_ref[...] += _colsum(w_ref[...].astype(F32) * acc[...])
            else:
                rest[0][...] = acc[...].astype(out_dtype)

    ins, in_specs = [a, b], [a_spec, b_spec]
    if second is not None:
        assert second[0].shape == a.shape and second[1].shape == b.shape
        ins += list(second)
        in_specs += [a_spec, b_spec]
    if a_scale is not None:
        assert mode != "tn"
        ins.append(a_scale)
        in_specs.append(pl.BlockSpec((1, tk), lambda i, j, k: (0, k)))
    out_shape, out_specs = jax.ShapeDtypeStruct((m, n), out_dtype), o_spec
    sem = ("parallel", "parallel", "arbitrary")
    if rowtab is not None:
        assert tn % rowtab.shape[1] == 0
        ins.append(rowtab)
        in_specs.append(pl.BlockSpec((tm, rowtab.shape[1]), lambda i, j, k: (i, 0)))
    if gated:
        ins += [resid, gate]
        in_specs += [o_spec, g_spec]
        out_shape = jax.ShapeDtypeStruct((m, n), F32)
    if norm is not None:
        assert gated and tn == n
        ins += list(norm[:3])
        in_specs += [g_spec] * 3
        out_shape = (out_shape, jax.ShapeDtypeStruct((m, n), norm[3]))
        out_specs = (o_spec, o_spec)
    if gate_grad is not None:
        assert mode == "tn" and tn == n
        ins += list(gate_grad)
        in_specs += [o_spec, g_spec]
        out_shape = (out_shape, jax.ShapeDtypeStruct((1, n), F32))
        out_specs = (o_spec, g_spec)
        sem = ("arbitrary", "arbitrary", "arbitrary")
    return pl.pallas_call(
        body, name=name, grid=(m // tm, n // tn, nk), in_specs=in_specs, out_specs=out_specs, out_shape=out_shape,
        scratch_shapes=[pltpu.VMEM((tm, tn), F32)],
        compiler_params=_params(sem),
    )(*ins)


def _rowwise(name, fn, tiled, bcast, outs, sums=(), tr=512):
    tiled = [t if isinstance(t, tuple) else (t, t.shape[1], 0) for t in tiled]
    s = tiled[0][0].shape[0]
    tr = min(tr, s)
    assert s % tr == 0
    n_t, n_b, n_o = len(tiled), len(bcast), len(outs)

    def body(*refs):
        i = pl.program_id(0)
        vals = [r[...] for r in refs[:n_t + n_b]]
        o_vals, s_vals = fn(*vals)
        for r, v in zip(refs[n_t + n_b:n_t + n_b + n_o], o_vals):
            r[...] = v.astype(r.dtype)
        s_refs = refs[n_t + n_b + n_o:]

        @pl.when(i == 0)
        def _():
            for r in s_refs:
                r[...] = jnp.zeros_like(r)

        for r, v in zip(s_refs, s_vals):
            r[...] += v

    in_specs = [pl.BlockSpec((tr, n), functools.partial(lambda cb, i: (i, cb), cb)) for (_, n, cb) in tiled]
    in_specs += [pl.BlockSpec(b.shape, functools.partial(lambda nd, i: (0,) * nd, b.ndim)) for b in bcast]
    out_specs = [pl.BlockSpec((tr, n), lambda i: (i, 0)) for (n, _) in outs]
    out_specs += [pl.BlockSpec((1, n), lambda i: (0, 0)) for n in sums]
    out_shape = [jax.ShapeDtypeStruct((s, n), dt) for (n, dt) in outs]
    out_shape += [jax.ShapeDtypeStruct((1, n), F32) for n in sums]
    res = pl.pallas_call(
        body, name=name, grid=(s // tr,), in_specs=in_specs, out_specs=tuple(out_specs), out_shape=tuple(out_shape),
        compiler_params=_params(("arbitrary",)),
    )(*[t[0] for t in tiled], *bcast)
    return res


def _colsum(v):
    return jnp.sum(v, axis=0, keepdims=True)


def _rms_fwd(name, x, g, scale=None, shift=None, out_dtype=BF16, ncols=None):
    mod = scale is not None

    def fn(xv, gv, *ss):
        if mod:
            return (_modulated_rmsnorm(xv, gv, ss[0], ss[1]),), ()
        return (xv * lax.rsqrt(jnp.mean(xv * xv, axis=-1, keepdims=True) + EPS) * gv,), ()

    n = ncols or x.shape[1]
    return _rowwise(name, fn, [(x, n, 0)], [g] + ([scale, shift] if mod else []), [(n, out_dtype)])[0]


def _rms_bwd(name, x, g, dh, scale=None, dx_in=None, ncols=None, out_dtype=F32):
    mod = scale is not None
    has_in = dx_in is not None

    def fn(*vals):
        xv, dhv = vals[0], vals[1].astype(F32)
        rest = list(vals[2:])
        dxi = rest.pop(0) if has_in else None
        gv = rest.pop(0)
        rstd = lax.rsqrt(jnp.mean(xv * xv, axis=-1, keepdims=True) + EPS)
        xhat = xv * rstd
        sums = []
        if mod:
            sc = rest.pop(0)
            dyn = dhv * (1.0 + sc)
            dshift, dscale = _colsum(dhv), _colsum(dhv * (xhat * gv))
        else:
            dyn = dhv
        dg = _colsum(dyn * xhat)
        dxhat = dyn * gv
        dx = rstd * (dxhat - xhat * jnp.mean(dxhat * xhat, axis=-1, keepdims=True))
        if has_in:
            dx = dx + dxi
        sums = [dg] + ([dshift, dscale] if mod else [])
        return (dx,), sums

    n = ncols or x.shape[1]
    tiled = [(x, n, 0), dh] + ([dx_in] if has_in else [])
    return _rowwise(name, fn, tiled, [g] + ([scale] if mod else []), [(n, out_dtype)], [n] * (3 if mod else 1))


def _loss_head(name, x, g, target):
    n = x.shape[1]

    def fn(xv, tv, gv):
        rstd = lax.rsqrt(jnp.mean(xv * xv, axis=-1, keepdims=True) + EPS)
        xhat = xv * rstd
        err = xhat * gv - tv
        loss = 0.5 * jnp.sum(jnp.sum(err * err, axis=-1, keepdims=True) / n, axis=0, keepdims=True)
        dy = err / n
        dg = _colsum(dy * xhat)
        dxhat = dy * gv
        dx = rstd * (dxhat - xhat * jnp.mean(dxhat * xhat, axis=-1, keepdims=True))
        return (dx,), (dg, jnp.broadcast_to(loss, (1, LANES)))

    return _rowwise(name, fn, [x, target], [g], [(n, F32)], [n, LANES])


def _krope_fwd(name, kv_ext, tabk):
    def fn(xv, tv):
        t = xv * tv
        return (t + pltpu.roll(t, 64, 1),), ()

    return _rowwise(name, fn, [(kv_ext, LANES, 2), tabk], [], [(LANES, BF16)])[0]


def _krope_bwd(name, dkd, tabk):
    def fn(dv, tv):
        d = dv[:, :LANES]
        for h in range(1, N_HEADS):
            d = d + dv[:, h * LANES:(h + 1) * LANES]
        return ((d + pltpu.roll(d, 64, 1)) * tv,), ()

    return _rowwise(name, fn, [dkd, tabk], [], [(LANES, F32)])[0]


def _adamw(name, w, g, m, v):
    def fn(wv, gv, mv, vv):
        m2 = ADAM_B1 * mv + (1.0 - ADAM_B1) * gv
        v2 = ADAM_B2 * vv + (1.0 - ADAM_B2) * (gv * gv)
        m_hat = m2 / (1.0 - ADAM_B1 ** ADAM_STEP)
        v_hat = v2 / (1.0 - ADAM_B2 ** ADAM_STEP)
        delta = -ADAM_LR * (m_hat / (jnp.sqrt(v_hat) + ADAM_EPS) + ADAM_WD * wv)
        return (delta, m2, v2), ()

    r, c = w.shape
    tr = r
    for cand in (512, 256, 128, 64, 32, 16, 8):
        if r % cand == 0 and r > cand:
            tr = cand
            break
    return _rowwise(name, fn, [w, g, m, v], [], [(c, F32)] * 3, tr=tr)


def _sum8(name, parts, into=None, slot=None):
    _, r, c = parts.shape
    tr = r
    for cand in (2048, 1024, 512, 256, 128, 64, 32, 16):
        if r % cand == 0 and r > cand and cand * c <= 256 * 1024:
            tr = cand
            break

    def body(*refs):
        p_ref, o_ref = refs[0], refs[-1]
        acc = p_ref[0].astype(F32)
        for k in range(1, N_DEV):
            acc = acc + p_ref[k].astype(F32)
        o_ref[...] = acc.reshape(o_ref.shape)

    if into is not None:
        return pl.pallas_call(
            body, name=name, grid=(r // tr,),
            in_specs=[pl.BlockSpec((N_DEV, tr, c), lambda i: (0, i, 0)), pl.BlockSpec(memory_space=pl.ANY)],
            out_specs=pl.BlockSpec((1, tr, c), lambda i: (slot, i, 0)), out_shape=jax.ShapeDtypeStruct(into.shape, F32),
            input_output_aliases={1: 0}, compiler_params=_params(("parallel",)),
        )(parts, into)
    return pl.pallas_call(
        body, name=name, grid=(r // tr,), in_specs=[pl.BlockSpec((N_DEV, tr, c), lambda i: (0, i, 0))],
        out_specs=pl.BlockSpec((tr, c), lambda i: (i, 0)), out_shape=jax.ShapeDtypeStruct((r, c), F32),
        compiler_params=_params(("parallel",)),
    )(parts)


def _mods_fwd(name, c_all, w, b):
    depth, d, n = w.shape

    def body(c_ref, w_ref, b_ref, o_ref):
        cv = c_ref[...]
        sc = (cv * (1.0 / (1.0 + jnp.exp(-cv)))).astype(BF16)
        o_ref[0] = _dot(sc, w_ref[0].astype(BF16), NN) + b_ref[0]

    return pl.pallas_call(
        body, name=name, grid=(depth,),
        in_specs=[pl.BlockSpec(c_all.shape, lambda l: (0, 0)), pl.BlockSpec((1, d, n), lambda l: (l, 0, 0)),
                  pl.BlockSpec((1, 1, n), lambda l: (l, 0, 0))],
        out_specs=pl.BlockSpec((1, c_all.shape[0], n), lambda l: (l, 0, 0)),
        out_shape=jax.ShapeDtypeStruct((depth, c_all.shape[0], n), F32),
        compiler_params=_params(("parallel",)),
    )(c_all, w, b.reshape(depth, 1, n))


def _mods_bwd(name, c_all, dm):
    depth, rows, n = dm.shape
    d = c_all.shape[1]

    def body(c_ref, dm_ref, o_ref):
        cv = c_ref[...]
        sc = (cv * (1.0 / (1.0 + jnp.exp(-cv)))).astype(BF16)
        o_ref[0] = _dot(sc, dm_ref[0].astype(BF16), TN)

    return pl.pallas_call(
        body, name=name, grid=(depth,),
        in_specs=[pl.BlockSpec(c_all.shape, lambda l: (0, 0)), pl.BlockSpec((1, rows, n), lambda l: (l, 0, 0))],
        out_specs=pl.BlockSpec((1, d, n), lambda l: (l, 0, 0)),
        out_shape=jax.ShapeDtypeStruct((depth, d, n), F32),
        compiler_params=_params(("parallel",)),
    )(c_all, dm)


POOL_TILE = 256


def _split_dot(band, val):
    hi = val.astype(BF16)
    lo = (val - hi.astype(F32)).astype(BF16)
    return _dot(band, hi, NN) + _dot(band, lo, NN)


def _pool_fwd(name, h1, x, pw, pb, ps, g1, norm):
    s, d = h1.shape
    t = POOL_TILE

    def body(hc_ref, hp_ref, x_ref, pw_ref, pb_ref, ps_ref, g_ref, ng_ref, sc_ref, sh_ref, xo_ref, zb_ref, pooled_ref, h2_ref):
        i = pl.program_id(0)
        r = lax.broadcasted_iota(jnp.int32, (t, t), 0)
        j = lax.broadcasted_iota(jnp.int32, (t, t), 1)
        pos = (i * t + lax.broadcasted_iota(jnp.int32, (t, 1), 0) + 1).astype(F32)
        has_prev = (i > 0).astype(F32)
        for grp, w in enumerate(POOL_WINDOWS):
            cs = slice(grp * POOL_GROUP, (grp + 1) * POOL_GROUP)
            hc = hc_ref[:, cs]
            band_cur = ((r - j >= 0) & (r - j < w)).astype(BF16)
            band_prev = (r + t - j < w).astype(BF16)
            ssum = _split_dot(band_cur, hc) + has_prev * _split_dot(band_prev, hp_ref[:, cs])
            pooled = (ssum / jnp.minimum(pos, float(w)) - hc).astype(BF16)
            zb = _dot(pooled, pw_ref[grp], NN) + pb_ref[:, cs]
            xo_ref[:, cs] = x_ref[:, cs] + g_ref[:, cs] * (zb * ps_ref[:, cs])
            zb_ref[:, cs] = zb
            pooled_ref[:, cs] = pooled
        h2_ref[...] = _modulated_rmsnorm(xo_ref[...], ng_ref[...], sc_ref[...], sh_ref[...]).astype(h2_ref.dtype)

    row = pl.BlockSpec((t, d), lambda i: (i, 0))
    vec = pl.BlockSpec((1, d), lambda i: (0, 0))
    return pl.pallas_call(
        body, name=name, grid=(s // t,),
        in_specs=[row, pl.BlockSpec((t, d), lambda i: (jnp.maximum(i - 1, 0), 0)), row,
                  pl.BlockSpec(pw.shape, lambda i: (0, 0, 0)), vec, vec, vec, vec, vec, vec],
        out_specs=(row, row, row, row),
        out_shape=(jax.ShapeDtypeStruct((s, d), F32), jax.ShapeDtypeStruct((s, d), F32), jax.ShapeDtypeStruct((s, d), BF16),
                   jax.ShapeDtypeStruct((s, d), BF16)),
        compiler_params=_params(("parallel",)),
    )(h1, h1, x, pw, pb, ps, g1, *norm)


def _pool_bwd(name, dxn, zb, pooled, pw, ps, g1):
    s, d = dxn.shape
    t = POOL_TILE
    nt = s // t

    def body(dc_ref, dn_ref, zb_ref, pooled_ref, pw_ref, ps_ref, g_ref, dh_ref, dpw_ref, dpb_ref, dps_ref, dg_ref):
        i = pl.program_id(0)

        @pl.when(i == 0)
        def _():
            dpw_ref[...] = jnp.zeros_like(dpw_ref)
            dpb_ref[...] = jnp.zeros_like(dpb_ref)
            dps_ref[...] = jnp.zeros_like(dps_ref)
            dg_ref[...] = jnp.zeros_like(dg_ref)

        jj = lax.broadcasted_iota(jnp.int32, (t, t), 0)
        rr = lax.broadcasted_iota(jnp.int32, (t, t), 1)
        pos = (i * t + lax.broadcasted_iota(jnp.int32, (t, 1), 0) + 1).astype(F32)
        has_next = (i < nt - 1).astype(F32)
        for grp, w in enumerate(POOL_WINDOWS):
            cs = slice(grp * POOL_GROUP, (grp + 1) * POOL_GROUP)
            gv, psv, zbv, dxc = g_ref[:, cs], ps_ref[:, cs], zb_ref[:, cs], dc_ref[:, cs]
            dg_ref[:, cs] += _colsum(dxc * (zbv * psv))
            dy = gv * dxc
            dps_ref[:, cs] += _colsum(dy * zbv)
            dz = dy * psv
            dpb_ref[:, cs] += _colsum(dz)
            dzb = dz.astype(BF16)
            dpw_ref[grp] += _dot(pooled_ref[:, cs], dzb, TN)
            dp = _dot(dzb, pw_ref[grp], NT)
            dzn = (gv * dn_ref[:, cs] * psv).astype(BF16)
            dpn = _dot(dzn, pw_ref[grp], NT) * (has_next / float(w))
            band_cur = ((rr - jj >= 0) & (rr - jj < w)).astype(BF16)
            band_next = (rr + t - jj < w).astype(BF16)
            dh_ref[:, cs] = _split_dot(band_cur, dp / jnp.minimum(pos, float(w))) + _split_dot(band_next, dpn) - dp

    row = pl.BlockSpec((t, d), lambda i: (i, 0))
    vec = pl.BlockSpec((1, d), lambda i: (0, 0))
    wspec = pl.BlockSpec(pw.shape, lambda i: (0, 0, 0))
    return pl.pallas_call(
        body, name=name, grid=(nt,),
        in_specs=[row, pl.BlockSpec((t, d), lambda i: (jnp.minimum(i + 1, nt - 1), 0)), row, row, wspec, vec, vec],
        out_specs=(row, wspec, vec, vec, vec),
        out_shape=(jax.ShapeDtypeStruct((s, d), F32), jax.ShapeDtypeStruct(pw.shape, F32),
                   jax.ShapeDtypeStruct((1, d), F32), jax.ShapeDtypeStruct((1, d), F32), jax.ShapeDtypeStruct((1, d), F32)),
        compiler_params=_params(("arbitrary",)),
    )(dxn, dxn, zb, pooled, pw, ps, g1)


GLU_TILE = 512
HALO = 16
INV_SQRT2 = 0.7071067811865476
INV_SQRT_2PI = 0.3989422804014327


def _up_glu_fwd(name, h2, wt, cw, cb):
    s, d = h2.shape
    f = wt.shape[0] // 2
    tm, tn = _tile(s, GLU_TILE), _tile(f, 1408)

    def body(h_ref, hh_ref, wa_ref, wv_ref, cw_ref, cb_ref, ua_ref, gl_ref, gpv_ref, ge_ref):
        i = pl.program_id(1)
        has_prev = (i > 0).astype(F32)
        a = _dot(h_ref[...], wa_ref[...], NT).astype(BF16)
        v = _dot(h_ref[...], wv_ref[...], NT)
        above = (_dot(hh_ref[...], wa_ref[...], NT) * has_prev).astype(BF16)
        ua_ref[...] = a
        ext = jnp.concatenate([above.astype(F32), a.astype(F32)], axis=0)
        e1 = pltpu.roll(ext, 1, 0)[HALO:]
        e2 = pltpu.roll(ext, 2, 0)[HALO:]
        pre = e2 * cw_ref[0:1, :] + e1 * cw_ref[1:2, :] + ext[HALO:] * cw_ref[2:3, :] + cb_ref[...]
        cdf = 0.5 * (1.0 + lax.erf(pre * INV_SQRT2))
        ge = pre * cdf
        gl_ref[...] = (ge * v).astype(gl_ref.dtype)
        gpv_ref[...] = ((cdf + pre * (INV_SQRT_2PI * jnp.exp(-0.5 * pre * pre))) * v).astype(gpv_ref.dtype)
        ge_ref[...] = ge.astype(ge_ref.dtype)

    blk = pl.BlockSpec((tm, tn), lambda j, i: (i, j))
    return pl.pallas_call(
        body, name=name, grid=(f // tn, s // tm),
        in_specs=[pl.BlockSpec((tm, d), lambda j, i: (i, 0)), pl.BlockSpec((HALO, d), lambda j, i: (jnp.maximum(i * (tm // HALO) - 1, 0), 0)),
                  pl.BlockSpec((tn, d), lambda j, i: (j, 0)), pl.BlockSpec((tn, d), lambda j, i: (j + f // tn, 0)),
                  pl.BlockSpec((3, tn), lambda j, i: (0, j)), pl.BlockSpec((1, tn), lambda j, i: (0, j))],
        out_specs=(blk, blk, blk, blk), out_shape=tuple(jax.ShapeDtypeStruct((s, f), BF16) for _ in range(4)),
        compiler_params=_params(("parallel", "parallel")),
    )(h2, h2, wt, wt, cw, cb)


def _down_glu_bwd(name, dx, gate, wd, ua, gpv, ge, cw):
    s, f = ua.shape
    d = dx.shape[1]
    t, tf = min(GLU_TILE, s), _tile(f, 1408)
    nt = s // t
    te = t + HALO

    def body(dy_ref, dyn_ref, gate_ref, wd_ref, a_ref, ah_ref, g_ref, gn_ref, ge_ref, cw_ref, da_ref, dv_ref, dcw_ref, dcb_ref):
        i = pl.program_id(1)

        @pl.when(i == 0)
        def _():
            dcw_ref[...] = jnp.zeros_like(dcw_ref)
            dcb_ref[...] = jnp.zeros_like(dcb_ref)

        has_prev = (i > 0).astype(F32)
        has_next = (i < nt - 1).astype(F32)
        wdv = wd_ref[...]
        dgl = _dot((dy_ref[...] * gate_ref[...]).astype(BF16), wdv, NT)
        dgl_below = _dot((dyn_ref[...] * gate_ref[...]).astype(BF16), wdv, NT) * has_next
        dpre = jnp.concatenate([dgl * g_ref[...].astype(F32), dgl_below * gn_ref[...].astype(F32)], axis=0)
        c0, c1, c2 = cw_ref[0:1, :], cw_ref[1:2, :], cw_ref[2:3, :]
        up1 = pltpu.roll(dpre, te - 1, 0)
        up2 = pltpu.roll(dpre, te - 2, 0)
        da_ref[...] = (dpre * c2 + up1 * c1 + up2 * c0)[:t].astype(da_ref.dtype)
        dv_ref[...] = (dgl * ge_ref[...].astype(F32)).astype(dv_ref.dtype)
        ext = jnp.concatenate([ah_ref[...].astype(F32) * has_prev, a_ref[...].astype(F32)], axis=0)
        dpt = dpre[:t]
        dcb_ref[...] += _colsum(dpt)
        dcw_ref[0:1, :] += _colsum(pltpu.roll(ext, 2, 0)[HALO:] * dpt)
        dcw_ref[1:2, :] += _colsum(pltpu.roll(ext, 1, 0)[HALO:] * dpt)
        dcw_ref[2:3, :] += _colsum(ext[HALO:] * dpt)

    blk = pl.BlockSpec((t, tf), lambda j, i: (i, j))
    prev = pl.BlockSpec((HALO, tf), lambda j, i: (jnp.maximum(i * (t // HALO) - 1, 0), j))
    below = lambda i: jnp.minimum((i + 1) * (t // HALO), s // HALO - 1)
    w3 = pl.BlockSpec((3, tf), lambda j, i: (0, j))
    w1 = pl.BlockSpec((1, tf), lambda j, i: (0, j))
    return pl.pallas_call(
        body, name=name, grid=(f // tf, nt),
        in_specs=[pl.BlockSpec((t, d), lambda j, i: (i, 0)), pl.BlockSpec((HALO, d), lambda j, i: (below(i), 0)),
                  pl.BlockSpec((1, d), lambda j, i: (0, 0)), pl.BlockSpec((tf, d), lambda j, i: (j, 0)), blk, prev, blk,
                  pl.BlockSpec((HALO, tf), lambda j, i: (below(i), j)), blk, w3],
        out_specs=(blk, blk, w3, w1),
        out_shape=(jax.ShapeDtypeStruct((s, f), BF16), jax.ShapeDtypeStruct((s, f), BF16),
                   jax.ShapeDtypeStruct((3, f), F32), jax.ShapeDtypeStruct((1, f), F32)),
        compiler_params=_params(("parallel", "arbitrary")),
    )(dx, dx, gate, wd, ua, ua, gpv, gpv, ge, cw)


ATT_TILE = 512
ATT_ROWS = 256
ATT_HEADS = 4
ATT_BWD_HEADS = 2
ATT_BWD_VMEM_BYTES = 58 * 1024 * 1024
LOG2E = 1.4426950408889634
LN2 = 0.6931471805599453


def _head_blocks_t(a, width):
    s = a.shape[0]
    t = min(ATT_TILE, s)
    return a.reshape(s // t, t, N_HEADS, width).transpose(2, 0, 3, 1)


def _causal_mask(sv, q0, k0):
    row = q0 + lax.broadcasted_iota(jnp.int32, sv.shape, 0)
    col = k0 + lax.broadcasted_iota(jnp.int32, sv.shape, 1)
    return jnp.where(col <= row, sv, NEG_BIG)


def _attn_fwd(name, q_rot, kt4, v_ext):
    s = q_rot.shape[0]
    t = min(ATT_TILE, s)
    nq = s // t
    rq = min(ATT_ROWS, t)
    nh = ATT_HEADS

    def body(q_ref, kt_ref, v_ref, o_ref, row_ref, acc_ref, m_ref):
        qi = pl.program_id(1)
        acc_ref[...] = jnp.zeros_like(acc_ref)
        m_ref[...] = jnp.full_like(m_ref, NEG_BIG)

        def step(j, masked):
            for hh in range(nh):
                cols = slice(hh * Q_EXT, (hh + 1) * Q_EXT)
                v_blk = v_ref[pl.ds(pl.multiple_of(j * t, t), t), cols]
                for r in range(t // rq):
                    rs = pl.ds(r * rq, rq)
                    sv = _dot(q_ref[rs, cols], kt_ref[hh, j], NN)
                    if masked:
                        sv = _causal_mask(sv, r * rq, 0)
                    m_prev = m_ref[hh, rs, :]
                    m_new = jnp.maximum(m_prev, jnp.max(sv, axis=-1, keepdims=True))
                    p = jnp.exp2(sv - m_new).astype(BF16)
                    acc_ref[hh, rs, :] = jnp.exp2(m_prev - m_new) * acc_ref[hh, rs, :] + _dot(p, v_blk, NN)
                    m_ref[hh, rs, :] = m_new

        def full_step(j, carry):
            step(j, False)
            return carry

        lax.fori_loop(0, qi, full_step, 0)
        step(qi, True)
        for hh in range(nh):
            l = acc_ref[hh, :, V_HEAD:V_HEAD + 1]
            o_ref[:, hh * V_HEAD:(hh + 1) * V_HEAD] = (acc_ref[hh, :, :V_HEAD] / l).astype(o_ref.dtype)
            lse = jnp.broadcast_to(m_ref[hh] + jnp.log(l) * LOG2E, (t, LANES))
            row_ref[hh, 0] = jnp.transpose(lse)[0:8, :]

    return pl.pallas_call(
        body, name=name, grid=(N_HEADS // nh, nq),
        in_specs=[pl.BlockSpec((t, nh * Q_EXT), lambda h, i: (i, h)), pl.BlockSpec((nh, nq, Q_EXT, t), lambda h, i: (h, 0, 0, 0)),
                  pl.BlockSpec((s, nh * Q_EXT), lambda h, i: (0, h))],
        out_specs=(pl.BlockSpec((t, nh * V_HEAD), lambda h, i: (i, h)), pl.BlockSpec((nh, 1, 8, t), lambda h, i: (h, i, 0, 0))),
        out_shape=(jax.ShapeDtypeStruct((s, N_HEADS * V_HEAD), BF16), jax.ShapeDtypeStruct((N_HEADS, nq, 8, t), F32)),
        scratch_shapes=[pltpu.VMEM((nh, t, Q_EXT), F32), pltpu.VMEM((nh, t, 1), F32)],
        compiler_params=_params(("parallel", "parallel")),
    )(q_rot, kt4, v_ext)


def _attn_delta(name, o, do):
    s = o.shape[0]
    t = min(ATT_TILE, s)

    def body(o_ref, do_ref, delta_ref):
        prod = do_ref[...].astype(F32) * o_ref[...].astype(F32)
        for h in range(N_HEADS):
            delta = jnp.sum(prod[:, h * V_HEAD:(h + 1) * V_HEAD], axis=-1, keepdims=True)
            delta_ref[h, 0] = jnp.transpose(jnp.broadcast_to(delta, (t, LANES)))[0:8, :]

    rows = pl.BlockSpec((t, N_HEADS * V_HEAD), lambda i: (i, 0))
    return pl.pallas_call(
        body, name=name, grid=(s // t,), in_specs=[rows, rows],
        out_specs=pl.BlockSpec((N_HEADS, 1, 8, t), lambda i: (0, i, 0, 0)),
        out_shape=jax.ShapeDtypeStruct((N_HEADS, s // t, 8, t), F32),
        compiler_params=_params(("parallel",)),
    )(o, do)


def _attn_bwd(name, kfull, v, qt4, q_rot, dot4, do, lse_row, delta_row, tabq, acc_in=None):
    s = kfull.shape[0]
    t = min(ATT_TILE, s)
    nq = s // t
    nh = ATT_BWD_HEADS
    has_in = acc_in is not None

    def body(*refs):
        k_ref, v_ref, qt_ref, q_ref, dot_ref, do_ref, lse_ref, delta_ref, tab_ref = refs[:9]
        dq_ref, dkn_ref, dkd_ref, dv_ref, dq_acc_ref, acck_ref, accv_ref = refs[-7:]
        kj = pl.program_id(1)

        @pl.when(kj == 0)
        def _():
            dq_acc_ref[...] = jnp.zeros_like(dq_acc_ref)

        acck_ref[...] = jnp.zeros_like(acck_ref)
        accv_ref[...] = jnp.zeros_like(accv_ref)

        def step(i, masked):
            qs = pl.ds(pl.multiple_of(i * t, t), t)
            for hh in range(nh):
                qc = slice(hh * Q_EXT, (hh + 1) * Q_EXT)
                vc = slice(hh * LANES, (hh + 1) * LANES)
                k_blk = k_ref[:, qc]
                st = _dot(k_blk, qt_ref[hh, i], NN)
                if masked:
                    krow = lax.broadcasted_iota(jnp.int32, st.shape, 0)
                    qcol = lax.broadcasted_iota(jnp.int32, st.shape, 1)
                    st = jnp.where(krow <= qcol, st, NEG_BIG)
                pt = jnp.exp2(st - lse_ref[hh, i, 0:1, :])
                accv_ref[hh] += _dot(pt.astype(BF16), do_ref[qs, vc], NN)
                dpt = _dot(v_ref[:, vc], dot_ref[hh, i], NN)
                dst = (pt * (dpt - delta_ref[hh, i, 0:1, :])).astype(BF16)
                acck_ref[hh] += _dot(dst, q_ref[qs, qc], NN)
                dq_acc_ref[hh, qs, :] += _dot(dst, k_blk, TN)

        def full_step(i, carry):
            step(i, False)
            return carry

        step(kj, True)
        lax.fori_loop(kj + 1, nq, full_step, 0)
        for hh in range(nh):
            vc = slice(hh * LANES, (hh + 1) * LANES)
            dk = acck_ref[hh] * LN2
            dkn, dkd, dv = dk[:, :QK_NOPE], dk[:, QK_NOPE:], accv_ref[hh]
            if has_in:
                dkn, dkd, dv = dkn + refs[9][:, vc], dkd + refs[10][:, vc], dv + refs[11][:, vc]
            dkn_ref[:, vc], dkd_ref[:, vc], dv_ref[:, vc] = dkn, dkd, dv

        @pl.when(kj == nq - 1)
        def _():
            for hh in range(nh):
                dq_ref[:, hh * Q_EXT:(hh + 1) * Q_EXT] = (dq_acc_ref[hh] * (tab_ref[...] * LN2)).astype(dq_ref.dtype)

    kblk = pl.BlockSpec((t, nh * LANES), lambda h, j: (j, h))
    col = pl.BlockSpec((s, nh * LANES), lambda h, j: (0, h))
    q_all = pl.BlockSpec((s, nh * Q_EXT), lambda h, j: (0, h))
    stat = pl.BlockSpec((nh, nq, 8, t), lambda h, j: (h, 0, 0, 0))
    ins = [kfull, v, qt4, q_rot, dot4, do, lse_row, delta_row, tabq]
    in_specs = [pl.BlockSpec((t, nh * Q_EXT), lambda h, j: (j, h)), kblk, pl.BlockSpec((nh, nq, Q_EXT, t), lambda h, j: (h, 0, 0, 0)),
                q_all, pl.BlockSpec((nh, nq, V_HEAD, t), lambda h, j: (h, 0, 0, 0)), col, stat, stat,
                pl.BlockSpec((s, Q_EXT), lambda h, j: (0, 0))]
    if has_in:
        ins += list(acc_in)
        in_specs += [kblk, kblk, kblk]
    wide = jax.ShapeDtypeStruct((s, N_HEADS * LANES), F32)
    return pl.pallas_call(
        body, name=name, grid=(N_HEADS // nh, nq), in_specs=in_specs, out_specs=(q_all, kblk, kblk, kblk),
        out_shape=(jax.ShapeDtypeStruct((s, N_HEADS * Q_EXT), BF16), wide, wide, wide),
        scratch_shapes=[pltpu.VMEM((nh, s, Q_EXT), F32), pltpu.VMEM((nh, t, Q_EXT), F32), pltpu.VMEM((nh, t, LANES), F32)],
        compiler_params=pltpu.CompilerParams(dimension_semantics=("parallel", "arbitrary"), vmem_limit_bytes=ATT_BWD_VMEM_BYTES),
    )(*ins)


def _swap_halves(w):
    half = w.shape[-1] // 2
    return jnp.concatenate([-w[..., half:], w[..., :half]], axis=-1)


def _unswap_halves(g):
    half = g.shape[-1] // 2
    return jnp.concatenate([g[..., half:], -g[..., :half]], axis=-1)


def _extend_w_dkv(w):
    return jnp.concatenate([w, _swap_halves(w[:, KV_RANK:])], axis=-1)


def _fold_w_dkv_grad(g):
    rope = g[:, KV_RANK:KV_RANK + QK_ROPE] + _unswap_halves(g[:, KV_RANK + QK_ROPE:])
    return jnp.concatenate([g[:, :KV_RANK], rope], axis=-1)


def _rope_tables(positions):
    inv = 1.0 / (ROPE_THETA ** (jnp.arange(0, QK_ROPE, 2, dtype=F32) / QK_ROPE))
    ang = positions.astype(F32)[:, None] * inv
    cos, sin = jnp.cos(ang), jnp.sin(ang)
    tabk = jnp.concatenate([cos, cos, sin, sin], axis=-1)
    scale = QK_HEAD ** -0.5 * LOG2E
    tabq = jnp.concatenate([jnp.full((positions.shape[0], QK_NOPE), scale, F32), tabk * scale], axis=-1)
    return tabq, tabk


def _forward_backward(x, target, mods, tabq, tabk, norm1_all, final_g, fetch, push):
    row = lambda vec: vec.reshape(1, -1)
    mod = [[row(mods[l, k * D_MODEL:(k + 1) * D_MODEL]) for k in range(N_MOD)] for l in range(DEPTH)]
    saved, weights = [], []
    kv = h1 = None
    for l in range(DEPTH):
        w, tok = fetch(l, x)
        sh1, sc1, g1, sh2, sc2, g2 = mod[l]
        g1 = g1 + tok
        norm2 = (row(w["norm2_g"]), sc2, sh2)
        if l == N_A_LAYERS:
            kvn = _rms_fwd("kvin_fwd", x, row(w["kv_in_g"]))
            kv_ext = _mm("dkv_fwd", kvn, w["w_dkv_ext"], out_dtype=F32)
            ckv = _rms_fwd("ckv_fwd", kv_ext, row(w["ckv_norm_g"]), ncols=KV_RANK)
            kd = _krope_fwd("krope_fwd", kv_ext, tabk)
            kn, v = _mm("uk_fwd", ckv, w["w_uk"]), _mm("uv_fwd", ckv, w["w_uv"])
            heads = lambda a: [a[:, h * LANES:(h + 1) * LANES] for h in range(N_HEADS)]
            kfull = jnp.concatenate([part for kh in heads(kn) for part in (kh, kd)], axis=-1)
            v_ext = jnp.concatenate([part for vh in heads(v) for part in (vh, jnp.ones_like(vh))], axis=-1)
            kv = dict(x=x, kvn=kvn, kv_ext=kv_ext, ckv=ckv, v=v, kfull=kfull, v_ext=v_ext,
                      kt4=_head_blocks_t(kfull, Q_EXT))
        x_in = x
        if l == 0:
            h1 = _rms_fwd("norm1_fwd_0", x, row(norm1_all[0]), sc1, sh1, out_dtype=F32)
        if l < N_A_LAYERS:
            x_mid, zb, pooled, h2 = _pool_fwd(f"pool_fwd_{l}", h1, x, w["pool_w"], row(w["pool_b"]), row(w["pool_scale"]), g1, norm2)
            mix = (zb, pooled)
        else:
            cq_pre = _mm(f"dq_fwd_{l}", h1, w["w_dq"], out_dtype=F32)
            cq = _rms_fwd(f"qnorm_fwd_{l}", cq_pre, row(w["q_norm_g"]))
            q_rot = _mm(f"uq_fwd_{l}", cq, w["w_uq_ext"], rowtab=tabq)
            o, lse_row = _attn_fwd(f"attn_fwd_{l}", q_rot, kv["kt4"], kv["v_ext"])
            x_mid, h2 = _mm(f"wo_fwd_{l}", o, w["w_o"], resid=x, gate=g1, norm=norm2 + (BF16,))
            mix = (h1, cq_pre, cq, q_rot, o, lse_row)
        w_up_t = w["w_up"](h2)
        ua, gl, gpv, ge = _up_glu_fwd(f"up_glu_fwd_{l}", h2, w_up_t, w["conv_w"], row(w["conv_b"]))
        w_down = w["w_down"](gl)
        if l + 1 < DEPTH:
            nxt = (row(norm1_all[l + 1]), mod[l + 1][1], mod[l + 1][0], F32 if l + 1 < N_A_LAYERS else BF16)
            x, h1 = _mm(f"down_fwd_{l}", gl, w_down, resid=x_mid, gate=g2, norm=nxt)
        else:
            x = _mm(f"down_fwd_{l}", gl, w_down, resid=x_mid, gate=g2)
        saved.append((x_in, x_mid, h2, ua, gpv, ge, gl, mix))
        weights.append(dict(w, w_up_t=w_up_t, w_down=w_down))

    dx, dfinal_g, loss = _loss_head("loss_head", x, row(final_g), target)
    g = {"final_g": dfinal_g.reshape(-1)}
    per_layer = {k: [None] * DEPTH for k in ("norm1_g", "norm2_g", "conv_w", "conv_b")}
    per_a = {k: [None] * N_A_LAYERS for k in ("pool_b", "pool_scale")}
    per_b = {k: [None] * N_B_LAYERS for k in ("q_norm_g",)}
    dmods = [None] * DEPTH
    dkv = None
    tok = 0.0
    for l in reversed(range(DEPTH)):
        w, big = weights[l], {}
        sh1, sc1, g1, sh2, sc2, g2 = mod[l]
        g2 = g2 + tok
        x_in, x_mid, h2, ua, gpv, ge, gl, mix = saved[l]
        dw_down, dg2 = _mm(f"down_wgrad_{l}", gl, dx, mode="tn", tm_cap=1408, gate_grad=(w["w_down"], g2))
        tok = push(l, "down", dict(w_down=dw_down), None)
        da, dv_, dcw, dcb = _down_glu_bwd(f"down_glu_bwd_{l}", dx, g2, w["w_down"], ua, gpv, ge, w["conv_w"] + tok)
        dh2 = _mm(f"up_bwd_{l}", da, w["w_up_t"][:D_FF], out_dtype=F32, second=(dv_, w["w_up_t"][D_FF:]))
        tok = push(l, "up", dict(w_up_t_a=_mm(f"up_a_wgrad_{l}", da, h2, mode="tn", tm_cap=1408),
                                 w_up_t_v=_mm(f"up_v_wgrad_{l}", dv_, h2, mode="tn", tm_cap=1408)), None)
        per_layer["conv_w"][l], per_layer["conv_b"][l] = dcw, dcb.reshape(-1)
        dx_mid, dn2, dsh2, dsc2 = _rms_bwd(f"norm2_bwd_{l}", x_mid, row(w["norm2_g"]), dh2, sc2 + tok, dx_in=dx)
        per_layer["norm2_g"][l] = dn2.reshape(-1)
        if l < N_A_LAYERS:
            zb, pooled = mix
            dh1, dpw, dpb, dps, dg1 = _pool_bwd(f"pool_bwd_{l}", dx_mid, zb, pooled, w["pool_w"], row(w["pool_scale"]), g1)
            big["pool_w"] = dpw
            per_a["pool_b"][l], per_a["pool_scale"][l] = dpb.reshape(-1), dps.reshape(-1)
        else:
            j = l - N_A_LAYERS
            h1, cq_pre, cq, q_rot, o, lse_row = mix
            do = _mm(f"wo_bwd_{l}", dx_mid, w["w_o"], mode="nt", a_scale=g1)
            big["w_o"], dg1 = _mm(f"wo_wgrad_{l}", o, dx_mid, mode="tn", gate_grad=(w["w_o"], g1))
            delta_row = _attn_delta(f"attn_delta_{l}", o, do)
            dq_ext, *dkv = _attn_bwd(f"attn_bwd_{l}", kv["kfull"], kv["v"], _head_blocks_t(q_rot, Q_EXT), q_rot, _head_blocks_t(do, V_HEAD), do,
                                     lse_row, delta_row, tabq, acc_in=dkv)
            dcq = _mm(f"uq_bwd_{l}", dq_ext, w["w_uq_ext"], mode="nt", out_dtype=F32)
            big["w_uq_ext"] = _mm(f"uq_wgrad_{l}", cq, dq_ext, mode="tn", out_dtype=F32)
            dcq_pre, dqn = _rms_bwd(f"qnorm_bwd_{l}", cq_pre, row(w["q_norm_g"]), dcq, out_dtype=BF16)
            per_b["q_norm_g"][j] = dqn.reshape(-1)
            dh1 = _mm(f"dq_bwd_{l}", dcq_pre, w["w_dq"], mode="nt")
            big["w_dq"] = _mm(f"dq_wgrad_{l}", h1, dcq_pre, mode="tn")
        dx, dn1, dsh1, dsc1 = _rms_bwd(f"norm1_bwd_{l}", x_in, row(w["norm1_g"]), dh1, sc1, dx_in=dx_mid)
        per_layer["norm1_g"][l] = dn1.reshape(-1)
        dmods[l] = jnp.concatenate([dsh1, dsc1, dg1, dsh2, dsc2, dg2], axis=-1).reshape(-1)
        if l == N_A_LAYERS:
            dkn, dkd, dv = dkv
            dckv = _mm("ukv_bwd", dkn, w["w_uk"], mode="nt", out_dtype=F32, second=(dv, w["w_uv"]))
            big["w_uk"] = _mm("uk_wgrad", kv["ckv"], dkn, mode="tn")
            big["w_uv"] = _mm("uv_wgrad", kv["ckv"], dv, mode="tn")
            dkr = _krope_bwd("krope_bwd", dkd, tabk)
            dc, dckv_g = _rms_bwd("ckv_bwd", kv["kv_ext"], row(w["ckv_norm_g"]), dckv, ncols=KV_RANK, out_dtype=BF16)
            dkv_ext = jnp.concatenate([dc, dkr.astype(BF16)], axis=-1)
            dkvn = _mm("dkv_bwd", dkv_ext, w["w_dkv_ext"], mode="nt")
            big["w_dkv_ext"] = _mm("dkv_wgrad", kv["kvn"], dkv_ext, mode="tn", out_dtype=F32)
            dx, dkv_in_g = _rms_bwd("kvin_bwd", kv["x"], row(w["kv_in_g"]), dkvn, dx_in=dx)
            g["ckv_norm_g"], g["kv_in_g"] = dckv_g.reshape(-1), dkv_in_g.reshape(-1)
        tok = push(l, "mix", big, dx)
    for group in (per_layer, per_a, per_b):
        for k, vals in group.items():
            g[k] = jnp.stack(vals)
    return loss, dx, g, jnp.stack(dmods)


def _my_index():
    return 4 * lax.axis_index("x") + 2 * lax.axis_index("y") + lax.axis_index("c")


def _peer(k):
    x, y, c = lax.axis_index("x"), lax.axis_index("y"), lax.axis_index("c")
    return (1 - x if k & 4 else x, 1 - y if k & 2 else y, 1 - c if k & 1 else c)


def _index_of(pos):
    return 4 * pos[0] + 2 * pos[1] + pos[2]


def _exchange_many(name, arrays, scatter):
    n = len(arrays)
    blocks = [tuple(a.shape[1:]) if scatter else tuple(a.shape) for a in arrays]

    def body(*refs):
        x_refs, o_refs = refs[:n], refs[n:2 * n]
        send_sems, recv_sems, local_sems = refs[2 * n:]
        me = _my_index()
        started = []
        for a in range(n):
            mine = pltpu.make_async_copy(x_refs[a].at[me] if scatter else x_refs[a], o_refs[a].at[me], local_sems.at[a])
            mine.start()
            started.append(mine)
        sends = []
        for k in range(1, N_DEV):
            peer = _peer(k)
            for a in range(n):
                cp = pltpu.make_async_remote_copy(
                    src_ref=x_refs[a].at[_index_of(peer)] if scatter else x_refs[a], dst_ref=o_refs[a].at[me],
                    send_sem=send_sems.at[a, k - 1], recv_sem=recv_sems.at[a, k - 1], device_id=peer, device_id_type=MESH)
                cp.start()
                sends.append(cp)
        for k in range(1, N_DEV):
            peer = _peer(k)
            for a in range(n):
                pltpu.make_async_remote_copy(
                    src_ref=x_refs[a].at[me] if scatter else x_refs[a], dst_ref=o_refs[a].at[_index_of(peer)],
                    send_sem=send_sems.at[a, k - 1], recv_sem=recv_sems.at[a, k - 1], device_id=peer, device_id_type=MESH).wait_recv()
        for cp in sends:
            cp.wait_send()
        for mine in started:
            mine.wait()

    return pl.pallas_call(
        body, name=name, out_shape=tuple(jax.ShapeDtypeStruct((N_DEV,) + blk, a.dtype) for blk, a in zip(blocks, arrays)),
        in_specs=[pl.BlockSpec(memory_space=pl.ANY)] * n, out_specs=tuple([pl.BlockSpec(memory_space=pl.ANY)] * n),
        scratch_shapes=[pltpu.SemaphoreType.DMA((n, N_DEV - 1)), pltpu.SemaphoreType.DMA((n, N_DEV - 1)), pltpu.SemaphoreType.DMA((n,))],
    )(*arrays)


def _exchange(name, x, scatter):
    return _exchange_many(name, [x], scatter)[0]


HBM_SPEC = pl.BlockSpec(memory_space=pltpu.HBM)
SEM_SPEC = pl.BlockSpec(memory_space=pltpu.SEMAPHORE)
DATAFLOW = pltpu.SideEffectType.DATAFLOW_SIDE_EFFECTING


def _remote_copies(x_refs, land_refs, send_sems, recv_sems, scatter, numbers=None):
    me = _my_index()
    numbers = list(range(len(x_refs))) if numbers is None else numbers
    out, inc = [], []
    for a in range(len(x_refs)):
        for k in range(1, N_DEV):
            peer = _peer(k)
            pair = numbers[a] * (N_DEV - 1) + k - 1
            sems = dict(send_sem=send_sems.at[pair], recv_sem=recv_sems.at[pair], device_id=peer, device_id_type=MESH)
            out.append(pltpu.make_async_remote_copy(
                src_ref=x_refs[a].at[_index_of(peer)] if scatter else x_refs[a], dst_ref=land_refs[a].at[me], **sems))
            inc.append(pltpu.make_async_remote_copy(
                src_ref=x_refs[a].at[me] if scatter else x_refs[a], dst_ref=land_refs[a].at[_index_of(peer)], **sems))
    return out, inc


def _exchange_start(name, arrays, scatter):
    n = len(arrays)
    blocks = [tuple(a.shape[1:]) if scatter else tuple(a.shape) for a in arrays]

    def body(*refs):
        x_refs, land_refs = refs[:n], refs[n:2 * n]
        send_sems, recv_sems = refs[2 * n], refs[2 * n + 1]
        for cp in _remote_copies(x_refs, land_refs, send_sems, recv_sems, scatter)[0]:
            cp.start()
        refs[-1][...] = jnp.zeros_like(refs[-1])

    sem_type = pltpu.SemaphoreType.DMA((n * (N_DEV - 1),))
    lands =[pltpu.with_memory_space_constraint(lax.empty((N_DEV,) + blk, a.dtype), pltpu.HBM) for blk, a in zip(blocks, arrays)]
    srcs = [pltpu.with_memory_space_constraint(a, pltpu.HBM) for a in arrays]
    res = pl.pallas_call(
        body, name=name,
        out_shape=(sem_type, sem_type, *[pltpu.HBM(a.shape, a.dtype) for a in srcs + lands], jax.ShapeDtypeStruct((8, LANES), F32)),
        in_specs=[HBM_SPEC] * (2 * n), out_specs=(SEM_SPEC, SEM_SPEC, *[HBM_SPEC] * (2 * n), pl.BlockSpec(memory_space=pltpu.VMEM)),
        input_output_aliases={i: 2 + i for i in range(2 * n)},
        compiler_params=pltpu.CompilerParams(has_side_effects=DATAFLOW),
    )(*srcs, *lands)
    return (res[0], res[1], list(res[2:2 + n]), list(res[2 + n:2 + 2 * n])), res[-1]


def _exchange_wait(name, handles, after, scatter, which=None):
    send_sems, recv_sems, srcs, lands = handles
    which = list(range(len(srcs))) if which is None else list(which)
    srcs, lands = [srcs[a] for a in which], [lands[a] for a in which]
    n = len(srcs)

    def body(*refs):
        x_refs, land_refs = refs[:n], refs[n:2 * n]
        out, inc = _remote_copies(x_refs, land_refs, refs[2 * n], refs[2 * n + 1], scatter, which)
        for cp in out:
            cp.wait_send()
        for cp in inc:
            cp.wait_recv()

    res = pl.pallas_call(
        body, name=name, out_shape=tuple(pltpu.HBM(a.shape, a.dtype) for a in srcs + lands),
        in_specs=[HBM_SPEC] * (2 * n) + [SEM_SPEC, SEM_SPEC, pl.BlockSpec(memory_space=pl.ANY)], out_specs=tuple([HBM_SPEC] * (2 * n)),
        input_output_aliases={i: i for i in range(2 * n)},
        compiler_params=pltpu.CompilerParams(has_side_effects=DATAFLOW),
    )(*srcs, *lands, send_sems, recv_sems, after)
    return list(res[n:])


def _pack(arrays, dtype, row_multiple):
    flat = jnp.concatenate([a.astype(dtype).reshape(-1) for a in arrays])
    rows = -(-flat.shape[0] // (LANES * row_multiple)) * row_multiple
    return jnp.pad(flat, (0, rows * LANES - flat.shape[0])).reshape(rows, LANES)


def _unpack(packed, shapes):
    lead = packed.shape[:-2]
    flat = packed.reshape(lead + (-1,))
    out, off = [], 0
    for shp in shapes:
        size = 1
        for d in shp:
            size *= d
        out.append(flat[..., off:off + size].reshape(lead + tuple(shp)))
        off += size
    return out


def _unshard(g8, axis):
    return jnp.concatenate([g8[j] for j in range(N_DEV)], axis=axis)


def _shard8(full, axis):
    n = full.shape[axis] // N_DEV
    return jnp.stack([lax.slice_in_dim(full, j * n, (j + 1) * n, axis=axis) for j in range(N_DEV)])


VECTOR_WEIGHTS = (("pool_b", 1), ("pool_scale", 1), ("conv_w", 2))
REPLICATED_WEIGHTS = ("norm1_g", "norm2_g", "kv_in_g", "ckv_norm_g", "q_norm_g", "conv_b", "final_g")
WEIGHT_ORDER = ("mod_w", "mod_b", "norm1_g", "norm2_g", "pool_w", "pool_b", "pool_scale", "kv_in_g", "w_dkv", "ckv_norm_g", "w_uk",
                "w_uv", "w_dq", "q_norm_g", "w_uq", "w_o", "w_up", "conv_w", "conv_b", "w_down", "final_g")
SMALL_ROW_MULTIPLE = 16


def _as_2d(a):
    if a.ndim == 1:
        return a.reshape(-1, LANES)
    return a.reshape(-1, a.shape[-1])


def kernel(x, c, positions, mod_w, mod_b, norm1_g, norm2_g, pool_w, pool_b, pool_scale, kv_in_g, w_dkv, ckv_norm_g, w_uk, w_uv, w_dq, q_norm_g, w_uq, w_o, w_up, conv_w, conv_b, w_down, final_g, loss_target, m_mod_w, m_mod_b, m_norm1_g, m_norm2_g, m_pool_w, m_pool_b, m_pool_scale, m_kv_in_g, m_w_dkv, m_ckv_norm_g, m_w_uk, m_w_uv, m_w_dq, m_q_norm_g, m_w_uq, m_w_o, m_w_up, m_conv_w, m_conv_b, m_w_down, m_final_g, v_mod_w, v_mod_b, v_norm1_g, v_norm2_g, v_pool_w, v_pool_b, v_pool_scale, v_kv_in_g, v_w_dkv, v_ckv_norm_g, v_w_uk, v_w_uv, v_w_dq, v_q_norm_g, v_w_uq, v_w_o, v_w_up, v_conv_w, v_conv_b, v_w_down, v_final_g):
    shard = dict(mod_w=mod_w, mod_b=mod_b, norm1_g=norm1_g, norm2_g=norm2_g, pool_w=pool_w, pool_b=pool_b, pool_scale=pool_scale,
                 kv_in_g=kv_in_g, w_dkv=w_dkv, ckv_norm_g=ckv_norm_g, w_uk=w_uk, w_uv=w_uv, w_dq=w_dq, q_norm_g=q_norm_g, w_uq=w_uq,
                 w_o=w_o, w_up=w_up, conv_w=conv_w, conv_b=conv_b, w_down=w_down, final_g=final_g)
    mom_m = dict(mod_w=m_mod_w, mod_b=m_mod_b, norm1_g=m_norm1_g, norm2_g=m_norm2_g, pool_w=m_pool_w, pool_b=m_pool_b,
                 pool_scale=m_pool_scale, kv_in_g=m_kv_in_g, w_dkv=m_w_dkv, ckv_norm_g=m_ckv_norm_g, w_uk=m_w_uk, w_uv=m_w_uv,
                 w_dq=m_w_dq, q_norm_g=m_q_norm_g, w_uq=m_w_uq, w_o=m_w_o, w_up=m_w_up, conv_w=m_conv_w, conv_b=m_conv_b,
                 w_down=m_w_down, final_g=m_final_g)
    mom_v = dict(mod_w=v_mod_w, mod_b=v_mod_b, norm1_g=v_norm1_g, norm2_g=v_norm2_g, pool_w=v_pool_w, pool_b=v_pool_b,
                 pool_scale=v_pool_scale, kv_in_g=v_kv_in_g, w_dkv=v_w_dkv, ckv_norm_g=v_ckv_norm_g, w_uk=v_w_uk, w_uv=v_w_uv,
                 w_dq=v_w_dq, q_norm_g=v_q_norm_g, w_uq=v_w_uq, w_o=v_w_o, w_up=v_w_up, conv_w=v_conv_w, conv_b=v_conv_b,
                 w_down=v_w_down, final_g=v_final_g)
    me = _my_index()
    d6 = N_MOD * D_MODEL
    mod_cols = d6 // N_DEV

    small_in = [c] + [shard[k] for k, _ in VECTOR_WEIGHTS]
    small_all = _exchange("gather_vectors", _pack(small_in, F32, SMALL_ROW_MULTIPLE), scatter=False)
    parts = _unpack(small_all, [a.shape for a in small_in])
    c_all = jnp.pad(parts[0].reshape(N_DEV, D_MODEL), ((0, N_DEV), (0, 0)))
    vec = {k: _unshard(p, ax) for (k, ax), p in zip(VECTOR_WEIGHTS, parts[1:])}

    my_mod_b = lax.dynamic_slice_in_dim(mod_b, me * mod_cols, mod_cols, axis=1)
    mods_mine = _mods_fwd("mods_fwd", c_all, mod_w, my_mod_b)
    mods_all = _exchange("gather_mods", _pack([mods_mine], F32, SMALL_ROW_MULTIPLE), scatter=False)
    mods_all = _unpack(mods_all, [mods_mine.shape])[0]
    mods = lax.dynamic_index_in_dim(mods_all, me, axis=2, keepdims=False)
    mods = jnp.moveaxis(mods, 0, 1).reshape(DEPTH, d6)

    tabq, tabk = _rope_tables(positions[0])
    half = N_DEV // 2
    up_view = lambda a: jnp.swapaxes(a, 1, 2)
    w_up_t = up_view(shard["w_up"])
    up_cols = w_up_t.shape[1]
    cat = lambda a, axis, lo=0, hi=N_DEV: jnp.concatenate([a[j] for j in range(lo, hi)], axis=axis)

    def stage_pieces(l):
        out = {"pool_w": shard["pool_w"].astype(BF16)} if l == 0 else {}
        if l == N_A_LAYERS:
            out.update({k: shard[k].astype(BF16) for k in ("w_dkv", "w_uk", "w_uv")})
        if l >= N_A_LAYERS:
            out.update({k: shard[k][l - N_A_LAYERS].astype(BF16) for k in ("w_dq", "w_uq", "w_o")})
        out.update(w_up=w_up_t[l].astype(BF16), w_down=shard["w_down"][l].astype(BF16))
        return out

    gathers, pool_all = {}, []

    def start_gather(l, behind=None):
        pieces = stage_pieces(l)
        if behind is not None:
            pieces, _ = lax.optimization_barrier((pieces, behind))
        handles, token = _exchange_start(f"gather_start_{l}", list(pieces.values()), scatter=False)
        gathers[l] = (handles, pieces)
        return token[0, 0]

    def wait_gather(l, keys, after, tag=""):
        handles, pieces = gathers[l]
        which = [list(pieces).index(k) for k in keys]
        lands = _exchange_wait(f"gather_wait_{l}{tag}", handles, after, scatter=False, which=which)
        return dict(zip(keys, own_slot(lands, [pieces[k] for k in keys])))

    def whole_weights(l, got):
        w = dict(norm1_g=norm1_g[l], norm2_g=norm2_g[l], conv_w=vec["conv_w"][l], conv_b=conv_b[l])
        if l == 0:
            pool_all.append(got["pool_w"])
        if l < N_A_LAYERS:
            w.update(pool_w=cat(pool_all[0][:, l], 1), pool_b=vec["pool_b"][l], pool_scale=vec["pool_scale"][l])
        else:
            rope = got["w_uq"][..., QK_NOPE:]
            ext = jnp.concatenate([got["w_uq"][..., :QK_NOPE], rope, _swap_halves(rope)], axis=-1)
            w.update(w_dq=got["w_dq"].reshape(D_MODEL, Q_RANK), w_uq_ext=cat(ext, -1), w_o=got["w_o"].reshape(D_MODEL, D_MODEL),
                     q_norm_g=q_norm_g[l - N_A_LAYERS])
        if l == N_A_LAYERS:
            w.update(w_dkv_ext=_extend_w_dkv(got["w_dkv"].reshape(D_MODEL, KV_RANK + QK_ROPE)), w_uk=cat(got["w_uk"], -1),
                     w_uv=cat(got["w_uv"], -1), kv_in_g=kv_in_g, ckv_norm_g=ckv_norm_g)
        return w

    def own_slot(lands, own):
        return [lax.dynamic_update_index_in_dim(p, o, me, 0) for p, o in zip(lands, own)]

    def fetch(l, after):
        up_parts = lambda g8: g8.reshape(N_DEV * up_cols, D_MODEL)
        if l == 0:
            start_gather(0, behind=mods)
            got = wait_gather(0, ["pool_w"], mods, "_pool")
            w_up = lambda aft: up_parts(wait_gather(0, ["w_up"], aft, "_up")["w_up"])
            w_down = lambda aft: wait_gather(0, ["w_down"], aft, "_down")["w_down"].reshape(D_FF, D_MODEL)
        else:
            got = wait_gather(l, list(gathers[l][1]), after)
            up, down = up_parts(got["w_up"]), got["w_down"].reshape(D_FF, D_MODEL)
            w_up, w_down = (lambda aft: up), (lambda aft: down)
        w = dict(whole_weights(l, got), w_up=w_up, w_down=w_down)
        return w, (start_gather(l + 1) if l + 1 < DEPTH else 0.0)

    scatters, pending, pool_grads, piece_grads = {}, {}, {}, {}

    stacked = {}

    def reduce_pieces(l, keys, got):
        for k, p in zip(keys, got):
            if k in ("w_up", "w_down"):
                buf = stacked.get(k, jnp.zeros((DEPTH,) + p.shape[1:], F32))
                stacked[k] = _sum8(f"sum_grads_{k}_{l}", p, into=buf, slot=l)
            else:
                piece_grads[(k, l)] = _sum8(f"sum_grads_{k}_{l}", p.reshape(N_DEV, -1, p.shape[-1])).reshape(p.shape[1:])

    def start_scatter(name, sent):
        sent = {k: a.astype(BF16) for k, a in sent.items()}
        handles, token = _exchange_start(f"scatter_start_{name}", list(sent.values()), scatter=True)
        scatters[name] = (handles, list(sent), [lax.dynamic_index_in_dim(a, me, 0, keepdims=False) for a in sent.values()])
        return token[0, 0]

    def finish_scatter(name, l, after):
        handles, keys, own = scatters.pop(name)
        reduce_pieces(l, keys, own_slot(_exchange_wait(f"scatter_wait_{name}", handles, after, scatter=True), own))

    def push(l, part, big, after):
        cut = lambda a, n, axis: jnp.stack([lax.slice_in_dim(a, j * n, (j + 1) * n, axis=axis) for j in range(N_DEV)])
        sent = {}
        if part == "down":
            sent["w_down"] = big["w_down"].reshape(N_DEV, D_FF // N_DEV, D_MODEL)
        elif part == "up":
            sent["w_up"] = jnp.concatenate([big[part].reshape(half, up_cols, D_MODEL) for part in ("w_up_t_a", "w_up_t_v")])
        elif l < N_A_LAYERS:
            pool_grads[l] = big["pool_w"]
        else:
            ext = cut(big["w_uq_ext"], Q_EXT, 1)
            rope = ext[..., QK_NOPE:QK_HEAD] + _unswap_halves(ext[..., QK_HEAD:])
            sent.update(w_dq=big["w_dq"].reshape(N_DEV, D_MODEL // N_DEV, Q_RANK), w_uq=jnp.concatenate([ext[..., :QK_NOPE], rope], axis=-1),
                        w_o=big["w_o"].reshape(N_DEV, D_MODEL // N_DEV, D_MODEL))
        if part == "mix" and l == N_A_LAYERS:
            sent.update(w_dkv=_fold_w_dkv_grad(big["w_dkv_ext"]).reshape(N_DEV, D_MODEL // N_DEV, KV_RANK + QK_ROPE),
                        w_uk=cut(big["w_uk"], QK_NOPE, 1), w_uv=cut(big["w_uv"], V_HEAD, 1))
        if l == 0 and part != "mix":
            return start_scatter(f"0_{part}", sent)
        if l == 0:
            finish_scatter("1", 1, after)
            pool = _shard8(jnp.stack([pool_grads[a] for a in range(N_A_LAYERS)]), 2).astype(BF16)
            reduce_pieces(0, ["pool_w"], _exchange_many("scatter_pool_grads", [pool], scatter=True))
            return 0.0
        pending.setdefault(l, {}).update(sent)
        if part != "mix":
            return 0.0
        if l + 1 < DEPTH:
            finish_scatter(str(l + 1), l + 1, after)
        return start_scatter(str(l), pending.pop(l))

    loss_row, dx, g, dmods = _forward_backward(x[0], loss_target[0], mods, tabq, tabk, norm1_g, final_g, fetch, push)
    layers_of = lambda k, ls: jnp.stack([piece_grads[(k, l)] for l in ls])
    grads = dict(w_dkv=piece_grads[("w_dkv", N_A_LAYERS)], w_uk=piece_grads[("w_uk", N_A_LAYERS)], w_uv=piece_grads[("w_uv", N_A_LAYERS)])
    for k in ("w_dq", "w_uq", "w_o"):
        grads[k] = layers_of(k, range(N_A_LAYERS, DEPTH))

    small_names = REPLICATED_WEIGHTS + tuple(k for k, _ in VECTOR_WEIGHTS)
    small_out = [dmods] + [g[k] for k in small_names] + [loss_row]
    small_shapes = [a.shape for a in small_out]
    small_got = _exchange("gather_small_grads", _pack(small_out, F32, SMALL_ROW_MULTIPLE), scatter=False)
    summed = _unpack(_sum8("sum_small_grads", small_got), small_shapes)
    grads["mod_b"] = summed[0]
    for k, s in zip(small_names, summed[1:-1]):
        grads[k] = s
    for k, ax in VECTOR_WEIGHTS:
        n = shard[k].shape[ax]
        grads[k] = lax.dynamic_slice_in_dim(grads[k], me * n, n, axis=ax)
    loss = summed[-1][0, 0]
    dmods_all = _unpack(small_got, small_shapes)[0]
    dm_mine = lax.dynamic_slice_in_dim(dmods_all, me * mod_cols, mod_cols, axis=2)
    dm_mine = jnp.pad(jnp.moveaxis(dm_mine, 0, 1), ((0, 0), (0, N_DEV), (0, 0)))
    grads["mod_w"] = _mods_bwd("mods_bwd", c_all, dm_mine)

    delta, new_m, new_v = {}, {}, {}

    def adamw(k):
        if k == "w_up":
            ops = [w_up_t, grads[k], up_view(mom_m[k]), up_view(mom_v[k])]
            res = _adamw(f"adamw_{k}", *[_as_2d(a) for a in ops])
            grads[k], delta[k], new_m[k], new_v[k] = [up_view(r.reshape(w_up_t.shape)) for r in (ops[1],) + tuple(res)]
            return
        shp = shard[k].shape
        grads[k] = grads[k].reshape(shp)
        res = _adamw(f"adamw_{k}", _as_2d(shard[k]), _as_2d(grads[k]), _as_2d(mom_m[k]), _as_2d(mom_v[k]))
        delta[k], new_m[k], new_v[k] = [r.reshape(shp) for r in res]

    late = ("w_up", "w_down", "pool_w")
    for k in WEIGHT_ORDER:
        if k not in late:
            adamw(k)
    finish_scatter("0_down", 0, delta["final_g"])
    finish_scatter("0_up", 0, delta["final_g"])
    grads.update(w_up=stacked["w_up"], w_down=stacked["w_down"], pool_w=piece_grads[("pool_w", 0)])
    for k in late:
        adamw(k)
    return (loss, dx[None], *[grads[k] for k in WEIGHT_ORDER], *[delta[k] for k in WEIGHT_ORDER],
            *[new_m[k] for k in WEIGHT_ORDER], *[new_v[k] for k in WEIGHT_ORDER])
```
